```python
import jax, jax.numpy as jnp
from jax import lax
import numpy as np

D_MODEL = 1024
BATCH = 8
SEQ = 2048
DEPTH = 1
DEC_BATCH = 128
DEC_SEQ = 4
PAST_LEN = 16384
PAGE_SIZE = 128

ML_HEADS = 4
ML_DK = 128
ML_DV = 128
GLA_HEADS = 4
GLA_DK = 64
GLA_DV = 128
GLA_RANK = 16
GLA_TAU = 16.0
MIX_WIDTH = ML_HEADS * ML_DV + GLA_HEADS * GLA_DV
CHUNK = 64
N_MEM = 256
XA_HEADS = 4
XA_DH = D_MODEL // XA_HEADS
D_FF = 2816
EPS = 1e-6
IN_SIZES = (ML_HEADS * ML_DK, ML_HEADS * ML_DK, ML_HEADS * ML_DV, ML_HEADS, ML_HEADS, ML_HEADS * ML_DV,
            GLA_HEADS * GLA_DK, GLA_HEADS * GLA_DK, GLA_HEADS * GLA_DV, GLA_RANK, GLA_HEADS * GLA_DV)
D_IN = sum(IN_SIZES)

kernel_name = 'hymba_mlstm_gla_macaron_memxattn_step'


def _rmsnorm(x, g):
    xf = x.astype(jnp.float32)
    y = xf * lax.rsqrt(jnp.mean(xf * xf, axis=-1, keepdims=True) + EPS)
    return (y * g.astype(jnp.float32)).astype(x.dtype)


def _head_rmsnorm(h, g):
    hf = h.astype(jnp.float32)
    y = hf * lax.rsqrt(jnp.mean(hf * hf, axis=-1, keepdims=True) + EPS)
    return y * g.astype(jnp.float32).reshape(h.shape[-2:])


def _swiglu(x, wg, wu, wd):
    return (jax.nn.silu(x @ wg) * (x @ wu)) @ wd


def _chunk_len(T):
    return CHUNK if T % CHUNK == 0 else T


def _to_chunks(a, L):
    B, T = a.shape[:2]
    return jnp.moveaxis(a.reshape((B, T // L, L) + a.shape[2:]), 1, 0)


def _from_chunks(a):
    a = jnp.moveaxis(a, 0, 1)
    return a.reshape((a.shape[0], a.shape[1] * a.shape[2]) + a.shape[3:])


def _mlstm(q, k, v, i_pre, logf, C0, n0, m0):
    L = _chunk_len(q.shape[1])
    mask = jnp.tril(jnp.ones((L, L), dtype=bool))[None, :, :, None]

    def step(carry, inp):
        C, n, m = carry
        qc, kc, vc, ic, fc = inp
        b = jnp.cumsum(fc, axis=1)
        a = b + m[:, None, :]
        Dm = jnp.where(mask, b[:, :, None, :] - b[:, None, :, :] + ic[:, None, :, :], -jnp.inf)
        mt = jnp.maximum(a, jnp.max(Dm, axis=2))
        w_inter = jnp.exp(a - mt)
        W = jnp.exp(Dm - mt[:, :, None, :])
        s = jnp.einsum('bthd,bshd->btsh', qc, kc) * W
        num = jnp.einsum('btsh,bshv->bthv', s, vc) + w_inter[..., None] * jnp.einsum('bthd,bhdv->bthv', qc, C)
        den = jnp.sum(s, axis=2) + w_inter * jnp.einsum('bthd,bhd->bth', qc, n)
        h = num / jnp.maximum(jnp.abs(den), jnp.exp(-mt))[..., None]
        wL = W[:, -1]
        iL = w_inter[:, -1]
        C_new = iL[..., None, None] * C + jnp.einsum('bsh,bshd,bshv->bhdv', wL, kc, vc)
        n_new = iL[..., None] * n + jnp.einsum('bsh,bshd->bhd', wL, kc)
        return (C_new, n_new, mt[:, -1]), h

    xs = (_to_chunks(q, L), _to_chunks(k, L), _to_chunks(v, L), _to_chunks(i_pre, L), _to_chunks(logf, L))
    init = (C0.astype(jnp.float32), n0.astype(jnp.float32), m0.astype(jnp.float32))
    (C, n, m), h = lax.scan(step, init, xs)
    return _from_chunks(h), C, n, m


def _gla(q, k, v, loga, S0):
    L = _chunk_len(q.shape[1])
    mask = jnp.tril(jnp.ones((L, L), dtype=bool))[None, :, :, None, None]

    def step(S, inp):
        qc, kc, vc, lc = inp
        Bc = jnp.cumsum(lc, axis=1)
        o_inter = jnp.einsum('bthd,bhdv->bthv', qc * jnp.exp(Bc), S)
        decay = jnp.exp(jnp.where(mask, Bc[:, :, None] - Bc[:, None, :], -jnp.inf))
        A = jnp.einsum('bthd,bshd,btshd->btsh', qc, kc, decay)
        o = o_inter + jnp.einsum('btsh,bshv->bthv', A, vc)
        BL = Bc[:, -1]
        S_new = jnp.exp(BL)[..., None] * S + jnp.einsum('bshd,bshv->bhdv', kc * jnp.exp(BL[:, None] - Bc), vc)
        return S_new, o

    xs = (_to_chunks(q, L), _to_chunks(k, L), _to_chunks(v, L), _to_chunks(loga, L))
    S, o = lax.scan(step, S0.astype(jnp.float32), xs)
    return _from_chunks(o), S


def _layer(x, mem_k, mem_v, C0, n0, m0, S0, w):
    Bsz, T, _ = x.shape
    f32 = jnp.float32
    x = x + 0.5 * _swiglu(_rmsnorm(x, w['ffn1_norm_g']), w['ffn1_w_gate'], w['ffn1_w_up'], w['ffn1_w_down'])
    z = _rmsnorm(x, w['mix_norm_g']) @ w['w_in']
    mq, mk, mv, mi, mf, mo, gq, gk, gv, ga, gg = jnp.split(z, np.cumsum(IN_SIZES)[:-1].tolist(), axis=-1)
    mq = mq.reshape(Bsz, T, ML_HEADS, ML_DK).astype(f32)
    mk = mk.reshape(Bsz, T, ML_HEADS, ML_DK).astype(f32) * (ML_DK ** -0.5)
    mv = mv.reshape(Bsz, T, ML_HEADS, ML_DV).astype(f32)
    i_pre = (mi + w['mlstm_b_i']).astype(f32)
    logf = jax.nn.log_sigmoid((mf + w['mlstm_b_f']).astype(f32))
    h_ml, C, n, m = _mlstm(mq, mk, mv, i_pre, logf, C0, n0, m0)
    y_ml = jax.nn.sigmoid(mo.astype(f32)) * _head_rmsnorm(h_ml, w['mlstm_out_g']).reshape(Bsz, T, ML_HEADS * ML_DV)
    gq = gq.reshape(Bsz, T, GLA_HEADS, GLA_DK).astype(f32) * (GLA_DK ** -0.5)
    gk = gk.reshape(Bsz, T, GLA_HEADS, GLA_DK).astype(f32)
    gv = gv.reshape(Bsz, T, GLA_HEADS, GLA_DV).astype(f32)
    loga = jax.nn.log_sigmoid((ga @ w['gla_w_a2'] + w['gla_b_a']).astype(f32)) / GLA_TAU
    loga = loga.reshape(Bsz, T, GLA_HEADS, GLA_DK)
    h_gla, S = _gla(gq, gk, gv, loga, S0)
    y_gla = jax.nn.silu(gg.astype(f32)) * _head_rmsnorm(h_gla, w['gla_out_g']).reshape(Bsz, T, GLA_HEADS * GLA_DV)
    y_mix = jnp.concatenate([y_ml, y_gla], axis=-1).astype(x.dtype)
    x = x + y_mix @ w['w_out']
    q = (_rmsnorm(x, w['xattn_norm_g']) @ w['xattn_w_q']).reshape(Bsz, T, XA_HEADS, XA_DH)
    s = jnp.einsum('bthd,bmhd->bhtm', q.astype(f32), mem_k.astype(f32)) * (XA_DH ** -0.5)
    p = jax.nn.softmax(s, axis=-1)
    o = jnp.einsum('bhtm,bmhd->bthd', p, mem_v.astype(f32)).reshape(Bsz, T, D_MODEL).astype(x.dtype)
    x = x + o @ w['xattn_w_o']
    x = x + 0.5 * _swiglu(_rmsnorm(x, w['ffn2_norm_g']), w['ffn2_w_gate'], w['ffn2_w_up'], w['ffn2_w_down'])
    return x, C.astype(C0.dtype), n.astype(n0.dtype), m.astype(m0.dtype), S.astype(S0.dtype)


def setup_inputs(seed: int = 0) -> dict:
    key = jax.random.key(seed)
    ks = list(jax.random.split(key, 40))
    ctr = [0]

    def nrm(shape, scale=1.0):
        k = ks[ctr[0]]
        ctr[0] += 1
        return scale * jax.random.normal(k, shape, jnp.float32)

    def gain(n):
        return 1.0 + 0.02 * nrm((DEPTH, n))

    d = {}
    d['x_prompt'] = nrm((BATCH, SEQ, D_MODEL))
    d['x_sample'] = nrm((DEC_BATCH, DEC_SEQ, D_MODEL))
    d['mem_prompt'] = nrm((BATCH, N_MEM, D_MODEL))
    d['cache_mem_k'] = nrm((DEPTH, DEC_BATCH, N_MEM, XA_HEADS, XA_DH))
    d['cache_mem_v'] = nrm((DEPTH, DEC_BATCH, N_MEM, XA_HEADS, XA_DH))
    d['state_mlstm_c'] = nrm((DEPTH, DEC_BATCH, ML_HEADS, ML_DK, ML_DV), 0.3)
    d['state_mlstm_n'] = nrm((DEPTH, DEC_BATCH, ML_HEADS, ML_DK), 0.3)
    d['state_mlstm_m'] = nrm((DEPTH, DEC_BATCH, ML_HEADS))
    d['state_gla_s'] = nrm((DEPTH, DEC_BATCH, GLA_HEADS, GLA_DK, GLA_DV), 0.3)
    d['ffn1_norm_g'] = gain(D_MODEL)
    d['ffn1_w_gate'] = nrm((DEPTH, D_MODEL, D_FF), D_MODEL ** -0.5)
    d['ffn1_w_up'] = nrm((DEPTH, D_MODEL, D_FF), D_MODEL ** -0.5)
    d['ffn1_w_down'] = nrm((DEPTH, D_FF, D_MODEL), D_FF ** -0.5)
    d['mix_norm_g'] = gain(D_MODEL)
    d['w_in'] = nrm((DEPTH, D_MODEL, D_IN), D_MODEL ** -0.5)
    d['mlstm_b_i'] = nrm((DEPTH, ML_HEADS), 0.1)
    d['mlstm_b_f'] = jnp.linspace(3.0, 6.0, ML_HEADS, dtype=jnp.float32)[None, :] + nrm((DEPTH, ML_HEADS), 0.1)
    d['mlstm_out_g'] = gain(ML_HEADS * ML_DV)
    d['gla_w_a2'] = nrm((DEPTH, GLA_RANK, GLA_HEADS * GLA_DK), GLA_RANK ** -0.5)
    d['gla_b_a'] = nrm((DEPTH, GLA_HEADS * GLA_DK), 0.1)
    d['gla_out_g'] = gain(GLA_HEADS * GLA_DV)
    d['w_out'] = nrm((DEPTH, MIX_WIDTH, D_MODEL), MIX_WIDTH ** -0.5)
    d['xattn_norm_g'] = gain(D_MODEL)
    d['mem_norm_g'] = gain(D_MODEL)
    d['xattn_w_q'] = nrm((DEPTH, D_MODEL, D_MODEL), D_MODEL ** -0.5)
    d['xattn_w_k'] = nrm((DEPTH, D_MODEL, D_MODEL), D_MODEL ** -0.5)
    d['xattn_w_v'] = nrm((DEPTH, D_MODEL, D_MODEL), D_MODEL ** -0.5)
    d['xattn_w_o'] = nrm((DEPTH, D_MODEL, D_MODEL), D_MODEL ** -0.5)
    d['ffn2_norm_g'] = gain(D_MODEL)
    d['ffn2_w_gate'] = nrm((DEPTH, D_MODEL, D_FF), D_MODEL ** -0.5)
    d['ffn2_w_up'] = nrm((DEPTH, D_MODEL, D_FF), D_MODEL ** -0.5)
    d['ffn2_w_down'] = nrm((DEPTH, D_FF, D_MODEL), D_FF ** -0.5)
    d['final_norm_g'] = 1.0 + 0.02 * nrm((D_MODEL,))
    return d


def reference(x_prompt, x_sample, mem_prompt, cache_mem_k, cache_mem_v, state_mlstm_c, state_mlstm_n,
              state_mlstm_m, state_gla_s, ffn1_norm_g, ffn1_w_gate, ffn1_w_up, ffn1_w_down, mix_norm_g, w_in,
              mlstm_b_i, mlstm_b_f, mlstm_out_g, gla_w_a2, gla_b_a, gla_out_g, w_out, xattn_norm_g, mem_norm_g,
              xattn_w_q, xattn_w_k, xattn_w_v, xattn_w_o, ffn2_norm_g, ffn2_w_gate, ffn2_w_up, ffn2_w_down,
              final_norm_g):
    layer_w = dict(ffn1_norm_g=ffn1_norm_g, ffn1_w_gate=ffn1_w_gate, ffn1_w_up=ffn1_w_up, ffn1_w_down=ffn1_w_down,
                   mix_norm_g=mix_norm_g, w_in=w_in, mlstm_b_i=mlstm_b_i, mlstm_b_f=mlstm_b_f,
                   mlstm_out_g=mlstm_out_g, gla_w_a2=gla_w_a2, gla_b_a=gla_b_a, gla_out_g=gla_out_g, w_out=w_out,
                   xattn_norm_g=xattn_norm_g, mem_norm_g=mem_norm_g, xattn_w_q=xattn_w_q, xattn_w_k=xattn_w_k,
                   xattn_w_v=xattn_w_v, xattn_w_o=xattn_w_o, ffn2_norm_g=ffn2_norm_g, ffn2_w_gate=ffn2_w_gate,
                   ffn2_w_up=ffn2_w_up, ffn2_w_down=ffn2_w_down)
    Bp = x_prompt.shape[0]
    dt = x_prompt.dtype
    xp, xs = x_prompt, x_sample
    mk_l, mv_l, cp_l, np_l, mp_l, sp_l, cs_l, ns_l, ms_l, ss_l = [], [], [], [], [], [], [], [], [], []
    for l in range(DEPTH):
        w = {name: arr[l] for name, arr in layer_w.items()}
        mem_n = _rmsnorm(mem_prompt, w['mem_norm_g'])
        mem_k_p = (mem_n @ w['xattn_w_k']).reshape(Bp, N_MEM, XA_HEADS, XA_DH)
        mem_v_p = (mem_n @ w['xattn_w_v']).reshape(Bp, N_MEM, XA_HEADS, XA_DH)
        c0 = jnp.zeros((Bp, ML_HEADS, ML_DK, ML_DV), dt)
        n0 = jnp.zeros((Bp, ML_HEADS, ML_DK), dt)
        m0 = jnp.zeros((Bp, ML_HEADS), dt)
        s0 = jnp.zeros((Bp, GLA_HEADS, GLA_DK, GLA_DV), dt)
        xp, c_p, n_p, m_p, s_p = _layer(xp, mem_k_p, mem_v_p, c0, n0, m0, s0, w)
        xs, c_s, n_s, m_s, s_s = _layer(xs, cache_mem_k[l], cache_mem_v[l], state_mlstm_c[l], state_mlstm_n[l],
                                        state_mlstm_m[l], state_gla_s[l], w)
        mk_l.append(mem_k_p)
        mv_l.append(mem_v_p)
        cp_l.append(c_p)
        np_l.append(n_p)
        mp_l.append(m_p)
        sp_l.append(s_p)
        cs_l.append(c_s)
        ns_l.append(n_s)
        ms_l.append(m_s)
        ss_l.append(s_s)
    y_prompt = _rmsnorm(xp, final_norm_g)
    y_sample = _rmsnorm(xs, final_norm_g)
    return (y_prompt, y_sample, jnp.stack(mk_l), jnp.stack(mv_l), jnp.stack(cp_l), jnp.stack(np_l),
            jnp.stack(mp_l), jnp.stack(sp_l), jnp.stack(cs_l), jnp.stack(ns_l), jnp.stack(ms_l), jnp.stack(ss_l))
```

```python
import functools

import jax
import jax.numpy as jnp
from jax import lax
from jax.experimental import pallas as pl
from jax.experimental.pallas import tpu as pltpu

F32 = jnp.float32
BF16 = jnp.bfloat16

D_MODEL = 1024
D_FF = 2816
ML_HEADS = 4
ML_DK = 128
ML_DV = 128
GLA_HEADS = 4
GLA_DK = 64
GLA_DV = 128
GLA_RANK = 16
GLA_TAU = 16.0
N_MEM = 256
XA_HEADS = 4
XA_DH = D_MODEL // XA_HEADS
EPS = 1e-6
CHUNK = 64

ZQ_MQ, ZQ_MK, ZQ_MV = 0, 512, 1024
ZQ_GQ, ZQ_GK, ZQ_GV = 1536, 1792, 2048
ZQ_W = 2560
ZG_MO, ZG_GG, ZG_SMALL = 0, 512, 1024
ZG_W = 1152
SMALL_W = 128
GLA_QK_W = GLA_HEADS * GLA_DK
GLA_V_W = GLA_HEADS * GLA_DV

FF_CHUNK = D_FF // 2
VMEM_LIMIT_BYTES = 56 * 1024 * 1024


def _rms(x, g):
    return x * lax.rsqrt(jnp.mean(x * x, axis=-1, keepdims=True) + EPS) * g


def _log_sigmoid(x):
    return jnp.minimum(x, 0.0) - jnp.log1p(jnp.exp(-jnp.abs(x)))


def _dot(a, b):
    return jnp.dot(a, b, preferred_element_type=F32)


def _dot_nt(a, b):
    return lax.dot_general(a, b, (((1,), (1,)), ((), ())), preferred_element_type=F32)


def _swiglu_residual(x, g_ref, wg_ref, wu_ref, wd_ref):
    h = _rms(x, g_ref[...]).astype(BF16)
    acc = jnp.zeros_like(x)
    for c in range(D_FF // FF_CHUNK):
        sl = slice(c * FF_CHUNK, (c + 1) * FF_CHUNK)
        g = _dot(h, wg_ref[:, sl])
        u = _dot(h, wu_ref[:, sl])
        a = (g * jax.nn.sigmoid(g)) * u
        acc = acc + _dot(a.astype(BF16), wd_ref[sl, :])
    return x + 0.5 * acc


def _const_spec(shape):
    nd = len(shape)
    return pl.BlockSpec(shape, lambda *_: (0,) * nd, pipeline_mode=pl.Buffered(1))


def _params(sem):
    return pltpu.CompilerParams(dimension_semantics=sem, vmem_limit_bytes=VMEM_LIMIT_BYTES)


def _ffn_in_kernel(x_ref, g1_ref, wg_ref, wu_ref, wd_ref, gm_ref, wq_ref, wgt_ref, x1_ref, zq_ref, zg_ref):
    x1 = _swiglu_residual(x_ref[...], g1_ref, wg_ref, wu_ref, wd_ref)
    x1_ref[...] = x1
    hm = _rms(x1, gm_ref[...]).astype(BF16)
    zq_ref[...] = _dot(hm, wq_ref[...]).astype(zq_ref.dtype)
    zg_ref[...] = _dot(hm, wgt_ref[...])


def _ffn_in(x, w, tm, zq_dtype):
    n = x.shape[0]
    row = lambda width: pl.BlockSpec((tm, width), lambda i: (i, 0))
    return pl.pallas_call(
        _ffn_in_kernel,
        grid=(n // tm,),
        in_specs=[row(D_MODEL), _const_spec((1, D_MODEL)), _const_spec((D_MODEL, D_FF)),
                  _const_spec((D_MODEL, D_FF)), _const_spec((D_FF, D_MODEL)), _const_spec((1, D_MODEL)),
                  _const_spec((D_MODEL, ZQ_W)), _const_spec((D_MODEL, ZG_W))],
        out_specs=[row(D_MODEL), row(ZQ_W), row(ZG_W)],
        out_shape=[jax.ShapeDtypeStruct((n, D_MODEL), F32), jax.ShapeDtypeStruct((n, ZQ_W), zq_dtype),
                   jax.ShapeDtypeStruct((n, ZG_W), F32)],
        compiler_params=_params(("arbitrary",)),
        name="ffn_in",
    )(x, w["ffn1_g"], w["ffn1_wg"], w["ffn1_wu"], w["ffn1_wd"], w["mix_g"], w["w_in_q"], w["w_in_g"])


def _cumsum_rows(x, tril):
    rows = x.shape[0]
    if rows <= 8:
        acc = x[0:1]
        out = [acc]
        for r in range(1, rows):
            acc = acc + x[r:r + 1]
            out.append(acc)
        return jnp.concatenate(out, axis=0)
    return jnp.dot(tril.astype(F32), x, precision=lax.Precision.HIGHEST, preferred_element_type=F32)


def _mixer_kernel(*refs, L, c, t_real, has_state, n_chunks):
    it = iter(refs)
    zq_ref, zg_ref, bias_ref, wa2_ref, ba_ref, gml_ref, ggl_ref, ee_ref, bdm_ref = (next(it) for _ in range(9))
    if has_state:
        c0_ref, n0_ref, m0_ref, s0_ref = (next(it) for _ in range(4))
    y_ref, co_ref, no_ref, mo_ref, so_ref = (next(it) for _ in range(5))
    c_s, n_s, m_s, s_s = (next(it) for _ in range(4))
    ci = pl.program_id(1)

    @pl.when(ci == 0)
    def _init():
        if has_state:
            c_s[...] = c0_ref[0]
            n_s[0:ML_HEADS, :] = n0_ref[0]
            m0 = m0_ref[0]
            for h in range(ML_HEADS):
                m_s[h:h + 1, :] = jnp.broadcast_to(m0[:, h:h + 1], (1, 128))
            s_s[...] = jnp.zeros_like(s_s)
            for h in range(GLA_HEADS):
                s_s[h * GLA_DK:(h + 1) * GLA_DK, h * GLA_DV:(h + 1) * GLA_DV] = s0_ref[0, h]
        else:
            c_s[...] = jnp.zeros_like(c_s)
            n_s[...] = jnp.zeros_like(n_s)
            m_s[...] = jnp.zeros_like(m_s)
            s_s[...] = jnp.zeros_like(s_s)

    padded = t_real < L
    n_blk = L // c
    zq = zq_ref[0]
    zg = zg_ref[0]
    small = zg[:, ZG_SMALL:ZG_SMALL + SMALL_W]
    valid = lax.broadcasted_iota(jnp.int32, (L, 1), 0) < t_real

    rr = lax.broadcasted_iota(jnp.int32, (L, L), 0)
    cc = lax.broadcasted_iota(jnp.int32, (L, L), 1)
    eye = rr == cc
    tril = cc <= rr
    triu = rr <= cc

    sm = small + bias_ref[...]
    lf = _log_sigmoid(sm)
    if padded:
        sm = jnp.where(valid, sm, -jnp.inf)
        lf = jnp.where(valid, lf, 0.0)
    for h in range(ML_HEADS):
        i_col = sm[:, h:h + 1]
        lf_col = lf[:, ML_HEADS + h:ML_HEADS + h + 1]
        i_row = jnp.sum(jnp.where(eye, i_col, 0.0), axis=0, keepdims=True)
        lf_row = jnp.sum(jnp.where(eye, lf_col, 0.0), axis=0, keepdims=True)
        b_row = jnp.sum(jnp.where(triu, lf_col, 0.0), axis=0, keepdims=True)
        b_col = jnp.sum(jnp.where(tril, lf_row, 0.0), axis=1, keepdims=True)
        m_prev = m_s[h:h + 1, 0:1]
        a_col = b_col + m_prev
        dm = jnp.where(tril, b_col - (b_row - i_row), -jnp.inf)
        mt = jnp.maximum(a_col, jnp.max(dm, axis=1, keepdims=True))
        w_inter = jnp.exp(a_col - mt)
        wmat = jnp.exp(dm - mt)

        hs = slice(h * ML_DK, (h + 1) * ML_DK)
        qf = zq[:, ZQ_MQ + h * ML_DK:ZQ_MQ + (h + 1) * ML_DK].astype(F32)
        kf = zq[:, ZQ_MK + h * ML_DK:ZQ_MK + (h + 1) * ML_DK].astype(F32) * (ML_DK ** -0.5)
        vf = zq[:, ZQ_MV + h * ML_DV:ZQ_MV + (h + 1) * ML_DV].astype(F32)
        if padded:
            kf = jnp.where(valid, kf, 0.0)
            vf = jnp.where(valid, vf, 0.0)
        qb, kb, vb = qf.astype(BF16), kf.astype(BF16), vf.astype(BF16)
        s = _dot_nt(qb, kb) * wmat
        c_prev = c_s[h]
        n_prev = n_s[h:h + 1, :]
        num = _dot(s.astype(BF16), vb) + w_inter * _dot(qb, c_prev.astype(BF16))
        den = jnp.sum(s, axis=1, keepdims=True) + w_inter * jnp.sum(qf * n_prev, axis=1, keepdims=True)
        hh = num / jnp.maximum(jnp.abs(den), jnp.exp(-mt))

        b_last = b_col[L - 1:L]
        mt_last = mt[L - 1:L]
        i_last = w_inter[L - 1:L]
        wl_col = jnp.exp((b_last - mt_last) - (b_col - i_col))
        kw = kf * wl_col
        c_s[h] = i_last * c_prev + _dot(kw.T.astype(BF16), vb)
        n_s[h:h + 1, :] = i_last * n_prev + jnp.sum(kw, axis=0, keepdims=True)
        m_s[h:h + 1, :] = jnp.broadcast_to(mt_last, (1, 128))

        yn = hh * lax.rsqrt(jnp.mean(hh * hh, axis=-1, keepdims=True) + EPS) * gml_ref[:, hs]
        gate = jax.nn.sigmoid(zg[:, ZG_MO + h * ML_DV:ZG_MO + (h + 1) * ML_DV])
        y_ref[0, :, h * ML_DV:(h + 1) * ML_DV] = (gate * yn).astype(y_ref.dtype)

    gq = zq[:, ZQ_GQ:ZQ_GQ + GLA_QK_W].astype(F32) * (GLA_DK ** -0.5)
    gk = zq[:, ZQ_GK:ZQ_GK + GLA_QK_W].astype(F32)
    gv = zq[:, ZQ_GV:ZQ_GV + GLA_V_W].astype(F32)
    la = _log_sigmoid(_dot(small.astype(BF16), wa2_ref[...]) + ba_ref[...]) * (1.0 / GLA_TAU)
    if padded:
        gk = jnp.where(valid, gk, 0.0)
        gv = jnp.where(valid, gv, 0.0)
        la = jnp.where(valid, la, 0.0)
    bc = _cumsum_rows(la, tril)
    s_prev = s_s[...]
    o_inter = _dot((gq * jnp.exp(bc)).astype(BF16), s_prev.astype(BF16))

    qk_head = lax.broadcasted_iota(jnp.int32, (1, GLA_QK_W), 1) // GLA_DK
    v_head = lax.broadcasted_iota(jnp.int32, (1, GLA_V_W), 1) // GLA_DV
    o_rows = [o_inter[0:c]]
    for i in range(1, n_blk):
        lo = i * c
        ref_row = bc[lo:lo + 1]
        q_i = gq[lo:lo + c] * jnp.exp(bc[lo:lo + c] - ref_row)
        k_i = gk[:lo] * jnp.exp(ref_row - bc[:lo])
        k_bd = jnp.concatenate([jnp.where(qk_head == h, k_i, 0.0) for h in range(GLA_HEADS)], axis=0)
        a_cat = _dot_nt(q_i.astype(BF16), k_bd.astype(BF16))
        v_i = gv[:lo]
        v_bd = jnp.concatenate([jnp.where(v_head == h, v_i, 0.0) for h in range(GLA_HEADS)], axis=0)
        o_rows.append(o_inter[lo:lo + c] + _dot(a_cat.astype(BF16), v_bd.astype(BF16)))
    o_gla = jnp.concatenate(o_rows, axis=0) if n_blk > 1 else o_rows[0]

    q3 = gq.reshape(n_blk, c, GLA_QK_W)
    k3 = gk.reshape(n_blk, c, GLA_QK_W)
    b3 = bc.reshape(n_blk, c, GLA_QK_W)
    v3 = gv.reshape(n_blk, c, GLA_V_W)
    t_in_blk = lax.broadcasted_iota(jnp.int32, (1, c, 1), 1)
    ee = ee_ref[...]
    for j in range(c):
        decay = jnp.exp(jnp.where(t_in_blk >= j, b3 - b3[:, j:j + 1, :], -jnp.inf))
        pair = (q3 * k3[:, j:j + 1, :] * decay).reshape(L, GLA_QK_W)
        a_j = _dot(pair.astype(BF16), ee)
        v_j = jnp.broadcast_to(v3[:, j:j + 1, :], (n_blk, c, GLA_V_W)).reshape(L, GLA_V_W)
        o_gla = o_gla + a_j * v_j

    bc_t = bc.T
    bl_col = bc_t[:, L - 1:L]
    kh_t = gk.T * jnp.exp(bl_col - bc_t)
    upd = _dot(kh_t.astype(BF16), gv.astype(BF16))
    s_s[...] = jnp.exp(bl_col) * s_prev + bdm_ref[...] * upd

    for h in range(GLA_HEADS):
        hs = slice(h * GLA_DV, (h + 1) * GLA_DV)
        oh = o_gla[:, hs]
        yn = oh * lax.rsqrt(jnp.mean(oh * oh, axis=-1, keepdims=True) + EPS) * ggl_ref[:, hs]
        gg = zg[:, ZG_GG + h * GLA_DV:ZG_GG + (h + 1) * GLA_DV]
        y_ref[0, :, ML_HEADS * ML_DV + h * GLA_DV:ML_HEADS * ML_DV + (h + 1) * GLA_DV] = (
            (gg * jax.nn.sigmoid(gg)) * yn).astype(y_ref.dtype)

    @pl.when(ci == n_chunks - 1)
    def _emit_state():
        co_ref[0] = c_s[...]
        no_ref[0] = n_s[0:ML_HEADS, :]
        lane = lax.broadcasted_iota(jnp.int32, (1, 128), 1)
        m_row = jnp.zeros((1, 128), F32)
        for h in range(ML_HEADS):
            m_row = jnp.where(lane == h, m_s[h:h + 1, :], m_row)
        mo_ref[0] = m_row[:, 0:ML_HEADS]
        for h in range(GLA_HEADS):
            so_ref[0, h] = s_s[h * GLA_DK:(h + 1) * GLA_DK, h * GLA_DV:(h + 1) * GLA_DV]


def _mixer(zq, zg, w, L, c, t_real, state, y_dtype):
    bsz, t = zq.shape[0], zq.shape[1]
    n_chunks = t // L
    has_state = state is not None
    seq = lambda width: pl.BlockSpec((1, L, width), lambda b, j: (b, j, 0))
    per_b = lambda *tail: pl.BlockSpec((1,) + tail, lambda b, j: (b,) + (0,) * len(tail))
    in_specs = [seq(ZQ_W), seq(ZG_W), _const_spec((1, SMALL_W)), _const_spec((SMALL_W, GLA_QK_W)),
                _const_spec((1, GLA_QK_W)), _const_spec((1, ML_HEADS * ML_DV)), _const_spec((1, GLA_V_W)),
                _const_spec((GLA_QK_W, GLA_V_W)), _const_spec((GLA_QK_W, GLA_V_W))]
    args = [zq, zg, w["gate_bias"], w["w_a2"], w["b_a"], w["mlstm_out_g"], w["gla_out_g"], w["head_sum"],
            w["head_mask"]]
    if has_state:
        in_specs += [per_b(ML_HEADS, ML_DK, ML_DV), per_b(ML_HEADS, ML_DK), per_b(1, ML_HEADS),
                     per_b(GLA_HEADS, GLA_DK, GLA_DV)]
        args += list(state)
    out_specs = [seq(D_MODEL), per_b(ML_HEADS, ML_DK, ML_DV), per_b(ML_HEADS, ML_DK), per_b(1, ML_HEADS),
                 per_b(GLA_HEADS, GLA_DK, GLA_DV)]
    out_shape = [jax.ShapeDtypeStruct((bsz, t, D_MODEL), y_dtype),
                 jax.ShapeDtypeStruct((bsz, ML_HEADS, ML_DK, ML_DV), F32),
                 jax.ShapeDtypeStruct((bsz, ML_HEADS, ML_DK), F32),
                 jax.ShapeDtypeStruct((bsz, 1, ML_HEADS), F32),
                 jax.ShapeDtypeStruct((bsz, GLA_HEADS, GLA_DK, GLA_DV), F32)]
    kern = functools.partial(_mixer_kernel, L=L, c=c, t_real=t_real, has_state=has_state, n_chunks=n_chunks)
    return pl.pallas_call(
        kern,
        grid=(bsz, n_chunks),
        in_specs=in_specs,
        out_specs=out_specs,
        out_shape=out_shape,
        scratch_shapes=[pltpu.VMEM((ML_HEADS, ML_DK, ML_DV), F32), pltpu.VMEM((8, 128), F32),
                        pltpu.VMEM((8, 128), F32), pltpu.VMEM((GLA_QK_W, GLA_V_W), F32)],
        compiler_params=_params(("arbitrary", "arbitrary")),
        name="mixer_state" if has_state else "mixer",
    )(*args)


def _post_mix_kernel(x1_ref, ym_ref, wout_ref, gx_ref, wq_ref, x2_ref, q_ref):
    x2 = x1_ref[...] + _dot(ym_ref[...].astype(BF16), wout_ref[...])
    x2_ref[...] = x2
    hq = _rms(x2, gx_ref[...]).astype(BF16)
    q_ref[...] = _dot(hq, wq_ref[...]).astype(q_ref.dtype)


def _post_mix(x1, ym, w, tm, q_dtype):
    n = x1.shape[0]
    row = lambda: pl.BlockSpec((tm, D_MODEL), lambda i: (i, 0))
    return pl.pallas_call(
        _post_mix_kernel,
        grid=(n // tm,),
        in_specs=[row(), row(), _const_spec((D_MODEL, D_MODEL)), _const_spec((1, D_MODEL)),
                  _const_spec((D_MODEL, D_MODEL))],
        out_specs=[row(), row()],
        out_shape=[jax.ShapeDtypeStruct((n, D_MODEL), F32), jax.ShapeDtypeStruct((n, D_MODEL), q_dtype)],
        compiler_params=_params(("arbitrary",)),
        name="post_mix",
    )(x1, ym, w["w_out"], w["xattn_g"], w["xattn_wq"])


def _xattn_kernel(q_ref, k_ref, v_ref, o_ref, *, bb):
    for b in range(bb):
        for h in range(XA_HEADS):
            hs = slice(h * XA_DH, (h + 1) * XA_DH)
            qh = q_ref[b, :, hs].astype(BF16)
            kh = k_ref[b, :, hs].astype(BF16)
            vh = v_ref[b, :, hs].astype(BF16)
            s = _dot_nt(qh, kh) * (XA_DH ** -0.5)
            e = jnp.exp(s - jnp.max(s, axis=-1, keepdims=True))
            p = e / jnp.sum(e, axis=-1, keepdims=True)
            o_ref[b, :, hs] = _dot(p.astype(BF16), vh).astype(o_ref.dtype)


def _xattn(q, k, v, bb, tq, o_dtype):
    bsz, t = q.shape[0], q.shape[1]
    qo = pl.BlockSpec((bb, tq, D_MODEL), lambda b, j: (b, j, 0))
    kv = pl.BlockSpec((bb, N_MEM, D_MODEL), lambda b, j: (b, 0, 0))
    return pl.pallas_call(
        functools.partial(_xattn_kernel, bb=bb),
        grid=(bsz // bb, t // tq),
        in_specs=[qo, kv, kv],
        out_specs=qo,
        out_shape=jax.ShapeDtypeStruct((bsz, t, D_MODEL), o_dtype),
        compiler_params=_params(("arbitrary", "arbitrary")),
        name="xattn",
    )(q, k, v)


def _ffn_out_kernel(x2_ref, o_ref, wo_ref, g2_ref, wg_ref, wu_ref, wd_ref, gf_ref, y_ref):
    x3 = x2_ref[...] + _dot(o_ref[...].astype(BF16), wo_ref[...])
    x4 = _swiglu_residual(x3, g2_ref, wg_ref, wu_ref, wd_ref)
    y_ref[...] = _rms(x4, gf_ref[...])


def _ffn_out(x2, o, w, tm):
    n = x2.shape[0]
    row = lambda: pl.BlockSpec((tm, D_MODEL), lambda i: (i, 0))
    return pl.pallas_call(
        _ffn_out_kernel,
        grid=(n // tm,),
        in_specs=[row(), row(), _const_spec((D_MODEL, D_MODEL)), _const_spec((1, D_MODEL)),
                  _const_spec((D_MODEL, D_FF)), _const_spec((D_MODEL, D_FF)), _const_spec((D_FF, D_MODEL)),
                  _const_spec((1, D_MODEL))],
        out_specs=row(),
        out_shape=jax.ShapeDtypeStruct((n, D_MODEL), F32),
        compiler_params=_params(("arbitrary",)),
        name="ffn_out",
    )(x2, o, w["xattn_wo"], w["ffn2_g"], w["ffn2_wg"], w["ffn2_wu"], w["ffn2_wd"], w["final_g"])


def _memkv_kernel(m_ref, g_ref, wk_ref, wv_ref, k_ref, v_ref):
    hn = _rms(m_ref[...], g_ref[...]).astype(BF16)
    k_ref[...] = _dot(hn, wk_ref[...])
    v_ref[...] = _dot(hn, wv_ref[...])


def _memkv(mem, w, tm):
    n = mem.shape[0]
    row = lambda: pl.BlockSpec((tm, D_MODEL), lambda i: (i, 0))
    return pl.pallas_call(
        _memkv_kernel,
        grid=(n // tm,),
        in_specs=[row(), _const_spec((1, D_MODEL)), _const_spec((D_MODEL, D_MODEL)),
                  _const_spec((D_MODEL, D_MODEL))],
        out_specs=[row(), row()],
        out_shape=[jax.ShapeDtypeStruct((n, D_MODEL), F32)] * 2,
        compiler_params=_params(("arbitrary",)),
        name="memkv",
    )(mem, w["mem_g"], w["xattn_wk"], w["xattn_wv"])


def _prep_weights(p):
    bf = lambda a: a.astype(BF16)
    row = lambda a: a.reshape(1, -1).astype(F32)
    w_in = p["w_in"]
    off = {}
    pos = 0
    for name, width in (("mq", 512), ("mk", 512), ("mv", 512), ("mi", 4), ("mf", 4), ("mo", 512), ("gq", 256),
                        ("gk", 256), ("gv", 512), ("ga", 16), ("gg", 512)):
        off[name] = w_in[:, pos:pos + width]
        pos += width
    pad_cols = SMALL_W - 2 * ML_HEADS - GLA_RANK
    w_in_q = jnp.concatenate([off[k] for k in ("mq", "mk", "mv", "gq", "gk", "gv")], axis=1)
    w_in_g = jnp.concatenate([off["mo"], off["gg"], off["mi"], off["mf"], off["ga"],
                              jnp.zeros((D_MODEL, pad_cols), F32)], axis=1)
    gate_bias = jnp.concatenate([p["mlstm_b_i"], p["mlstm_b_f"], jnp.zeros((SMALL_W - 2 * ML_HEADS,), F32)])
    w_a2 = jnp.zeros((SMALL_W, GLA_QK_W), F32).at[2 * ML_HEADS:2 * ML_HEADS + GLA_RANK].set(p["gla_w_a2"])
    qk_head = jnp.arange(GLA_QK_W) // GLA_DK
    v_head = jnp.arange(GLA_V_W) // GLA_DV
    head_mask = (qk_head[:, None] == v_head[None, :]).astype(F32)
    return dict(
        ffn1_g=row(p["ffn1_norm_g"]), ffn1_wg=bf(p["ffn1_w_gate"]), ffn1_wu=bf(p["ffn1_w_up"]),
        ffn1_wd=bf(p["ffn1_w_down"]), mix_g=row(p["mix_norm_g"]), w_in_q=bf(w_in_q), w_in_g=bf(w_in_g),
        gate_bias=row(gate_bias), w_a2=bf(w_a2), b_a=row(p["gla_b_a"]), mlstm_out_g=row(p["mlstm_out_g"]),
        gla_out_g=row(p["gla_out_g"]), head_sum=bf(head_mask), head_mask=head_mask,
        w_out=bf(p["w_out"]), xattn_g=row(p["xattn_norm_g"]), xattn_wq=bf(p["xattn_w_q"]),
        xattn_wo=bf(p["xattn_w_o"]), mem_g=row(p["mem_norm_g"]), xattn_wk=bf(p["xattn_w_k"]),
        xattn_wv=bf(p["xattn_w_v"]), ffn2_g=row(p["ffn2_norm_g"]), ffn2_wg=bf(p["ffn2_w_gate"]),
        ffn2_wu=bf(p["ffn2_w_up"]), ffn2_wd=bf(p["ffn2_w_down"]), final_g=row(p["final_g"]))


def _token_layers(x, mem_k, mem_v, state, w, *, tm, chunk, sub, pad_to, xattn_bb, xattn_tq, act_dtype):
    bsz, t, _ = x.shape
    n = bsz * t
    x1, zq, zg = _ffn_in(x.reshape(n, D_MODEL), w, tm, act_dtype)
    zq = zq.reshape(bsz, t, ZQ_W)
    zg = zg.reshape(bsz, t, ZG_W)
    if pad_to > t:
        zq = jnp.pad(zq, ((0, 0), (0, pad_to - t), (0, 0)))
        zg = jnp.pad(zg, ((0, 0), (0, pad_to - t), (0, 0)))
    ym, c_new, n_new, m_new, s_new = _mixer(zq, zg, w, chunk, sub, t, state, act_dtype)
    ym = ym[:, :t].reshape(n, D_MODEL)
    x2, q = _post_mix(x1, ym, w, tm, act_dtype)
    o = _xattn(q.reshape(bsz, t, D_MODEL), mem_k, mem_v, xattn_bb, xattn_tq, act_dtype)
    y = _ffn_out(x2, o.reshape(n, D_MODEL), w, tm)
    return y.reshape(bsz, t, D_MODEL), c_new, n_new, m_new.reshape(bsz, ML_HEADS), s_new


def kernel(x_prompt, x_sample, mem_prompt, cache_mem_k, cache_mem_v, state_mlstm_c, state_mlstm_n, state_mlstm_m, state_gla_s, ffn1_norm_g, ffn1_w_gate, ffn1_w_up, ffn1_w_down, mix_norm_g, w_in, mlstm_b_i, mlstm_b_f, mlstm_out_g, gla_w_a2, gla_b_a, gla_out_g, w_out, xattn_norm_g, mem_norm_g, xattn_w_q, xattn_w_k, xattn_w_v, xattn_w_o, ffn2_norm_g, ffn2_w_gate, ffn2_w_up, ffn2_w_down, final_norm_g):
    assert ffn1_norm_g.shape[0] == 1, "single-layer stack"
    layer = dict(ffn1_norm_g=ffn1_norm_g, ffn1_w_gate=ffn1_w_gate, ffn1_w_up=ffn1_w_up, ffn1_w_down=ffn1_w_down,
                 mix_norm_g=mix_norm_g, w_in=w_in, mlstm_b_i=mlstm_b_i, mlstm_b_f=mlstm_b_f,
                 mlstm_out_g=mlstm_out_g, gla_w_a2=gla_w_a2, gla_b_a=gla_b_a, gla_out_g=gla_out_g, w_out=w_out,
                 xattn_norm_g=xattn_norm_g, mem_norm_g=mem_norm_g, xattn_w_q=xattn_w_q, xattn_w_k=xattn_w_k,
                 xattn_w_v=xattn_w_v, xattn_w_o=xattn_w_o, ffn2_norm_g=ffn2_norm_g, ffn2_w_gate=ffn2_w_gate,
                 ffn2_w_up=ffn2_w_up, ffn2_w_down=ffn2_w_down)
    p = {name: arr[0] for name, arr in layer.items()}
    p["final_g"] = final_norm_g
    w = _prep_weights(p)

    bp, tp, _ = x_prompt.shape
    bs, ts, _ = x_sample.shape

    mem_k_p, mem_v_p = _memkv(mem_prompt.reshape(bp * N_MEM, D_MODEL), w, 512)
    mem_k_p = mem_k_p.reshape(bp, N_MEM, D_MODEL)
    mem_v_p = mem_v_p.reshape(bp, N_MEM, D_MODEL)
    y_p, c_p, n_p, m_p, s_p = _token_layers(
        x_prompt, mem_k_p, mem_v_p, None, w, tm=256, chunk=CHUNK, sub=16, pad_to=tp, xattn_bb=1, xattn_tq=512,
        act_dtype=BF16)

    state = (state_mlstm_c[0], state_mlstm_n[0], state_mlstm_m[0].reshape(bs, 1, ML_HEADS), state_gla_s[0])
    y_s, c_s, n_s, m_s, s_s = _token_layers(
        x_sample, cache_mem_k[0].reshape(bs, N_MEM, D_MODEL), cache_mem_v[0].reshape(bs, N_MEM, D_MODEL), state, w,
        tm=256, chunk=8, sub=8, pad_to=8, xattn_bb=4, xattn_tq=ts, act_dtype=F32)

    kv_shape = (1, bp, N_MEM, XA_HEADS, XA_DH)
    return (y_p, y_s, mem_k_p.reshape(kv_shape), mem_v_p.reshape(kv_shape), c_p[None], n_p[None], m_p[None],
            s_p[None], c_s[None], n_s[None], m_s[None], s_s[None])
```

```python
import functools

import jax
import jax.numpy as jnp
from jax import lax
from jax.experimental import pallas as pl
from jax.experimental.pallas import tpu as pltpu

F32 = jnp.float32
BF16 = jnp.bfloat16

D_MODEL = 1024
D_FF = 2816
ML_HEADS = 4
ML_DK = 128
ML_DV = 128
GLA_HEADS = 4
GLA_DK = 64
GLA_DV = 128
GLA_RANK = 16
GLA_TAU = 16.0
N_MEM = 256
XA_HEADS = 4
XA_DH = D_MODEL // XA_HEADS
EPS = 1e-6
CHUNK = 64
LANES = 128
SUBLANES = 8

ZQ_MQ, ZQ_MK, ZQ_MV = 0, 512, 1024
ZQ_GQ, ZQ_GK, ZQ_GV = 1536, 1792, 2048
ZQ_W = 2560
ZG_MO, ZG_GG, ZG_SMALL = 0, 512, 1024
ZG_W = 1152
SMALL_W = LANES
GLA_QK_W = GLA_HEADS * GLA_DK
GLA_V_W = GLA_HEADS * GLA_DV
GLA_PAIRS = GLA_HEADS // 2
GLA_SUB = SUBLANES

FF_CHUNK = D_FF // 2
VMEM_LIMIT_BYTES = 56 * 1024 * 1024


def _rms(x, g):
    return x * lax.rsqrt(jnp.mean(x * x, axis=-1, keepdims=True) + EPS) * g


def _log_sigmoid(x):
    return jnp.minimum(x, 0.0) - jnp.log1p(jnp.exp(-jnp.abs(x)))


def _dot(a, b):
    return jnp.dot(a, b, preferred_element_type=F32)


def _dot_nt(a, b):
    return lax.dot_general(a, b, (((1,), (1,)), ((), ())), preferred_element_type=F32)


def _dot_f32(a, b):
    return jnp.dot(a, b, precision=lax.Precision.HIGHEST, preferred_element_type=F32)


def _swiglu_residual(x, g_ref, wg_ref, wu_ref, wd_ref):
    h = _rms(x, g_ref[...]).astype(BF16)
    acc = jnp.zeros_like(x)
    for c in range(D_FF // FF_CHUNK):
        sl = slice(c * FF_CHUNK, (c + 1) * FF_CHUNK)
        g = _dot(h, wg_ref[:, sl])
        u = _dot(h, wu_ref[:, sl])
        a = (g * jax.nn.sigmoid(g)) * u
        acc = acc + _dot(a.astype(BF16), wd_ref[sl, :])
    return x + 0.5 * acc


def _const_spec(shape):
    nd = len(shape)
    return pl.BlockSpec(shape, lambda *_: (0,) * nd, pipeline_mode=pl.Buffered(1))


def _params(sem):
    return pltpu.CompilerParams(dimension_semantics=sem, vmem_limit_bytes=VMEM_LIMIT_BYTES)


def _ffn_in_kernel(x_ref, g1_ref, wg_ref, wu_ref, wd_ref, gm_ref, wq_ref, wgt_ref, x1_ref, zq_ref, zg_ref):
    x1 = _swiglu_residual(x_ref[...], g1_ref, wg_ref, wu_ref, wd_ref)
    x1_ref[...] = x1
    hm = _rms(x1, gm_ref[...]).astype(BF16)
    zq_ref[...] = _dot(hm, wq_ref[...]).astype(zq_ref.dtype)
    zg_ref[...] = _dot(hm, wgt_ref[...])


def _ffn_in(x, w, tm, zq_dtype):
    n = x.shape[0]
    row = lambda width: pl.BlockSpec((tm, width), lambda i: (i, 0))
    return pl.pallas_call(
        _ffn_in_kernel,
        grid=(n // tm,),
        in_specs=[row(D_MODEL), _const_spec((1, D_MODEL)), _const_spec((D_MODEL, D_FF)),
                  _const_spec((D_MODEL, D_FF)), _const_spec((D_FF, D_MODEL)), _const_spec((1, D_MODEL)),
                  _const_spec((D_MODEL, ZQ_W)), _const_spec((D_MODEL, ZG_W))],
        out_specs=[row(D_MODEL), row(ZQ_W), row(ZG_W)],
        out_shape=[jax.ShapeDtypeStruct((n, D_MODEL), F32), jax.ShapeDtypeStruct((n, ZQ_W), zq_dtype),
                   jax.ShapeDtypeStruct((n, ZG_W), F32)],
        compiler_params=_params(("arbitrary",)),
        name="ffn_in",
    )(x, w["ffn1_g"], w["ffn1_wg"], w["ffn1_wu"], w["ffn1_wd"], w["mix_g"], w["w_in_q"], w["w_in_g"])


def _mlstm_chunk(zq, zg, sm, lf, c_prev, n_prev, m_prev, gml, valid, *, L, padded):
    rr = lax.broadcasted_iota(jnp.int32, (L, L), 0)
    cc = lax.broadcasted_iota(jnp.int32, (L, L), 1)
    eye = rr == cc
    tril = cc <= rr
    triu = rr <= cc
    if L > SUBLANES:
        b_cols = _dot_f32(tril.astype(F32), lf)
        lf_t = lf.T
        sm_t = sm.T
        b_rows = _dot_f32(lf_t[0:SUBLANES], triu.astype(F32))
    ys, c_new, n_new, m_new = [], [], [], []
    for h in range(ML_HEADS):
        i_col = sm[:, h:h + 1]
        if L > SUBLANES:
            b_col = b_cols[:, ML_HEADS + h:ML_HEADS + h + 1]
            b_row = b_rows[ML_HEADS + h:ML_HEADS + h + 1, :]
            i_row = sm_t[h:h + 1, :]
        else:
            lf_col = lf[:, ML_HEADS + h:ML_HEADS + h + 1]
            i_row = jnp.sum(jnp.where(eye, i_col, 0.0), axis=0, keepdims=True)
            lf_row = jnp.sum(jnp.where(eye, lf_col, 0.0), axis=0, keepdims=True)
            b_row = jnp.sum(jnp.where(triu, lf_col, 0.0), axis=0, keepdims=True)
            b_col = jnp.sum(jnp.where(tril, lf_row, 0.0), axis=1, keepdims=True)
        a_col = b_col + m_prev[h]
        dm = jnp.where(tril, b_col - (b_row - i_row), -jnp.inf)
        mt = jnp.maximum(a_col, jnp.max(dm, axis=1, keepdims=True))
        w_inter = jnp.exp(a_col - mt)
        wmat = jnp.exp(dm - mt)

        qf = zq[:, ZQ_MQ + h * ML_DK:ZQ_MQ + (h + 1) * ML_DK].astype(F32)
        kf = zq[:, ZQ_MK + h * ML_DK:ZQ_MK + (h + 1) * ML_DK].astype(F32) * (ML_DK ** -0.5)
        vf = zq[:, ZQ_MV + h * ML_DV:ZQ_MV + (h + 1) * ML_DV].astype(F32)
        if padded:
            kf = jnp.where(valid, kf, 0.0)
            vf = jnp.where(valid, vf, 0.0)
        qb, kb, vb = qf.astype(BF16), kf.astype(BF16), vf.astype(BF16)
        s = _dot_nt(qb, kb) * wmat
        num = _dot(s.astype(BF16), vb) + w_inter * _dot(qb, c_prev[h].astype(BF16))
        den = jnp.sum(s, axis=1, keepdims=True) + w_inter * jnp.sum(qf * n_prev[h], axis=1, keepdims=True)
        hh = num / jnp.maximum(jnp.abs(den), jnp.exp(-mt))

        mt_last = mt[L - 1:L]
        i_last = w_inter[L - 1:L]
        wl_col = jnp.exp((b_col[L - 1:L] - mt_last) - (b_col - i_col))
        kw = kf * wl_col
        c_new.append(i_last * c_prev[h] + _dot(kw.T.astype(BF16), vb))
        n_new.append(i_last * n_prev[h] + jnp.sum(kw, axis=0, keepdims=True))
        m_new.append(mt_last)

        hs = slice(h * ML_DV, (h + 1) * ML_DV)
        yn = hh * lax.rsqrt(jnp.mean(hh * hh, axis=-1, keepdims=True) + EPS) * gml[:, hs]
        ys.append(jax.nn.sigmoid(zg[:, ZG_MO + h * ML_DV:ZG_MO + (h + 1) * ML_DV]) * yn)
    return ys, c_new, n_new, m_new


def _gla_chunk(zq, zg, small, s_prev, wa2, ba, ggl, ee_ref, valid, *, L, t_real, padded):
    c = GLA_SUB
    nb = L // c
    gq = zq[:, ZQ_GQ:ZQ_GQ + GLA_QK_W].astype(F32) * (GLA_DK ** -0.5)
    gk = zq[:, ZQ_GK:ZQ_GK + GLA_QK_W].astype(F32)
    gv = zq[:, ZQ_GV:ZQ_GV + GLA_V_W].astype(F32)
    la = _log_sigmoid(_dot(small.astype(BF16), wa2) + ba) * (1.0 / GLA_TAU)
    if padded:
        gk = jnp.where(valid, gk, 0.0)
        gv = jnp.where(valid, gv, 0.0)
        la = jnp.where(valid, la, 0.0)
    if L > SUBLANES:
        rr = lax.broadcasted_iota(jnp.int32, (L, L), 0)
        cc = lax.broadcasted_iota(jnp.int32, (L, L), 1)
        bc = _dot_f32((cc <= rr).astype(F32), la)
    else:
        acc = la[0:1]
        rows = [acc]
        for r in range(1, L):
            acc = acc + la[r:r + 1]
            rows.append(acc)
        bc = jnp.concatenate(rows, axis=0)

    stack = lambda x: jnp.concatenate([x[:, :LANES], x[:, LANES:]], axis=0)
    q2, k2, b2 = stack(gq), stack(gk), stack(bc)
    q3 = q2.reshape(2 * nb, c, LANES)
    k3 = k2.reshape(2 * nb, c, LANES)
    b3 = b2.reshape(2 * nb, c, LANES)

    t_in = lax.broadcasted_iota(jnp.int32, (1, c, 1), 1)
    acc = jnp.zeros((2 * L, LANES), F32)
    for j in range(min(c, t_real)):
        decay = jnp.exp(jnp.where(t_in >= j, b3 - b3[:, j:j + 1, :], -jnp.inf))
        pair_j = (q3 * k3[:, j:j + 1, :] * decay).reshape(2 * L, LANES)
        acc = acc + _dot(pair_j.astype(BF16), ee_ref[j])
    row_blk = (lax.broadcasted_iota(jnp.int32, (2 * L, 1), 0) % L) // c
    lane = lax.broadcasted_iota(jnp.int32, (1, LANES), 1)
    lane_blk = (lane % GLA_DK) // c
    lane_head = lane // GLA_DK
    a_diag = jnp.where(lane_blk == row_blk, acc, 0.0)

    if nb > 1:
        kt2 = (k3 * jnp.exp(b3[:, c - 1:c, :] - b3)).reshape(2 * L, LANES)
    v_lane_head = lax.broadcasted_iota(jnp.int32, (1, 2 * GLA_DV), 1) // GLA_DV
    ys, s_new = [], []
    for p in range(GLA_PAIRS):
        rows_p = slice(p * L, (p + 1) * L)
        q_p, k_p, b_p = q2[rows_p], k2[rows_p], b2[rows_p]
        a_p = a_diag[rows_p]
        if nb > 1:
            kt_p = kt2[rows_p].astype(BF16)
            k_bd = jnp.concatenate([jnp.where(lane_head == hh, kt_p, 0.0) for hh in range(2)], axis=0)
            slabs, offs = [], []
            off = 0
            for j in range(nb - 1):
                lo = (j + 1) * c
                slabs.append(q_p[lo:] * jnp.exp(b_p[lo:] - b_p[lo - 1:lo]))
                offs.append(off)
                off += L - lo
            r = _dot_nt(jnp.concatenate(slabs, axis=0).astype(BF16), k_bd)
            blocks = []
            for i in range(nb):
                blk = a_p[i * c:(i + 1) * c]
                for j in range(i):
                    lo_r = offs[j] + (i - j - 1) * c
                    blk = jnp.where(lane_blk == j, r[lo_r:lo_r + c], blk)
                blocks.append(blk)
            a_p = jnp.concatenate(blocks, axis=0)

        v_f = gv[:, p * 2 * GLA_DV:(p + 1) * 2 * GLA_DV]
        v_p = v_f.astype(BF16)
        if L < GLA_DK:
            v_rows = []
            for hh in range(2):
                v_rows += [jnp.where(v_lane_head == hh, v_f, 0.0), jnp.zeros((GLA_DK - L, 2 * GLA_DV), F32)]
            v_bd = jnp.concatenate(v_rows, axis=0).astype(BF16)
        else:
            v_bd = jnp.concatenate([jnp.where(v_lane_head == hh, v_p, 0.0) for hh in range(2)], axis=0)
        zero_blk = jnp.zeros((GLA_DK, GLA_DV), BF16)
        s_bd = jnp.concatenate(
            [jnp.concatenate([s_prev[2 * p].astype(BF16), zero_blk], axis=1),
             jnp.concatenate([zero_blk, s_prev[2 * p + 1].astype(BF16)], axis=1)], axis=0)
        o_p = _dot((q_p * jnp.exp(b_p)).astype(BF16), s_bd) + _dot(a_p.astype(BF16), v_bd)

        kh_t = (k_p * jnp.exp(b_p[L - 1:L] - b_p)).T.astype(BF16)
        decay_col = jnp.exp(b_p[L - SUBLANES:L].T[:, SUBLANES - 1:SUBLANES])
        for hh in range(2):
            h = 2 * p + hh
            ds = slice(hh * GLA_DK, (hh + 1) * GLA_DK)
            vs = slice(hh * GLA_DV, (hh + 1) * GLA_DV)
            s_new.append(decay_col[ds] * s_prev[h] + _dot(kh_t[ds], v_p[:, vs]))
            oh = o_p[:, vs]
            yn = oh * lax.rsqrt(jnp.mean(oh * oh, axis=-1, keepdims=True) + EPS) * ggl[:, h * GLA_DV:(h + 1) * GLA_DV]
            gg = zg[:, ZG_GG + h * GLA_DV:ZG_GG + (h + 1) * GLA_DV]
            ys.append((gg * jax.nn.sigmoid(gg)) * yn)
    return ys, s_new


def _mixer_kernel(*refs, L, t_real, has_state, n_chunks, bb):
    it = iter(refs)
    zq_ref, zg_ref, bias_ref, wa2_ref, ba_ref, gml_ref, ggl_ref, ee_ref = (next(it) for _ in range(8))
    if has_state:
        c0_ref, n0_ref, m0_ref, s0_ref = (next(it) for _ in range(4))
    y_ref, co_ref, no_ref, mo_ref, so_ref = (next(it) for _ in range(5))
    if not has_state:
        c_s, n_s, m_s, s_s = (next(it) for _ in range(4))
        ci = pl.program_id(1)

        @pl.when(ci == 0)
        def _init():
            c_s[...] = jnp.zeros_like(c_s)
            n_s[...] = jnp.zeros_like(n_s)
            m_s[...] = jnp.zeros_like(m_s)
            s_s[...] = jnp.zeros_like(s_s)

    padded = t_real < L
    valid = lax.broadcasted_iota(jnp.int32, (L, 1), 0) < t_real
    bias, wa2, ba, gml, ggl = bias_ref[...], wa2_ref[...], ba_ref[...], gml_ref[...], ggl_ref[...]
    lane = lax.broadcasted_iota(jnp.int32, (1, LANES), 1)
    for b in range(bb):
        zq = zq_ref[b]
        zg = zg_ref[b]
        if has_state:
            c_prev = [c0_ref[b, h] for h in range(ML_HEADS)]
            n_all = n0_ref[b]
            n_prev = [n_all[h:h + 1, :] for h in range(ML_HEADS)]
            m_all = m0_ref[b]
            m_prev = [m_all[:, h:h + 1] for h in range(ML_HEADS)]
            s_prev = [s0_ref[b, h] for h in range(GLA_HEADS)]
        else:
            c_prev = [c_s[h] for h in range(ML_HEADS)]
            n_prev = [n_s[h:h + 1, :] for h in range(ML_HEADS)]
            m_prev = [m_s[h:h + 1, 0:1] for h in range(ML_HEADS)]
            s_prev = [s_s[h] for h in range(GLA_HEADS)]

        small = zg[:, ZG_SMALL:ZG_SMALL + SMALL_W]
        sm = small + bias
        lf = _log_sigmoid(sm)
        if padded:
            sm = jnp.where(valid, sm, -jnp.inf)
            lf = jnp.where(valid, lf, 0.0)
        y_ml, c_new, n_new, m_new = _mlstm_chunk(zq, zg, sm, lf, c_prev, n_prev, m_prev, gml, valid, L=L,
                                                 padded=padded)
        y_gla, s_new = _gla_chunk(zq, zg, small, s_prev, wa2, ba, ggl, ee_ref, valid, L=L, t_real=t_real,
                                  padded=padded)

        y_ref[b] = jnp.concatenate(y_ml + y_gla, axis=1).astype(y_ref.dtype)
        m_row = jnp.zeros((1, LANES), F32)
        for h in range(ML_HEADS):
            m_row = jnp.where(lane == h, m_new[h], m_row)
        if has_state:
            for h in range(ML_HEADS):
                co_ref[b, h] = c_new[h]
            no_ref[b] = jnp.concatenate(n_new, axis=0)
            mo_ref[b] = m_row[:, 0:ML_HEADS]
            for h in range(GLA_HEADS):
                so_ref[b, h] = s_new[h]
        else:
            for h in range(ML_HEADS):
                c_s[h] = c_new[h]
                n_s[h:h + 1, :] = n_new[h]
                m_s[h:h + 1, :] = jnp.broadcast_to(m_new[h], (1, LANES))
                s_s[h] = s_new[h]

            @pl.when(ci == n_chunks - 1)
            def _emit_state():
                for h in range(ML_HEADS):
                    co_ref[b, h] = c_new[h]
                    so_ref[b, h] = s_new[h]
                no_ref[b] = jnp.concatenate(n_new, axis=0)
                mo_ref[b] = m_row[:, 0:ML_HEADS]


def _mixer(zq, zg, w, L, t_real, state, y_dtype, bb):
    bsz, t = zq.shape[0], zq.shape[1]
    n_chunks = t // L
    has_state = state is not None
    assert n_chunks == 1 if has_state else bb == 1
    seq = lambda width: pl.BlockSpec((bb, L, width), lambda b, j: (b, j, 0))
    per_b = lambda *tail: pl.BlockSpec((bb,) + tail, lambda b, j: (b,) + (0,) * len(tail))
    in_specs = [seq(ZQ_W), seq(ZG_W), _const_spec((1, SMALL_W)), _const_spec((SMALL_W, GLA_QK_W)),
                _const_spec((1, GLA_QK_W)), _const_spec((1, ML_HEADS * ML_DV)), _const_spec((1, GLA_V_W)),
                _const_spec((GLA_SUB, LANES, LANES))]
    args = [zq, zg, w["gate_bias"], w["w_a2"], w["b_a"], w["mlstm_out_g"], w["gla_out_g"], w["diag_sum"]]
    scratch = []
    if has_state:
        in_specs += [per_b(ML_HEADS, ML_DK, ML_DV), per_b(ML_HEADS, ML_DK), per_b(1, ML_HEADS),
                     per_b(GLA_HEADS, GLA_DK, GLA_DV)]
        args += list(state)
    else:
        scratch = [pltpu.VMEM((ML_HEADS, ML_DK, ML_DV), F32), pltpu.VMEM((SUBLANES, LANES), F32),
                   pltpu.VMEM((SUBLANES, LANES), F32), pltpu.VMEM((GLA_HEADS, GLA_DK, GLA_DV), F32)]
    out_specs = [seq(D_MODEL), per_b(ML_HEADS, ML_DK, ML_DV), per_b(ML_HEADS, ML_DK), per_b(1, ML_HEADS),
                 per_b(GLA_HEADS, GLA_DK, GLA_DV)]
    out_shape = [jax.ShapeDtypeStruct((bsz, t, D_MODEL), y_dtype),
                 jax.ShapeDtypeStruct((bsz, ML_HEADS, ML_DK, ML_DV), F32),
                 jax.ShapeDtypeStruct((bsz, ML_HEADS, ML_DK), F32),
                 jax.ShapeDtypeStruct((bsz, 1, ML_HEADS), F32),
                 jax.ShapeDtypeStruct((bsz, GLA_HEADS, GLA_DK, GLA_DV), F32)]
    kern = functools.partial(_mixer_kernel, L=L, t_real=t_real, has_state=has_state, n_chunks=n_chunks, bb=bb)
    return pl.pallas_call(
        kern,
        grid=(bsz // bb, n_chunks),
        in_specs=in_specs,
        out_specs=out_specs,
        out_shape=out_shape,
        scratch_shapes=scratch,
        compiler_params=_params(("arbitrary", "arbitrary")),
        name="mixer_state" if has_state else "mixer",
    )(*args)


def _post_mix_kernel(x1_ref, ym_ref, wout_ref, gx_ref, wq_ref, x2_ref, q_ref):
    x2 = x1_ref[...] + _dot(ym_ref[...].astype(BF16), wout_ref[...])
    x2_ref[...] = x2
    hq = _rms(x2, gx_ref[...]).astype(BF16)
    q_ref[...] = _dot(hq, wq_ref[...]).astype(q_ref.dtype)


def _post_mix(x1, ym, w, tm, q_dtype):
    n = x1.shape[0]
    row = lambda: pl.BlockSpec((tm, D_MODEL), lambda i: (i, 0))
    return pl.pallas_call(
        _post_mix_kernel,
        grid=(n // tm,),
        in_specs=[row(), row(), _const_spec((D_MODEL, D_MODEL)), _const_spec((1, D_MODEL)),
                  _const_spec((D_MODEL, D_MODEL))],
        out_specs=[row(), row()],
        out_shape=[jax.ShapeDtypeStruct((n, D_MODEL), F32), jax.ShapeDtypeStruct((n, D_MODEL), q_dtype)],
        compiler_params=_params(("arbitrary",)),
        name="post_mix",
    )(x1, ym, w["w_out"], w["xattn_g"], w["xattn_wq"])


def _softmax(s):
    e = jnp.exp(s - jnp.max(s, axis=-1, keepdims=True))
    return e / jnp.sum(e, axis=-1, keepdims=True)


def _xattn_kernel(q_ref, k_ref, v_ref, o_ref):
    for h in range(XA_HEADS):
        hs = slice(h * XA_DH, (h + 1) * XA_DH)
        s = _dot_nt(q_ref[0, :, hs].astype(BF16), k_ref[0, :, hs].astype(BF16)) * (XA_DH ** -0.5)
        o_ref[0, :, hs] = _dot(_softmax(s).astype(BF16), v_ref[0, :, hs].astype(BF16)).astype(o_ref.dtype)


def _xattn(q, k, v, tq, o_dtype):
    bsz, t = q.shape[0], q.shape[1]
    qo = pl.BlockSpec((1, tq, D_MODEL), lambda b, j: (b, j, 0))
    kv = pl.BlockSpec((1, N_MEM, D_MODEL), lambda b, j: (b, 0, 0))
    return pl.pallas_call(
        _xattn_kernel,
        grid=(bsz, t // tq),
        in_specs=[qo, kv, kv],
        out_specs=qo,
        out_shape=jax.ShapeDtypeStruct((bsz, t, D_MODEL), o_dtype),
        compiler_params=_params(("arbitrary", "arbitrary")),
        name="xattn",
    )(q, k, v)


XA_LANE_TILES = XA_DH // LANES
CACHE_ROW_GROUP = XA_LANE_TILES * XA_HEADS


def _cache_rows_view(x):
    bsz = x.shape[0]
    x = x.reshape(bsz, N_MEM, XA_HEADS, XA_LANE_TILES, LANES)
    return x.transpose(0, 1, 3, 2, 4).reshape(bsz, N_MEM * CACHE_ROW_GROUP, LANES)


def _cache_rows_load(ref, b):
    cols = [ref[b, pl.ds(lt * XA_HEADS + h, N_MEM, stride=CACHE_ROW_GROUP), :]
            for h in range(XA_HEADS) for lt in range(XA_LANE_TILES)]
    return jnp.concatenate(cols, axis=1).astype(BF16)


def _xattn_cache_kernel(q_ref, k_ref, v_ref, o_ref, *, bb, tq):
    lane_head = lax.broadcasted_iota(jnp.int32, (1, D_MODEL), 1) // XA_DH
    scores = []
    for b in range(bb):
        q = q_ref[b]
        q_bd = jnp.concatenate([jnp.where(lane_head == h, q, 0.0) for h in range(XA_HEADS)], axis=0)
        k_full = _cache_rows_load(k_ref, b)
        scores.append(_dot_nt(q_bd.astype(BF16), k_full) * (XA_DH ** -0.5))
    p_all = _softmax(jnp.concatenate(scores, axis=0)).astype(BF16)
    rows = XA_HEADS * tq
    for b in range(bb):
        v_full = _cache_rows_load(v_ref, b)
        o_full = _dot(p_all[b * rows:(b + 1) * rows], v_full)
        o = jnp.zeros((tq, D_MODEL), F32)
        for h in range(XA_HEADS):
            o = jnp.where(lane_head == h, o_full[h * tq:(h + 1) * tq], o)
        o_ref[b] = o.astype(o_ref.dtype)


def _xattn_cache(q, k, v, bb, o_dtype):
    bsz, tq = q.shape[0], q.shape[1]
    k, v = _cache_rows_view(k), _cache_rows_view(v)
    qo = pl.BlockSpec((bb, tq, D_MODEL), lambda b: (b, 0, 0))
    kv = pl.BlockSpec((bb, N_MEM * CACHE_ROW_GROUP, LANES), lambda b: (b, 0, 0))
    return pl.pallas_call(
        functools.partial(_xattn_cache_kernel, bb=bb, tq=tq),
        grid=(bsz // bb,),
        in_specs=[qo, kv, kv],
        out_specs=qo,
        out_shape=jax.ShapeDtypeStruct((bsz, tq, D_MODEL), o_dtype),
        compiler_params=_params(("arbitrary",)),
        name="xattn_cache",
    )(q, k, v)


def _ffn_out_kernel(x2_ref, o_ref, wo_ref, g2_ref, wg_ref, wu_ref, wd_ref, gf_ref, y_ref):
    x3 = x2_ref[...] + _dot(o_ref[...].astype(BF16), wo_ref[...])
    x4 = _swiglu_residual(x3, g2_ref, wg_ref, wu_ref, wd_ref)
    y_ref[...] = _rms(x4, gf_ref[...])


def _ffn_out(x2, o, w, tm):
    n = x2.shape[0]
    row = lambda: pl.BlockSpec((tm, D_MODEL), lambda i: (i, 0))
    return pl.pallas_call(
        _ffn_out_kernel,
        grid=(n // tm,),
        in_specs=[row(), row(), _const_spec((D_MODEL, D_MODEL)), _const_spec((1, D_MODEL)),
                  _const_spec((D_MODEL, D_FF)), _const_spec((D_MODEL, D_FF)), _const_spec((D_FF, D_MODEL)),
                  _const_spec((1, D_MODEL))],
        out_specs=row(),
        out_shape=jax.ShapeDtypeStruct((n, D_MODEL), F32),
        compiler_params=_params(("arbitrary",)),
        name="ffn_out",
    )(x2, o, w["xattn_wo"], w["ffn2_g"], w["ffn2_wg"], w["ffn2_wu"], w["ffn2_wd"], w["final_g"])


def _memkv_kernel(m_ref, g_ref, wk_ref, wv_ref, k_ref, v_ref):
    hn = _rms(m_ref[...], g_ref[...]).astype(BF16)
    k_ref[...] = _dot(hn, wk_ref[...])
    v_ref[...] = _dot(hn, wv_ref[...])


def _memkv(mem, w, tm):
    n = mem.shape[0]
    row = lambda: pl.BlockSpec((tm, D_MODEL), lambda i: (i, 0))
    return pl.pallas_call(
        _memkv_kernel,
        grid=(n // tm,),
        in_specs=[row(), _const_spec((1, D_MODEL)), _const_spec((D_MODEL, D_MODEL)),
                  _const_spec((D_MODEL, D_MODEL))],
        out_specs=[row(), row()],
        out_shape=[jax.ShapeDtypeStruct((n, D_MODEL), F32)] * 2,
        compiler_params=_params(("arbitrary",)),
        name="memkv",
    )(mem, w["mem_g"], w["xattn_wk"], w["xattn_wv"])


def _prep_weights(p):
    bf = lambda a: a.astype(BF16)
    row = lambda a: a.reshape(1, -1).astype(F32)
    w_in = p["w_in"]
    off = {}
    pos = 0
    for name, width in (("mq", 512), ("mk", 512), ("mv", 512), ("mi", 4), ("mf", 4), ("mo", 512), ("gq", 256),
                        ("gk", 256), ("gv", 512), ("ga", 16), ("gg", 512)):
        off[name] = w_in[:, pos:pos + width]
        pos += width
    pad_cols = SMALL_W - 2 * ML_HEADS - GLA_RANK
    w_in_q = jnp.concatenate([off[k] for k in ("mq", "mk", "mv", "gq", "gk", "gv")], axis=1)
    w_in_g = jnp.concatenate([off["mo"], off["gg"], off["mi"], off["mf"], off["ga"],
                              jnp.zeros((D_MODEL, pad_cols), F32)], axis=1)
    gate_bias = jnp.concatenate([p["mlstm_b_i"], p["mlstm_b_f"], jnp.zeros((SMALL_W - 2 * ML_HEADS,), F32)])
    w_a2 = jnp.zeros((SMALL_W, GLA_QK_W), F32).at[2 * ML_HEADS:2 * ML_HEADS + GLA_RANK].set(p["gla_w_a2"])
    lane = jnp.arange(LANES)
    same_head = (lane[:, None] // GLA_DK) == (lane[None, :] // GLA_DK)
    diag_sum = jnp.stack([same_head & ((lane[None, :] % GLA_SUB) == j) for j in range(GLA_SUB)]).astype(BF16)
    return dict(
        ffn1_g=row(p["ffn1_norm_g"]), ffn1_wg=bf(p["ffn1_w_gate"]), ffn1_wu=bf(p["ffn1_w_up"]),
        ffn1_wd=bf(p["ffn1_w_down"]), mix_g=row(p["mix_norm_g"]), w_in_q=bf(w_in_q), w_in_g=bf(w_in_g),
        gate_bias=row(gate_bias), w_a2=bf(w_a2), b_a=row(p["gla_b_a"]), mlstm_out_g=row(p["mlstm_out_g"]),
        gla_out_g=row(p["gla_out_g"]), diag_sum=diag_sum,
        w_out=bf(p["w_out"]), xattn_g=row(p["xattn_norm_g"]), xattn_wq=bf(p["xattn_w_q"]),
        xattn_wo=bf(p["xattn_w_o"]), mem_g=row(p["mem_norm_g"]), xattn_wk=bf(p["xattn_w_k"]),
        xattn_wv=bf(p["xattn_w_v"]), ffn2_g=row(p["ffn2_norm_g"]), ffn2_wg=bf(p["ffn2_w_gate"]),
        ffn2_wu=bf(p["ffn2_w_up"]), ffn2_wd=bf(p["ffn2_w_down"]), final_g=row(p["final_g"]))


def _token_layers(x, attend, state, w, *, tm, chunk, pad_to, mixer_bb, act_dtype):
    bsz, t, _ = x.shape
    n = bsz * t
    x1, zq, zg = _ffn_in(x.reshape(n, D_MODEL), w, tm, act_dtype)
    zq = zq.reshape(bsz, t, ZQ_W)
    zg = zg.reshape(bsz, t, ZG_W)
    if pad_to > t:
        zq = jnp.pad(zq, ((0, 0), (0, pad_to - t), (0, 0)))
        zg = jnp.pad(zg, ((0, 0), (0, pad_to - t), (0, 0)))
    ym, c_new, n_new, m_new, s_new = _mixer(zq, zg, w, chunk, t, state, act_dtype, mixer_bb)
    ym = ym[:, :t].reshape(n, D_MODEL)
    x2, q = _post_mix(x1, ym, w, tm, act_dtype)
    o = attend(q.reshape(bsz, t, D_MODEL))
    y = _ffn_out(x2, o.reshape(n, D_MODEL), w, tm)
    return y.reshape(bsz, t, D_MODEL), c_new, n_new, m_new.reshape(bsz, ML_HEADS), s_new


def kernel(x_prompt, x_sample, mem_prompt, cache_mem_k, cache_mem_v, state_mlstm_c, state_mlstm_n, state_mlstm_m, state_gla_s, ffn1_norm_g, ffn1_w_gate, ffn1_w_up, ffn1_w_down, mix_norm_g, w_in, mlstm_b_i, mlstm_b_f, mlstm_out_g, gla_w_a2, gla_b_a, gla_out_g, w_out, xattn_norm_g, mem_norm_g, xattn_w_q, xattn_w_k, xattn_w_v, xattn_w_o, ffn2_norm_g, ffn2_w_gate, ffn2_w_up, ffn2_w_down, final_norm_g):
    assert ffn1_norm_g.shape[0] == 1, "single-layer stack"
    layer = dict(ffn1_norm_g=ffn1_norm_g, ffn1_w_gate=ffn1_w_gate, ffn1_w_up=ffn1_w_up, ffn1_w_down=ffn1_w_down,
                 mix_norm_g=mix_norm_g, w_in=w_in, mlstm_b_i=mlstm_b_i, mlstm_b_f=mlstm_b_f,
                 mlstm_out_g=mlstm_out_g, gla_w_a2=gla_w_a2, gla_b_a=gla_b_a, gla_out_g=gla_out_g, w_out=w_out,
                 xattn_norm_g=xattn_norm_g, mem_norm_g=mem_norm_g, xattn_w_q=xattn_w_q, xattn_w_k=xattn_w_k,
                 xattn_w_v=xattn_w_v, xattn_w_o=xattn_w_o, ffn2_norm_g=ffn2_norm_g, ffn2_w_gate=ffn2_w_gate,
                 ffn2_w_up=ffn2_w_up, ffn2_w_down=ffn2_w_down)
    p = {name: arr[0] for name, arr in layer.items()}
    p["final_g"] = final_norm_g
    w = _prep_weights(p)

    bp, tp, _ = x_prompt.shape
    bs, ts, _ = x_sample.shape

    mem_k_p, mem_v_p = _memkv(mem_prompt.reshape(bp * N_MEM, D_MODEL), w, 512)
    mem_k_p = mem_k_p.reshape(bp, N_MEM, D_MODEL)
    mem_v_p = mem_v_p.reshape(bp, N_MEM, D_MODEL)
    y_p, c_p, n_p, m_p, s_p = _token_layers(
        x_prompt, lambda q: _xattn(q, mem_k_p, mem_v_p, 512, BF16), None, w, tm=256, chunk=CHUNK, pad_to=tp,
        mixer_bb=1, act_dtype=BF16)

    state = (state_mlstm_c[0], state_mlstm_n[0], state_mlstm_m[0].reshape(bs, 1, ML_HEADS), state_gla_s[0])
    y_s, c_s, n_s, m_s, s_s = _token_layers(
        x_sample, lambda q: _xattn_cache(q, cache_mem_k[0], cache_mem_v[0], 4, F32), state, w, tm=256,
        chunk=SUBLANES, pad_to=SUBLANES, mixer_bb=8, act_dtype=F32)

    kv_shape = (1, bp, N_MEM, XA_HEADS, XA_DH)
    return (y_p, y_s, mem_k_p.reshape(kv_shape), mem_v_p.reshape(kv_shape), c_p[None], n_p[None], m_p[None],
            s_p[None], c_s[None], n_s[None], m_s[None], s_s[None])
```

```python
import functools

import jax
import jax.numpy as jnp
from jax import lax
from jax.experimental import pallas as pl
from jax.experimental.pallas import tpu as pltpu

F32 = jnp.float32
BF16 = jnp.bfloat16

D_MODEL = 1024
D_FF = 2816
ML_HEADS = 4
ML_DK = 128
ML_DV = 128
GLA_HEADS = 4
GLA_DK = 64
GLA_DV = 128
GLA_RANK = 16
GLA_TAU = 16.0
N_MEM = 256
XA_HEADS = 4
XA_DH = D_MODEL // XA_HEADS
EPS = 1e-6
CHUNK = 64
LANES = 128
SUBLANES = 8

ZQ_MQ, ZQ_MK, ZQ_MV = 0, 512, 1024
ZQ_GQ, ZQ_GK, ZQ_GV = 1536, 1792, 2048
ZQ_W = 2560
ZG_MO, ZG_GG, ZG_SMALL = 0, 512, 1024
ZG_W = 1152
SMALL_W = LANES
GLA_QK_W = GLA_HEADS * GLA_DK
GLA_V_W = GLA_HEADS * GLA_DV
GLA_PAIRS = GLA_HEADS // 2
GLA_SUB = SUBLANES

FF_CHUNK = D_FF // 2
VMEM_LIMIT_BYTES = 56 * 1024 * 1024


def _rms(x, g):
    return x * lax.rsqrt(jnp.mean(x * x, axis=-1, keepdims=True) + EPS) * g


def _log_sigmoid(x):
    return jnp.minimum(x, 0.0) - jnp.log1p(jnp.exp(-jnp.abs(x)))


def _dot(a, b):
    return jnp.dot(a, b, preferred_element_type=F32)


def _dot_nt(a, b):
    return lax.dot_general(a, b, (((1,), (1,)), ((), ())), preferred_element_type=F32)


def _dot_f32(a, b):
    return jnp.dot(a, b, precision=lax.Precision.HIGHEST, preferred_element_type=F32)


def _swiglu_residual(x, g_ref, wg_ref, wu_ref, wd_ref):
    h = _rms(x, g_ref[...]).astype(BF16)
    acc = jnp.zeros_like(x)
    for c in range(D_FF // FF_CHUNK):
        sl = slice(c * FF_CHUNK, (c + 1) * FF_CHUNK)
        g = _dot(h, wg_ref[:, sl])
        u = _dot(h, wu_ref[:, sl])
        a = (g * jax.nn.sigmoid(g)) * u
        acc = acc + _dot(a.astype(BF16), wd_ref[sl, :])
    return x + 0.5 * acc


def _const_spec(shape):
    nd = len(shape)
    return pl.BlockSpec(shape, lambda *_: (0,) * nd, pipeline_mode=pl.Buffered(1))


def _params(sem):
    return pltpu.CompilerParams(dimension_semantics=sem, vmem_limit_bytes=VMEM_LIMIT_BYTES)


def _ffn_in_kernel(x_ref, g1_ref, wg_ref, wu_ref, wd_ref, gm_ref, wq_ref, wgt_ref, x1_ref, zq_ref, zg_ref):
    x1 = _swiglu_residual(x_ref[...], g1_ref, wg_ref, wu_ref, wd_ref)
    x1_ref[...] = x1
    hm = _rms(x1, gm_ref[...]).astype(BF16)
    zq_ref[...] = _dot(hm, wq_ref[...]).astype(zq_ref.dtype)
    zg_ref[...] = _dot(hm, wgt_ref[...])


def _ffn_in(x, w, tm, zq_dtype):
    n = x.shape[0]
    row = lambda width: pl.BlockSpec((tm, width), lambda i: (i, 0))
    return pl.pallas_call(
        _ffn_in_kernel,
        grid=(n // tm,),
        in_specs=[row(D_MODEL), _const_spec((1, D_MODEL)), _const_spec((D_MODEL, D_FF)),
                  _const_spec((D_MODEL, D_FF)), _const_spec((D_FF, D_MODEL)), _const_spec((1, D_MODEL)),
                  _const_spec((D_MODEL, ZQ_W)), _const_spec((D_MODEL, ZG_W))],
        out_specs=[row(D_MODEL), row(ZQ_W), row(ZG_W)],
        out_shape=[jax.ShapeDtypeStruct((n, D_MODEL), F32), jax.ShapeDtypeStruct((n, ZQ_W), zq_dtype),
                   jax.ShapeDtypeStruct((n, ZG_W), F32)],
        compiler_params=_params(("arbitrary",)),
        name="ffn_in",
    )(x, w["ffn1_g"], w["ffn1_wg"], w["ffn1_wu"], w["ffn1_wd"], w["mix_g"], w["w_in_q"], w["w_in_g"])


def _mixer_compute(zqs, zgs, states, consts, ee_ref, *, L, t_real):
    bias, wa2, ba, gml, ggl = consts
    nchunk = len(zqs)
    groups = [(b, h) for b in range(nchunk) for h in range(ML_HEADS)]
    pairs = [(b, p) for b in range(nchunk) for p in range(GLA_PAIRS)]
    padded = t_real < L
    valid = lax.broadcasted_iota(jnp.int32, (L, 1), 0) < t_real
    rr = lax.broadcasted_iota(jnp.int32, (L, L), 0)
    cc = lax.broadcasted_iota(jnp.int32, (L, L), 1)
    tril = cc <= rr
    c = GLA_SUB
    nb = L // c
    lane = lax.broadcasted_iota(jnp.int32, (1, LANES), 1)
    lane_blk = (lane % GLA_DK) // c
    lane_head = lane // GLA_DK
    v_lane_head = lax.broadcasted_iota(jnp.int32, (1, 2 * GLA_DV), 1) // GLA_DV
    row_blk = (lax.broadcasted_iota(jnp.int32, (2 * L, 1), 0) % L) // c
    t_in = lax.broadcasted_iota(jnp.int32, (1, c, 1), 1)

    smalls, sms, lfs, b_cols, b_rows, sm_ts, las, bcs = [], [], [], [], [], [], [], []
    for b in range(nchunk):
        small = zgs[b][:, ZG_SMALL:ZG_SMALL + SMALL_W]
        sm = small + bias
        lf = _log_sigmoid(sm)
        if padded:
            sm = jnp.where(valid, sm, -jnp.inf)
            lf = jnp.where(valid, lf, 0.0)
        smalls.append(small)
        sms.append(sm)
        lfs.append(lf)
    def cumsum_rows(x):
        if L > SUBLANES:
            return _dot_f32(tril.astype(F32), x)
        acc = x[0:1]
        rows = [acc]
        for r in range(1, L):
            acc = acc + x[r:r + 1]
            rows.append(acc)
        return jnp.concatenate(rows, axis=0)

    triu = (rr <= cc).astype(F32)
    for b in range(nchunk):
        b_cols.append(cumsum_rows(lfs[b]))
        b_rows.append(_dot_f32(lfs[b].T[0:SUBLANES], triu) if L > SUBLANES else b_cols[b].T[0:SUBLANES])
        sm_ts.append(sms[b].T)
        la = _log_sigmoid(_dot(smalls[b].astype(BF16), wa2) + ba) * (1.0 / GLA_TAU)
        las.append(jnp.where(valid, la, 0.0) if padded else la)
    for b in range(nchunk):
        bcs.append(cumsum_rows(las[b]))

    ml = {}
    for g in groups:
        b, h = g
        zq = zqs[b]
        qf = zq[:, ZQ_MQ + h * ML_DK:ZQ_MQ + (h + 1) * ML_DK].astype(F32)
        kf = zq[:, ZQ_MK + h * ML_DK:ZQ_MK + (h + 1) * ML_DK].astype(F32) * (ML_DK ** -0.5)
        vf = zq[:, ZQ_MV + h * ML_DV:ZQ_MV + (h + 1) * ML_DV].astype(F32)
        if padded:
            kf = jnp.where(valid, kf, 0.0)
            vf = jnp.where(valid, vf, 0.0)
        qb, kb, vb = qf.astype(BF16), kf.astype(BF16), vf.astype(BF16)
        ml[g] = dict(qf=qf, kf=kf, vb=vb, qk=_dot_nt(qb, kb), qc=_dot(qb, states[b]["c"][h].astype(BF16)))

    for g in groups:
        b, h = g
        d = ml[g]
        i_col = sms[b][:, h:h + 1]
        b_col = b_cols[b][:, ML_HEADS + h:ML_HEADS + h + 1]
        b_row = b_rows[b][ML_HEADS + h:ML_HEADS + h + 1, :]
        i_row = sm_ts[b][h:h + 1, :]
        a_col = b_col + states[b]["m"][h]
        dm = jnp.where(tril, b_col - (b_row - i_row), -jnp.inf)
        mt = jnp.maximum(a_col, jnp.max(dm, axis=1, keepdims=True))
        w_inter = jnp.exp(a_col - mt)
        s = d["qk"] * jnp.exp(dm - mt)
        kw = d["kf"] * jnp.exp((b_col[L - 1:L] - mt[L - 1:L]) - (b_col - i_col))
        d.update(mt=mt, w_inter=w_inter, s=s, kw=kw, kw_t=kw.T.astype(BF16))

    gl = {}
    for b in range(nchunk):
        zq = zqs[b]
        gq = zq[:, ZQ_GQ:ZQ_GQ + GLA_QK_W].astype(F32) * (GLA_DK ** -0.5)
        gk = zq[:, ZQ_GK:ZQ_GK + GLA_QK_W].astype(F32)
        gv = zq[:, ZQ_GV:ZQ_GV + GLA_V_W].astype(F32)
        if padded:
            gk = jnp.where(valid, gk, 0.0)
            gv = jnp.where(valid, gv, 0.0)
        stack = lambda x: jnp.concatenate([x[:, :LANES], x[:, LANES:]], axis=0)
        q2, k2, b2 = stack(gq), stack(gk), stack(bcs[b])
        q3 = q2.reshape(2 * nb, c, LANES)
        k3 = k2.reshape(2 * nb, c, LANES)
        b3 = b2.reshape(2 * nb, c, LANES)
        acc = jnp.zeros((2 * L, LANES), F32)
        for j in range(min(c, t_real)):
            decay = jnp.exp(jnp.where(t_in >= j, b3 - b3[:, j:j + 1, :], -jnp.inf))
            pair_j = (q3 * k3[:, j:j + 1, :] * decay).reshape(2 * L, LANES)
            acc = acc + _dot(pair_j.astype(BF16), ee_ref[j])
        a_diag = jnp.where(lane_blk == row_blk, acc, 0.0)
        kt2 = (k3 * jnp.exp(b3[:, c - 1:c, :] - b3)).reshape(2 * L, LANES) if nb > 1 else None
        gl[b] = dict(gv=gv, q2=q2, k2=k2, b2=b2, a_diag=a_diag, kt2=kt2)

    for bp in pairs:
        b, p = bp
        d = gl[b]
        rows_p = slice(p * L, (p + 1) * L)
        q_p, k_p, b_p = d["q2"][rows_p], d["k2"][rows_p], d["b2"][rows_p]
        s_prev = states[b]["s"]
        zero_blk = jnp.zeros((GLA_DK, GLA_DV), BF16)
        s_bd = jnp.concatenate(
            [jnp.concatenate([s_prev[2 * p].astype(BF16), zero_blk], axis=1),
             jnp.concatenate([zero_blk, s_prev[2 * p + 1].astype(BF16)], axis=1)], axis=0)
        e = dict(o_inter=_dot((q_p * jnp.exp(b_p)).astype(BF16), s_bd),
                 kh_t=(k_p * jnp.exp(b_p[L - 1:L] - b_p)).T.astype(BF16),
                 decay_col=jnp.exp(b_p[L - SUBLANES:L].T[:, SUBLANES - 1:SUBLANES]))
        if nb > 1:
            kt_p = d["kt2"][rows_p].astype(BF16)
            k_bd = jnp.concatenate([jnp.where(lane_head == hh, kt_p, 0.0) for hh in range(2)], axis=0)
            slabs, offs = [], []
            off = 0
            for j in range(nb - 1):
                lo = (j + 1) * c
                slabs.append(q_p[lo:] * jnp.exp(b_p[lo:] - b_p[lo - 1:lo]))
                offs.append(off)
                off += L - lo
            e["r"] = _dot_nt(jnp.concatenate(slabs, axis=0).astype(BF16), k_bd)
            e["offs"] = offs
        gl[bp] = e

    for g in groups:
        d = ml[g]
        d["sv"] = _dot(d["s"].astype(BF16), d["vb"])
        d["c_upd"] = _dot(d["kw_t"], d["vb"])
        b, h = g
        d["den"] = (jnp.sum(d["s"], axis=1, keepdims=True)
                    + d["w_inter"] * jnp.sum(d["qf"] * states[b]["n"][h], axis=1, keepdims=True))
    for bp in pairs:
        b, p = bp
        d, e = gl[b], gl[bp]
        a_p = d["a_diag"][p * L:(p + 1) * L]
        if nb > 1:
            blocks = []
            for i in range(nb):
                blk = a_p[i * c:(i + 1) * c]
                for j in range(i):
                    lo_r = e["offs"][j] + (i - j - 1) * c
                    blk = jnp.where(lane_blk == j, e["r"][lo_r:lo_r + c], blk)
                blocks.append(blk)
            a_p = jnp.concatenate(blocks, axis=0)
        v_f = d["gv"][:, p * 2 * GLA_DV:(p + 1) * 2 * GLA_DV]
        v_p = v_f.astype(BF16)
        if L < GLA_DK:
            v_rows = []
            for hh in range(2):
                v_rows += [jnp.where(v_lane_head == hh, v_f, 0.0), jnp.zeros((GLA_DK - L, 2 * GLA_DV), F32)]
            v_bd = jnp.concatenate(v_rows, axis=0).astype(BF16)
        else:
            v_bd = jnp.concatenate([jnp.where(v_lane_head == hh, v_p, 0.0) for hh in range(2)], axis=0)
        e["o"] = e["o_inter"] + _dot(a_p.astype(BF16), v_bd)
        e["s_upd"] = [_dot(e["kh_t"][hh * GLA_DK:(hh + 1) * GLA_DK], v_p[:, hh * GLA_DV:(hh + 1) * GLA_DV])
                      for hh in range(2)]

    ys = [[None] * (ML_HEADS + GLA_HEADS) for _ in range(nchunk)]
    new_states = [dict(c=[None] * ML_HEADS, n=[None] * ML_HEADS, m=[None] * ML_HEADS, s=[None] * GLA_HEADS)
                  for _ in range(nchunk)]
    for g in groups:
        b, h = g
        d = ml[g]
        mt, w_inter = d["mt"], d["w_inter"]
        hh = (d["sv"] + w_inter * d["qc"]) / jnp.maximum(jnp.abs(d["den"]), jnp.exp(-mt))
        i_last = w_inter[L - 1:L]
        st = states[b]
        new_states[b]["c"][h] = i_last * st["c"][h] + d["c_upd"]
        new_states[b]["n"][h] = i_last * st["n"][h] + jnp.sum(d["kw"], axis=0, keepdims=True)
        new_states[b]["m"][h] = mt[L - 1:L]
        hs = slice(h * ML_DV, (h + 1) * ML_DV)
        yn = hh * lax.rsqrt(jnp.mean(hh * hh, axis=-1, keepdims=True) + EPS) * gml[:, hs]
        ys[b][h] = jax.nn.sigmoid(zgs[b][:, ZG_MO + h * ML_DV:ZG_MO + (h + 1) * ML_DV]) * yn
    for bp in pairs:
        b, p = bp
        e = gl[bp]
        for hh in range(2):
            h = 2 * p + hh
            ds = slice(hh * GLA_DK, (hh + 1) * GLA_DK)
            vs = slice(hh * GLA_DV, (hh + 1) * GLA_DV)
            new_states[b]["s"][h] = e["decay_col"][ds] * states[b]["s"][h] + e["s_upd"][hh]
            oh = e["o"][:, vs]
            yn = oh * lax.rsqrt(jnp.mean(oh * oh, axis=-1, keepdims=True) + EPS) * ggl[:, h * GLA_DV:(h + 1) * GLA_DV]
            gg = zgs[b][:, ZG_GG + h * GLA_DV:ZG_GG + (h + 1) * GLA_DV]
            ys[b][ML_HEADS + h] = (gg * jax.nn.sigmoid(gg)) * yn
    return [jnp.concatenate(y, axis=1) for y in ys], new_states


def _mixer_kernel(*refs, L, t_real, has_state, n_chunks, bb):
    it = iter(refs)
    zq_ref, zg_ref, bias_ref, wa2_ref, ba_ref, gml_ref, ggl_ref, ee_ref = (next(it) for _ in range(8))
    if has_state:
        c0_ref, n0_ref, m0_ref, s0_ref = (next(it) for _ in range(4))
    y_ref, co_ref, no_ref, mo_ref, so_ref = (next(it) for _ in range(5))
    if not has_state:
        c_s, n_s, m_s, s_s = (next(it) for _ in range(4))
        ci = pl.program_id(1)

        @pl.when(ci == 0)
        def _init():
            c_s[...] = jnp.zeros_like(c_s)
            n_s[...] = jnp.zeros_like(n_s)
            m_s[...] = jnp.zeros_like(m_s)
            s_s[...] = jnp.zeros_like(s_s)

    consts = (bias_ref[...], wa2_ref[...], ba_ref[...], gml_ref[...], ggl_ref[...])
    lane = lax.broadcasted_iota(jnp.int32, (1, LANES), 1)
    zqs = [zq_ref[b] for b in range(bb)]
    zgs = [zg_ref[b] for b in range(bb)]
    states = []
    for b in range(bb):
        if has_state:
            n_all = n0_ref[b]
            m_all = m0_ref[b]
            states.append(dict(c=[c0_ref[b, h] for h in range(ML_HEADS)],
                               n=[n_all[h:h + 1, :] for h in range(ML_HEADS)],
                               m=[m_all[:, h:h + 1] for h in range(ML_HEADS)],
                               s=[s0_ref[b, h] for h in range(GLA_HEADS)]))
        else:
            states.append(dict(c=[c_s[b, h] for h in range(ML_HEADS)],
                               n=[n_s[b, h:h + 1, :] for h in range(ML_HEADS)],
                               m=[m_s[b, h:h + 1, 0:1] for h in range(ML_HEADS)],
                               s=[s_s[b, h] for h in range(GLA_HEADS)]))
    ys, new_states = _mixer_compute(zqs, zgs, states, consts, ee_ref, L=L, t_real=t_real)

    for b in range(bb):
        st = new_states[b]
        y_ref[b] = ys[b].astype(y_ref.dtype)
        m_row = jnp.zeros((1, LANES), F32)
        for h in range(ML_HEADS):
            m_row = jnp.where(lane == h, st["m"][h], m_row)

        def emit_state(b=b, st=st, m_row=m_row):
            for h in range(ML_HEADS):
                co_ref[b, h] = st["c"][h]
                so_ref[b, h] = st["s"][h]
            no_ref[b] = jnp.concatenate(st["n"], axis=0)
            mo_ref[b] = m_row[:, 0:ML_HEADS]

        if has_state:
            emit_state()
        else:
            for h in range(ML_HEADS):
                c_s[b, h] = st["c"][h]
                n_s[b, h:h + 1, :] = st["n"][h]
                m_s[b, h:h + 1, :] = jnp.broadcast_to(st["m"][h], (1, LANES))
                s_s[b, h] = st["s"][h]
            pl.when(ci == n_chunks - 1)(emit_state)


def _mixer(zq, zg, w, L, t_real, state, y_dtype, bb):
    bsz, t = zq.shape[0], zq.shape[1]
    n_chunks = t // L
    has_state = state is not None
    assert n_chunks == 1 or not has_state
    seq = lambda width: pl.BlockSpec((bb, L, width), lambda b, j: (b, j, 0))
    per_b = lambda *tail: pl.BlockSpec((bb,) + tail, lambda b, j: (b,) + (0,) * len(tail))
    in_specs = [seq(ZQ_W), seq(ZG_W), _const_spec((1, SMALL_W)), _const_spec((SMALL_W, GLA_QK_W)),
                _const_spec((1, GLA_QK_W)), _const_spec((1, ML_HEADS * ML_DV)), _const_spec((1, GLA_V_W)),
                _const_spec((GLA_SUB, LANES, LANES))]
    args = [zq, zg, w["gate_bias"], w["w_a2"], w["b_a"], w["mlstm_out_g"], w["gla_out_g"], w["diag_sum"]]
    scratch = []
    if has_state:
        in_specs += [per_b(ML_HEADS, ML_DK, ML_DV), per_b(ML_HEADS, ML_DK), per_b(1, ML_HEADS),
                     per_b(GLA_HEADS, GLA_DK, GLA_DV)]
        args += list(state)
    else:
        scratch = [pltpu.VMEM((bb, ML_HEADS, ML_DK, ML_DV), F32), pltpu.VMEM((bb, SUBLANES, LANES), F32),
                   pltpu.VMEM((bb, SUBLANES, LANES), F32), pltpu.VMEM((bb, GLA_HEADS, GLA_DK, GLA_DV), F32)]
    out_specs = [seq(D_MODEL), per_b(ML_HEADS, ML_DK, ML_DV), per_b(ML_HEADS, ML_DK), per_b(1, ML_HEADS),
                 per_b(GLA_HEADS, GLA_DK, GLA_DV)]
    out_shape = [jax.ShapeDtypeStruct((bsz, t, D_MODEL), y_dtype),
                 jax.ShapeDtypeStruct((bsz, ML_HEADS, ML_DK, ML_DV), F32),
                 jax.ShapeDtypeStruct((bsz, ML_HEADS, ML_DK), F32),
                 jax.ShapeDtypeStruct((bsz, 1, ML_HEADS), F32),
                 jax.ShapeDtypeStruct((bsz, GLA_HEADS, GLA_DK, GLA_DV), F32)]
    kern = functools.partial(_mixer_kernel, L=L, t_real=t_real, has_state=has_state, n_chunks=n_chunks, bb=bb)
    return pl.pallas_call(
        kern,
        grid=(bsz // bb, n_chunks),
        in_specs=in_specs,
        out_specs=out_specs,
        out_shape=out_shape,
        scratch_shapes=scratch,
        compiler_params=_params(("arbitrary", "arbitrary")),
        name="mixer_state" if has_state else "mixer",
    )(*args)


def _post_mix_kernel(x1_ref, ym_ref, wout_ref, gx_ref, wq_ref, x2_ref, q_ref):
    x2 = x1_ref[...] + _dot(ym_ref[...].astype(BF16), wout_ref[...])
    x2_ref[...] = x2
    hq = _rms(x2, gx_ref[...]).astype(BF16)
    q_ref[...] = _dot(hq, wq_ref[...]).astype(q_ref.dtype)


def _post_mix(x1, ym, w, tm, q_dtype):
    n = x1.shape[0]
    row = lambda: pl.BlockSpec((tm, D_MODEL), lambda i: (i, 0))
    return pl.pallas_call(
        _post_mix_kernel,
        grid=(n // tm,),
        in_specs=[row(), row(), _const_spec((D_MODEL, D_MODEL)), _const_spec((1, D_MODEL)),
                  _const_spec((D_MODEL, D_MODEL))],
        out_specs=[row(), row()],
        out_shape=[jax.ShapeDtypeStruct((n, D_MODEL), F32), jax.ShapeDtypeStruct((n, D_MODEL), q_dtype)],
        compiler_params=_params(("arbitrary",)),
        name="post_mix",
    )(x1, ym, w["w_out"], w["xattn_g"], w["xattn_wq"])


def _softmax(s):
    e = jnp.exp(s - jnp.max(s, axis=-1, keepdims=True))
    return e / jnp.sum(e, axis=-1, keepdims=True)


def _xattn_kernel(q_ref, k_ref, v_ref, o_ref):
    heads = [slice(h * XA_DH, (h + 1) * XA_DH) for h in range(XA_HEADS)]
    scores = [_dot_nt(q_ref[0, :, hs].astype(BF16), k_ref[0, :, hs].astype(BF16)) * (XA_DH ** -0.5) for hs in heads]
    probs = [_softmax(s).astype(BF16) for s in scores]
    for hs, p in zip(heads, probs):
        o_ref[0, :, hs] = _dot(p, v_ref[0, :, hs].astype(BF16)).astype(o_ref.dtype)


def _xattn(q, k, v, tq, o_dtype):
    bsz, t = q.shape[0], q.shape[1]
    qo = pl.BlockSpec((1, tq, D_MODEL), lambda b, j: (b, j, 0))
    kv = pl.BlockSpec((1, N_MEM, D_MODEL), lambda b, j: (b, 0, 0))
    return pl.pallas_call(
        _xattn_kernel,
        grid=(bsz, t // tq),
        in_specs=[qo, kv, kv],
        out_specs=qo,
        out_shape=jax.ShapeDtypeStruct((bsz, t, D_MODEL), o_dtype),
        compiler_params=_params(("arbitrary", "arbitrary")),
        name="xattn",
    )(q, k, v)


XA_LANE_TILES = XA_DH // LANES
CACHE_ROW_GROUP = XA_LANE_TILES * XA_HEADS


def _cache_rows_view(x):
    bsz = x.shape[0]
    x = x.reshape(bsz, N_MEM, XA_HEADS, XA_LANE_TILES, LANES)
    return x.transpose(0, 1, 3, 2, 4).reshape(bsz, N_MEM * CACHE_ROW_GROUP, LANES)


def _cache_rows_load(ref, b):
    cols = [ref[b, pl.ds(lt * XA_HEADS + h, N_MEM, stride=CACHE_ROW_GROUP), :]
            for h in range(XA_HEADS) for lt in range(XA_LANE_TILES)]
    return jnp.concatenate(cols, axis=1).astype(BF16)


def _xattn_cache_kernel(q_ref, k_ref, v_ref, o_ref, *, bb, tq):
    lane_head = lax.broadcasted_iota(jnp.int32, (1, D_MODEL), 1) // XA_DH
    scores = []
    for b in range(bb):
        q = q_ref[b]
        q_bd = jnp.concatenate([jnp.where(lane_head == h, q, 0.0) for h in range(XA_HEADS)], axis=0)
        k_full = _cache_rows_load(k_ref, b)
        scores.append(_dot_nt(q_bd.astype(BF16), k_full) * (XA_DH ** -0.5))
    p_all = _softmax(jnp.concatenate(scores, axis=0)).astype(BF16)
    rows = XA_HEADS * tq
    for b in range(bb):
        v_full = _cache_rows_load(v_ref, b)
        o_full = _dot(p_all[b * rows:(b + 1) * rows], v_full)
        o = jnp.zeros((tq, D_MODEL), F32)
        for h in range(XA_HEADS):
            o = jnp.where(lane_head == h, o_full[h * tq:(h + 1) * tq], o)
        o_ref[b] = o.astype(o_ref.dtype)


def _xattn_cache(q, k, v, bb, o_dtype):
    bsz, tq = q.shape[0], q.shape[1]
    k, v = _cache_rows_view(k), _cache_rows_view(v)
    qo = pl.BlockSpec((bb, tq, D_MODEL), lambda b: (b, 0, 0))
    kv = pl.BlockSpec((bb, N_MEM * CACHE_ROW_GROUP, LANES), lambda b: (b, 0, 0))
    return pl.pallas_call(
        functools.partial(_xattn_cache_kernel, bb=bb, tq=tq),
        grid=(bsz // bb,),
        in_specs=[qo, kv, kv],
        out_specs=qo,
        out_shape=jax.ShapeDtypeStruct((bsz, tq, D_MODEL), o_dtype),
        compiler_params=_params(("arbitrary",)),
        name="xattn_cache",
    )(q, k, v)


def _ffn_out_kernel(x2_ref, o_ref, wo_ref, g2_ref, wg_ref, wu_ref, wd_ref, gf_ref, y_ref):
    x3 = x2_ref[...] + _dot(o_ref[...].astype(BF16), wo_ref[...])
    x4 = _swiglu_residual(x3, g2_ref, wg_ref, wu_ref, wd_ref)
    y_ref[...] = _rms(x4, gf_ref[...])


def _ffn_out(x2, o, w, tm):
    n = x2.shape[0]
    row = lambda: pl.BlockSpec((tm, D_MODEL), lambda i: (i, 0))
    return pl.pallas_call(
        _ffn_out_kernel,
        grid=(n // tm,),
        in_specs=[row(), row(), _const_spec((D_MODEL, D_MODEL)), _const_spec((1, D_MODEL)),
                  _const_spec((D_MODEL, D_FF)), _const_spec((D_MODEL, D_FF)), _const_spec((D_FF, D_MODEL)),
                  _const_spec((1, D_MODEL))],
        out_specs=row(),
        out_shape=jax.ShapeDtypeStruct((n, D_MODEL), F32),
        compiler_params=_params(("arbitrary",)),
        name="ffn_out",
    )(x2, o, w["xattn_wo"], w["ffn2_g"], w["ffn2_wg"], w["ffn2_wu"], w["ffn2_wd"], w["final_g"])


def _memkv_kernel(m_ref, g_ref, wk_ref, wv_ref, k_ref, v_ref):
    hn = _rms(m_ref[...], g_ref[...]).astype(BF16)
    k_ref[...] = _dot(hn, wk_ref[...])
    v_ref[...] = _dot(hn, wv_ref[...])


def _memkv(mem, w, tm):
    n = mem.shape[0]
    row = lambda: pl.BlockSpec((tm, D_MODEL), lambda i: (i, 0))
    return pl.pallas_call(
        _memkv_kernel,
        grid=(n // tm,),
        in_specs=[row(), _const_spec((1, D_MODEL)), _const_spec((D_MODEL, D_MODEL)),
                  _const_spec((D_MODEL, D_MODEL))],
        out_specs=[row(), row()],
        out_shape=[jax.ShapeDtypeStruct((n, D_MODEL), F32)] * 2,
        compiler_params=_params(("arbitrary",)),
        name="memkv",
    )(mem, w["mem_g"], w["xattn_wk"], w["xattn_wv"])


def _prep_weights(p):
    bf = lambda a: a.astype(BF16)
    row = lambda a: a.reshape(1, -1).astype(F32)
    w_in = p["w_in"]
    off = {}
    pos = 0
    for name, width in (("mq", 512), ("mk", 512), ("mv", 512), ("mi", 4), ("mf", 4), ("mo", 512), ("gq", 256),
                        ("gk", 256), ("gv", 512), ("ga", 16), ("gg", 512)):
        off[name] = w_in[:, pos:pos + width]
        pos += width
    pad_cols = SMALL_W - 2 * ML_HEADS - GLA_RANK
    w_in_q = jnp.concatenate([off[k] for k in ("mq", "mk", "mv", "gq", "gk", "gv")], axis=1)
    w_in_g = jnp.concatenate([off["mo"], off["gg"], off["mi"], off["mf"], off["ga"],
                              jnp.zeros((D_MODEL, pad_cols), F32)], axis=1)
    gate_bias = jnp.concatenate([p["mlstm_b_i"], p["mlstm_b_f"], jnp.zeros((SMALL_W - 2 * ML_HEADS,), F32)])
    w_a2 = jnp.zeros((SMALL_W, GLA_QK_W), F32).at[2 * ML_HEADS:2 * ML_HEADS + GLA_RANK].set(p["gla_w_a2"])
    lane = jnp.arange(LANES)
    same_head = (lane[:, None] // GLA_DK) == (lane[None, :] // GLA_DK)
    diag_sum = jnp.stack([same_head & ((lane[None, :] % GLA_SUB) == j) for j in range(GLA_SUB)]).astype(BF16)
    return dict(
        ffn1_g=row(p["ffn1_norm_g"]), ffn1_wg=bf(p["ffn1_w_gate"]), ffn1_wu=bf(p["ffn1_w_up"]),
        ffn1_wd=bf(p["ffn1_w_down"]), mix_g=row(p["mix_norm_g"]), w_in_q=bf(w_in_q), w_in_g=bf(w_in_g),
        gate_bias=row(gate_bias), w_a2=bf(w_a2), b_a=row(p["gla_b_a"]), mlstm_out_g=row(p["mlstm_out_g"]),
        gla_out_g=row(p["gla_out_g"]), diag_sum=diag_sum,
        w_out=bf(p["w_out"]), xattn_g=row(p["xattn_norm_g"]), xattn_wq=bf(p["xattn_w_q"]),
        xattn_wo=bf(p["xattn_w_o"]), mem_g=row(p["mem_norm_g"]), xattn_wk=bf(p["xattn_w_k"]),
        xattn_wv=bf(p["xattn_w_v"]), ffn2_g=row(p["ffn2_norm_g"]), ffn2_wg=bf(p["ffn2_w_gate"]),
        ffn2_wu=bf(p["ffn2_w_up"]), ffn2_wd=bf(p["ffn2_w_down"]), final_g=row(p["final_g"]))


def _token_layers(x, attend, state, w, *, tm, chunk, pad_to, mixer_bb, act_dtype):
    bsz, t, _ = x.shape
    n = bsz * t
    x1, zq, zg = _ffn_in(x.reshape(n, D_MODEL), w, tm, act_dtype)
    zq = zq.reshape(bsz, t, ZQ_W)
    zg = zg.reshape(bsz, t, ZG_W)
    if pad_to > t:
        zq = jnp.pad(zq, ((0, 0), (0, pad_to - t), (0, 0)))
        zg = jnp.pad(zg, ((0, 0), (0, pad_to - t), (0, 0)))
    ym, c_new, n_new, m_new, s_new = _mixer(zq, zg, w, chunk, t, state, act_dtype, mixer_bb)
    ym = ym[:, :t].reshape(n, D_MODEL)
    x2, q = _post_mix(x1, ym, w, tm, act_dtype)
    o = attend(q.reshape(bsz, t, D_MODEL))
    y = _ffn_out(x2, o.reshape(n, D_MODEL), w, tm)
    return y.reshape(bsz, t, D_MODEL), c_new, n_new, m_new.reshape(bsz, ML_HEADS), s_new


def kernel(x_prompt, x_sample, mem_prompt, cache_mem_k, cache_mem_v, state_mlstm_c, state_mlstm_n, state_mlstm_m, state_gla_s, ffn1_norm_g, ffn1_w_gate, ffn1_w_up, ffn1_w_down, mix_norm_g, w_in, mlstm_b_i, mlstm_b_f, mlstm_out_g, gla_w_a2, gla_b_a, gla_out_g, w_out, xattn_norm_g, mem_norm_g, xattn_w_q, xattn_w_k, xattn_w_v, xattn_w_o, ffn2_norm_g, ffn2_w_gate, ffn2_w_up, ffn2_w_down, final_norm_g):
    assert ffn1_norm_g.shape[0] == 1, "single-layer stack"
    layer = dict(ffn1_norm_g=ffn1_norm_g, ffn1_w_gate=ffn1_w_gate, ffn1_w_up=ffn1_w_up, ffn1_w_down=ffn1_w_down,
                 mix_norm_g=mix_norm_g, w_in=w_in, mlstm_b_i=mlstm_b_i, mlstm_b_f=mlstm_b_f,
                 mlstm_out_g=mlstm_out_g, gla_w_a2=gla_w_a2, gla_b_a=gla_b_a, gla_out_g=gla_out_g, w_out=w_out,
                 xattn_norm_g=xattn_norm_g, mem_norm_g=mem_norm_g, xattn_w_q=xattn_w_q, xattn_w_k=xattn_w_k,
                 xattn_w_v=xattn_w_v, xattn_w_o=xattn_w_o, ffn2_norm_g=ffn2_norm_g, ffn2_w_gate=ffn2_w_gate,
                 ffn2_w_up=ffn2_w_up, ffn2_w_down=ffn2_w_down)
    p = {name: arr[0] for name, arr in layer.items()}
    p["final_g"] = final_norm_g
    w = _prep_weights(p)

    bp, tp, _ = x_prompt.shape
    bs, ts, _ = x_sample.shape

    mem_k_p, mem_v_p = _memkv(mem_prompt.reshape(bp * N_MEM, D_MODEL), w, 512)
    mem_k_p = mem_k_p.reshape(bp, N_MEM, D_MODEL)
    mem_v_p = mem_v_p.reshape(bp, N_MEM, D_MODEL)
    y_p, c_p, n_p, m_p, s_p = _token_layers(
        x_prompt, lambda q: _xattn(q, mem_k_p, mem_v_p, 512, BF16), None, w, tm=256, chunk=CHUNK, pad_to=tp,
        mixer_bb=1, act_dtype=BF16)

    state = (state_mlstm_c[0], state_mlstm_n[0], state_mlstm_m[0].reshape(bs, 1, ML_HEADS), state_gla_s[0])
    y_s, c_s, n_s, m_s, s_s = _token_layers(
        x_sample, lambda q: _xattn_cache(q, cache_mem_k[0], cache_mem_v[0], 4, F32), state, w, tm=256,
        chunk=SUBLANES, pad_to=SUBLANES, mixer_bb=8, act_dtype=F32)

    kv_shape = (1, bp, N_MEM, XA_HEADS, XA_DH)
    return (y_p, y_s, mem_k_p.reshape(kv_shape), mem_v_p.reshape(kv_shape), c_p[None], n_p[None], m_p[None],
            s_p[None], c_s[None], n_s[None], m_s[None], s_s[None])
```

```python
import functools

import jax
import jax.numpy as jnp
from jax import lax
from jax.experimental import pallas as pl
from jax.experimental.pallas import tpu as pltpu

F32 = jnp.float32
BF16 = jnp.bfloat16

D_MODEL = 1024
D_FF = 2816
ML_HEADS = 4
ML_DK = 128
ML_DV = 128
GLA_HEADS = 4
GLA_DK = 64
GLA_DV = 128
GLA_RANK = 16
GLA_TAU = 16.0
N_MEM = 256
XA_HEADS = 4
XA_DH = D_MODEL // XA_HEADS
EPS = 1e-6
CHUNK = 64
LANES = 128
SUBLANES = 8

ZQ_MQ, ZQ_MK, ZQ_MV = 0, 512, 1024
ZQ_GQ, ZQ_GK, ZQ_GV = 1536, 1792, 2048
ZQ_W = 2560
ZG_MO, ZG_GG, ZG_SMALL = 0, 512, 1024
ZG_W = 1152
SMALL_W = LANES
GLA_QK_W = GLA_HEADS * GLA_DK
GLA_V_W = GLA_HEADS * GLA_DV
GLA_PAIRS = GLA_HEADS // 2
GLA_SUB = SUBLANES

MXU_WIDTH = 256
FF_CHUNKS = ((0, 6 * MXU_WIDTH), (6 * MXU_WIDTH, D_FF))
VMEM_LIMIT_BYTES = 56 * 1024 * 1024

TM_FFN_IN = MXU_WIDTH
TM_POST = 2 * MXU_WIDTH
SAMPLE_MIXER_BATCHES = 8
SAMPLE_XATTN_BATCHES = 4


def _rms(x, g):
    return x * lax.rsqrt(jnp.mean(x * x, axis=-1, keepdims=True) + EPS) * g


def _log_sigmoid(x):
    return jnp.minimum(x, 0.0) - jnp.log1p(jnp.exp(-jnp.abs(x)))


def _dot(a, b):
    return jnp.dot(a, b, preferred_element_type=F32)


def _dot_nt(a, b):
    return lax.dot_general(a, b, (((1,), (1,)), ((), ())), preferred_element_type=F32)


def _dot_f32(a, b):
    return jnp.dot(a, b, precision=lax.Precision.HIGHEST, preferred_element_type=F32)


def _swiglu_residual(x, g_ref, wg_ref, wu_ref, wd_ref):
    h = _rms(x, g_ref[...]).astype(BF16)
    acc = jnp.zeros_like(x)
    for lo, hi in FF_CHUNKS:
        sl = slice(lo, hi)
        g = _dot(h, wg_ref[:, sl])
        u = _dot(h, wu_ref[:, sl])
        a = (g * jax.nn.sigmoid(g)) * u
        acc = acc + _dot(a.astype(BF16), wd_ref[sl, :])
    return x + 0.5 * acc


def _const_spec(shape):
    nd = len(shape)
    return pl.BlockSpec(shape, lambda *_: (0,) * nd, pipeline_mode=pl.Buffered(1))


def _params(sem):
    return pltpu.CompilerParams(dimension_semantics=sem, vmem_limit_bytes=VMEM_LIMIT_BYTES)


def _ffn_in_kernel(x_ref, g1_ref, wg_ref, wu_ref, wd_ref, gm_ref, wq_ref, wgt_ref, x1_ref, zq_ref, zg_ref):
    x1 = _swiglu_residual(x_ref[...], g1_ref, wg_ref, wu_ref, wd_ref)
    x1_ref[...] = x1
    hm = _rms(x1, gm_ref[...]).astype(BF16)
    zq_ref[...] = _dot(hm, wq_ref[...]).astype(zq_ref.dtype)
    zg_ref[...] = _dot(hm, wgt_ref[...])


def _ffn_in(x, w, tm, zq_dtype):
    n = x.shape[0]
    row = lambda width: pl.BlockSpec((tm, width), lambda i: (i, 0))
    return pl.pallas_call(
        _ffn_in_kernel,
        grid=(n // tm,),
        in_specs=[row(D_MODEL), _const_spec((1, D_MODEL)), _const_spec((D_MODEL, D_FF)),
                  _const_spec((D_MODEL, D_FF)), _const_spec((D_FF, D_MODEL)), _const_spec((1, D_MODEL)),
                  _const_spec((D_MODEL, ZQ_W)), _const_spec((D_MODEL, ZG_W))],
        out_specs=[row(D_MODEL), row(ZQ_W), row(ZG_W)],
        out_shape=[jax.ShapeDtypeStruct((n, D_MODEL), F32), jax.ShapeDtypeStruct((n, ZQ_W), zq_dtype),
                   jax.ShapeDtypeStruct((n, ZG_W), F32)],
        compiler_params=_params(("arbitrary",)),
        name="ffn_in",
    )(x, w["ffn1_g"], w["ffn1_wg"], w["ffn1_wu"], w["ffn1_wd"], w["mix_g"], w["w_in_q"], w["w_in_g"])


def _mixer_compute(zqs, zgs, states, consts, ee_ref, *, L, t_real):
    bias, wa2, ba, gml, ggl = consts
    nchunk = len(zqs)
    groups = [(b, h) for b in range(nchunk) for h in range(ML_HEADS)]
    pairs = [(b, p) for b in range(nchunk) for p in range(GLA_PAIRS)]
    padded = t_real < L
    valid = lax.broadcasted_iota(jnp.int32, (L, 1), 0) < t_real
    rr = lax.broadcasted_iota(jnp.int32, (L, L), 0)
    cc = lax.broadcasted_iota(jnp.int32, (L, L), 1)
    tril = cc <= rr
    c = GLA_SUB
    nb = L // c
    lane = lax.broadcasted_iota(jnp.int32, (1, LANES), 1)
    lane_blk = (lane % GLA_DK) // c
    lane_head = lane // GLA_DK
    v_lane_head = lax.broadcasted_iota(jnp.int32, (1, 2 * GLA_DV), 1) // GLA_DV
    row_blk = (lax.broadcasted_iota(jnp.int32, (2 * L, 1), 0) % L) // c
    t_in = lax.broadcasted_iota(jnp.int32, (1, c, 1), 1)

    smalls, sms, lfs, b_cols, b_rows, sm_ts, las, bcs = [], [], [], [], [], [], [], []
    for b in range(nchunk):
        small = zgs[b][:, ZG_SMALL:ZG_SMALL + SMALL_W]
        sm = small + bias
        lf = _log_sigmoid(sm)
        if padded:
            sm = jnp.where(valid, sm, -jnp.inf)
            lf = jnp.where(valid, lf, 0.0)
        smalls.append(small)
        sms.append(sm)
        lfs.append(lf)
    def cumsum_rows(x):
        if L > SUBLANES:
            return _dot_f32(tril.astype(F32), x)
        acc = x[0:1]
        rows = [acc]
        for r in range(1, L):
            acc = acc + x[r:r + 1]
            rows.append(acc)
        return jnp.concatenate(rows, axis=0)

    triu = (rr <= cc).astype(F32)
    for b in range(nchunk):
        b_cols.append(cumsum_rows(lfs[b]))
        b_rows.append(_dot_f32(lfs[b].T[0:SUBLANES], triu) if L > SUBLANES else b_cols[b].T[0:SUBLANES])
        sm_ts.append(sms[b].T)
        la = _log_sigmoid(_dot(smalls[b].astype(BF16), wa2) + ba) * (1.0 / GLA_TAU)
        las.append(jnp.where(valid, la, 0.0) if padded else la)
    for b in range(nchunk):
        bcs.append(cumsum_rows(las[b]))

    ml = {}
    for g in groups:
        b, h = g
        zq = zqs[b]
        qf = zq[:, ZQ_MQ + h * ML_DK:ZQ_MQ + (h + 1) * ML_DK].astype(F32)
        kf = zq[:, ZQ_MK + h * ML_DK:ZQ_MK + (h + 1) * ML_DK].astype(F32) * (ML_DK ** -0.5)
        vf = zq[:, ZQ_MV + h * ML_DV:ZQ_MV + (h + 1) * ML_DV].astype(F32)
        if padded:
            kf = jnp.where(valid, kf, 0.0)
            vf = jnp.where(valid, vf, 0.0)
        qb, kb, vb = qf.astype(BF16), kf.astype(BF16), vf.astype(BF16)
        ml[g] = dict(qf=qf, kf=kf, vb=vb, qk=_dot_nt(qb, kb), qc=_dot(qb, states[b]["c"][h].astype(BF16)))

    for g in groups:
        b, h = g
        d = ml[g]
        i_col = sms[b][:, h:h + 1]
        b_col = b_cols[b][:, ML_HEADS + h:ML_HEADS + h + 1]
        b_row = b_rows[b][ML_HEADS + h:ML_HEADS + h + 1, :]
        i_row = sm_ts[b][h:h + 1, :]
        a_col = b_col + states[b]["m"][h]
        dm = jnp.where(tril, b_col - (b_row - i_row), -jnp.inf)
        mt = jnp.maximum(a_col, jnp.max(dm, axis=1, keepdims=True))
        w_inter = jnp.exp(a_col - mt)
        s = d["qk"] * jnp.exp(dm - mt)
        kw = d["kf"] * jnp.exp((b_col[L - 1:L] - mt[L - 1:L]) - (b_col - i_col))
        d.update(mt=mt, w_inter=w_inter, s=s, kw=kw, kw_t=kw.T.astype(BF16))

    gl = {}
    for b in range(nchunk):
        zq = zqs[b]
        gq = zq[:, ZQ_GQ:ZQ_GQ + GLA_QK_W].astype(F32) * (GLA_DK ** -0.5)
        gk = zq[:, ZQ_GK:ZQ_GK + GLA_QK_W].astype(F32)
        gv = zq[:, ZQ_GV:ZQ_GV + GLA_V_W].astype(F32)
        if padded:
            gk = jnp.where(valid, gk, 0.0)
            gv = jnp.where(valid, gv, 0.0)
        stack = lambda x: jnp.concatenate([x[:, :LANES], x[:, LANES:]], axis=0)
        q2, k2, b2 = stack(gq), stack(gk), stack(bcs[b])
        q3 = q2.reshape(2 * nb, c, LANES)
        k3 = k2.reshape(2 * nb, c, LANES)
        b3 = b2.reshape(2 * nb, c, LANES)
        acc = jnp.zeros((2 * L, LANES), F32)
        for j in range(min(c, t_real)):
            decay = jnp.exp(jnp.where(t_in >= j, b3 - b3[:, j:j + 1, :], -jnp.inf))
            pair_j = (q3 * k3[:, j:j + 1, :] * decay).reshape(2 * L, LANES)
            acc = acc + _dot(pair_j.astype(BF16), ee_ref[j])
        a_diag = jnp.where(lane_blk == row_blk, acc, 0.0)
        kt2 = (k3 * jnp.exp(b3[:, c - 1:c, :] - b3)).reshape(2 * L, LANES) if nb > 1 else None
        gl[b] = dict(gv=gv, q2=q2, k2=k2, b2=b2, a_diag=a_diag, kt2=kt2)

    for bp in pairs:
        b, p = bp
        d = gl[b]
        rows_p = slice(p * L, (p + 1) * L)
        q_p, k_p, b_p = d["q2"][rows_p], d["k2"][rows_p], d["b2"][rows_p]
        s_prev = states[b]["s"]
        zero_blk = jnp.zeros((GLA_DK, GLA_DV), BF16)
        s_bd = jnp.concatenate(
            [jnp.concatenate([s_prev[2 * p].astype(BF16), zero_blk], axis=1),
             jnp.concatenate([zero_blk, s_prev[2 * p + 1].astype(BF16)], axis=1)], axis=0)
        e = dict(o_inter=_dot((q_p * jnp.exp(b_p)).astype(BF16), s_bd),
                 kh_t=(k_p * jnp.exp(b_p[L - 1:L] - b_p)).T.astype(BF16),
                 decay_col=jnp.exp(b_p[L - SUBLANES:L].T[:, SUBLANES - 1:SUBLANES]))
        if nb > 1:
            kt_p = d["kt2"][rows_p].astype(BF16)
            k_bd = jnp.concatenate([jnp.where(lane_head == hh, kt_p, 0.0) for hh in range(2)], axis=0)
            slabs, offs = [], []
            off = 0
            for j in range(nb - 1):
                lo = (j + 1) * c
                slabs.append(q_p[lo:] * jnp.exp(b_p[lo:] - b_p[lo - 1:lo]))
                offs.append(off)
                off += L - lo
            e["r"] = _dot_nt(jnp.concatenate(slabs, axis=0).astype(BF16), k_bd)
            e["offs"] = offs
        gl[bp] = e

    for g in groups:
        d = ml[g]
        d["sv"] = _dot(d["s"].astype(BF16), d["vb"])
        d["c_upd"] = _dot(d["kw_t"], d["vb"])
        b, h = g
        d["den"] = (jnp.sum(d["s"], axis=1, keepdims=True)
                    + d["w_inter"] * jnp.sum(d["qf"] * states[b]["n"][h], axis=1, keepdims=True))
    for bp in pairs:
        b, p = bp
        d, e = gl[b], gl[bp]
        a_p = d["a_diag"][p * L:(p + 1) * L]
        if nb > 1:
            blocks = []
            for i in range(nb):
                blk = a_p[i * c:(i + 1) * c]
                for j in range(i):
                    lo_r = e["offs"][j] + (i - j - 1) * c
                    blk = jnp.where(lane_blk == j, e["r"][lo_r:lo_r + c], blk)
                blocks.append(blk)
            a_p = jnp.concatenate(blocks, axis=0)
        v_f = d["gv"][:, p * 2 * GLA_DV:(p + 1) * 2 * GLA_DV]
        v_p = v_f.astype(BF16)
        if L < GLA_DK:
            v_rows = []
            for hh in range(2):
                v_rows += [jnp.where(v_lane_head == hh, v_f, 0.0), jnp.zeros((GLA_DK - L, 2 * GLA_DV), F32)]
            v_bd = jnp.concatenate(v_rows, axis=0).astype(BF16)
        else:
            v_bd = jnp.concatenate([jnp.where(v_lane_head == hh, v_p, 0.0) for hh in range(2)], axis=0)
        e["o"] = e["o_inter"] + _dot(a_p.astype(BF16), v_bd)
        e["s_upd"] = [_dot(e["kh_t"][hh * GLA_DK:(hh + 1) * GLA_DK], v_p[:, hh * GLA_DV:(hh + 1) * GLA_DV])
                      for hh in range(2)]

    ys = [[None] * (ML_HEADS + GLA_HEADS) for _ in range(nchunk)]
    new_states = [dict(c=[None] * ML_HEADS, n=[None] * ML_HEADS, m=[None] * ML_HEADS, s=[None] * GLA_HEADS)
                  for _ in range(nchunk)]
    for g in groups:
        b, h = g
        d = ml[g]
        mt, w_inter = d["mt"], d["w_inter"]
        hh = (d["sv"] + w_inter * d["qc"]) / jnp.maximum(jnp.abs(d["den"]), jnp.exp(-mt))
        i_last = w_inter[L - 1:L]
        st = states[b]
        new_states[b]["c"][h] = i_last * st["c"][h] + d["c_upd"]
        new_states[b]["n"][h] = i_last * st["n"][h] + jnp.sum(d["kw"], axis=0, keepdims=True)
        new_states[b]["m"][h] = mt[L - 1:L]
        hs = slice(h * ML_DV, (h + 1) * ML_DV)
        yn = hh * lax.rsqrt(jnp.mean(hh * hh, axis=-1, keepdims=True) + EPS) * gml[:, hs]
        ys[b][h] = jax.nn.sigmoid(zgs[b][:, ZG_MO + h * ML_DV:ZG_MO + (h + 1) * ML_DV]) * yn
    for bp in pairs:
        b, p = bp
        e = gl[bp]
        for hh in range(2):
            h = 2 * p + hh
            ds = slice(hh * GLA_DK, (hh + 1) * GLA_DK)
            vs = slice(hh * GLA_DV, (hh + 1) * GLA_DV)
            new_states[b]["s"][h] = e["decay_col"][ds] * states[b]["s"][h] + e["s_upd"][hh]
            oh = e["o"][:, vs]
            yn = oh * lax.rsqrt(jnp.mean(oh * oh, axis=-1, keepdims=True) + EPS) * ggl[:, h * GLA_DV:(h + 1) * GLA_DV]
            gg = zgs[b][:, ZG_GG + h * GLA_DV:ZG_GG + (h + 1) * GLA_DV]
            ys[b][ML_HEADS + h] = (gg * jax.nn.sigmoid(gg)) * yn
    return [jnp.concatenate(y, axis=1) for y in ys], new_states


def _mixer_kernel(*refs, L, t_real, has_state, n_chunks, bb):
    it = iter(refs)
    zq_ref, zg_ref, bias_ref, wa2_ref, ba_ref, gml_ref, ggl_ref, ee_ref = (next(it) for _ in range(8))
    if has_state:
        c0_ref, n0_ref, m0_ref, s0_ref = (next(it) for _ in range(4))
    y_ref, co_ref, no_ref, mo_ref, so_ref = (next(it) for _ in range(5))
    if not has_state:
        c_s, n_s, m_s, s_s = (next(it) for _ in range(4))
        ci = pl.program_id(1)

        @pl.when(ci == 0)
        def _init():
            c_s[...] = jnp.zeros_like(c_s)
            n_s[...] = jnp.zeros_like(n_s)
            m_s[...] = jnp.zeros_like(m_s)
            s_s[...] = jnp.zeros_like(s_s)

    consts = (bias_ref[...], wa2_ref[...], ba_ref[...], gml_ref[...], ggl_ref[...])
    lane = lax.broadcasted_iota(jnp.int32, (1, LANES), 1)
    zqs = [zq_ref[b] for b in range(bb)]
    zgs = [zg_ref[b] for b in range(bb)]
    states = []
    for b in range(bb):
        if has_state:
            n_all = n0_ref[b]
            m_all = m0_ref[b]
            states.append(dict(c=[c0_ref[b, h] for h in range(ML_HEADS)],
                               n=[n_all[h:h + 1, :] for h in range(ML_HEADS)],
                               m=[m_all[:, h:h + 1] for h in range(ML_HEADS)],
                               s=[s0_ref[b, h] for h in range(GLA_HEADS)]))
        else:
            states.append(dict(c=[c_s[b, h] for h in range(ML_HEADS)],
                               n=[n_s[b, h:h + 1, :] for h in range(ML_HEADS)],
                               m=[m_s[b, h:h + 1, 0:1] for h in range(ML_HEADS)],
                               s=[s_s[b, h] for h in range(GLA_HEADS)]))
    ys, new_states = _mixer_compute(zqs, zgs, states, consts, ee_ref, L=L, t_real=t_real)

    for b in range(bb):
        st = new_states[b]
        y_ref[b] = ys[b].astype(y_ref.dtype)
        m_row = jnp.zeros((1, LANES), F32)
        for h in range(ML_HEADS):
            m_row = jnp.where(lane == h, st["m"][h], m_row)

        def emit_state(b=b, st=st, m_row=m_row):
            for h in range(ML_HEADS):
                co_ref[b, h] = st["c"][h]
                so_ref[b, h] = st["s"][h]
            no_ref[b] = jnp.concatenate(st["n"], axis=0)
            mo_ref[b] = m_row[:, 0:ML_HEADS]

        if has_state:
            emit_state()
        else:
            for h in range(ML_HEADS):
                c_s[b, h] = st["c"][h]
                n_s[b, h:h + 1, :] = st["n"][h]
                m_s[b, h:h + 1, :] = jnp.broadcast_to(st["m"][h], (1, LANES))
                s_s[b, h] = st["s"][h]
            pl.when(ci == n_chunks - 1)(emit_state)


def _mixer(zq, zg, w, L, t_real, state, y_dtype, bb):
    bsz, t = zq.shape[0], zq.shape[1]
    n_chunks = t // L
    has_state = state is not None
    assert n_chunks == 1 or not has_state
    seq = lambda width: pl.BlockSpec((bb, L, width), lambda b, j: (b, j, 0))
    per_b = lambda *tail: pl.BlockSpec((bb,) + tail, lambda b, j: (b,) + (0,) * len(tail))
    in_specs = [seq(ZQ_W), seq(ZG_W), _const_spec((1, SMALL_W)), _const_spec((SMALL_W, GLA_QK_W)),
                _const_spec((1, GLA_QK_W)), _const_spec((1, ML_HEADS * ML_DV)), _const_spec((1, GLA_V_W)),
                _const_spec((GLA_SUB, LANES, LANES))]
    args = [zq, zg, w["gate_bias"], w["w_a2"], w["b_a"], w["mlstm_out_g"], w["gla_out_g"], w["diag_sum"]]
    scratch = []
    if has_state:
        in_specs += [per_b(ML_HEADS, ML_DK, ML_DV), per_b(ML_HEADS, ML_DK), per_b(1, ML_HEADS),
                     per_b(GLA_HEADS, GLA_DK, GLA_DV)]
        args += list(state)
    else:
        scratch = [pltpu.VMEM((bb, ML_HEADS, ML_DK, ML_DV), F32), pltpu.VMEM((bb, SUBLANES, LANES), F32),
                   pltpu.VMEM((bb, SUBLANES, LANES), F32), pltpu.VMEM((bb, GLA_HEADS, GLA_DK, GLA_DV), F32)]
    out_specs = [seq(D_MODEL), per_b(ML_HEADS, ML_DK, ML_DV), per_b(ML_HEADS, ML_DK), per_b(1, ML_HEADS),
                 per_b(GLA_HEADS, GLA_DK, GLA_DV)]
    out_shape = [jax.ShapeDtypeStruct((bsz, t, D_MODEL), y_dtype),
                 jax.ShapeDtypeStruct((bsz, ML_HEADS, ML_DK, ML_DV), F32),
                 jax.ShapeDtypeStruct((bsz, ML_HEADS, ML_DK), F32),
                 jax.ShapeDtypeStruct((bsz, 1, ML_HEADS), F32),
                 jax.ShapeDtypeStruct((bsz, GLA_HEADS, GLA_DK, GLA_DV), F32)]
    kern = functools.partial(_mixer_kernel, L=L, t_real=t_real, has_state=has_state, n_chunks=n_chunks, bb=bb)
    return pl.pallas_call(
        kern,
        grid=(bsz // bb, n_chunks),
        in_specs=in_specs,
        out_specs=out_specs,
        out_shape=out_shape,
        scratch_shapes=scratch,
        compiler_params=_params(("arbitrary", "arbitrary")),
        name="mixer_state" if has_state else "mixer",
    )(*args)


def _post_mix_kernel(x1_ref, ym_ref, wout_ref, gx_ref, wq_ref, x2_ref, q_ref):
    x2 = x1_ref[...] + _dot(ym_ref[...].astype(BF16), wout_ref[...])
    x2_ref[...] = x2
    hq = _rms(x2, gx_ref[...]).astype(BF16)
    q_ref[...] = _dot(hq, wq_ref[...]).astype(q_ref.dtype)


def _post_mix(x1, ym, w, tm, q_dtype):
    n = x1.shape[0]
    row = lambda: pl.BlockSpec((tm, D_MODEL), lambda i: (i, 0))
    return pl.pallas_call(
        _post_mix_kernel,
        grid=(n // tm,),
        in_specs=[row(), row(), _const_spec((D_MODEL, D_MODEL)), _const_spec((1, D_MODEL)),
                  _const_spec((D_MODEL, D_MODEL))],
        out_specs=[row(), row()],
        out_shape=[jax.ShapeDtypeStruct((n, D_MODEL), F32), jax.ShapeDtypeStruct((n, D_MODEL), q_dtype)],
        compiler_params=_params(("arbitrary",)),
        name="post_mix",
    )(x1, ym, w["w_out"], w["xattn_g"], w["xattn_wq"])


def _softmax(s):
    e = jnp.exp(s - jnp.max(s, axis=-1, keepdims=True))
    return e / jnp.sum(e, axis=-1, keepdims=True)


def _post_fused_kernel(x1_ref, ym_ref, k_ref, v_ref, wout_ref, gx_ref, wq_ref, wo_ref, g2_ref, wg_ref, wu_ref,
                       wd_ref, gf_ref, y_ref):
    x2 = x1_ref[0] + _dot(ym_ref[0].astype(BF16), wout_ref[...])
    q = _dot(_rms(x2, gx_ref[...]).astype(BF16), wq_ref[...]).astype(BF16)
    heads = [slice(h * XA_DH, (h + 1) * XA_DH) for h in range(XA_HEADS)]
    scores = [_dot_nt(q[:, hs], k_ref[0, :, hs].astype(BF16)) * (XA_DH ** -0.5) for hs in heads]
    probs = [_softmax(s).astype(BF16) for s in scores]
    o = jnp.concatenate([_dot(p, v_ref[0, :, hs].astype(BF16)).astype(BF16) for hs, p in zip(heads, probs)], axis=1)
    x3 = x2 + _dot(o, wo_ref[...])
    x4 = _swiglu_residual(x3, g2_ref, wg_ref, wu_ref, wd_ref)
    y_ref[0] = _rms(x4, gf_ref[...])


def _post_fused(x1, ym, k, v, w, tm):
    bsz, t = x1.shape[0], x1.shape[1]
    row = lambda: pl.BlockSpec((1, tm, D_MODEL), lambda b, j: (b, j, 0))
    kv = pl.BlockSpec((1, N_MEM, D_MODEL), lambda b, j: (b, 0, 0))
    sq = _const_spec((D_MODEL, D_MODEL))
    vec = _const_spec((1, D_MODEL))
    return pl.pallas_call(
        _post_fused_kernel,
        grid=(bsz, t // tm),
        in_specs=[row(), row(), kv, kv, sq, vec, sq, sq, vec, _const_spec((D_MODEL, D_FF)),
                  _const_spec((D_MODEL, D_FF)), _const_spec((D_FF, D_MODEL)), vec],
        out_specs=row(),
        out_shape=jax.ShapeDtypeStruct((bsz, t, D_MODEL), F32),
        compiler_params=_params(("arbitrary", "arbitrary")),
        name="post_fused",
    )(x1, ym, k, v, w["w_out"], w["xattn_g"], w["xattn_wq"], w["xattn_wo"], w["ffn2_g"], w["ffn2_wg"],
      w["ffn2_wu"], w["ffn2_wd"], w["final_g"])


XA_LANE_TILES = XA_DH // LANES
CACHE_ROW_GROUP = XA_LANE_TILES * XA_HEADS


def _cache_rows_view(x):
    bsz = x.shape[0]
    x = x.reshape(bsz, N_MEM, XA_HEADS, XA_LANE_TILES, LANES)
    return x.transpose(0, 1, 3, 2, 4).reshape(bsz, N_MEM * CACHE_ROW_GROUP, LANES)


def _cache_rows_load(ref, b):
    cols = [ref[b, pl.ds(lt * XA_HEADS + h, N_MEM, stride=CACHE_ROW_GROUP), :]
            for h in range(XA_HEADS) for lt in range(XA_LANE_TILES)]
    return jnp.concatenate(cols, axis=1).astype(BF16)


def _xattn_cache_kernel(q_ref, k_ref, v_ref, o_ref, *, bb, tq):
    lane_head = lax.broadcasted_iota(jnp.int32, (1, D_MODEL), 1) // XA_DH
    scores = []
    for b in range(bb):
        q = q_ref[b]
        q_bd = jnp.concatenate([jnp.where(lane_head == h, q, 0.0) for h in range(XA_HEADS)], axis=0)
        k_full = _cache_rows_load(k_ref, b)
        scores.append(_dot_nt(q_bd.astype(BF16), k_full) * (XA_DH ** -0.5))
    p_all = _softmax(jnp.concatenate(scores, axis=0)).astype(BF16)
    rows = XA_HEADS * tq
    for b in range(bb):
        v_full = _cache_rows_load(v_ref, b)
        o_full = _dot(p_all[b * rows:(b + 1) * rows], v_full)
        o = jnp.zeros((tq, D_MODEL), F32)
        for h in range(XA_HEADS):
            o = jnp.where(lane_head == h, o_full[h * tq:(h + 1) * tq], o)
        o_ref[b] = o.astype(o_ref.dtype)


def _xattn_cache(q, k, v, bb, o_dtype):
    bsz, tq = q.shape[0], q.shape[1]
    k, v = _cache_rows_view(k), _cache_rows_view(v)
    qo = pl.BlockSpec((bb, tq, D_MODEL), lambda b: (b, 0, 0))
    kv = pl.BlockSpec((bb, N_MEM * CACHE_ROW_GROUP, LANES), lambda b: (b, 0, 0))
    return pl.pallas_call(
        functools.partial(_xattn_cache_kernel, bb=bb, tq=tq),
        grid=(bsz // bb,),
        in_specs=[qo, kv, kv],
        out_specs=qo,
        out_shape=jax.ShapeDtypeStruct((bsz, tq, D_MODEL), o_dtype),
        compiler_params=_params(("arbitrary",)),
        name="xattn_cache",
    )(q, k, v)


def _ffn_out_kernel(x2_ref, o_ref, wo_ref, g2_ref, wg_ref, wu_ref, wd_ref, gf_ref, y_ref):
    x3 = x2_ref[...] + _dot(o_ref[...].astype(BF16), wo_ref[...])
    x4 = _swiglu_residual(x3, g2_ref, wg_ref, wu_ref, wd_ref)
    y_ref[...] = _rms(x4, gf_ref[...])


def _ffn_out(x2, o, w, tm):
    n = x2.shape[0]
    row = lambda: pl.BlockSpec((tm, D_MODEL), lambda i: (i, 0))
    return pl.pallas_call(
        _ffn_out_kernel,
        grid=(n // tm,),
        in_specs=[row(), row(), _const_spec((D_MODEL, D_MODEL)), _const_spec((1, D_MODEL)),
                  _const_spec((D_MODEL, D_FF)), _const_spec((D_MODEL, D_FF)), _const_spec((D_FF, D_MODEL)),
                  _const_spec((1, D_MODEL))],
        out_specs=row(),
        out_shape=jax.ShapeDtypeStruct((n, D_MODEL), F32),
        compiler_params=_params(("arbitrary",)),
        name="ffn_out",
    )(x2, o, w["xattn_wo"], w["ffn2_g"], w["ffn2_wg"], w["ffn2_wu"], w["ffn2_wd"], w["final_g"])


def _memkv_kernel(m_ref, g_ref, wk_ref, wv_ref, k_ref, v_ref):
    hn = _rms(m_ref[...], g_ref[...]).astype(BF16)
    k_ref[...] = _dot(hn, wk_ref[...])
    v_ref[...] = _dot(hn, wv_ref[...])


def _memkv(mem, w, tm):
    n = mem.shape[0]
    row = lambda: pl.BlockSpec((tm, D_MODEL), lambda i: (i, 0))
    return pl.pallas_call(
        _memkv_kernel,
        grid=(n // tm,),
        in_specs=[row(), _const_spec((1, D_MODEL)), _const_spec((D_MODEL, D_MODEL)),
                  _const_spec((D_MODEL, D_MODEL))],
        out_specs=[row(), row()],
        out_shape=[jax.ShapeDtypeStruct((n, D_MODEL), F32)] * 2,
        compiler_params=_params(("arbitrary",)),
        name="memkv",
    )(mem, w["mem_g"], w["xattn_wk"], w["xattn_wv"])


def _prep_weights(p):
    bf = lambda a: a.astype(BF16)
    row = lambda a: a.reshape(1, -1).astype(F32)
    w_in = p["w_in"]
    off = {}
    pos = 0
    for name, width in (("mq", 512), ("mk", 512), ("mv", 512), ("mi", 4), ("mf", 4), ("mo", 512), ("gq", 256),
                        ("gk", 256), ("gv", 512), ("ga", 16), ("gg", 512)):
        off[name] = w_in[:, pos:pos + width]
        pos += width
    pad_cols = SMALL_W - 2 * ML_HEADS - GLA_RANK
    w_in_q = jnp.concatenate([off[k] for k in ("mq", "mk", "mv", "gq", "gk", "gv")], axis=1)
    w_in_g = jnp.concatenate([off["mo"], off["gg"], off["mi"], off["mf"], off["ga"],
                              jnp.zeros((D_MODEL, pad_cols), F32)], axis=1)
    gate_bias = jnp.concatenate([p["mlstm_b_i"], p["mlstm_b_f"], jnp.zeros((SMALL_W - 2 * ML_HEADS,), F32)])
    w_a2 = jnp.zeros((SMALL_W, GLA_QK_W), F32).at[2 * ML_HEADS:2 * ML_HEADS + GLA_RANK].set(p["gla_w_a2"])
    lane = jnp.arange(LANES)
    same_head = (lane[:, None] // GLA_DK) == (lane[None, :] // GLA_DK)
    diag_sum = jnp.stack([same_head & ((lane[None, :] % GLA_SUB) == j) for j in range(GLA_SUB)]).astype(BF16)
    return dict(
        ffn1_g=row(p["ffn1_norm_g"]), ffn1_wg=bf(p["ffn1_w_gate"]), ffn1_wu=bf(p["ffn1_w_up"]),
        ffn1_wd=bf(p["ffn1_w_down"]), mix_g=row(p["mix_norm_g"]), w_in_q=bf(w_in_q), w_in_g=bf(w_in_g),
        gate_bias=row(gate_bias), w_a2=bf(w_a2), b_a=row(p["gla_b_a"]), mlstm_out_g=row(p["mlstm_out_g"]),
        gla_out_g=row(p["gla_out_g"]), diag_sum=diag_sum,
        w_out=bf(p["w_out"]), xattn_g=row(p["xattn_norm_g"]), xattn_wq=bf(p["xattn_w_q"]),
        xattn_wo=bf(p["xattn_w_o"]), mem_g=row(p["mem_norm_g"]), xattn_wk=bf(p["xattn_w_k"]),
        xattn_wv=bf(p["xattn_w_v"]), ffn2_g=row(p["ffn2_norm_g"]), ffn2_wg=bf(p["ffn2_w_gate"]),
        ffn2_wu=bf(p["ffn2_w_up"]), ffn2_wd=bf(p["ffn2_w_down"]), final_g=row(p["final_g"]))


def _ffn_and_mixers(x, state, w, *, chunk, mixer_bb, act_dtype):
    bsz, t, _ = x.shape
    x1, zq, zg = _ffn_in(x.reshape(bsz * t, D_MODEL), w, TM_FFN_IN, act_dtype)
    zq = zq.reshape(bsz, t, ZQ_W)
    zg = zg.reshape(bsz, t, ZG_W)
    pad = -t % chunk
    if pad:
        zq = jnp.pad(zq, ((0, 0), (0, pad), (0, 0)))
        zg = jnp.pad(zg, ((0, 0), (0, pad), (0, 0)))
    ym, c_new, n_new, m_new, s_new = _mixer(zq, zg, w, chunk, t, state, act_dtype, mixer_bb)
    return x1, ym[:, :t], (c_new[None], n_new[None], m_new.reshape(1, bsz, ML_HEADS), s_new[None])


def kernel(x_prompt, x_sample, mem_prompt, cache_mem_k, cache_mem_v, state_mlstm_c, state_mlstm_n, state_mlstm_m, state_gla_s, ffn1_norm_g, ffn1_w_gate, ffn1_w_up, ffn1_w_down, mix_norm_g, w_in, mlstm_b_i, mlstm_b_f, mlstm_out_g, gla_w_a2, gla_b_a, gla_out_g, w_out, xattn_norm_g, mem_norm_g, xattn_w_q, xattn_w_k, xattn_w_v, xattn_w_o, ffn2_norm_g, ffn2_w_gate, ffn2_w_up, ffn2_w_down, final_norm_g):
    assert ffn1_norm_g.shape[0] == 1, "single-layer stack"
    layer = dict(ffn1_norm_g=ffn1_norm_g, ffn1_w_gate=ffn1_w_gate, ffn1_w_up=ffn1_w_up, ffn1_w_down=ffn1_w_down,
                 mix_norm_g=mix_norm_g, w_in=w_in, mlstm_b_i=mlstm_b_i, mlstm_b_f=mlstm_b_f,
                 mlstm_out_g=mlstm_out_g, gla_w_a2=gla_w_a2, gla_b_a=gla_b_a, gla_out_g=gla_out_g, w_out=w_out,
                 xattn_norm_g=xattn_norm_g, mem_norm_g=mem_norm_g, xattn_w_q=xattn_w_q, xattn_w_k=xattn_w_k,
                 xattn_w_v=xattn_w_v, xattn_w_o=xattn_w_o, ffn2_norm_g=ffn2_norm_g, ffn2_w_gate=ffn2_w_gate,
                 ffn2_w_up=ffn2_w_up, ffn2_w_down=ffn2_w_down)
    p = {name: arr[0] for name, arr in layer.items()}
    p["final_g"] = final_norm_g
    w = _prep_weights(p)

    bp, tp, _ = x_prompt.shape
    bs, ts, _ = x_sample.shape

    mem_k_p, mem_v_p = _memkv(mem_prompt.reshape(bp * N_MEM, D_MODEL), w, TM_POST)
    mem_k_p = mem_k_p.reshape(bp, N_MEM, D_MODEL)
    mem_v_p = mem_v_p.reshape(bp, N_MEM, D_MODEL)
    x1_p, ym_p, states_p = _ffn_and_mixers(x_prompt, None, w, chunk=CHUNK, mixer_bb=1, act_dtype=BF16)
    y_p = _post_fused(x1_p.reshape(bp, tp, D_MODEL), ym_p, mem_k_p, mem_v_p, w, TM_POST)

    state = (state_mlstm_c[0], state_mlstm_n[0], state_mlstm_m[0].reshape(bs, 1, ML_HEADS), state_gla_s[0])
    x1_s, ym_s, states_s = _ffn_and_mixers(x_sample, state, w, chunk=SUBLANES, mixer_bb=SAMPLE_MIXER_BATCHES,
                                           act_dtype=F32)
    x2_s, q_s = _post_mix(x1_s, ym_s.reshape(bs * ts, D_MODEL), w, TM_FFN_IN, F32)
    o_s = _xattn_cache(q_s.reshape(bs, ts, D_MODEL), cache_mem_k[0], cache_mem_v[0], SAMPLE_XATTN_BATCHES, F32)
    y_s = _ffn_out(x2_s, o_s.reshape(bs * ts, D_MODEL), w, TM_FFN_IN).reshape(bs, ts, D_MODEL)

    kv_shape = (1, bp, N_MEM, XA_HEADS, XA_DH)
    return (y_p, y_s, mem_k_p.reshape(kv_shape), mem_v_p.reshape(kv_shape)) + states_p + states_s
```

```python
import functools

import jax
import jax.numpy as jnp
from jax import lax
from jax.experimental import pallas as pl
from jax.experimental.pallas import tpu as pltpu

F32 = jnp.float32
BF16 = jnp.bfloat16

D_MODEL = 1024
D_FF = 2816
ML_HEADS = 4
ML_DK = 128
ML_DV = 128
GLA_HEADS = 4
GLA_DK = 64
GLA_DV = 128
GLA_RANK = 16
GLA_TAU = 16.0
N_MEM = 256
XA_HEADS = 4
XA_DH = D_MODEL // XA_HEADS
EPS = 1e-6
CHUNK = 64
LOG2_E = 1.4426950408889634
LN_2 = 0.6931471805599453
LANES = 128
SUBLANES = 8

ZQ_MQ, ZQ_MK, ZQ_MV = 0, 512, 1024
ZQ_GQ, ZQ_GK, ZQ_GV = 1536, 1792, 2048
ZQ_W = 2560
ZG_MO, ZG_GG, ZG_SMALL = 0, 512, 1024
ZG_W = 1152
SMALL_W = LANES
GLA_QK_W = GLA_HEADS * GLA_DK
GLA_V_W = GLA_HEADS * GLA_DV
GLA_PAIRS = GLA_HEADS // 2
GLA_SUB = SUBLANES

MXU_WIDTH = 256
FF_CHUNKS = ((0, 6 * MXU_WIDTH), (6 * MXU_WIDTH, D_FF))
VMEM_LIMIT_BYTES = 56 * 1024 * 1024

TM_FFN_IN = MXU_WIDTH
TM_POST = 2 * MXU_WIDTH
PROMPT_MIXER_CHUNKS = 1
SAMPLE_MIXER_BATCHES = 8
SAMPLE_XATTN_BATCHES = 4


def _rms(x, g):
    return x * lax.rsqrt(jnp.mean(x * x, axis=-1, keepdims=True) + EPS) * g


def _log_sigmoid(x):
    return jnp.minimum(x, 0.0) - jnp.log1p(jnp.exp(-jnp.abs(x)))


def _dot(a, b):
    return jnp.dot(a, b, preferred_element_type=F32)


def _dot_nt(a, b):
    return lax.dot_general(a, b, (((1,), (1,)), ((), ())), preferred_element_type=F32)


def _dot_f32(a, b):
    return jnp.dot(a, b, precision=lax.Precision.HIGHEST, preferred_element_type=F32)


def _swiglu_residual(x, g_ref, wg_ref, wu_ref, wd_ref):
    h = _rms(x, g_ref[...]).astype(BF16)
    acts = []
    for lo, hi in FF_CHUNKS:
        g = _dot(h, wg_ref[:, lo:hi])
        u = _dot(h, wu_ref[:, lo:hi])
        acts.append(((g * jax.nn.sigmoid(g)) * u).astype(BF16))
    acc = jnp.zeros_like(x)
    for (lo, hi), a in zip(FF_CHUNKS, acts):
        acc = acc + _dot(a, wd_ref[lo:hi, :])
    return x + 0.5 * acc


def _const_spec(shape):
    nd = len(shape)
    return pl.BlockSpec(shape, lambda *_: (0,) * nd, pipeline_mode=pl.Buffered(1))


def _params(sem):
    return pltpu.CompilerParams(dimension_semantics=sem, vmem_limit_bytes=VMEM_LIMIT_BYTES)


def _ffn_in_kernel(x_ref, g1_ref, wg_ref, wu_ref, wd_ref, gm_ref, wq_ref, wgt_ref, x1_ref, zq_ref, zg_ref):
    x1 = _swiglu_residual(x_ref[...], g1_ref, wg_ref, wu_ref, wd_ref)
    x1_ref[...] = x1
    hm = _rms(x1, gm_ref[...]).astype(BF16)
    zq_ref[...] = _dot(hm, wq_ref[...]).astype(zq_ref.dtype)
    zg_ref[...] = _dot(hm, wgt_ref[...])


def _ffn_in(x, w, tm, zq_dtype):
    n = x.shape[0]
    row = lambda width: pl.BlockSpec((tm, width), lambda i: (i, 0))
    return pl.pallas_call(
        _ffn_in_kernel,
        grid=(n // tm,),
        in_specs=[row(D_MODEL), _const_spec((1, D_MODEL)), _const_spec((D_MODEL, D_FF)),
                  _const_spec((D_MODEL, D_FF)), _const_spec((D_FF, D_MODEL)), _const_spec((1, D_MODEL)),
                  _const_spec((D_MODEL, ZQ_W)), _const_spec((D_MODEL, ZG_W))],
        out_specs=[row(D_MODEL), row(ZQ_W), row(ZG_W)],
        out_shape=[jax.ShapeDtypeStruct((n, D_MODEL), F32), jax.ShapeDtypeStruct((n, ZQ_W), zq_dtype),
                   jax.ShapeDtypeStruct((n, ZG_W), F32)],
        compiler_params=_params(("arbitrary",)),
        name="ffn_in",
    )(x, w["ffn1_g"], w["ffn1_wg"], w["ffn1_wu"], w["ffn1_wd"], w["mix_g"], w["w_in_q"], w["w_in_g"])


def _mixer_stages(zqs, zgs, states, consts, ee_ref, *, L, t_real, chain):
    bias, wa2, ba, gml, ggl = consts
    nchunk = len(zqs)
    groups = [(b, h) for b in range(nchunk) for h in range(ML_HEADS)]
    pairs = [(b, p) for b in range(nchunk) for p in range(GLA_PAIRS)]
    padded = t_real < L
    valid = lax.broadcasted_iota(jnp.int32, (L, 1), 0) < t_real
    rr = lax.broadcasted_iota(jnp.int32, (L, L), 0)
    cc = lax.broadcasted_iota(jnp.int32, (L, L), 1)
    tril = cc <= rr
    c = GLA_SUB
    nb = L // c
    lane = lax.broadcasted_iota(jnp.int32, (1, LANES), 1)
    lane_blk = (lane % GLA_DK) // c
    lane_head = lane // GLA_DK
    v_lane_head = lax.broadcasted_iota(jnp.int32, (1, 2 * GLA_DV), 1) // GLA_DV
    row_blk = (lax.broadcasted_iota(jnp.int32, (2 * L, 1), 0) % L) // c
    t_in = lax.broadcasted_iota(jnp.int32, (1, c, 1), 1)
    new_states = [dict(c=[None] * ML_HEADS, n=[None] * ML_HEADS, m=[None] * ML_HEADS, s=[None] * GLA_HEADS)
                  for _ in range(nchunk)]
    prev = lambda b: new_states[b - 1] if chain and b > 0 else states[0 if chain else b]
    known = lambda b: not chain or b == 0

    carried = {}

    def carried_mlstm(b, h):
        st = prev(b)
        d = ml[(b, h)]
        carried[(b, h)] = _dot(d["qb"], st["c"][h].astype(BF16))

    def carried_gla(b, p):
        s_prev = prev(b)["s"]
        zero_blk = jnp.zeros((GLA_DK, GLA_DV), BF16)
        s_bd = jnp.concatenate(
            [jnp.concatenate([s_prev[2 * p].astype(BF16), zero_blk], axis=1),
             jnp.concatenate([zero_blk, s_prev[2 * p + 1].astype(BF16)], axis=1)], axis=0)
        carried[(b, "gla", p)] = _dot(gl[(b, p)]["q_dec"], s_bd)

    smalls, sms, lfs, b_cols, b_rows, sm_ts, las, bcs = [], [], [], [], [], [], [], []
    for b in range(nchunk):
        small = zgs[b][:, ZG_SMALL:ZG_SMALL + SMALL_W]
        sm = small + bias
        lf = _log_sigmoid(sm) * LOG2_E
        sm = sm * LOG2_E
        if padded:
            sm = jnp.where(valid, sm, -jnp.inf)
            lf = jnp.where(valid, lf, 0.0)
        smalls.append(small)
        sms.append(sm)
        lfs.append(lf)

    def cumsum_rows(x):
        if L > SUBLANES:
            return _dot_f32(tril.astype(F32), x)
        acc = x[0:1]
        rows = [acc]
        for r in range(1, L):
            acc = acc + x[r:r + 1]
            rows.append(acc)
        return jnp.concatenate(rows, axis=0)

    triu = (rr <= cc).astype(F32)
    yield
    for b in range(nchunk):
        b_cols.append(cumsum_rows(lfs[b]))
        b_rows.append(_dot_f32(lfs[b].T[0:SUBLANES], triu) if L > SUBLANES else b_cols[b].T[0:SUBLANES])
        sm_ts.append(sms[b].T)
        la = _log_sigmoid(_dot(smalls[b].astype(BF16), wa2) + ba) * (LOG2_E / GLA_TAU)
        las.append(jnp.where(valid, la, 0.0) if padded else la)

    ml = {}
    for g in groups:
        b, h = g
        zq = zqs[b]
        qf = zq[:, ZQ_MQ + h * ML_DK:ZQ_MQ + (h + 1) * ML_DK].astype(F32)
        kf = zq[:, ZQ_MK + h * ML_DK:ZQ_MK + (h + 1) * ML_DK].astype(F32) * (ML_DK ** -0.5)
        vf = zq[:, ZQ_MV + h * ML_DV:ZQ_MV + (h + 1) * ML_DV].astype(F32)
        if padded:
            kf = jnp.where(valid, kf, 0.0)
            vf = jnp.where(valid, vf, 0.0)
        ml[g] = dict(qf=qf, qb=qf.astype(BF16), kf=kf, kb=kf.astype(BF16), vb=vf.astype(BF16))
    yield
    for b in range(nchunk):
        bcs.append(cumsum_rows(las[b]))
    for g in groups:
        d = ml[g]
        d["qk"] = _dot_nt(d["qb"], d["kb"])
        if known(g[0]):
            carried_mlstm(*g)

    yield
    for g in groups:
        b, h = g
        d = ml[g]
        i_col = sms[b][:, h:h + 1]
        b_col = b_cols[b][:, ML_HEADS + h:ML_HEADS + h + 1]
        b_row = b_rows[b][ML_HEADS + h:ML_HEADS + h + 1, :]
        i_row = sm_ts[b][h:h + 1, :]
        a_col = b_col + prev(b)["m"][h] * LOG2_E
        dm = jnp.where(tril, b_col - (b_row - i_row), -jnp.inf)
        mt = jnp.maximum(a_col, jnp.max(dm, axis=1, keepdims=True))
        w_inter = jnp.exp2(a_col - mt)
        s = d["qk"] * jnp.exp2(dm - mt)
        kw = d["kf"] * jnp.exp2((b_col[L - 1:L] - mt[L - 1:L]) - (b_col - i_col))
        d.update(mt=mt, w_inter=w_inter, s=s, kw=kw, kw_t=kw.T.astype(BF16))
        new_states[b]["m"][h] = mt[L - 1:L] * LN_2

    gl = {}
    for b in range(nchunk):
        zq = zqs[b]
        gq = zq[:, ZQ_GQ:ZQ_GQ + GLA_QK_W].astype(F32) * (GLA_DK ** -0.5)
        gk = zq[:, ZQ_GK:ZQ_GK + GLA_QK_W].astype(F32)
        gv = zq[:, ZQ_GV:ZQ_GV + GLA_V_W].astype(F32)
        if padded:
            gk = jnp.where(valid, gk, 0.0)
            gv = jnp.where(valid, gv, 0.0)
        stack = lambda x: jnp.concatenate([x[:, :LANES], x[:, LANES:]], axis=0)
        q2, k2, b2 = stack(gq), stack(gk), stack(bcs[b])
        q3 = q2.reshape(2 * nb, c, LANES)
        k3 = k2.reshape(2 * nb, c, LANES)
        b3 = b2.reshape(2 * nb, c, LANES)
        pair_terms = []
        for j in range(min(c, t_real)):
            decay = jnp.exp2(jnp.where(t_in >= j, b3 - b3[:, j:j + 1, :], -jnp.inf))
            pair_terms.append((q3 * k3[:, j:j + 1, :] * decay).reshape(2 * L, LANES).astype(BF16))
        kt2 = (k3 * jnp.exp2(b3[:, c - 1:c, :] - b3)).reshape(2 * L, LANES) if nb > 1 else None
        gl[b] = dict(gv=gv, q2=q2, k2=k2, b2=b2, pair_terms=pair_terms, kt2=kt2)
    yield
    for b in range(nchunk):
        acc = jnp.zeros((2 * L, LANES), F32)
        for j, pair_j in enumerate(gl[b]["pair_terms"]):
            acc = acc + _dot(pair_j, ee_ref[j])
        gl[b]["a_diag"] = jnp.where(lane_blk == row_blk, acc, 0.0)

    for bp in pairs:
        b, p = bp
        d = gl[b]
        rows_p = slice(p * L, (p + 1) * L)
        q_p, k_p, b_p = d["q2"][rows_p], d["k2"][rows_p], d["b2"][rows_p]
        e = dict(q_dec=(q_p * jnp.exp2(b_p)).astype(BF16),
                 kh_t=(k_p * jnp.exp2(b_p[L - 1:L] - b_p)).T.astype(BF16),
                 decay_col=jnp.exp2(b_p[L - SUBLANES:L].T[:, SUBLANES - 1:SUBLANES]))
        if nb > 1:
            kt_p = d["kt2"][rows_p].astype(BF16)
            k_bd = jnp.concatenate([jnp.where(lane_head == hh, kt_p, 0.0) for hh in range(2)], axis=0)
            slabs, offs = [], []
            off = 0
            for j in range(nb - 1):
                lo = (j + 1) * c
                slabs.append(q_p[lo:] * jnp.exp2(b_p[lo:] - b_p[lo - 1:lo]))
                offs.append(off)
                off += L - lo
            e.update(q_var=jnp.concatenate(slabs, axis=0).astype(BF16), k_bd=k_bd, offs=offs)
        gl[bp] = e
    yield
    for bp in pairs:
        e = gl[bp]
        if nb > 1:
            e["r"] = _dot_nt(e["q_var"], e["k_bd"])
        if known(bp[0]):
            carried_gla(*bp)

    for g in groups:
        d = ml[g]
        d["sv"] = _dot(d["s"].astype(BF16), d["vb"])
        d["c_upd"] = _dot(d["kw_t"], d["vb"])
        d["den"] = jnp.sum(d["s"], axis=1, keepdims=True)
        if known(g[0]):
            d["qn"] = jnp.sum(d["qf"] * prev(g[0])["n"][g[1]], axis=1, keepdims=True)
    yield
    for bp in pairs:
        b, p = bp
        d, e = gl[b], gl[bp]
        a_p = d["a_diag"][p * L:(p + 1) * L]
        if nb > 1:
            blocks = []
            for i in range(nb):
                blk = a_p[i * c:(i + 1) * c]
                for j in range(i):
                    lo_r = e["offs"][j] + (i - j - 1) * c
                    blk = jnp.where(lane_blk == j, e["r"][lo_r:lo_r + c], blk)
                blocks.append(blk)
            a_p = jnp.concatenate(blocks, axis=0)
        v_f = d["gv"][:, p * 2 * GLA_DV:(p + 1) * 2 * GLA_DV]
        v_p = v_f.astype(BF16)
        if L < GLA_DK:
            v_rows = []
            for hh in range(2):
                v_rows += [jnp.where(v_lane_head == hh, v_f, 0.0), jnp.zeros((GLA_DK - L, 2 * GLA_DV), F32)]
            v_bd = jnp.concatenate(v_rows, axis=0).astype(BF16)
        else:
            v_bd = jnp.concatenate([jnp.where(v_lane_head == hh, v_p, 0.0) for hh in range(2)], axis=0)
        e["s_upd"] = [_dot(e["kh_t"][hh * GLA_DK:(hh + 1) * GLA_DK], v_p[:, hh * GLA_DV:(hh + 1) * GLA_DV])
                      for hh in range(2)]
        e.update(scores=a_p.astype(BF16), v_bd=v_bd)
    yield
    for bp in pairs:
        e = gl[bp]
        e["o_intra"] = _dot(e["scores"], e["v_bd"])

    yield
    ys = [[None] * (ML_HEADS + GLA_HEADS) for _ in range(nchunk)]

    def finish_mlstm(b):
        st = prev(b)
        for h in range(ML_HEADS):
            d = ml[(b, h)]
            mt, w_inter = d["mt"], d["w_inter"]
            qn = d["qn"] if known(b) else jnp.sum(d["qf"] * st["n"][h], axis=1, keepdims=True)
            den = d["den"] + w_inter * qn
            hh = (d["sv"] + w_inter * carried[(b, h)]) / jnp.maximum(jnp.abs(den), jnp.exp2(-mt))
            i_last = w_inter[L - 1:L]
            new_states[b]["c"][h] = i_last * st["c"][h] + d["c_upd"]
            new_states[b]["n"][h] = i_last * st["n"][h] + jnp.sum(d["kw"], axis=0, keepdims=True)
            hs = slice(h * ML_DV, (h + 1) * ML_DV)
            yn = hh * lax.rsqrt(jnp.mean(hh * hh, axis=-1, keepdims=True) + EPS) * gml[:, hs]
            ys[b][h] = jax.nn.sigmoid(zgs[b][:, ZG_MO + h * ML_DV:ZG_MO + (h + 1) * ML_DV]) * yn

    def finish_gla(b):
        st = prev(b)
        for p in range(GLA_PAIRS):
            e = gl[(b, p)]
            o = carried[(b, "gla", p)] + e["o_intra"]
            for hh in range(2):
                h = 2 * p + hh
                ds = slice(hh * GLA_DK, (hh + 1) * GLA_DK)
                vs = slice(hh * GLA_DV, (hh + 1) * GLA_DV)
                new_states[b]["s"][h] = e["decay_col"][ds] * st["s"][h] + e["s_upd"][hh]
                oh = o[:, vs]
                yn = oh * lax.rsqrt(jnp.mean(oh * oh, axis=-1, keepdims=True) + EPS) * ggl[:, h * GLA_DV:(h + 1) * GLA_DV]
                gg = zgs[b][:, ZG_GG + h * GLA_DV:ZG_GG + (h + 1) * GLA_DV]
                ys[b][ML_HEADS + h] = (gg * jax.nn.sigmoid(gg)) * yn

    if chain:
        for b in range(nchunk):
            if not known(b):
                for h in range(ML_HEADS):
                    carried_mlstm(b, h)
                for p in range(GLA_PAIRS):
                    carried_gla(b, p)
            finish_mlstm(b)
            finish_gla(b)
    else:
        for b in range(nchunk):
            finish_mlstm(b)
        for b in range(nchunk):
            finish_gla(b)
    return [jnp.concatenate(y, axis=1) for y in ys], new_states


def _drain(gen):
    try:
        while True:
            next(gen)
    except StopIteration as done:
        return done.value


def _mixer_compute(*args, **kwargs):
    return _drain(_mixer_stages(*args, **kwargs))


def _mixer_kernel(*refs, L, t_real, has_state, n_steps, nblk):
    it = iter(refs)
    zq_ref, zg_ref, bias_ref, wa2_ref, ba_ref, gml_ref, ggl_ref, ee_ref = (next(it) for _ in range(8))
    if has_state:
        c0_ref, n0_ref, m0_ref, s0_ref = (next(it) for _ in range(4))
    y_ref, co_ref, no_ref, mo_ref, so_ref = (next(it) for _ in range(5))
    consts = (bias_ref[...], wa2_ref[...], ba_ref[...], gml_ref[...], ggl_ref[...])
    lane = lax.broadcasted_iota(jnp.int32, (1, LANES), 1)

    def emit_state(b, st):
        for h in range(ML_HEADS):
            co_ref[b, h] = st["c"][h]
            so_ref[b, h] = st["s"][h]
        no_ref[b] = jnp.concatenate(st["n"], axis=0)
        m_row = jnp.zeros((1, LANES), F32)
        for h in range(ML_HEADS):
            m_row = jnp.where(lane == h, st["m"][h], m_row)
        mo_ref[b] = m_row[:, 0:ML_HEADS]

    if has_state:
        states = []
        for b in range(nblk):
            n_all = n0_ref[b]
            m_all = m0_ref[b]
            states.append(dict(c=[c0_ref[b, h] for h in range(ML_HEADS)],
                               n=[n_all[h:h + 1, :] for h in range(ML_HEADS)],
                               m=[m_all[:, h:h + 1] for h in range(ML_HEADS)],
                               s=[s0_ref[b, h] for h in range(GLA_HEADS)]))
        ys, new_states = _mixer_compute([zq_ref[b] for b in range(nblk)], [zg_ref[b] for b in range(nblk)], states,
                                        consts, ee_ref, L=L, t_real=t_real, chain=False)
        for b in range(nblk):
            y_ref[b] = ys[b].astype(y_ref.dtype)
            emit_state(b, new_states[b])
        return

    c_s, n_s, m_s, s_s = (next(it) for _ in range(4))
    step = pl.program_id(1)

    @pl.when(step == 0)
    def _init():
        c_s[...] = jnp.zeros_like(c_s)
        n_s[...] = jnp.zeros_like(n_s)
        m_s[...] = jnp.zeros_like(m_s)
        s_s[...] = jnp.zeros_like(s_s)

    rows = [slice(k * L, (k + 1) * L) for k in range(nblk)]
    state = dict(c=[c_s[h] for h in range(ML_HEADS)], n=[n_s[h:h + 1, :] for h in range(ML_HEADS)],
                 m=[m_s[h:h + 1, 0:1] for h in range(ML_HEADS)], s=[s_s[h] for h in range(GLA_HEADS)])
    ys, new_states = _mixer_compute([zq_ref[0, r, :] for r in rows], [zg_ref[0, r, :] for r in rows], [state],
                                    consts, ee_ref, L=L, t_real=t_real, chain=True)
    for r, y in zip(rows, ys):
        y_ref[0, r, :] = y.astype(y_ref.dtype)
    last = new_states[-1]
    for h in range(ML_HEADS):
        c_s[h] = last["c"][h]
        n_s[h:h + 1, :] = last["n"][h]
        m_s[h:h + 1, :] = jnp.broadcast_to(last["m"][h], (1, LANES))
        s_s[h] = last["s"][h]
    pl.when(step == n_steps - 1)(lambda: emit_state(0, last))


def _mixer(zq, zg, w, L, t_real, state, y_dtype, nblk):
    bsz, t = zq.shape[0], zq.shape[1]
    has_state = state is not None
    consts = [_const_spec((1, SMALL_W)), _const_spec((SMALL_W, GLA_QK_W)), _const_spec((1, GLA_QK_W)),
              _const_spec((1, ML_HEADS * ML_DV)), _const_spec((1, GLA_V_W)), _const_spec((GLA_SUB, LANES, LANES))]
    args = [zq, zg, w["gate_bias"], w["w_a2"], w["b_a"], w["mlstm_out_g"], w["gla_out_g"], w["diag_sum"]]
    if has_state:
        assert t == L
        bb, n_steps, scratch = nblk, 1, []
        seq = lambda width: pl.BlockSpec((bb, L, width), lambda b, j: (b, 0, 0))
    else:
        assert t % (nblk * L) == 0 and t_real == t
        bb, n_steps = 1, t // (nblk * L)
        seq = lambda width: pl.BlockSpec((1, nblk * L, width), lambda b, j: (b, j, 0))
        scratch = [pltpu.VMEM((ML_HEADS, ML_DK, ML_DV), F32), pltpu.VMEM((SUBLANES, LANES), F32),
                   pltpu.VMEM((SUBLANES, LANES), F32), pltpu.VMEM((GLA_HEADS, GLA_DK, GLA_DV), F32)]
    per_b = lambda *tail: pl.BlockSpec((bb,) + tail, lambda b, j: (b,) + (0,) * len(tail))
    state_specs = [per_b(ML_HEADS, ML_DK, ML_DV), per_b(ML_HEADS, ML_DK), per_b(1, ML_HEADS),
                   per_b(GLA_HEADS, GLA_DK, GLA_DV)]
    in_specs = [seq(ZQ_W), seq(ZG_W)] + consts
    if has_state:
        in_specs += state_specs
        args += list(state)
    out_shape = [jax.ShapeDtypeStruct((bsz, t, D_MODEL), y_dtype),
                 jax.ShapeDtypeStruct((bsz, ML_HEADS, ML_DK, ML_DV), F32),
                 jax.ShapeDtypeStruct((bsz, ML_HEADS, ML_DK), F32),
                 jax.ShapeDtypeStruct((bsz, 1, ML_HEADS), F32),
                 jax.ShapeDtypeStruct((bsz, GLA_HEADS, GLA_DK, GLA_DV), F32)]
    kern = functools.partial(_mixer_kernel, L=L, t_real=t_real, has_state=has_state, n_steps=n_steps, nblk=nblk)
    return pl.pallas_call(
        kern,
        grid=(bsz // bb, n_steps),
        in_specs=in_specs,
        out_specs=[seq(D_MODEL)] + state_specs,
        out_shape=out_shape,
        scratch_shapes=scratch,
        compiler_params=_params(("arbitrary", "arbitrary")),
        name="mixer_state" if has_state else "mixer",
    )(*args)


def _ffn_in_pieces(x, g1_ref, wg_ref, wu_ref, wd_ref, gm_ref, wq_ref, wgt_ref, x1_ref, zq_out, zg_out):
    h = _rms(x, g1_ref[...]).astype(BF16)
    acts = []
    for lo in range(0, D_FF, MXU_WIDTH):
        g = _dot(h, wg_ref[:, lo:lo + MXU_WIDTH])
        yield
        u = _dot(h, wu_ref[:, lo:lo + MXU_WIDTH])
        yield
        acts.append(((g * jax.nn.sigmoid(g)) * u).astype(BF16))
    a = jnp.concatenate(acts, axis=1)
    down = []
    for lo in range(0, D_MODEL, MXU_WIDTH):
        down.append(_dot(a, wd_ref[:, lo:lo + MXU_WIDTH]))
        yield
    x1 = x + 0.5 * jnp.concatenate(down, axis=1)
    x1_ref[...] = x1
    hm = _rms(x1, gm_ref[...]).astype(BF16)
    for lo in range(0, ZQ_W, MXU_WIDTH):
        zq_out[:, lo:lo + MXU_WIDTH] = _dot(hm, wq_ref[:, lo:lo + MXU_WIDTH]).astype(zq_out.dtype)
        yield
    for lo in range(0, ZG_W, MXU_WIDTH):
        hi = min(lo + MXU_WIDTH, ZG_W)
        zg_out[:, lo:hi] = _dot(hm, wgt_ref[:, lo:hi])
        yield


FFN_PIECES_PER_MIXER_STAGE = 1


def _ffn_mix_kernel(x_ref, g1_ref, wg_ref, wu_ref, wd_ref, gm_ref, wq_ref, wgt_ref, bias_ref, wa2_ref, ba_ref,
                    gml_ref, ggl_ref, ee_ref, x1_ref, ym_ref, co_ref, no_ref, mo_ref, so_ref,
                    zq_s, zg_s, c_s, n_s, m_s, s_s, *, tiles_per_batch):
    i = pl.program_id(0)

    @pl.when(i == 0)
    def _init():
        for ref in (zq_s, zg_s, c_s, n_s, m_s, s_s):
            ref[...] = jnp.zeros_like(ref)

    consts = (bias_ref[...], wa2_ref[...], ba_ref[...], gml_ref[...], ggl_ref[...])
    n_chunks = zq_s.shape[0] // CHUNK
    rows = [slice(k * CHUNK, (k + 1) * CHUNK) for k in range(n_chunks)]
    zqs = [zq_s[r, :] for r in rows]
    zgs = [zg_s[r, :] for r in rows]
    starts_batch = (i - 1) % tiles_per_batch == 0
    carry = lambda v: jnp.where(starts_batch, 0.0, v)
    state = dict(c=[carry(c_s[h]) for h in range(ML_HEADS)], n=[carry(n_s[h:h + 1, :]) for h in range(ML_HEADS)],
                 m=[carry(m_s[h:h + 1, 0:1]) for h in range(ML_HEADS)], s=[carry(s_s[h]) for h in range(GLA_HEADS)])

    def mixers():
        st = state
        for k in range(n_chunks):
            ys, new = yield from _mixer_stages([zqs[k]], [zgs[k]], [st], consts, ee_ref, L=CHUNK, t_real=CHUNK,
                                               chain=True)
            ym_ref[rows[k], :] = ys[0].astype(ym_ref.dtype)
            st = new[0]
            yield
        return st

    ffn = _ffn_in_pieces(x_ref[...], g1_ref, wg_ref, wu_ref, wd_ref, gm_ref, wq_ref, wgt_ref, x1_ref, zq_s, zg_s)
    mix = mixers()
    ffn_live = True
    while True:
        try:
            next(mix)
        except StopIteration as done:
            last = done.value
            break
        for _ in range(FFN_PIECES_PER_MIXER_STAGE):
            if ffn_live:
                try:
                    next(ffn)
                except StopIteration:
                    ffn_live = False
    if ffn_live:
        _drain(ffn)

    for h in range(ML_HEADS):
        c_s[h] = last["c"][h]
        n_s[h:h + 1, :] = last["n"][h]
        m_s[h:h + 1, :] = jnp.broadcast_to(last["m"][h], (1, LANES))
        s_s[h] = last["s"][h]

    @pl.when(jnp.logical_and(i >= 1, (i - 1) % tiles_per_batch == tiles_per_batch - 1))
    def _emit_state():
        lane = lax.broadcasted_iota(jnp.int32, (1, LANES), 1)
        for h in range(ML_HEADS):
            co_ref[0, h] = last["c"][h]
            so_ref[0, h] = last["s"][h]
        no_ref[0] = jnp.concatenate(last["n"], axis=0)
        m_row = jnp.zeros((1, LANES), F32)
        for h in range(ML_HEADS):
            m_row = jnp.where(lane == h, last["m"][h], m_row)
        mo_ref[0] = m_row[:, 0:ML_HEADS]


def _ffn_mix(x, w, tm):
    bsz, t, _ = x.shape
    tiles_per_batch = t // tm
    n_tiles = bsz * tiles_per_batch
    assert t % tm == 0 and tm % CHUNK == 0
    cur = lambda width: pl.BlockSpec((tm, width), lambda i: (jnp.minimum(i, n_tiles - 1), 0))
    prev_tile = lambda i: jnp.maximum(i - 1, 0)
    per_b = lambda *tail: pl.BlockSpec((1,) + tail, lambda i: (prev_tile(i) // tiles_per_batch,) + (0,) * len(tail))
    state_specs = [per_b(ML_HEADS, ML_DK, ML_DV), per_b(ML_HEADS, ML_DK), per_b(1, ML_HEADS),
                   per_b(GLA_HEADS, GLA_DK, GLA_DV)]
    x1, ym, c_new, n_new, m_new, s_new = pl.pallas_call(
        functools.partial(_ffn_mix_kernel, tiles_per_batch=tiles_per_batch),
        grid=(n_tiles + 1,),
        in_specs=[cur(D_MODEL), _const_spec((1, D_MODEL)), _const_spec((D_MODEL, D_FF)),
                  _const_spec((D_MODEL, D_FF)), _const_spec((D_FF, D_MODEL)), _const_spec((1, D_MODEL)),
                  _const_spec((D_MODEL, ZQ_W)), _const_spec((D_MODEL, ZG_W)), _const_spec((1, SMALL_W)),
                  _const_spec((SMALL_W, GLA_QK_W)), _const_spec((1, GLA_QK_W)), _const_spec((1, ML_HEADS * ML_DV)),
                  _const_spec((1, GLA_V_W)), _const_spec((GLA_SUB, LANES, LANES))],
        out_specs=[cur(D_MODEL), pl.BlockSpec((tm, D_MODEL), lambda i: (prev_tile(i), 0))] + state_specs,
        out_shape=[jax.ShapeDtypeStruct((n_tiles * tm, D_MODEL), F32),
                   jax.ShapeDtypeStruct((n_tiles * tm, D_MODEL), BF16),
                   jax.ShapeDtypeStruct((bsz, ML_HEADS, ML_DK, ML_DV), F32),
                   jax.ShapeDtypeStruct((bsz, ML_HEADS, ML_DK), F32),
                   jax.ShapeDtypeStruct((bsz, 1, ML_HEADS), F32),
                   jax.ShapeDtypeStruct((bsz, GLA_HEADS, GLA_DK, GLA_DV), F32)],
        scratch_shapes=[pltpu.VMEM((tm, ZQ_W), BF16), pltpu.VMEM((tm, ZG_W), F32),
                        pltpu.VMEM((ML_HEADS, ML_DK, ML_DV), F32), pltpu.VMEM((SUBLANES, LANES), F32),
                        pltpu.VMEM((SUBLANES, LANES), F32), pltpu.VMEM((GLA_HEADS, GLA_DK, GLA_DV), F32)],
        compiler_params=_params(("arbitrary",)),
        name="ffn_mix",
    )(x.reshape(bsz * t, D_MODEL), w["ffn1_g"], w["ffn1_wg"], w["ffn1_wu"], w["ffn1_wd"], w["mix_g"], w["w_in_q"],
      w["w_in_g"], w["gate_bias"], w["w_a2"], w["b_a"], w["mlstm_out_g"], w["gla_out_g"], w["diag_sum"])
    return x1, ym, (c_new[None], n_new[None], m_new.reshape(1, bsz, ML_HEADS), s_new[None])


def _post_mix_kernel(x1_ref, ym_ref, wout_ref, gx_ref, wq_ref, x2_ref, q_ref):
    x2 = x1_ref[...] + _dot(ym_ref[...].astype(BF16), wout_ref[...])
    x2_ref[...] = x2
    hq = _rms(x2, gx_ref[...]).astype(BF16)
    q_ref[...] = _dot(hq, wq_ref[...]).astype(q_ref.dtype)


def _post_mix(x1, ym, w, tm, q_dtype):
    n = x1.shape[0]
    row = lambda: pl.BlockSpec((tm, D_MODEL), lambda i: (i, 0))
    return pl.pallas_call(
        _post_mix_kernel,
        grid=(n // tm,),
        in_specs=[row(), row(), _const_spec((D_MODEL, D_MODEL)), _const_spec((1, D_MODEL)),
                  _const_spec((D_MODEL, D_MODEL))],
        out_specs=[row(), row()],
        out_shape=[jax.ShapeDtypeStruct((n, D_MODEL), F32), jax.ShapeDtypeStruct((n, D_MODEL), q_dtype)],
        compiler_params=_params(("arbitrary",)),
        name="post_mix",
    )(x1, ym, w["w_out"], w["xattn_g"], w["xattn_wq"])


def _softmax(s):
    e = jnp.exp(s - jnp.max(s, axis=-1, keepdims=True))
    return e / jnp.sum(e, axis=-1, keepdims=True)


def _post_fused_kernel(x1_ref, ym_ref, k_ref, v_ref, wout_ref, gx_ref, wq_ref, wo_ref, g2_ref, wg_ref, wu_ref,
                       wd_ref, gf_ref, y_ref):
    x2 = x1_ref[0] + _dot(ym_ref[0].astype(BF16), wout_ref[...])
    q = _dot(_rms(x2, gx_ref[...]).astype(BF16), wq_ref[...]).astype(BF16)
    heads = [slice(h * XA_DH, (h + 1) * XA_DH) for h in range(XA_HEADS)]
    scores = [_dot_nt(q[:, hs], k_ref[0, :, hs].astype(BF16)) * (XA_DH ** -0.5) for hs in heads]
    probs = [_softmax(s).astype(BF16) for s in scores]
    o = jnp.concatenate([_dot(p, v_ref[0, :, hs].astype(BF16)).astype(BF16) for hs, p in zip(heads, probs)], axis=1)
    x3 = x2 + _dot(o, wo_ref[...])
    x4 = _swiglu_residual(x3, g2_ref, wg_ref, wu_ref, wd_ref)
    y_ref[0] = _rms(x4, gf_ref[...])


def _post_fused(x1, ym, k, v, w, tm):
    bsz, t = x1.shape[0], x1.shape[1]
    row = lambda: pl.BlockSpec((1, tm, D_MODEL), lambda b, j: (b, j, 0))
    kv = pl.BlockSpec((1, N_MEM, D_MODEL), lambda b, j: (b, 0, 0))
    sq = _const_spec((D_MODEL, D_MODEL))
    vec = _const_spec((1, D_MODEL))
    return pl.pallas_call(
        _post_fused_kernel,
        grid=(bsz, t // tm),
        in_specs=[row(), row(), kv, kv, sq, vec, sq, sq, vec, _const_spec((D_MODEL, D_FF)),
                  _const_spec((D_MODEL, D_FF)), _const_spec((D_FF, D_MODEL)), vec],
        out_specs=row(),
        out_shape=jax.ShapeDtypeStruct((bsz, t, D_MODEL), F32),
        compiler_params=_params(("arbitrary", "arbitrary")),
        name="post_fused",
    )(x1, ym, k, v, w["w_out"], w["xattn_g"], w["xattn_wq"], w["xattn_wo"], w["ffn2_g"], w["ffn2_wg"],
      w["ffn2_wu"], w["ffn2_wd"], w["final_g"])


XA_LANE_TILES = XA_DH // LANES
CACHE_ROW_GROUP = XA_LANE_TILES * XA_HEADS


def _cache_rows_view(x):
    bsz = x.shape[0]
    x = x.reshape(bsz, N_MEM, XA_HEADS, XA_LANE_TILES, LANES)
    return x.transpose(0, 1, 3, 2, 4).reshape(bsz, N_MEM * CACHE_ROW_GROUP, LANES)


def _cache_rows_load(ref, b):
    cols = [ref[b, pl.ds(lt * XA_HEADS + h, N_MEM, stride=CACHE_ROW_GROUP), :]
            for h in range(XA_HEADS) for lt in range(XA_LANE_TILES)]
    return jnp.concatenate(cols, axis=1).astype(BF16)


def _xattn_cache_kernel(q_ref, k_ref, v_ref, o_ref, *, bb, tq):
    lane_head = lax.broadcasted_iota(jnp.int32, (1, D_MODEL), 1) // XA_DH
    scores = []
    for b in range(bb):
        q = q_ref[b]
        q_bd = jnp.concatenate([jnp.where(lane_head == h, q, 0.0) for h in range(XA_HEADS)], axis=0)
        k_full = _cache_rows_load(k_ref, b)
        scores.append(_dot_nt(q_bd.astype(BF16), k_full) * (XA_DH ** -0.5))
    p_all = _softmax(jnp.concatenate(scores, axis=0)).astype(BF16)
    rows = XA_HEADS * tq
    for b in range(bb):
        v_full = _cache_rows_load(v_ref, b)
        o_full = _dot(p_all[b * rows:(b + 1) * rows], v_full)
        o = jnp.zeros((tq, D_MODEL), F32)
        for h in range(XA_HEADS):
            o = jnp.where(lane_head == h, o_full[h * tq:(h + 1) * tq], o)
        o_ref[b] = o.astype(o_ref.dtype)


def _xattn_cache(q, k, v, bb, o_dtype):
    bsz, tq = q.shape[0], q.shape[1]
    k, v = _cache_rows_view(k), _cache_rows_view(v)
    qo = pl.BlockSpec((bb, tq, D_MODEL), lambda b: (b, 0, 0))
    kv = pl.BlockSpec((bb, N_MEM * CACHE_ROW_GROUP, LANES), lambda b: (b, 0, 0))
    return pl.pallas_call(
        functools.partial(_xattn_cache_kernel, bb=bb, tq=tq),
        grid=(bsz // bb,),
        in_specs=[qo, kv, kv],
        out_specs=qo,
        out_shape=jax.ShapeDtypeStruct((bsz, tq, D_MODEL), o_dtype),
        compiler_params=_params(("arbitrary",)),
        name="xattn_cache",
    )(q, k, v)


def _ffn_out_kernel(x2_ref, o_ref, wo_ref, g2_ref, wg_ref, wu_ref, wd_ref, gf_ref, y_ref):
    x3 = x2_ref[...] + _dot(o_ref[...].astype(BF16), wo_ref[...])
    x4 = _swiglu_residual(x3, g2_ref, wg_ref, wu_ref, wd_ref)
    y_ref[...] = _rms(x4, gf_ref[...])


def _ffn_out(x2, o, w, tm):
    n = x2.shape[0]
    row = lambda: pl.BlockSpec((tm, D_MODEL), lambda i: (i, 0))
    return pl.pallas_call(
        _ffn_out_kernel,
        grid=(n // tm,),
        in_specs=[row(), row(), _const_spec((D_MODEL, D_MODEL)), _const_spec((1, D_MODEL)),
                  _const_spec((D_MODEL, D_FF)), _const_spec((D_MODEL, D_FF)), _const_spec((D_FF, D_MODEL)),
                  _const_spec((1, D_MODEL))],
        out_specs=row(),
        out_shape=jax.ShapeDtypeStruct((n, D_MODEL), F32),
        compiler_params=_params(("arbitrary",)),
        name="ffn_out",
    )(x2, o, w["xattn_wo"], w["ffn2_g"], w["ffn2_wg"], w["ffn2_wu"], w["ffn2_wd"], w["final_g"])


def _memkv_kernel(m_ref, g_ref, wk_ref, wv_ref, k_ref, v_ref):
    hn = _rms(m_ref[...], g_ref[...]).astype(BF16)
    k_ref[...] = _dot(hn, wk_ref[...])
    v_ref[...] = _dot(hn, wv_ref[...])


def _memkv(mem, w, tm):
    n = mem.shape[0]
    row = lambda: pl.BlockSpec((tm, D_MODEL), lambda i: (i, 0))
    return pl.pallas_call(
        _memkv_kernel,
        grid=(n // tm,),
        in_specs=[row(), _const_spec((1, D_MODEL)), _const_spec((D_MODEL, D_MODEL)),
                  _const_spec((D_MODEL, D_MODEL))],
        out_specs=[row(), row()],
        out_shape=[jax.ShapeDtypeStruct((n, D_MODEL), F32)] * 2,
        compiler_params=_params(("arbitrary",)),
        name="memkv",
    )(mem, w["mem_g"], w["xattn_wk"], w["xattn_wv"])


def _prep_weights(p):
    bf = lambda a: a.astype(BF16)
    row = lambda a: a.reshape(1, -1).astype(F32)
    w_in = p["w_in"]
    off = {}
    pos = 0
    for name, width in (("mq", 512), ("mk", 512), ("mv", 512), ("mi", 4), ("mf", 4), ("mo", 512), ("gq", 256),
                        ("gk", 256), ("gv", 512), ("ga", 16), ("gg", 512)):
        off[name] = w_in[:, pos:pos + width]
        pos += width
    pad_cols = SMALL_W - 2 * ML_HEADS - GLA_RANK
    w_in_q = jnp.concatenate([off[k] for k in ("mq", "mk", "mv", "gq", "gk", "gv")], axis=1)
    w_in_g = jnp.concatenate([off["mo"], off["gg"], off["mi"], off["mf"], off["ga"],
                              jnp.zeros((D_MODEL, pad_cols), F32)], axis=1)
    gate_bias = jnp.concatenate([p["mlstm_b_i"], p["mlstm_b_f"], jnp.zeros((SMALL_W - 2 * ML_HEADS,), F32)])
    w_a2 = jnp.zeros((SMALL_W, GLA_QK_W), F32).at[2 * ML_HEADS:2 * ML_HEADS + GLA_RANK].set(p["gla_w_a2"])
    lane = jnp.arange(LANES)
    same_head = (lane[:, None] // GLA_DK) == (lane[None, :] // GLA_DK)
    diag_sum = jnp.stack([same_head & ((lane[None, :] % GLA_SUB) == j) for j in range(GLA_SUB)]).astype(BF16)
    return dict(
        ffn1_g=row(p["ffn1_norm_g"]), ffn1_wg=bf(p["ffn1_w_gate"]), ffn1_wu=bf(p["ffn1_w_up"]),
        ffn1_wd=bf(p["ffn1_w_down"]), mix_g=row(p["mix_norm_g"]), w_in_q=bf(w_in_q), w_in_g=bf(w_in_g),
        gate_bias=row(gate_bias), w_a2=bf(w_a2), b_a=row(p["gla_b_a"]), mlstm_out_g=row(p["mlstm_out_g"]),
        gla_out_g=row(p["gla_out_g"]), diag_sum=diag_sum,
        w_out=bf(p["w_out"]), xattn_g=row(p["xattn_norm_g"]), xattn_wq=bf(p["xattn_w_q"]),
        xattn_wo=bf(p["xattn_w_o"]), mem_g=row(p["mem_norm_g"]), xattn_wk=bf(p["xattn_w_k"]),
        xattn_wv=bf(p["xattn_w_v"]), ffn2_g=row(p["ffn2_norm_g"]), ffn2_wg=bf(p["ffn2_w_gate"]),
        ffn2_wu=bf(p["ffn2_w_up"]), ffn2_wd=bf(p["ffn2_w_down"]), final_g=row(p["final_g"]))


def _ffn_and_mixers(x, state, w, *, chunk, mixer_nblk, act_dtype):
    bsz, t, _ = x.shape
    x1, zq, zg = _ffn_in(x.reshape(bsz * t, D_MODEL), w, TM_FFN_IN, act_dtype)
    zq = zq.reshape(bsz, t, ZQ_W)
    zg = zg.reshape(bsz, t, ZG_W)
    pad = -t % chunk
    if pad:
        zq = jnp.pad(zq, ((0, 0), (0, pad), (0, 0)))
        zg = jnp.pad(zg, ((0, 0), (0, pad), (0, 0)))
    ym, c_new, n_new, m_new, s_new = _mixer(zq, zg, w, chunk, t, state, act_dtype, mixer_nblk)
    return x1, ym[:, :t], (c_new[None], n_new[None], m_new.reshape(1, bsz, ML_HEADS), s_new[None])


def kernel(x_prompt, x_sample, mem_prompt, cache_mem_k, cache_mem_v, state_mlstm_c, state_mlstm_n, state_mlstm_m, state_gla_s, ffn1_norm_g, ffn1_w_gate, ffn1_w_up, ffn1_w_down, mix_norm_g, w_in, mlstm_b_i, mlstm_b_f, mlstm_out_g, gla_w_a2, gla_b_a, gla_out_g, w_out, xattn_norm_g, mem_norm_g, xattn_w_q, xattn_w_k, xattn_w_v, xattn_w_o, ffn2_norm_g, ffn2_w_gate, ffn2_w_up, ffn2_w_down, final_norm_g):
    assert ffn1_norm_g.shape[0] == 1, "single-layer stack"
    layer = dict(ffn1_norm_g=ffn1_norm_g, ffn1_w_gate=ffn1_w_gate, ffn1_w_up=ffn1_w_up, ffn1_w_down=ffn1_w_down,
                 mix_norm_g=mix_norm_g, w_in=w_in, mlstm_b_i=mlstm_b_i, mlstm_b_f=mlstm_b_f,
                 mlstm_out_g=mlstm_out_g, gla_w_a2=gla_w_a2, gla_b_a=gla_b_a, gla_out_g=gla_out_g, w_out=w_out,
                 xattn_norm_g=xattn_norm_g, mem_norm_g=mem_norm_g, xattn_w_q=xattn_w_q, xattn_w_k=xattn_w_k,
                 xattn_w_v=xattn_w_v, xattn_w_o=xattn_w_o, ffn2_norm_g=ffn2_norm_g, ffn2_w_gate=ffn2_w_gate,
                 ffn2_w_up=ffn2_w_up, ffn2_w_down=ffn2_w_down)
    p = {name: arr[0] for name, arr in layer.items()}
    p["final_g"] = final_norm_g
    w = _prep_weights(p)

    bp, tp, _ = x_prompt.shape
    bs, ts, _ = x_sample.shape

    mem_k_p, mem_v_p = _memkv(mem_prompt.reshape(bp * N_MEM, D_MODEL), w, TM_POST)
    mem_k_p = mem_k_p.reshape(bp, N_MEM, D_MODEL)
    mem_v_p = mem_v_p.reshape(bp, N_MEM, D_MODEL)
    x1_p, ym_p, states_p = _ffn_mix(x_prompt, w, TM_FFN_IN)
    y_p = _post_fused(x1_p.reshape(bp, tp, D_MODEL), ym_p.reshape(bp, tp, D_MODEL), mem_k_p, mem_v_p, w, TM_POST)

    state = (state_mlstm_c[0], state_mlstm_n[0], state_mlstm_m[0].reshape(bs, 1, ML_HEADS), state_gla_s[0])
    x1_s, ym_s, states_s = _ffn_and_mixers(x_sample, state, w, chunk=SUBLANES, mixer_nblk=SAMPLE_MIXER_BATCHES,
                                           act_dtype=F32)
    x2_s, q_s = _post_mix(x1_s, ym_s.reshape(bs * ts, D_MODEL), w, TM_FFN_IN, F32)
    o_s = _xattn_cache(q_s.reshape(bs, ts, D_MODEL), cache_mem_k[0], cache_mem_v[0], SAMPLE_XATTN_BATCHES, F32)
    y_s = _ffn_out(x2_s, o_s.reshape(bs * ts, D_MODEL), w, TM_FFN_IN).reshape(bs, ts, D_MODEL)

    kv_shape = (1, bp, N_MEM, XA_HEADS, XA_DH)
    return (y_p, y_s, mem_k_p.reshape(kv_shape), mem_v_p.reshape(kv_shape)) + states_p + states_s
```

```python
import functools

import jax
import jax.numpy as jnp
from jax import lax
from jax.experimental import pallas as pl
from jax.experimental.pallas import tpu as pltpu

F32 = jnp.float32
BF16 = jnp.bfloat16

D_MODEL = 1024
D_FF = 2816
ML_HEADS = 4
ML_DK = 128
ML_DV = 128
GLA_HEADS = 4
GLA_DK = 64
GLA_DV = 128
GLA_RANK = 16
GLA_TAU = 16.0
N_MEM = 256
XA_HEADS = 4
XA_DH = D_MODEL // XA_HEADS
EPS = 1e-6
CHUNK = 64
LOG2_E = 1.4426950408889634
LN_2 = 0.6931471805599453
LANES = 128
SUBLANES = 8

ZQ_MQ, ZQ_MK, ZQ_MV = 0, 512, 1024
ZQ_GQ, ZQ_GK, ZQ_GV = 1536, 1792, 2048
ZQ_W = 2560
ZG_MO, ZG_GG, ZG_SMALL = 0, 512, 1024
ZG_W = 1152
SMALL_W = LANES
GLA_QK_W = GLA_HEADS * GLA_DK
GLA_V_W = GLA_HEADS * GLA_DV
GLA_PAIRS = GLA_HEADS // 2
GLA_SUB = SUBLANES

MXU_WIDTH = 256
FF_CHUNKS = ((0, 6 * MXU_WIDTH), (6 * MXU_WIDTH, D_FF))
VMEM_LIMIT_BYTES = 56 * 1024 * 1024

TM_FFN_IN = MXU_WIDTH
TM_POST = 2 * MXU_WIDTH
PROMPT_MIXER_CHUNKS = 1
SAMPLE_MIXER_BATCHES = 8
SAMPLE_XATTN_BATCHES = 4


def _rms(x, g):
    return x * lax.rsqrt(jnp.mean(x * x, axis=-1, keepdims=True) + EPS) * g


def _log_sigmoid(x):
    return jnp.minimum(x, 0.0) - jnp.log1p(jnp.exp(-jnp.abs(x)))


def _dot(a, b):
    return jnp.dot(a, b, preferred_element_type=F32)


def _dot_nt(a, b):
    return lax.dot_general(a, b, (((1,), (1,)), ((), ())), preferred_element_type=F32)


def _dot_f32(a, b):
    return jnp.dot(a, b, precision=lax.Precision.HIGHEST, preferred_element_type=F32)


def _swiglu_residual(x, g_ref, wg_ref, wu_ref, wd_ref):
    h = _rms(x, g_ref[...]).astype(BF16)
    acts = []
    for lo, hi in FF_CHUNKS:
        g = _dot(h, wg_ref[:, lo:hi])
        u = _dot(h, wu_ref[:, lo:hi])
        acts.append(((g * jax.nn.sigmoid(g)) * u).astype(BF16))
    acc = jnp.zeros_like(x)
    for (lo, hi), a in zip(FF_CHUNKS, acts):
        acc = acc + _dot(a, wd_ref[lo:hi, :])
    return x + 0.5 * acc


def _const_spec(shape):
    nd = len(shape)
    return pl.BlockSpec(shape, lambda *_: (0,) * nd, pipeline_mode=pl.Buffered(1))


def _params(sem):
    return pltpu.CompilerParams(dimension_semantics=sem, vmem_limit_bytes=VMEM_LIMIT_BYTES)


def _ffn_in_kernel(x_ref, g1_ref, wg_ref, wu_ref, wd_ref, gm_ref, wq_ref, wgt_ref, x1_ref, zq_ref, zg_ref):
    x1 = _swiglu_residual(x_ref[...], g1_ref, wg_ref, wu_ref, wd_ref)
    x1_ref[...] = x1
    hm = _rms(x1, gm_ref[...]).astype(BF16)
    zq_ref[...] = _dot(hm, wq_ref[...]).astype(zq_ref.dtype)
    zg_ref[...] = _dot(hm, wgt_ref[...])


def _ffn_in(x, w, tm, zq_dtype):
    n = x.shape[0]
    row = lambda width: pl.BlockSpec((tm, width), lambda i: (i, 0))
    return pl.pallas_call(
        _ffn_in_kernel,
        grid=(n // tm,),
        in_specs=[row(D_MODEL), _const_spec((1, D_MODEL)), _const_spec((D_MODEL, D_FF)),
                  _const_spec((D_MODEL, D_FF)), _const_spec((D_FF, D_MODEL)), _const_spec((1, D_MODEL)),
                  _const_spec((D_MODEL, ZQ_W)), _const_spec((D_MODEL, ZG_W))],
        out_specs=[row(D_MODEL), row(ZQ_W), row(ZG_W)],
        out_shape=[jax.ShapeDtypeStruct((n, D_MODEL), F32), jax.ShapeDtypeStruct((n, ZQ_W), zq_dtype),
                   jax.ShapeDtypeStruct((n, ZG_W), F32)],
        compiler_params=_params(("arbitrary",)),
        name="ffn_in",
    )(x, w["ffn1_g"], w["ffn1_wg"], w["ffn1_wu"], w["ffn1_wd"], w["mix_g"], w["w_in_q"], w["w_in_g"])


def _mixer_stages(zqs, zgs, states, consts, ee_ref, *, L, t_real, chain):
    bias, wa2, ba, gml, ggl = consts
    nchunk = len(zqs)
    groups = [(b, h) for b in range(nchunk) for h in range(ML_HEADS)]
    pairs = [(b, p) for b in range(nchunk) for p in range(GLA_PAIRS)]
    padded = t_real < L
    valid = lax.broadcasted_iota(jnp.int32, (L, 1), 0) < t_real
    rr = lax.broadcasted_iota(jnp.int32, (L, L), 0)
    cc = lax.broadcasted_iota(jnp.int32, (L, L), 1)
    tril = cc <= rr
    c = GLA_SUB
    nb = L // c
    lane = lax.broadcasted_iota(jnp.int32, (1, LANES), 1)
    lane_blk = (lane % GLA_DK) // c
    lane_head = lane // GLA_DK
    v_lane_head = lax.broadcasted_iota(jnp.int32, (1, 2 * GLA_DV), 1) // GLA_DV
    row_blk = (lax.broadcasted_iota(jnp.int32, (2 * L, 1), 0) % L) // c
    t_in = lax.broadcasted_iota(jnp.int32, (1, c, 1), 1)
    new_states = [dict(c=[None] * ML_HEADS, n=[None] * ML_HEADS, m=[None] * ML_HEADS, s=[None] * GLA_HEADS)
                  for _ in range(nchunk)]
    prev = lambda b: new_states[b - 1] if chain and b > 0 else states[0 if chain else b]
    known = lambda b: not chain or b == 0

    carried = {}

    def carried_mlstm(b, h):
        st = prev(b)
        d = ml[(b, h)]
        carried[(b, h)] = _dot(d["qb"], st["c"][h].astype(BF16))

    def carried_gla(b, p):
        s_prev = prev(b)["s"]
        zero_blk = jnp.zeros((GLA_DK, GLA_DV), BF16)
        s_bd = jnp.concatenate(
            [jnp.concatenate([s_prev[2 * p].astype(BF16), zero_blk], axis=1),
             jnp.concatenate([zero_blk, s_prev[2 * p + 1].astype(BF16)], axis=1)], axis=0)
        carried[(b, "gla", p)] = _dot(gl[(b, p)]["q_dec"], s_bd)

    smalls, sms, lfs, b_cols, b_rows, sm_ts, las, bcs = [], [], [], [], [], [], [], []
    for b in range(nchunk):
        small = zgs[b][:, ZG_SMALL:ZG_SMALL + SMALL_W]
        sm = small + bias
        lf = _log_sigmoid(sm) * LOG2_E
        sm = sm * LOG2_E
        if padded:
            sm = jnp.where(valid, sm, -jnp.inf)
            lf = jnp.where(valid, lf, 0.0)
        smalls.append(small)
        sms.append(sm)
        lfs.append(lf)

    long_chunk = L > SUBLANES
    tril_f, triu_f = tril.astype(F32), (rr <= cc).astype(F32)

    def cumsum_rows(x):
        if long_chunk:
            return _dot_f32(tril_f, x)
        acc = x[0:1]
        rows = [acc]
        for r in range(1, L):
            acc = acc + x[r:r + 1]
            rows.append(acc)
        return jnp.concatenate(rows, axis=0)

    lf_ts = [lf.T[0:SUBLANES] for lf in lfs] if long_chunk else None
    yield
    for b in range(nchunk):
        b_cols.append(cumsum_rows(lfs[b]))
        b_rows.append(_dot_f32(lf_ts[b], triu_f) if long_chunk else b_cols[b].T[0:SUBLANES])
        sm_ts.append(sms[b].T)
        la = _log_sigmoid(_dot(smalls[b].astype(BF16), wa2) + ba) * (LOG2_E / GLA_TAU)
        las.append(jnp.where(valid, la, 0.0) if padded else la)

    ml = {}
    for g in groups:
        b, h = g
        zq = zqs[b]
        qf = zq[:, ZQ_MQ + h * ML_DK:ZQ_MQ + (h + 1) * ML_DK].astype(F32)
        kf = zq[:, ZQ_MK + h * ML_DK:ZQ_MK + (h + 1) * ML_DK].astype(F32) * (ML_DK ** -0.5)
        vf = zq[:, ZQ_MV + h * ML_DV:ZQ_MV + (h + 1) * ML_DV].astype(F32)
        if padded:
            kf = jnp.where(valid, kf, 0.0)
            vf = jnp.where(valid, vf, 0.0)
        ml[g] = dict(qf=qf, qb=qf.astype(BF16), kf=kf, kb=kf.astype(BF16), vb=vf.astype(BF16))
    yield
    for b in range(nchunk):
        bcs.append(cumsum_rows(las[b]))
    for g in groups:
        d = ml[g]
        d["qk"] = _dot_nt(d["qb"], d["kb"])
        if known(g[0]):
            carried_mlstm(*g)

    yield
    for g in groups:
        b, h = g
        d = ml[g]
        i_col = sms[b][:, h:h + 1]
        b_col = b_cols[b][:, ML_HEADS + h:ML_HEADS + h + 1]
        b_row = b_rows[b][ML_HEADS + h:ML_HEADS + h + 1, :]
        i_row = sm_ts[b][h:h + 1, :]
        a_col = b_col + prev(b)["m"][h] * LOG2_E
        dm = jnp.where(tril, b_col - (b_row - i_row), -jnp.inf)
        mt = jnp.maximum(a_col, jnp.max(dm, axis=1, keepdims=True))
        w_inter = jnp.exp2(a_col - mt)
        s = d["qk"] * jnp.exp2(dm - mt)
        kw = d["kf"] * jnp.exp2((b_col[L - 1:L] - mt[L - 1:L]) - (b_col - i_col))
        d.update(mt=mt, w_inter=w_inter, s=s, kw=kw, kw_t=kw.T.astype(BF16))
        new_states[b]["m"][h] = mt[L - 1:L] * LN_2

    gl = {}
    for b in range(nchunk):
        zq = zqs[b]
        gq = zq[:, ZQ_GQ:ZQ_GQ + GLA_QK_W].astype(F32) * (GLA_DK ** -0.5)
        gk = zq[:, ZQ_GK:ZQ_GK + GLA_QK_W].astype(F32)
        gv = zq[:, ZQ_GV:ZQ_GV + GLA_V_W].astype(F32)
        if padded:
            gk = jnp.where(valid, gk, 0.0)
            gv = jnp.where(valid, gv, 0.0)
        stack = lambda x: jnp.concatenate([x[:, :LANES], x[:, LANES:]], axis=0)
        q2, k2, b2 = stack(gq), stack(gk), stack(bcs[b])
        q3 = q2.reshape(2 * nb, c, LANES)
        k3 = k2.reshape(2 * nb, c, LANES)
        b3 = b2.reshape(2 * nb, c, LANES)
        pair_terms = []
        for j in range(min(c, t_real)):
            decay = jnp.exp2(jnp.where(t_in >= j, b3 - b3[:, j:j + 1, :], -jnp.inf))
            pair_terms.append((q3 * k3[:, j:j + 1, :] * decay).reshape(2 * L, LANES).astype(BF16))
        kt2 = (k3 * jnp.exp2(b3[:, c - 1:c, :] - b3)).reshape(2 * L, LANES) if nb > 1 else None
        gl[b] = dict(gv=gv, q2=q2, k2=k2, b2=b2, pair_terms=pair_terms, kt2=kt2)
    yield
    for b in range(nchunk):
        acc = jnp.zeros((2 * L, LANES), F32)
        for j, pair_j in enumerate(gl[b]["pair_terms"]):
            acc = acc + _dot(pair_j, ee_ref[j])
        gl[b]["a_diag"] = jnp.where(lane_blk == row_blk, acc, 0.0)

    for bp in pairs:
        b, p = bp
        d = gl[b]
        rows_p = slice(p * L, (p + 1) * L)
        q_p, k_p, b_p = d["q2"][rows_p], d["k2"][rows_p], d["b2"][rows_p]
        e = dict(q_dec=(q_p * jnp.exp2(b_p)).astype(BF16),
                 kh_t=(k_p * jnp.exp2(b_p[L - 1:L] - b_p)).T.astype(BF16),
                 decay_col=jnp.exp2(b_p[L - SUBLANES:L].T[:, SUBLANES - 1:SUBLANES]))
        if nb > 1:
            kt_p = d["kt2"][rows_p].astype(BF16)
            k_bd = jnp.concatenate([jnp.where(lane_head == hh, kt_p, 0.0) for hh in range(2)], axis=0)
            slabs, offs = [], []
            off = 0
            for j in range(nb - 1):
                lo = (j + 1) * c
                slabs.append(q_p[lo:] * jnp.exp2(b_p[lo:] - b_p[lo - 1:lo]))
                offs.append(off)
                off += L - lo
            e.update(q_var=jnp.concatenate(slabs, axis=0).astype(BF16), k_bd=k_bd, offs=offs)
        gl[bp] = e
    yield
    for bp in pairs:
        e = gl[bp]
        if nb > 1:
            e["r"] = _dot_nt(e["q_var"], e["k_bd"])
        if known(bp[0]):
            carried_gla(*bp)

    for g in groups:
        d = ml[g]
        d["sv"] = _dot(d["s"].astype(BF16), d["vb"])
        d["c_upd"] = _dot(d["kw_t"], d["vb"])
        d["den"] = jnp.sum(d["s"], axis=1, keepdims=True)
        if known(g[0]):
            d["qn"] = jnp.sum(d["qf"] * prev(g[0])["n"][g[1]], axis=1, keepdims=True)
    yield
    for bp in pairs:
        b, p = bp
        d, e = gl[b], gl[bp]
        a_p = d["a_diag"][p * L:(p + 1) * L]
        if nb > 1:
            blocks = []
            for i in range(nb):
                blk = a_p[i * c:(i + 1) * c]
                for j in range(i):
                    lo_r = e["offs"][j] + (i - j - 1) * c
                    blk = jnp.where(lane_blk == j, e["r"][lo_r:lo_r + c], blk)
                blocks.append(blk)
            a_p = jnp.concatenate(blocks, axis=0)
        v_f = d["gv"][:, p * 2 * GLA_DV:(p + 1) * 2 * GLA_DV]
        v_p = v_f.astype(BF16)
        if L < GLA_DK:
            v_rows = []
            for hh in range(2):
                v_rows += [jnp.where(v_lane_head == hh, v_f, 0.0), jnp.zeros((GLA_DK - L, 2 * GLA_DV), F32)]
            v_bd = jnp.concatenate(v_rows, axis=0).astype(BF16)
        else:
            v_bd = jnp.concatenate([jnp.where(v_lane_head == hh, v_p, 0.0) for hh in range(2)], axis=0)
        e["s_upd"] = [_dot(e["kh_t"][hh * GLA_DK:(hh + 1) * GLA_DK], v_p[:, hh * GLA_DV:(hh + 1) * GLA_DV])
                      for hh in range(2)]
        e.update(scores=a_p.astype(BF16), v_bd=v_bd)
    yield
    for bp in pairs:
        e = gl[bp]
        e["o_intra"] = _dot(e["scores"], e["v_bd"])

    yield
    ys = [[None] * (ML_HEADS + GLA_HEADS) for _ in range(nchunk)]

    def finish_mlstm(b):
        st = prev(b)
        for h in range(ML_HEADS):
            d = ml[(b, h)]
            mt, w_inter = d["mt"], d["w_inter"]
            qn = d["qn"] if known(b) else jnp.sum(d["qf"] * st["n"][h], axis=1, keepdims=True)
            den = d["den"] + w_inter * qn
            hh = (d["sv"] + w_inter * carried[(b, h)]) / jnp.maximum(jnp.abs(den), jnp.exp2(-mt))
            i_last = w_inter[L - 1:L]
            new_states[b]["c"][h] = i_last * st["c"][h] + d["c_upd"]
            new_states[b]["n"][h] = i_last * st["n"][h] + jnp.sum(d["kw"], axis=0, keepdims=True)
            hs = slice(h * ML_DV, (h + 1) * ML_DV)
            yn = hh * lax.rsqrt(jnp.mean(hh * hh, axis=-1, keepdims=True) + EPS) * gml[:, hs]
            ys[b][h] = jax.nn.sigmoid(zgs[b][:, ZG_MO + h * ML_DV:ZG_MO + (h + 1) * ML_DV]) * yn

    def finish_gla(b):
        st = prev(b)
        for p in range(GLA_PAIRS):
            e = gl[(b, p)]
            o = carried[(b, "gla", p)] + e["o_intra"]
            for hh in range(2):
                h = 2 * p + hh
                ds = slice(hh * GLA_DK, (hh + 1) * GLA_DK)
                vs = slice(hh * GLA_DV, (hh + 1) * GLA_DV)
                new_states[b]["s"][h] = e["decay_col"][ds] * st["s"][h] + e["s_upd"][hh]
                oh = o[:, vs]
                yn = oh * lax.rsqrt(jnp.mean(oh * oh, axis=-1, keepdims=True) + EPS) * ggl[:, h * GLA_DV:(h + 1) * GLA_DV]
                gg = zgs[b][:, ZG_GG + h * GLA_DV:ZG_GG + (h + 1) * GLA_DV]
                ys[b][ML_HEADS + h] = (gg * jax.nn.sigmoid(gg)) * yn

    if chain:
        for b in range(nchunk):
            if not known(b):
                for h in range(ML_HEADS):
                    carried_mlstm(b, h)
                for p in range(GLA_PAIRS):
                    carried_gla(b, p)
            finish_mlstm(b)
            finish_gla(b)
    else:
        for b in range(nchunk):
            finish_mlstm(b)
        for b in range(nchunk):
            finish_gla(b)
    return [jnp.concatenate(y, axis=1) for y in ys], new_states


def _drain(gen):
    try:
        while True:
            next(gen)
    except StopIteration as done:
        return done.value


def _mixer_compute(*args, **kwargs):
    return _drain(_mixer_stages(*args, **kwargs))


def _mixer_kernel(*refs, L, t_real, has_state, n_steps, nblk):
    it = iter(refs)
    zq_ref, zg_ref, bias_ref, wa2_ref, ba_ref, gml_ref, ggl_ref, ee_ref = (next(it) for _ in range(8))
    if has_state:
        c0_ref, n0_ref, m0_ref, s0_ref = (next(it) for _ in range(4))
    y_ref, co_ref, no_ref, mo_ref, so_ref = (next(it) for _ in range(5))
    consts = (bias_ref[...], wa2_ref[...], ba_ref[...], gml_ref[...], ggl_ref[...])
    lane = lax.broadcasted_iota(jnp.int32, (1, LANES), 1)

    def emit_state(b, st):
        for h in range(ML_HEADS):
            co_ref[b, h] = st["c"][h]
            so_ref[b, h] = st["s"][h]
        no_ref[b] = jnp.concatenate(st["n"], axis=0)
        m_row = jnp.zeros((1, LANES), F32)
        for h in range(ML_HEADS):
            m_row = jnp.where(lane == h, st["m"][h], m_row)
        mo_ref[b] = m_row[:, 0:ML_HEADS]

    if has_state:
        states = []
        for b in range(nblk):
            n_all = n0_ref[b]
            m_all = m0_ref[b]
            states.append(dict(c=[c0_ref[b, h] for h in range(ML_HEADS)],
                               n=[n_all[h:h + 1, :] for h in range(ML_HEADS)],
                               m=[m_all[:, h:h + 1] for h in range(ML_HEADS)],
                               s=[s0_ref[b, h] for h in range(GLA_HEADS)]))
        ys, new_states = _mixer_compute([zq_ref[b] for b in range(nblk)], [zg_ref[b] for b in range(nblk)], states,
                                        consts, ee_ref, L=L, t_real=t_real, chain=False)
        for b in range(nblk):
            y_ref[b] = ys[b].astype(y_ref.dtype)
            emit_state(b, new_states[b])
        return

    c_s, n_s, m_s, s_s = (next(it) for _ in range(4))
    step = pl.program_id(1)

    @pl.when(step == 0)
    def _init():
        c_s[...] = jnp.zeros_like(c_s)
        n_s[...] = jnp.zeros_like(n_s)
        m_s[...] = jnp.zeros_like(m_s)
        s_s[...] = jnp.zeros_like(s_s)

    rows = [slice(k * L, (k + 1) * L) for k in range(nblk)]
    state = dict(c=[c_s[h] for h in range(ML_HEADS)], n=[n_s[h:h + 1, :] for h in range(ML_HEADS)],
                 m=[m_s[h:h + 1, 0:1] for h in range(ML_HEADS)], s=[s_s[h] for h in range(GLA_HEADS)])
    ys, new_states = _mixer_compute([zq_ref[0, r, :] for r in rows], [zg_ref[0, r, :] for r in rows], [state],
                                    consts, ee_ref, L=L, t_real=t_real, chain=True)
    for r, y in zip(rows, ys):
        y_ref[0, r, :] = y.astype(y_ref.dtype)
    last = new_states[-1]
    for h in range(ML_HEADS):
        c_s[h] = last["c"][h]
        n_s[h:h + 1, :] = last["n"][h]
        m_s[h:h + 1, :] = jnp.broadcast_to(last["m"][h], (1, LANES))
        s_s[h] = last["s"][h]
    pl.when(step == n_steps - 1)(lambda: emit_state(0, last))


def _mixer(zq, zg, w, L, t_real, state, y_dtype, nblk):
    bsz, t = zq.shape[0], zq.shape[1]
    has_state = state is not None
    consts = [_const_spec((1, SMALL_W)), _const_spec((SMALL_W, GLA_QK_W)), _const_spec((1, GLA_QK_W)),
              _const_spec((1, ML_HEADS * ML_DV)), _const_spec((1, GLA_V_W)), _const_spec((GLA_SUB, LANES, LANES))]
    args = [zq, zg, w["gate_bias"], w["w_a2"], w["b_a"], w["mlstm_out_g"], w["gla_out_g"], w["diag_sum"]]
    if has_state:
        assert t == L
        bb, n_steps, scratch = nblk, 1, []
        seq = lambda width: pl.BlockSpec((bb, L, width), lambda b, j: (b, 0, 0))
    else:
        assert t % (nblk * L) == 0 and t_real == t
        bb, n_steps = 1, t // (nblk * L)
        seq = lambda width: pl.BlockSpec((1, nblk * L, width), lambda b, j: (b, j, 0))
        scratch = [pltpu.VMEM((ML_HEADS, ML_DK, ML_DV), F32), pltpu.VMEM((SUBLANES, LANES), F32),
                   pltpu.VMEM((SUBLANES, LANES), F32), pltpu.VMEM((GLA_HEADS, GLA_DK, GLA_DV), F32)]
    per_b = lambda *tail: pl.BlockSpec((bb,) + tail, lambda b, j: (b,) + (0,) * len(tail))
    state_specs = [per_b(ML_HEADS, ML_DK, ML_DV), per_b(ML_HEADS, ML_DK), per_b(1, ML_HEADS),
                   per_b(GLA_HEADS, GLA_DK, GLA_DV)]
    in_specs = [seq(ZQ_W), seq(ZG_W)] + consts
    if has_state:
        in_specs += state_specs
        args += list(state)
    out_shape = [jax.ShapeDtypeStruct((bsz, t, D_MODEL), y_dtype),
                 jax.ShapeDtypeStruct((bsz, ML_HEADS, ML_DK, ML_DV), F32),
                 jax.ShapeDtypeStruct((bsz, ML_HEADS, ML_DK), F32),
                 jax.ShapeDtypeStruct((bsz, 1, ML_HEADS), F32),
                 jax.ShapeDtypeStruct((bsz, GLA_HEADS, GLA_DK, GLA_DV), F32)]
    kern = functools.partial(_mixer_kernel, L=L, t_real=t_real, has_state=has_state, n_steps=n_steps, nblk=nblk)
    return pl.pallas_call(
        kern,
        grid=(bsz // bb, n_steps),
        in_specs=in_specs,
        out_specs=[seq(D_MODEL)] + state_specs,
        out_shape=out_shape,
        scratch_shapes=scratch,
        compiler_params=_params(("arbitrary", "arbitrary")),
        name="mixer_state" if has_state else "mixer",
    )(*args)


def _ffn_in_pieces(x, g1_ref, wg_ref, wu_ref, wd_ref, gm_ref, wq_ref, wgt_ref, x1_ref, zq_out, zg_out):
    h = _rms(x, g1_ref[...]).astype(BF16)
    acts = []
    for lo in range(0, D_FF, MXU_WIDTH):
        g = _dot(h, wg_ref[:, lo:lo + MXU_WIDTH])
        yield
        u = _dot(h, wu_ref[:, lo:lo + MXU_WIDTH])
        yield
        acts.append(((g * jax.nn.sigmoid(g)) * u).astype(BF16))
    a = jnp.concatenate(acts, axis=1)
    down = []
    for lo in range(0, D_MODEL, MXU_WIDTH):
        down.append(_dot(a, wd_ref[:, lo:lo + MXU_WIDTH]))
        yield
    x1 = x + 0.5 * jnp.concatenate(down, axis=1)
    x1_ref[...] = x1
    hm = _rms(x1, gm_ref[...]).astype(BF16)
    for lo in range(0, ZQ_W, MXU_WIDTH):
        zq_out[:, lo:lo + MXU_WIDTH] = _dot(hm, wq_ref[:, lo:lo + MXU_WIDTH]).astype(zq_out.dtype)
        yield
    for lo in range(0, ZG_W, MXU_WIDTH):
        hi = min(lo + MXU_WIDTH, ZG_W)
        zg_out[:, lo:hi] = _dot(hm, wgt_ref[:, lo:hi])
        yield


FFN_PIECES_PER_MIXER_STAGE = 1


def _ffn_mix_kernel(x_ref, g1_ref, wg_ref, wu_ref, wd_ref, gm_ref, wq_ref, wgt_ref, bias_ref, wa2_ref, ba_ref,
                    gml_ref, ggl_ref, ee_ref, x1_ref, ym_ref, co_ref, no_ref, mo_ref, so_ref,
                    zq_s, zg_s, c_s, n_s, m_s, s_s, *, tiles_per_batch):
    i = pl.program_id(0)

    @pl.when(i == 0)
    def _init():
        for ref in (zq_s, zg_s, c_s, n_s, m_s, s_s):
            ref[...] = jnp.zeros_like(ref)

    consts = (bias_ref[...], wa2_ref[...], ba_ref[...], gml_ref[...], ggl_ref[...])
    n_chunks = zq_s.shape[0] // CHUNK
    rows = [slice(k * CHUNK, (k + 1) * CHUNK) for k in range(n_chunks)]
    zqs = [zq_s[r, :] for r in rows]
    zgs = [zg_s[r, :] for r in rows]
    starts_batch = (i - 1) % tiles_per_batch == 0
    carry = lambda v: jnp.where(starts_batch, 0.0, v)
    state = dict(c=[carry(c_s[h]) for h in range(ML_HEADS)], n=[carry(n_s[h:h + 1, :]) for h in range(ML_HEADS)],
                 m=[carry(m_s[h:h + 1, 0:1]) for h in range(ML_HEADS)], s=[carry(s_s[h]) for h in range(GLA_HEADS)])

    def mixers():
        st = state
        for k in range(n_chunks):
            ys, new = yield from _mixer_stages([zqs[k]], [zgs[k]], [st], consts, ee_ref, L=CHUNK, t_real=CHUNK,
                                               chain=True)
            ym_ref[rows[k], :] = ys[0].astype(ym_ref.dtype)
            st = new[0]
            yield
        return st

    ffn = _ffn_in_pieces(x_ref[...], g1_ref, wg_ref, wu_ref, wd_ref, gm_ref, wq_ref, wgt_ref, x1_ref, zq_s, zg_s)
    mix = mixers()
    ffn_live = True
    while True:
        try:
            next(mix)
        except StopIteration as done:
            last = done.value
            break
        for _ in range(FFN_PIECES_PER_MIXER_STAGE):
            if ffn_live:
                try:
                    next(ffn)
                except StopIteration:
                    ffn_live = False
    if ffn_live:
        _drain(ffn)

    for h in range(ML_HEADS):
        c_s[h] = last["c"][h]
        n_s[h:h + 1, :] = last["n"][h]
        m_s[h:h + 1, :] = jnp.broadcast_to(last["m"][h], (1, LANES))
        s_s[h] = last["s"][h]

    @pl.when(jnp.logical_and(i >= 1, (i - 1) % tiles_per_batch == tiles_per_batch - 1))
    def _emit_state():
        lane = lax.broadcasted_iota(jnp.int32, (1, LANES), 1)
        for h in range(ML_HEADS):
            co_ref[0, h] = last["c"][h]
            so_ref[0, h] = last["s"][h]
        no_ref[0] = jnp.concatenate(last["n"], axis=0)
        m_row = jnp.zeros((1, LANES), F32)
        for h in range(ML_HEADS):
            m_row = jnp.where(lane == h, last["m"][h], m_row)
        mo_ref[0] = m_row[:, 0:ML_HEADS]


def _ffn_mix(x, w, tm):
    bsz, t, _ = x.shape
    tiles_per_batch = t // tm
    n_tiles = bsz * tiles_per_batch
    assert t % tm == 0 and tm % CHUNK == 0
    cur = lambda width: pl.BlockSpec((tm, width), lambda i: (jnp.minimum(i, n_tiles - 1), 0))
    prev_tile = lambda i: jnp.maximum(i - 1, 0)
    per_b = lambda *tail: pl.BlockSpec((1,) + tail, lambda i: (prev_tile(i) // tiles_per_batch,) + (0,) * len(tail))
    state_specs = [per_b(ML_HEADS, ML_DK, ML_DV), per_b(ML_HEADS, ML_DK), per_b(1, ML_HEADS),
                   per_b(GLA_HEADS, GLA_DK, GLA_DV)]
    x1, ym, c_new, n_new, m_new, s_new = pl.pallas_call(
        functools.partial(_ffn_mix_kernel, tiles_per_batch=tiles_per_batch),
        grid=(n_tiles + 1,),
        in_specs=[cur(D_MODEL), _const_spec((1, D_MODEL)), _const_spec((D_MODEL, D_FF)),
                  _const_spec((D_MODEL, D_FF)), _const_spec((D_FF, D_MODEL)), _const_spec((1, D_MODEL)),
                  _const_spec((D_MODEL, ZQ_W)), _const_spec((D_MODEL, ZG_W)), _const_spec((1, SMALL_W)),
                  _const_spec((SMALL_W, GLA_QK_W)), _const_spec((1, GLA_QK_W)), _const_spec((1, ML_HEADS * ML_DV)),
                  _const_spec((1, GLA_V_W)), _const_spec((GLA_SUB, LANES, LANES))],
        out_specs=[cur(D_MODEL), pl.BlockSpec((tm, D_MODEL), lambda i: (prev_tile(i), 0))] + state_specs,
        out_shape=[jax.ShapeDtypeStruct((n_tiles * tm, D_MODEL), F32),
                   jax.ShapeDtypeStruct((n_tiles * tm, D_MODEL), BF16),
                   jax.ShapeDtypeStruct((bsz, ML_HEADS, ML_DK, ML_DV), F32),
                   jax.ShapeDtypeStruct((bsz, ML_HEADS, ML_DK), F32),
                   jax.ShapeDtypeStruct((bsz, 1, ML_HEADS), F32),
                   jax.ShapeDtypeStruct((bsz, GLA_HEADS, GLA_DK, GLA_DV), F32)],
        scratch_shapes=[pltpu.VMEM((tm, ZQ_W), BF16), pltpu.VMEM((tm, ZG_W), F32),
                        pltpu.VMEM((ML_HEADS, ML_DK, ML_DV), F32), pltpu.VMEM((SUBLANES, LANES), F32),
                        pltpu.VMEM((SUBLANES, LANES), F32), pltpu.VMEM((GLA_HEADS, GLA_DK, GLA_DV), F32)],
        compiler_params=_params(("arbitrary",)),
        name="ffn_mix",
    )(x.reshape(bsz * t, D_MODEL), w["ffn1_g"], w["ffn1_wg"], w["ffn1_wu"], w["ffn1_wd"], w["mix_g"], w["w_in_q"],
      w["w_in_g"], w["gate_bias"], w["w_a2"], w["b_a"], w["mlstm_out_g"], w["gla_out_g"], w["diag_sum"])
    return x1, ym, (c_new[None], n_new[None], m_new.reshape(1, bsz, ML_HEADS), s_new[None])


def _post_mix_kernel(x1_ref, ym_ref, wout_ref, gx_ref, wq_ref, x2_ref, q_ref):
    x2 = x1_ref[...] + _dot(ym_ref[...].astype(BF16), wout_ref[...])
    x2_ref[...] = x2
    hq = _rms(x2, gx_ref[...]).astype(BF16)
    q_ref[...] = _dot(hq, wq_ref[...]).astype(q_ref.dtype)


def _post_mix(x1, ym, w, tm, q_dtype):
    n = x1.shape[0]
    row = lambda: pl.BlockSpec((tm, D_MODEL), lambda i: (i, 0))
    return pl.pallas_call(
        _post_mix_kernel,
        grid=(n // tm,),
        in_specs=[row(), row(), _const_spec((D_MODEL, D_MODEL)), _const_spec((1, D_MODEL)),
                  _const_spec((D_MODEL, D_MODEL))],
        out_specs=[row(), row()],
        out_shape=[jax.ShapeDtypeStruct((n, D_MODEL), F32), jax.ShapeDtypeStruct((n, D_MODEL), q_dtype)],
        compiler_params=_params(("arbitrary",)),
        name="post_mix",
    )(x1, ym, w["w_out"], w["xattn_g"], w["xattn_wq"])


def _softmax(s):
    e = jnp.exp(s - jnp.max(s, axis=-1, keepdims=True))
    return e / jnp.sum(e, axis=-1, keepdims=True)


def _post_fused_kernel(x1_ref, ym_ref, k_ref, v_ref, wout_ref, gx_ref, wq_ref, wo_ref, g2_ref, wg_ref, wu_ref,
                       wd_ref, gf_ref, y_ref):
    x2 = x1_ref[0] + _dot(ym_ref[0].astype(BF16), wout_ref[...])
    q = _dot(_rms(x2, gx_ref[...]).astype(BF16), wq_ref[...]).astype(BF16)
    heads = [slice(h * XA_DH, (h + 1) * XA_DH) for h in range(XA_HEADS)]
    k_full, v_full = _cache_rows_load(k_ref, 0), _cache_rows_load(v_ref, 0)
    scores = [_dot_nt(q[:, hs], k_full[:, hs]) * (XA_DH ** -0.5) for hs in heads]
    probs = [_softmax(s).astype(BF16) for s in scores]
    o = jnp.concatenate([_dot(p, v_full[:, hs]).astype(BF16) for hs, p in zip(heads, probs)], axis=1)
    x3 = x2 + _dot(o, wo_ref[...])
    x4 = _swiglu_residual(x3, g2_ref, wg_ref, wu_ref, wd_ref)
    y_ref[0] = _rms(x4, gf_ref[...])


def _post_fused(x1, ym, k, v, w, tm):
    bsz, t = x1.shape[0], x1.shape[1]
    row = lambda: pl.BlockSpec((1, tm, D_MODEL), lambda b, j: (b, j, 0))
    kv = pl.BlockSpec((1, N_MEM * CACHE_ROW_GROUP, LANES), lambda b, j: (b, 0, 0))
    sq = _const_spec((D_MODEL, D_MODEL))
    vec = _const_spec((1, D_MODEL))
    return pl.pallas_call(
        _post_fused_kernel,
        grid=(bsz, t // tm),
        in_specs=[row(), row(), kv, kv, sq, vec, sq, sq, vec, _const_spec((D_MODEL, D_FF)),
                  _const_spec((D_MODEL, D_FF)), _const_spec((D_FF, D_MODEL)), vec],
        out_specs=row(),
        out_shape=jax.ShapeDtypeStruct((bsz, t, D_MODEL), F32),
        compiler_params=_params(("arbitrary", "arbitrary")),
        name="post_fused",
    )(x1, ym, k, v, w["w_out"], w["xattn_g"], w["xattn_wq"], w["xattn_wo"], w["ffn2_g"], w["ffn2_wg"],
      w["ffn2_wu"], w["ffn2_wd"], w["final_g"])


XA_LANE_TILES = XA_DH // LANES
CACHE_ROW_GROUP = XA_LANE_TILES * XA_HEADS


def _cache_rows_view(x):
    bsz = x.shape[0]
    x = x.reshape(bsz, N_MEM, XA_HEADS, XA_LANE_TILES, LANES)
    return x.transpose(0, 1, 3, 2, 4).reshape(bsz, N_MEM * CACHE_ROW_GROUP, LANES)


def _cache_rows_unview(x):
    bsz = x.shape[0]
    x = x.reshape(bsz, N_MEM, XA_LANE_TILES, XA_HEADS, LANES)
    return x.transpose(0, 1, 3, 2, 4).reshape(bsz, N_MEM, XA_HEADS, XA_DH)


def _cache_rows_store(ref, b, x):
    for h in range(XA_HEADS):
        for lt in range(XA_LANE_TILES):
            lo = h * XA_DH + lt * LANES
            ref[b, pl.ds(lt * XA_HEADS + h, N_MEM, stride=CACHE_ROW_GROUP), :] = x[:, lo:lo + LANES]


def _cache_rows_load(ref, b):
    cols = [ref[b, pl.ds(lt * XA_HEADS + h, N_MEM, stride=CACHE_ROW_GROUP), :]
            for h in range(XA_HEADS) for lt in range(XA_LANE_TILES)]
    return jnp.concatenate(cols, axis=1).astype(BF16)


def _xattn_cache_kernel(q_ref, k_ref, v_ref, o_ref, *, bb, tq):
    lane_head = lax.broadcasted_iota(jnp.int32, (1, D_MODEL), 1) // XA_DH
    scores = []
    for b in range(bb):
        q = q_ref[b]
        q_bd = jnp.concatenate([jnp.where(lane_head == h, q, 0.0) for h in range(XA_HEADS)], axis=0)
        k_full = _cache_rows_load(k_ref, b)
        scores.append(_dot_nt(q_bd.astype(BF16), k_full) * (XA_DH ** -0.5))
    p_all = _softmax(jnp.concatenate(scores, axis=0)).astype(BF16)
    rows = XA_HEADS * tq
    for b in range(bb):
        v_full = _cache_rows_load(v_ref, b)
        o_full = _dot(p_all[b * rows:(b + 1) * rows], v_full)
        o = jnp.zeros((tq, D_MODEL), F32)
        for h in range(XA_HEADS):
            o = jnp.where(lane_head == h, o_full[h * tq:(h + 1) * tq], o)
        o_ref[b] = o.astype(o_ref.dtype)


def _xattn_cache(q, k, v, bb, o_dtype):
    bsz, tq = q.shape[0], q.shape[1]
    k, v = _cache_rows_view(k), _cache_rows_view(v)
    qo = pl.BlockSpec((bb, tq, D_MODEL), lambda b: (b, 0, 0))
    kv = pl.BlockSpec((bb, N_MEM * CACHE_ROW_GROUP, LANES), lambda b: (b, 0, 0))
    return pl.pallas_call(
        functools.partial(_xattn_cache_kernel, bb=bb, tq=tq),
        grid=(bsz // bb,),
        in_specs=[qo, kv, kv],
        out_specs=qo,
        out_shape=jax.ShapeDtypeStruct((bsz, tq, D_MODEL), o_dtype),
        compiler_params=_params(("arbitrary",)),
        name="xattn_cache",
    )(q, k, v)


def _ffn_out_kernel(x2_ref, o_ref, wo_ref, g2_ref, wg_ref, wu_ref, wd_ref, gf_ref, y_ref):
    x3 = x2_ref[...] + _dot(o_ref[...].astype(BF16), wo_ref[...])
    x4 = _swiglu_residual(x3, g2_ref, wg_ref, wu_ref, wd_ref)
    y_ref[...] = _rms(x4, gf_ref[...])


def _ffn_out(x2, o, w, tm):
    n = x2.shape[0]
    row = lambda: pl.BlockSpec((tm, D_MODEL), lambda i: (i, 0))
    return pl.pallas_call(
        _ffn_out_kernel,
        grid=(n // tm,),
        in_specs=[row(), row(), _const_spec((D_MODEL, D_MODEL)), _const_spec((1, D_MODEL)),
                  _const_spec((D_MODEL, D_FF)), _const_spec((D_MODEL, D_FF)), _const_spec((D_FF, D_MODEL)),
                  _const_spec((1, D_MODEL))],
        out_specs=row(),
        out_shape=jax.ShapeDtypeStruct((n, D_MODEL), F32),
        compiler_params=_params(("arbitrary",)),
        name="ffn_out",
    )(x2, o, w["xattn_wo"], w["ffn2_g"], w["ffn2_wg"], w["ffn2_wu"], w["ffn2_wd"], w["final_g"])


def _memkv_kernel(m_ref, g_ref, wk_ref, wv_ref, k_ref, v_ref):
    hn = _rms(m_ref[0], g_ref[...]).astype(BF16)
    _cache_rows_store(k_ref, 0, _dot(hn, wk_ref[...]))
    _cache_rows_store(v_ref, 0, _dot(hn, wv_ref[...]))


def _memkv(mem, w):
    bsz = mem.shape[0]
    rows = pl.BlockSpec((1, N_MEM * CACHE_ROW_GROUP, LANES), lambda b: (b, 0, 0))
    return pl.pallas_call(
        _memkv_kernel,
        grid=(bsz,),
        in_specs=[pl.BlockSpec((1, N_MEM, D_MODEL), lambda b: (b, 0, 0)), _const_spec((1, D_MODEL)),
                  _const_spec((D_MODEL, D_MODEL)), _const_spec((D_MODEL, D_MODEL))],
        out_specs=[rows, rows],
        out_shape=[jax.ShapeDtypeStruct((bsz, N_MEM * CACHE_ROW_GROUP, LANES), F32)] * 2,
        compiler_params=_params(("arbitrary",)),
        name="memkv",
    )(mem, w["mem_g"], w["xattn_wk"], w["xattn_wv"])


def _prep_weights(p):
    bf = lambda a: a.astype(BF16)
    row = lambda a: a.reshape(1, -1).astype(F32)
    w_in = p["w_in"]
    off = {}
    pos = 0
    for name, width in (("mq", 512), ("mk", 512), ("mv", 512), ("mi", 4), ("mf", 4), ("mo", 512), ("gq", 256),
                        ("gk", 256), ("gv", 512), ("ga", 16), ("gg", 512)):
        off[name] = w_in[:, pos:pos + width]
        pos += width
    pad_cols = SMALL_W - 2 * ML_HEADS - GLA_RANK
    w_in_q = jnp.concatenate([off[k] for k in ("mq", "mk", "mv", "gq", "gk", "gv")], axis=1)
    w_in_g = jnp.concatenate([off["mo"], off["gg"], off["mi"], off["mf"], off["ga"],
                              jnp.zeros((D_MODEL, pad_cols), F32)], axis=1)
    gate_bias = jnp.concatenate([p["mlstm_b_i"], p["mlstm_b_f"], jnp.zeros((SMALL_W - 2 * ML_HEADS,), F32)])
    w_a2 = jnp.zeros((SMALL_W, GLA_QK_W), F32).at[2 * ML_HEADS:2 * ML_HEADS + GLA_RANK].set(p["gla_w_a2"])
    lane = jnp.arange(LANES)
    same_head = (lane[:, None] // GLA_DK) == (lane[None, :] // GLA_DK)
    diag_sum = jnp.stack([same_head & ((lane[None, :] % GLA_SUB) == j) for j in range(GLA_SUB)]).astype(BF16)
    return dict(
        ffn1_g=row(p["ffn1_norm_g"]), ffn1_wg=bf(p["ffn1_w_gate"]), ffn1_wu=bf(p["ffn1_w_up"]),
        ffn1_wd=bf(p["ffn1_w_down"]), mix_g=row(p["mix_norm_g"]), w_in_q=bf(w_in_q), w_in_g=bf(w_in_g),
        gate_bias=row(gate_bias), w_a2=bf(w_a2), b_a=row(p["gla_b_a"]), mlstm_out_g=row(p["mlstm_out_g"]),
        gla_out_g=row(p["gla_out_g"]), diag_sum=diag_sum,
        w_out=bf(p["w_out"]), xattn_g=row(p["xattn_norm_g"]), xattn_wq=bf(p["xattn_w_q"]),
        xattn_wo=bf(p["xattn_w_o"]), mem_g=row(p["mem_norm_g"]), xattn_wk=bf(p["xattn_w_k"]),
        xattn_wv=bf(p["xattn_w_v"]), ffn2_g=row(p["ffn2_norm_g"]), ffn2_wg=bf(p["ffn2_w_gate"]),
        ffn2_wu=bf(p["ffn2_w_up"]), ffn2_wd=bf(p["ffn2_w_down"]), final_g=row(p["final_g"]))


def _ffn_and_mixers(x, state, w, *, chunk, mixer_nblk, act_dtype):
    bsz, t, _ = x.shape
    x1, zq, zg = _ffn_in(x.reshape(bsz * t, D_MODEL), w, TM_FFN_IN, act_dtype)
    zq = zq.reshape(bsz, t, ZQ_W)
    zg = zg.reshape(bsz, t, ZG_W)
    pad = -t % chunk
    if pad:
        zq = jnp.pad(zq, ((0, 0), (0, pad), (0, 0)))
        zg = jnp.pad(zg, ((0, 0), (0, pad), (0, 0)))
    ym, c_new, n_new, m_new, s_new = _mixer(zq, zg, w, chunk, t, state, act_dtype, mixer_nblk)
    return x1, ym[:, :t], (c_new[None], n_new[None], m_new.reshape(1, bsz, ML_HEADS), s_new[None])


def kernel(x_prompt, x_sample, mem_prompt, cache_mem_k, cache_mem_v, state_mlstm_c, state_mlstm_n, state_mlstm_m, state_gla_s, ffn1_norm_g, ffn1_w_gate, ffn1_w_up, ffn1_w_down, mix_norm_g, w_in, mlstm_b_i, mlstm_b_f, mlstm_out_g, gla_w_a2, gla_b_a, gla_out_g, w_out, xattn_norm_g, mem_norm_g, xattn_w_q, xattn_w_k, xattn_w_v, xattn_w_o, ffn2_norm_g, ffn2_w_gate, ffn2_w_up, ffn2_w_down, final_norm_g):
    assert ffn1_norm_g.shape[0] == 1, "single-layer stack"
    layer = dict(ffn1_norm_g=ffn1_norm_g, ffn1_w_gate=ffn1_w_gate, ffn1_w_up=ffn1_w_up, ffn1_w_down=ffn1_w_down,
                 mix_norm_g=mix_norm_g, w_in=w_in, mlstm_b_i=mlstm_b_i, mlstm_b_f=mlstm_b_f,
                 mlstm_out_g=mlstm_out_g, gla_w_a2=gla_w_a2, gla_b_a=gla_b_a, gla_out_g=gla_out_g, w_out=w_out,
                 xattn_norm_g=xattn_norm_g, mem_norm_g=mem_norm_g, xattn_w_q=xattn_w_q, xattn_w_k=xattn_w_k,
                 xattn_w_v=xattn_w_v, xattn_w_o=xattn_w_o, ffn2_norm_g=ffn2_norm_g, ffn2_w_gate=ffn2_w_gate,
                 ffn2_w_up=ffn2_w_up, ffn2_w_down=ffn2_w_down)
    p = {name: arr[0] for name, arr in layer.items()}
    p["final_g"] = final_norm_g
    w = _prep_weights(p)

    bp, tp, _ = x_prompt.shape
    bs, ts, _ = x_sample.shape

    mem_k_p, mem_v_p = _memkv(mem_prompt, w)
    x1_p, ym_p, states_p = _ffn_mix(x_prompt, w, TM_FFN_IN)
    y_p = _post_fused(x1_p.reshape(bp, tp, D_MODEL), ym_p.reshape(bp, tp, D_MODEL), mem_k_p, mem_v_p, w, TM_POST)

    state = (state_mlstm_c[0], state_mlstm_n[0], state_mlstm_m[0].reshape(bs, 1, ML_HEADS), state_gla_s[0])
    x1_s, ym_s, states_s = _ffn_and_mixers(x_sample, state, w, chunk=SUBLANES, mixer_nblk=SAMPLE_MIXER_BATCHES,
                                           act_dtype=F32)
    x2_s, q_s = _post_mix(x1_s, ym_s.reshape(bs * ts, D_MODEL), w, TM_FFN_IN, F32)
    o_s = _xattn_cache(q_s.reshape(bs, ts, D_MODEL), cache_mem_k[0], cache_mem_v[0], SAMPLE_XATTN_BATCHES, F32)
    y_s = _ffn_out(x2_s, o_s.reshape(bs * ts, D_MODEL), w, TM_FFN_IN).reshape(bs, ts, D_MODEL)

    return (y_p, y_s, _cache_rows_unview(mem_k_p)[None], _cache_rows_unview(mem_v_p)[None]) + states_p + states_s
```

```python
import functools

import jax
import jax.numpy as jnp
from jax import lax
from jax.experimental import pallas as pl
from jax.experimental.pallas import tpu as pltpu

F32 = jnp.float32
BF16 = jnp.bfloat16

D_MODEL = 1024
D_FF = 2816
ML_HEADS = 4
ML_DK = 128
ML_DV = 128
GLA_HEADS = 4
GLA_DK = 64
GLA_DV = 128
GLA_RANK = 16
GLA_TAU = 16.0
N_MEM = 256
XA_HEADS = 4
XA_DH = D_MODEL // XA_HEADS
EPS = 1e-6
CHUNK = 64
LOG2_E = 1.4426950408889634
LN_2 = 0.6931471805599453
LANES = 128
SUBLANES = 8

ZQ_MQ, ZQ_MK, ZQ_MV = 0, 512, 1024
ZQ_GQ, ZQ_GK, ZQ_GV = 1536, 1792, 2048
ZQ_W = 2560
ZG_MO, ZG_GG, ZG_SMALL = 0, 512, 1024
ZG_W = 1152
SMALL_W = LANES
GLA_QK_W = GLA_HEADS * GLA_DK
GLA_V_W = GLA_HEADS * GLA_DV
GLA_PAIRS = GLA_HEADS // 2
GLA_SUB = SUBLANES

MXU_WIDTH = 256
FF_CHUNKS = ((0, 6 * MXU_WIDTH), (6 * MXU_WIDTH, D_FF))
VMEM_LIMIT_BYTES = 56 * 1024 * 1024

TM_FFN_IN = MXU_WIDTH
TM_POST = 2 * MXU_WIDTH
PROMPT_MIXER_CHUNKS = 1
SAMPLE_MIXER_BATCHES = 8
SAMPLE_XATTN_BATCHES = 4


def _rms(x, g):
    return x * lax.rsqrt(jnp.mean(x * x, axis=-1, keepdims=True) + EPS) * g


def _log_sigmoid(x):
    return jnp.minimum(x, 0.0) - jnp.log1p(jnp.exp(-jnp.abs(x)))


def _dot(a, b):
    return jnp.dot(a, b, preferred_element_type=F32)


def _dot_nt(a, b):
    return lax.dot_general(a, b, (((1,), (1,)), ((), ())), preferred_element_type=F32)


def _dot_f32(a, b):
    return jnp.dot(a, b, precision=lax.Precision.HIGHEST, preferred_element_type=F32)


def _swiglu_residual(x, g_ref, wg_ref, wu_ref, wd_ref):
    h = _rms(x, g_ref[...]).astype(BF16)
    acts = []
    for lo, hi in FF_CHUNKS:
        g = _dot(h, wg_ref[:, lo:hi])
        u = _dot(h, wu_ref[:, lo:hi])
        acts.append(((g * jax.nn.sigmoid(g)) * u).astype(BF16))
    acc = jnp.zeros_like(x)
    for (lo, hi), a in zip(FF_CHUNKS, acts):
        acc = acc + _dot(a, wd_ref[lo:hi, :])
    return x + 0.5 * acc


def _const_spec(shape):
    nd = len(shape)
    return pl.BlockSpec(shape, lambda *_: (0,) * nd, pipeline_mode=pl.Buffered(1))


def _params(sem):
    return pltpu.CompilerParams(dimension_semantics=sem, vmem_limit_bytes=VMEM_LIMIT_BYTES)


def _ffn_in_kernel(x_ref, g1_ref, wg_ref, wu_ref, wd_ref, gm_ref, wq_ref, wgt_ref, x1_ref, zq_ref, zg_ref):
    x1 = _swiglu_residual(x_ref[...], g1_ref, wg_ref, wu_ref, wd_ref)
    x1_ref[...] = x1
    hm = _rms(x1, gm_ref[...]).astype(BF16)
    zq_ref[...] = _dot(hm, wq_ref[...]).astype(zq_ref.dtype)
    zg_ref[...] = _dot(hm, wgt_ref[...])


def _ffn_in(x, w, tm, zq_dtype):
    n = x.shape[0]
    row = lambda width: pl.BlockSpec((tm, width), lambda i: (i, 0))
    return pl.pallas_call(
        _ffn_in_kernel,
        grid=(n // tm,),
        in_specs=[row(D_MODEL), _const_spec((1, D_MODEL)), _const_spec((D_MODEL, D_FF)),
                  _const_spec((D_MODEL, D_FF)), _const_spec((D_FF, D_MODEL)), _const_spec((1, D_MODEL)),
                  _const_spec((D_MODEL, ZQ_W)), _const_spec((D_MODEL, ZG_W))],
        out_specs=[row(D_MODEL), row(ZQ_W), row(ZG_W)],
        out_shape=[jax.ShapeDtypeStruct((n, D_MODEL), F32), jax.ShapeDtypeStruct((n, ZQ_W), zq_dtype),
                   jax.ShapeDtypeStruct((n, ZG_W), F32)],
        compiler_params=_params(("arbitrary",)),
        name="ffn_in",
    )(x, w["ffn1_g"], w["ffn1_wg"], w["ffn1_wu"], w["ffn1_wd"], w["mix_g"], w["w_in_q"], w["w_in_g"])


def _mixer_stages(zqs, zgs, states, consts, ee_ref, *, L, t_real, chain):
    bias, wa2, ba, gml, ggl = consts
    nchunk = len(zqs)
    groups = [(b, h) for b in range(nchunk) for h in range(ML_HEADS)]
    pairs = [(b, p) for b in range(nchunk) for p in range(GLA_PAIRS)]
    padded = t_real < L
    valid = lax.broadcasted_iota(jnp.int32, (L, 1), 0) < t_real
    rr = lax.broadcasted_iota(jnp.int32, (L, L), 0)
    cc = lax.broadcasted_iota(jnp.int32, (L, L), 1)
    tril = cc <= rr
    c = GLA_SUB
    nb = L // c
    lane = lax.broadcasted_iota(jnp.int32, (1, LANES), 1)
    lane_blk = (lane % GLA_DK) // c
    lane_head = lane // GLA_DK
    v_lane_head = lax.broadcasted_iota(jnp.int32, (1, 2 * GLA_DV), 1) // GLA_DV
    row_blk = (lax.broadcasted_iota(jnp.int32, (2 * L, 1), 0) % L) // c
    t_in = lax.broadcasted_iota(jnp.int32, (1, c, 1), 1)
    new_states = [dict(c=[None] * ML_HEADS, n=[None] * ML_HEADS, m=[None] * ML_HEADS, s=[None] * GLA_HEADS)
                  for _ in range(nchunk)]
    prev = lambda b: new_states[b - 1] if chain and b > 0 else states[0 if chain else b]
    known = lambda b: not chain or b == 0

    carried = {}

    def carried_mlstm(b, h):
        st = prev(b)
        d = ml[(b, h)]
        carried[(b, h)] = _dot(d["qb"], st["c"][h].astype(BF16))

    def carried_gla(b, p):
        s_prev = prev(b)["s"]
        zero_blk = jnp.zeros((GLA_DK, GLA_DV), BF16)
        s_bd = jnp.concatenate(
            [jnp.concatenate([s_prev[2 * p].astype(BF16), zero_blk], axis=1),
             jnp.concatenate([zero_blk, s_prev[2 * p + 1].astype(BF16)], axis=1)], axis=0)
        carried[(b, "gla", p)] = _dot(gl[(b, p)]["q_dec"], s_bd)

    smalls, sms, lfs, b_cols, b_rows, sm_ts, las, bcs = [], [], [], [], [], [], [], []
    for b in range(nchunk):
        small = zgs[b][:, ZG_SMALL:ZG_SMALL + SMALL_W]
        sm = small + bias
        lf = _log_sigmoid(sm) * LOG2_E
        sm = sm * LOG2_E
        if padded:
            sm = jnp.where(valid, sm, -jnp.inf)
            lf = jnp.where(valid, lf, 0.0)
        smalls.append(small)
        sms.append(sm)
        lfs.append(lf)

    long_chunk = L > SUBLANES
    col_w = 1 if long_chunk else LANES
    tril_f, triu_f = tril.astype(F32), (rr <= cc).astype(F32)

    def cumsum_rows(x):
        if long_chunk:
            return _dot_f32(tril_f, x)
        acc = x[0:1]
        rows = [acc]
        for r in range(1, L):
            acc = acc + x[r:r + 1]
            rows.append(acc)
        return jnp.concatenate(rows, axis=0)

    lf_ts = [lf.T[0:SUBLANES] for lf in lfs] if long_chunk else None
    yield
    for b in range(nchunk):
        b_cols.append(cumsum_rows(lfs[b]))
        b_rows.append(_dot_f32(lf_ts[b], triu_f) if long_chunk else b_cols[b].T[0:SUBLANES])
        sm_ts.append(sms[b].T)
        la = _log_sigmoid(_dot(smalls[b].astype(BF16), wa2) + ba) * (LOG2_E / GLA_TAU)
        las.append(jnp.where(valid, la, 0.0) if padded else la)

    ml = {}
    for g in groups:
        b, h = g
        zq = zqs[b]
        qf = zq[:, ZQ_MQ + h * ML_DK:ZQ_MQ + (h + 1) * ML_DK].astype(F32)
        kf = zq[:, ZQ_MK + h * ML_DK:ZQ_MK + (h + 1) * ML_DK].astype(F32) * (ML_DK ** -0.5)
        vf = zq[:, ZQ_MV + h * ML_DV:ZQ_MV + (h + 1) * ML_DV].astype(F32)
        if padded:
            kf = jnp.where(valid, kf, 0.0)
            vf = jnp.where(valid, vf, 0.0)
        ml[g] = dict(qf=qf, qb=qf.astype(BF16), kf=kf, kb=kf.astype(BF16), vb=vf.astype(BF16))
    yield
    for b in range(nchunk):
        bcs.append(cumsum_rows(las[b]))
    for g in groups:
        d = ml[g]
        d["qk"] = _dot_nt(d["qb"], d["kb"])
        if known(g[0]):
            carried_mlstm(*g)

    yield
    for g in groups:
        b, h = g
        d = ml[g]
        i_col = jnp.broadcast_to(sms[b][:, h:h + 1], (L, col_w))
        b_col = jnp.broadcast_to(b_cols[b][:, ML_HEADS + h:ML_HEADS + h + 1], (L, col_w))
        b_row = b_rows[b][ML_HEADS + h:ML_HEADS + h + 1, :]
        i_row = sm_ts[b][h:h + 1, :]
        a_col = b_col + prev(b)["m"][h] * LOG2_E
        dm = jnp.where(tril, b_col[:, :L] - (b_row - i_row), -jnp.inf)
        mt = jnp.maximum(a_col, jnp.max(dm, axis=1, keepdims=True))
        w_inter = jnp.exp2(a_col - mt)
        s = d["qk"] * jnp.exp2(dm - mt[:, :L])
        kw = d["kf"] * jnp.exp2((b_col[L - 1:L] - mt[L - 1:L]) - (b_col - i_col))
        d.update(mt=mt, w_inter=w_inter, s=s, kw=kw, kw_t=kw.T.astype(BF16))
        new_states[b]["m"][h] = mt[L - 1:L, 0:1] * LN_2

    gl = {}
    for b in range(nchunk):
        zq = zqs[b]
        gq = zq[:, ZQ_GQ:ZQ_GQ + GLA_QK_W].astype(F32) * (GLA_DK ** -0.5)
        gk = zq[:, ZQ_GK:ZQ_GK + GLA_QK_W].astype(F32)
        gv = zq[:, ZQ_GV:ZQ_GV + GLA_V_W].astype(F32)
        if padded:
            gk = jnp.where(valid, gk, 0.0)
            gv = jnp.where(valid, gv, 0.0)
        stack = lambda x: jnp.concatenate([x[:, :LANES], x[:, LANES:]], axis=0)
        q2, k2, b2 = stack(gq), stack(gk), stack(bcs[b])
        q3 = q2.reshape(2 * nb, c, LANES)
        k3 = k2.reshape(2 * nb, c, LANES)
        b3 = b2.reshape(2 * nb, c, LANES)
        pair_terms = []
        for j in range(min(c, t_real)):
            decay = jnp.exp2(jnp.where(t_in >= j, b3 - b3[:, j:j + 1, :], -jnp.inf))
            pair_terms.append((q3 * k3[:, j:j + 1, :] * decay).reshape(2 * L, LANES).astype(BF16))
        kt2 = (k3 * jnp.exp2(b3[:, c - 1:c, :] - b3)).reshape(2 * L, LANES) if nb > 1 else None
        gl[b] = dict(gv=gv, q2=q2, k2=k2, b2=b2, pair_terms=pair_terms, kt2=kt2)
    yield
    for b in range(nchunk):
        acc = jnp.zeros((2 * L, LANES), F32)
        for j, pair_j in enumerate(gl[b]["pair_terms"]):
            acc = acc + _dot(pair_j, ee_ref[j])
        gl[b]["a_diag"] = jnp.where(lane_blk == row_blk, acc, 0.0)

    for bp in pairs:
        b, p = bp
        d = gl[b]
        rows_p = slice(p * L, (p + 1) * L)
        q_p, k_p, b_p = d["q2"][rows_p], d["k2"][rows_p], d["b2"][rows_p]
        e = dict(q_dec=(q_p * jnp.exp2(b_p)).astype(BF16),
                 kh_t=(k_p * jnp.exp2(b_p[L - 1:L] - b_p)).T.astype(BF16),
                 decay_col=jnp.exp2(b_p[L - SUBLANES:L].T[:, SUBLANES - 1:SUBLANES]))
        if nb > 1:
            kt_p = d["kt2"][rows_p].astype(BF16)
            k_bd = jnp.concatenate([jnp.where(lane_head == hh, kt_p, 0.0) for hh in range(2)], axis=0)
            slabs, offs = [], []
            off = 0
            for j in range(nb - 1):
                lo = (j + 1) * c
                slabs.append(q_p[lo:] * jnp.exp2(b_p[lo:] - b_p[lo - 1:lo]))
                offs.append(off)
                off += L - lo
            e.update(q_var=jnp.concatenate(slabs, axis=0).astype(BF16), k_bd=k_bd, offs=offs)
        gl[bp] = e
    yield
    for bp in pairs:
        e = gl[bp]
        if nb > 1:
            e["r"] = _dot_nt(e["q_var"], e["k_bd"])
        if known(bp[0]):
            carried_gla(*bp)

    for g in groups:
        d = ml[g]
        d["sv"] = _dot(d["s"].astype(BF16), d["vb"])
        d["c_upd"] = _dot(d["kw_t"], d["vb"])
        d["den"] = jnp.sum(d["s"], axis=1, keepdims=True)
        if known(g[0]):
            d["qn"] = jnp.sum(d["qf"] * prev(g[0])["n"][g[1]], axis=1, keepdims=True)
    yield
    for bp in pairs:
        b, p = bp
        d, e = gl[b], gl[bp]
        a_p = d["a_diag"][p * L:(p + 1) * L]
        if nb > 1:
            blocks = []
            for i in range(nb):
                blk = a_p[i * c:(i + 1) * c]
                for j in range(i):
                    lo_r = e["offs"][j] + (i - j - 1) * c
                    blk = jnp.where(lane_blk == j, e["r"][lo_r:lo_r + c], blk)
                blocks.append(blk)
            a_p = jnp.concatenate(blocks, axis=0)
        v_f = d["gv"][:, p * 2 * GLA_DV:(p + 1) * 2 * GLA_DV]
        v_p = v_f.astype(BF16)
        if L < GLA_DK:
            v_rows = []
            for hh in range(2):
                v_rows += [jnp.where(v_lane_head == hh, v_f, 0.0), jnp.zeros((GLA_DK - L, 2 * GLA_DV), F32)]
            v_bd = jnp.concatenate(v_rows, axis=0).astype(BF16)
        else:
            v_bd = jnp.concatenate([jnp.where(v_lane_head == hh, v_p, 0.0) for hh in range(2)], axis=0)
        e["s_upd"] = [_dot(e["kh_t"][hh * GLA_DK:(hh + 1) * GLA_DK], v_p[:, hh * GLA_DV:(hh + 1) * GLA_DV])
                      for hh in range(2)]
        e.update(scores=a_p.astype(BF16), v_bd=v_bd)
    yield
    for bp in pairs:
        e = gl[bp]
        e["o_intra"] = _dot(e["scores"], e["v_bd"])

    yield
    ys = [[None] * (ML_HEADS + GLA_HEADS) for _ in range(nchunk)]

    def finish_mlstm(b):
        st = prev(b)
        for h in range(ML_HEADS):
            d = ml[(b, h)]
            mt, w_inter = d["mt"], d["w_inter"]
            qn = d["qn"] if known(b) else jnp.sum(d["qf"] * st["n"][h], axis=1, keepdims=True)
            den = d["den"] + w_inter * qn
            hh = (d["sv"] + w_inter * carried[(b, h)]) / jnp.maximum(jnp.abs(den), jnp.exp2(-mt))
            i_last = w_inter[L - 1:L]
            new_states[b]["c"][h] = i_last * st["c"][h] + d["c_upd"]
            new_states[b]["n"][h] = i_last * st["n"][h] + jnp.sum(d["kw"], axis=0, keepdims=True)
            hs = slice(h * ML_DV, (h + 1) * ML_DV)
            yn = hh * lax.rsqrt(jnp.mean(hh * hh, axis=-1, keepdims=True) + EPS) * gml[:, hs]
            ys[b][h] = jax.nn.sigmoid(zgs[b][:, ZG_MO + h * ML_DV:ZG_MO + (h + 1) * ML_DV]) * yn

    def finish_gla(b):
        st = prev(b)
        for p in range(GLA_PAIRS):
            e = gl[(b, p)]
            o = carried[(b, "gla", p)] + e["o_intra"]
            for hh in range(2):
                h = 2 * p + hh
                ds = slice(hh * GLA_DK, (hh + 1) * GLA_DK)
                vs = slice(hh * GLA_DV, (hh + 1) * GLA_DV)
                new_states[b]["s"][h] = e["decay_col"][ds] * st["s"][h] + e["s_upd"][hh]
                oh = o[:, vs]
                yn = oh * lax.rsqrt(jnp.mean(oh * oh, axis=-1, keepdims=True) + EPS) * ggl[:, h * GLA_DV:(h + 1) * GLA_DV]
                gg = zgs[b][:, ZG_GG + h * GLA_DV:ZG_GG + (h + 1) * GLA_DV]
                ys[b][ML_HEADS + h] = (gg * jax.nn.sigmoid(gg)) * yn

    if chain:
        for b in range(nchunk):
            if not known(b):
                for h in range(ML_HEADS):
                    carried_mlstm(b, h)
                for p in range(GLA_PAIRS):
                    carried_gla(b, p)
            finish_mlstm(b)
            finish_gla(b)
    else:
        for b in range(nchunk):
            finish_mlstm(b)
        for b in range(nchunk):
            finish_gla(b)
    return [jnp.concatenate(y, axis=1) for y in ys], new_states


def _drain(gen):
    try:
        while True:
            next(gen)
    except StopIteration as done:
        return done.value


def _mixer_compute(*args, **kwargs):
    return _drain(_mixer_stages(*args, **kwargs))


def _mixer_kernel(*refs, L, t_real, has_state, n_steps, nblk):
    it = iter(refs)
    zq_ref, zg_ref, bias_ref, wa2_ref, ba_ref, gml_ref, ggl_ref, ee_ref = (next(it) for _ in range(8))
    if has_state:
        c0_ref, n0_ref, m0_ref, s0_ref = (next(it) for _ in range(4))
    y_ref, co_ref, no_ref, mo_ref, so_ref = (next(it) for _ in range(5))
    consts = (bias_ref[...], wa2_ref[...], ba_ref[...], gml_ref[...], ggl_ref[...])
    lane = lax.broadcasted_iota(jnp.int32, (1, LANES), 1)

    def emit_state(b, st):
        for h in range(ML_HEADS):
            co_ref[b, h] = st["c"][h]
            so_ref[b, h] = st["s"][h]
        no_ref[b] = jnp.concatenate(st["n"], axis=0)
        m_row = jnp.zeros((1, LANES), F32)
        for h in range(ML_HEADS):
            m_row = jnp.where(lane == h, st["m"][h], m_row)
        mo_ref[b] = m_row[:, 0:ML_HEADS]

    if has_state:
        states = []
        for b in range(nblk):
            n_all = n0_ref[b]
            m_all = m0_ref[b]
            states.append(dict(c=[c0_ref[b, h] for h in range(ML_HEADS)],
                               n=[n_all[h:h + 1, :] for h in range(ML_HEADS)],
                               m=[m_all[:, h:h + 1] for h in range(ML_HEADS)],
                               s=[s0_ref[b, h] for h in range(GLA_HEADS)]))
        ys, new_states = _mixer_compute([zq_ref[b] for b in range(nblk)], [zg_ref[b] for b in range(nblk)], states,
                                        consts, ee_ref, L=L, t_real=t_real, chain=False)
        for b in range(nblk):
            y_ref[b] = ys[b].astype(y_ref.dtype)
            emit_state(b, new_states[b])
        return

    c_s, n_s, m_s, s_s = (next(it) for _ in range(4))
    step = pl.program_id(1)

    @pl.when(step == 0)
    def _init():
        c_s[...] = jnp.zeros_like(c_s)
        n_s[...] = jnp.zeros_like(n_s)
        m_s[...] = jnp.zeros_like(m_s)
        s_s[...] = jnp.zeros_like(s_s)

    rows = [slice(k * L, (k + 1) * L) for k in range(nblk)]
    state = dict(c=[c_s[h] for h in range(ML_HEADS)], n=[n_s[h:h + 1, :] for h in range(ML_HEADS)],
                 m=[m_s[h:h + 1, 0:1] for h in range(ML_HEADS)], s=[s_s[h] for h in range(GLA_HEADS)])
    ys, new_states = _mixer_compute([zq_ref[0, r, :] for r in rows], [zg_ref[0, r, :] for r in rows], [state],
                                    consts, ee_ref, L=L, t_real=t_real, chain=True)
    for r, y in zip(rows, ys):
        y_ref[0, r, :] = y.astype(y_ref.dtype)
    last = new_states[-1]
    for h in range(ML_HEADS):
        c_s[h] = last["c"][h]
        n_s[h:h + 1, :] = last["n"][h]
        m_s[h:h + 1, :] = jnp.broadcast_to(last["m"][h], (1, LANES))
        s_s[h] = last["s"][h]
    pl.when(step == n_steps - 1)(lambda: emit_state(0, last))


def _mixer(zq, zg, w, L, t_real, state, y_dtype, nblk):
    bsz, t = zq.shape[0], zq.shape[1]
    has_state = state is not None
    consts = [_const_spec((1, SMALL_W)), _const_spec((SMALL_W, GLA_QK_W)), _const_spec((1, GLA_QK_W)),
              _const_spec((1, ML_HEADS * ML_DV)), _const_spec((1, GLA_V_W)), _const_spec((GLA_SUB, LANES, LANES))]
    args = [zq, zg, w["gate_bias"], w["w_a2"], w["b_a"], w["mlstm_out_g"], w["gla_out_g"], w["diag_sum"]]
    if has_state:
        assert t == L
        bb, n_steps, scratch = nblk, 1, []
        seq = lambda width: pl.BlockSpec((bb, L, width), lambda b, j: (b, 0, 0))
    else:
        assert t % (nblk * L) == 0 and t_real == t
        bb, n_steps = 1, t // (nblk * L)
        seq = lambda width: pl.BlockSpec((1, nblk * L, width), lambda b, j: (b, j, 0))
        scratch = [pltpu.VMEM((ML_HEADS, ML_DK, ML_DV), F32), pltpu.VMEM((SUBLANES, LANES), F32),
                   pltpu.VMEM((SUBLANES, LANES), F32), pltpu.VMEM((GLA_HEADS, GLA_DK, GLA_DV), F32)]
    per_b = lambda *tail: pl.BlockSpec((bb,) + tail, lambda b, j: (b,) + (0,) * len(tail))
    state_specs = [per_b(ML_HEADS, ML_DK, ML_DV), per_b(ML_HEADS, ML_DK), per_b(1, ML_HEADS),
                   per_b(GLA_HEADS, GLA_DK, GLA_DV)]
    in_specs = [seq(ZQ_W), seq(ZG_W)] + consts
    if has_state:
        in_specs += state_specs
        args += list(state)
    out_shape = [jax.ShapeDtypeStruct((bsz, t, D_MODEL), y_dtype),
                 jax.ShapeDtypeStruct((bsz, ML_HEADS, ML_DK, ML_DV), F32),
                 jax.ShapeDtypeStruct((bsz, ML_HEADS, ML_DK), F32),
                 jax.ShapeDtypeStruct((bsz, 1, ML_HEADS), F32),
                 jax.ShapeDtypeStruct((bsz, GLA_HEADS, GLA_DK, GLA_DV), F32)]
    kern = functools.partial(_mixer_kernel, L=L, t_real=t_real, has_state=has_state, n_steps=n_steps, nblk=nblk)
    return pl.pallas_call(
        kern,
        grid=(bsz // bb, n_steps),
        in_specs=in_specs,
        out_specs=[seq(D_MODEL)] + state_specs,
        out_shape=out_shape,
        scratch_shapes=scratch,
        compiler_params=_params(("arbitrary", "arbitrary")),
        name="mixer_state" if has_state else "mixer",
    )(*args)


def _ffn_in_pieces(x, g1_ref, wg_ref, wu_ref, wd_ref, gm_ref, wq_ref, wgt_ref, x1_ref, zq_out, zg_out):
    h = _rms(x, g1_ref[...]).astype(BF16)
    acts = []
    for lo in range(0, D_FF, MXU_WIDTH):
        g = _dot(h, wg_ref[:, lo:lo + MXU_WIDTH])
        yield
        u = _dot(h, wu_ref[:, lo:lo + MXU_WIDTH])
        yield
        acts.append(((g * jax.nn.sigmoid(g)) * u).astype(BF16))
    a = jnp.concatenate(acts, axis=1)
    down = []
    for lo in range(0, D_MODEL, MXU_WIDTH):
        down.append(_dot(a, wd_ref[:, lo:lo + MXU_WIDTH]))
        yield
    x1 = x + 0.5 * jnp.concatenate(down, axis=1)
    x1_ref[...] = x1
    hm = _rms(x1, gm_ref[...]).astype(BF16)
    for lo in range(0, ZQ_W, MXU_WIDTH):
        zq_out[:, lo:lo + MXU_WIDTH] = _dot(hm, wq_ref[:, lo:lo + MXU_WIDTH]).astype(zq_out.dtype)
        yield
    for lo in range(0, ZG_W, MXU_WIDTH):
        hi = min(lo + MXU_WIDTH, ZG_W)
        zg_out[:, lo:hi] = _dot(hm, wgt_ref[:, lo:hi])
        yield


FFN_PIECES_PER_MIXER_STAGE = 1


def _ffn_mix_kernel(x_ref, g1_ref, wg_ref, wu_ref, wd_ref, gm_ref, wq_ref, wgt_ref, bias_ref, wa2_ref, ba_ref,
                    gml_ref, ggl_ref, ee_ref, x1_ref, ym_ref, co_ref, no_ref, mo_ref, so_ref,
                    zq_s, zg_s, c_s, n_s, m_s, s_s, *, tiles_per_batch):
    i = pl.program_id(0)

    @pl.when(i == 0)
    def _init():
        for ref in (zq_s, zg_s, c_s, n_s, m_s, s_s):
            ref[...] = jnp.zeros_like(ref)

    consts = (bias_ref[...], wa2_ref[...], ba_ref[...], gml_ref[...], ggl_ref[...])
    n_chunks = zq_s.shape[0] // CHUNK
    rows = [slice(k * CHUNK, (k + 1) * CHUNK) for k in range(n_chunks)]
    zqs = [zq_s[r, :] for r in rows]
    zgs = [zg_s[r, :] for r in rows]
    starts_batch = (i - 1) % tiles_per_batch == 0
    carry = lambda v: jnp.where(starts_batch, 0.0, v)
    state = dict(c=[carry(c_s[h]) for h in range(ML_HEADS)], n=[carry(n_s[h:h + 1, :]) for h in range(ML_HEADS)],
                 m=[carry(m_s[h:h + 1, 0:1]) for h in range(ML_HEADS)], s=[carry(s_s[h]) for h in range(GLA_HEADS)])

    def mixers():
        st = state
        for k in range(n_chunks):
            ys, new = yield from _mixer_stages([zqs[k]], [zgs[k]], [st], consts, ee_ref, L=CHUNK, t_real=CHUNK,
                                               chain=True)
            ym_ref[rows[k], :] = ys[0].astype(ym_ref.dtype)
            st = new[0]
            yield
        return st

    ffn = _ffn_in_pieces(x_ref[...], g1_ref, wg_ref, wu_ref, wd_ref, gm_ref, wq_ref, wgt_ref, x1_ref, zq_s, zg_s)
    mix = mixers()
    ffn_live = True
    while True:
        try:
            next(mix)
        except StopIteration as done:
            last = done.value
            break
        for _ in range(FFN_PIECES_PER_MIXER_STAGE):
            if ffn_live:
                try:
                    next(ffn)
                except StopIteration:
                    ffn_live = False
    if ffn_live:
        _drain(ffn)

    for h in range(ML_HEADS):
        c_s[h] = last["c"][h]
        n_s[h:h + 1, :] = last["n"][h]
        m_s[h:h + 1, :] = jnp.broadcast_to(last["m"][h], (1, LANES))
        s_s[h] = last["s"][h]

    @pl.when(jnp.logical_and(i >= 1, (i - 1) % tiles_per_batch == tiles_per_batch - 1))
    def _emit_state():
        lane = lax.broadcasted_iota(jnp.int32, (1, LANES), 1)
        for h in range(ML_HEADS):
            co_ref[0, h] = last["c"][h]
            so_ref[0, h] = last["s"][h]
        no_ref[0] = jnp.concatenate(last["n"], axis=0)
        m_row = jnp.zeros((1, LANES), F32)
        for h in range(ML_HEADS):
            m_row = jnp.where(lane == h, last["m"][h], m_row)
        mo_ref[0] = m_row[:, 0:ML_HEADS]


def _ffn_mix(x, w, tm):
    bsz, t, _ = x.shape
    tiles_per_batch = t // tm
    n_tiles = bsz * tiles_per_batch
    assert t % tm == 0 and tm % CHUNK == 0
    cur = lambda width: pl.BlockSpec((tm, width), lambda i: (jnp.minimum(i, n_tiles - 1), 0))
    prev_tile = lambda i: jnp.maximum(i - 1, 0)
    per_b = lambda *tail: pl.BlockSpec((1,) + tail, lambda i: (prev_tile(i) // tiles_per_batch,) + (0,) * len(tail))
    state_specs = [per_b(ML_HEADS, ML_DK, ML_DV), per_b(ML_HEADS, ML_DK), per_b(1, ML_HEADS),
                   per_b(GLA_HEADS, GLA_DK, GLA_DV)]
    x1, ym, c_new, n_new, m_new, s_new = pl.pallas_call(
        functools.partial(_ffn_mix_kernel, tiles_per_batch=tiles_per_batch),
        grid=(n_tiles + 1,),
        in_specs=[cur(D_MODEL), _const_spec((1, D_MODEL)), _const_spec((D_MODEL, D_FF)),
                  _const_spec((D_MODEL, D_FF)), _const_spec((D_FF, D_MODEL)), _const_spec((1, D_MODEL)),
                  _const_spec((D_MODEL, ZQ_W)), _const_spec((D_MODEL, ZG_W)), _const_spec((1, SMALL_W)),
                  _const_spec((SMALL_W, GLA_QK_W)), _const_spec((1, GLA_QK_W)), _const_spec((1, ML_HEADS * ML_DV)),
                  _const_spec((1, GLA_V_W)), _const_spec((GLA_SUB, LANES, LANES))],
        out_specs=[cur(D_MODEL), pl.BlockSpec((tm, D_MODEL), lambda i: (prev_tile(i), 0))] + state_specs,
        out_shape=[jax.ShapeDtypeStruct((n_tiles * tm, D_MODEL), F32),
                   jax.ShapeDtypeStruct((n_tiles * tm, D_MODEL), BF16),
                   jax.ShapeDtypeStruct((bsz, ML_HEADS, ML_DK, ML_DV), F32),
                   jax.ShapeDtypeStruct((bsz, ML_HEADS, ML_DK), F32),
                   jax.ShapeDtypeStruct((bsz, 1, ML_HEADS), F32),
                   jax.ShapeDtypeStruct((bsz, GLA_HEADS, GLA_DK, GLA_DV), F32)],
        scratch_shapes=[pltpu.VMEM((tm, ZQ_W), BF16), pltpu.VMEM((tm, ZG_W), F32),
                        pltpu.VMEM((ML_HEADS, ML_DK, ML_DV), F32), pltpu.VMEM((SUBLANES, LANES), F32),
                        pltpu.VMEM((SUBLANES, LANES), F32), pltpu.VMEM((GLA_HEADS, GLA_DK, GLA_DV), F32)],
        compiler_params=_params(("arbitrary",)),
        name="ffn_mix",
    )(x.reshape(bsz * t, D_MODEL), w["ffn1_g"], w["ffn1_wg"], w["ffn1_wu"], w["ffn1_wd"], w["mix_g"], w["w_in_q"],
      w["w_in_g"], w["gate_bias"], w["w_a2"], w["b_a"], w["mlstm_out_g"], w["gla_out_g"], w["diag_sum"])
    return x1, ym, (c_new[None], n_new[None], m_new.reshape(1, bsz, ML_HEADS), s_new[None])


def _post_mix_kernel(x1_ref, ym_ref, wout_ref, gx_ref, wq_ref, x2_ref, q_ref):
    x2 = x1_ref[...] + _dot(ym_ref[...].astype(BF16), wout_ref[...])
    x2_ref[...] = x2
    hq = _rms(x2, gx_ref[...]).astype(BF16)
    q_ref[...] = _dot(hq, wq_ref[...]).astype(q_ref.dtype)


def _post_mix(x1, ym, w, tm, q_dtype):
    n = x1.shape[0]
    row = lambda: pl.BlockSpec((tm, D_MODEL), lambda i: (i, 0))
    return pl.pallas_call(
        _post_mix_kernel,
        grid=(n // tm,),
        in_specs=[row(), row(), _const_spec((D_MODEL, D_MODEL)), _const_spec((1, D_MODEL)),
                  _const_spec((D_MODEL, D_MODEL))],
        out_specs=[row(), row()],
        out_shape=[jax.ShapeDtypeStruct((n, D_MODEL), F32), jax.ShapeDtypeStruct((n, D_MODEL), q_dtype)],
        compiler_params=_params(("arbitrary",)),
        name="post_mix",
    )(x1, ym, w["w_out"], w["xattn_g"], w["xattn_wq"])


def _softmax(s):
    e = jnp.exp(s - jnp.max(s, axis=-1, keepdims=True))
    return e / jnp.sum(e, axis=-1, keepdims=True)


def _post_fused_kernel(x1_ref, ym_ref, k_ref, v_ref, wout_ref, gx_ref, wq_ref, wo_ref, g2_ref, wg_ref, wu_ref,
                       wd_ref, gf_ref, y_ref):
    x2 = x1_ref[0] + _dot(ym_ref[0].astype(BF16), wout_ref[...])
    q = _dot(_rms(x2, gx_ref[...]).astype(BF16), wq_ref[...]).astype(BF16)
    heads = [slice(h * XA_DH, (h + 1) * XA_DH) for h in range(XA_HEADS)]
    k_full, v_full = _cache_rows_load(k_ref, 0), _cache_rows_load(v_ref, 0)
    scores = [_dot_nt(q[:, hs], k_full[:, hs]) * (XA_DH ** -0.5) for hs in heads]
    probs = [_softmax(s).astype(BF16) for s in scores]
    o = jnp.concatenate([_dot(p, v_full[:, hs]).astype(BF16) for hs, p in zip(heads, probs)], axis=1)
    x3 = x2 + _dot(o, wo_ref[...])
    x4 = _swiglu_residual(x3, g2_ref, wg_ref, wu_ref, wd_ref)
    y_ref[0] = _rms(x4, gf_ref[...])


def _post_fused(x1, ym, k, v, w, tm):
    bsz, t = x1.shape[0], x1.shape[1]
    row = lambda: pl.BlockSpec((1, tm, D_MODEL), lambda b, j: (b, j, 0))
    kv = pl.BlockSpec((1, N_MEM * CACHE_ROW_GROUP, LANES), lambda b, j: (b, 0, 0))
    sq = _const_spec((D_MODEL, D_MODEL))
    vec = _const_spec((1, D_MODEL))
    return pl.pallas_call(
        _post_fused_kernel,
        grid=(bsz, t // tm),
        in_specs=[row(), row(), kv, kv, sq, vec, sq, sq, vec, _const_spec((D_MODEL, D_FF)),
                  _const_spec((D_MODEL, D_FF)), _const_spec((D_FF, D_MODEL)), vec],
        out_specs=row(),
        out_shape=jax.ShapeDtypeStruct((bsz, t, D_MODEL), F32),
        compiler_params=_params(("arbitrary", "arbitrary")),
        name="post_fused",
    )(x1, ym, k, v, w["w_out"], w["xattn_g"], w["xattn_wq"], w["xattn_wo"], w["ffn2_g"], w["ffn2_wg"],
      w["ffn2_wu"], w["ffn2_wd"], w["final_g"])


XA_LANE_TILES = XA_DH // LANES
CACHE_ROW_GROUP = XA_LANE_TILES * XA_HEADS


def _cache_rows_view(x):
    bsz = x.shape[0]
    x = x.reshape(bsz, N_MEM, XA_HEADS, XA_LANE_TILES, LANES)
    return x.transpose(0, 1, 3, 2, 4).reshape(bsz, N_MEM * CACHE_ROW_GROUP, LANES)


def _cache_rows_unview(x):
    bsz = x.shape[0]
    x = x.reshape(bsz, N_MEM, XA_LANE_TILES, XA_HEADS, LANES)
    return x.transpose(0, 1, 3, 2, 4).reshape(bsz, N_MEM, XA_HEADS, XA_DH)


def _cache_rows_store(ref, b, x):
    for h in range(XA_HEADS):
        for lt in range(XA_LANE_TILES):
            lo = h * XA_DH + lt * LANES
            ref[b, pl.ds(lt * XA_HEADS + h, N_MEM, stride=CACHE_ROW_GROUP), :] = x[:, lo:lo + LANES]


def _cache_rows_load(ref, b):
    cols = [ref[b, pl.ds(lt * XA_HEADS + h, N_MEM, stride=CACHE_ROW_GROUP), :]
            for h in range(XA_HEADS) for lt in range(XA_LANE_TILES)]
    return jnp.concatenate(cols, axis=1).astype(BF16)


def _xattn_cache_kernel(q_ref, k_ref, v_ref, o_ref, *, bb, tq):
    lane_head = lax.broadcasted_iota(jnp.int32, (1, D_MODEL), 1) // XA_DH
    scores = []
    for b in range(bb):
        q = q_ref[b]
        q_bd = jnp.concatenate([jnp.where(lane_head == h, q, 0.0) for h in range(XA_HEADS)], axis=0)
        k_full = _cache_rows_load(k_ref, b)
        scores.append(_dot_nt(q_bd.astype(BF16), k_full) * (XA_DH ** -0.5))
    p_all = _softmax(jnp.concatenate(scores, axis=0)).astype(BF16)
    rows = XA_HEADS * tq
    for b in range(bb):
        v_full = _cache_rows_load(v_ref, b)
        o_full = _dot(p_all[b * rows:(b + 1) * rows], v_full)
        o = jnp.zeros((tq, D_MODEL), F32)
        for h in range(XA_HEADS):
            o = jnp.where(lane_head == h, o_full[h * tq:(h + 1) * tq], o)
        o_ref[b] = o.astype(o_ref.dtype)


def _xattn_cache(q, k, v, bb, o_dtype):
    bsz, tq = q.shape[0], q.shape[1]
    k, v = _cache_rows_view(k), _cache_rows_view(v)
    qo = pl.BlockSpec((bb, tq, D_MODEL), lambda b: (b, 0, 0))
    kv = pl.BlockSpec((bb, N_MEM * CACHE_ROW_GROUP, LANES), lambda b: (b, 0, 0))
    return pl.pallas_call(
        functools.partial(_xattn_cache_kernel, bb=bb, tq=tq),
        grid=(bsz // bb,),
        in_specs=[qo, kv, kv],
        out_specs=qo,
        out_shape=jax.ShapeDtypeStruct((bsz, tq, D_MODEL), o_dtype),
        compiler_params=_params(("arbitrary",)),
        name="xattn_cache",
    )(q, k, v)


def _ffn_out_kernel(x2_ref, o_ref, wo_ref, g2_ref, wg_ref, wu_ref, wd_ref, gf_ref, y_ref):
    x3 = x2_ref[...] + _dot(o_ref[...].astype(BF16), wo_ref[...])
    x4 = _swiglu_residual(x3, g2_ref, wg_ref, wu_ref, wd_ref)
    y_ref[...] = _rms(x4, gf_ref[...])


def _ffn_out(x2, o, w, tm):
    n = x2.shape[0]
    row = lambda: pl.BlockSpec((tm, D_MODEL), lambda i: (i, 0))
    return pl.pallas_call(
        _ffn_out_kernel,
        grid=(n // tm,),
        in_specs=[row(), row(), _const_spec((D_MODEL, D_MODEL)), _const_spec((1, D_MODEL)),
                  _const_spec((D_MODEL, D_FF)), _const_spec((D_MODEL, D_FF)), _const_spec((D_FF, D_MODEL)),
                  _const_spec((1, D_MODEL))],
        out_specs=row(),
        out_shape=jax.ShapeDtypeStruct((n, D_MODEL), F32),
        compiler_params=_params(("arbitrary",)),
        name="ffn_out",
    )(x2, o, w["xattn_wo"], w["ffn2_g"], w["ffn2_wg"], w["ffn2_wu"], w["ffn2_wd"], w["final_g"])


def _memkv_kernel(m_ref, g_ref, wk_ref, wv_ref, k_ref, v_ref):
    hn = _rms(m_ref[0], g_ref[...]).astype(BF16)
    _cache_rows_store(k_ref, 0, _dot(hn, wk_ref[...]))
    _cache_rows_store(v_ref, 0, _dot(hn, wv_ref[...]))


def _memkv(mem, w):
    bsz = mem.shape[0]
    rows = pl.BlockSpec((1, N_MEM * CACHE_ROW_GROUP, LANES), lambda b: (b, 0, 0))
    return pl.pallas_call(
        _memkv_kernel,
        grid=(bsz,),
        in_specs=[pl.BlockSpec((1, N_MEM, D_MODEL), lambda b: (b, 0, 0)), _const_spec((1, D_MODEL)),
                  _const_spec((D_MODEL, D_MODEL)), _const_spec((D_MODEL, D_MODEL))],
        out_specs=[rows, rows],
        out_shape=[jax.ShapeDtypeStruct((bsz, N_MEM * CACHE_ROW_GROUP, LANES), F32)] * 2,
        compiler_params=_params(("arbitrary",)),
        name="memkv",
    )(mem, w["mem_g"], w["xattn_wk"], w["xattn_wv"])


def _prep_weights(p):
    bf = lambda a: a.astype(BF16)
    row = lambda a: a.reshape(1, -1).astype(F32)
    w_in = p["w_in"]
    off = {}
    pos = 0
    for name, width in (("mq", 512), ("mk", 512), ("mv", 512), ("mi", 4), ("mf", 4), ("mo", 512), ("gq", 256),
                        ("gk", 256), ("gv", 512), ("ga", 16), ("gg", 512)):
        off[name] = w_in[:, pos:pos + width]
        pos += width
    pad_cols = SMALL_W - 2 * ML_HEADS - GLA_RANK
    w_in_q = jnp.concatenate([off[k] for k in ("mq", "mk", "mv", "gq", "gk", "gv")], axis=1)
    w_in_g = jnp.concatenate([off["mo"], off["gg"], off["mi"], off["mf"], off["ga"],
                              jnp.zeros((D_MODEL, pad_cols), F32)], axis=1)
    gate_bias = jnp.concatenate([p["mlstm_b_i"], p["mlstm_b_f"], jnp.zeros((SMALL_W - 2 * ML_HEADS,), F32)])
    w_a2 = jnp.zeros((SMALL_W, GLA_QK_W), F32).at[2 * ML_HEADS:2 * ML_HEADS + GLA_RANK].set(p["gla_w_a2"])
    lane = jnp.arange(LANES)
    same_head = (lane[:, None] // GLA_DK) == (lane[None, :] // GLA_DK)
    diag_sum = jnp.stack([same_head & ((lane[None, :] % GLA_SUB) == j) for j in range(GLA_SUB)]).astype(BF16)
    return dict(
        ffn1_g=row(p["ffn1_norm_g"]), ffn1_wg=bf(p["ffn1_w_gate"]), ffn1_wu=bf(p["ffn1_w_up"]),
        ffn1_wd=bf(p["ffn1_w_down"]), mix_g=row(p["mix_norm_g"]), w_in_q=bf(w_in_q), w_in_g=bf(w_in_g),
        gate_bias=row(gate_bias), w_a2=bf(w_a2), b_a=row(p["gla_b_a"]), mlstm_out_g=row(p["mlstm_out_g"]),
        gla_out_g=row(p["gla_out_g"]), diag_sum=diag_sum,
        w_out=bf(p["w_out"]), xattn_g=row(p["xattn_norm_g"]), xattn_wq=bf(p["xattn_w_q"]),
        xattn_wo=bf(p["xattn_w_o"]), mem_g=row(p["mem_norm_g"]), xattn_wk=bf(p["xattn_w_k"]),
        xattn_wv=bf(p["xattn_w_v"]), ffn2_g=row(p["ffn2_norm_g"]), ffn2_wg=bf(p["ffn2_w_gate"]),
        ffn2_wu=bf(p["ffn2_w_up"]), ffn2_wd=bf(p["ffn2_w_down"]), final_g=row(p["final_g"]))


def _ffn_and_mixers(x, state, w, *, chunk, mixer_nblk, act_dtype):
    bsz, t, _ = x.shape
    x1, zq, zg = _ffn_in(x.reshape(bsz * t, D_MODEL), w, TM_FFN_IN, act_dtype)
    zq = zq.reshape(bsz, t, ZQ_W)
    zg = zg.reshape(bsz, t, ZG_W)
    pad = -t % chunk
    if pad:
        zq = jnp.pad(zq, ((0, 0), (0, pad), (0, 0)))
        zg = jnp.pad(zg, ((0, 0), (0, pad), (0, 0)))
    ym, c_new, n_new, m_new, s_new = _mixer(zq, zg, w, chunk, t, state, act_dtype, mixer_nblk)
    return x1, ym[:, :t], (c_new[None], n_new[None], m_new.reshape(1, bsz, ML_HEADS), s_new[None])


def kernel(x_prompt, x_sample, mem_prompt, cache_mem_k, cache_mem_v, state_mlstm_c, state_mlstm_n, state_mlstm_m, state_gla_s, ffn1_norm_g, ffn1_w_gate, ffn1_w_up, ffn1_w_down, mix_norm_g, w_in, mlstm_b_i, mlstm_b_f, mlstm_out_g, gla_w_a2, gla_b_a, gla_out_g, w_out, xattn_norm_g, mem_norm_g, xattn_w_q, xattn_w_k, xattn_w_v, xattn_w_o, ffn2_norm_g, ffn2_w_gate, ffn2_w_up, ffn2_w_down, final_norm_g):
    assert ffn1_norm_g.shape[0] == 1, "single-layer stack"
    layer = dict(ffn1_norm_g=ffn1_norm_g, ffn1_w_gate=ffn1_w_gate, ffn1_w_up=ffn1_w_up, ffn1_w_down=ffn1_w_down,
                 mix_norm_g=mix_norm_g, w_in=w_in, mlstm_b_i=mlstm_b_i, mlstm_b_f=mlstm_b_f,
                 mlstm_out_g=mlstm_out_g, gla_w_a2=gla_w_a2, gla_b_a=gla_b_a, gla_out_g=gla_out_g, w_out=w_out,
                 xattn_norm_g=xattn_norm_g, mem_norm_g=mem_norm_g, xattn_w_q=xattn_w_q, xattn_w_k=xattn_w_k,
                 xattn_w_v=xattn_w_v, xattn_w_o=xattn_w_o, ffn2_norm_g=ffn2_norm_g, ffn2_w_gate=ffn2_w_gate,
                 ffn2_w_up=ffn2_w_up, ffn2_w_down=ffn2_w_down)
    p = {name: arr[0] for name, arr in layer.items()}
    p["final_g"] = final_norm_g
    w = _prep_weights(p)

    bp, tp, _ = x_prompt.shape
    bs, ts, _ = x_sample.shape

    mem_k_p, mem_v_p = _memkv(mem_prompt, w)
    x1_p, ym_p, states_p = _ffn_mix(x_prompt, w, TM_FFN_IN)
    y_p = _post_fused(x1_p.reshape(bp, tp, D_MODEL), ym_p.reshape(bp, tp, D_MODEL), mem_k_p, mem_v_p, w, TM_POST)

    state = (state_mlstm_c[0], state_mlstm_n[0], state_mlstm_m[0].reshape(bs, 1, ML_HEADS), state_gla_s[0])
    x1_s, ym_s, states_s = _ffn_and_mixers(x_sample, state, w, chunk=SUBLANES, mixer_nblk=SAMPLE_MIXER_BATCHES,
                                           act_dtype=F32)
    x2_s, q_s = _post_mix(x1_s, ym_s.reshape(bs * ts, D_MODEL), w, TM_FFN_IN, F32)
    o_s = _xattn_cache(q_s.reshape(bs, ts, D_MODEL), cache_mem_k[0], cache_mem_v[0], SAMPLE_XATTN_BATCHES, F32)
    y_s = _ffn_out(x2_s, o_s.reshape(bs * ts, D_MODEL), w, TM_FFN_IN).reshape(bs, ts, D_MODEL)

    return (y_p, y_s, _cache_rows_unview(mem_k_p)[None], _cache_rows_unview(mem_v_p)[None]) + states_p + states_s
```

```python
import functools

import jax
import jax.numpy as jnp
from jax import lax
from jax.experimental import pallas as pl
from jax.experimental.pallas import tpu as pltpu

F32 = jnp.float32
BF16 = jnp.bfloat16

D_MODEL = 1024
D_FF = 2816
ML_HEADS = 4
ML_DK = 128
ML_DV = 128
GLA_HEADS = 4
GLA_DK = 64
GLA_DV = 128
GLA_RANK = 16
GLA_TAU = 16.0
N_MEM = 256
XA_HEADS = 4
XA_DH = D_MODEL // XA_HEADS
EPS = 1e-6
CHUNK = 64
LOG2_E = 1.4426950408889634
LN_2 = 0.6931471805599453
LANES = 128
SUBLANES = 8

ZQ_MQ, ZQ_MK, ZQ_MV = 0, 512, 1024
ZQ_GQ, ZQ_GK, ZQ_GV = 1536, 1792, 2048
ZQ_W = 2560
ZG_MO, ZG_GG, ZG_SMALL = 0, 512, 1024
ZG_W = 1152
SMALL_W = LANES
GLA_QK_W = GLA_HEADS * GLA_DK
GLA_V_W = GLA_HEADS * GLA_DV
GLA_PAIRS = GLA_HEADS // 2
GLA_SUB = SUBLANES

MXU_WIDTH = 256
FF_CHUNKS = ((0, 6 * MXU_WIDTH), (6 * MXU_WIDTH, D_FF))
VMEM_LIMIT_BYTES = 56 * 1024 * 1024

TM_FFN_IN = MXU_WIDTH
TM_POST = 2 * MXU_WIDTH
PROMPT_MIXER_CHUNKS = 1
SAMPLE_MIXER_BATCHES = 8


def _rms(x, g):
    return x * lax.rsqrt(jnp.mean(x * x, axis=-1, keepdims=True) + EPS) * g


def _log_sigmoid(x):
    return jnp.minimum(x, 0.0) - jnp.log1p(jnp.exp(-jnp.abs(x)))


def _dot(a, b):
    return jnp.dot(a, b, preferred_element_type=F32)


def _dot_nt(a, b):
    return lax.dot_general(a, b, (((1,), (1,)), ((), ())), preferred_element_type=F32)


def _dot_f32(a, b):
    return jnp.dot(a, b, precision=lax.Precision.HIGHEST, preferred_element_type=F32)


def _swiglu_residual(x, g_ref, wg_ref, wu_ref, wd_ref):
    h = _rms(x, g_ref[...]).astype(BF16)
    acts = []
    for lo, hi in FF_CHUNKS:
        g = _dot(h, wg_ref[:, lo:hi])
        u = _dot(h, wu_ref[:, lo:hi])
        acts.append(((g * jax.nn.sigmoid(g)) * u).astype(BF16))
    acc = jnp.zeros_like(x)
    for (lo, hi), a in zip(FF_CHUNKS, acts):
        acc = acc + _dot(a, wd_ref[lo:hi, :])
    return x + 0.5 * acc


def _const_spec(shape):
    nd = len(shape)
    return pl.BlockSpec(shape, lambda *_: (0,) * nd, pipeline_mode=pl.Buffered(1))


def _params(sem):
    return pltpu.CompilerParams(dimension_semantics=sem, vmem_limit_bytes=VMEM_LIMIT_BYTES)


def _ffn_in_kernel(x_ref, g1_ref, wg_ref, wu_ref, wd_ref, gm_ref, wq_ref, wgt_ref, x1_ref, zq_ref, zg_ref):
    x1 = _swiglu_residual(x_ref[...], g1_ref, wg_ref, wu_ref, wd_ref)
    x1_ref[...] = x1
    hm = _rms(x1, gm_ref[...]).astype(BF16)
    zq_ref[...] = _dot(hm, wq_ref[...]).astype(zq_ref.dtype)
    zg_ref[...] = _dot(hm, wgt_ref[...])


def _ffn_in(x, w, tm, zq_dtype):
    n = x.shape[0]
    row = lambda width: pl.BlockSpec((tm, width), lambda i: (i, 0))
    return pl.pallas_call(
        _ffn_in_kernel,
        grid=(n // tm,),
        in_specs=[row(D_MODEL), _const_spec((1, D_MODEL)), _const_spec((D_MODEL, D_FF)),
                  _const_spec((D_MODEL, D_FF)), _const_spec((D_FF, D_MODEL)), _const_spec((1, D_MODEL)),
                  _const_spec((D_MODEL, ZQ_W)), _const_spec((D_MODEL, ZG_W))],
        out_specs=[row(D_MODEL), row(ZQ_W), row(ZG_W)],
        out_shape=[jax.ShapeDtypeStruct((n, D_MODEL), F32), jax.ShapeDtypeStruct((n, ZQ_W), zq_dtype),
                   jax.ShapeDtypeStruct((n, ZG_W), F32)],
        compiler_params=_params(("arbitrary",)),
        name="ffn_in",
    )(x, w["ffn1_g"], w["ffn1_wg"], w["ffn1_wu"], w["ffn1_wd"], w["mix_g"], w["w_in_q"], w["w_in_g"])


def _mixer_stages(zqs, zgs, states, consts, ee_ref, *, L, t_real, chain):
    bias, wa2, ba, gml, ggl = consts
    nchunk = len(zqs)
    groups = [(b, h) for b in range(nchunk) for h in range(ML_HEADS)]
    pairs = [(b, p) for b in range(nchunk) for p in range(GLA_PAIRS)]
    padded = t_real < L
    valid = lax.broadcasted_iota(jnp.int32, (L, 1), 0) < t_real
    rr = lax.broadcasted_iota(jnp.int32, (L, L), 0)
    cc = lax.broadcasted_iota(jnp.int32, (L, L), 1)
    tril = cc <= rr
    c = GLA_SUB
    nb = L // c
    lane = lax.broadcasted_iota(jnp.int32, (1, LANES), 1)
    lane_blk = (lane % GLA_DK) // c
    lane_head = lane // GLA_DK
    v_lane_head = lax.broadcasted_iota(jnp.int32, (1, 2 * GLA_DV), 1) // GLA_DV
    row_blk = (lax.broadcasted_iota(jnp.int32, (2 * L, 1), 0) % L) // c
    t_in = lax.broadcasted_iota(jnp.int32, (1, c, 1), 1)
    new_states = [dict(c=[None] * ML_HEADS, n=[None] * ML_HEADS, m=[None] * ML_HEADS, s=[None] * GLA_HEADS)
                  for _ in range(nchunk)]
    prev = lambda b: new_states[b - 1] if chain and b > 0 else states[0 if chain else b]
    known = lambda b: not chain or b == 0

    carried = {}

    def carried_mlstm(b, h):
        st = prev(b)
        d = ml[(b, h)]
        carried[(b, h)] = _dot(d["qb"], st["c"][h].astype(BF16))

    def carried_gla(b, p):
        s_prev = prev(b)["s"]
        zero_blk = jnp.zeros((GLA_DK, GLA_DV), BF16)
        s_bd = jnp.concatenate(
            [jnp.concatenate([s_prev[2 * p].astype(BF16), zero_blk], axis=1),
             jnp.concatenate([zero_blk, s_prev[2 * p + 1].astype(BF16)], axis=1)], axis=0)
        carried[(b, "gla", p)] = _dot(gl[(b, p)]["q_dec"], s_bd)

    smalls, sms, lfs, b_cols, b_rows, sm_ts, las, bcs = [], [], [], [], [], [], [], []
    for b in range(nchunk):
        small = zgs[b][:, ZG_SMALL:ZG_SMALL + SMALL_W]
        sm = small + bias
        lf = _log_sigmoid(sm) * LOG2_E
        sm = sm * LOG2_E
        if padded:
            sm = jnp.where(valid, sm, -jnp.inf)
            lf = jnp.where(valid, lf, 0.0)
        smalls.append(small)
        sms.append(sm)
        lfs.append(lf)

    long_chunk = L > SUBLANES
    col_w = 1 if long_chunk else LANES
    tril_f, triu_f = tril.astype(F32), (rr <= cc).astype(F32)

    def cumsum_rows(x):
        if long_chunk:
            return _dot_f32(tril_f, x)
        acc = x[0:1]
        rows = [acc]
        for r in range(1, L):
            acc = acc + x[r:r + 1]
            rows.append(acc)
        return jnp.concatenate(rows, axis=0)

    lf_ts = [lf.T[0:SUBLANES] for lf in lfs] if long_chunk else None
    yield
    for b in range(nchunk):
        b_cols.append(cumsum_rows(lfs[b]))
        b_rows.append(_dot_f32(lf_ts[b], triu_f) if long_chunk else b_cols[b].T[0:SUBLANES])
        sm_ts.append(sms[b].T)
        la = _log_sigmoid(_dot(smalls[b].astype(BF16), wa2) + ba) * (LOG2_E / GLA_TAU)
        las.append(jnp.where(valid, la, 0.0) if padded else la)

    ml = {}
    for g in groups:
        b, h = g
        zq = zqs[b]
        qf = zq[:, ZQ_MQ + h * ML_DK:ZQ_MQ + (h + 1) * ML_DK].astype(F32)
        kf = zq[:, ZQ_MK + h * ML_DK:ZQ_MK + (h + 1) * ML_DK].astype(F32) * (ML_DK ** -0.5)
        vf = zq[:, ZQ_MV + h * ML_DV:ZQ_MV + (h + 1) * ML_DV].astype(F32)
        if padded:
            kf = jnp.where(valid, kf, 0.0)
            vf = jnp.where(valid, vf, 0.0)
        ml[g] = dict(qf=qf, qb=qf.astype(BF16), kf=kf, kb=kf.astype(BF16), vb=vf.astype(BF16))
    yield
    for b in range(nchunk):
        bcs.append(cumsum_rows(las[b]))
    for g in groups:
        d = ml[g]
        d["qk"] = _dot_nt(d["qb"], d["kb"])
        if known(g[0]):
            carried_mlstm(*g)

    yield
    for g in groups:
        b, h = g
        d = ml[g]
        i_col = jnp.broadcast_to(sms[b][:, h:h + 1], (L, col_w))
        b_col = jnp.broadcast_to(b_cols[b][:, ML_HEADS + h:ML_HEADS + h + 1], (L, col_w))
        b_row = b_rows[b][ML_HEADS + h:ML_HEADS + h + 1, :]
        i_row = sm_ts[b][h:h + 1, :]
        a_col = b_col + prev(b)["m"][h] * LOG2_E
        dm = jnp.where(tril, b_col[:, :L] - (b_row - i_row), -jnp.inf)
        mt = jnp.maximum(a_col, jnp.max(dm, axis=1, keepdims=True))
        w_inter = jnp.exp2(a_col - mt)
        s = d["qk"] * jnp.exp2(dm - mt[:, :L])
        kw = d["kf"] * jnp.exp2((b_col[L - 1:L] - mt[L - 1:L]) - (b_col - i_col))
        d.update(mt=mt, w_inter=w_inter, s=s, kw=kw, kw_t=kw.T.astype(BF16))
        new_states[b]["m"][h] = mt[L - 1:L, 0:1] * LN_2

    gl = {}
    for b in range(nchunk):
        zq = zqs[b]
        gq = zq[:, ZQ_GQ:ZQ_GQ + GLA_QK_W].astype(F32) * (GLA_DK ** -0.5)
        gk = zq[:, ZQ_GK:ZQ_GK + GLA_QK_W].astype(F32)
        gv = zq[:, ZQ_GV:ZQ_GV + GLA_V_W].astype(F32)
        if padded:
            gk = jnp.where(valid, gk, 0.0)
            gv = jnp.where(valid, gv, 0.0)
        stack = lambda x: jnp.concatenate([x[:, :LANES], x[:, LANES:]], axis=0)
        q2, k2, b2 = stack(gq), stack(gk), stack(bcs[b])
        q3 = q2.reshape(2 * nb, c, LANES)
        k3 = k2.reshape(2 * nb, c, LANES)
        b3 = b2.reshape(2 * nb, c, LANES)
        pair_terms = []
        for j in range(min(c, t_real)):
            decay = jnp.exp2(jnp.where(t_in >= j, b3 - b3[:, j:j + 1, :], -jnp.inf))
            pair_terms.append((q3 * k3[:, j:j + 1, :] * decay).reshape(2 * L, LANES).astype(BF16))
        kt2 = (k3 * jnp.exp2(b3[:, c - 1:c, :] - b3)).reshape(2 * L, LANES) if nb > 1 else None
        gl[b] = dict(gv=gv, q2=q2, k2=k2, b2=b2, pair_terms=pair_terms, kt2=kt2)
    yield
    for b in range(nchunk):
        acc = jnp.zeros((2 * L, LANES), F32)
        for j, pair_j in enumerate(gl[b]["pair_terms"]):
            acc = acc + _dot(pair_j, ee_ref[j])
        gl[b]["a_diag"] = jnp.where(lane_blk == row_blk, acc, 0.0)

    for bp in pairs:
        b, p = bp
        d = gl[b]
        rows_p = slice(p * L, (p + 1) * L)
        q_p, k_p, b_p = d["q2"][rows_p], d["k2"][rows_p], d["b2"][rows_p]
        e = dict(q_dec=(q_p * jnp.exp2(b_p)).astype(BF16),
                 kh_t=(k_p * jnp.exp2(b_p[L - 1:L] - b_p)).T.astype(BF16),
                 decay_col=jnp.exp2(b_p[L - SUBLANES:L].T[:, SUBLANES - 1:SUBLANES]))
        if nb > 1:
            kt_p = d["kt2"][rows_p].astype(BF16)
            k_bd = jnp.concatenate([jnp.where(lane_head == hh, kt_p, 0.0) for hh in range(2)], axis=0)
            slabs, offs = [], []
            off = 0
            for j in range(nb - 1):
                lo = (j + 1) * c
                slabs.append(q_p[lo:] * jnp.exp2(b_p[lo:] - b_p[lo - 1:lo]))
                offs.append(off)
                off += L - lo
            e.update(q_var=jnp.concatenate(slabs, axis=0).astype(BF16), k_bd=k_bd, offs=offs)
        gl[bp] = e
    yield
    for bp in pairs:
        e = gl[bp]
        if nb > 1:
            e["r"] = _dot_nt(e["q_var"], e["k_bd"])
        if known(bp[0]):
            carried_gla(*bp)

    for g in groups:
        d = ml[g]
        d["sv"] = _dot(d["s"].astype(BF16), d["vb"])
        d["c_upd"] = _dot(d["kw_t"], d["vb"])
        d["den"] = jnp.sum(d["s"], axis=1, keepdims=True)
        if known(g[0]):
            d["qn"] = jnp.sum(d["qf"] * prev(g[0])["n"][g[1]], axis=1, keepdims=True)
    yield
    for bp in pairs:
        b, p = bp
        d, e = gl[b], gl[bp]
        a_p = d["a_diag"][p * L:(p + 1) * L]
        if nb > 1:
            blocks = []
            for i in range(nb):
                blk = a_p[i * c:(i + 1) * c]
                for j in range(i):
                    lo_r = e["offs"][j] + (i - j - 1) * c
                    blk = jnp.where(lane_blk == j, e["r"][lo_r:lo_r + c], blk)
                blocks.append(blk)
            a_p = jnp.concatenate(blocks, axis=0)
        v_f = d["gv"][:, p * 2 * GLA_DV:(p + 1) * 2 * GLA_DV]
        v_p = v_f.astype(BF16)
        if L < GLA_DK:
            v_rows = []
            for hh in range(2):
                v_rows += [jnp.where(v_lane_head == hh, v_f, 0.0), jnp.zeros((GLA_DK - L, 2 * GLA_DV), F32)]
            v_bd = jnp.concatenate(v_rows, axis=0).astype(BF16)
        else:
            v_bd = jnp.concatenate([jnp.where(v_lane_head == hh, v_p, 0.0) for hh in range(2)], axis=0)
        e["s_upd"] = [_dot(e["kh_t"][hh * GLA_DK:(hh + 1) * GLA_DK], v_p[:, hh * GLA_DV:(hh + 1) * GLA_DV])
                      for hh in range(2)]
        e.update(scores=a_p.astype(BF16), v_bd=v_bd)
    yield
    for bp in pairs:
        e = gl[bp]
        e["o_intra"] = _dot(e["scores"], e["v_bd"])

    yield
    ys = [[None] * (ML_HEADS + GLA_HEADS) for _ in range(nchunk)]

    def finish_mlstm(b):
        st = prev(b)
        for h in range(ML_HEADS):
            d = ml[(b, h)]
            mt, w_inter = d["mt"], d["w_inter"]
            qn = d["qn"] if known(b) else jnp.sum(d["qf"] * st["n"][h], axis=1, keepdims=True)
            den = d["den"] + w_inter * qn
            hh = (d["sv"] + w_inter * carried[(b, h)]) / jnp.maximum(jnp.abs(den), jnp.exp2(-mt))
            i_last = w_inter[L - 1:L]
            new_states[b]["c"][h] = i_last * st["c"][h] + d["c_upd"]
            new_states[b]["n"][h] = i_last * st["n"][h] + jnp.sum(d["kw"], axis=0, keepdims=True)
            hs = slice(h * ML_DV, (h + 1) * ML_DV)
            yn = hh * lax.rsqrt(jnp.mean(hh * hh, axis=-1, keepdims=True) + EPS) * gml[:, hs]
            ys[b][h] = jax.nn.sigmoid(zgs[b][:, ZG_MO + h * ML_DV:ZG_MO + (h + 1) * ML_DV]) * yn

    def finish_gla(b):
        st = prev(b)
        for p in range(GLA_PAIRS):
            e = gl[(b, p)]
            o = carried[(b, "gla", p)] + e["o_intra"]
            for hh in range(2):
                h = 2 * p + hh
                ds = slice(hh * GLA_DK, (hh + 1) * GLA_DK)
                vs = slice(hh * GLA_DV, (hh + 1) * GLA_DV)
                new_states[b]["s"][h] = e["decay_col"][ds] * st["s"][h] + e["s_upd"][hh]
                oh = o[:, vs]
                yn = oh * lax.rsqrt(jnp.mean(oh * oh, axis=-1, keepdims=True) + EPS) * ggl[:, h * GLA_DV:(h + 1) * GLA_DV]
                gg = zgs[b][:, ZG_GG + h * GLA_DV:ZG_GG + (h + 1) * GLA_DV]
                ys[b][ML_HEADS + h] = (gg * jax.nn.sigmoid(gg)) * yn

    if chain:
        for b in range(nchunk):
            if not known(b):
                for h in range(ML_HEADS):
                    carried_mlstm(b, h)
                for p in range(GLA_PAIRS):
                    carried_gla(b, p)
            finish_mlstm(b)
            finish_gla(b)
    else:
        for b in range(nchunk):
            finish_mlstm(b)
        for b in range(nchunk):
            finish_gla(b)
    return [jnp.concatenate(y, axis=1) for y in ys], new_states


def _drain(gen):
    try:
        while True:
            next(gen)
    except StopIteration as done:
        return done.value


def _mixer_compute(*args, **kwargs):
    return _drain(_mixer_stages(*args, **kwargs))


def _mixer_kernel(*refs, L, t_real, has_state, n_steps, nblk):
    it = iter(refs)
    zq_ref, zg_ref, bias_ref, wa2_ref, ba_ref, gml_ref, ggl_ref, ee_ref = (next(it) for _ in range(8))
    if has_state:
        c0_ref, n0_ref, m0_ref, s0_ref = (next(it) for _ in range(4))
    y_ref, co_ref, no_ref, mo_ref, so_ref = (next(it) for _ in range(5))
    consts = (bias_ref[...], wa2_ref[...], ba_ref[...], gml_ref[...], ggl_ref[...])
    lane = lax.broadcasted_iota(jnp.int32, (1, LANES), 1)

    def emit_state(b, st):
        for h in range(ML_HEADS):
            co_ref[b, h] = st["c"][h]
            so_ref[b, h] = st["s"][h]
        no_ref[b] = jnp.concatenate(st["n"], axis=0)
        m_row = jnp.zeros((1, LANES), F32)
        for h in range(ML_HEADS):
            m_row = jnp.where(lane == h, st["m"][h], m_row)
        mo_ref[b] = m_row[:, 0:ML_HEADS]

    if has_state:
        states = []
        for b in range(nblk):
            n_all = n0_ref[b]
            m_all = m0_ref[b]
            states.append(dict(c=[c0_ref[b, h] for h in range(ML_HEADS)],
                               n=[n_all[h:h + 1, :] for h in range(ML_HEADS)],
                               m=[m_all[:, h:h + 1] for h in range(ML_HEADS)],
                               s=[s0_ref[b, h] for h in range(GLA_HEADS)]))
        ys, new_states = _mixer_compute([zq_ref[b] for b in range(nblk)], [zg_ref[b] for b in range(nblk)], states,
                                        consts, ee_ref, L=L, t_real=t_real, chain=False)
        for b in range(nblk):
            y_ref[b] = ys[b].astype(y_ref.dtype)
            emit_state(b, new_states[b])
        return

    c_s, n_s, m_s, s_s = (next(it) for _ in range(4))
    step = pl.program_id(1)

    @pl.when(step == 0)
    def _init():
        c_s[...] = jnp.zeros_like(c_s)
        n_s[...] = jnp.zeros_like(n_s)
        m_s[...] = jnp.zeros_like(m_s)
        s_s[...] = jnp.zeros_like(s_s)

    rows = [slice(k * L, (k + 1) * L) for k in range(nblk)]
    state = dict(c=[c_s[h] for h in range(ML_HEADS)], n=[n_s[h:h + 1, :] for h in range(ML_HEADS)],
                 m=[m_s[h:h + 1, 0:1] for h in range(ML_HEADS)], s=[s_s[h] for h in range(GLA_HEADS)])
    ys, new_states = _mixer_compute([zq_ref[0, r, :] for r in rows], [zg_ref[0, r, :] for r in rows], [state],
                                    consts, ee_ref, L=L, t_real=t_real, chain=True)
    for r, y in zip(rows, ys):
        y_ref[0, r, :] = y.astype(y_ref.dtype)
    last = new_states[-1]
    for h in range(ML_HEADS):
        c_s[h] = last["c"][h]
        n_s[h:h + 1, :] = last["n"][h]
        m_s[h:h + 1, :] = jnp.broadcast_to(last["m"][h], (1, LANES))
        s_s[h] = last["s"][h]
    pl.when(step == n_steps - 1)(lambda: emit_state(0, last))


def _mixer(zq, zg, w, L, t_real, state, y_dtype, nblk):
    bsz, t = zq.shape[0], zq.shape[1]
    has_state = state is not None
    consts = [_const_spec((1, SMALL_W)), _const_spec((SMALL_W, GLA_QK_W)), _const_spec((1, GLA_QK_W)),
              _const_spec((1, ML_HEADS * ML_DV)), _const_spec((1, GLA_V_W)), _const_spec((GLA_SUB, LANES, LANES))]
    args = [zq, zg, w["gate_bias"], w["w_a2"], w["b_a"], w["mlstm_out_g"], w["gla_out_g"], w["diag_sum"]]
    if has_state:
        assert t == L
        bb, n_steps, scratch = nblk, 1, []
        seq = lambda width: pl.BlockSpec((bb, L, width), lambda b, j: (b, 0, 0))
    else:
        assert t % (nblk * L) == 0 and t_real == t
        bb, n_steps = 1, t // (nblk * L)
        seq = lambda width: pl.BlockSpec((1, nblk * L, width), lambda b, j: (b, j, 0))
        scratch = [pltpu.VMEM((ML_HEADS, ML_DK, ML_DV), F32), pltpu.VMEM((SUBLANES, LANES), F32),
                   pltpu.VMEM((SUBLANES, LANES), F32), pltpu.VMEM((GLA_HEADS, GLA_DK, GLA_DV), F32)]
    per_b = lambda *tail: pl.BlockSpec((bb,) + tail, lambda b, j: (b,) + (0,) * len(tail))
    state_specs = [per_b(ML_HEADS, ML_DK, ML_DV), per_b(ML_HEADS, ML_DK), per_b(1, ML_HEADS),
                   per_b(GLA_HEADS, GLA_DK, GLA_DV)]
    in_specs = [seq(ZQ_W), seq(ZG_W)] + consts
    if has_state:
        in_specs += state_specs
        args += list(state)
    out_shape = [jax.ShapeDtypeStruct((bsz, t, D_MODEL), y_dtype),
                 jax.ShapeDtypeStruct((bsz, ML_HEADS, ML_DK, ML_DV), F32),
                 jax.ShapeDtypeStruct((bsz, ML_HEADS, ML_DK), F32),
                 jax.ShapeDtypeStruct((bsz, 1, ML_HEADS), F32),
                 jax.ShapeDtypeStruct((bsz, GLA_HEADS, GLA_DK, GLA_DV), F32)]
    kern = functools.partial(_mixer_kernel, L=L, t_real=t_real, has_state=has_state, n_steps=n_steps, nblk=nblk)
    return pl.pallas_call(
        kern,
        grid=(bsz // bb, n_steps),
        in_specs=in_specs,
        out_specs=[seq(D_MODEL)] + state_specs,
        out_shape=out_shape,
        scratch_shapes=scratch,
        compiler_params=_params(("arbitrary", "arbitrary")),
        name="mixer_state" if has_state else "mixer",
    )(*args)


def _ffn_in_pieces(x, g1_ref, wg_ref, wu_ref, wd_ref, gm_ref, wq_ref, wgt_ref, x1_ref, zq_out, zg_out):
    h = _rms(x, g1_ref[...]).astype(BF16)
    acts = []
    for lo in range(0, D_FF, MXU_WIDTH):
        g = _dot(h, wg_ref[:, lo:lo + MXU_WIDTH])
        yield
        u = _dot(h, wu_ref[:, lo:lo + MXU_WIDTH])
        yield
        acts.append(((g * jax.nn.sigmoid(g)) * u).astype(BF16))
    a = jnp.concatenate(acts, axis=1)
    down = []
    for lo in range(0, D_MODEL, MXU_WIDTH):
        down.append(_dot(a, wd_ref[:, lo:lo + MXU_WIDTH]))
        yield
    x1 = x + 0.5 * jnp.concatenate(down, axis=1)
    x1_ref[...] = x1
    hm = _rms(x1, gm_ref[...]).astype(BF16)
    for lo in range(0, ZQ_W, MXU_WIDTH):
        zq_out[:, lo:lo + MXU_WIDTH] = _dot(hm, wq_ref[:, lo:lo + MXU_WIDTH]).astype(zq_out.dtype)
        yield
    for lo in range(0, ZG_W, MXU_WIDTH):
        hi = min(lo + MXU_WIDTH, ZG_W)
        zg_out[:, lo:hi] = _dot(hm, wgt_ref[:, lo:hi])
        yield


ATTN_STAGE_EVERY = 7
FFN_PIECES_PER_MIXER_STAGE = 1


def _ffn_mix_kernel(x_ref, g1_ref, wg_ref, wu_ref, wd_ref, gm_ref, wq_ref, wgt_ref, bias_ref, wa2_ref, ba_ref,
                    gml_ref, ggl_ref, ee_ref, qs_ref, ks_ref, vs_ref, x1_ref, ym_ref, co_ref, no_ref, mo_ref, so_ref,
                    os_ref, zq_s, zg_s, c_s, n_s, m_s, s_s, *, tiles_per_batch):
    i = pl.program_id(0)

    @pl.when(i == 0)
    def _init():
        for ref in (zq_s, zg_s, c_s, n_s, m_s, s_s):
            ref[...] = jnp.zeros_like(ref)

    consts = (bias_ref[...], wa2_ref[...], ba_ref[...], gml_ref[...], ggl_ref[...])
    n_chunks = zq_s.shape[0] // CHUNK
    rows = [slice(k * CHUNK, (k + 1) * CHUNK) for k in range(n_chunks)]
    zqs = [zq_s[r, :] for r in rows]
    zgs = [zg_s[r, :] for r in rows]
    starts_batch = (i - 1) % tiles_per_batch == 0
    carry = lambda v: jnp.where(starts_batch, 0.0, v)
    state = dict(c=[carry(c_s[h]) for h in range(ML_HEADS)], n=[carry(n_s[h:h + 1, :]) for h in range(ML_HEADS)],
                 m=[carry(m_s[h:h + 1, 0:1]) for h in range(ML_HEADS)], s=[carry(s_s[h]) for h in range(GLA_HEADS)])

    def mixers():
        st = state
        for k in range(n_chunks):
            ys, new = yield from _mixer_stages([zqs[k]], [zgs[k]], [st], consts, ee_ref, L=CHUNK, t_real=CHUNK,
                                               chain=True)
            ym_ref[rows[k], :] = ys[0].astype(ym_ref.dtype)
            st = new[0]
            yield
        return st

    ffn = _ffn_in_pieces(x_ref[...], g1_ref, wg_ref, wu_ref, wd_ref, gm_ref, wq_ref, wgt_ref, x1_ref, zq_s, zg_s)
    mix = mixers()
    attn = _xattn_cache_stages(qs_ref, ks_ref, vs_ref, os_ref)
    live = {"ffn": True, "attn": True}

    def advance(name, gen):
        if live[name]:
            try:
                next(gen)
            except StopIteration:
                live[name] = False

    slot = 0
    while True:
        try:
            next(mix)
        except StopIteration as done:
            last = done.value
            break
        for _ in range(FFN_PIECES_PER_MIXER_STAGE):
            advance("ffn", ffn)
        if slot % ATTN_STAGE_EVERY == ATTN_STAGE_EVERY // 2:
            advance("attn", attn)
        slot += 1
    for name, gen in (("ffn", ffn), ("attn", attn)):
        if live[name]:
            _drain(gen)

    for h in range(ML_HEADS):
        c_s[h] = last["c"][h]
        n_s[h:h + 1, :] = last["n"][h]
        m_s[h:h + 1, :] = jnp.broadcast_to(last["m"][h], (1, LANES))
        s_s[h] = last["s"][h]

    @pl.when(jnp.logical_and(i >= 1, (i - 1) % tiles_per_batch == tiles_per_batch - 1))
    def _emit_state():
        lane = lax.broadcasted_iota(jnp.int32, (1, LANES), 1)
        for h in range(ML_HEADS):
            co_ref[0, h] = last["c"][h]
            so_ref[0, h] = last["s"][h]
        no_ref[0] = jnp.concatenate(last["n"], axis=0)
        m_row = jnp.zeros((1, LANES), F32)
        for h in range(ML_HEADS):
            m_row = jnp.where(lane == h, last["m"][h], m_row)
        mo_ref[0] = m_row[:, 0:ML_HEADS]


def _ffn_mix(x, w, tm, q_s, k_cache, v_cache):
    bsz, t, _ = x.shape
    tiles_per_batch = t // tm
    n_tiles = bsz * tiles_per_batch
    assert t % tm == 0 and tm % CHUNK == 0
    bs, ts, _ = q_s.shape
    assert bs % n_tiles == 0
    sb = bs // n_tiles
    cur_tile = lambda i: jnp.minimum(i, n_tiles - 1)
    prev_tile = lambda i: jnp.maximum(i - 1, 0)
    cur = lambda width: pl.BlockSpec((tm, width), lambda i: (cur_tile(i), 0))
    per_b = lambda *tail: pl.BlockSpec((1,) + tail, lambda i: (prev_tile(i) // tiles_per_batch,) + (0,) * len(tail))
    state_specs = [per_b(ML_HEADS, ML_DK, ML_DV), per_b(ML_HEADS, ML_DK), per_b(1, ML_HEADS),
                   per_b(GLA_HEADS, GLA_DK, GLA_DV)]
    qo_s = pl.BlockSpec((sb, ts, D_MODEL), lambda i: (cur_tile(i), 0, 0))
    kv_s = pl.BlockSpec((sb, N_MEM * CACHE_ROW_GROUP, LANES), lambda i: (cur_tile(i), 0, 0))
    x1, ym, c_new, n_new, m_new, s_new, o_s = pl.pallas_call(
        functools.partial(_ffn_mix_kernel, tiles_per_batch=tiles_per_batch),
        grid=(n_tiles + 1,),
        in_specs=[cur(D_MODEL), _const_spec((1, D_MODEL)), _const_spec((D_MODEL, D_FF)),
                  _const_spec((D_MODEL, D_FF)), _const_spec((D_FF, D_MODEL)), _const_spec((1, D_MODEL)),
                  _const_spec((D_MODEL, ZQ_W)), _const_spec((D_MODEL, ZG_W)), _const_spec((1, SMALL_W)),
                  _const_spec((SMALL_W, GLA_QK_W)), _const_spec((1, GLA_QK_W)), _const_spec((1, ML_HEADS * ML_DV)),
                  _const_spec((1, GLA_V_W)), _const_spec((GLA_SUB, LANES, LANES)), qo_s, kv_s, kv_s],
        out_specs=[cur(D_MODEL), pl.BlockSpec((tm, D_MODEL), lambda i: (prev_tile(i), 0))] + state_specs + [qo_s],
        out_shape=[jax.ShapeDtypeStruct((n_tiles * tm, D_MODEL), F32),
                   jax.ShapeDtypeStruct((n_tiles * tm, D_MODEL), BF16),
                   jax.ShapeDtypeStruct((bsz, ML_HEADS, ML_DK, ML_DV), F32),
                   jax.ShapeDtypeStruct((bsz, ML_HEADS, ML_DK), F32),
                   jax.ShapeDtypeStruct((bsz, 1, ML_HEADS), F32),
                   jax.ShapeDtypeStruct((bsz, GLA_HEADS, GLA_DK, GLA_DV), F32),
                   jax.ShapeDtypeStruct((bs, ts, D_MODEL), F32)],
        scratch_shapes=[pltpu.VMEM((tm, ZQ_W), BF16), pltpu.VMEM((tm, ZG_W), F32),
                        pltpu.VMEM((ML_HEADS, ML_DK, ML_DV), F32), pltpu.VMEM((SUBLANES, LANES), F32),
                        pltpu.VMEM((SUBLANES, LANES), F32), pltpu.VMEM((GLA_HEADS, GLA_DK, GLA_DV), F32)],
        compiler_params=_params(("arbitrary",)),
        name="ffn_mix",
    )(x.reshape(bsz * t, D_MODEL), w["ffn1_g"], w["ffn1_wg"], w["ffn1_wu"], w["ffn1_wd"], w["mix_g"], w["w_in_q"],
      w["w_in_g"], w["gate_bias"], w["w_a2"], w["b_a"], w["mlstm_out_g"], w["gla_out_g"], w["diag_sum"],
      q_s, _cache_rows_view(k_cache), _cache_rows_view(v_cache))
    return x1, ym, (c_new[None], n_new[None], m_new.reshape(1, bsz, ML_HEADS), s_new[None]), o_s


def _post_mix_kernel(x1_ref, ym_ref, wout_ref, gx_ref, wq_ref, x2_ref, q_ref):
    x2 = x1_ref[...] + _dot(ym_ref[...].astype(BF16), wout_ref[...])
    x2_ref[...] = x2
    hq = _rms(x2, gx_ref[...]).astype(BF16)
    q_ref[...] = _dot(hq, wq_ref[...]).astype(q_ref.dtype)


def _post_mix(x1, ym, w, tm, q_dtype):
    n = x1.shape[0]
    row = lambda: pl.BlockSpec((tm, D_MODEL), lambda i: (i, 0))
    return pl.pallas_call(
        _post_mix_kernel,
        grid=(n // tm,),
        in_specs=[row(), row(), _const_spec((D_MODEL, D_MODEL)), _const_spec((1, D_MODEL)),
                  _const_spec((D_MODEL, D_MODEL))],
        out_specs=[row(), row()],
        out_shape=[jax.ShapeDtypeStruct((n, D_MODEL), F32), jax.ShapeDtypeStruct((n, D_MODEL), q_dtype)],
        compiler_params=_params(("arbitrary",)),
        name="post_mix",
    )(x1, ym, w["w_out"], w["xattn_g"], w["xattn_wq"])


def _softmax(s):
    e = jnp.exp(s - jnp.max(s, axis=-1, keepdims=True))
    return e / jnp.sum(e, axis=-1, keepdims=True)


def _post_fused_kernel(x1_ref, ym_ref, k_ref, v_ref, wout_ref, gx_ref, wq_ref, wo_ref, g2_ref, wg_ref, wu_ref,
                       wd_ref, gf_ref, y_ref):
    x2 = x1_ref[0] + _dot(ym_ref[0].astype(BF16), wout_ref[...])
    q = _dot(_rms(x2, gx_ref[...]).astype(BF16), wq_ref[...]).astype(BF16)
    heads = [slice(h * XA_DH, (h + 1) * XA_DH) for h in range(XA_HEADS)]
    k_full, v_full = _cache_rows_load(k_ref, 0), _cache_rows_load(v_ref, 0)
    scores = [_dot_nt(q[:, hs], k_full[:, hs]) * (XA_DH ** -0.5) for hs in heads]
    probs = [_softmax(s).astype(BF16) for s in scores]
    o = jnp.concatenate([_dot(p, v_full[:, hs]).astype(BF16) for hs, p in zip(heads, probs)], axis=1)
    x3 = x2 + _dot(o, wo_ref[...])
    x4 = _swiglu_residual(x3, g2_ref, wg_ref, wu_ref, wd_ref)
    y_ref[0] = _rms(x4, gf_ref[...])


def _post_fused(x1, ym, k, v, w, tm):
    bsz, t = x1.shape[0], x1.shape[1]
    row = lambda: pl.BlockSpec((1, tm, D_MODEL), lambda b, j: (b, j, 0))
    kv = pl.BlockSpec((1, N_MEM * CACHE_ROW_GROUP, LANES), lambda b, j: (b, 0, 0))
    sq = _const_spec((D_MODEL, D_MODEL))
    vec = _const_spec((1, D_MODEL))
    return pl.pallas_call(
        _post_fused_kernel,
        grid=(bsz, t // tm),
        in_specs=[row(), row(), kv, kv, sq, vec, sq, sq, vec, _const_spec((D_MODEL, D_FF)),
                  _const_spec((D_MODEL, D_FF)), _const_spec((D_FF, D_MODEL)), vec],
        out_specs=row(),
        out_shape=jax.ShapeDtypeStruct((bsz, t, D_MODEL), F32),
        compiler_params=_params(("arbitrary", "arbitrary")),
        name="post_fused",
    )(x1, ym, k, v, w["w_out"], w["xattn_g"], w["xattn_wq"], w["xattn_wo"], w["ffn2_g"], w["ffn2_wg"],
      w["ffn2_wu"], w["ffn2_wd"], w["final_g"])


XA_LANE_TILES = XA_DH // LANES
CACHE_ROW_GROUP = XA_LANE_TILES * XA_HEADS


def _cache_rows_view(x):
    bsz = x.shape[0]
    x = x.reshape(bsz, N_MEM, XA_HEADS, XA_LANE_TILES, LANES)
    return x.transpose(0, 1, 3, 2, 4).reshape(bsz, N_MEM * CACHE_ROW_GROUP, LANES)


def _cache_rows_unview(x):
    bsz = x.shape[0]
    x = x.reshape(bsz, N_MEM, XA_LANE_TILES, XA_HEADS, LANES)
    return x.transpose(0, 1, 3, 2, 4).reshape(bsz, N_MEM, XA_HEADS, XA_DH)


def _cache_rows_store(ref, b, x):
    for h in range(XA_HEADS):
        for lt in range(XA_LANE_TILES):
            lo = h * XA_DH + lt * LANES
            ref[b, pl.ds(lt * XA_HEADS + h, N_MEM, stride=CACHE_ROW_GROUP), :] = x[:, lo:lo + LANES]


def _cache_rows_load(ref, b):
    cols = [ref[b, pl.ds(lt * XA_HEADS + h, N_MEM, stride=CACHE_ROW_GROUP), :]
            for h in range(XA_HEADS) for lt in range(XA_LANE_TILES)]
    return jnp.concatenate(cols, axis=1).astype(BF16)


def _xattn_cache_stages(q_ref, k_ref, v_ref, o_ref):
    bb, tq = q_ref.shape[0], q_ref.shape[1]
    lane_head = lax.broadcasted_iota(jnp.int32, (1, D_MODEL), 1) // XA_DH
    q_bds = [jnp.concatenate([jnp.where(lane_head == h, q_ref[b], 0.0) for h in range(XA_HEADS)], axis=0).astype(BF16)
             for b in range(bb)]
    k_fulls = [_cache_rows_load(k_ref, b) for b in range(bb)]
    yield
    scores = [_dot_nt(q_bds[b], k_fulls[b]) * (XA_DH ** -0.5) for b in range(bb)]
    yield
    p_all = _softmax(jnp.concatenate(scores, axis=0)).astype(BF16)
    v_fulls = [_cache_rows_load(v_ref, b) for b in range(bb)]
    yield
    rows = XA_HEADS * tq
    o_fulls = [_dot(p_all[b * rows:(b + 1) * rows], v_fulls[b]) for b in range(bb)]
    yield
    for b in range(bb):
        o = jnp.zeros((tq, D_MODEL), F32)
        for h in range(XA_HEADS):
            o = jnp.where(lane_head == h, o_fulls[b][h * tq:(h + 1) * tq], o)
        o_ref[b] = o.astype(o_ref.dtype)


def _ffn_out_kernel(x2_ref, o_ref, wo_ref, g2_ref, wg_ref, wu_ref, wd_ref, gf_ref, y_ref):
    x3 = x2_ref[...] + _dot(o_ref[...].astype(BF16), wo_ref[...])
    x4 = _swiglu_residual(x3, g2_ref, wg_ref, wu_ref, wd_ref)
    y_ref[...] = _rms(x4, gf_ref[...])


def _ffn_out(x2, o, w, tm):
    n = x2.shape[0]
    row = lambda: pl.BlockSpec((tm, D_MODEL), lambda i: (i, 0))
    return pl.pallas_call(
        _ffn_out_kernel,
        grid=(n // tm,),
        in_specs=[row(), row(), _const_spec((D_MODEL, D_MODEL)), _const_spec((1, D_MODEL)),
                  _const_spec((D_MODEL, D_FF)), _const_spec((D_MODEL, D_FF)), _const_spec((D_FF, D_MODEL)),
                  _const_spec((1, D_MODEL))],
        out_specs=row(),
        out_shape=jax.ShapeDtypeStruct((n, D_MODEL), F32),
        compiler_params=_params(("arbitrary",)),
        name="ffn_out",
    )(x2, o, w["xattn_wo"], w["ffn2_g"], w["ffn2_wg"], w["ffn2_wu"], w["ffn2_wd"], w["final_g"])


def _memkv_kernel(m_ref, g_ref, wk_ref, wv_ref, k_ref, v_ref):
    hn = _rms(m_ref[0], g_ref[...]).astype(BF16)
    _cache_rows_store(k_ref, 0, _dot(hn, wk_ref[...]))
    _cache_rows_store(v_ref, 0, _dot(hn, wv_ref[...]))


def _memkv(mem, w):
    bsz = mem.shape[0]
    rows = pl.BlockSpec((1, N_MEM * CACHE_ROW_GROUP, LANES), lambda b: (b, 0, 0))
    return pl.pallas_call(
        _memkv_kernel,
        grid=(bsz,),
        in_specs=[pl.BlockSpec((1, N_MEM, D_MODEL), lambda b: (b, 0, 0)), _const_spec((1, D_MODEL)),
                  _const_spec((D_MODEL, D_MODEL)), _const_spec((D_MODEL, D_MODEL))],
        out_specs=[rows, rows],
        out_shape=[jax.ShapeDtypeStruct((bsz, N_MEM * CACHE_ROW_GROUP, LANES), F32)] * 2,
        compiler_params=_params(("arbitrary",)),
        name="memkv",
    )(mem, w["mem_g"], w["xattn_wk"], w["xattn_wv"])


def _prep_weights(p):
    bf = lambda a: a.astype(BF16)
    row = lambda a: a.reshape(1, -1).astype(F32)
    w_in = p["w_in"]
    off = {}
    pos = 0
    for name, width in (("mq", 512), ("mk", 512), ("mv", 512), ("mi", 4), ("mf", 4), ("mo", 512), ("gq", 256),
                        ("gk", 256), ("gv", 512), ("ga", 16), ("gg", 512)):
        off[name] = w_in[:, pos:pos + width]
        pos += width
    pad_cols = SMALL_W - 2 * ML_HEADS - GLA_RANK
    w_in_q = jnp.concatenate([off[k] for k in ("mq", "mk", "mv", "gq", "gk", "gv")], axis=1)
    w_in_g = jnp.concatenate([off["mo"], off["gg"], off["mi"], off["mf"], off["ga"],
                              jnp.zeros((D_MODEL, pad_cols), F32)], axis=1)
    gate_bias = jnp.concatenate([p["mlstm_b_i"], p["mlstm_b_f"], jnp.zeros((SMALL_W - 2 * ML_HEADS,), F32)])
    w_a2 = jnp.zeros((SMALL_W, GLA_QK_W), F32).at[2 * ML_HEADS:2 * ML_HEADS + GLA_RANK].set(p["gla_w_a2"])
    lane = jnp.arange(LANES)
    same_head = (lane[:, None] // GLA_DK) == (lane[None, :] // GLA_DK)
    diag_sum = jnp.stack([same_head & ((lane[None, :] % GLA_SUB) == j) for j in range(GLA_SUB)]).astype(BF16)
    return dict(
        ffn1_g=row(p["ffn1_norm_g"]), ffn1_wg=bf(p["ffn1_w_gate"]), ffn1_wu=bf(p["ffn1_w_up"]),
        ffn1_wd=bf(p["ffn1_w_down"]), mix_g=row(p["mix_norm_g"]), w_in_q=bf(w_in_q), w_in_g=bf(w_in_g),
        gate_bias=row(gate_bias), w_a2=bf(w_a2), b_a=row(p["gla_b_a"]), mlstm_out_g=row(p["mlstm_out_g"]),
        gla_out_g=row(p["gla_out_g"]), diag_sum=diag_sum,
        w_out=bf(p["w_out"]), xattn_g=row(p["xattn_norm_g"]), xattn_wq=bf(p["xattn_w_q"]),
        xattn_wo=bf(p["xattn_w_o"]), mem_g=row(p["mem_norm_g"]), xattn_wk=bf(p["xattn_w_k"]),
        xattn_wv=bf(p["xattn_w_v"]), ffn2_g=row(p["ffn2_norm_g"]), ffn2_wg=bf(p["ffn2_w_gate"]),
        ffn2_wu=bf(p["ffn2_w_up"]), ffn2_wd=bf(p["ffn2_w_down"]), final_g=row(p["final_g"]))


def _ffn_and_mixers(x, state, w, *, chunk, mixer_nblk, act_dtype):
    bsz, t, _ = x.shape
    x1, zq, zg = _ffn_in(x.reshape(bsz * t, D_MODEL), w, TM_FFN_IN, act_dtype)
    zq = zq.reshape(bsz, t, ZQ_W)
    zg = zg.reshape(bsz, t, ZG_W)
    pad = -t % chunk
    if pad:
        zq = jnp.pad(zq, ((0, 0), (0, pad), (0, 0)))
        zg = jnp.pad(zg, ((0, 0), (0, pad), (0, 0)))
    ym, c_new, n_new, m_new, s_new = _mixer(zq, zg, w, chunk, t, state, act_dtype, mixer_nblk)
    return x1, ym[:, :t], (c_new[None], n_new[None], m_new.reshape(1, bsz, ML_HEADS), s_new[None])


def kernel(x_prompt, x_sample, mem_prompt, cache_mem_k, cache_mem_v, state_mlstm_c, state_mlstm_n, state_mlstm_m, state_gla_s, ffn1_norm_g, ffn1_w_gate, ffn1_w_up, ffn1_w_down, mix_norm_g, w_in, mlstm_b_i, mlstm_b_f, mlstm_out_g, gla_w_a2, gla_b_a, gla_out_g, w_out, xattn_norm_g, mem_norm_g, xattn_w_q, xattn_w_k, xattn_w_v, xattn_w_o, ffn2_norm_g, ffn2_w_gate, ffn2_w_up, ffn2_w_down, final_norm_g):
    assert ffn1_norm_g.shape[0] == 1, "single-layer stack"
    layer = dict(ffn1_norm_g=ffn1_norm_g, ffn1_w_gate=ffn1_w_gate, ffn1_w_up=ffn1_w_up, ffn1_w_down=ffn1_w_down,
                 mix_norm_g=mix_norm_g, w_in=w_in, mlstm_b_i=mlstm_b_i, mlstm_b_f=mlstm_b_f,
                 mlstm_out_g=mlstm_out_g, gla_w_a2=gla_w_a2, gla_b_a=gla_b_a, gla_out_g=gla_out_g, w_out=w_out,
                 xattn_norm_g=xattn_norm_g, mem_norm_g=mem_norm_g, xattn_w_q=xattn_w_q, xattn_w_k=xattn_w_k,
                 xattn_w_v=xattn_w_v, xattn_w_o=xattn_w_o, ffn2_norm_g=ffn2_norm_g, ffn2_w_gate=ffn2_w_gate,
                 ffn2_w_up=ffn2_w_up, ffn2_w_down=ffn2_w_down)
    p = {name: arr[0] for name, arr in layer.items()}
    p["final_g"] = final_norm_g
    w = _prep_weights(p)

    bp, tp, _ = x_prompt.shape
    bs, ts, _ = x_sample.shape

    state = (state_mlstm_c[0], state_mlstm_n[0], state_mlstm_m[0].reshape(bs, 1, ML_HEADS), state_gla_s[0])
    x1_s, ym_s, states_s = _ffn_and_mixers(x_sample, state, w, chunk=SUBLANES, mixer_nblk=SAMPLE_MIXER_BATCHES,
                                           act_dtype=F32)
    x2_s, q_s = _post_mix(x1_s, ym_s.reshape(bs * ts, D_MODEL), w, TM_FFN_IN, F32)

    mem_k_p, mem_v_p = _memkv(mem_prompt, w)
    x1_p, ym_p, states_p, o_s = _ffn_mix(x_prompt, w, TM_FFN_IN, q_s.reshape(bs, ts, D_MODEL), cache_mem_k[0],
                                         cache_mem_v[0])
    y_p = _post_fused(x1_p.reshape(bp, tp, D_MODEL), ym_p.reshape(bp, tp, D_MODEL), mem_k_p, mem_v_p, w, TM_POST)
    y_s = _ffn_out(x2_s, o_s.reshape(bs * ts, D_MODEL), w, TM_FFN_IN).reshape(bs, ts, D_MODEL)

    return (y_p, y_s, _cache_rows_unview(mem_k_p)[None], _cache_rows_unview(mem_v_p)[None]) + states_p + states_s
```

```python
import functools

import jax
import jax.numpy as jnp
from jax import lax
from jax.experimental import pallas as pl
from jax.experimental.pallas import tpu as pltpu

F32 = jnp.float32
BF16 = jnp.bfloat16

D_MODEL = 1024
D_FF = 2816
ML_HEADS = 4
ML_DK = 128
ML_DV = 128
GLA_HEADS = 4
GLA_DK = 64
GLA_DV = 128
GLA_RANK = 16
GLA_TAU = 16.0
N_MEM = 256
XA_HEADS = 4
XA_DH = D_MODEL // XA_HEADS
EPS = 1e-6
CHUNK = 64
LOG2_E = 1.4426950408889634
LN_2 = 0.6931471805599453
LANES = 128
SUBLANES = 8

ZQ_MQ, ZQ_MK, ZQ_MV = 0, 512, 1024
ZQ_GQ, ZQ_GK, ZQ_GV = 1536, 1792, 2048
ZQ_W = 2560
ZG_MO, ZG_GG, ZG_SMALL = 0, 512, 1024
ZG_W = 1152
SMALL_W = LANES
GLA_QK_W = GLA_HEADS * GLA_DK
GLA_V_W = GLA_HEADS * GLA_DV
GLA_PAIRS = GLA_HEADS // 2
GLA_SUB = SUBLANES

MXU_WIDTH = 256
FF_CHUNKS = ((0, 6 * MXU_WIDTH), (6 * MXU_WIDTH, D_FF))
VMEM_LIMIT_BYTES = 56 * 1024 * 1024

TM_FFN_IN = MXU_WIDTH
TM_POST = 2 * MXU_WIDTH
SAMPLE_MIXER_BATCHES = 8


def _rms(x, g):
    return x * lax.rsqrt(jnp.mean(x * x, axis=-1, keepdims=True) + EPS) * g


def _log_sigmoid(x):
    return jnp.minimum(x, 0.0) - jnp.log1p(jnp.exp(-jnp.abs(x)))


def _dot(a, b):
    return jnp.dot(a, b, preferred_element_type=F32)


def _dot_nt(a, b):
    return lax.dot_general(a, b, (((1,), (1,)), ((), ())), preferred_element_type=F32)


def _dot_f32(a, b):
    return jnp.dot(a, b, precision=lax.Precision.HIGHEST, preferred_element_type=F32)


def _swiglu_residual(x, g_ref, wg_ref, wu_ref, wd_ref):
    h = _rms(x, g_ref[...]).astype(BF16)
    acts = []
    for lo, hi in FF_CHUNKS:
        g = _dot(h, wg_ref[:, lo:hi])
        u = _dot(h, wu_ref[:, lo:hi])
        acts.append(((g * jax.nn.sigmoid(g)) * u).astype(BF16))
    acc = jnp.zeros_like(x)
    for (lo, hi), a in zip(FF_CHUNKS, acts):
        acc = acc + _dot(a, wd_ref[lo:hi, :])
    return x + 0.5 * acc


def _const_spec(shape):
    nd = len(shape)
    return pl.BlockSpec(shape, lambda *_: (0,) * nd, pipeline_mode=pl.Buffered(1))


def _params(sem):
    return pltpu.CompilerParams(dimension_semantics=sem, vmem_limit_bytes=VMEM_LIMIT_BYTES)


def _ffn_in_kernel(x_ref, g1_ref, wg_ref, wu_ref, wd_ref, gm_ref, wq_ref, wgt_ref, x1_ref, zq_ref, zg_ref):
    x1 = _swiglu_residual(x_ref[...], g1_ref, wg_ref, wu_ref, wd_ref)
    x1_ref[...] = x1
    hm = _rms(x1, gm_ref[...]).astype(BF16)
    zq_ref[...] = _dot(hm, wq_ref[...]).astype(zq_ref.dtype)
    zg_ref[...] = _dot(hm, wgt_ref[...])


def _ffn_in(x, w, tm, zq_dtype):
    n = x.shape[0]
    row = lambda width: pl.BlockSpec((tm, width), lambda i: (i, 0))
    return pl.pallas_call(
        _ffn_in_kernel,
        grid=(n // tm,),
        in_specs=[row(D_MODEL), _const_spec((1, D_MODEL)), _const_spec((D_MODEL, D_FF)),
                  _const_spec((D_MODEL, D_FF)), _const_spec((D_FF, D_MODEL)), _const_spec((1, D_MODEL)),
                  _const_spec((D_MODEL, ZQ_W)), _const_spec((D_MODEL, ZG_W))],
        out_specs=[row(D_MODEL), row(ZQ_W), row(ZG_W)],
        out_shape=[jax.ShapeDtypeStruct((n, D_MODEL), F32), jax.ShapeDtypeStruct((n, ZQ_W), zq_dtype),
                   jax.ShapeDtypeStruct((n, ZG_W), F32)],
        compiler_params=_params(("arbitrary",)),
        name="ffn_in",
    )(x, w["ffn1_g"], w["ffn1_wg"], w["ffn1_wu"], w["ffn1_wd"], w["mix_g"], w["w_in_q"], w["w_in_g"])


def _mixer_stages(zqs, zgs, states, consts, ee_ref, *, L, t_real, chain):
    bias, wa2, ba, gml, ggl = consts
    nchunk = len(zqs)
    groups = [(b, h) for b in range(nchunk) for h in range(ML_HEADS)]
    pairs = [(b, p) for b in range(nchunk) for p in range(GLA_PAIRS)]
    padded = t_real < L
    valid = lax.broadcasted_iota(jnp.int32, (L, 1), 0) < t_real
    rr = lax.broadcasted_iota(jnp.int32, (L, L), 0)
    cc = lax.broadcasted_iota(jnp.int32, (L, L), 1)
    tril = cc <= rr
    c = GLA_SUB
    nb = L // c
    lane = lax.broadcasted_iota(jnp.int32, (1, LANES), 1)
    lane_blk = (lane % GLA_DK) // c
    lane_head = lane // GLA_DK
    v_lane_head = lax.broadcasted_iota(jnp.int32, (1, 2 * GLA_DV), 1) // GLA_DV
    row_blk = (lax.broadcasted_iota(jnp.int32, (2 * L, 1), 0) % L) // c
    t_in = lax.broadcasted_iota(jnp.int32, (1, c, 1), 1)
    new_states = [dict(c=[None] * ML_HEADS, n=[None] * ML_HEADS, m=[None] * ML_HEADS, s=[None] * GLA_HEADS)
                  for _ in range(nchunk)]
    prev = lambda b: new_states[b - 1] if chain and b > 0 else states[0 if chain else b]
    known = lambda b: not chain or b == 0

    carried = {}

    def carried_mlstm(b, h):
        st = prev(b)
        d = ml[(b, h)]
        carried[(b, h)] = _dot(d["qb"], st["c"][h].astype(BF16))

    def carried_gla(b, p):
        s_prev = prev(b)["s"]
        zero_blk = jnp.zeros((GLA_DK, GLA_DV), BF16)
        s_bd = jnp.concatenate(
            [jnp.concatenate([s_prev[2 * p].astype(BF16), zero_blk], axis=1),
             jnp.concatenate([zero_blk, s_prev[2 * p + 1].astype(BF16)], axis=1)], axis=0)
        carried[(b, "gla", p)] = _dot(gl[(b, p)]["q_dec"], s_bd)

    smalls, sms, lfs, b_cols, b_rows, sm_ts, las, bcs = [], [], [], [], [], [], [], []
    for b in range(nchunk):
        small = zgs[b][:, ZG_SMALL:ZG_SMALL + SMALL_W]
        sm = small + bias
        lf = _log_sigmoid(sm) * LOG2_E
        sm = sm * LOG2_E
        if padded:
            sm = jnp.where(valid, sm, -jnp.inf)
            lf = jnp.where(valid, lf, 0.0)
        smalls.append(small)
        sms.append(sm)
        lfs.append(lf)

    long_chunk = L > SUBLANES
    col_w = 1 if long_chunk else LANES
    tril_f, triu_f = tril.astype(F32), (rr <= cc).astype(F32)

    def cumsum_rows(x):
        if long_chunk:
            return _dot_f32(tril_f, x)
        acc = x[0:1]
        rows = [acc]
        for r in range(1, L):
            acc = acc + x[r:r + 1]
            rows.append(acc)
        return jnp.concatenate(rows, axis=0)

    lf_ts = [lf.T[0:SUBLANES] for lf in lfs] if long_chunk else None
    yield
    for b in range(nchunk):
        b_cols.append(cumsum_rows(lfs[b]))
        b_rows.append(_dot_f32(lf_ts[b], triu_f) if long_chunk else b_cols[b].T[0:SUBLANES])
        sm_ts.append(sms[b].T)
        la = _log_sigmoid(_dot(smalls[b].astype(BF16), wa2) + ba) * (LOG2_E / GLA_TAU)
        las.append(jnp.where(valid, la, 0.0) if padded else la)

    ml = {}
    for g in groups:
        b, h = g
        zq = zqs[b]
        qf = zq[:, ZQ_MQ + h * ML_DK:ZQ_MQ + (h + 1) * ML_DK].astype(F32)
        kf = zq[:, ZQ_MK + h * ML_DK:ZQ_MK + (h + 1) * ML_DK].astype(F32) * (ML_DK ** -0.5)
        vf = zq[:, ZQ_MV + h * ML_DV:ZQ_MV + (h + 1) * ML_DV].astype(F32)
        if padded:
            kf = jnp.where(valid, kf, 0.0)
            vf = jnp.where(valid, vf, 0.0)
        ml[g] = dict(qf=qf, qb=qf.astype(BF16), kf=kf, kb=kf.astype(BF16), vb=vf.astype(BF16))
    yield
    for b in range(nchunk):
        bcs.append(cumsum_rows(las[b]))
    for g in groups:
        d = ml[g]
        d["qk"] = _dot_nt(d["qb"], d["kb"])
        if known(g[0]):
            carried_mlstm(*g)

    yield
    for g in groups:
        b, h = g
        d = ml[g]
        i_col = jnp.broadcast_to(sms[b][:, h:h + 1], (L, col_w))
        b_col = jnp.broadcast_to(b_cols[b][:, ML_HEADS + h:ML_HEADS + h + 1], (L, col_w))
        b_row = b_rows[b][ML_HEADS + h:ML_HEADS + h + 1, :]
        i_row = sm_ts[b][h:h + 1, :]
        a_col = b_col + prev(b)["m"][h] * LOG2_E
        dm = jnp.where(tril, b_col[:, :L] - (b_row - i_row), -jnp.inf)
        mt = jnp.maximum(a_col, jnp.max(dm, axis=1, keepdims=True))
        w_inter = jnp.exp2(a_col - mt)
        s = d["qk"] * jnp.exp2(dm - mt[:, :L])
        kw = d["kf"] * jnp.exp2((b_col[L - 1:L] - mt[L - 1:L]) - (b_col - i_col))
        d.update(mt=mt, w_inter=w_inter, s=s, kw=kw, kw_t=kw.T.astype(BF16))
        new_states[b]["m"][h] = mt[L - 1:L, 0:1] * LN_2

    gl = {}
    for b in range(nchunk):
        zq = zqs[b]
        gq = zq[:, ZQ_GQ:ZQ_GQ + GLA_QK_W].astype(F32) * (GLA_DK ** -0.5)
        gk = zq[:, ZQ_GK:ZQ_GK + GLA_QK_W].astype(F32)
        gv = zq[:, ZQ_GV:ZQ_GV + GLA_V_W].astype(F32)
        if padded:
            gk = jnp.where(valid, gk, 0.0)
            gv = jnp.where(valid, gv, 0.0)
        stack = lambda x: jnp.concatenate([x[:, :LANES], x[:, LANES:]], axis=0)
        q2, k2, b2 = stack(gq), stack(gk), stack(bcs[b])
        q3 = q2.reshape(2 * nb, c, LANES)
        k3 = k2.reshape(2 * nb, c, LANES)
        b3 = b2.reshape(2 * nb, c, LANES)
        pair_terms = []
        for j in range(min(c, t_real)):
            decay = jnp.exp2(jnp.where(t_in >= j, b3 - b3[:, j:j + 1, :], -jnp.inf))
            pair_terms.append((q3 * k3[:, j:j + 1, :] * decay).reshape(2 * L, LANES).astype(BF16))
        kt2 = (k3 * jnp.exp2(b3[:, c - 1:c, :] - b3)).reshape(2 * L, LANES) if nb > 1 else None
        gl[b] = dict(gv=gv, q2=q2, k2=k2, b2=b2, pair_terms=pair_terms, kt2=kt2)
    yield
    for b in range(nchunk):
        acc = jnp.zeros((2 * L, LANES), F32)
        for j, pair_j in enumerate(gl[b]["pair_terms"]):
            acc = acc + _dot(pair_j, ee_ref[j])
        gl[b]["a_diag"] = jnp.where(lane_blk == row_blk, acc, 0.0)

    for bp in pairs:
        b, p = bp
        d = gl[b]
        rows_p = slice(p * L, (p + 1) * L)
        q_p, k_p, b_p = d["q2"][rows_p], d["k2"][rows_p], d["b2"][rows_p]
        e = dict(q_dec=(q_p * jnp.exp2(b_p)).astype(BF16),
                 kh_t=(k_p * jnp.exp2(b_p[L - 1:L] - b_p)).T.astype(BF16),
                 decay_col=jnp.exp2(b_p[L - SUBLANES:L].T[:, SUBLANES - 1:SUBLANES]))
        if nb > 1:
            kt_p = d["kt2"][rows_p].astype(BF16)
            k_bd = jnp.concatenate([jnp.where(lane_head == hh, kt_p, 0.0) for hh in range(2)], axis=0)
            slabs, offs = [], []
            off = 0
            for j in range(nb - 1):
                lo = (j + 1) * c
                slabs.append(q_p[lo:] * jnp.exp2(b_p[lo:] - b_p[lo - 1:lo]))
                offs.append(off)
                off += L - lo
            e.update(q_var=jnp.concatenate(slabs, axis=0).astype(BF16), k_bd=k_bd, offs=offs)
        gl[bp] = e
    yield
    for bp in pairs:
        e = gl[bp]
        if nb > 1:
            e["r"] = _dot_nt(e["q_var"], e["k_bd"])
        if known(bp[0]):
            carried_gla(*bp)

    for g in groups:
        d = ml[g]
        d["sv"] = _dot(d["s"].astype(BF16), d["vb"])
        d["c_upd"] = _dot(d["kw_t"], d["vb"])
        d["den"] = jnp.sum(d["s"], axis=1, keepdims=True)
        if known(g[0]):
            d["qn"] = jnp.sum(d["qf"] * prev(g[0])["n"][g[1]], axis=1, keepdims=True)
    yield
    for bp in pairs:
        b, p = bp
        d, e = gl[b], gl[bp]
        a_p = d["a_diag"][p * L:(p + 1) * L]
        if nb > 1:
            blocks = []
            for i in range(nb):
                blk = a_p[i * c:(i + 1) * c]
                for j in range(i):
                    lo_r = e["offs"][j] + (i - j - 1) * c
                    blk = jnp.where(lane_blk == j, e["r"][lo_r:lo_r + c], blk)
                blocks.append(blk)
            a_p = jnp.concatenate(blocks, axis=0)
        v_f = d["gv"][:, p * 2 * GLA_DV:(p + 1) * 2 * GLA_DV]
        v_p = v_f.astype(BF16)
        if L < GLA_DK:
            v_rows = []
            for hh in range(2):
                v_rows += [jnp.where(v_lane_head == hh, v_f, 0.0), jnp.zeros((GLA_DK - L, 2 * GLA_DV), F32)]
            v_bd = jnp.concatenate(v_rows, axis=0).astype(BF16)
        else:
            v_bd = jnp.concatenate([jnp.where(v_lane_head == hh, v_p, 0.0) for hh in range(2)], axis=0)
        e["s_upd"] = [_dot(e["kh_t"][hh * GLA_DK:(hh + 1) * GLA_DK], v_p[:, hh * GLA_DV:(hh + 1) * GLA_DV])
                      for hh in range(2)]
        e.update(scores=a_p.astype(BF16), v_bd=v_bd)
    yield
    for bp in pairs:
        e = gl[bp]
        e["o_intra"] = _dot(e["scores"], e["v_bd"])

    yield
    ys = [[None] * (ML_HEADS + GLA_HEADS) for _ in range(nchunk)]

    def finish_mlstm(b):
        st = prev(b)
        for h in range(ML_HEADS):
            d = ml[(b, h)]
            mt, w_inter = d["mt"], d["w_inter"]
            qn = d["qn"] if known(b) else jnp.sum(d["qf"] * st["n"][h], axis=1, keepdims=True)
            den = d["den"] + w_inter * qn
            hh = (d["sv"] + w_inter * carried[(b, h)]) / jnp.maximum(jnp.abs(den), jnp.exp2(-mt))
            i_last = w_inter[L - 1:L]
            new_states[b]["c"][h] = i_last * st["c"][h] + d["c_upd"]
            new_states[b]["n"][h] = i_last * st["n"][h] + jnp.sum(d["kw"], axis=0, keepdims=True)
            hs = slice(h * ML_DV, (h + 1) * ML_DV)
            yn = hh * lax.rsqrt(jnp.mean(hh * hh, axis=-1, keepdims=True) + EPS) * gml[:, hs]
            ys[b][h] = jax.nn.sigmoid(zgs[b][:, ZG_MO + h * ML_DV:ZG_MO + (h + 1) * ML_DV]) * yn

    def finish_gla(b):
        st = prev(b)
        for p in range(GLA_PAIRS):
            e = gl[(b, p)]
            o = carried[(b, "gla", p)] + e["o_intra"]
            for hh in range(2):
                h = 2 * p + hh
                ds = slice(hh * GLA_DK, (hh + 1) * GLA_DK)
                vs = slice(hh * GLA_DV, (hh + 1) * GLA_DV)
                new_states[b]["s"][h] = e["decay_col"][ds] * st["s"][h] + e["s_upd"][hh]
                oh = o[:, vs]
                yn = oh * lax.rsqrt(jnp.mean(oh * oh, axis=-1, keepdims=True) + EPS) * ggl[:, h * GLA_DV:(h + 1) * GLA_DV]
                gg = zgs[b][:, ZG_GG + h * GLA_DV:ZG_GG + (h + 1) * GLA_DV]
                ys[b][ML_HEADS + h] = (gg * jax.nn.sigmoid(gg)) * yn

    if chain:
        for b in range(nchunk):
            if not known(b):
                for h in range(ML_HEADS):
                    carried_mlstm(b, h)
                for p in range(GLA_PAIRS):
                    carried_gla(b, p)
            finish_mlstm(b)
            finish_gla(b)
    else:
        for b in range(nchunk):
            finish_mlstm(b)
        for b in range(nchunk):
            finish_gla(b)
    return [jnp.concatenate(y, axis=1) for y in ys], new_states


def _drain(gen):
    try:
        while True:
            next(gen)
    except StopIteration as done:
        return done.value


def _mixer_compute(*args, **kwargs):
    return _drain(_mixer_stages(*args, **kwargs))


def _mixer_state_kernel(zq_ref, zg_ref, bias_ref, wa2_ref, ba_ref, gml_ref, ggl_ref, ee_ref, c0_ref, n0_ref, m0_ref,
                        s0_ref, y_ref, co_ref, no_ref, mo_ref, so_ref, *, L, t_real, nblk):
    consts = (bias_ref[...], wa2_ref[...], ba_ref[...], gml_ref[...], ggl_ref[...])
    lane = lax.broadcasted_iota(jnp.int32, (1, LANES), 1)

    def chunk_rows(ref, b):
        rows = ref[b * t_real:(b + 1) * t_real, :]
        return jnp.concatenate([rows, jnp.zeros((L - t_real, rows.shape[1]), rows.dtype)], axis=0)

    states = []
    for b in range(nblk):
        n_all = n0_ref[b]
        m_all = m0_ref[b]
        states.append(dict(c=[c0_ref[b, h] for h in range(ML_HEADS)],
                           n=[n_all[h:h + 1, :] for h in range(ML_HEADS)],
                           m=[m_all[:, h:h + 1] for h in range(ML_HEADS)],
                           s=[s0_ref[b, h] for h in range(GLA_HEADS)]))
    ys, new_states = _mixer_compute([chunk_rows(zq_ref, b) for b in range(nblk)],
                                    [chunk_rows(zg_ref, b) for b in range(nblk)], states, consts, ee_ref, L=L,
                                    t_real=t_real, chain=False)
    for b in range(nblk):
        st = new_states[b]
        y_ref[b * t_real:(b + 1) * t_real, :] = ys[b][0:t_real].astype(y_ref.dtype)
        for h in range(ML_HEADS):
            co_ref[b, h] = st["c"][h]
            so_ref[b, h] = st["s"][h]
        no_ref[b] = jnp.concatenate(st["n"], axis=0)
        m_row = jnp.zeros((1, LANES), F32)
        for h in range(ML_HEADS):
            m_row = jnp.where(lane == h, st["m"][h], m_row)
        mo_ref[b] = m_row[:, 0:ML_HEADS]


def _mixer_state(zq, zg, w, state, t_real, nblk):
    bsz = zq.shape[0] // t_real
    L = -(-t_real // SUBLANES) * SUBLANES
    rows = lambda width: pl.BlockSpec((nblk * t_real, width), lambda b: (b, 0))
    per_b = lambda *tail: pl.BlockSpec((nblk,) + tail, lambda b: (b,) + (0,) * len(tail))
    state_specs = [per_b(ML_HEADS, ML_DK, ML_DV), per_b(ML_HEADS, ML_DK), per_b(1, ML_HEADS),
                   per_b(GLA_HEADS, GLA_DK, GLA_DV)]
    consts = [_const_spec((1, SMALL_W)), _const_spec((SMALL_W, GLA_QK_W)), _const_spec((1, GLA_QK_W)),
              _const_spec((1, ML_HEADS * ML_DV)), _const_spec((1, GLA_V_W)), _const_spec((GLA_SUB, LANES, LANES))]
    return pl.pallas_call(
        functools.partial(_mixer_state_kernel, L=L, t_real=t_real, nblk=nblk),
        grid=(bsz // nblk,),
        in_specs=[rows(ZQ_W), rows(ZG_W)] + consts + state_specs,
        out_specs=[rows(D_MODEL)] + state_specs,
        out_shape=[jax.ShapeDtypeStruct((bsz * t_real, D_MODEL), F32),
                   jax.ShapeDtypeStruct((bsz, ML_HEADS, ML_DK, ML_DV), F32),
                   jax.ShapeDtypeStruct((bsz, ML_HEADS, ML_DK), F32),
                   jax.ShapeDtypeStruct((bsz, 1, ML_HEADS), F32),
                   jax.ShapeDtypeStruct((bsz, GLA_HEADS, GLA_DK, GLA_DV), F32)],
        compiler_params=_params(("arbitrary",)),
        name="mixer_state",
    )(zq, zg, w["gate_bias"], w["w_a2"], w["b_a"], w["mlstm_out_g"], w["gla_out_g"], w["diag_sum"], *state)


def _ffn_in_pieces(x, g1_ref, wg_ref, wu_ref, wd_ref, gm_ref, wq_ref, wgt_ref, x1_ref, zq_out, zg_out):
    h = _rms(x, g1_ref[...]).astype(BF16)
    acts = []
    for lo in range(0, D_FF, MXU_WIDTH):
        g = _dot(h, wg_ref[:, lo:lo + MXU_WIDTH])
        yield
        u = _dot(h, wu_ref[:, lo:lo + MXU_WIDTH])
        yield
        acts.append(((g * jax.nn.sigmoid(g)) * u).astype(BF16))
    a = jnp.concatenate(acts, axis=1)
    down = []
    for lo in range(0, D_MODEL, MXU_WIDTH):
        down.append(_dot(a, wd_ref[:, lo:lo + MXU_WIDTH]))
        yield
    x1 = x + 0.5 * jnp.concatenate(down, axis=1)
    x1_ref[...] = x1
    hm = _rms(x1, gm_ref[...]).astype(BF16)
    for lo in range(0, ZQ_W, MXU_WIDTH):
        zq_out[:, lo:lo + MXU_WIDTH] = _dot(hm, wq_ref[:, lo:lo + MXU_WIDTH]).astype(zq_out.dtype)
        yield
    for lo in range(0, ZG_W, MXU_WIDTH):
        hi = min(lo + MXU_WIDTH, ZG_W)
        zg_out[:, lo:hi] = _dot(hm, wgt_ref[:, lo:hi])
        yield


ATTN_STAGE_EVERY = 7
FFN_PIECES_PER_MIXER_STAGE = 1


def _ffn_mix_kernel(x_ref, g1_ref, wg_ref, wu_ref, wd_ref, gm_ref, wq_ref, wgt_ref, bias_ref, wa2_ref, ba_ref,
                    gml_ref, ggl_ref, ee_ref, qs_ref, ks_ref, vs_ref, x1_ref, ym_ref, co_ref, no_ref, mo_ref, so_ref,
                    os_ref, zq_s, zg_s, c_s, n_s, m_s, s_s, *, tiles_per_batch):
    i = pl.program_id(0)

    @pl.when(i == 0)
    def _init():
        for ref in (zq_s, zg_s, c_s, n_s, m_s, s_s):
            ref[...] = jnp.zeros_like(ref)

    consts = (bias_ref[...], wa2_ref[...], ba_ref[...], gml_ref[...], ggl_ref[...])
    n_chunks = zq_s.shape[0] // CHUNK
    rows = [slice(k * CHUNK, (k + 1) * CHUNK) for k in range(n_chunks)]
    zqs = [zq_s[r, :] for r in rows]
    zgs = [zg_s[r, :] for r in rows]
    starts_batch = (i - 1) % tiles_per_batch == 0
    carry = lambda v: jnp.where(starts_batch, 0.0, v)
    state = dict(c=[carry(c_s[h]) for h in range(ML_HEADS)], n=[carry(n_s[h:h + 1, :]) for h in range(ML_HEADS)],
                 m=[carry(m_s[h:h + 1, 0:1]) for h in range(ML_HEADS)], s=[carry(s_s[h]) for h in range(GLA_HEADS)])

    def mixers():
        st = state
        for k in range(n_chunks):
            ys, new = yield from _mixer_stages([zqs[k]], [zgs[k]], [st], consts, ee_ref, L=CHUNK, t_real=CHUNK,
                                               chain=True)
            ym_ref[rows[k], :] = ys[0].astype(ym_ref.dtype)
            st = new[0]
            yield
        return st

    ffn = _ffn_in_pieces(x_ref[...], g1_ref, wg_ref, wu_ref, wd_ref, gm_ref, wq_ref, wgt_ref, x1_ref, zq_s, zg_s)
    mix = mixers()
    attn = _xattn_cache_stages(qs_ref, ks_ref, vs_ref, os_ref)
    live = {"ffn": True, "attn": True}

    def advance(name, gen):
        if live[name]:
            try:
                next(gen)
            except StopIteration:
                live[name] = False

    slot = 0
    while True:
        try:
            next(mix)
        except StopIteration as done:
            last = done.value
            break
        for _ in range(FFN_PIECES_PER_MIXER_STAGE):
            advance("ffn", ffn)
        if slot % ATTN_STAGE_EVERY == ATTN_STAGE_EVERY // 2:
            advance("attn", attn)
        slot += 1
    for name, gen in (("ffn", ffn), ("attn", attn)):
        if live[name]:
            _drain(gen)

    for h in range(ML_HEADS):
        c_s[h] = last["c"][h]
        n_s[h:h + 1, :] = last["n"][h]
        m_s[h:h + 1, :] = jnp.broadcast_to(last["m"][h], (1, LANES))
        s_s[h] = last["s"][h]

    @pl.when(jnp.logical_and(i >= 1, (i - 1) % tiles_per_batch == tiles_per_batch - 1))
    def _emit_state():
        lane = lax.broadcasted_iota(jnp.int32, (1, LANES), 1)
        for h in range(ML_HEADS):
            co_ref[0, h] = last["c"][h]
            so_ref[0, h] = last["s"][h]
        no_ref[0] = jnp.concatenate(last["n"], axis=0)
        m_row = jnp.zeros((1, LANES), F32)
        for h in range(ML_HEADS):
            m_row = jnp.where(lane == h, last["m"][h], m_row)
        mo_ref[0] = m_row[:, 0:ML_HEADS]


def _ffn_mix(x, w, tm, q_s, k_cache, v_cache):
    bsz, t, _ = x.shape
    tiles_per_batch = t // tm
    n_tiles = bsz * tiles_per_batch
    assert t % tm == 0 and tm % CHUNK == 0
    bs = k_cache.shape[0]
    ts = q_s.shape[0] // bs
    assert bs % n_tiles == 0
    sb = bs // n_tiles
    cur_tile = lambda i: jnp.minimum(i, n_tiles - 1)
    prev_tile = lambda i: jnp.maximum(i - 1, 0)
    cur = lambda width: pl.BlockSpec((tm, width), lambda i: (cur_tile(i), 0))
    per_b = lambda *tail: pl.BlockSpec((1,) + tail, lambda i: (prev_tile(i) // tiles_per_batch,) + (0,) * len(tail))
    state_specs = [per_b(ML_HEADS, ML_DK, ML_DV), per_b(ML_HEADS, ML_DK), per_b(1, ML_HEADS),
                   per_b(GLA_HEADS, GLA_DK, GLA_DV)]
    qo_s = pl.BlockSpec((sb * ts, D_MODEL), lambda i: (cur_tile(i), 0))
    kv_s = pl.BlockSpec((sb, N_MEM * CACHE_ROW_GROUP, LANES), lambda i: (cur_tile(i), 0, 0))
    x1, ym, c_new, n_new, m_new, s_new, o_s = pl.pallas_call(
        functools.partial(_ffn_mix_kernel, tiles_per_batch=tiles_per_batch),
        grid=(n_tiles + 1,),
        in_specs=[cur(D_MODEL), _const_spec((1, D_MODEL)), _const_spec((D_MODEL, D_FF)),
                  _const_spec((D_MODEL, D_FF)), _const_spec((D_FF, D_MODEL)), _const_spec((1, D_MODEL)),
                  _const_spec((D_MODEL, ZQ_W)), _const_spec((D_MODEL, ZG_W)), _const_spec((1, SMALL_W)),
                  _const_spec((SMALL_W, GLA_QK_W)), _const_spec((1, GLA_QK_W)), _const_spec((1, ML_HEADS * ML_DV)),
                  _const_spec((1, GLA_V_W)), _const_spec((GLA_SUB, LANES, LANES)), qo_s, kv_s, kv_s],
        out_specs=[cur(D_MODEL), pl.BlockSpec((tm, D_MODEL), lambda i: (prev_tile(i), 0))] + state_specs + [qo_s],
        out_shape=[jax.ShapeDtypeStruct((n_tiles * tm, D_MODEL), F32),
                   jax.ShapeDtypeStruct((n_tiles * tm, D_MODEL), BF16),
                   jax.ShapeDtypeStruct((bsz, ML_HEADS, ML_DK, ML_DV), F32),
                   jax.ShapeDtypeStruct((bsz, ML_HEADS, ML_DK), F32),
                   jax.ShapeDtypeStruct((bsz, 1, ML_HEADS), F32),
                   jax.ShapeDtypeStruct((bsz, GLA_HEADS, GLA_DK, GLA_DV), F32),
                   jax.ShapeDtypeStruct((bs * ts, D_MODEL), F32)],
        scratch_shapes=[pltpu.VMEM((tm, ZQ_W), BF16), pltpu.VMEM((tm, ZG_W), F32),
                        pltpu.VMEM((ML_HEADS, ML_DK, ML_DV), F32), pltpu.VMEM((SUBLANES, LANES), F32),
                        pltpu.VMEM((SUBLANES, LANES), F32), pltpu.VMEM((GLA_HEADS, GLA_DK, GLA_DV), F32)],
        compiler_params=_params(("arbitrary",)),
        name="ffn_mix",
    )(x.reshape(bsz * t, D_MODEL), w["ffn1_g"], w["ffn1_wg"], w["ffn1_wu"], w["ffn1_wd"], w["mix_g"], w["w_in_q"],
      w["w_in_g"], w["gate_bias"], w["w_a2"], w["b_a"], w["mlstm_out_g"], w["gla_out_g"], w["diag_sum"],
      q_s, _cache_rows_view(k_cache), _cache_rows_view(v_cache))
    return x1, ym, (c_new[None], n_new[None], m_new.reshape(1, bsz, ML_HEADS), s_new[None]), o_s


def _post_mix_kernel(x1_ref, ym_ref, wout_ref, gx_ref, wq_ref, x2_ref, q_ref):
    x2 = x1_ref[...] + _dot(ym_ref[...].astype(BF16), wout_ref[...])
    x2_ref[...] = x2
    hq = _rms(x2, gx_ref[...]).astype(BF16)
    q_ref[...] = _dot(hq, wq_ref[...]).astype(q_ref.dtype)


def _post_mix(x1, ym, w, tm, q_dtype):
    n = x1.shape[0]
    row = lambda: pl.BlockSpec((tm, D_MODEL), lambda i: (i, 0))
    return pl.pallas_call(
        _post_mix_kernel,
        grid=(n // tm,),
        in_specs=[row(), row(), _const_spec((D_MODEL, D_MODEL)), _const_spec((1, D_MODEL)),
                  _const_spec((D_MODEL, D_MODEL))],
        out_specs=[row(), row()],
        out_shape=[jax.ShapeDtypeStruct((n, D_MODEL), F32), jax.ShapeDtypeStruct((n, D_MODEL), q_dtype)],
        compiler_params=_params(("arbitrary",)),
        name="post_mix",
    )(x1, ym, w["w_out"], w["xattn_g"], w["xattn_wq"])


def _softmax(s):
    e = jnp.exp(s - jnp.max(s, axis=-1, keepdims=True))
    return e / jnp.sum(e, axis=-1, keepdims=True)


def _post_fused_kernel(x1_ref, ym_ref, k_ref, v_ref, wout_ref, gx_ref, wq_ref, wo_ref, g2_ref, wg_ref, wu_ref,
                       wd_ref, gf_ref, y_ref):
    x2 = x1_ref[0] + _dot(ym_ref[0].astype(BF16), wout_ref[...])
    q = _dot(_rms(x2, gx_ref[...]).astype(BF16), wq_ref[...]).astype(BF16)
    heads = [slice(h * XA_DH, (h + 1) * XA_DH) for h in range(XA_HEADS)]
    k_full, v_full = _cache_rows_load(k_ref, 0), _cache_rows_load(v_ref, 0)
    scores = [_dot_nt(q[:, hs], k_full[:, hs]) * (XA_DH ** -0.5) for hs in heads]
    probs = [_softmax(s).astype(BF16) for s in scores]
    o = jnp.concatenate([_dot(p, v_full[:, hs]).astype(BF16) for hs, p in zip(heads, probs)], axis=1)
    x3 = x2 + _dot(o, wo_ref[...])
    x4 = _swiglu_residual(x3, g2_ref, wg_ref, wu_ref, wd_ref)
    y_ref[0] = _rms(x4, gf_ref[...])


def _post_fused(x1, ym, k, v, w, tm):
    bsz, t = x1.shape[0], x1.shape[1]
    row = lambda: pl.BlockSpec((1, tm, D_MODEL), lambda b, j: (b, j, 0))
    kv = pl.BlockSpec((1, N_MEM * CACHE_ROW_GROUP, LANES), lambda b, j: (b, 0, 0))
    sq = _const_spec((D_MODEL, D_MODEL))
    vec = _const_spec((1, D_MODEL))
    return pl.pallas_call(
        _post_fused_kernel,
        grid=(bsz, t // tm),
        in_specs=[row(), row(), kv, kv, sq, vec, sq, sq, vec, _const_spec((D_MODEL, D_FF)),
                  _const_spec((D_MODEL, D_FF)), _const_spec((D_FF, D_MODEL)), vec],
        out_specs=row(),
        out_shape=jax.ShapeDtypeStruct((bsz, t, D_MODEL), F32),
        compiler_params=_params(("arbitrary", "arbitrary")),
        name="post_fused",
    )(x1, ym, k, v, w["w_out"], w["xattn_g"], w["xattn_wq"], w["xattn_wo"], w["ffn2_g"], w["ffn2_wg"],
      w["ffn2_wu"], w["ffn2_wd"], w["final_g"])


XA_LANE_TILES = XA_DH // LANES
CACHE_ROW_GROUP = XA_LANE_TILES * XA_HEADS


def _cache_rows_view(x):
    bsz = x.shape[0]
    x = x.reshape(bsz, N_MEM, XA_HEADS, XA_LANE_TILES, LANES)
    return x.transpose(0, 1, 3, 2, 4).reshape(bsz, N_MEM * CACHE_ROW_GROUP, LANES)


def _cache_rows_unview(x):
    bsz = x.shape[0]
    x = x.reshape(bsz, N_MEM, XA_LANE_TILES, XA_HEADS, LANES)
    return x.transpose(0, 1, 3, 2, 4).reshape(bsz, N_MEM, XA_HEADS, XA_DH)


def _cache_rows_store(ref, b, x):
    for h in range(XA_HEADS):
        for lt in range(XA_LANE_TILES):
            lo = h * XA_DH + lt * LANES
            ref[b, pl.ds(lt * XA_HEADS + h, N_MEM, stride=CACHE_ROW_GROUP), :] = x[:, lo:lo + LANES]


def _cache_rows_load(ref, b):
    cols = [ref[b, pl.ds(lt * XA_HEADS + h, N_MEM, stride=CACHE_ROW_GROUP), :]
            for h in range(XA_HEADS) for lt in range(XA_LANE_TILES)]
    return jnp.concatenate(cols, axis=1).astype(BF16)


def _xattn_cache_stages(q_ref, k_ref, v_ref, o_ref):
    bb = k_ref.shape[0]
    tq = q_ref.shape[0] // bb
    lane_head = lax.broadcasted_iota(jnp.int32, (1, D_MODEL), 1) // XA_DH
    qs = [q_ref[b * tq:(b + 1) * tq, :] for b in range(bb)]
    q_bds = [jnp.concatenate([jnp.where(lane_head == h, q, 0.0) for h in range(XA_HEADS)], axis=0).astype(BF16)
             for q in qs]
    k_fulls = [_cache_rows_load(k_ref, b) for b in range(bb)]
    yield
    scores = [_dot_nt(q_bds[b], k_fulls[b]) * (XA_DH ** -0.5) for b in range(bb)]
    yield
    p_all = _softmax(jnp.concatenate(scores, axis=0)).astype(BF16)
    v_fulls = [_cache_rows_load(v_ref, b) for b in range(bb)]
    yield
    rows = XA_HEADS * tq
    o_fulls = [_dot(p_all[b * rows:(b + 1) * rows], v_fulls[b]) for b in range(bb)]
    yield
    for b in range(bb):
        o = jnp.zeros((tq, D_MODEL), F32)
        for h in range(XA_HEADS):
            o = jnp.where(lane_head == h, o_fulls[b][h * tq:(h + 1) * tq], o)
        o_ref[b * tq:(b + 1) * tq, :] = o.astype(o_ref.dtype)


def _ffn_out_kernel(x2_ref, o_ref, wo_ref, g2_ref, wg_ref, wu_ref, wd_ref, gf_ref, y_ref):
    x3 = x2_ref[...] + _dot(o_ref[...].astype(BF16), wo_ref[...])
    x4 = _swiglu_residual(x3, g2_ref, wg_ref, wu_ref, wd_ref)
    y_ref[...] = _rms(x4, gf_ref[...])


def _ffn_out(x2, o, w, tm):
    n = x2.shape[0]
    row = lambda: pl.BlockSpec((tm, D_MODEL), lambda i: (i, 0))
    return pl.pallas_call(
        _ffn_out_kernel,
        grid=(n // tm,),
        in_specs=[row(), row(), _const_spec((D_MODEL, D_MODEL)), _const_spec((1, D_MODEL)),
                  _const_spec((D_MODEL, D_FF)), _const_spec((D_MODEL, D_FF)), _const_spec((D_FF, D_MODEL)),
                  _const_spec((1, D_MODEL))],
        out_specs=row(),
        out_shape=jax.ShapeDtypeStruct((n, D_MODEL), F32),
        compiler_params=_params(("arbitrary",)),
        name="ffn_out",
    )(x2, o, w["xattn_wo"], w["ffn2_g"], w["ffn2_wg"], w["ffn2_wu"], w["ffn2_wd"], w["final_g"])


def _memkv_kernel(m_ref, g_ref, wk_ref, wv_ref, k_ref, v_ref):
    hn = _rms(m_ref[0], g_ref[...]).astype(BF16)
    _cache_rows_store(k_ref, 0, _dot(hn, wk_ref[...]))
    _cache_rows_store(v_ref, 0, _dot(hn, wv_ref[...]))


def _memkv(mem, w):
    bsz = mem.shape[0]
    rows = pl.BlockSpec((1, N_MEM * CACHE_ROW_GROUP, LANES), lambda b: (b, 0, 0))
    return pl.pallas_call(
        _memkv_kernel,
        grid=(bsz,),
        in_specs=[pl.BlockSpec((1, N_MEM, D_MODEL), lambda b: (b, 0, 0)), _const_spec((1, D_MODEL)),
                  _const_spec((D_MODEL, D_MODEL)), _const_spec((D_MODEL, D_MODEL))],
        out_specs=[rows, rows],
        out_shape=[jax.ShapeDtypeStruct((bsz, N_MEM * CACHE_ROW_GROUP, LANES), F32)] * 2,
        compiler_params=_params(("arbitrary",)),
        name="memkv",
    )(mem, w["mem_g"], w["xattn_wk"], w["xattn_wv"])


def _prep_weights(p):
    bf = lambda a: a.astype(BF16)
    row = lambda a: a.reshape(1, -1).astype(F32)
    w_in = p["w_in"]
    off = {}
    pos = 0
    for name, width in (("mq", 512), ("mk", 512), ("mv", 512), ("mi", 4), ("mf", 4), ("mo", 512), ("gq", 256),
                        ("gk", 256), ("gv", 512), ("ga", 16), ("gg", 512)):
        off[name] = w_in[:, pos:pos + width]
        pos += width
    pad_cols = SMALL_W - 2 * ML_HEADS - GLA_RANK
    w_in_q = jnp.concatenate([off[k] for k in ("mq", "mk", "mv", "gq", "gk", "gv")], axis=1)
    w_in_g = jnp.concatenate([off["mo"], off["gg"], off["mi"], off["mf"], off["ga"],
                              jnp.zeros((D_MODEL, pad_cols), F32)], axis=1)
    gate_bias = jnp.concatenate([p["mlstm_b_i"], p["mlstm_b_f"], jnp.zeros((SMALL_W - 2 * ML_HEADS,), F32)])
    w_a2 = jnp.zeros((SMALL_W, GLA_QK_W), F32).at[2 * ML_HEADS:2 * ML_HEADS + GLA_RANK].set(p["gla_w_a2"])
    lane = jnp.arange(LANES)
    same_head = (lane[:, None] // GLA_DK) == (lane[None, :] // GLA_DK)
    diag_sum = jnp.stack([same_head & ((lane[None, :] % GLA_SUB) == j) for j in range(GLA_SUB)]).astype(BF16)
    return dict(
        ffn1_g=row(p["ffn1_norm_g"]), ffn1_wg=bf(p["ffn1_w_gate"]), ffn1_wu=bf(p["ffn1_w_up"]),
        ffn1_wd=bf(p["ffn1_w_down"]), mix_g=row(p["mix_norm_g"]), w_in_q=bf(w_in_q), w_in_g=bf(w_in_g),
        gate_bias=row(gate_bias), w_a2=bf(w_a2), b_a=row(p["gla_b_a"]), mlstm_out_g=row(p["mlstm_out_g"]),
        gla_out_g=row(p["gla_out_g"]), diag_sum=diag_sum,
        w_out=bf(p["w_out"]), xattn_g=row(p["xattn_norm_g"]), xattn_wq=bf(p["xattn_w_q"]),
        xattn_wo=bf(p["xattn_w_o"]), mem_g=row(p["mem_norm_g"]), xattn_wk=bf(p["xattn_w_k"]),
        xattn_wv=bf(p["xattn_w_v"]), ffn2_g=row(p["ffn2_norm_g"]), ffn2_wg=bf(p["ffn2_w_gate"]),
        ffn2_wu=bf(p["ffn2_w_up"]), ffn2_wd=bf(p["ffn2_w_down"]), final_g=row(p["final_g"]))


def kernel(x_prompt, x_sample, mem_prompt, cache_mem_k, cache_mem_v, state_mlstm_c, state_mlstm_n, state_mlstm_m, state_gla_s, ffn1_norm_g, ffn1_w_gate, ffn1_w_up, ffn1_w_down, mix_norm_g, w_in, mlstm_b_i, mlstm_b_f, mlstm_out_g, gla_w_a2, gla_b_a, gla_out_g, w_out, xattn_norm_g, mem_norm_g, xattn_w_q, xattn_w_k, xattn_w_v, xattn_w_o, ffn2_norm_g, ffn2_w_gate, ffn2_w_up, ffn2_w_down, final_norm_g):
    assert ffn1_norm_g.shape[0] == 1, "single-layer stack"
    layer = dict(ffn1_norm_g=ffn1_norm_g, ffn1_w_gate=ffn1_w_gate, ffn1_w_up=ffn1_w_up, ffn1_w_down=ffn1_w_down,
                 mix_norm_g=mix_norm_g, w_in=w_in, mlstm_b_i=mlstm_b_i, mlstm_b_f=mlstm_b_f,
                 mlstm_out_g=mlstm_out_g, gla_w_a2=gla_w_a2, gla_b_a=gla_b_a, gla_out_g=gla_out_g, w_out=w_out,
                 xattn_norm_g=xattn_norm_g, mem_norm_g=mem_norm_g, xattn_w_q=xattn_w_q, xattn_w_k=xattn_w_k,
                 xattn_w_v=xattn_w_v, xattn_w_o=xattn_w_o, ffn2_norm_g=ffn2_norm_g, ffn2_w_gate=ffn2_w_gate,
                 ffn2_w_up=ffn2_w_up, ffn2_w_down=ffn2_w_down)
    p = {name: arr[0] for name, arr in layer.items()}
    p["final_g"] = final_norm_g
    w = _prep_weights(p)

    bp, tp, _ = x_prompt.shape
    bs, ts, _ = x_sample.shape

    state = (state_mlstm_c[0], state_mlstm_n[0], state_mlstm_m[0].reshape(bs, 1, ML_HEADS), state_gla_s[0])
    x1_s, zq_s, zg_s = _ffn_in(x_sample.reshape(bs * ts, D_MODEL), w, TM_FFN_IN, F32)
    ym_s, c_s, n_s, m_s, s_s = _mixer_state(zq_s, zg_s, w, state, ts, SAMPLE_MIXER_BATCHES)
    states_s = (c_s[None], n_s[None], m_s.reshape(1, bs, ML_HEADS), s_s[None])
    x2_s, q_s = _post_mix(x1_s, ym_s, w, TM_FFN_IN, F32)

    mem_k_p, mem_v_p = _memkv(mem_prompt, w)
    x1_p, ym_p, states_p, o_s = _ffn_mix(x_prompt, w, TM_FFN_IN, q_s, cache_mem_k[0], cache_mem_v[0])
    y_p = _post_fused(x1_p.reshape(bp, tp, D_MODEL), ym_p.reshape(bp, tp, D_MODEL), mem_k_p, mem_v_p, w, TM_POST)
    y_s = _ffn_out(x2_s, o_s, w, TM_FFN_IN).reshape(bs, ts, D_MODEL)

    return (y_p, y_s, _cache_rows_unview(mem_k_p)[None], _cache_rows_unview(mem_v_p)[None]) + states_p + states_s
```

```python
import functools

import jax
import jax.numpy as jnp
from jax import lax
from jax.experimental import pallas as pl
from jax.experimental.pallas import tpu as pltpu

F32 = jnp.float32
BF16 = jnp.bfloat16

D_MODEL = 1024
D_FF = 2816
ML_HEADS = 4
ML_DK = 128
ML_DV = 128
GLA_HEADS = 4
GLA_DK = 64
GLA_DV = 128
GLA_RANK = 16
GLA_TAU = 16.0
N_MEM = 256
XA_HEADS = 4
XA_DH = D_MODEL // XA_HEADS
EPS = 1e-6
CHUNK = 64
LOG2_E = 1.4426950408889634
LN_2 = 0.6931471805599453
LANES = 128
SUBLANES = 8

ZQ_MQ, ZQ_MK, ZQ_MV = 0, 512, 1024
ZQ_GQ, ZQ_GK, ZQ_GV = 1536, 1792, 2048
ZQ_W = 2560
ZG_MO, ZG_GG, ZG_SMALL = 0, 512, 1024
ZG_W = 1152
SMALL_W = LANES
GLA_QK_W = GLA_HEADS * GLA_DK
GLA_V_W = GLA_HEADS * GLA_DV
GLA_PAIRS = GLA_HEADS // 2
GLA_SUB = SUBLANES

MXU_WIDTH = 256
FF_CHUNKS = ((0, 6 * MXU_WIDTH), (6 * MXU_WIDTH, D_FF))
VMEM_LIMIT_BYTES = 56 * 1024 * 1024

TM_FFN_IN = MXU_WIDTH
TM_POST = 2 * MXU_WIDTH
SAMPLE_MIXER_BATCHES = 8


def _rms(x, g):
    return x * lax.rsqrt(jnp.mean(x * x, axis=-1, keepdims=True) + EPS) * g


def _log_sigmoid(x):
    return jnp.minimum(x, 0.0) - jnp.log1p(jnp.exp(-jnp.abs(x)))


def _dot(a, b):
    return jnp.dot(a, b, preferred_element_type=F32)


def _dot_nt(a, b):
    return lax.dot_general(a, b, (((1,), (1,)), ((), ())), preferred_element_type=F32)


def _dot_f32(a, b):
    return jnp.dot(a, b, precision=lax.Precision.HIGHEST, preferred_element_type=F32)


def _swiglu_residual(x, g_ref, wg_ref, wu_ref, wd_ref):
    h = _rms(x, g_ref[...]).astype(BF16)
    acts = []
    for lo, hi in FF_CHUNKS:
        g = _dot(h, wg_ref[:, lo:hi])
        u = _dot(h, wu_ref[:, lo:hi])
        acts.append(((g * jax.nn.sigmoid(g)) * u).astype(BF16))
    acc = jnp.zeros_like(x)
    for (lo, hi), a in zip(FF_CHUNKS, acts):
        acc = acc + _dot(a, wd_ref[lo:hi, :])
    return x + 0.5 * acc


def _const_spec(shape):
    nd = len(shape)
    return pl.BlockSpec(shape, lambda *_: (0,) * nd, pipeline_mode=pl.Buffered(1))


def _params(sem):
    return pltpu.CompilerParams(dimension_semantics=sem, vmem_limit_bytes=VMEM_LIMIT_BYTES)


def _ffn_in_kernel(x_ref, g1_ref, wg_ref, wu_ref, wd_ref, gm_ref, wq_ref, wgt_ref, x1_ref, zq_ref, zg_ref):
    x1 = _swiglu_residual(x_ref[...], g1_ref, wg_ref, wu_ref, wd_ref)
    x1_ref[...] = x1
    hm = _rms(x1, gm_ref[...]).astype(BF16)
    zq_ref[...] = _dot(hm, wq_ref[...]).astype(zq_ref.dtype)
    zg_ref[...] = _dot(hm, wgt_ref[...])


def _ffn_in(x, w, tm, zq_dtype):
    n = x.shape[0]
    row = lambda width: pl.BlockSpec((tm, width), lambda i: (i, 0))
    return pl.pallas_call(
        _ffn_in_kernel,
        grid=(n // tm,),
        in_specs=[row(D_MODEL), _const_spec((1, D_MODEL)), _const_spec((D_MODEL, D_FF)),
                  _const_spec((D_MODEL, D_FF)), _const_spec((D_FF, D_MODEL)), _const_spec((1, D_MODEL)),
                  _const_spec((D_MODEL, ZQ_W)), _const_spec((D_MODEL, ZG_W))],
        out_specs=[row(D_MODEL), row(ZQ_W), row(ZG_W)],
        out_shape=[jax.ShapeDtypeStruct((n, D_MODEL), F32), jax.ShapeDtypeStruct((n, ZQ_W), zq_dtype),
                   jax.ShapeDtypeStruct((n, ZG_W), F32)],
        compiler_params=_params(("arbitrary",)),
        name="ffn_in",
    )(x, w["ffn1_g"], w["ffn1_wg"], w["ffn1_wu"], w["ffn1_wd"], w["mix_g"], w["w_in_q"], w["w_in_g"])


def _mixer_stages(zqs, zgs, states, consts, ee_ref, *, L, t_real, chain):
    bias, wa2, ba, gml, ggl = consts
    nchunk = len(zqs)
    groups = [(b, h) for b in range(nchunk) for h in range(ML_HEADS)]
    pairs = [(b, p) for b in range(nchunk) for p in range(GLA_PAIRS)]
    padded = t_real < L
    valid = lax.broadcasted_iota(jnp.int32, (L, 1), 0) < t_real
    rr = lax.broadcasted_iota(jnp.int32, (L, L), 0)
    cc = lax.broadcasted_iota(jnp.int32, (L, L), 1)
    tril = cc <= rr
    c = GLA_SUB
    nb = L // c
    lane = lax.broadcasted_iota(jnp.int32, (1, LANES), 1)
    lane_blk = (lane % GLA_DK) // c
    lane_head = lane // GLA_DK
    v_lane_head = lax.broadcasted_iota(jnp.int32, (1, 2 * GLA_DV), 1) // GLA_DV
    row_blk = (lax.broadcasted_iota(jnp.int32, (2 * L, 1), 0) % L) // c
    t_in = lax.broadcasted_iota(jnp.int32, (1, c, 1), 1)
    new_states = [dict(c=[None] * ML_HEADS, n=[None] * ML_HEADS, m=[None] * ML_HEADS, s=[None] * GLA_HEADS)
                  for _ in range(nchunk)]
    prev = lambda b: new_states[b - 1] if chain and b > 0 else states[0 if chain else b]
    known = lambda b: not chain or b == 0

    carried = {}

    def carried_mlstm(b, h):
        st = prev(b)
        d = ml[(b, h)]
        carried[(b, h)] = _dot(d["qb"], st["c"][h].astype(BF16))

    def carried_gla(b, p):
        s_prev = prev(b)["s"]
        zero_blk = jnp.zeros((GLA_DK, GLA_DV), BF16)
        s_bd = jnp.concatenate(
            [jnp.concatenate([s_prev[2 * p].astype(BF16), zero_blk], axis=1),
             jnp.concatenate([zero_blk, s_prev[2 * p + 1].astype(BF16)], axis=1)], axis=0)
        carried[(b, "gla", p)] = _dot(gl[(b, p)]["q_dec"], s_bd)

    smalls, sms, lfs, b_cols, b_rows, sm_ts, las, bcs = [], [], [], [], [], [], [], []
    for b in range(nchunk):
        small = zgs[b][:, ZG_SMALL:ZG_SMALL + SMALL_W]
        sm = small + bias
        lf = _log_sigmoid(sm) * LOG2_E
        sm = sm * LOG2_E
        if padded:
            sm = jnp.where(valid, sm, -jnp.inf)
            lf = jnp.where(valid, lf, 0.0)
        smalls.append(small)
        sms.append(sm)
        lfs.append(lf)

    long_chunk = L > SUBLANES
    col_w = 1 if long_chunk else LANES
    tril_f, triu_f = tril.astype(F32), (rr <= cc).astype(F32)

    def cumsum_rows(x):
        if long_chunk:
            return _dot_f32(tril_f, x)
        acc = x[0:1]
        rows = [acc]
        for r in range(1, L):
            acc = acc + x[r:r + 1]
            rows.append(acc)
        return jnp.concatenate(rows, axis=0)

    lf_ts = [lf.T[0:SUBLANES] for lf in lfs] if long_chunk else None
    yield
    for b in range(nchunk):
        b_cols.append(cumsum_rows(lfs[b]))
        b_rows.append(_dot_f32(lf_ts[b], triu_f) if long_chunk else b_cols[b].T[0:SUBLANES])
        sm_ts.append(sms[b].T)
        la = _log_sigmoid(_dot(smalls[b].astype(BF16), wa2) + ba) * (LOG2_E / GLA_TAU)
        las.append(jnp.where(valid, la, 0.0) if padded else la)

    ml = {}
    for g in groups:
        b, h = g
        zq = zqs[b]
        qf = zq[:, ZQ_MQ + h * ML_DK:ZQ_MQ + (h + 1) * ML_DK].astype(F32)
        kf = zq[:, ZQ_MK + h * ML_DK:ZQ_MK + (h + 1) * ML_DK].astype(F32) * (ML_DK ** -0.5)
        vf = zq[:, ZQ_MV + h * ML_DV:ZQ_MV + (h + 1) * ML_DV].astype(F32)
        if padded:
            kf = jnp.where(valid, kf, 0.0)
            vf = jnp.where(valid, vf, 0.0)
        ml[g] = dict(qf=qf, qb=qf.astype(BF16), kf=kf, kb=kf.astype(BF16), vb=vf.astype(BF16))
    yield
    for b in range(nchunk):
        bcs.append(cumsum_rows(las[b]))
    for g in groups:
        d = ml[g]
        d["qk"] = _dot_nt(d["qb"], d["kb"])
        if known(g[0]):
            carried_mlstm(*g)

    yield
    for g in groups:
        b, h = g
        d = ml[g]
        i_col = jnp.broadcast_to(sms[b][:, h:h + 1], (L, col_w))
        b_col = jnp.broadcast_to(b_cols[b][:, ML_HEADS + h:ML_HEADS + h + 1], (L, col_w))
        b_row = b_rows[b][ML_HEADS + h:ML_HEADS + h + 1, :]
        i_row = sm_ts[b][h:h + 1, :]
        a_col = b_col + prev(b)["m"][h] * LOG2_E
        dm = jnp.where(tril, b_col[:, :L] - (b_row - i_row), -jnp.inf)
        mt = jnp.maximum(a_col, jnp.max(dm, axis=1, keepdims=True))
        w_inter = jnp.exp2(a_col - mt)
        s = d["qk"] * jnp.exp2(dm - mt[:, :L])
        kw = d["kf"] * jnp.exp2((b_col[L - 1:L] - mt[L - 1:L]) - (b_col - i_col))
        d.update(mt=mt, w_inter=w_inter, s=s, kw=kw, kw_t=kw.T.astype(BF16))
        new_states[b]["m"][h] = mt[L - 1:L, 0:1] * LN_2

    gl = {}
    for b in range(nchunk):
        zq = zqs[b]
        gq = zq[:, ZQ_GQ:ZQ_GQ + GLA_QK_W].astype(F32) * (GLA_DK ** -0.5)
        gk = zq[:, ZQ_GK:ZQ_GK + GLA_QK_W].astype(F32)
        gv = zq[:, ZQ_GV:ZQ_GV + GLA_V_W].astype(F32)
        if padded:
            gk = jnp.where(valid, gk, 0.0)
            gv = jnp.where(valid, gv, 0.0)
        stack = lambda x: jnp.concatenate([x[:, :LANES], x[:, LANES:]], axis=0)
        q2, k2, b2 = stack(gq), stack(gk), stack(bcs[b])
        q3 = q2.reshape(2 * nb, c, LANES)
        k3 = k2.reshape(2 * nb, c, LANES)
        b3 = b2.reshape(2 * nb, c, LANES)
        pair_terms = []
        for j in range(min(c, t_real)):
            decay = jnp.exp2(jnp.where(t_in >= j, b3 - b3[:, j:j + 1, :], -jnp.inf))
            pair_terms.append((q3 * k3[:, j:j + 1, :] * decay).reshape(2 * L, LANES).astype(BF16))
        kt2 = (k3 * jnp.exp2(b3[:, c - 1:c, :] - b3)).reshape(2 * L, LANES) if nb > 1 else None
        gl[b] = dict(gv=gv, q2=q2, k2=k2, b2=b2, pair_terms=pair_terms, kt2=kt2)
    yield
    for b in range(nchunk):
        acc = jnp.zeros((2 * L, LANES), F32)
        for j, pair_j in enumerate(gl[b]["pair_terms"]):
            acc = acc + _dot(pair_j, ee_ref[j])
        gl[b]["a_diag"] = jnp.where(lane_blk == row_blk, acc, 0.0)

    for bp in pairs:
        b, p = bp
        d = gl[b]
        rows_p = slice(p * L, (p + 1) * L)
        q_p, k_p, b_p = d["q2"][rows_p], d["k2"][rows_p], d["b2"][rows_p]
        e = dict(q_dec=(q_p * jnp.exp2(b_p)).astype(BF16),
                 kh_t=(k_p * jnp.exp2(b_p[L - 1:L] - b_p)).T.astype(BF16),
                 decay_col=jnp.exp2(b_p[L - SUBLANES:L].T[:, SUBLANES - 1:SUBLANES]))
        if nb > 1:
            kt_p = d["kt2"][rows_p].astype(BF16)
            k_bd = jnp.concatenate([jnp.where(lane_head == hh, kt_p, 0.0) for hh in range(2)], axis=0)
            slabs, offs = [], []
            off = 0
            for j in range(nb - 1):
                lo = (j + 1) * c
                slabs.append(q_p[lo:] * jnp.exp2(b_p[lo:] - b_p[lo - 1:lo]))
                offs.append(off)
                off += L - lo
            e.update(q_var=jnp.concatenate(slabs, axis=0).astype(BF16), k_bd=k_bd, offs=offs)
        gl[bp] = e
    yield
    for bp in pairs:
        e = gl[bp]
        if nb > 1:
            e["r"] = _dot_nt(e["q_var"], e["k_bd"])
        if known(bp[0]):
            carried_gla(*bp)

    for g in groups:
        d = ml[g]
        d["sv"] = _dot(d["s"].astype(BF16), d["vb"])
        d["c_upd"] = _dot(d["kw_t"], d["vb"])
        d["den"] = jnp.sum(d["s"], axis=1, keepdims=True)
        if known(g[0]):
            d["qn"] = jnp.sum(d["qf"] * prev(g[0])["n"][g[1]], axis=1, keepdims=True)
    yield
    for bp in pairs:
        b, p = bp
        d, e = gl[b], gl[bp]
        a_p = d["a_diag"][p * L:(p + 1) * L]
        if nb > 1:
            blocks = []
            for i in range(nb):
                blk = a_p[i * c:(i + 1) * c]
                for j in range(i):
                    lo_r = e["offs"][j] + (i - j - 1) * c
                    blk = jnp.where(lane_blk == j, e["r"][lo_r:lo_r + c], blk)
                blocks.append(blk)
            a_p = jnp.concatenate(blocks, axis=0)
        v_f = d["gv"][:, p * 2 * GLA_DV:(p + 1) * 2 * GLA_DV]
        v_p = v_f.astype(BF16)
        if L < GLA_DK:
            v_rows = []
            for hh in range(2):
                v_rows += [jnp.where(v_lane_head == hh, v_f, 0.0), jnp.zeros((GLA_DK - L, 2 * GLA_DV), F32)]
            v_bd = jnp.concatenate(v_rows, axis=0).astype(BF16)
        else:
            v_bd = jnp.concatenate([jnp.where(v_lane_head == hh, v_p, 0.0) for hh in range(2)], axis=0)
        e["s_upd"] = [_dot(e["kh_t"][hh * GLA_DK:(hh + 1) * GLA_DK], v_p[:, hh * GLA_DV:(hh + 1) * GLA_DV])
                      for hh in range(2)]
        e.update(scores=a_p.astype(BF16), v_bd=v_bd)
    yield
    for bp in pairs:
        e = gl[bp]
        e["o_intra"] = _dot(e["scores"], e["v_bd"])

    yield
    ys = [[None] * (ML_HEADS + GLA_HEADS) for _ in range(nchunk)]

    def finish_mlstm(b):
        st = prev(b)
        for h in range(ML_HEADS):
            d = ml[(b, h)]
            mt, w_inter = d["mt"], d["w_inter"]
            qn = d["qn"] if known(b) else jnp.sum(d["qf"] * st["n"][h], axis=1, keepdims=True)
            den = d["den"] + w_inter * qn
            hh = (d["sv"] + w_inter * carried[(b, h)]) / jnp.maximum(jnp.abs(den), jnp.exp2(-mt))
            i_last = w_inter[L - 1:L]
            new_states[b]["c"][h] = i_last * st["c"][h] + d["c_upd"]
            new_states[b]["n"][h] = i_last * st["n"][h] + jnp.sum(d["kw"], axis=0, keepdims=True)
            hs = slice(h * ML_DV, (h + 1) * ML_DV)
            yn = hh * lax.rsqrt(jnp.mean(hh * hh, axis=-1, keepdims=True) + EPS) * gml[:, hs]
            ys[b][h] = jax.nn.sigmoid(zgs[b][:, ZG_MO + h * ML_DV:ZG_MO + (h + 1) * ML_DV]) * yn

    def finish_gla(b):
        st = prev(b)
        for p in range(GLA_PAIRS):
            e = gl[(b, p)]
            o = carried[(b, "gla", p)] + e["o_intra"]
            for hh in range(2):
                h = 2 * p + hh
                ds = slice(hh * GLA_DK, (hh + 1) * GLA_DK)
                vs = slice(hh * GLA_DV, (hh + 1) * GLA_DV)
                new_states[b]["s"][h] = e["decay_col"][ds] * st["s"][h] + e["s_upd"][hh]
                oh = o[:, vs]
                yn = oh * lax.rsqrt(jnp.mean(oh * oh, axis=-1, keepdims=True) + EPS) * ggl[:, h * GLA_DV:(h + 1) * GLA_DV]
                gg = zgs[b][:, ZG_GG + h * GLA_DV:ZG_GG + (h + 1) * GLA_DV]
                ys[b][ML_HEADS + h] = (gg * jax.nn.sigmoid(gg)) * yn

    if chain:
        for b in range(nchunk):
            if not known(b):
                for h in range(ML_HEADS):
                    carried_mlstm(b, h)
                for p in range(GLA_PAIRS):
                    carried_gla(b, p)
            finish_mlstm(b)
            finish_gla(b)
    else:
        for b in range(nchunk):
            finish_mlstm(b)
        for b in range(nchunk):
            finish_gla(b)
    return [jnp.concatenate(y, axis=1) for y in ys], new_states


def _drain(gen):
    try:
        while True:
            next(gen)
    except StopIteration as done:
        return done.value


def _mixer_compute(*args, **kwargs):
    return _drain(_mixer_stages(*args, **kwargs))


def _mixer_state_kernel(zq_ref, zg_ref, bias_ref, wa2_ref, ba_ref, gml_ref, ggl_ref, ee_ref, c0_ref, n0_ref, m0_ref,
                        s0_ref, y_ref, co_ref, no_ref, mo_ref, so_ref, *, L, t_real, nblk):
    consts = (bias_ref[...], wa2_ref[...], ba_ref[...], gml_ref[...], ggl_ref[...])
    lane = lax.broadcasted_iota(jnp.int32, (1, LANES), 1)

    def chunk_rows(ref, b):
        rows = ref[b * t_real:(b + 1) * t_real, :]
        return jnp.concatenate([rows, jnp.zeros((L - t_real, rows.shape[1]), rows.dtype)], axis=0)

    states = []
    for b in range(nblk):
        n_all = n0_ref[b]
        m_all = m0_ref[b]
        states.append(dict(c=[c0_ref[b, h] for h in range(ML_HEADS)],
                           n=[n_all[h:h + 1, :] for h in range(ML_HEADS)],
                           m=[m_all[:, h:h + 1] for h in range(ML_HEADS)],
                           s=[s0_ref[b, h] for h in range(GLA_HEADS)]))
    ys, new_states = _mixer_compute([chunk_rows(zq_ref, b) for b in range(nblk)],
                                    [chunk_rows(zg_ref, b) for b in range(nblk)], states, consts, ee_ref, L=L,
                                    t_real=t_real, chain=False)
    for b in range(nblk):
        st = new_states[b]
        y_ref[b * t_real:(b + 1) * t_real, :] = ys[b][0:t_real].astype(y_ref.dtype)
        for h in range(ML_HEADS):
            co_ref[b, h] = st["c"][h]
            so_ref[b, h] = st["s"][h]
        no_ref[b] = jnp.concatenate(st["n"], axis=0)
        m_row = jnp.zeros((1, LANES), F32)
        for h in range(ML_HEADS):
            m_row = jnp.where(lane == h, st["m"][h], m_row)
        mo_ref[b] = m_row[:, 0:ML_HEADS]


def _mixer_state(zq, zg, w, state, t_real, nblk):
    bsz = zq.shape[0] // t_real
    L = -(-t_real // SUBLANES) * SUBLANES
    rows = lambda width: pl.BlockSpec((nblk * t_real, width), lambda b: (b, 0))
    per_b = lambda *tail: pl.BlockSpec((nblk,) + tail, lambda b: (b,) + (0,) * len(tail))
    state_specs = [per_b(ML_HEADS, ML_DK, ML_DV), per_b(ML_HEADS, ML_DK), per_b(1, ML_HEADS),
                   per_b(GLA_HEADS, GLA_DK, GLA_DV)]
    consts = [_const_spec((1, SMALL_W)), _const_spec((SMALL_W, GLA_QK_W)), _const_spec((1, GLA_QK_W)),
              _const_spec((1, ML_HEADS * ML_DV)), _const_spec((1, GLA_V_W)), _const_spec((GLA_SUB, LANES, LANES))]
    return pl.pallas_call(
        functools.partial(_mixer_state_kernel, L=L, t_real=t_real, nblk=nblk),
        grid=(bsz // nblk,),
        in_specs=[rows(ZQ_W), rows(ZG_W)] + consts + state_specs,
        out_specs=[rows(D_MODEL)] + state_specs,
        out_shape=[jax.ShapeDtypeStruct((bsz * t_real, D_MODEL), F32),
                   jax.ShapeDtypeStruct((bsz, ML_HEADS, ML_DK, ML_DV), F32),
                   jax.ShapeDtypeStruct((bsz, ML_HEADS, ML_DK), F32),
                   jax.ShapeDtypeStruct((bsz, 1, ML_HEADS), F32),
                   jax.ShapeDtypeStruct((bsz, GLA_HEADS, GLA_DK, GLA_DV), F32)],
        compiler_params=_params(("arbitrary",)),
        name="mixer_state",
    )(zq, zg, w["gate_bias"], w["w_a2"], w["b_a"], w["mlstm_out_g"], w["gla_out_g"], w["diag_sum"], *state)


def _ffn_in_pieces(x, g1_ref, wg_ref, wu_ref, wd_ref, gm_ref, wq_ref, wgt_ref, x1_ref, zq_out, zg_out):
    h = _rms(x, g1_ref[...]).astype(BF16)
    acts = []
    for lo in range(0, D_FF, MXU_WIDTH):
        g = _dot(h, wg_ref[:, lo:lo + MXU_WIDTH])
        yield
        u = _dot(h, wu_ref[:, lo:lo + MXU_WIDTH])
        yield
        acts.append(((g * jax.nn.sigmoid(g)) * u).astype(BF16))
    a = jnp.concatenate(acts, axis=1)
    down = []
    for lo in range(0, D_MODEL, MXU_WIDTH):
        down.append(_dot(a, wd_ref[:, lo:lo + MXU_WIDTH]))
        yield
    x1 = x + 0.5 * jnp.concatenate(down, axis=1)
    x1_ref[...] = x1
    hm = _rms(x1, gm_ref[...]).astype(BF16)
    for lo in range(0, ZQ_W, MXU_WIDTH):
        zq_out[:, lo:lo + MXU_WIDTH] = _dot(hm, wq_ref[:, lo:lo + MXU_WIDTH]).astype(zq_out.dtype)
        yield
    for lo in range(0, ZG_W, MXU_WIDTH):
        hi = min(lo + MXU_WIDTH, ZG_W)
        zg_out[:, lo:hi] = _dot(hm, wgt_ref[:, lo:hi])
        yield


ATTN_STAGE_EVERY = 7
FFN_PIECES_PER_MIXER_STAGE = 1


def _ffn_mix_kernel(*refs, tiles_per_batch, n_cast):
    it = iter(refs)
    (x_ref, g1_ref, wg_ref, wu_ref, wd_ref, gm_ref, wq_ref, wgt_ref, bias_ref, wa2_ref, ba_ref, gml_ref, ggl_ref,
     ee_ref, qs_ref, ks_ref, vs_ref) = (next(it) for _ in range(17))
    cast_src = [next(it) for _ in range(n_cast)]
    x1_ref, ym_ref, co_ref, no_ref, mo_ref, so_ref, os_ref = (next(it) for _ in range(7))
    cast_dst = [next(it) for _ in range(n_cast)]
    zq_s, zg_s, c_s, n_s, m_s, s_s = (next(it) for _ in range(6))
    i = pl.program_id(0)
    for src, dst in zip(cast_src, cast_dst):
        dst[...] = src[...].astype(dst.dtype)

    @pl.when(i == 0)
    def _init():
        for ref in (zq_s, zg_s, c_s, n_s, m_s, s_s):
            ref[...] = jnp.zeros_like(ref)

    consts = (bias_ref[...], wa2_ref[...], ba_ref[...], gml_ref[...], ggl_ref[...])
    n_chunks = zq_s.shape[0] // CHUNK
    rows = [slice(k * CHUNK, (k + 1) * CHUNK) for k in range(n_chunks)]
    zqs = [zq_s[r, :] for r in rows]
    zgs = [zg_s[r, :] for r in rows]
    starts_batch = (i - 1) % tiles_per_batch == 0
    carry = lambda v: jnp.where(starts_batch, 0.0, v)
    state = dict(c=[carry(c_s[h]) for h in range(ML_HEADS)], n=[carry(n_s[h:h + 1, :]) for h in range(ML_HEADS)],
                 m=[carry(m_s[h:h + 1, 0:1]) for h in range(ML_HEADS)], s=[carry(s_s[h]) for h in range(GLA_HEADS)])

    def mixers():
        st = state
        for k in range(n_chunks):
            ys, new = yield from _mixer_stages([zqs[k]], [zgs[k]], [st], consts, ee_ref, L=CHUNK, t_real=CHUNK,
                                               chain=True)
            ym_ref[rows[k], :] = ys[0].astype(ym_ref.dtype)
            st = new[0]
            yield
        return st

    ffn = _ffn_in_pieces(x_ref[...], g1_ref, wg_ref, wu_ref, wd_ref, gm_ref, wq_ref, wgt_ref, x1_ref, zq_s, zg_s)
    mix = mixers()
    attn = _xattn_cache_stages(qs_ref, ks_ref, vs_ref, os_ref)
    live = {"ffn": True, "attn": True}

    def advance(name, gen):
        if live[name]:
            try:
                next(gen)
            except StopIteration:
                live[name] = False

    slot = 0
    while True:
        try:
            next(mix)
        except StopIteration as done:
            last = done.value
            break
        for _ in range(FFN_PIECES_PER_MIXER_STAGE):
            advance("ffn", ffn)
        if slot % ATTN_STAGE_EVERY == ATTN_STAGE_EVERY // 2:
            advance("attn", attn)
        slot += 1
    for name, gen in (("ffn", ffn), ("attn", attn)):
        if live[name]:
            _drain(gen)

    for h in range(ML_HEADS):
        c_s[h] = last["c"][h]
        n_s[h:h + 1, :] = last["n"][h]
        m_s[h:h + 1, :] = jnp.broadcast_to(last["m"][h], (1, LANES))
        s_s[h] = last["s"][h]

    @pl.when(jnp.logical_and(i >= 1, (i - 1) % tiles_per_batch == tiles_per_batch - 1))
    def _emit_state():
        lane = lax.broadcasted_iota(jnp.int32, (1, LANES), 1)
        for h in range(ML_HEADS):
            co_ref[0, h] = last["c"][h]
            so_ref[0, h] = last["s"][h]
        no_ref[0] = jnp.concatenate(last["n"], axis=0)
        m_row = jnp.zeros((1, LANES), F32)
        for h in range(ML_HEADS):
            m_row = jnp.where(lane == h, last["m"][h], m_row)
        mo_ref[0] = m_row[:, 0:ML_HEADS]


def _ffn_mix(x, w, tm, q_s, k_cache, v_cache, to_cast):
    bsz, t, _ = x.shape
    tiles_per_batch = t // tm
    n_tiles = bsz * tiles_per_batch
    assert t % tm == 0 and tm % CHUNK == 0
    bs = k_cache.shape[0]
    ts = q_s.shape[0] // bs
    assert bs % n_tiles == 0
    sb = bs // n_tiles
    cur_tile = lambda i: jnp.minimum(i, n_tiles - 1)
    prev_tile = lambda i: jnp.maximum(i - 1, 0)
    cur = lambda width: pl.BlockSpec((tm, width), lambda i: (cur_tile(i), 0))
    per_b = lambda *tail: pl.BlockSpec((1,) + tail, lambda i: (prev_tile(i) // tiles_per_batch,) + (0,) * len(tail))
    state_specs = [per_b(ML_HEADS, ML_DK, ML_DV), per_b(ML_HEADS, ML_DK), per_b(1, ML_HEADS),
                   per_b(GLA_HEADS, GLA_DK, GLA_DV)]
    qo_s = pl.BlockSpec((sb * ts, D_MODEL), lambda i: (cur_tile(i), 0))
    kv_s = pl.BlockSpec((sb, N_MEM * CACHE_ROW_GROUP, LANES), lambda i: (cur_tile(i), 0, 0))
    cast_specs = []
    for a in to_cast:
        rb = next(r for r in range(16, a.shape[0] + 1, 16) if a.shape[0] % r == 0 and a.shape[0] // r <= n_tiles)
        cast_specs.append(pl.BlockSpec((rb, a.shape[1]), lambda i, last=a.shape[0] // rb - 1: (jnp.minimum(i, last), 0)))
    x1, ym, c_new, n_new, m_new, s_new, o_s, *cast = pl.pallas_call(
        functools.partial(_ffn_mix_kernel, tiles_per_batch=tiles_per_batch, n_cast=len(to_cast)),
        grid=(n_tiles + 1,),
        in_specs=[cur(D_MODEL), _const_spec((1, D_MODEL)), _const_spec((D_MODEL, D_FF)),
                  _const_spec((D_MODEL, D_FF)), _const_spec((D_FF, D_MODEL)), _const_spec((1, D_MODEL)),
                  _const_spec((D_MODEL, ZQ_W)), _const_spec((D_MODEL, ZG_W)), _const_spec((1, SMALL_W)),
                  _const_spec((SMALL_W, GLA_QK_W)), _const_spec((1, GLA_QK_W)), _const_spec((1, ML_HEADS * ML_DV)),
                  _const_spec((1, GLA_V_W)), _const_spec((GLA_SUB, LANES, LANES)), qo_s, kv_s, kv_s] + cast_specs,
        out_specs=([cur(D_MODEL), pl.BlockSpec((tm, D_MODEL), lambda i: (prev_tile(i), 0))] + state_specs + [qo_s]
                   + cast_specs),
        out_shape=[jax.ShapeDtypeStruct((n_tiles * tm, D_MODEL), F32),
                   jax.ShapeDtypeStruct((n_tiles * tm, D_MODEL), BF16),
                   jax.ShapeDtypeStruct((bsz, ML_HEADS, ML_DK, ML_DV), F32),
                   jax.ShapeDtypeStruct((bsz, ML_HEADS, ML_DK), F32),
                   jax.ShapeDtypeStruct((bsz, 1, ML_HEADS), F32),
                   jax.ShapeDtypeStruct((bsz, GLA_HEADS, GLA_DK, GLA_DV), F32),
                   jax.ShapeDtypeStruct((bs * ts, D_MODEL), F32)] + [jax.ShapeDtypeStruct(a.shape, BF16) for a in to_cast],
        scratch_shapes=[pltpu.VMEM((tm, ZQ_W), BF16), pltpu.VMEM((tm, ZG_W), F32),
                        pltpu.VMEM((ML_HEADS, ML_DK, ML_DV), F32), pltpu.VMEM((SUBLANES, LANES), F32),
                        pltpu.VMEM((SUBLANES, LANES), F32), pltpu.VMEM((GLA_HEADS, GLA_DK, GLA_DV), F32)],
        compiler_params=_params(("arbitrary",)),
        name="ffn_mix",
    )(x.reshape(bsz * t, D_MODEL), w["ffn1_g"], w["ffn1_wg"], w["ffn1_wu"], w["ffn1_wd"], w["mix_g"], w["w_in_q"],
      w["w_in_g"], w["gate_bias"], w["w_a2"], w["b_a"], w["mlstm_out_g"], w["gla_out_g"], w["diag_sum"],
      q_s, _cache_rows_view(k_cache), _cache_rows_view(v_cache), *to_cast)
    return x1, ym, (c_new[None], n_new[None], m_new.reshape(1, bsz, ML_HEADS), s_new[None]), o_s, cast


def _post_mix_kernel(x1_ref, ym_ref, wout_ref, gx_ref, wq_ref, x2_ref, q_ref):
    x2 = x1_ref[...] + _dot(ym_ref[...].astype(BF16), wout_ref[...])
    x2_ref[...] = x2
    hq = _rms(x2, gx_ref[...]).astype(BF16)
    q_ref[...] = _dot(hq, wq_ref[...]).astype(q_ref.dtype)


def _post_mix(x1, ym, w, tm, q_dtype):
    n = x1.shape[0]
    row = lambda: pl.BlockSpec((tm, D_MODEL), lambda i: (i, 0))
    return pl.pallas_call(
        _post_mix_kernel,
        grid=(n // tm,),
        in_specs=[row(), row(), _const_spec((D_MODEL, D_MODEL)), _const_spec((1, D_MODEL)),
                  _const_spec((D_MODEL, D_MODEL))],
        out_specs=[row(), row()],
        out_shape=[jax.ShapeDtypeStruct((n, D_MODEL), F32), jax.ShapeDtypeStruct((n, D_MODEL), q_dtype)],
        compiler_params=_params(("arbitrary",)),
        name="post_mix",
    )(x1, ym, w["w_out"], w["xattn_g"], w["xattn_wq"])


def _softmax(s):
    e = jnp.exp(s - jnp.max(s, axis=-1, keepdims=True))
    return e / jnp.sum(e, axis=-1, keepdims=True)


def _post_fused_kernel(x1_ref, ym_ref, k_ref, v_ref, wout_ref, gx_ref, wq_ref, wo_ref, g2_ref, wg_ref, wu_ref,
                       wd_ref, gf_ref, y_ref):
    x2 = x1_ref[0] + _dot(ym_ref[0].astype(BF16), wout_ref[...])
    q = _dot(_rms(x2, gx_ref[...]).astype(BF16), wq_ref[...]).astype(BF16)
    heads = [slice(h * XA_DH, (h + 1) * XA_DH) for h in range(XA_HEADS)]
    k_full, v_full = _cache_rows_load(k_ref, 0), _cache_rows_load(v_ref, 0)
    scores = [_dot_nt(q[:, hs], k_full[:, hs]) * (XA_DH ** -0.5) for hs in heads]
    probs = [_softmax(s).astype(BF16) for s in scores]
    o = jnp.concatenate([_dot(p, v_full[:, hs]).astype(BF16) for hs, p in zip(heads, probs)], axis=1)
    x3 = x2 + _dot(o, wo_ref[...])
    x4 = _swiglu_residual(x3, g2_ref, wg_ref, wu_ref, wd_ref)
    y_ref[0] = _rms(x4, gf_ref[...])


def _post_fused(x1, ym, k, v, w, tm):
    bsz, t = x1.shape[0], x1.shape[1]
    row = lambda: pl.BlockSpec((1, tm, D_MODEL), lambda b, j: (b, j, 0))
    kv = pl.BlockSpec((1, N_MEM * CACHE_ROW_GROUP, LANES), lambda b, j: (b, 0, 0))
    sq = _const_spec((D_MODEL, D_MODEL))
    vec = _const_spec((1, D_MODEL))
    return pl.pallas_call(
        _post_fused_kernel,
        grid=(bsz, t // tm),
        in_specs=[row(), row(), kv, kv, sq, vec, sq, sq, vec, _const_spec((D_MODEL, D_FF)),
                  _const_spec((D_MODEL, D_FF)), _const_spec((D_FF, D_MODEL)), vec],
        out_specs=row(),
        out_shape=jax.ShapeDtypeStruct((bsz, t, D_MODEL), F32),
        compiler_params=_params(("arbitrary", "arbitrary")),
        name="post_fused",
    )(x1, ym, k, v, w["w_out"], w["xattn_g"], w["xattn_wq"], w["xattn_wo"], w["ffn2_g"], w["ffn2_wg"],
      w["ffn2_wu"], w["ffn2_wd"], w["final_g"])


XA_LANE_TILES = XA_DH // LANES
CACHE_ROW_GROUP = XA_LANE_TILES * XA_HEADS


def _cache_rows_view(x):
    bsz = x.shape[0]
    x = x.reshape(bsz, N_MEM, XA_HEADS, XA_LANE_TILES, LANES)
    return x.transpose(0, 1, 3, 2, 4).reshape(bsz, N_MEM * CACHE_ROW_GROUP, LANES)


def _cache_rows_unview(x):
    bsz = x.shape[0]
    x = x.reshape(bsz, N_MEM, XA_LANE_TILES, XA_HEADS, LANES)
    return x.transpose(0, 1, 3, 2, 4).reshape(bsz, N_MEM, XA_HEADS, XA_DH)


def _cache_rows_store(ref, b, x):
    for h in range(XA_HEADS):
        for lt in range(XA_LANE_TILES):
            lo = h * XA_DH + lt * LANES
            ref[b, pl.ds(lt * XA_HEADS + h, N_MEM, stride=CACHE_ROW_GROUP), :] = x[:, lo:lo + LANES]


def _cache_rows_load(ref, b):
    cols = [ref[b, pl.ds(lt * XA_HEADS + h, N_MEM, stride=CACHE_ROW_GROUP), :]
            for h in range(XA_HEADS) for lt in range(XA_LANE_TILES)]
    return jnp.concatenate(cols, axis=1).astype(BF16)


def _xattn_cache_stages(q_ref, k_ref, v_ref, o_ref):
    bb = k_ref.shape[0]
    tq = q_ref.shape[0] // bb
    lane_head = lax.broadcasted_iota(jnp.int32, (1, D_MODEL), 1) // XA_DH
    qs = [q_ref[b * tq:(b + 1) * tq, :] for b in range(bb)]
    q_bds = [jnp.concatenate([jnp.where(lane_head == h, q, 0.0) for h in range(XA_HEADS)], axis=0).astype(BF16)
             for q in qs]
    k_fulls = [_cache_rows_load(k_ref, b) for b in range(bb)]
    yield
    scores = [_dot_nt(q_bds[b], k_fulls[b]) * (XA_DH ** -0.5) for b in range(bb)]
    yield
    p_all = _softmax(jnp.concatenate(scores, axis=0)).astype(BF16)
    v_fulls = [_cache_rows_load(v_ref, b) for b in range(bb)]
    yield
    rows = XA_HEADS * tq
    o_fulls = [_dot(p_all[b * rows:(b + 1) * rows], v_fulls[b]) for b in range(bb)]
    yield
    for b in range(bb):
        o = jnp.zeros((tq, D_MODEL), F32)
        for h in range(XA_HEADS):
            o = jnp.where(lane_head == h, o_fulls[b][h * tq:(h + 1) * tq], o)
        o_ref[b * tq:(b + 1) * tq, :] = o.astype(o_ref.dtype)


def _ffn_out_kernel(x2_ref, o_ref, wo_ref, g2_ref, wg_ref, wu_ref, wd_ref, gf_ref, y_ref):
    x3 = x2_ref[...] + _dot(o_ref[...].astype(BF16), wo_ref[...])
    x4 = _swiglu_residual(x3, g2_ref, wg_ref, wu_ref, wd_ref)
    y_ref[...] = _rms(x4, gf_ref[...])


def _ffn_out(x2, o, w, tm):
    n = x2.shape[0]
    row = lambda: pl.BlockSpec((tm, D_MODEL), lambda i: (i, 0))
    return pl.pallas_call(
        _ffn_out_kernel,
        grid=(n // tm,),
        in_specs=[row(), row(), _const_spec((D_MODEL, D_MODEL)), _const_spec((1, D_MODEL)),
                  _const_spec((D_MODEL, D_FF)), _const_spec((D_MODEL, D_FF)), _const_spec((D_FF, D_MODEL)),
                  _const_spec((1, D_MODEL))],
        out_specs=row(),
        out_shape=jax.ShapeDtypeStruct((n, D_MODEL), F32),
        compiler_params=_params(("arbitrary",)),
        name="ffn_out",
    )(x2, o, w["xattn_wo"], w["ffn2_g"], w["ffn2_wg"], w["ffn2_wu"], w["ffn2_wd"], w["final_g"])


def _memkv_kernel(m_ref, g_ref, wk_ref, wv_ref, k_ref, v_ref):
    hn = _rms(m_ref[0], g_ref[...]).astype(BF16)
    _cache_rows_store(k_ref, 0, _dot(hn, wk_ref[...]))
    _cache_rows_store(v_ref, 0, _dot(hn, wv_ref[...]))


def _memkv(mem, w):
    bsz = mem.shape[0]
    rows = pl.BlockSpec((1, N_MEM * CACHE_ROW_GROUP, LANES), lambda b: (b, 0, 0))
    return pl.pallas_call(
        _memkv_kernel,
        grid=(bsz,),
        in_specs=[pl.BlockSpec((1, N_MEM, D_MODEL), lambda b: (b, 0, 0)), _const_spec((1, D_MODEL)),
                  _const_spec((D_MODEL, D_MODEL)), _const_spec((D_MODEL, D_MODEL))],
        out_specs=[rows, rows],
        out_shape=[jax.ShapeDtypeStruct((bsz, N_MEM * CACHE_ROW_GROUP, LANES), F32)] * 2,
        compiler_params=_params(("arbitrary",)),
        name="memkv",
    )(mem, w["mem_g"], w["xattn_wk"], w["xattn_wv"])


def _prep_weights(p):
    bf = lambda a: a.astype(BF16)
    row = lambda a: a.reshape(1, -1).astype(F32)
    w_in = p["w_in"]
    off = {}
    pos = 0
    for name, width in (("mq", 512), ("mk", 512), ("mv", 512), ("mi", 4), ("mf", 4), ("mo", 512), ("gq", 256),
                        ("gk", 256), ("gv", 512), ("ga", 16), ("gg", 512)):
        off[name] = w_in[:, pos:pos + width]
        pos += width
    pad_cols = SMALL_W - 2 * ML_HEADS - GLA_RANK
    w_in_q = jnp.concatenate([off[k] for k in ("mq", "mk", "mv", "gq", "gk", "gv")], axis=1)
    w_in_g = jnp.concatenate([off["mo"], off["gg"], off["mi"], off["mf"], off["ga"],
                              jnp.zeros((D_MODEL, pad_cols), F32)], axis=1)
    gate_bias = jnp.concatenate([p["mlstm_b_i"], p["mlstm_b_f"], jnp.zeros((SMALL_W - 2 * ML_HEADS,), F32)])
    w_a2 = jnp.zeros((SMALL_W, GLA_QK_W), F32).at[2 * ML_HEADS:2 * ML_HEADS + GLA_RANK].set(p["gla_w_a2"])
    lane = jnp.arange(LANES)
    same_head = (lane[:, None] // GLA_DK) == (lane[None, :] // GLA_DK)
    diag_sum = jnp.stack([same_head & ((lane[None, :] % GLA_SUB) == j) for j in range(GLA_SUB)]).astype(BF16)
    return dict(
        ffn1_g=row(p["ffn1_norm_g"]), ffn1_wg=bf(p["ffn1_w_gate"]), ffn1_wu=bf(p["ffn1_w_up"]),
        ffn1_wd=bf(p["ffn1_w_down"]), mix_g=row(p["mix_norm_g"]), w_in_q=bf(w_in_q), w_in_g=bf(w_in_g),
        gate_bias=row(gate_bias), w_a2=bf(w_a2), b_a=row(p["gla_b_a"]), mlstm_out_g=row(p["mlstm_out_g"]),
        gla_out_g=row(p["gla_out_g"]), diag_sum=diag_sum,
        w_out=bf(p["w_out"]), xattn_g=row(p["xattn_norm_g"]), xattn_wq=bf(p["xattn_w_q"]),
        mem_g=row(p["mem_norm_g"]), ffn2_g=row(p["ffn2_norm_g"]), final_g=row(p["final_g"]),
        xattn_wo=p["xattn_w_o"], xattn_wk=p["xattn_w_k"], xattn_wv=p["xattn_w_v"], ffn2_wg=p["ffn2_w_gate"],
        ffn2_wu=p["ffn2_w_up"], ffn2_wd=p["ffn2_w_down"])


LATE_WEIGHTS = ("xattn_wo", "xattn_wk", "xattn_wv", "ffn2_wg", "ffn2_wu", "ffn2_wd")


def kernel(x_prompt, x_sample, mem_prompt, cache_mem_k, cache_mem_v, state_mlstm_c, state_mlstm_n, state_mlstm_m, state_gla_s, ffn1_norm_g, ffn1_w_gate, ffn1_w_up, ffn1_w_down, mix_norm_g, w_in, mlstm_b_i, mlstm_b_f, mlstm_out_g, gla_w_a2, gla_b_a, gla_out_g, w_out, xattn_norm_g, mem_norm_g, xattn_w_q, xattn_w_k, xattn_w_v, xattn_w_o, ffn2_norm_g, ffn2_w_gate, ffn2_w_up, ffn2_w_down, final_norm_g):
    assert ffn1_norm_g.shape[0] == 1, "single-layer stack"
    layer = dict(ffn1_norm_g=ffn1_norm_g, ffn1_w_gate=ffn1_w_gate, ffn1_w_up=ffn1_w_up, ffn1_w_down=ffn1_w_down,
                 mix_norm_g=mix_norm_g, w_in=w_in, mlstm_b_i=mlstm_b_i, mlstm_b_f=mlstm_b_f,
                 mlstm_out_g=mlstm_out_g, gla_w_a2=gla_w_a2, gla_b_a=gla_b_a, gla_out_g=gla_out_g, w_out=w_out,
                 xattn_norm_g=xattn_norm_g, mem_norm_g=mem_norm_g, xattn_w_q=xattn_w_q, xattn_w_k=xattn_w_k,
                 xattn_w_v=xattn_w_v, xattn_w_o=xattn_w_o, ffn2_norm_g=ffn2_norm_g, ffn2_w_gate=ffn2_w_gate,
                 ffn2_w_up=ffn2_w_up, ffn2_w_down=ffn2_w_down)
    p = {name: arr[0] for name, arr in layer.items()}
    p["final_g"] = final_norm_g
    w = _prep_weights(p)

    bp, tp, _ = x_prompt.shape
    bs, ts, _ = x_sample.shape

    state = (state_mlstm_c[0], state_mlstm_n[0], state_mlstm_m[0].reshape(bs, 1, ML_HEADS), state_gla_s[0])
    x1_s, zq_s, zg_s = _ffn_in(x_sample.reshape(bs * ts, D_MODEL), w, TM_FFN_IN, F32)
    ym_s, c_s, n_s, m_s, s_s = _mixer_state(zq_s, zg_s, w, state, ts, SAMPLE_MIXER_BATCHES)
    states_s = (c_s[None], n_s[None], m_s.reshape(1, bs, ML_HEADS), s_s[None])
    x2_s, q_s = _post_mix(x1_s, ym_s, w, TM_FFN_IN, F32)

    x1_p, ym_p, states_p, o_s, late = _ffn_mix(x_prompt, w, TM_FFN_IN, q_s, cache_mem_k[0], cache_mem_v[0],
                                               [w[name] for name in LATE_WEIGHTS])
    w.update(zip(LATE_WEIGHTS, late))
    mem_k_p, mem_v_p = _memkv(mem_prompt, w)
    y_p = _post_fused(x1_p.reshape(bp, tp, D_MODEL), ym_p.reshape(bp, tp, D_MODEL), mem_k_p, mem_v_p, w, TM_POST)
    y_s = _ffn_out(x2_s, o_s, w, TM_FFN_IN).reshape(bs, ts, D_MODEL)

    return (y_p, y_s, _cache_rows_unview(mem_k_p)[None], _cache_rows_unview(mem_v_p)[None]) + states_p + states_s
```

```python
import functools

import jax
import jax.numpy as jnp
from jax import lax
from jax.experimental import pallas as pl
from jax.experimental.pallas import tpu as pltpu

F32 = jnp.float32
BF16 = jnp.bfloat16

D_MODEL = 1024
D_FF = 2816
ML_HEADS = 4
ML_DK = 128
ML_DV = 128
GLA_HEADS = 4
GLA_DK = 64
GLA_DV = 128
GLA_RANK = 16
GLA_TAU = 16.0
N_MEM = 256
XA_HEADS = 4
XA_DH = D_MODEL // XA_HEADS
EPS = 1e-6
CHUNK = 64
LOG2_E = 1.4426950408889634
LN_2 = 0.6931471805599453
LANES = 128
SUBLANES = 8
BF16_SUBLANES = 2 * SUBLANES

ZQ_MQ, ZQ_MK, ZQ_MV = 0, 512, 1024
ZQ_GQ, ZQ_GK, ZQ_GV = 1536, 1792, 2048
ZQ_W = 2560
ZG_MO, ZG_GG, ZG_SMALL = 0, 512, 1024
ZG_W = 1152
SMALL_W = LANES
GLA_QK_W = GLA_HEADS * GLA_DK
GLA_V_W = GLA_HEADS * GLA_DV
GLA_PAIRS = GLA_HEADS // 2
GLA_SUB = SUBLANES

MXU_WIDTH = 256
FF_CHUNKS = ((0, 6 * MXU_WIDTH), (6 * MXU_WIDTH, D_FF))
VMEM_LIMIT_BYTES = 56 * 1024 * 1024

TM_FFN_IN = MXU_WIDTH
TM_POST = 2 * MXU_WIDTH
SAMPLE_MIXER_BATCHES = 16


def _rms(x, g):
    return x * lax.rsqrt(jnp.mean(x * x, axis=-1, keepdims=True) + EPS) * g


def _log_sigmoid(x):
    return jnp.minimum(x, 0.0) - jnp.log1p(jnp.exp(-jnp.abs(x)))


def _dot(a, b):
    return jnp.dot(a, b, preferred_element_type=F32)


def _dot_nt(a, b):
    return lax.dot_general(a, b, (((1,), (1,)), ((), ())), preferred_element_type=F32)


def _dot_f32(a, b):
    return jnp.dot(a, b, precision=lax.Precision.HIGHEST, preferred_element_type=F32)


def _swiglu_residual(x, g_ref, wg_ref, wu_ref, wd_ref):
    h = _rms(x, g_ref[...]).astype(BF16)
    acts = []
    for lo, hi in FF_CHUNKS:
        g = _dot(h, wg_ref[:, lo:hi])
        u = _dot(h, wu_ref[:, lo:hi])
        acts.append(((g * jax.nn.sigmoid(g)) * u).astype(BF16))
    acc = jnp.zeros_like(x)
    for (lo, hi), a in zip(FF_CHUNKS, acts):
        acc = acc + _dot(a, wd_ref[lo:hi, :])
    return x + 0.5 * acc


def _const_spec(shape):
    nd = len(shape)
    return pl.BlockSpec(shape, lambda *_: (0,) * nd, pipeline_mode=pl.Buffered(1))


def _params(sem):
    return pltpu.CompilerParams(dimension_semantics=sem, vmem_limit_bytes=VMEM_LIMIT_BYTES)


def _ffn_in_kernel(x_ref, g1_ref, wg_ref, wu_ref, wd_ref, gm_ref, wq_ref, wgt_ref, x1_ref, zq_ref, zg_ref):
    _drain(_ffn_in_pieces(x_ref[...], g1_ref, wg_ref, wu_ref, wd_ref, gm_ref, wq_ref, wgt_ref, x1_ref, zq_ref, zg_ref))


def _ffn_in(x, w, tm, zq_dtype):
    n = x.shape[0]
    row = lambda width: pl.BlockSpec((tm, width), lambda i: (i, 0))
    return pl.pallas_call(
        _ffn_in_kernel,
        grid=(n // tm,),
        in_specs=[row(D_MODEL), _const_spec((1, D_MODEL)), _const_spec((D_MODEL, D_FF)),
                  _const_spec((D_MODEL, D_FF)), _const_spec((D_FF, D_MODEL)), _const_spec((1, D_MODEL)),
                  _const_spec((D_MODEL, ZQ_W)), _const_spec((D_MODEL, ZG_W))],
        out_specs=[row(D_MODEL), row(ZQ_W), row(ZG_W)],
        out_shape=[jax.ShapeDtypeStruct((n, D_MODEL), F32), jax.ShapeDtypeStruct((n, ZQ_W), zq_dtype),
                   jax.ShapeDtypeStruct((n, ZG_W), F32)],
        compiler_params=_params(("arbitrary",)),
        name="ffn_in",
    )(x, w["ffn1_g"], w["ffn1_wg"], w["ffn1_wu"], w["ffn1_wd"], w["mix_g"], w["w_in_q"], w["w_in_g"])


def _mixer_stages(zqs, zgs, states, consts, ee_ref, *, L, t_real, chain):
    bias, wa2, ba, gml, ggl = consts
    nchunk = len(zqs)
    groups = [(b, h) for b in range(nchunk) for h in range(ML_HEADS)]
    pairs = [(b, p) for b in range(nchunk) for p in range(GLA_PAIRS)]
    padded = t_real < L
    valid = lax.broadcasted_iota(jnp.int32, (L, 1), 0) < t_real
    rr = lax.broadcasted_iota(jnp.int32, (L, L), 0)
    cc = lax.broadcasted_iota(jnp.int32, (L, L), 1)
    tril = cc <= rr
    c = GLA_SUB
    nb = L // c
    lane = lax.broadcasted_iota(jnp.int32, (1, LANES), 1)
    lane_blk = (lane % GLA_DK) // c
    lane_head = lane // GLA_DK
    v_lane_head = lax.broadcasted_iota(jnp.int32, (1, 2 * GLA_DV), 1) // GLA_DV
    row_blk = (lax.broadcasted_iota(jnp.int32, (2 * L, 1), 0) % L) // c
    t_in = lax.broadcasted_iota(jnp.int32, (1, c, 1), 1)
    new_states = [dict(c=[None] * ML_HEADS, n=[None] * ML_HEADS, m=[None] * ML_HEADS, s=[None] * GLA_HEADS)
                  for _ in range(nchunk)]
    prev = lambda b: new_states[b - 1] if chain and b > 0 else states[0 if chain else b]
    known = lambda b: not chain or b == 0

    carried = {}

    def carried_mlstm(b, h):
        st = prev(b)
        d = ml[(b, h)]
        carried[(b, h)] = _dot(d["qb"], st["c"][h].astype(BF16))

    def carried_gla(b, p):
        s_prev = prev(b)["s"]
        zero_blk = jnp.zeros((GLA_DK, GLA_DV), BF16)
        s_bd = jnp.concatenate(
            [jnp.concatenate([s_prev[2 * p].astype(BF16), zero_blk], axis=1),
             jnp.concatenate([zero_blk, s_prev[2 * p + 1].astype(BF16)], axis=1)], axis=0)
        carried[(b, "gla", p)] = _dot(gl[(b, p)]["q_dec"], s_bd)

    smalls, sms, lfs, b_cols, b_rows, sm_ts, las, bcs = [], [], [], [], [], [], [], []
    for b in range(nchunk):
        small = zgs[b][:, ZG_SMALL:ZG_SMALL + SMALL_W]
        sm = small + bias
        lf = _log_sigmoid(sm) * LOG2_E
        sm = sm * LOG2_E
        if padded:
            sm = jnp.where(valid, sm, -jnp.inf)
            lf = jnp.where(valid, lf, 0.0)
        smalls.append(small)
        sms.append(sm)
        lfs.append(lf)

    long_chunk = L > SUBLANES
    col_w = 1 if long_chunk else LANES
    tril_f, triu_f = tril.astype(F32), (rr <= cc).astype(F32)

    def cumsum_rows(x):
        if long_chunk:
            return _dot_f32(tril_f, x)
        acc = x[0:1]
        rows = [acc]
        for r in range(1, L):
            acc = acc + x[r:r + 1]
            rows.append(acc)
        return jnp.concatenate(rows, axis=0)

    lf_ts = [lf.T[0:SUBLANES] for lf in lfs] if long_chunk else None
    yield
    for b in range(nchunk):
        b_cols.append(cumsum_rows(lfs[b]))
        b_rows.append(_dot_f32(lf_ts[b], triu_f) if long_chunk else b_cols[b].T[0:SUBLANES])
        sm_ts.append(sms[b].T)
        la = _log_sigmoid(_dot(smalls[b].astype(BF16), wa2) + ba) * (LOG2_E / GLA_TAU)
        las.append(jnp.where(valid, la, 0.0) if padded else la)

    ml = {}
    for g in groups:
        b, h = g
        zq = zqs[b]
        qf = zq[:, ZQ_MQ + h * ML_DK:ZQ_MQ + (h + 1) * ML_DK].astype(F32)
        kf = zq[:, ZQ_MK + h * ML_DK:ZQ_MK + (h + 1) * ML_DK].astype(F32) * (ML_DK ** -0.5)
        vf = zq[:, ZQ_MV + h * ML_DV:ZQ_MV + (h + 1) * ML_DV].astype(F32)
        if padded:
            kf = jnp.where(valid, kf, 0.0)
            vf = jnp.where(valid, vf, 0.0)
        ml[g] = dict(qf=qf, qb=qf.astype(BF16), kf=kf, kb=kf.astype(BF16), vb=vf.astype(BF16))
    yield
    for b in range(nchunk):
        bcs.append(cumsum_rows(las[b]))
    for g in groups:
        d = ml[g]
        d["qk"] = _dot_nt(d["qb"], d["kb"])
        if known(g[0]):
            carried_mlstm(*g)

    yield
    for g in groups:
        b, h = g
        d = ml[g]
        i_col = jnp.broadcast_to(sms[b][:, h:h + 1], (L, col_w))
        b_col = jnp.broadcast_to(b_cols[b][:, ML_HEADS + h:ML_HEADS + h + 1], (L, col_w))
        b_row = b_rows[b][ML_HEADS + h:ML_HEADS + h + 1, :]
        i_row = sm_ts[b][h:h + 1, :]
        a_col = b_col + prev(b)["m"][h] * LOG2_E
        dm = jnp.where(tril, b_col[:, :L] - (b_row - i_row), -jnp.inf)
        mt = jnp.maximum(a_col, jnp.max(dm, axis=1, keepdims=True))
        w_inter = jnp.exp2(a_col - mt)
        s = d["qk"] * jnp.exp2(dm - mt[:, :L])
        kw = d["kf"] * jnp.exp2((b_col[L - 1:L] - mt[L - 1:L]) - (b_col - i_col))
        d.update(mt=mt, w_inter=w_inter, s=s, kw=kw, kw_t=kw.T.astype(BF16))
        new_states[b]["m"][h] = mt[L - 1:L, 0:1] * LN_2

    gl = {}
    for b in range(nchunk):
        zq = zqs[b]
        gq = zq[:, ZQ_GQ:ZQ_GQ + GLA_QK_W].astype(F32) * (GLA_DK ** -0.5)
        gk = zq[:, ZQ_GK:ZQ_GK + GLA_QK_W].astype(F32)
        gv = zq[:, ZQ_GV:ZQ_GV + GLA_V_W].astype(F32)
        if padded:
            gk = jnp.where(valid, gk, 0.0)
            gv = jnp.where(valid, gv, 0.0)
        stack = lambda x: jnp.concatenate([x[:, :LANES], x[:, LANES:]], axis=0)
        q2, k2, b2 = stack(gq), stack(gk), stack(bcs[b])
        q3 = q2.reshape(2 * nb, c, LANES)
        k3 = k2.reshape(2 * nb, c, LANES)
        b3 = b2.reshape(2 * nb, c, LANES)
        pair_terms = []
        for j in range(min(c, t_real)):
            decay = jnp.exp2(jnp.where(t_in >= j, b3 - b3[:, j:j + 1, :], -jnp.inf))
            pair_terms.append((q3 * k3[:, j:j + 1, :] * decay).reshape(2 * L, LANES).astype(BF16))
        kt2 = (k3 * jnp.exp2(b3[:, c - 1:c, :] - b3)).reshape(2 * L, LANES) if nb > 1 else None
        gl[b] = dict(gv=gv, q2=q2, k2=k2, b2=b2, pair_terms=pair_terms, kt2=kt2)
    yield
    for b in range(nchunk):
        acc = jnp.zeros((2 * L, LANES), F32)
        for j, pair_j in enumerate(gl[b]["pair_terms"]):
            acc = acc + _dot(pair_j, ee_ref[j])
        gl[b]["a_diag"] = jnp.where(lane_blk == row_blk, acc, 0.0)

    for bp in pairs:
        b, p = bp
        d = gl[b]
        rows_p = slice(p * L, (p + 1) * L)
        q_p, k_p, b_p = d["q2"][rows_p], d["k2"][rows_p], d["b2"][rows_p]
        if long_chunk:
            decay_col = jnp.exp2(b_p[L - SUBLANES:L].T[:, SUBLANES - 1:SUBLANES])
        else:
            decay_col = jnp.exp2(jnp.broadcast_to(b_p[L - 1:L], (LANES, LANES)).T)
        e = dict(q_dec=(q_p * jnp.exp2(b_p)).astype(BF16),
                 kh_t=(k_p * jnp.exp2(b_p[L - 1:L] - b_p)).T.astype(BF16),
                 decay_col=decay_col)
        if nb > 1:
            kt_p = d["kt2"][rows_p].astype(BF16)
            k_bd = jnp.concatenate([jnp.where(lane_head == hh, kt_p, 0.0) for hh in range(2)], axis=0)
            slabs, offs = [], []
            off = 0
            for j in range(nb - 1):
                lo = (j + 1) * c
                slabs.append(q_p[lo:] * jnp.exp2(b_p[lo:] - b_p[lo - 1:lo]))
                offs.append(off)
                off += L - lo
            e.update(q_var=jnp.concatenate(slabs, axis=0).astype(BF16), k_bd=k_bd, offs=offs)
        gl[bp] = e
    yield
    for bp in pairs:
        e = gl[bp]
        if nb > 1:
            e["r"] = _dot_nt(e["q_var"], e["k_bd"])
        if known(bp[0]):
            carried_gla(*bp)

    for g in groups:
        d = ml[g]
        d["sv"] = _dot(d["s"].astype(BF16), d["vb"])
        d["c_upd"] = _dot(d["kw_t"], d["vb"])
        d["den"] = jnp.sum(d["s"], axis=1, keepdims=True)
        if known(g[0]):
            d["qn"] = jnp.sum(d["qf"] * prev(g[0])["n"][g[1]], axis=1, keepdims=True)
    yield
    for bp in pairs:
        b, p = bp
        d, e = gl[b], gl[bp]
        a_p = d["a_diag"][p * L:(p + 1) * L]
        if nb > 1:
            blocks = []
            for i in range(nb):
                blk = a_p[i * c:(i + 1) * c]
                for j in range(i):
                    lo_r = e["offs"][j] + (i - j - 1) * c
                    blk = jnp.where(lane_blk == j, e["r"][lo_r:lo_r + c], blk)
                blocks.append(blk)
            a_p = jnp.concatenate(blocks, axis=0)
        v_f = d["gv"][:, p * 2 * GLA_DV:(p + 1) * 2 * GLA_DV]
        v_p = v_f.astype(BF16)
        if L < GLA_DK:
            v_rows = []
            for hh in range(2):
                v_rows += [jnp.where(v_lane_head == hh, v_f, 0.0), jnp.zeros((GLA_DK - L, 2 * GLA_DV), F32)]
            v_bd = jnp.concatenate(v_rows, axis=0).astype(BF16)
        else:
            v_bd = jnp.concatenate([jnp.where(v_lane_head == hh, v_p, 0.0) for hh in range(2)], axis=0)
        e["s_upd"] = [_dot(e["kh_t"][hh * GLA_DK:(hh + 1) * GLA_DK], v_p[:, hh * GLA_DV:(hh + 1) * GLA_DV])
                      for hh in range(2)]
        e.update(scores=a_p.astype(BF16), v_bd=v_bd)
    yield
    for bp in pairs:
        e = gl[bp]
        e["o_intra"] = _dot(e["scores"], e["v_bd"])

    yield
    ys = [[None] * (ML_HEADS + GLA_HEADS) for _ in range(nchunk)]

    def finish_mlstm(b):
        st = prev(b)
        for h in range(ML_HEADS):
            d = ml[(b, h)]
            mt, w_inter = d["mt"], d["w_inter"]
            qn = d["qn"] if known(b) else jnp.sum(d["qf"] * st["n"][h], axis=1, keepdims=True)
            den = d["den"] + w_inter * qn
            hh = (d["sv"] + w_inter * carried[(b, h)]) / jnp.maximum(jnp.abs(den), jnp.exp2(-mt))
            i_last = w_inter[L - 1:L]
            new_states[b]["c"][h] = i_last * st["c"][h] + d["c_upd"]
            new_states[b]["n"][h] = i_last * st["n"][h] + jnp.sum(d["kw"], axis=0, keepdims=True)
            hs = slice(h * ML_DV, (h + 1) * ML_DV)
            yn = hh * lax.rsqrt(jnp.mean(hh * hh, axis=-1, keepdims=True) + EPS) * gml[:, hs]
            ys[b][h] = jax.nn.sigmoid(zgs[b][:, ZG_MO + h * ML_DV:ZG_MO + (h + 1) * ML_DV]) * yn

    def finish_gla(b):
        st = prev(b)
        for p in range(GLA_PAIRS):
            e = gl[(b, p)]
            o = carried[(b, "gla", p)] + e["o_intra"]
            for hh in range(2):
                h = 2 * p + hh
                ds = slice(hh * GLA_DK, (hh + 1) * GLA_DK)
                vs = slice(hh * GLA_DV, (hh + 1) * GLA_DV)
                new_states[b]["s"][h] = e["decay_col"][ds] * st["s"][h] + e["s_upd"][hh]
                oh = o[:, vs]
                yn = oh * lax.rsqrt(jnp.mean(oh * oh, axis=-1, keepdims=True) + EPS) * ggl[:, h * GLA_DV:(h + 1) * GLA_DV]
                gg = zgs[b][:, ZG_GG + h * GLA_DV:ZG_GG + (h + 1) * GLA_DV]
                ys[b][ML_HEADS + h] = (gg * jax.nn.sigmoid(gg)) * yn

    if chain:
        for b in range(nchunk):
            if not known(b):
                for h in range(ML_HEADS):
                    carried_mlstm(b, h)
                for p in range(GLA_PAIRS):
                    carried_gla(b, p)
            finish_mlstm(b)
            finish_gla(b)
    else:
        for b in range(nchunk):
            finish_mlstm(b)
        for b in range(nchunk):
            finish_gla(b)
    return [jnp.concatenate(y, axis=1) for y in ys], new_states


def _drain(gen):
    try:
        while True:
            next(gen)
    except StopIteration as done:
        return done.value


def _mixer_compute(*args, **kwargs):
    return _drain(_mixer_stages(*args, **kwargs))


def _mixer_state_kernel(zq_ref, zg_ref, bias_ref, wa2_ref, ba_ref, gml_ref, ggl_ref, ee_ref, c0_ref, n0_ref, m0_ref,
                        s0_ref, y_ref, co_ref, no_ref, mo_ref, so_ref, *, L, t_real, nblk):
    consts = (bias_ref[...], wa2_ref[...], ba_ref[...], gml_ref[...], ggl_ref[...])
    lane = lax.broadcasted_iota(jnp.int32, (1, LANES), 1)

    def chunk_rows(ref, b):
        rows = ref[b * t_real:(b + 1) * t_real, :]
        return jnp.concatenate([rows, jnp.zeros((L - t_real, rows.shape[1]), rows.dtype)], axis=0)

    states = []
    for b in range(nblk):
        n_all = n0_ref[b]
        m_all = m0_ref[b]
        states.append(dict(c=[c0_ref[b, h] for h in range(ML_HEADS)],
                           n=[n_all[h:h + 1, :] for h in range(ML_HEADS)],
                           m=[m_all[:, h:h + 1] for h in range(ML_HEADS)],
                           s=[s0_ref[b, h] for h in range(GLA_HEADS)]))
    ys, new_states = _mixer_compute([chunk_rows(zq_ref, b) for b in range(nblk)],
                                    [chunk_rows(zg_ref, b) for b in range(nblk)], states, consts, ee_ref, L=L,
                                    t_real=t_real, chain=False)
    for b in range(nblk):
        st = new_states[b]
        y_ref[b * t_real:(b + 1) * t_real, :] = ys[b][0:t_real].astype(y_ref.dtype)
        for h in range(ML_HEADS):
            co_ref[b, h] = st["c"][h]
            so_ref[b, h] = st["s"][h]
        no_ref[b] = jnp.concatenate(st["n"], axis=0)
        m_row = jnp.zeros((1, LANES), F32)
        for h in range(ML_HEADS):
            m_row = jnp.where(lane == h, st["m"][h], m_row)
        mo_ref[b] = m_row[:, 0:ML_HEADS]


def _mixer_state(zq, zg, w, state, t_real, nblk):
    bsz = zq.shape[0] // t_real
    L = -(-t_real // SUBLANES) * SUBLANES
    rows = lambda width: pl.BlockSpec((nblk * t_real, width), lambda b: (b, 0))
    per_b = lambda *tail: pl.BlockSpec((nblk,) + tail, lambda b: (b,) + (0,) * len(tail))
    state_specs = [per_b(ML_HEADS, ML_DK, ML_DV), per_b(ML_HEADS, ML_DK), per_b(1, ML_HEADS),
                   per_b(GLA_HEADS, GLA_DK, GLA_DV)]
    consts = [_const_spec((1, SMALL_W)), _const_spec((SMALL_W, GLA_QK_W)), _const_spec((1, GLA_QK_W)),
              _const_spec((1, ML_HEADS * ML_DV)), _const_spec((1, GLA_V_W)), _const_spec((GLA_SUB, LANES, LANES))]
    return pl.pallas_call(
        functools.partial(_mixer_state_kernel, L=L, t_real=t_real, nblk=nblk),
        grid=(bsz // nblk,),
        in_specs=[rows(ZQ_W), rows(ZG_W)] + consts + state_specs,
        out_specs=[rows(D_MODEL)] + state_specs,
        out_shape=[jax.ShapeDtypeStruct((bsz * t_real, D_MODEL), F32),
                   jax.ShapeDtypeStruct((bsz, ML_HEADS, ML_DK, ML_DV), F32),
                   jax.ShapeDtypeStruct((bsz, ML_HEADS, ML_DK), F32),
                   jax.ShapeDtypeStruct((bsz, 1, ML_HEADS), F32),
                   jax.ShapeDtypeStruct((bsz, GLA_HEADS, GLA_DK, GLA_DV), F32)],
        compiler_params=_params(("arbitrary",)),
        name="mixer_state",
    )(zq, zg, w["gate_bias"], w["w_a2"], w["b_a"], w["mlstm_out_g"], w["gla_out_g"], w["diag_sum"], *state)


def _ffn_in_pieces(x, g1_ref, wg_ref, wu_ref, wd_ref, gm_ref, wq_ref, wgt_ref, x1_ref, zq_out, zg_out):
    h = _rms(x, g1_ref[...]).astype(BF16)
    acts = []
    for lo in range(0, D_FF, MXU_WIDTH):
        g = _dot(h, wg_ref[:, lo:lo + MXU_WIDTH])
        yield
        u = _dot(h, wu_ref[:, lo:lo + MXU_WIDTH])
        yield
        acts.append(((g * jax.nn.sigmoid(g)) * u).astype(BF16))
    a = jnp.concatenate(acts, axis=1)
    down = []
    for lo in range(0, D_MODEL, MXU_WIDTH):
        down.append(_dot(a, wd_ref[:, lo:lo + MXU_WIDTH]))
        yield
    x1 = x + 0.5 * jnp.concatenate(down, axis=1)
    x1_ref[...] = x1
    hm = _rms(x1, gm_ref[...]).astype(BF16)
    for lo in range(0, ZQ_W, MXU_WIDTH):
        zq_out[:, lo:lo + MXU_WIDTH] = _dot(hm, wq_ref[:, lo:lo + MXU_WIDTH]).astype(zq_out.dtype)
        yield
    for lo in range(0, ZG_W, MXU_WIDTH):
        hi = min(lo + MXU_WIDTH, ZG_W)
        zg_out[:, lo:hi] = _dot(hm, wgt_ref[:, lo:hi])
        yield


ATTN_STAGE_EVERY = 7
FFN_PIECES_PER_MIXER_STAGE = 1


def _ffn_mix_kernel(*refs, tiles_per_batch, n_cast):
    it = iter(refs)
    (x_ref, g1_ref, wg_ref, wu_ref, wd_ref, gm_ref, wq_ref, wgt_ref, bias_ref, wa2_ref, ba_ref, gml_ref, ggl_ref,
     ee_ref, qs_ref, ks_ref, vs_ref) = (next(it) for _ in range(17))
    cast_src = [next(it) for _ in range(n_cast)]
    x1_ref, ym_ref, co_ref, no_ref, mo_ref, so_ref, os_ref = (next(it) for _ in range(7))
    cast_dst = [next(it) for _ in range(n_cast)]
    zq_s, zg_s, c_s, n_s, m_s, s_s = (next(it) for _ in range(6))
    i = pl.program_id(0)
    for src, dst in zip(cast_src, cast_dst):
        dst[...] = src[...].astype(dst.dtype)

    @pl.when(i == 0)
    def _init():
        for ref in (zq_s, zg_s, c_s, n_s, m_s, s_s):
            ref[...] = jnp.zeros_like(ref)

    consts = (bias_ref[...], wa2_ref[...], ba_ref[...], gml_ref[...], ggl_ref[...])
    n_chunks = zq_s.shape[0] // CHUNK
    rows = [slice(k * CHUNK, (k + 1) * CHUNK) for k in range(n_chunks)]
    zqs = [zq_s[r, :] for r in rows]
    zgs = [zg_s[r, :] for r in rows]
    starts_batch = (i - 1) % tiles_per_batch == 0
    carry = lambda v: jnp.where(starts_batch, 0.0, v)
    state = dict(c=[carry(c_s[h]) for h in range(ML_HEADS)], n=[carry(n_s[h:h + 1, :]) for h in range(ML_HEADS)],
                 m=[carry(m_s[h:h + 1, 0:1]) for h in range(ML_HEADS)], s=[carry(s_s[h]) for h in range(GLA_HEADS)])

    def mixers():
        st = state
        for k in range(n_chunks):
            ys, new = yield from _mixer_stages([zqs[k]], [zgs[k]], [st], consts, ee_ref, L=CHUNK, t_real=CHUNK,
                                               chain=True)
            ym_ref[rows[k], :] = ys[0].astype(ym_ref.dtype)
            st = new[0]
            yield
        return st

    ffn = _ffn_in_pieces(x_ref[...], g1_ref, wg_ref, wu_ref, wd_ref, gm_ref, wq_ref, wgt_ref, x1_ref, zq_s, zg_s)
    mix = mixers()
    attn = _xattn_cache_stages(qs_ref, ks_ref, vs_ref, os_ref)
    live = {"ffn": True, "attn": True}

    def advance(name, gen):
        if live[name]:
            try:
                next(gen)
            except StopIteration:
                live[name] = False

    slot = 0
    while True:
        try:
            next(mix)
        except StopIteration as done:
            last = done.value
            break
        for _ in range(FFN_PIECES_PER_MIXER_STAGE):
            advance("ffn", ffn)
        if slot % ATTN_STAGE_EVERY == ATTN_STAGE_EVERY // 2:
            advance("attn", attn)
        slot += 1
    for name, gen in (("ffn", ffn), ("attn", attn)):
        if live[name]:
            _drain(gen)

    for h in range(ML_HEADS):
        c_s[h] = last["c"][h]
        n_s[h:h + 1, :] = last["n"][h]
        m_s[h:h + 1, :] = jnp.broadcast_to(last["m"][h], (1, LANES))
        s_s[h] = last["s"][h]

    @pl.when(jnp.logical_and(i >= 1, (i - 1) % tiles_per_batch == tiles_per_batch - 1))
    def _emit_state():
        lane = lax.broadcasted_iota(jnp.int32, (1, LANES), 1)
        for h in range(ML_HEADS):
            co_ref[0, h] = last["c"][h]
            so_ref[0, h] = last["s"][h]
        no_ref[0] = jnp.concatenate(last["n"], axis=0)
        m_row = jnp.zeros((1, LANES), F32)
        for h in range(ML_HEADS):
            m_row = jnp.where(lane == h, last["m"][h], m_row)
        mo_ref[0] = m_row[:, 0:ML_HEADS]


def _ffn_mix(x, w, tm, q_s, k_cache, v_cache, to_cast):
    bsz, t, _ = x.shape
    tiles_per_batch = t // tm
    n_tiles = bsz * tiles_per_batch
    assert t % tm == 0 and tm % CHUNK == 0
    bs = k_cache.shape[0]
    ts = q_s.shape[0] // bs
    assert bs % n_tiles == 0
    sb = bs // n_tiles
    cur_tile = lambda i: jnp.minimum(i, n_tiles - 1)
    prev_tile = lambda i: jnp.maximum(i - 1, 0)
    cur = lambda width: pl.BlockSpec((tm, width), lambda i: (cur_tile(i), 0))
    per_b = lambda *tail: pl.BlockSpec((1,) + tail, lambda i: (prev_tile(i) // tiles_per_batch,) + (0,) * len(tail))
    state_specs = [per_b(ML_HEADS, ML_DK, ML_DV), per_b(ML_HEADS, ML_DK), per_b(1, ML_HEADS),
                   per_b(GLA_HEADS, GLA_DK, GLA_DV)]
    qo_s = pl.BlockSpec((sb * ts, D_MODEL), lambda i: (cur_tile(i), 0))
    kv_s = pl.BlockSpec((sb, N_MEM * CACHE_ROW_GROUP, LANES), lambda i: (cur_tile(i), 0, 0))
    cast_specs = []
    for a in to_cast:
        rb = next(r for r in range(BF16_SUBLANES, a.shape[0] + 1, BF16_SUBLANES)
                  if a.shape[0] % r == 0 and a.shape[0] // r <= n_tiles)
        cast_specs.append(pl.BlockSpec((rb, a.shape[1]), lambda i, last=a.shape[0] // rb - 1: (jnp.minimum(i, last), 0)))
    x1, ym, c_new, n_new, m_new, s_new, o_s, *cast = pl.pallas_call(
        functools.partial(_ffn_mix_kernel, tiles_per_batch=tiles_per_batch, n_cast=len(to_cast)),
        grid=(n_tiles + 1,),
        in_specs=[cur(D_MODEL), _const_spec((1, D_MODEL)), _const_spec((D_MODEL, D_FF)),
                  _const_spec((D_MODEL, D_FF)), _const_spec((D_FF, D_MODEL)), _const_spec((1, D_MODEL)),
                  _const_spec((D_MODEL, ZQ_W)), _const_spec((D_MODEL, ZG_W)), _const_spec((1, SMALL_W)),
                  _const_spec((SMALL_W, GLA_QK_W)), _const_spec((1, GLA_QK_W)), _const_spec((1, ML_HEADS * ML_DV)),
                  _const_spec((1, GLA_V_W)), _const_spec((GLA_SUB, LANES, LANES)), qo_s, kv_s, kv_s] + cast_specs,
        out_specs=([cur(D_MODEL), pl.BlockSpec((tm, D_MODEL), lambda i: (prev_tile(i), 0))] + state_specs + [qo_s]
                   + cast_specs),
        out_shape=[jax.ShapeDtypeStruct((n_tiles * tm, D_MODEL), F32),
                   jax.ShapeDtypeStruct((n_tiles * tm, D_MODEL), BF16),
                   jax.ShapeDtypeStruct((bsz, ML_HEADS, ML_DK, ML_DV), F32),
                   jax.ShapeDtypeStruct((bsz, ML_HEADS, ML_DK), F32),
                   jax.ShapeDtypeStruct((bsz, 1, ML_HEADS), F32),
                   jax.ShapeDtypeStruct((bsz, GLA_HEADS, GLA_DK, GLA_DV), F32),
                   jax.ShapeDtypeStruct((bs * ts, D_MODEL), F32)] + [jax.ShapeDtypeStruct(a.shape, BF16) for a in to_cast],
        scratch_shapes=[pltpu.VMEM((tm, ZQ_W), BF16), pltpu.VMEM((tm, ZG_W), F32),
                        pltpu.VMEM((ML_HEADS, ML_DK, ML_DV), F32), pltpu.VMEM((SUBLANES, LANES), F32),
                        pltpu.VMEM((SUBLANES, LANES), F32), pltpu.VMEM((GLA_HEADS, GLA_DK, GLA_DV), F32)],
        compiler_params=_params(("arbitrary",)),
        name="ffn_mix",
    )(x.reshape(bsz * t, D_MODEL), w["ffn1_g"], w["ffn1_wg"], w["ffn1_wu"], w["ffn1_wd"], w["mix_g"], w["w_in_q"],
      w["w_in_g"], w["gate_bias"], w["w_a2"], w["b_a"], w["mlstm_out_g"], w["gla_out_g"], w["diag_sum"],
      q_s, _cache_rows_view(k_cache), _cache_rows_view(v_cache), *to_cast)
    return x1, ym, (c_new[None], n_new[None], m_new.reshape(1, bsz, ML_HEADS), s_new[None]), o_s, cast


def _post_mix_kernel(x1_ref, ym_ref, wout_ref, gx_ref, wq_ref, x2_ref, q_ref):
    x2 = x1_ref[...] + _dot(ym_ref[...].astype(BF16), wout_ref[...])
    x2_ref[...] = x2
    hq = _rms(x2, gx_ref[...]).astype(BF16)
    q_ref[...] = _dot(hq, wq_ref[...]).astype(q_ref.dtype)


def _post_mix(x1, ym, w, tm, q_dtype):
    n = x1.shape[0]
    row = lambda: pl.BlockSpec((tm, D_MODEL), lambda i: (i, 0))
    return pl.pallas_call(
        _post_mix_kernel,
        grid=(n // tm,),
        in_specs=[row(), row(), _const_spec((D_MODEL, D_MODEL)), _const_spec((1, D_MODEL)),
                  _const_spec((D_MODEL, D_MODEL))],
        out_specs=[row(), row()],
        out_shape=[jax.ShapeDtypeStruct((n, D_MODEL), F32), jax.ShapeDtypeStruct((n, D_MODEL), q_dtype)],
        compiler_params=_params(("arbitrary",)),
        name="post_mix",
    )(x1, ym, w["w_out"], w["xattn_g"], w["xattn_wq"])


def _softmax(s):
    e = jnp.exp(s - jnp.max(s, axis=-1, keepdims=True))
    return e / jnp.sum(e, axis=-1, keepdims=True)


def _post_fused_kernel(x1_ref, ym_ref, k_ref, v_ref, wout_ref, gx_ref, wq_ref, wo_ref, g2_ref, wg_ref, wu_ref,
                       wd_ref, gf_ref, y_ref):
    x2 = x1_ref[0] + _dot(ym_ref[0].astype(BF16), wout_ref[...])
    q = _dot(_rms(x2, gx_ref[...]).astype(BF16), wq_ref[...]).astype(BF16)
    heads = [slice(h * XA_DH, (h + 1) * XA_DH) for h in range(XA_HEADS)]
    k_full, v_full = _cache_rows_load(k_ref, 0), _cache_rows_load(v_ref, 0)
    scores = [_dot_nt(q[:, hs], k_full[:, hs]) * (XA_DH ** -0.5) for hs in heads]
    probs = [_softmax(s).astype(BF16) for s in scores]
    o = jnp.concatenate([_dot(p, v_full[:, hs]).astype(BF16) for hs, p in zip(heads, probs)], axis=1)
    x3 = x2 + _dot(o, wo_ref[...])
    x4 = _swiglu_residual(x3, g2_ref, wg_ref, wu_ref, wd_ref)
    y_ref[0] = _rms(x4, gf_ref[...])


def _post_fused(x1, ym, k, v, w, tm):
    bsz, t = x1.shape[0], x1.shape[1]
    row = lambda: pl.BlockSpec((1, tm, D_MODEL), lambda b, j: (b, j, 0))
    kv = pl.BlockSpec((1, N_MEM * CACHE_ROW_GROUP, LANES), lambda b, j: (b, 0, 0))
    sq = _const_spec((D_MODEL, D_MODEL))
    vec = _const_spec((1, D_MODEL))
    return pl.pallas_call(
        _post_fused_kernel,
        grid=(bsz, t // tm),
        in_specs=[row(), row(), kv, kv, sq, vec, sq, sq, vec, _const_spec((D_MODEL, D_FF)),
                  _const_spec((D_MODEL, D_FF)), _const_spec((D_FF, D_MODEL)), vec],
        out_specs=row(),
        out_shape=jax.ShapeDtypeStruct((bsz, t, D_MODEL), F32),
        compiler_params=_params(("arbitrary", "arbitrary")),
        name="post_fused",
    )(x1, ym, k, v, w["w_out"], w["xattn_g"], w["xattn_wq"], w["xattn_wo"], w["ffn2_g"], w["ffn2_wg"],
      w["ffn2_wu"], w["ffn2_wd"], w["final_g"])


XA_LANE_TILES = XA_DH // LANES
CACHE_ROW_GROUP = XA_LANE_TILES * XA_HEADS


def _cache_rows_view(x):
    bsz = x.shape[0]
    x = x.reshape(bsz, N_MEM, XA_HEADS, XA_LANE_TILES, LANES)
    return x.transpose(0, 1, 3, 2, 4).reshape(bsz, N_MEM * CACHE_ROW_GROUP, LANES)


def _cache_rows_unview(x):
    bsz = x.shape[0]
    x = x.reshape(bsz, N_MEM, XA_LANE_TILES, XA_HEADS, LANES)
    return x.transpose(0, 1, 3, 2, 4).reshape(bsz, N_MEM, XA_HEADS, XA_DH)


def _cache_rows_store(ref, b, x):
    for h in range(XA_HEADS):
        for lt in range(XA_LANE_TILES):
            lo = h * XA_DH + lt * LANES
            ref[b, pl.ds(lt * XA_HEADS + h, N_MEM, stride=CACHE_ROW_GROUP), :] = x[:, lo:lo + LANES]


def _cache_rows_load(ref, b):
    cols = [ref[b, pl.ds(lt * XA_HEADS + h, N_MEM, stride=CACHE_ROW_GROUP), :]
            for h in range(XA_HEADS) for lt in range(XA_LANE_TILES)]
    return jnp.concatenate(cols, axis=1).astype(BF16)


def _xattn_cache_stages(q_ref, k_ref, v_ref, o_ref):
    bb = k_ref.shape[0]
    tq = q_ref.shape[0] // bb
    lane_head = lax.broadcasted_iota(jnp.int32, (1, D_MODEL), 1) // XA_DH
    qs = [q_ref[b * tq:(b + 1) * tq, :] for b in range(bb)]
    q_bds = [jnp.concatenate([jnp.where(lane_head == h, q, 0.0) for h in range(XA_HEADS)], axis=0).astype(BF16)
             for q in qs]
    k_fulls = [_cache_rows_load(k_ref, b) for b in range(bb)]
    yield
    scores = [_dot_nt(q_bds[b], k_fulls[b]) * (XA_DH ** -0.5) for b in range(bb)]
    yield
    p_all = _softmax(jnp.concatenate(scores, axis=0)).astype(BF16)
    v_fulls = [_cache_rows_load(v_ref, b) for b in range(bb)]
    yield
    rows = XA_HEADS * tq
    o_fulls = [_dot(p_all[b * rows:(b + 1) * rows], v_fulls[b]) for b in range(bb)]
    yield
    for b in range(bb):
        o = jnp.zeros((tq, D_MODEL), F32)
        for h in range(XA_HEADS):
            o = jnp.where(lane_head == h, o_fulls[b][h * tq:(h + 1) * tq], o)
        o_ref[b * tq:(b + 1) * tq, :] = o.astype(o_ref.dtype)


def _ffn_out_kernel(x2_ref, o_ref, wo_ref, g2_ref, wg_ref, wu_ref, wd_ref, gf_ref, y_ref):
    x3 = x2_ref[...] + _dot(o_ref[...].astype(BF16), wo_ref[...])
    x4 = _swiglu_residual(x3, g2_ref, wg_ref, wu_ref, wd_ref)
    y_ref[...] = _rms(x4, gf_ref[...])


def _ffn_out(x2, o, w, tm):
    n = x2.shape[0]
    row = lambda: pl.BlockSpec((tm, D_MODEL), lambda i: (i, 0))
    return pl.pallas_call(
        _ffn_out_kernel,
        grid=(n // tm,),
        in_specs=[row(), row(), _const_spec((D_MODEL, D_MODEL)), _const_spec((1, D_MODEL)),
                  _const_spec((D_MODEL, D_FF)), _const_spec((D_MODEL, D_FF)), _const_spec((D_FF, D_MODEL)),
                  _const_spec((1, D_MODEL))],
        out_specs=row(),
        out_shape=jax.ShapeDtypeStruct((n, D_MODEL), F32),
        compiler_params=_params(("arbitrary",)),
        name="ffn_out",
    )(x2, o, w["xattn_wo"], w["ffn2_g"], w["ffn2_wg"], w["ffn2_wu"], w["ffn2_wd"], w["final_g"])


def _memkv_kernel(m_ref, g_ref, wk_ref, wv_ref, k_ref, v_ref):
    hn = _rms(m_ref[0], g_ref[...]).astype(BF16)
    _cache_rows_store(k_ref, 0, _dot(hn, wk_ref[...]))
    _cache_rows_store(v_ref, 0, _dot(hn, wv_ref[...]))


def _memkv(mem, w):
    bsz = mem.shape[0]
    rows = pl.BlockSpec((1, N_MEM * CACHE_ROW_GROUP, LANES), lambda b: (b, 0, 0))
    return pl.pallas_call(
        _memkv_kernel,
        grid=(bsz,),
        in_specs=[pl.BlockSpec((1, N_MEM, D_MODEL), lambda b: (b, 0, 0)), _const_spec((1, D_MODEL)),
                  _const_spec((D_MODEL, D_MODEL)), _const_spec((D_MODEL, D_MODEL))],
        out_specs=[rows, rows],
        out_shape=[jax.ShapeDtypeStruct((bsz, N_MEM * CACHE_ROW_GROUP, LANES), F32)] * 2,
        compiler_params=_params(("arbitrary",)),
        name="memkv",
    )(mem, w["mem_g"], w["xattn_wk"], w["xattn_wv"])


def _prep_weights(p):
    bf = lambda a: a.astype(BF16)
    row = lambda a: a.reshape(1, -1).astype(F32)
    w_in = p["w_in"]
    off = {}
    pos = 0
    for name, width in (("mq", 512), ("mk", 512), ("mv", 512), ("mi", 4), ("mf", 4), ("mo", 512), ("gq", 256),
                        ("gk", 256), ("gv", 512), ("ga", 16), ("gg", 512)):
        off[name] = w_in[:, pos:pos + width]
        pos += width
    pad_cols = SMALL_W - 2 * ML_HEADS - GLA_RANK
    w_in_q = jnp.concatenate([off[k] for k in ("mq", "mk", "mv", "gq", "gk", "gv")], axis=1)
    w_in_g = jnp.concatenate([off["mo"], off["gg"], off["mi"], off["mf"], off["ga"],
                              jnp.zeros((D_MODEL, pad_cols), F32)], axis=1)
    gate_bias = jnp.concatenate([p["mlstm_b_i"], p["mlstm_b_f"], jnp.zeros((SMALL_W - 2 * ML_HEADS,), F32)])
    w_a2 = jnp.zeros((SMALL_W, GLA_QK_W), F32).at[2 * ML_HEADS:2 * ML_HEADS + GLA_RANK].set(p["gla_w_a2"])
    lane = jnp.arange(LANES)
    same_head = (lane[:, None] // GLA_DK) == (lane[None, :] // GLA_DK)
    diag_sum = jnp.stack([same_head & ((lane[None, :] % GLA_SUB) == j) for j in range(GLA_SUB)]).astype(BF16)
    return dict(
        ffn1_g=row(p["ffn1_norm_g"]), ffn1_wg=bf(p["ffn1_w_gate"]), ffn1_wu=bf(p["ffn1_w_up"]),
        ffn1_wd=bf(p["ffn1_w_down"]), mix_g=row(p["mix_norm_g"]), w_in_q=bf(w_in_q), w_in_g=bf(w_in_g),
        gate_bias=row(gate_bias), w_a2=bf(w_a2), b_a=row(p["gla_b_a"]), mlstm_out_g=row(p["mlstm_out_g"]),
        gla_out_g=row(p["gla_out_g"]), diag_sum=diag_sum,
        w_out=bf(p["w_out"]), xattn_g=row(p["xattn_norm_g"]), xattn_wq=bf(p["xattn_w_q"]),
        mem_g=row(p["mem_norm_g"]), ffn2_g=row(p["ffn2_norm_g"]), final_g=row(p["final_g"]),
        xattn_wo=p["xattn_w_o"], xattn_wk=p["xattn_w_k"], xattn_wv=p["xattn_w_v"], ffn2_wg=p["ffn2_w_gate"],
        ffn2_wu=p["ffn2_w_up"], ffn2_wd=p["ffn2_w_down"])


LATE_WEIGHTS = ("xattn_wo", "xattn_wk", "xattn_wv", "ffn2_wg", "ffn2_wu", "ffn2_wd")


def kernel(x_prompt, x_sample, mem_prompt, cache_mem_k, cache_mem_v, state_mlstm_c, state_mlstm_n, state_mlstm_m, state_gla_s, ffn1_norm_g, ffn1_w_gate, ffn1_w_up, ffn1_w_down, mix_norm_g, w_in, mlstm_b_i, mlstm_b_f, mlstm_out_g, gla_w_a2, gla_b_a, gla_out_g, w_out, xattn_norm_g, mem_norm_g, xattn_w_q, xattn_w_k, xattn_w_v, xattn_w_o, ffn2_norm_g, ffn2_w_gate, ffn2_w_up, ffn2_w_down, final_norm_g):
    assert ffn1_norm_g.shape[0] == 1, "single-layer stack"
    layer = dict(ffn1_norm_g=ffn1_norm_g, ffn1_w_gate=ffn1_w_gate, ffn1_w_up=ffn1_w_up, ffn1_w_down=ffn1_w_down,
                 mix_norm_g=mix_norm_g, w_in=w_in, mlstm_b_i=mlstm_b_i, mlstm_b_f=mlstm_b_f,
                 mlstm_out_g=mlstm_out_g, gla_w_a2=gla_w_a2, gla_b_a=gla_b_a, gla_out_g=gla_out_g, w_out=w_out,
                 xattn_norm_g=xattn_norm_g, mem_norm_g=mem_norm_g, xattn_w_q=xattn_w_q, xattn_w_k=xattn_w_k,
                 xattn_w_v=xattn_w_v, xattn_w_o=xattn_w_o, ffn2_norm_g=ffn2_norm_g, ffn2_w_gate=ffn2_w_gate,
                 ffn2_w_up=ffn2_w_up, ffn2_w_down=ffn2_w_down)
    p = {name: arr[0] for name, arr in layer.items()}
    p["final_g"] = final_norm_g
    w = _prep_weights(p)

    bp, tp, _ = x_prompt.shape
    bs, ts, _ = x_sample.shape

    state = (state_mlstm_c[0], state_mlstm_n[0], state_mlstm_m[0].reshape(bs, 1, ML_HEADS), state_gla_s[0])
    x1_s, zq_s, zg_s = _ffn_in(x_sample.reshape(bs * ts, D_MODEL), w, TM_FFN_IN, F32)
    ym_s, c_s, n_s, m_s, s_s = _mixer_state(zq_s, zg_s, w, state, ts, SAMPLE_MIXER_BATCHES)
    states_s = (c_s[None], n_s[None], m_s.reshape(1, bs, ML_HEADS), s_s[None])
    x2_s, q_s = _post_mix(x1_s, ym_s, w, TM_FFN_IN, F32)

    x1_p, ym_p, states_p, o_s, late = _ffn_mix(x_prompt, w, TM_FFN_IN, q_s, cache_mem_k[0], cache_mem_v[0],
                                               [w[name] for name in LATE_WEIGHTS])
    w.update(zip(LATE_WEIGHTS, late))
    mem_k_p, mem_v_p = _memkv(mem_prompt, w)
    y_p = _post_fused(x1_p.reshape(bp, tp, D_MODEL), ym_p.reshape(bp, tp, D_MODEL), mem_k_p, mem_v_p, w, TM_POST)
    y_s = _ffn_out(x2_s, o_s, w, TM_FFN_IN).reshape(bs, ts, D_MODEL)

    return (y_p, y_s, _cache_rows_unview(mem_k_p)[None], _cache_rows_unview(mem_v_p)[None]) + states_p + states_s
```

```python
import functools

import jax
import jax.numpy as jnp
from jax import lax
from jax.experimental import pallas as pl
from jax.experimental.pallas import tpu as pltpu

F32 = jnp.float32
BF16 = jnp.bfloat16

D_MODEL = 1024
D_FF = 2816
ML_HEADS = 4
ML_DK = 128
ML_DV = 128
GLA_HEADS = 4
GLA_DK = 64
GLA_DV = 128
GLA_RANK = 16
GLA_TAU = 16.0
N_MEM = 256
XA_HEADS = 4
XA_DH = D_MODEL // XA_HEADS
EPS = 1e-6
CHUNK = 64
LOG2_E = 1.4426950408889634
LN_2 = 0.6931471805599453
LANES = 128
SUBLANES = 8
BF16_SUBLANES = 2 * SUBLANES

ZQ_MQ, ZQ_MK, ZQ_MV = 0, 512, 1024
ZQ_GQ, ZQ_GK, ZQ_GV = 1536, 1792, 2048
ZQ_W = 2560
ZG_MO, ZG_GG, ZG_SMALL = 0, 512, 1024
ZG_W = 1152
SMALL_W = LANES
GLA_QK_W = GLA_HEADS * GLA_DK
GLA_V_W = GLA_HEADS * GLA_DV
GLA_PAIRS = GLA_HEADS // 2
GLA_SUB = SUBLANES

MXU_WIDTH = 256
FF_CHUNKS = ((0, 6 * MXU_WIDTH), (6 * MXU_WIDTH, D_FF))
VMEM_LIMIT_BYTES = 56 * 1024 * 1024

TM_FFN_IN = MXU_WIDTH
TM_POST = 2 * MXU_WIDTH
SAMPLE_MIXER_BATCHES = 16


def _rms(x, g):
    return x * lax.rsqrt(jnp.mean(x * x, axis=-1, keepdims=True) + EPS) * g


def _log_sigmoid(x):
    return jnp.minimum(x, 0.0) - jnp.log1p(jnp.exp(-jnp.abs(x)))


def _dot(a, b):
    return jnp.dot(a, b, preferred_element_type=F32)


def _dot_nt(a, b):
    return lax.dot_general(a, b, (((1,), (1,)), ((), ())), preferred_element_type=F32)


def _dot_f32(a, b):
    return jnp.dot(a, b, precision=lax.Precision.HIGHEST, preferred_element_type=F32)


def _swiglu_residual(x, g_ref, wg_ref, wu_ref, wd_ref):
    h = _rms(x, g_ref[...]).astype(BF16)
    acts = []
    for lo, hi in FF_CHUNKS:
        g = _dot(h, wg_ref[:, lo:hi])
        u = _dot(h, wu_ref[:, lo:hi])
        acts.append(((g * jax.nn.sigmoid(g)) * u).astype(BF16))
    acc = jnp.zeros_like(x)
    for (lo, hi), a in zip(FF_CHUNKS, acts):
        acc = acc + _dot(a, wd_ref[lo:hi, :])
    return x + 0.5 * acc


def _const_spec(shape):
    nd = len(shape)
    return pl.BlockSpec(shape, lambda *_: (0,) * nd, pipeline_mode=pl.Buffered(1))


def _params(sem):
    return pltpu.CompilerParams(dimension_semantics=sem, vmem_limit_bytes=VMEM_LIMIT_BYTES)


def _ffn_in_kernel(x_ref, g1_ref, wg_ref, wu_ref, wd_ref, gm_ref, wq_ref, wgt_ref, x1_ref, zq_ref, zg_ref):
    _drain(_ffn_in_pieces(x_ref[...], g1_ref, wg_ref, wu_ref, wd_ref, gm_ref, wq_ref, wgt_ref, x1_ref, zq_ref, zg_ref))


def _ffn_in(x, w, tm, zq_dtype):
    n = x.shape[0]
    row = lambda width: pl.BlockSpec((tm, width), lambda i: (i, 0))
    return pl.pallas_call(
        _ffn_in_kernel,
        grid=(n // tm,),
        in_specs=[row(D_MODEL), _const_spec((1, D_MODEL)), _const_spec((D_MODEL, D_FF)),
                  _const_spec((D_MODEL, D_FF)), _const_spec((D_FF, D_MODEL)), _const_spec((1, D_MODEL)),
                  _const_spec((D_MODEL, ZQ_W)), _const_spec((D_MODEL, ZG_W))],
        out_specs=[row(D_MODEL), row(ZQ_W), row(ZG_W)],
        out_shape=[jax.ShapeDtypeStruct((n, D_MODEL), F32), jax.ShapeDtypeStruct((n, ZQ_W), zq_dtype),
                   jax.ShapeDtypeStruct((n, ZG_W), F32)],
        compiler_params=_params(("arbitrary",)),
        name="ffn_in",
    )(x, w["ffn1_g"], w["ffn1_wg"], w["ffn1_wu"], w["ffn1_wd"], w["mix_g"], w["w_in_q"], w["w_in_g"])


def _mixer_stages(zqs, zgs, states, consts, ee_ref, *, L, t_real, chain):
    bias, wa2, ba, gml, ggl = consts
    nchunk = len(zqs)
    groups = [(b, h) for b in range(nchunk) for h in range(ML_HEADS)]
    pairs = [(b, p) for b in range(nchunk) for p in range(GLA_PAIRS)]
    padded = t_real < L
    valid = lax.broadcasted_iota(jnp.int32, (L, 1), 0) < t_real
    rr = lax.broadcasted_iota(jnp.int32, (L, L), 0)
    cc = lax.broadcasted_iota(jnp.int32, (L, L), 1)
    tril = cc <= rr
    c = GLA_SUB
    nb = L // c
    lane = lax.broadcasted_iota(jnp.int32, (1, LANES), 1)
    lane_blk = (lane % GLA_DK) // c
    lane_head = lane // GLA_DK
    v_lane_head = lax.broadcasted_iota(jnp.int32, (1, 2 * GLA_DV), 1) // GLA_DV
    row_blk = (lax.broadcasted_iota(jnp.int32, (2 * L, 1), 0) % L) // c
    t_in = lax.broadcasted_iota(jnp.int32, (1, c, 1), 1)
    new_states = [dict(c=[None] * ML_HEADS, n=[None] * ML_HEADS, m=[None] * ML_HEADS, s=[None] * GLA_HEADS)
                  for _ in range(nchunk)]
    prev = lambda b: new_states[b - 1] if chain and b > 0 else states[0 if chain else b]
    known = lambda b: not chain or b == 0

    carried = {}

    def carried_mlstm(b, h):
        st = prev(b)
        d = ml[(b, h)]
        carried[(b, h)] = _dot(d["qb"], st["c"][h].astype(BF16))

    def carried_gla(b, p):
        s_prev = prev(b)["s"]
        zero_blk = jnp.zeros((GLA_DK, GLA_DV), BF16)
        s_bd = jnp.concatenate(
            [jnp.concatenate([s_prev[2 * p].astype(BF16), zero_blk], axis=1),
             jnp.concatenate([zero_blk, s_prev[2 * p + 1].astype(BF16)], axis=1)], axis=0)
        carried[(b, "gla", p)] = _dot(gl[(b, p)]["q_dec"], s_bd)

    smalls, sms, lfs, b_cols, b_rows, sm_ts, las, bcs = [], [], [], [], [], [], [], []
    for b in range(nchunk):
        small = zgs[b][:, ZG_SMALL:ZG_SMALL + SMALL_W]
        sm = small + bias
        lf = _log_sigmoid(sm) * LOG2_E
        sm = sm * LOG2_E
        if padded:
            sm = jnp.where(valid, sm, -jnp.inf)
            lf = jnp.where(valid, lf, 0.0)
        smalls.append(small)
        sms.append(sm)
        lfs.append(lf)

    long_chunk = L > SUBLANES
    col_w = 1 if long_chunk else LANES
    tril_f, triu_f = tril.astype(F32), (rr <= cc).astype(F32)

    def cumsum_rows(x):
        if long_chunk:
            return _dot_f32(tril_f, x)
        acc = x[0:1]
        rows = [acc]
        for r in range(1, L):
            acc = acc + x[r:r + 1]
            rows.append(acc)
        return jnp.concatenate(rows, axis=0)

    lf_ts = [lf.T[0:SUBLANES] for lf in lfs] if long_chunk else None
    yield
    for b in range(nchunk):
        b_cols.append(cumsum_rows(lfs[b]))
        b_rows.append(_dot_f32(lf_ts[b], triu_f) if long_chunk else b_cols[b].T[0:SUBLANES])
        sm_ts.append(sms[b].T)
        la = _log_sigmoid(_dot(smalls[b].astype(BF16), wa2) + ba) * (LOG2_E / GLA_TAU)
        las.append(jnp.where(valid, la, 0.0) if padded else la)

    ml = {}
    for g in groups:
        b, h = g
        zq = zqs[b]
        qf = zq[:, ZQ_MQ + h * ML_DK:ZQ_MQ + (h + 1) * ML_DK].astype(F32)
        kf = zq[:, ZQ_MK + h * ML_DK:ZQ_MK + (h + 1) * ML_DK].astype(F32) * (ML_DK ** -0.5)
        vf = zq[:, ZQ_MV + h * ML_DV:ZQ_MV + (h + 1) * ML_DV].astype(F32)
        if padded:
            kf = jnp.where(valid, kf, 0.0)
            vf = jnp.where(valid, vf, 0.0)
        ml[g] = dict(qf=qf, qb=qf.astype(BF16), kf=kf, kb=kf.astype(BF16), vb=vf.astype(BF16))
    yield
    for b in range(nchunk):
        bcs.append(cumsum_rows(las[b]))
    for g in groups:
        d = ml[g]
        d["qk"] = _dot_nt(d["qb"], d["kb"])
        if known(g[0]):
            carried_mlstm(*g)

    yield
    for g in groups:
        b, h = g
        d = ml[g]
        i_col = jnp.broadcast_to(sms[b][:, h:h + 1], (L, col_w))
        b_col = jnp.broadcast_to(b_cols[b][:, ML_HEADS + h:ML_HEADS + h + 1], (L, col_w))
        b_row = b_rows[b][ML_HEADS + h:ML_HEADS + h + 1, :]
        i_row = sm_ts[b][h:h + 1, :]
        a_col = b_col + prev(b)["m"][h] * LOG2_E
        dm = jnp.where(tril, b_col[:, :L] - (b_row - i_row), -jnp.inf)
        mt = jnp.maximum(a_col, jnp.max(dm, axis=1, keepdims=True))
        w_inter = jnp.exp2(a_col - mt)
        s = d["qk"] * jnp.exp2(dm - mt[:, :L])
        kw = d["kf"] * jnp.exp2((b_col[L - 1:L] - mt[L - 1:L]) - (b_col - i_col))
        d.update(mt=mt, w_inter=w_inter, s=s, kw=kw, kw_t=kw.T.astype(BF16))
        new_states[b]["m"][h] = mt[L - 1:L, 0:1] * LN_2

    gl = {}
    for b in range(nchunk):
        zq = zqs[b]
        gq = zq[:, ZQ_GQ:ZQ_GQ + GLA_QK_W].astype(F32) * (GLA_DK ** -0.5)
        gk = zq[:, ZQ_GK:ZQ_GK + GLA_QK_W].astype(F32)
        gv = zq[:, ZQ_GV:ZQ_GV + GLA_V_W].astype(F32)
        if padded:
            gk = jnp.where(valid, gk, 0.0)
            gv = jnp.where(valid, gv, 0.0)
        stack = lambda x: jnp.concatenate([x[:, :LANES], x[:, LANES:]], axis=0)
        q2, k2, b2 = stack(gq), stack(gk), stack(bcs[b])
        q3 = q2.reshape(2 * nb, c, LANES)
        k3 = k2.reshape(2 * nb, c, LANES)
        b3 = b2.reshape(2 * nb, c, LANES)
        pair_terms = []
        for j in range(min(c, t_real)):
            decay = jnp.exp2(jnp.where(t_in >= j, b3 - b3[:, j:j + 1, :], -jnp.inf))
            pair_terms.append((q3 * k3[:, j:j + 1, :] * decay).reshape(2 * L, LANES).astype(BF16))
        kt2 = (k3 * jnp.exp2(b3[:, c - 1:c, :] - b3)).reshape(2 * L, LANES) if nb > 1 else None
        gl[b] = dict(gv=gv, q2=q2, k2=k2, b2=b2, pair_terms=pair_terms, kt2=kt2)
    yield
    for b in range(nchunk):
        acc = jnp.zeros((2 * L, LANES), F32)
        for j, pair_j in enumerate(gl[b]["pair_terms"]):
            acc = acc + _dot(pair_j, ee_ref[j])
        gl[b]["a_diag"] = jnp.where(lane_blk == row_blk, acc, 0.0)

    for bp in pairs:
        b, p = bp
        d = gl[b]
        rows_p = slice(p * L, (p + 1) * L)
        q_p, k_p, b_p = d["q2"][rows_p], d["k2"][rows_p], d["b2"][rows_p]
        if long_chunk:
            decay_col = jnp.exp2(b_p[L - SUBLANES:L].T[:, SUBLANES - 1:SUBLANES])
        else:
            decay_col = jnp.exp2(jnp.broadcast_to(b_p[L - 1:L], (LANES, LANES)).T)
        e = dict(q_dec=(q_p * jnp.exp2(b_p)).astype(BF16),
                 kh_t=(k_p * jnp.exp2(b_p[L - 1:L] - b_p)).T.astype(BF16),
                 decay_col=decay_col)
        if nb > 1:
            kt_p = d["kt2"][rows_p].astype(BF16)
            k_bd = jnp.concatenate([jnp.where(lane_head == hh, kt_p, 0.0) for hh in range(2)], axis=0)
            slabs, offs = [], []
            off = 0
            for j in range(nb - 1):
                lo = (j + 1) * c
                slabs.append(q_p[lo:] * jnp.exp2(b_p[lo:] - b_p[lo - 1:lo]))
                offs.append(off)
                off += L - lo
            e.update(q_var=jnp.concatenate(slabs, axis=0).astype(BF16), k_bd=k_bd, offs=offs)
        gl[bp] = e
    yield
    for bp in pairs:
        e = gl[bp]
        if nb > 1:
            e["r"] = _dot_nt(e["q_var"], e["k_bd"])
        if known(bp[0]):
            carried_gla(*bp)

    for g in groups:
        d = ml[g]
        d["sv"] = _dot(d["s"].astype(BF16), d["vb"])
        d["c_upd"] = _dot(d["kw_t"], d["vb"])
        d["den"] = jnp.sum(d["s"], axis=1, keepdims=True)
        if known(g[0]):
            d["qn"] = jnp.sum(d["qf"] * prev(g[0])["n"][g[1]], axis=1, keepdims=True)
    yield
    for bp in pairs:
        b, p = bp
        d, e = gl[b], gl[bp]
        a_p = d["a_diag"][p * L:(p + 1) * L]
        if nb > 1:
            blocks = []
            for i in range(nb):
                blk = a_p[i * c:(i + 1) * c]
                for j in range(i):
                    lo_r = e["offs"][j] + (i - j - 1) * c
                    blk = jnp.where(lane_blk == j, e["r"][lo_r:lo_r + c], blk)
                blocks.append(blk)
            a_p = jnp.concatenate(blocks, axis=0)
        v_f = d["gv"][:, p * 2 * GLA_DV:(p + 1) * 2 * GLA_DV]
        v_p = v_f.astype(BF16)
        if L < GLA_DK:
            v_rows = []
            for hh in range(2):
                v_rows += [jnp.where(v_lane_head == hh, v_f, 0.0), jnp.zeros((GLA_DK - L, 2 * GLA_DV), F32)]
            v_bd = jnp.concatenate(v_rows, axis=0).astype(BF16)
        else:
            v_bd = jnp.concatenate([jnp.where(v_lane_head == hh, v_p, 0.0) for hh in range(2)], axis=0)
        e["s_upd"] = [_dot(e["kh_t"][hh * GLA_DK:(hh + 1) * GLA_DK], v_p[:, hh * GLA_DV:(hh + 1) * GLA_DV])
                      for hh in range(2)]
        e.update(scores=a_p.astype(BF16), v_bd=v_bd)
    yield
    for bp in pairs:
        e = gl[bp]
        e["o_intra"] = _dot(e["scores"], e["v_bd"])

    yield
    ys = [[None] * (ML_HEADS + GLA_HEADS) for _ in range(nchunk)]

    def finish_mlstm(b):
        st = prev(b)
        for h in range(ML_HEADS):
            d = ml[(b, h)]
            mt, w_inter = d["mt"], d["w_inter"]
            qn = d["qn"] if known(b) else jnp.sum(d["qf"] * st["n"][h], axis=1, keepdims=True)
            den = d["den"] + w_inter * qn
            hh = (d["sv"] + w_inter * carried[(b, h)]) / jnp.maximum(jnp.abs(den), jnp.exp2(-mt))
            i_last = w_inter[L - 1:L]
            new_states[b]["c"][h] = i_last * st["c"][h] + d["c_upd"]
            new_states[b]["n"][h] = i_last * st["n"][h] + jnp.sum(d["kw"], axis=0, keepdims=True)
            hs = slice(h * ML_DV, (h + 1) * ML_DV)
            yn = hh * lax.rsqrt(jnp.mean(hh * hh, axis=-1, keepdims=True) + EPS) * gml[:, hs]
            ys[b][h] = jax.nn.sigmoid(zgs[b][:, ZG_MO + h * ML_DV:ZG_MO + (h + 1) * ML_DV]) * yn

    def finish_gla(b):
        st = prev(b)
        for p in range(GLA_PAIRS):
            e = gl[(b, p)]
            o = carried[(b, "gla", p)] + e["o_intra"]
            for hh in range(2):
                h = 2 * p + hh
                ds = slice(hh * GLA_DK, (hh + 1) * GLA_DK)
                vs = slice(hh * GLA_DV, (hh + 1) * GLA_DV)
                new_states[b]["s"][h] = e["decay_col"][ds] * st["s"][h] + e["s_upd"][hh]
                oh = o[:, vs]
                yn = oh * lax.rsqrt(jnp.mean(oh * oh, axis=-1, keepdims=True) + EPS) * ggl[:, h * GLA_DV:(h + 1) * GLA_DV]
                gg = zgs[b][:, ZG_GG + h * GLA_DV:ZG_GG + (h + 1) * GLA_DV]
                ys[b][ML_HEADS + h] = (gg * jax.nn.sigmoid(gg)) * yn

    if chain:
        for b in range(nchunk):
            if not known(b):
                for h in range(ML_HEADS):
                    carried_mlstm(b, h)
                for p in range(GLA_PAIRS):
                    carried_gla(b, p)
            finish_mlstm(b)
            finish_gla(b)
    else:
        for b in range(nchunk):
            finish_mlstm(b)
        for b in range(nchunk):
            finish_gla(b)
    return [jnp.concatenate(y, axis=1) for y in ys], new_states


def _drain(gen):
    try:
        while True:
            next(gen)
    except StopIteration as done:
        return done.value


def _mixer_compute(*args, **kwargs):
    return _drain(_mixer_stages(*args, **kwargs))


def _mixer_state_kernel(zq_ref, zg_ref, bias_ref, wa2_ref, ba_ref, gml_ref, ggl_ref, ee_ref, c0_ref, n0_ref, m0_ref,
                        s0_ref, y_ref, co_ref, no_ref, mo_ref, so_ref, *, L, t_real, nblk):
    consts = (bias_ref[...], wa2_ref[...], ba_ref[...], gml_ref[...], ggl_ref[...])
    lane = lax.broadcasted_iota(jnp.int32, (1, LANES), 1)

    def chunk_rows(ref, b):
        rows = ref[b * t_real:(b + 1) * t_real, :]
        return jnp.concatenate([rows, jnp.zeros((L - t_real, rows.shape[1]), rows.dtype)], axis=0)

    states = []
    for b in range(nblk):
        n_all = n0_ref[b]
        m_all = m0_ref[b]
        states.append(dict(c=[c0_ref[b, h] for h in range(ML_HEADS)],
                           n=[n_all[h:h + 1, :] for h in range(ML_HEADS)],
                           m=[m_all[:, h:h + 1] for h in range(ML_HEADS)],
                           s=[s0_ref[b, h] for h in range(GLA_HEADS)]))
    ys, new_states = _mixer_compute([chunk_rows(zq_ref, b) for b in range(nblk)],
                                    [chunk_rows(zg_ref, b) for b in range(nblk)], states, consts, ee_ref, L=L,
                                    t_real=t_real, chain=False)
    for b in range(nblk):
        st = new_states[b]
        y_ref[b * t_real:(b + 1) * t_real, :] = ys[b][0:t_real].astype(y_ref.dtype)
        for h in range(ML_HEADS):
            co_ref[b, h] = st["c"][h]
            so_ref[b, h] = st["s"][h]
        no_ref[b] = jnp.concatenate(st["n"], axis=0)
        m_row = jnp.zeros((1, LANES), F32)
        for h in range(ML_HEADS):
            m_row = jnp.where(lane == h, st["m"][h], m_row)
        mo_ref[b] = m_row[:, 0:ML_HEADS]


def _mixer_state(zq, zg, w, state, t_real, nblk):
    bsz = zq.shape[0] // t_real
    L = -(-t_real // SUBLANES) * SUBLANES
    rows = lambda width: pl.BlockSpec((nblk * t_real, width), lambda b: (b, 0))
    per_b = lambda *tail: pl.BlockSpec((nblk,) + tail, lambda b: (b,) + (0,) * len(tail))
    state_specs = [per_b(ML_HEADS, ML_DK, ML_DV), per_b(ML_HEADS, ML_DK), per_b(1, ML_HEADS),
                   per_b(GLA_HEADS, GLA_DK, GLA_DV)]
    consts = [_const_spec((1, SMALL_W)), _const_spec((SMALL_W, GLA_QK_W)), _const_spec((1, GLA_QK_W)),
              _const_spec((1, ML_HEADS * ML_DV)), _const_spec((1, GLA_V_W)), _const_spec((GLA_SUB, LANES, LANES))]
    return pl.pallas_call(
        functools.partial(_mixer_state_kernel, L=L, t_real=t_real, nblk=nblk),
        grid=(bsz // nblk,),
        in_specs=[rows(ZQ_W), rows(ZG_W)] + consts + state_specs,
        out_specs=[rows(D_MODEL)] + state_specs,
        out_shape=[jax.ShapeDtypeStruct((bsz * t_real, D_MODEL), F32),
                   jax.ShapeDtypeStruct((bsz, ML_HEADS, ML_DK, ML_DV), F32),
                   jax.ShapeDtypeStruct((bsz, ML_HEADS, ML_DK), F32),
                   jax.ShapeDtypeStruct((bsz, 1, ML_HEADS), F32),
                   jax.ShapeDtypeStruct((bsz, GLA_HEADS, GLA_DK, GLA_DV), F32)],
        compiler_params=_params(("arbitrary",)),
        name="mixer_state",
    )(zq, zg, w["gate_bias"], w["w_a2"], w["b_a"], w["mlstm_out_g"], w["gla_out_g"], w["diag_sum"], *state)


def _ffn_in_pieces(x, g1_ref, wg_ref, wu_ref, wd_ref, gm_ref, wq_ref, wgt_ref, x1_ref, zq_out, zg_out):
    h = _rms(x, g1_ref[...]).astype(BF16)
    acts = []
    for lo in range(0, D_FF, MXU_WIDTH):
        g = _dot(h, wg_ref[:, lo:lo + MXU_WIDTH])
        yield
        u = _dot(h, wu_ref[:, lo:lo + MXU_WIDTH])
        yield
        acts.append(((g * jax.nn.sigmoid(g)) * u).astype(BF16))
    a = jnp.concatenate(acts, axis=1)
    down = []
    for lo in range(0, D_MODEL, MXU_WIDTH):
        down.append(_dot(a, wd_ref[:, lo:lo + MXU_WIDTH]))
        yield
    x1 = x + 0.5 * jnp.concatenate(down, axis=1)
    x1_ref[...] = x1
    hm = _rms(x1, gm_ref[...]).astype(BF16)
    for lo in range(0, ZQ_W, MXU_WIDTH):
        zq_out[:, lo:lo + MXU_WIDTH] = _dot(hm, wq_ref[:, lo:lo + MXU_WIDTH]).astype(zq_out.dtype)
        yield
    for lo in range(0, ZG_W, MXU_WIDTH):
        hi = min(lo + MXU_WIDTH, ZG_W)
        zg_out[:, lo:hi] = _dot(hm, wgt_ref[:, lo:hi])
        yield


ATTN_STAGE_EVERY = 7
FFN_PIECES_PER_MIXER_STAGE = 1


def _ffn_mix_kernel(*refs, tiles_per_batch, n_cast):
    it = iter(refs)
    (x_ref, g1_ref, wg_ref, wu_ref, wd_ref, gm_ref, wq_ref, wgt_ref, bias_ref, wa2_ref, ba_ref, gml_ref, ggl_ref,
     ee_ref, qs_ref, ks_ref, vs_ref) = (next(it) for _ in range(17))
    cast_src = [next(it) for _ in range(n_cast)]
    x1_ref, ym_ref, co_ref, no_ref, mo_ref, so_ref, os_ref = (next(it) for _ in range(7))
    cast_dst = [next(it) for _ in range(n_cast)]
    zq_s, zg_s, c_s, n_s, m_s, s_s = (next(it) for _ in range(6))
    i = pl.program_id(0)
    for src, dst in zip(cast_src, cast_dst):
        dst[...] = src[...].astype(dst.dtype)

    @pl.when(i == 0)
    def _init():
        for ref in (zq_s, zg_s, c_s, n_s, m_s, s_s):
            ref[...] = jnp.zeros_like(ref)

    consts = (bias_ref[...], wa2_ref[...], ba_ref[...], gml_ref[...], ggl_ref[...])
    n_chunks = zq_s.shape[0] // CHUNK
    rows = [slice(k * CHUNK, (k + 1) * CHUNK) for k in range(n_chunks)]
    zqs = [zq_s[r, :] for r in rows]
    zgs = [zg_s[r, :] for r in rows]
    starts_batch = (i - 1) % tiles_per_batch == 0
    carry = lambda v: jnp.where(starts_batch, 0.0, v)
    state = dict(c=[carry(c_s[h]) for h in range(ML_HEADS)], n=[carry(n_s[h:h + 1, :]) for h in range(ML_HEADS)],
                 m=[carry(m_s[h:h + 1, 0:1]) for h in range(ML_HEADS)], s=[carry(s_s[h]) for h in range(GLA_HEADS)])

    def mixers():
        st = state
        for k in range(n_chunks):
            ys, new = yield from _mixer_stages([zqs[k]], [zgs[k]], [st], consts, ee_ref, L=CHUNK, t_real=CHUNK,
                                               chain=True)
            ym_ref[rows[k], :] = ys[0].astype(ym_ref.dtype)
            st = new[0]
            yield
        return st

    ffn = _ffn_in_pieces(x_ref[...], g1_ref, wg_ref, wu_ref, wd_ref, gm_ref, wq_ref, wgt_ref, x1_ref, zq_s, zg_s)
    mix = mixers()
    attn = _xattn_cache_stages(qs_ref, ks_ref, vs_ref, os_ref)
    live = {"ffn": True, "attn": True}

    def advance(name, gen):
        if live[name]:
            try:
                next(gen)
            except StopIteration:
                live[name] = False

    slot = 0
    while True:
        try:
            next(mix)
        except StopIteration as done:
            last = done.value
            break
        for _ in range(FFN_PIECES_PER_MIXER_STAGE):
            advance("ffn", ffn)
        if slot % ATTN_STAGE_EVERY == ATTN_STAGE_EVERY // 2:
            advance("attn", attn)
        slot += 1
    for name, gen in (("ffn", ffn), ("attn", attn)):
        if live[name]:
            _drain(gen)

    for h in range(ML_HEADS):
        c_s[h] = last["c"][h]
        n_s[h:h + 1, :] = last["n"][h]
        m_s[h:h + 1, :] = jnp.broadcast_to(last["m"][h], (1, LANES))
        s_s[h] = last["s"][h]

    @pl.when(jnp.logical_and(i >= 1, (i - 1) % tiles_per_batch == tiles_per_batch - 1))
    def _emit_state():
        lane = lax.broadcasted_iota(jnp.int32, (1, LANES), 1)
        for h in range(ML_HEADS):
            co_ref[0, h] = last["c"][h]
            so_ref[0, h] = last["s"][h]
        no_ref[0] = jnp.concatenate(last["n"], axis=0)
        m_row = jnp.zeros((1, LANES), F32)
        for h in range(ML_HEADS):
            m_row = jnp.where(lane == h, last["m"][h], m_row)
        mo_ref[0] = m_row[:, 0:ML_HEADS]


def _ffn_mix(x, w, tm, q_s, k_cache, v_cache, to_cast):
    bsz, t, _ = x.shape
    tiles_per_batch = t // tm
    n_tiles = bsz * tiles_per_batch
    assert t % tm == 0 and tm % CHUNK == 0
    bs = k_cache.shape[0]
    ts = q_s.shape[0] // bs
    assert bs % n_tiles == 0
    sb = bs // n_tiles
    cur_tile = lambda i: jnp.minimum(i, n_tiles - 1)
    prev_tile = lambda i: jnp.maximum(i - 1, 0)
    cur = lambda width: pl.BlockSpec((tm, width), lambda i: (cur_tile(i), 0))
    per_b = lambda *tail: pl.BlockSpec((1,) + tail, lambda i: (prev_tile(i) // tiles_per_batch,) + (0,) * len(tail))
    state_specs = [per_b(ML_HEADS, ML_DK, ML_DV), per_b(ML_HEADS, ML_DK), per_b(1, ML_HEADS),
                   per_b(GLA_HEADS, GLA_DK, GLA_DV)]
    qo_s = pl.BlockSpec((sb * ts, D_MODEL), lambda i: (cur_tile(i), 0))
    kv_s = pl.BlockSpec((sb, N_MEM * CACHE_ROW_GROUP, LANES), lambda i: (cur_tile(i), 0, 0))
    cast_specs = []
    for a in to_cast:
        rb = next(r for r in range(BF16_SUBLANES, a.shape[0] + 1, BF16_SUBLANES)
                  if a.shape[0] % r == 0 and a.shape[0] // r <= n_tiles)
        cast_specs.append(pl.BlockSpec((rb, a.shape[1]), lambda i, last=a.shape[0] // rb - 1: (jnp.minimum(i, last), 0)))
    x1, ym, c_new, n_new, m_new, s_new, o_s, *cast = pl.pallas_call(
        functools.partial(_ffn_mix_kernel, tiles_per_batch=tiles_per_batch, n_cast=len(to_cast)),
        grid=(n_tiles + 1,),
        in_specs=[cur(D_MODEL), _const_spec((1, D_MODEL)), _const_spec((D_MODEL, D_FF)),
                  _const_spec((D_MODEL, D_FF)), _const_spec((D_FF, D_MODEL)), _const_spec((1, D_MODEL)),
                  _const_spec((D_MODEL, ZQ_W)), _const_spec((D_MODEL, ZG_W)), _const_spec((1, SMALL_W)),
                  _const_spec((SMALL_W, GLA_QK_W)), _const_spec((1, GLA_QK_W)), _const_spec((1, ML_HEADS * ML_DV)),
                  _const_spec((1, GLA_V_W)), _const_spec((GLA_SUB, LANES, LANES)), qo_s, kv_s, kv_s] + cast_specs,
        out_specs=([cur(D_MODEL), pl.BlockSpec((tm, D_MODEL), lambda i: (prev_tile(i), 0))] + state_specs + [qo_s]
                   + cast_specs),
        out_shape=[jax.ShapeDtypeStruct((n_tiles * tm, D_MODEL), F32),
                   jax.ShapeDtypeStruct((n_tiles * tm, D_MODEL), BF16),
                   jax.ShapeDtypeStruct((bsz, ML_HEADS, ML_DK, ML_DV), F32),
                   jax.ShapeDtypeStruct((bsz, ML_HEADS, ML_DK), F32),
                   jax.ShapeDtypeStruct((bsz, 1, ML_HEADS), F32),
                   jax.ShapeDtypeStruct((bsz, GLA_HEADS, GLA_DK, GLA_DV), F32),
                   jax.ShapeDtypeStruct((bs * ts, D_MODEL), F32)] + [jax.ShapeDtypeStruct(a.shape, BF16) for a in to_cast],
        scratch_shapes=[pltpu.VMEM((tm, ZQ_W), BF16), pltpu.VMEM((tm, ZG_W), F32),
                        pltpu.VMEM((ML_HEADS, ML_DK, ML_DV), F32), pltpu.VMEM((SUBLANES, LANES), F32),
                        pltpu.VMEM((SUBLANES, LANES), F32), pltpu.VMEM((GLA_HEADS, GLA_DK, GLA_DV), F32)],
        compiler_params=_params(("arbitrary",)),
        name="ffn_mix",
    )(x.reshape(bsz * t, D_MODEL), w["ffn1_g"], w["ffn1_wg"], w["ffn1_wu"], w["ffn1_wd"], w["mix_g"], w["w_in_q"],
      w["w_in_g"], w["gate_bias"], w["w_a2"], w["b_a"], w["mlstm_out_g"], w["gla_out_g"], w["diag_sum"],
      q_s, _cache_rows_view(k_cache), _cache_rows_view(v_cache), *to_cast)
    return x1, ym, (c_new[None], n_new[None], m_new.reshape(1, bsz, ML_HEADS), s_new[None]), o_s, cast


def _post_mix_kernel(x1_ref, ym_ref, wout_ref, gx_ref, wq_ref, x2_ref, q_ref):
    x2 = x1_ref[...] + _dot(ym_ref[...].astype(BF16), wout_ref[...])
    x2_ref[...] = x2
    hq = _rms(x2, gx_ref[...]).astype(BF16)
    q_ref[...] = _dot(hq, wq_ref[...]).astype(q_ref.dtype)


def _post_mix(x1, ym, w, tm, q_dtype):
    n = x1.shape[0]
    row = lambda: pl.BlockSpec((tm, D_MODEL), lambda i: (i, 0))
    return pl.pallas_call(
        _post_mix_kernel,
        grid=(n // tm,),
        in_specs=[row(), row(), _const_spec((D_MODEL, D_MODEL)), _const_spec((1, D_MODEL)),
                  _const_spec((D_MODEL, D_MODEL))],
        out_specs=[row(), row()],
        out_shape=[jax.ShapeDtypeStruct((n, D_MODEL), F32), jax.ShapeDtypeStruct((n, D_MODEL), q_dtype)],
        compiler_params=_params(("arbitrary",)),
        name="post_mix",
    )(x1, ym, w["w_out"], w["xattn_g"], w["xattn_wq"])


def _softmax(s):
    e = jnp.exp(s - jnp.max(s, axis=-1, keepdims=True))
    return e / jnp.sum(e, axis=-1, keepdims=True)


def _post_fused_kernel(x1_ref, ym_ref, k_ref, v_ref, wout_ref, gx_ref, wq_ref, wo_ref, g2_ref, wg_ref, wu_ref,
                       wd_ref, gf_ref, y_ref):
    x2 = x1_ref[0] + _dot(ym_ref[0].astype(BF16), wout_ref[...])
    q = _dot(_rms(x2, gx_ref[...]).astype(BF16), wq_ref[...]).astype(BF16)
    heads = [slice(h * XA_DH, (h + 1) * XA_DH) for h in range(XA_HEADS)]
    k_full, v_full = _cache_rows_load(k_ref, 0), _cache_rows_load(v_ref, 0)
    scores = [_dot_nt(q[:, hs], k_full[:, hs]) * (XA_DH ** -0.5) for hs in heads]
    probs = [_softmax(s).astype(BF16) for s in scores]
    o = jnp.concatenate([_dot(p, v_full[:, hs]).astype(BF16) for hs, p in zip(heads, probs)], axis=1)
    x3 = x2 + _dot(o, wo_ref[...])
    x4 = _swiglu_residual(x3, g2_ref, wg_ref, wu_ref, wd_ref)
    y_ref[0] = _rms(x4, gf_ref[...])


def _post_fused(x1, ym, k, v, w, tm):
    bsz, t = x1.shape[0], x1.shape[1]
    row = lambda: pl.BlockSpec((1, tm, D_MODEL), lambda b, j: (b, j, 0))
    kv = pl.BlockSpec((1, N_MEM * CACHE_ROW_GROUP, LANES), lambda b, j: (b, 0, 0))
    sq = _const_spec((D_MODEL, D_MODEL))
    vec = _const_spec((1, D_MODEL))
    return pl.pallas_call(
        _post_fused_kernel,
        grid=(bsz, t // tm),
        in_specs=[row(), row(), kv, kv, sq, vec, sq, sq, vec, _const_spec((D_MODEL, D_FF)),
                  _const_spec((D_MODEL, D_FF)), _const_spec((D_FF, D_MODEL)), vec],
        out_specs=row(),
        out_shape=jax.ShapeDtypeStruct((bsz, t, D_MODEL), F32),
        compiler_params=_params(("arbitrary", "arbitrary")),
        name="post_fused",
    )(x1, ym, k, v, w["w_out"], w["xattn_g"], w["xattn_wq"], w["xattn_wo"], w["ffn2_g"], w["ffn2_wg"],
      w["ffn2_wu"], w["ffn2_wd"], w["final_g"])


XA_LANE_TILES = XA_DH // LANES
CACHE_ROW_GROUP = XA_LANE_TILES * XA_HEADS


def _cache_rows_view(x):
    bsz = x.shape[0]
    x = x.reshape(bsz, N_MEM, XA_HEADS, XA_LANE_TILES, LANES)
    return x.transpose(0, 1, 3, 2, 4).reshape(bsz, N_MEM * CACHE_ROW_GROUP, LANES)


def _cache_rows_unview(x):
    bsz = x.shape[0]
    x = x.reshape(bsz, N_MEM, XA_LANE_TILES, XA_HEADS, LANES)
    return x.transpose(0, 1, 3, 2, 4).reshape(bsz, N_MEM, XA_HEADS, XA_DH)


def _cache_rows_store(ref, b, x):
    for h in range(XA_HEADS):
        for lt in range(XA_LANE_TILES):
            lo = h * XA_DH + lt * LANES
            ref[b, pl.ds(lt * XA_HEADS + h, N_MEM, stride=CACHE_ROW_GROUP), :] = x[:, lo:lo + LANES]


def _cache_rows_load(ref, b):
    cols = [ref[b, pl.ds(lt * XA_HEADS + h, N_MEM, stride=CACHE_ROW_GROUP), :]
            for h in range(XA_HEADS) for lt in range(XA_LANE_TILES)]
    return jnp.concatenate(cols, axis=1).astype(BF16)


def _xattn_cache_stages(q_ref, k_ref, v_ref, o_ref):
    bb = k_ref.shape[0]
    tq = q_ref.shape[0] // bb
    lane_head = lax.broadcasted_iota(jnp.int32, (1, D_MODEL), 1) // XA_DH
    qs = [q_ref[b * tq:(b + 1) * tq, :] for b in range(bb)]
    q_bds = [jnp.concatenate([jnp.where(lane_head == h, q, 0.0) for h in range(XA_HEADS)], axis=0).astype(BF16)
             for q in qs]
    k_fulls = [_cache_rows_load(k_ref, b) for b in range(bb)]
    yield
    scores = [_dot_nt(q_bds[b], k_fulls[b]) * (XA_DH ** -0.5) for b in range(bb)]
    yield
    p_all = _softmax(jnp.concatenate(scores, axis=0)).astype(BF16)
    v_fulls = [_cache_rows_load(v_ref, b) for b in range(bb)]
    yield
    rows = XA_HEADS * tq
    o_fulls = [_dot(p_all[b * rows:(b + 1) * rows], v_fulls[b]) for b in range(bb)]
    yield
    for b in range(bb):
        o = jnp.zeros((tq, D_MODEL), F32)
        for h in range(XA_HEADS):
            o = jnp.where(lane_head == h, o_fulls[b][h * tq:(h + 1) * tq], o)
        o_ref[b * tq:(b + 1) * tq, :] = o.astype(o_ref.dtype)


def _ffn_out_kernel(x2_ref, o_ref, wo_ref, g2_ref, wg_ref, wu_ref, wd_ref, gf_ref, y_ref):
    x3 = x2_ref[...] + _dot(o_ref[...].astype(BF16), wo_ref[...])
    x4 = _swiglu_residual(x3, g2_ref, wg_ref, wu_ref, wd_ref)
    y_ref[...] = _rms(x4, gf_ref[...])


def _ffn_out(x2, o, w, tm):
    n = x2.shape[0]
    row = lambda: pl.BlockSpec((tm, D_MODEL), lambda i: (i, 0))
    return pl.pallas_call(
        _ffn_out_kernel,
        grid=(n // tm,),
        in_specs=[row(), row(), _const_spec((D_MODEL, D_MODEL)), _const_spec((1, D_MODEL)),
                  _const_spec((D_MODEL, D_FF)), _const_spec((D_MODEL, D_FF)), _const_spec((D_FF, D_MODEL)),
                  _const_spec((1, D_MODEL))],
        out_specs=row(),
        out_shape=jax.ShapeDtypeStruct((n, D_MODEL), F32),
        compiler_params=_params(("arbitrary",)),
        name="ffn_out",
    )(x2, o, w["xattn_wo"], w["ffn2_g"], w["ffn2_wg"], w["ffn2_wu"], w["ffn2_wd"], w["final_g"])


def _memkv_kernel(m_ref, g_ref, wk_ref, wv_ref, k_ref, v_ref):
    hn = _rms(m_ref[0], g_ref[...]).astype(BF16)
    _cache_rows_store(k_ref, 0, _dot(hn, wk_ref[...]))
    _cache_rows_store(v_ref, 0, _dot(hn, wv_ref[...]))


def _memkv(mem, w):
    bsz = mem.shape[0]
    rows = pl.BlockSpec((1, N_MEM * CACHE_ROW_GROUP, LANES), lambda b: (b, 0, 0))
    return pl.pallas_call(
        _memkv_kernel,
        grid=(bsz,),
        in_specs=[pl.BlockSpec((1, N_MEM, D_MODEL), lambda b: (b, 0, 0)), _const_spec((1, D_MODEL)),
                  _const_spec((D_MODEL, D_MODEL)), _const_spec((D_MODEL, D_MODEL))],
        out_specs=[rows, rows],
        out_shape=[jax.ShapeDtypeStruct((bsz, N_MEM * CACHE_ROW_GROUP, LANES), F32)] * 2,
        compiler_params=_params(("arbitrary",)),
        name="memkv",
    )(mem, w["mem_g"], w["xattn_wk"], w["xattn_wv"])


IN_SIZES = (("mq", ML_HEADS * ML_DK), ("mk", ML_HEADS * ML_DK), ("mv", ML_HEADS * ML_DV), ("mi", ML_HEADS),
            ("mf", ML_HEADS), ("mo", ML_HEADS * ML_DV), ("gq", GLA_QK_W), ("gk", GLA_QK_W), ("gv", GLA_V_W),
            ("ga", GLA_RANK), ("gg", GLA_V_W))
IN_OFFSET = {name: sum(width for _, width in IN_SIZES[:i]) for i, (name, _) in enumerate(IN_SIZES)}
D_IN = sum(width for _, width in IN_SIZES)
IN_MOVES = ((0, ZQ_MQ, IN_OFFSET["mq"], ZQ_GQ - ZQ_MQ), (0, ZQ_GQ, IN_OFFSET["gq"], ZQ_W - ZQ_GQ),
            (1, ZG_MO, IN_OFFSET["mo"], ZG_GG - ZG_MO), (1, ZG_GG, IN_OFFSET["gg"], ZG_SMALL - ZG_GG))
IN_TAIL = D_IN // LANES * LANES
TM_REGROUP = MXU_WIDTH


def _lane_shift_select(k_rows, shift):
    r = lax.broadcasted_iota(jnp.int32, (k_rows, LANES), 0)
    c = lax.broadcasted_iota(jnp.int32, (k_rows, LANES), 1)
    return jnp.where(r == c + shift, 1.0, 0.0).astype(BF16)


def _w_in_regroup_kernel(w_ref, aux_ref, q_ref, g_ref):
    for slab, dst, src, width in IN_MOVES:
        out_ref = (q_ref, g_ref)[slab]
        for t in range(width // LANES):
            s = src + t * LANES
            base = s // LANES * LANES
            shift = s - base
            if shift == 0:
                tile = w_ref[:, s:s + LANES]
            elif base + 2 * LANES <= D_IN:
                tile = _dot(w_ref[:, base:base + 2 * LANES].astype(BF16), _lane_shift_select(2 * LANES, shift))
            else:
                assert base + LANES == IN_TAIL and s + LANES == D_IN
                tile = (_dot(w_ref[:, base:base + LANES].astype(BF16), _lane_shift_select(LANES, shift))
                        + aux_ref[:, LANES:].astype(F32))
            out_ref[:, dst + t * LANES:dst + (t + 1) * LANES] = tile.astype(BF16)
    g_ref[:, ZG_SMALL:] = aux_ref[:, :LANES]


def _w_in_regroup(w_in):
    assert w_in.shape == (D_MODEL, D_IN)
    col = lambda name, width: w_in[:, IN_OFFSET[name]:IN_OFFSET[name] + width]
    small = jnp.concatenate([col("mi", ML_HEADS), col("mf", ML_HEADS), col("ga", GLA_RANK)], axis=1)
    tail = w_in[:, IN_TAIL:]
    aux = jnp.concatenate([jnp.pad(small, ((0, 0), (0, SMALL_W - small.shape[1]))),
                           jnp.pad(tail, ((0, 0), (LANES - tail.shape[1], 0)))], axis=1).astype(BF16)
    rows = lambda width: pl.BlockSpec((TM_REGROUP, width), lambda i: (i, 0))
    return pl.pallas_call(
        _w_in_regroup_kernel,
        grid=(D_MODEL // TM_REGROUP,),
        in_specs=[rows(D_IN), rows(2 * LANES)],
        out_specs=[rows(ZQ_W), rows(ZG_W)],
        out_shape=[jax.ShapeDtypeStruct((D_MODEL, ZQ_W), BF16), jax.ShapeDtypeStruct((D_MODEL, ZG_W), BF16)],
        compiler_params=_params(("arbitrary",)),
        name="w_in_regroup",
    )(w_in, aux)


def _prep_weights(p):
    bf = lambda a: a.astype(BF16)
    row = lambda a: a.reshape(1, -1).astype(F32)
    w_in_q, w_in_g = _w_in_regroup(p["w_in"])
    gate_bias = jnp.concatenate([p["mlstm_b_i"], p["mlstm_b_f"], jnp.zeros((SMALL_W - 2 * ML_HEADS,), F32)])
    w_a2 = jnp.zeros((SMALL_W, GLA_QK_W), F32).at[2 * ML_HEADS:2 * ML_HEADS + GLA_RANK].set(p["gla_w_a2"])
    lane = jnp.arange(LANES)
    same_head = (lane[:, None] // GLA_DK) == (lane[None, :] // GLA_DK)
    diag_sum = jnp.stack([same_head & ((lane[None, :] % GLA_SUB) == j) for j in range(GLA_SUB)]).astype(BF16)
    return dict(
        ffn1_g=row(p["ffn1_norm_g"]), ffn1_wg=bf(p["ffn1_w_gate"]), ffn1_wu=bf(p["ffn1_w_up"]),
        ffn1_wd=bf(p["ffn1_w_down"]), mix_g=row(p["mix_norm_g"]), w_in_q=w_in_q, w_in_g=w_in_g,
        gate_bias=row(gate_bias), w_a2=bf(w_a2), b_a=row(p["gla_b_a"]), mlstm_out_g=row(p["mlstm_out_g"]),
        gla_out_g=row(p["gla_out_g"]), diag_sum=diag_sum,
        w_out=bf(p["w_out"]), xattn_g=row(p["xattn_norm_g"]), xattn_wq=bf(p["xattn_w_q"]),
        mem_g=row(p["mem_norm_g"]), ffn2_g=row(p["ffn2_norm_g"]), final_g=row(p["final_g"]),
        xattn_wo=p["xattn_w_o"], xattn_wk=p["xattn_w_k"], xattn_wv=p["xattn_w_v"], ffn2_wg=p["ffn2_w_gate"],
        ffn2_wu=p["ffn2_w_up"], ffn2_wd=p["ffn2_w_down"])


LATE_WEIGHTS = ("xattn_wo", "xattn_wk", "xattn_wv", "ffn2_wg", "ffn2_wu", "ffn2_wd")


def kernel(x_prompt, x_sample, mem_prompt, cache_mem_k, cache_mem_v, state_mlstm_c, state_mlstm_n, state_mlstm_m, state_gla_s, ffn1_norm_g, ffn1_w_gate, ffn1_w_up, ffn1_w_down, mix_norm_g, w_in, mlstm_b_i, mlstm_b_f, mlstm_out_g, gla_w_a2, gla_b_a, gla_out_g, w_out, xattn_norm_g, mem_norm_g, xattn_w_q, xattn_w_k, xattn_w_v, xattn_w_o, ffn2_norm_g, ffn2_w_gate, ffn2_w_up, ffn2_w_down, final_norm_g):
    assert ffn1_norm_g.shape[0] == 1, "single-layer stack"
    layer = dict(ffn1_norm_g=ffn1_norm_g, ffn1_w_gate=ffn1_w_gate, ffn1_w_up=ffn1_w_up, ffn1_w_down=ffn1_w_down,
                 mix_norm_g=mix_norm_g, w_in=w_in, mlstm_b_i=mlstm_b_i, mlstm_b_f=mlstm_b_f,
                 mlstm_out_g=mlstm_out_g, gla_w_a2=gla_w_a2, gla_b_a=gla_b_a, gla_out_g=gla_out_g, w_out=w_out,
                 xattn_norm_g=xattn_norm_g, mem_norm_g=mem_norm_g, xattn_w_q=xattn_w_q, xattn_w_k=xattn_w_k,
                 xattn_w_v=xattn_w_v, xattn_w_o=xattn_w_o, ffn2_norm_g=ffn2_norm_g, ffn2_w_gate=ffn2_w_gate,
                 ffn2_w_up=ffn2_w_up, ffn2_w_down=ffn2_w_down)
    p = {name: arr[0] for name, arr in layer.items()}
    p["final_g"] = final_norm_g
    w = _prep_weights(p)

    bp, tp, _ = x_prompt.shape
    bs, ts, _ = x_sample.shape

    state = (state_mlstm_c[0], state_mlstm_n[0], state_mlstm_m[0].reshape(bs, 1, ML_HEADS), state_gla_s[0])
    x1_s, zq_s, zg_s = _ffn_in(x_sample.reshape(bs * ts, D_MODEL), w, TM_FFN_IN, F32)
    ym_s, c_s, n_s, m_s, s_s = _mixer_state(zq_s, zg_s, w, state, ts, SAMPLE_MIXER_BATCHES)
    states_s = (c_s[None], n_s[None], m_s.reshape(1, bs, ML_HEADS), s_s[None])
    x2_s, q_s = _post_mix(x1_s, ym_s, w, TM_FFN_IN, F32)

    x1_p, ym_p, states_p, o_s, late = _ffn_mix(x_prompt, w, TM_FFN_IN, q_s, cache_mem_k[0], cache_mem_v[0],
                                               [w[name] for name in LATE_WEIGHTS])
    w.update(zip(LATE_WEIGHTS, late))
    mem_k_p, mem_v_p = _memkv(mem_prompt, w)
    y_p = _post_fused(x1_p.reshape(bp, tp, D_MODEL), ym_p.reshape(bp, tp, D_MODEL), mem_k_p, mem_v_p, w, TM_POST)
    y_s = _ffn_out(x2_s, o_s, w, TM_FFN_IN).reshape(bs, ts, D_MODEL)

    return (y_p, y_s, _cache_rows_unview(mem_k_p)[None], _cache_rows_unview(mem_v_p)[None]) + states_p + states_s
```

```python
import functools

import jax
import jax.numpy as jnp
from jax import lax
from jax.experimental import pallas as pl
from jax.experimental.pallas import tpu as pltpu

F32 = jnp.float32
BF16 = jnp.bfloat16

D_MODEL = 1024
D_FF = 2816
ML_HEADS = 4
ML_DK = 128
ML_DV = 128
GLA_HEADS = 4
GLA_DK = 64
GLA_DV = 128
GLA_RANK = 16
GLA_TAU = 16.0
N_MEM = 256
XA_HEADS = 4
XA_DH = D_MODEL // XA_HEADS
EPS = 1e-6
CHUNK = 64
LOG2_E = 1.4426950408889634
LN_2 = 0.6931471805599453
LANES = 128
SUBLANES = 8
BF16_SUBLANES = 2 * SUBLANES

ZQ_MQ, ZQ_MK, ZQ_MV = 0, 512, 1024
ZQ_GQ, ZQ_GK, ZQ_GV = 1536, 1792, 2048
ZQ_W = 2560
ZG_MO, ZG_GG, ZG_SMALL = 0, 512, 1024
ZG_W = 1152
SMALL_W = LANES
GLA_QK_W = GLA_HEADS * GLA_DK
GLA_V_W = GLA_HEADS * GLA_DV
GLA_PAIRS = GLA_HEADS // 2
GLA_SUB = SUBLANES

MXU_WIDTH = 256
FF_CHUNKS = ((0, 6 * MXU_WIDTH), (6 * MXU_WIDTH, D_FF))
VMEM_LIMIT_BYTES = 56 * 1024 * 1024

TM_FFN_IN = MXU_WIDTH
TM_POST = 2 * MXU_WIDTH
SAMPLE_MIXER_BATCHES = 16


def _rms(x, g):
    return x * lax.rsqrt(jnp.mean(x * x, axis=-1, keepdims=True) + EPS) * g


def _log_sigmoid(x):
    return jnp.minimum(x, 0.0) - jnp.log1p(jnp.exp(-jnp.abs(x)))


def _dot(a, b):
    return jnp.dot(a, b, preferred_element_type=F32)


def _dot_nt(a, b):
    return lax.dot_general(a, b, (((1,), (1,)), ((), ())), preferred_element_type=F32)


def _dot_f32(a, b):
    return jnp.dot(a, b, precision=lax.Precision.HIGHEST, preferred_element_type=F32)


def _swiglu_residual(x, g_ref, wg_ref, wu_ref, wd_ref):
    h = _rms(x, g_ref[...]).astype(BF16)
    acts = []
    for lo, hi in FF_CHUNKS:
        g = _dot(h, wg_ref[:, lo:hi])
        u = _dot(h, wu_ref[:, lo:hi])
        acts.append(((g * jax.nn.sigmoid(g)) * u).astype(BF16))
    acc = jnp.zeros_like(x)
    for (lo, hi), a in zip(FF_CHUNKS, acts):
        acc = acc + _dot(a, wd_ref[lo:hi, :])
    return x + 0.5 * acc


def _const_spec(shape):
    nd = len(shape)
    return pl.BlockSpec(shape, lambda *_: (0,) * nd, pipeline_mode=pl.Buffered(1))


def _params(sem):
    return pltpu.CompilerParams(dimension_semantics=sem, vmem_limit_bytes=VMEM_LIMIT_BYTES)


def _ffn_in_kernel(x_ref, g1_ref, wg_ref, wu_ref, wd_ref, gm_ref, wq_ref, wgt_ref, x1_ref, zq_ref, zg_ref):
    _drain(_ffn_in_pieces(x_ref[...], g1_ref, wg_ref, wu_ref, wd_ref, gm_ref, wq_ref, wgt_ref, x1_ref, zq_ref, zg_ref))


def _ffn_in(x, w, tm, zq_dtype):
    n = x.shape[0]
    row = lambda width: pl.BlockSpec((tm, width), lambda i: (i, 0))
    return pl.pallas_call(
        _ffn_in_kernel,
        grid=(n // tm,),
        in_specs=[row(D_MODEL), _const_spec((1, D_MODEL)), _const_spec((D_MODEL, D_FF)),
                  _const_spec((D_MODEL, D_FF)), _const_spec((D_FF, D_MODEL)), _const_spec((1, D_MODEL)),
                  _const_spec((D_MODEL, ZQ_W)), _const_spec((D_MODEL, ZG_W))],
        out_specs=[row(D_MODEL), row(ZQ_W), row(ZG_W)],
        out_shape=[jax.ShapeDtypeStruct((n, D_MODEL), F32), jax.ShapeDtypeStruct((n, ZQ_W), zq_dtype),
                   jax.ShapeDtypeStruct((n, ZG_W), F32)],
        compiler_params=_params(("arbitrary",)),
        name="ffn_in",
    )(x, w["ffn1_g"], w["ffn1_wg"], w["ffn1_wu"], w["ffn1_wd"], w["mix_g"], w["w_in_q"], w["w_in_g"])


def _mixer_stages(zqs, zgs, states, consts, ee_ref, *, L, t_real, chain):
    bias, wa2, ba, gml, ggl = consts
    nchunk = len(zqs)
    groups = [(b, h) for b in range(nchunk) for h in range(ML_HEADS)]
    pairs = [(b, p) for b in range(nchunk) for p in range(GLA_PAIRS)]
    padded = t_real < L
    valid = lax.broadcasted_iota(jnp.int32, (L, 1), 0) < t_real
    rr = lax.broadcasted_iota(jnp.int32, (L, L), 0)
    cc = lax.broadcasted_iota(jnp.int32, (L, L), 1)
    tril = cc <= rr
    c = GLA_SUB
    nb = L // c
    lane = lax.broadcasted_iota(jnp.int32, (1, LANES), 1)
    lane_blk = (lane % GLA_DK) // c
    lane_head = lane // GLA_DK
    v_lane_head = lax.broadcasted_iota(jnp.int32, (1, 2 * GLA_DV), 1) // GLA_DV
    row_blk = (lax.broadcasted_iota(jnp.int32, (2 * L, 1), 0) % L) // c
    t_in = lax.broadcasted_iota(jnp.int32, (1, c, 1), 1)
    new_states = [dict(c=[None] * ML_HEADS, n=[None] * ML_HEADS, m=[None] * ML_HEADS, s=[None] * GLA_HEADS)
                  for _ in range(nchunk)]
    prev = lambda b: new_states[b - 1] if chain and b > 0 else states[0 if chain else b]
    known = lambda b: not chain or b == 0

    carried = {}

    def carried_mlstm(b, h):
        st = prev(b)
        d = ml[(b, h)]
        carried[(b, h)] = _dot(d["qb"], st["c"][h].astype(BF16))

    def carried_gla(b, p):
        s_prev = prev(b)["s"]
        zero_blk = jnp.zeros((GLA_DK, GLA_DV), BF16)
        s_bd = jnp.concatenate(
            [jnp.concatenate([s_prev[2 * p].astype(BF16), zero_blk], axis=1),
             jnp.concatenate([zero_blk, s_prev[2 * p + 1].astype(BF16)], axis=1)], axis=0)
        carried[(b, "gla", p)] = _dot(gl[(b, p)]["q_dec"], s_bd)

    smalls, sms, lfs, b_cols, b_rows, sm_ts, las, bcs = [], [], [], [], [], [], [], []
    for b in range(nchunk):
        small = zgs[b][:, ZG_SMALL:ZG_SMALL + SMALL_W]
        sm = small + bias
        lf = _log_sigmoid(sm) * LOG2_E
        sm = sm * LOG2_E
        if padded:
            sm = jnp.where(valid, sm, -jnp.inf)
            lf = jnp.where(valid, lf, 0.0)
        smalls.append(small)
        sms.append(sm)
        lfs.append(lf)

    long_chunk = L > SUBLANES
    col_w = 1 if long_chunk else LANES
    tril_f, triu_f = tril.astype(F32), (rr <= cc).astype(F32)

    def cumsum_rows(x):
        if long_chunk:
            return _dot_f32(tril_f, x)
        acc = x[0:1]
        rows = [acc]
        for r in range(1, L):
            acc = acc + x[r:r + 1]
            rows.append(acc)
        return jnp.concatenate(rows, axis=0)

    lf_ts = [lf.T[0:SUBLANES] for lf in lfs] if long_chunk else None
    yield
    for b in range(nchunk):
        b_cols.append(cumsum_rows(lfs[b]))
        b_rows.append(_dot_f32(lf_ts[b], triu_f) if long_chunk else b_cols[b].T[0:SUBLANES])
        sm_ts.append(sms[b].T)
        la = _log_sigmoid(_dot(smalls[b].astype(BF16), wa2) + ba) * (LOG2_E / GLA_TAU)
        las.append(jnp.where(valid, la, 0.0) if padded else la)

    ml = {}
    for g in groups:
        b, h = g
        zq = zqs[b]
        qf = zq[:, ZQ_MQ + h * ML_DK:ZQ_MQ + (h + 1) * ML_DK].astype(F32)
        kf = zq[:, ZQ_MK + h * ML_DK:ZQ_MK + (h + 1) * ML_DK].astype(F32) * (ML_DK ** -0.5)
        vf = zq[:, ZQ_MV + h * ML_DV:ZQ_MV + (h + 1) * ML_DV].astype(F32)
        if padded:
            kf = jnp.where(valid, kf, 0.0)
            vf = jnp.where(valid, vf, 0.0)
        ml[g] = dict(qf=qf, qb=qf.astype(BF16), kf=kf, kb=kf.astype(BF16), vb=vf.astype(BF16))
    yield
    for b in range(nchunk):
        bcs.append(cumsum_rows(las[b]))
    for g in groups:
        d = ml[g]
        d["qk"] = _dot_nt(d["qb"], d["kb"])
        if known(g[0]):
            carried_mlstm(*g)

    yield
    for g in groups:
        b, h = g
        d = ml[g]
        i_col = jnp.broadcast_to(sms[b][:, h:h + 1], (L, col_w))
        b_col = jnp.broadcast_to(b_cols[b][:, ML_HEADS + h:ML_HEADS + h + 1], (L, col_w))
        b_row = b_rows[b][ML_HEADS + h:ML_HEADS + h + 1, :]
        i_row = sm_ts[b][h:h + 1, :]
        a_col = b_col + prev(b)["m"][h] * LOG2_E
        dm = jnp.where(tril, b_col[:, :L] - (b_row - i_row), -jnp.inf)
        mt = jnp.maximum(a_col, jnp.max(dm, axis=1, keepdims=True))
        w_inter = jnp.exp2(a_col - mt)
        s = d["qk"] * jnp.exp2(dm - mt[:, :L])
        kw = d["kf"] * jnp.exp2((b_col[L - 1:L] - mt[L - 1:L]) - (b_col - i_col))
        d.update(mt=mt, w_inter=w_inter, s=s, kw=kw, kw_t=kw.T.astype(BF16))
        new_states[b]["m"][h] = mt[L - 1:L, 0:1] * LN_2

    gl = {}
    for b in range(nchunk):
        zq = zqs[b]
        gq = zq[:, ZQ_GQ:ZQ_GQ + GLA_QK_W].astype(F32) * (GLA_DK ** -0.5)
        gk = zq[:, ZQ_GK:ZQ_GK + GLA_QK_W].astype(F32)
        gv = zq[:, ZQ_GV:ZQ_GV + GLA_V_W].astype(F32)
        if padded:
            gk = jnp.where(valid, gk, 0.0)
            gv = jnp.where(valid, gv, 0.0)
        stack = lambda x: jnp.concatenate([x[:, :LANES], x[:, LANES:]], axis=0)
        q2, k2, b2 = stack(gq), stack(gk), stack(bcs[b])
        q3 = q2.reshape(2 * nb, c, LANES)
        k3 = k2.reshape(2 * nb, c, LANES)
        b3 = b2.reshape(2 * nb, c, LANES)
        pair_terms = []
        for j in range(min(c, t_real)):
            decay = jnp.exp2(jnp.where(t_in >= j, b3 - b3[:, j:j + 1, :], -jnp.inf))
            pair_terms.append((q3 * k3[:, j:j + 1, :] * decay).reshape(2 * L, LANES).astype(BF16))
        kt2 = (k3 * jnp.exp2(b3[:, c - 1:c, :] - b3)).reshape(2 * L, LANES) if nb > 1 else None
        gl[b] = dict(gv=gv, q2=q2, k2=k2, b2=b2, pair_terms=pair_terms, kt2=kt2)
    yield
    for b in range(nchunk):
        acc = jnp.zeros((2 * L, LANES), F32)
        for j, pair_j in enumerate(gl[b]["pair_terms"]):
            acc = acc + _dot(pair_j, ee_ref[j])
        gl[b]["a_diag"] = jnp.where(lane_blk == row_blk, acc, 0.0)

    for bp in pairs:
        b, p = bp
        d = gl[b]
        rows_p = slice(p * L, (p + 1) * L)
        q_p, k_p, b_p = d["q2"][rows_p], d["k2"][rows_p], d["b2"][rows_p]
        if long_chunk:
            decay_col = jnp.exp2(b_p[L - SUBLANES:L].T[:, SUBLANES - 1:SUBLANES])
        else:
            decay_col = jnp.exp2(jnp.broadcast_to(b_p[L - 1:L], (LANES, LANES)).T)
        e = dict(q_dec=(q_p * jnp.exp2(b_p)).astype(BF16),
                 kh_t=(k_p * jnp.exp2(b_p[L - 1:L] - b_p)).T.astype(BF16),
                 decay_col=decay_col)
        if nb > 1:
            kt_p = d["kt2"][rows_p].astype(BF16)
            k_bd = jnp.concatenate([jnp.where(lane_head == hh, kt_p, 0.0) for hh in range(2)], axis=0)
            slabs, offs = [], []
            off = 0
            for j in range(nb - 1):
                lo = (j + 1) * c
                slabs.append(q_p[lo:] * jnp.exp2(b_p[lo:] - b_p[lo - 1:lo]))
                offs.append(off)
                off += L - lo
            e.update(q_var=jnp.concatenate(slabs, axis=0).astype(BF16), k_bd=k_bd, offs=offs)
        gl[bp] = e
    yield
    for bp in pairs:
        e = gl[bp]
        if nb > 1:
            e["r"] = _dot_nt(e["q_var"], e["k_bd"])
        if known(bp[0]):
            carried_gla(*bp)

    for g in groups:
        d = ml[g]
        d["sv"] = _dot(d["s"].astype(BF16), d["vb"])
        d["c_upd"] = _dot(d["kw_t"], d["vb"])
        d["den"] = jnp.sum(d["s"], axis=1, keepdims=True)
        if known(g[0]):
            d["qn"] = jnp.sum(d["qf"] * prev(g[0])["n"][g[1]], axis=1, keepdims=True)
    yield
    for bp in pairs:
        b, p = bp
        d, e = gl[b], gl[bp]
        a_p = d["a_diag"][p * L:(p + 1) * L]
        if nb > 1:
            blocks = []
            for i in range(nb):
                blk = a_p[i * c:(i + 1) * c]
                for j in range(i):
                    lo_r = e["offs"][j] + (i - j - 1) * c
                    blk = jnp.where(lane_blk == j, e["r"][lo_r:lo_r + c], blk)
                blocks.append(blk)
            a_p = jnp.concatenate(blocks, axis=0)
        v_f = d["gv"][:, p * 2 * GLA_DV:(p + 1) * 2 * GLA_DV]
        v_p = v_f.astype(BF16)
        if L < GLA_DK:
            v_rows = []
            for hh in range(2):
                v_rows += [jnp.where(v_lane_head == hh, v_f, 0.0), jnp.zeros((GLA_DK - L, 2 * GLA_DV), F32)]
            v_bd = jnp.concatenate(v_rows, axis=0).astype(BF16)
        else:
            v_bd = jnp.concatenate([jnp.where(v_lane_head == hh, v_p, 0.0) for hh in range(2)], axis=0)
        e["s_upd"] = [_dot(e["kh_t"][hh * GLA_DK:(hh + 1) * GLA_DK], v_p[:, hh * GLA_DV:(hh + 1) * GLA_DV])
                      for hh in range(2)]
        e.update(scores=a_p.astype(BF16), v_bd=v_bd)
    yield
    for bp in pairs:
        e = gl[bp]
        e["o_intra"] = _dot(e["scores"], e["v_bd"])

    yield
    ys = [[None] * (ML_HEADS + GLA_HEADS) for _ in range(nchunk)]

    def finish_mlstm(b):
        st = prev(b)
        for h in range(ML_HEADS):
            d = ml[(b, h)]
            mt, w_inter = d["mt"], d["w_inter"]
            qn = d["qn"] if known(b) else jnp.sum(d["qf"] * st["n"][h], axis=1, keepdims=True)
            den = d["den"] + w_inter * qn
            hh = (d["sv"] + w_inter * carried[(b, h)]) / jnp.maximum(jnp.abs(den), jnp.exp2(-mt))
            i_last = w_inter[L - 1:L]
            new_states[b]["c"][h] = i_last * st["c"][h] + d["c_upd"]
            new_states[b]["n"][h] = i_last * st["n"][h] + jnp.sum(d["kw"], axis=0, keepdims=True)
            hs = slice(h * ML_DV, (h + 1) * ML_DV)
            yn = hh * lax.rsqrt(jnp.mean(hh * hh, axis=-1, keepdims=True) + EPS) * gml[:, hs]
            ys[b][h] = jax.nn.sigmoid(zgs[b][:, ZG_MO + h * ML_DV:ZG_MO + (h + 1) * ML_DV]) * yn

    def finish_gla(b):
        st = prev(b)
        for p in range(GLA_PAIRS):
            e = gl[(b, p)]
            o = carried[(b, "gla", p)] + e["o_intra"]
            for hh in range(2):
                h = 2 * p + hh
                ds = slice(hh * GLA_DK, (hh + 1) * GLA_DK)
                vs = slice(hh * GLA_DV, (hh + 1) * GLA_DV)
                new_states[b]["s"][h] = e["decay_col"][ds] * st["s"][h] + e["s_upd"][hh]
                oh = o[:, vs]
                yn = oh * lax.rsqrt(jnp.mean(oh * oh, axis=-1, keepdims=True) + EPS) * ggl[:, h * GLA_DV:(h + 1) * GLA_DV]
                gg = zgs[b][:, ZG_GG + h * GLA_DV:ZG_GG + (h + 1) * GLA_DV]
                ys[b][ML_HEADS + h] = (gg * jax.nn.sigmoid(gg)) * yn

    if chain:
        for b in range(nchunk):
            if not known(b):
                for h in range(ML_HEADS):
                    carried_mlstm(b, h)
                for p in range(GLA_PAIRS):
                    carried_gla(b, p)
            finish_mlstm(b)
            finish_gla(b)
    else:
        for b in range(nchunk):
            finish_mlstm(b)
        for b in range(nchunk):
            finish_gla(b)
    return [jnp.concatenate(y, axis=1) for y in ys], new_states


def _drain(gen):
    try:
        while True:
            next(gen)
    except StopIteration as done:
        return done.value


def _mixer_compute(*args, **kwargs):
    return _drain(_mixer_stages(*args, **kwargs))


def _mixer_state_kernel(zq_ref, zg_ref, bias_ref, wa2_ref, ba_ref, gml_ref, ggl_ref, ee_ref, c0_ref, n0_ref, m0_ref,
                        s0_ref, y_ref, co_ref, no_ref, mo_ref, so_ref, *, L, t_real, nblk):
    consts = (bias_ref[...], wa2_ref[...], ba_ref[...], gml_ref[...], ggl_ref[...])
    lane = lax.broadcasted_iota(jnp.int32, (1, LANES), 1)

    def chunk_rows(ref, b):
        rows = ref[b * t_real:(b + 1) * t_real, :]
        return jnp.concatenate([rows, jnp.zeros((L - t_real, rows.shape[1]), rows.dtype)], axis=0)

    states = []
    for b in range(nblk):
        n_all = n0_ref[b]
        m_all = m0_ref[b]
        states.append(dict(c=[c0_ref[b, h] for h in range(ML_HEADS)],
                           n=[n_all[h:h + 1, :] for h in range(ML_HEADS)],
                           m=[m_all[:, h:h + 1] for h in range(ML_HEADS)],
                           s=[s0_ref[b, h] for h in range(GLA_HEADS)]))
    ys, new_states = _mixer_compute([chunk_rows(zq_ref, b) for b in range(nblk)],
                                    [chunk_rows(zg_ref, b) for b in range(nblk)], states, consts, ee_ref, L=L,
                                    t_real=t_real, chain=False)
    for b in range(nblk):
        st = new_states[b]
        y_ref[b * t_real:(b + 1) * t_real, :] = ys[b][0:t_real].astype(y_ref.dtype)
        for h in range(ML_HEADS):
            co_ref[b, h] = st["c"][h]
            so_ref[b, h] = st["s"][h]
        no_ref[b] = jnp.concatenate(st["n"], axis=0)
        m_row = jnp.zeros((1, LANES), F32)
        for h in range(ML_HEADS):
            m_row = jnp.where(lane == h, st["m"][h], m_row)
        mo_ref[b] = m_row[:, 0:ML_HEADS]


def _mixer_state(zq, zg, w, state, t_real, nblk):
    bsz = zq.shape[0] // t_real
    L = -(-t_real // SUBLANES) * SUBLANES
    rows = lambda width: pl.BlockSpec((nblk * t_real, width), lambda b: (b, 0))
    per_b = lambda *tail: pl.BlockSpec((nblk,) + tail, lambda b: (b,) + (0,) * len(tail))
    state_specs = [per_b(ML_HEADS, ML_DK, ML_DV), per_b(ML_HEADS, ML_DK), per_b(1, ML_HEADS),
                   per_b(GLA_HEADS, GLA_DK, GLA_DV)]
    consts = [_const_spec((1, SMALL_W)), _const_spec((SMALL_W, GLA_QK_W)), _const_spec((1, GLA_QK_W)),
              _const_spec((1, ML_HEADS * ML_DV)), _const_spec((1, GLA_V_W)), _const_spec((GLA_SUB, LANES, LANES))]
    return pl.pallas_call(
        functools.partial(_mixer_state_kernel, L=L, t_real=t_real, nblk=nblk),
        grid=(bsz // nblk,),
        in_specs=[rows(ZQ_W), rows(ZG_W)] + consts + state_specs,
        out_specs=[rows(D_MODEL)] + state_specs,
        out_shape=[jax.ShapeDtypeStruct((bsz * t_real, D_MODEL), F32),
                   jax.ShapeDtypeStruct((bsz, ML_HEADS, ML_DK, ML_DV), F32),
                   jax.ShapeDtypeStruct((bsz, ML_HEADS, ML_DK), F32),
                   jax.ShapeDtypeStruct((bsz, 1, ML_HEADS), F32),
                   jax.ShapeDtypeStruct((bsz, GLA_HEADS, GLA_DK, GLA_DV), F32)],
        compiler_params=_params(("arbitrary",)),
        name="mixer_state",
    )(zq, zg, w["gate_bias"], w["w_a2"], w["b_a"], w["mlstm_out_g"], w["gla_out_g"], w["diag_sum"], *state)


def _ffn_in_pieces(x, g1_ref, wg_ref, wu_ref, wd_ref, gm_ref, wq_ref, wgt_ref, x1_ref, zq_out, zg_out):
    h = _rms(x, g1_ref[...]).astype(BF16)
    acts = []
    for lo in range(0, D_FF, MXU_WIDTH):
        g = _dot(h, wg_ref[:, lo:lo + MXU_WIDTH])
        yield
        u = _dot(h, wu_ref[:, lo:lo + MXU_WIDTH])
        yield
        acts.append(((g * jax.nn.sigmoid(g)) * u).astype(BF16))
    a = jnp.concatenate(acts, axis=1)
    down = []
    for lo in range(0, D_MODEL, MXU_WIDTH):
        down.append(_dot(a, wd_ref[:, lo:lo + MXU_WIDTH]))
        yield
    x1 = x + 0.5 * jnp.concatenate(down, axis=1)
    x1_ref[...] = x1
    hm = _rms(x1, gm_ref[...]).astype(BF16)
    for lo in range(0, ZQ_W, MXU_WIDTH):
        zq_out[:, lo:lo + MXU_WIDTH] = _dot(hm, wq_ref[:, lo:lo + MXU_WIDTH]).astype(zq_out.dtype)
        yield
    for lo in range(0, ZG_W, MXU_WIDTH):
        hi = min(lo + MXU_WIDTH, ZG_W)
        zg_out[:, lo:hi] = _dot(hm, wgt_ref[:, lo:hi])
        yield


ATTN_STAGE_EVERY = 7
FFN_PIECES_PER_MIXER_STAGE = 1


def _ffn_mix_kernel(*refs, tiles_per_batch, n_cast):
    it = iter(refs)
    (x_ref, g1_ref, wg_ref, wu_ref, wd_ref, gm_ref, wq_ref, wgt_ref, bias_ref, wa2_ref, ba_ref, gml_ref, ggl_ref,
     ee_ref, qs_ref, ks_ref, vs_ref) = (next(it) for _ in range(17))
    cast_src = [next(it) for _ in range(n_cast)]
    x1_ref, ym_ref, co_ref, no_ref, mo_ref, so_ref, os_ref = (next(it) for _ in range(7))
    cast_dst = [next(it) for _ in range(n_cast)]
    zq_s, zg_s, c_s, n_s, m_s, s_s = (next(it) for _ in range(6))
    i = pl.program_id(0)
    for src, dst in zip(cast_src, cast_dst):
        dst[...] = src[...].astype(dst.dtype)

    @pl.when(i == 0)
    def _init():
        for ref in (zq_s, zg_s, c_s, n_s, m_s, s_s):
            ref[...] = jnp.zeros_like(ref)

    consts = (bias_ref[...], wa2_ref[...], ba_ref[...], gml_ref[...], ggl_ref[...])
    n_chunks = zq_s.shape[0] // CHUNK
    rows = [slice(k * CHUNK, (k + 1) * CHUNK) for k in range(n_chunks)]
    zqs = [zq_s[r, :] for r in rows]
    zgs = [zg_s[r, :] for r in rows]
    starts_batch = (i - 1) % tiles_per_batch == 0
    carry = lambda v: jnp.where(starts_batch, 0.0, v)
    state = dict(c=[carry(c_s[h]) for h in range(ML_HEADS)], n=[carry(n_s[h:h + 1, :]) for h in range(ML_HEADS)],
                 m=[carry(m_s[h:h + 1, 0:1]) for h in range(ML_HEADS)], s=[carry(s_s[h]) for h in range(GLA_HEADS)])

    def mixers():
        st = state
        for k in range(n_chunks):
            ys, new = yield from _mixer_stages([zqs[k]], [zgs[k]], [st], consts, ee_ref, L=CHUNK, t_real=CHUNK,
                                               chain=True)
            ym_ref[rows[k], :] = ys[0].astype(ym_ref.dtype)
            st = new[0]
            yield
        return st

    ffn = _ffn_in_pieces(x_ref[...], g1_ref, wg_ref, wu_ref, wd_ref, gm_ref, wq_ref, wgt_ref, x1_ref, zq_s, zg_s)
    mix = mixers()
    attn = _xattn_cache_stages(qs_ref, ks_ref, vs_ref, os_ref)
    live = {"ffn": True, "attn": True}

    def advance(name, gen):
        if live[name]:
            try:
                next(gen)
            except StopIteration:
                live[name] = False

    slot = 0
    while True:
        try:
            next(mix)
        except StopIteration as done:
            last = done.value
            break
        for _ in range(FFN_PIECES_PER_MIXER_STAGE):
            advance("ffn", ffn)
        if slot % ATTN_STAGE_EVERY == ATTN_STAGE_EVERY // 2:
            advance("attn", attn)
        slot += 1
    for name, gen in (("ffn", ffn), ("attn", attn)):
        if live[name]:
            _drain(gen)

    for h in range(ML_HEADS):
        c_s[h] = last["c"][h]
        n_s[h:h + 1, :] = last["n"][h]
        m_s[h:h + 1, :] = jnp.broadcast_to(last["m"][h], (1, LANES))
        s_s[h] = last["s"][h]

    @pl.when(jnp.logical_and(i >= 1, (i - 1) % tiles_per_batch == tiles_per_batch - 1))
    def _emit_state():
        lane = lax.broadcasted_iota(jnp.int32, (1, LANES), 1)
        for h in range(ML_HEADS):
            co_ref[0, h] = last["c"][h]
            so_ref[0, h] = last["s"][h]
        no_ref[0] = jnp.concatenate(last["n"], axis=0)
        m_row = jnp.zeros((1, LANES), F32)
        for h in range(ML_HEADS):
            m_row = jnp.where(lane == h, last["m"][h], m_row)
        mo_ref[0] = m_row[:, 0:ML_HEADS]


def _ffn_mix(x, w, tm, q_s, k_cache, v_cache, to_cast):
    bsz, t, _ = x.shape
    tiles_per_batch = t // tm
    n_tiles = bsz * tiles_per_batch
    assert t % tm == 0 and tm % CHUNK == 0
    bs = k_cache.shape[0]
    ts = q_s.shape[0] // bs
    assert bs % n_tiles == 0
    sb = bs // n_tiles
    cur_tile = lambda i: jnp.minimum(i, n_tiles - 1)
    prev_tile = lambda i: jnp.maximum(i - 1, 0)
    cur = lambda width: pl.BlockSpec((tm, width), lambda i: (cur_tile(i), 0))
    per_b = lambda *tail: pl.BlockSpec((1,) + tail, lambda i: (prev_tile(i) // tiles_per_batch,) + (0,) * len(tail))
    state_specs = [per_b(ML_HEADS, ML_DK, ML_DV), per_b(ML_HEADS, ML_DK), per_b(1, ML_HEADS),
                   per_b(GLA_HEADS, GLA_DK, GLA_DV)]
    qo_s = pl.BlockSpec((sb * ts, D_MODEL), lambda i: (cur_tile(i), 0))
    kv_s = pl.BlockSpec((sb, N_MEM * CACHE_ROW_GROUP, LANES), lambda i: (cur_tile(i), 0, 0))
    cast_specs = []
    for a in to_cast:
        rb = next(r for r in range(BF16_SUBLANES, a.shape[0] + 1, BF16_SUBLANES)
                  if a.shape[0] % r == 0 and a.shape[0] // r <= n_tiles)
        cast_specs.append(pl.BlockSpec((rb, a.shape[1]), lambda i, last=a.shape[0] // rb - 1: (jnp.minimum(i, last), 0)))
    x1, ym, c_new, n_new, m_new, s_new, o_s, *cast = pl.pallas_call(
        functools.partial(_ffn_mix_kernel, tiles_per_batch=tiles_per_batch, n_cast=len(to_cast)),
        grid=(n_tiles + 1,),
        in_specs=[cur(D_MODEL), _const_spec((1, D_MODEL)), _const_spec((D_MODEL, D_FF)),
                  _const_spec((D_MODEL, D_FF)), _const_spec((D_FF, D_MODEL)), _const_spec((1, D_MODEL)),
                  _const_spec((D_MODEL, ZQ_W)), _const_spec((D_MODEL, ZG_W)), _const_spec((1, SMALL_W)),
                  _const_spec((SMALL_W, GLA_QK_W)), _const_spec((1, GLA_QK_W)), _const_spec((1, ML_HEADS * ML_DV)),
                  _const_spec((1, GLA_V_W)), _const_spec((GLA_SUB, LANES, LANES)), qo_s, kv_s, kv_s] + cast_specs,
        out_specs=([cur(D_MODEL), pl.BlockSpec((tm, D_MODEL), lambda i: (prev_tile(i), 0))] + state_specs + [qo_s]
                   + cast_specs),
        out_shape=[jax.ShapeDtypeStruct((n_tiles * tm, D_MODEL), F32),
                   jax.ShapeDtypeStruct((n_tiles * tm, D_MODEL), BF16),
                   jax.ShapeDtypeStruct((bsz, ML_HEADS, ML_DK, ML_DV), F32),
                   jax.ShapeDtypeStruct((bsz, ML_HEADS, ML_DK), F32),
                   jax.ShapeDtypeStruct((bsz, 1, ML_HEADS), F32),
                   jax.ShapeDtypeStruct((bsz, GLA_HEADS, GLA_DK, GLA_DV), F32),
                   jax.ShapeDtypeStruct((bs * ts, D_MODEL), F32)] + [jax.ShapeDtypeStruct(a.shape, BF16) for a in to_cast],
        scratch_shapes=[pltpu.VMEM((tm, ZQ_W), BF16), pltpu.VMEM((tm, ZG_W), F32),
                        pltpu.VMEM((ML_HEADS, ML_DK, ML_DV), F32), pltpu.VMEM((SUBLANES, LANES), F32),
                        pltpu.VMEM((SUBLANES, LANES), F32), pltpu.VMEM((GLA_HEADS, GLA_DK, GLA_DV), F32)],
        compiler_params=_params(("arbitrary",)),
        name="ffn_mix",
    )(x.reshape(bsz * t, D_MODEL), w["ffn1_g"], w["ffn1_wg"], w["ffn1_wu"], w["ffn1_wd"], w["mix_g"], w["w_in_q"],
      w["w_in_g"], w["gate_bias"], w["w_a2"], w["b_a"], w["mlstm_out_g"], w["gla_out_g"], w["diag_sum"],
      q_s, _cache_rows_view(k_cache), _cache_rows_view(v_cache), *to_cast)
    return x1, ym, (c_new[None], n_new[None], m_new.reshape(1, bsz, ML_HEADS), s_new[None]), o_s, cast


def _post_mix_kernel(x1_ref, ym_ref, wout_ref, gx_ref, wq_ref, x2_ref, q_ref):
    x2 = x1_ref[...] + _dot(ym_ref[...].astype(BF16), wout_ref[...])
    x2_ref[...] = x2
    hq = _rms(x2, gx_ref[...]).astype(BF16)
    q_ref[...] = _dot(hq, wq_ref[...]).astype(q_ref.dtype)


def _post_mix(x1, ym, w, tm, q_dtype):
    n = x1.shape[0]
    row = lambda: pl.BlockSpec((tm, D_MODEL), lambda i: (i, 0))
    return pl.pallas_call(
        _post_mix_kernel,
        grid=(n // tm,),
        in_specs=[row(), row(), _const_spec((D_MODEL, D_MODEL)), _const_spec((1, D_MODEL)),
                  _const_spec((D_MODEL, D_MODEL))],
        out_specs=[row(), row()],
        out_shape=[jax.ShapeDtypeStruct((n, D_MODEL), F32), jax.ShapeDtypeStruct((n, D_MODEL), q_dtype)],
        compiler_params=_params(("arbitrary",)),
        name="post_mix",
    )(x1, ym, w["w_out"], w["xattn_g"], w["xattn_wq"])


def _softmax(s):
    e = jnp.exp(s - jnp.max(s, axis=-1, keepdims=True))
    return e / jnp.sum(e, axis=-1, keepdims=True)


def _post_fused_kernel(x1_ref, ym_ref, k_ref, v_ref, wout_ref, gx_ref, wq_ref, wo_ref, g2_ref, wg_ref, wu_ref,
                       wd_ref, gf_ref, y_ref):
    x2 = x1_ref[0] + _dot(ym_ref[0].astype(BF16), wout_ref[...])
    q = _dot(_rms(x2, gx_ref[...]).astype(BF16), wq_ref[...]).astype(BF16)
    heads = [slice(h * XA_DH, (h + 1) * XA_DH) for h in range(XA_HEADS)]
    k_full, v_full = _cache_rows_load(k_ref, 0), _cache_rows_load(v_ref, 0)
    scores = [_dot_nt(q[:, hs], k_full[:, hs]) * (XA_DH ** -0.5) for hs in heads]
    probs = [_softmax(s).astype(BF16) for s in scores]
    o = jnp.concatenate([_dot(p, v_full[:, hs]).astype(BF16) for hs, p in zip(heads, probs)], axis=1)
    x3 = x2 + _dot(o, wo_ref[...])
    x4 = _swiglu_residual(x3, g2_ref, wg_ref, wu_ref, wd_ref)
    y_ref[0] = _rms(x4, gf_ref[...])


def _post_fused(x1, ym, k, v, w, tm):
    bsz, t = x1.shape[0], x1.shape[1]
    row = lambda: pl.BlockSpec((1, tm, D_MODEL), lambda b, j: (b, j, 0))
    kv = pl.BlockSpec((1, N_MEM * CACHE_ROW_GROUP, LANES), lambda b, j: (b, 0, 0))
    sq = _const_spec((D_MODEL, D_MODEL))
    vec = _const_spec((1, D_MODEL))
    return pl.pallas_call(
        _post_fused_kernel,
        grid=(bsz, t // tm),
        in_specs=[row(), row(), kv, kv, sq, vec, sq, sq, vec, _const_spec((D_MODEL, D_FF)),
                  _const_spec((D_MODEL, D_FF)), _const_spec((D_FF, D_MODEL)), vec],
        out_specs=row(),
        out_shape=jax.ShapeDtypeStruct((bsz, t, D_MODEL), F32),
        compiler_params=_params(("arbitrary", "arbitrary")),
        name="post_fused",
    )(x1, ym, k, v, w["w_out"], w["xattn_g"], w["xattn_wq"], w["xattn_wo"], w["ffn2_g"], w["ffn2_wg"],
      w["ffn2_wu"], w["ffn2_wd"], w["final_g"])


XA_LANE_TILES = XA_DH // LANES
CACHE_ROW_GROUP = XA_LANE_TILES * XA_HEADS


def _cache_rows_view(x):
    bsz = x.shape[0]
    x = x.reshape(bsz, N_MEM, XA_HEADS, XA_LANE_TILES, LANES)
    return x.transpose(0, 1, 3, 2, 4).reshape(bsz, N_MEM * CACHE_ROW_GROUP, LANES)


def _cache_rows_unview(x):
    bsz = x.shape[0]
    x = x.reshape(bsz, N_MEM, XA_LANE_TILES, XA_HEADS, LANES)
    return x.transpose(0, 1, 3, 2, 4).reshape(bsz, N_MEM, XA_HEADS, XA_DH)


def _cache_rows_store(ref, b, x):
    for h in range(XA_HEADS):
        for lt in range(XA_LANE_TILES):
            lo = h * XA_DH + lt * LANES
            ref[b, pl.ds(lt * XA_HEADS + h, N_MEM, stride=CACHE_ROW_GROUP), :] = x[:, lo:lo + LANES]


def _cache_rows_load(ref, b):
    cols = [ref[b, pl.ds(lt * XA_HEADS + h, N_MEM, stride=CACHE_ROW_GROUP), :]
            for h in range(XA_HEADS) for lt in range(XA_LANE_TILES)]
    return jnp.concatenate(cols, axis=1).astype(BF16)


def _xattn_cache_stages(q_ref, k_ref, v_ref, o_ref):
    bb = k_ref.shape[0]
    tq = q_ref.shape[0] // bb
    lane_head = lax.broadcasted_iota(jnp.int32, (1, D_MODEL), 1) // XA_DH
    qs = [q_ref[b * tq:(b + 1) * tq, :] for b in range(bb)]
    q_bds = [jnp.concatenate([jnp.where(lane_head == h, q, 0.0) for h in range(XA_HEADS)], axis=0).astype(BF16)
             for q in qs]
    k_fulls = [_cache_rows_load(k_ref, b) for b in range(bb)]
    yield
    scores = [_dot_nt(q_bds[b], k_fulls[b]) * (XA_DH ** -0.5) for b in range(bb)]
    yield
    p_all = _softmax(jnp.concatenate(scores, axis=0)).astype(BF16)
    v_fulls = [_cache_rows_load(v_ref, b) for b in range(bb)]
    yield
    rows = XA_HEADS * tq
    o_fulls = [_dot(p_all[b * rows:(b + 1) * rows], v_fulls[b]) for b in range(bb)]
    yield
    for b in range(bb):
        o = jnp.zeros((tq, D_MODEL), F32)
        for h in range(XA_HEADS):
            o = jnp.where(lane_head == h, o_fulls[b][h * tq:(h + 1) * tq], o)
        o_ref[b * tq:(b + 1) * tq, :] = o.astype(o_ref.dtype)


def _ffn_out_kernel(x2_ref, o_ref, wo_ref, g2_ref, wg_ref, wu_ref, wd_ref, gf_ref, y_ref):
    x3 = x2_ref[...] + _dot(o_ref[...].astype(BF16), wo_ref[...])
    x4 = _swiglu_residual(x3, g2_ref, wg_ref, wu_ref, wd_ref)
    y_ref[...] = _rms(x4, gf_ref[...])


def _ffn_out(x2, o, w, tm):
    n = x2.shape[0]
    row = lambda: pl.BlockSpec((tm, D_MODEL), lambda i: (i, 0))
    return pl.pallas_call(
        _ffn_out_kernel,
        grid=(n // tm,),
        in_specs=[row(), row(), _const_spec((D_MODEL, D_MODEL)), _const_spec((1, D_MODEL)),
                  _const_spec((D_MODEL, D_FF)), _const_spec((D_MODEL, D_FF)), _const_spec((D_FF, D_MODEL)),
                  _const_spec((1, D_MODEL))],
        out_specs=row(),
        out_shape=jax.ShapeDtypeStruct((n, D_MODEL), F32),
        compiler_params=_params(("arbitrary",)),
        name="ffn_out",
    )(x2, o, w["xattn_wo"], w["ffn2_g"], w["ffn2_wg"], w["ffn2_wu"], w["ffn2_wd"], w["final_g"])


def _memkv_kernel(m_ref, g_ref, wk_ref, wv_ref, k_ref, v_ref):
    hn = _rms(m_ref[0], g_ref[...]).astype(BF16)
    _cache_rows_store(k_ref, 0, _dot(hn, wk_ref[...]))
    _cache_rows_store(v_ref, 0, _dot(hn, wv_ref[...]))


def _memkv(mem, w):
    bsz = mem.shape[0]
    rows = pl.BlockSpec((1, N_MEM * CACHE_ROW_GROUP, LANES), lambda b: (b, 0, 0))
    return pl.pallas_call(
        _memkv_kernel,
        grid=(bsz,),
        in_specs=[pl.BlockSpec((1, N_MEM, D_MODEL), lambda b: (b, 0, 0)), _const_spec((1, D_MODEL)),
                  _const_spec((D_MODEL, D_MODEL)), _const_spec((D_MODEL, D_MODEL))],
        out_specs=[rows, rows],
        out_shape=[jax.ShapeDtypeStruct((bsz, N_MEM * CACHE_ROW_GROUP, LANES), F32)] * 2,
        compiler_params=_params(("arbitrary",)),
        name="memkv",
    )(mem, w["mem_g"], w["xattn_wk"], w["xattn_wv"])


IN_SIZES = (("mq", ML_HEADS * ML_DK), ("mk", ML_HEADS * ML_DK), ("mv", ML_HEADS * ML_DV), ("mi", ML_HEADS),
            ("mf", ML_HEADS), ("mo", ML_HEADS * ML_DV), ("gq", GLA_QK_W), ("gk", GLA_QK_W), ("gv", GLA_V_W),
            ("ga", GLA_RANK), ("gg", GLA_V_W))
IN_OFFSET = {name: sum(width for _, width in IN_SIZES[:i]) for i, (name, _) in enumerate(IN_SIZES)}
D_IN = sum(width for _, width in IN_SIZES)
IN_MOVES = ((0, ZQ_MQ, IN_OFFSET["mq"], ZQ_GQ - ZQ_MQ), (0, ZQ_GQ, IN_OFFSET["gq"], ZQ_W - ZQ_GQ),
            (1, ZG_MO, IN_OFFSET["mo"], ZG_GG - ZG_MO), (1, ZG_GG, IN_OFFSET["gg"], ZG_SMALL - ZG_GG))
TM_REGROUP = MXU_WIDTH
assert IN_OFFSET["mf"] == IN_OFFSET["mi"] + ML_HEADS
assert all(IN_OFFSET[name] % SUBLANES == 0 for name in ("mq", "mi", "mo", "gq", "ga", "gg"))


def _w_in_regroup_kernel(wt_ref, q_ref, g_ref):
    def put(out_ref, dst, rows_t):
        out_ref[:, dst:dst + LANES] = rows_t.T.astype(BF16)

    for slab, dst, src, width in IN_MOVES:
        for off in range(0, width, LANES):
            put((q_ref, g_ref)[slab], dst + off, wt_ref[src + off:src + off + LANES, :])
    narrow = 2 * ML_HEADS + GLA_RANK
    put(g_ref, ZG_SMALL, jnp.concatenate(
        [wt_ref[IN_OFFSET["mi"]:IN_OFFSET["mi"] + 2 * ML_HEADS, :],
         wt_ref[IN_OFFSET["ga"]:IN_OFFSET["ga"] + GLA_RANK, :],
         jnp.zeros((SMALL_W - narrow, wt_ref.shape[1]), F32)], axis=0))


def _w_in_regroup(w_in):
    assert w_in.shape == (D_MODEL, D_IN)
    rows = lambda width: pl.BlockSpec((TM_REGROUP, width), lambda i: (i, 0))
    return pl.pallas_call(
        _w_in_regroup_kernel,
        grid=(D_MODEL // TM_REGROUP,),
        in_specs=[pl.BlockSpec((D_IN, TM_REGROUP), lambda i: (0, i))],
        out_specs=[rows(ZQ_W), rows(ZG_W)],
        out_shape=[jax.ShapeDtypeStruct((D_MODEL, ZQ_W), BF16), jax.ShapeDtypeStruct((D_MODEL, ZG_W), BF16)],
        compiler_params=_params(("arbitrary",)),
        name="w_in_regroup",
    )(w_in.T)


def _prep_weights(p):
    bf = lambda a: a.astype(BF16)
    row = lambda a: a.reshape(1, -1).astype(F32)
    w_in_q, w_in_g = _w_in_regroup(p["w_in"])
    gate_bias = jnp.concatenate([p["mlstm_b_i"], p["mlstm_b_f"], jnp.zeros((SMALL_W - 2 * ML_HEADS,), F32)])
    w_a2 = jnp.zeros((SMALL_W, GLA_QK_W), F32).at[2 * ML_HEADS:2 * ML_HEADS + GLA_RANK].set(p["gla_w_a2"])
    lane = jnp.arange(LANES)
    same_head = (lane[:, None] // GLA_DK) == (lane[None, :] // GLA_DK)
    diag_sum = jnp.stack([same_head & ((lane[None, :] % GLA_SUB) == j) for j in range(GLA_SUB)]).astype(BF16)
    return dict(
        ffn1_g=row(p["ffn1_norm_g"]), ffn1_wg=bf(p["ffn1_w_gate"]), ffn1_wu=bf(p["ffn1_w_up"]),
        ffn1_wd=bf(p["ffn1_w_down"]), mix_g=row(p["mix_norm_g"]), w_in_q=w_in_q, w_in_g=w_in_g,
        gate_bias=row(gate_bias), w_a2=bf(w_a2), b_a=row(p["gla_b_a"]), mlstm_out_g=row(p["mlstm_out_g"]),
        gla_out_g=row(p["gla_out_g"]), diag_sum=diag_sum,
        w_out=bf(p["w_out"]), xattn_g=row(p["xattn_norm_g"]), xattn_wq=bf(p["xattn_w_q"]),
        mem_g=row(p["mem_norm_g"]), ffn2_g=row(p["ffn2_norm_g"]), final_g=row(p["final_g"]),
        xattn_wo=p["xattn_w_o"], xattn_wk=p["xattn_w_k"], xattn_wv=p["xattn_w_v"], ffn2_wg=p["ffn2_w_gate"],
        ffn2_wu=p["ffn2_w_up"], ffn2_wd=p["ffn2_w_down"])


LATE_WEIGHTS = ("xattn_wo", "xattn_wk", "xattn_wv", "ffn2_wg", "ffn2_wu", "ffn2_wd")


def kernel(x_prompt, x_sample, mem_prompt, cache_mem_k, cache_mem_v, state_mlstm_c, state_mlstm_n, state_mlstm_m, state_gla_s, ffn1_norm_g, ffn1_w_gate, ffn1_w_up, ffn1_w_down, mix_norm_g, w_in, mlstm_b_i, mlstm_b_f, mlstm_out_g, gla_w_a2, gla_b_a, gla_out_g, w_out, xattn_norm_g, mem_norm_g, xattn_w_q, xattn_w_k, xattn_w_v, xattn_w_o, ffn2_norm_g, ffn2_w_gate, ffn2_w_up, ffn2_w_down, final_norm_g):
    assert ffn1_norm_g.shape[0] == 1, "single-layer stack"
    layer = dict(ffn1_norm_g=ffn1_norm_g, ffn1_w_gate=ffn1_w_gate, ffn1_w_up=ffn1_w_up, ffn1_w_down=ffn1_w_down,
                 mix_norm_g=mix_norm_g, w_in=w_in, mlstm_b_i=mlstm_b_i, mlstm_b_f=mlstm_b_f,
                 mlstm_out_g=mlstm_out_g, gla_w_a2=gla_w_a2, gla_b_a=gla_b_a, gla_out_g=gla_out_g, w_out=w_out,
                 xattn_norm_g=xattn_norm_g, mem_norm_g=mem_norm_g, xattn_w_q=xattn_w_q, xattn_w_k=xattn_w_k,
                 xattn_w_v=xattn_w_v, xattn_w_o=xattn_w_o, ffn2_norm_g=ffn2_norm_g, ffn2_w_gate=ffn2_w_gate,
                 ffn2_w_up=ffn2_w_up, ffn2_w_down=ffn2_w_down)
    p = {name: arr[0] for name, arr in layer.items()}
    p["final_g"] = final_norm_g
    w = _prep_weights(p)

    bp, tp, _ = x_prompt.shape
    bs, ts, _ = x_sample.shape

    state = (state_mlstm_c[0], state_mlstm_n[0], state_mlstm_m[0].reshape(bs, 1, ML_HEADS), state_gla_s[0])
    x1_s, zq_s, zg_s = _ffn_in(x_sample.reshape(bs * ts, D_MODEL), w, TM_FFN_IN, F32)
    ym_s, c_s, n_s, m_s, s_s = _mixer_state(zq_s, zg_s, w, state, ts, SAMPLE_MIXER_BATCHES)
    states_s = (c_s[None], n_s[None], m_s.reshape(1, bs, ML_HEADS), s_s[None])
    x2_s, q_s = _post_mix(x1_s, ym_s, w, TM_FFN_IN, F32)

    x1_p, ym_p, states_p, o_s, late = _ffn_mix(x_prompt, w, TM_FFN_IN, q_s, cache_mem_k[0], cache_mem_v[0],
                                               [w[name] for name in LATE_WEIGHTS])
    w.update(zip(LATE_WEIGHTS, late))
    mem_k_p, mem_v_p = _memkv(mem_prompt, w)
    y_p = _post_fused(x1_p.reshape(bp, tp, D_MODEL), ym_p.reshape(bp, tp, D_MODEL), mem_k_p, mem_v_p, w, TM_POST)
    y_s = _ffn_out(x2_s, o_s, w, TM_FFN_IN).reshape(bs, ts, D_MODEL)

    return (y_p, y_s, _cache_rows_unview(mem_k_p)[None], _cache_rows_unview(mem_v_p)[None]) + states_p + states_s
```

```python
import functools

import jax
import jax.numpy as jnp
from jax import lax
from jax.experimental import pallas as pl
from jax.experimental.pallas import tpu as pltpu

F32 = jnp.float32
BF16 = jnp.bfloat16

D_MODEL = 1024
D_FF = 2816
ML_HEADS = 4
ML_DK = 128
ML_DV = 128
GLA_HEADS = 4
GLA_DK = 64
GLA_DV = 128
GLA_RANK = 16
GLA_TAU = 16.0
N_MEM = 256
XA_HEADS = 4
XA_DH = D_MODEL // XA_HEADS
EPS = 1e-6
CHUNK = 64
LOG2_E = 1.4426950408889634
LN_2 = 0.6931471805599453
LANES = 128
SUBLANES = 8
BF16_SUBLANES = 2 * SUBLANES

ZQ_MQ, ZQ_MK, ZQ_MV = 0, 512, 1024
ZQ_GQ, ZQ_GK, ZQ_GV = 1536, 1792, 2048
ZQ_W = 2560
ZG_MO, ZG_GG, ZG_SMALL = 0, 512, 1024
ZG_W = 1152
SMALL_W = LANES
GLA_QK_W = GLA_HEADS * GLA_DK
GLA_V_W = GLA_HEADS * GLA_DV
GLA_PAIRS = GLA_HEADS // 2
GLA_SUB = SUBLANES

MXU_WIDTH = 256
FF_CHUNKS = ((0, 6 * MXU_WIDTH), (6 * MXU_WIDTH, D_FF))
VMEM_LIMIT_BYTES = 56 * 1024 * 1024

TM_FFN_IN = MXU_WIDTH
TM_POST = 2 * MXU_WIDTH
SAMPLE_MIXER_BATCHES = 16


def _rms(x, g):
    return x * lax.rsqrt(jnp.mean(x * x, axis=-1, keepdims=True) + EPS) * g


def _log_sigmoid(x):
    return jnp.minimum(x, 0.0) - jnp.log1p(jnp.exp(-jnp.abs(x)))


def _dot(a, b):
    return jnp.dot(a, b, preferred_element_type=F32)


def _dot_nt(a, b):
    return lax.dot_general(a, b, (((1,), (1,)), ((), ())), preferred_element_type=F32)


def _dot_f32(a, b):
    return jnp.dot(a, b, precision=lax.Precision.HIGHEST, preferred_element_type=F32)


def _swiglu_residual(x, g_ref, wg_ref, wu_ref, wd_ref):
    h = _rms(x, g_ref[...]).astype(BF16)
    acts = []
    for lo, hi in FF_CHUNKS:
        g = _dot(h, wg_ref[:, lo:hi])
        u = _dot(h, wu_ref[:, lo:hi])
        acts.append(((g * jax.nn.sigmoid(g)) * u).astype(BF16))
    acc = jnp.zeros_like(x)
    for (lo, hi), a in zip(FF_CHUNKS, acts):
        acc = acc + _dot(a, wd_ref[lo:hi, :])
    return x + 0.5 * acc


def _const_spec(shape):
    nd = len(shape)
    return pl.BlockSpec(shape, lambda *_: (0,) * nd, pipeline_mode=pl.Buffered(1))


def _params(sem):
    return pltpu.CompilerParams(dimension_semantics=sem, vmem_limit_bytes=VMEM_LIMIT_BYTES)


ZQ_BLOCKS = 4


def _ffn_in_stream_kernel(x_ref, g1_ref, wg_ref, wu_ref, wd_ref, gm_ref, wq_ref, wgt_ref,
                          x1_ref, zq_ref, zg_ref, wg_out, wu_out, wd_out, h_s, acc_s, *, n_ff):
    j = pl.program_id(0)

    @pl.when(j == 0)
    def _norm():
        h_s[...] = _rms(x_ref[...], g1_ref[...]).astype(BF16)
        acc_s[...] = jnp.zeros_like(acc_s)

    @pl.when(j < n_ff)
    def _ffn_chunk():
        wg, wu, wd = (ref[...].astype(BF16) for ref in (wg_ref, wu_ref, wd_ref))
        wg_out[...] = wg
        wu_out[...] = wu
        wd_out[...] = wd
        h = h_s[...]
        g = _dot(h, wg)
        u = _dot(h, wu)
        acc_s[...] += _dot(((g * jax.nn.sigmoid(g)) * u).astype(BF16), wd)

    @pl.when(j == n_ff)
    def _residual():
        x1 = x_ref[...] + 0.5 * acc_s[...]
        x1_ref[...] = x1
        h_s[...] = _rms(x1, gm_ref[...]).astype(BF16)

    @pl.when((j >= n_ff) & (j < n_ff + ZQ_BLOCKS))
    def _zq_block():
        zq_ref[...] = _dot(h_s[...], wq_ref[...]).astype(zq_ref.dtype)

    @pl.when(j == n_ff + ZQ_BLOCKS)
    def _zg():
        zg_ref[...] = _dot(h_s[...], wgt_ref[...])


def _ffn_in_stream(x, w, zq_dtype):
    n = x.shape[0]
    n_ff = D_FF // MXU_WIDTH
    zq_blk = ZQ_W // ZQ_BLOCKS
    ff = lambda j: jnp.minimum(j, n_ff - 1)
    zqb = lambda j: jnp.clip(j - n_ff, 0, ZQ_BLOCKS - 1)
    ff_cols = pl.BlockSpec((D_MODEL, MXU_WIDTH), lambda j: (0, ff(j)))
    ff_rows = pl.BlockSpec((MXU_WIDTH, D_MODEL), lambda j: (ff(j), 0))
    resident = lambda width: pl.BlockSpec((n, width), lambda j: (0, 0))
    return pl.pallas_call(
        functools.partial(_ffn_in_stream_kernel, n_ff=n_ff),
        grid=(n_ff + ZQ_BLOCKS + 1,),
        in_specs=[_const_spec((n, D_MODEL)), _const_spec((1, D_MODEL)), ff_cols, ff_cols, ff_rows,
                  _const_spec((1, D_MODEL)), pl.BlockSpec((D_MODEL, zq_blk), lambda j: (0, zqb(j))),
                  _const_spec((D_MODEL, ZG_W))],
        out_specs=[resident(D_MODEL), pl.BlockSpec((n, zq_blk), lambda j: (0, zqb(j))), resident(ZG_W),
                   ff_cols, ff_cols, ff_rows],
        out_shape=[jax.ShapeDtypeStruct((n, D_MODEL), F32), jax.ShapeDtypeStruct((n, ZQ_W), zq_dtype),
                   jax.ShapeDtypeStruct((n, ZG_W), F32), jax.ShapeDtypeStruct((D_MODEL, D_FF), BF16),
                   jax.ShapeDtypeStruct((D_MODEL, D_FF), BF16), jax.ShapeDtypeStruct((D_FF, D_MODEL), BF16)],
        scratch_shapes=[pltpu.VMEM((n, D_MODEL), BF16), pltpu.VMEM((n, D_MODEL), F32)],
        compiler_params=_params(("arbitrary",)),
        name="ffn_in",
    )(x, w["ffn1_g"], w["ffn1_wg"], w["ffn1_wu"], w["ffn1_wd"], w["mix_g"], w["w_in_q"], w["w_in_g"])


def _mixer_stages(zqs, zgs, states, consts, ee_ref, *, L, t_real, chain):
    bias, wa2, ba, gml, ggl = consts
    nchunk = len(zqs)
    groups = [(b, h) for b in range(nchunk) for h in range(ML_HEADS)]
    pairs = [(b, p) for b in range(nchunk) for p in range(GLA_PAIRS)]
    padded = t_real < L
    valid = lax.broadcasted_iota(jnp.int32, (L, 1), 0) < t_real
    rr = lax.broadcasted_iota(jnp.int32, (L, L), 0)
    cc = lax.broadcasted_iota(jnp.int32, (L, L), 1)
    tril = cc <= rr
    c = GLA_SUB
    nb = L // c
    lane = lax.broadcasted_iota(jnp.int32, (1, LANES), 1)
    lane_blk = (lane % GLA_DK) // c
    lane_head = lane // GLA_DK
    v_lane_head = lax.broadcasted_iota(jnp.int32, (1, 2 * GLA_DV), 1) // GLA_DV
    row_blk = (lax.broadcasted_iota(jnp.int32, (2 * L, 1), 0) % L) // c
    t_in = lax.broadcasted_iota(jnp.int32, (1, c, 1), 1)
    new_states = [dict(c=[None] * ML_HEADS, n=[None] * ML_HEADS, m=[None] * ML_HEADS, s=[None] * GLA_HEADS)
                  for _ in range(nchunk)]
    prev = lambda b: new_states[b - 1] if chain and b > 0 else states[0 if chain else b]
    known = lambda b: not chain or b == 0

    carried = {}

    def carried_mlstm(b, h):
        st = prev(b)
        d = ml[(b, h)]
        carried[(b, h)] = _dot(d["qb"], st["c"][h].astype(BF16))

    def carried_gla(b, p):
        s_prev = prev(b)["s"]
        zero_blk = jnp.zeros((GLA_DK, GLA_DV), BF16)
        s_bd = jnp.concatenate(
            [jnp.concatenate([s_prev[2 * p].astype(BF16), zero_blk], axis=1),
             jnp.concatenate([zero_blk, s_prev[2 * p + 1].astype(BF16)], axis=1)], axis=0)
        carried[(b, "gla", p)] = _dot(gl[(b, p)]["q_dec"], s_bd)

    smalls, sms, lfs, b_cols, b_rows, sm_ts, las, bcs = [], [], [], [], [], [], [], []
    for b in range(nchunk):
        small = zgs[b][:, ZG_SMALL:ZG_SMALL + SMALL_W]
        sm = small + bias
        lf = _log_sigmoid(sm) * LOG2_E
        sm = sm * LOG2_E
        if padded:
            sm = jnp.where(valid, sm, -jnp.inf)
            lf = jnp.where(valid, lf, 0.0)
        smalls.append(small)
        sms.append(sm)
        lfs.append(lf)

    long_chunk = L > SUBLANES
    col_w = 1 if long_chunk else LANES
    tril_f, triu_f = tril.astype(F32), (rr <= cc).astype(F32)

    def cumsum_rows(x):
        if long_chunk:
            return _dot_f32(tril_f, x)
        acc = x[0:1]
        rows = [acc]
        for r in range(1, L):
            acc = acc + x[r:r + 1]
            rows.append(acc)
        return jnp.concatenate(rows, axis=0)

    lf_ts = [lf.T[0:SUBLANES] for lf in lfs] if long_chunk else None
    yield
    for b in range(nchunk):
        b_cols.append(cumsum_rows(lfs[b]))
        b_rows.append(_dot_f32(lf_ts[b], triu_f) if long_chunk else b_cols[b].T[0:SUBLANES])
        sm_ts.append(sms[b].T)
        la = _log_sigmoid(_dot(smalls[b].astype(BF16), wa2) + ba) * (LOG2_E / GLA_TAU)
        las.append(jnp.where(valid, la, 0.0) if padded else la)

    ml = {}
    for g in groups:
        b, h = g
        zq = zqs[b]
        qf = zq[:, ZQ_MQ + h * ML_DK:ZQ_MQ + (h + 1) * ML_DK].astype(F32)
        kf = zq[:, ZQ_MK + h * ML_DK:ZQ_MK + (h + 1) * ML_DK].astype(F32) * (ML_DK ** -0.5)
        vf = zq[:, ZQ_MV + h * ML_DV:ZQ_MV + (h + 1) * ML_DV].astype(F32)
        if padded:
            kf = jnp.where(valid, kf, 0.0)
            vf = jnp.where(valid, vf, 0.0)
        ml[g] = dict(qf=qf, qb=qf.astype(BF16), kf=kf, kb=kf.astype(BF16), vb=vf.astype(BF16))
    yield
    for b in range(nchunk):
        bcs.append(cumsum_rows(las[b]))
    for g in groups:
        d = ml[g]
        d["qk"] = _dot_nt(d["qb"], d["kb"])
        if known(g[0]):
            carried_mlstm(*g)

    yield
    for g in groups:
        b, h = g
        d = ml[g]
        i_col = jnp.broadcast_to(sms[b][:, h:h + 1], (L, col_w))
        b_col = jnp.broadcast_to(b_cols[b][:, ML_HEADS + h:ML_HEADS + h + 1], (L, col_w))
        b_row = b_rows[b][ML_HEADS + h:ML_HEADS + h + 1, :]
        i_row = sm_ts[b][h:h + 1, :]
        a_col = b_col + prev(b)["m"][h] * LOG2_E
        dm = jnp.where(tril, b_col[:, :L] - (b_row - i_row), -jnp.inf)
        mt = jnp.maximum(a_col, jnp.max(dm, axis=1, keepdims=True))
        w_inter = jnp.exp2(a_col - mt)
        s = d["qk"] * jnp.exp2(dm - mt[:, :L])
        kw = d["kf"] * jnp.exp2((b_col[L - 1:L] - mt[L - 1:L]) - (b_col - i_col))
        d.update(mt=mt, w_inter=w_inter, s=s, kw=kw, kw_t=kw.T.astype(BF16))
        new_states[b]["m"][h] = mt[L - 1:L, 0:1] * LN_2

    gl = {}
    for b in range(nchunk):
        zq = zqs[b]
        gq = zq[:, ZQ_GQ:ZQ_GQ + GLA_QK_W].astype(F32) * (GLA_DK ** -0.5)
        gk = zq[:, ZQ_GK:ZQ_GK + GLA_QK_W].astype(F32)
        gv = zq[:, ZQ_GV:ZQ_GV + GLA_V_W].astype(F32)
        if padded:
            gk = jnp.where(valid, gk, 0.0)
            gv = jnp.where(valid, gv, 0.0)
        stack = lambda x: jnp.concatenate([x[:, :LANES], x[:, LANES:]], axis=0)
        q2, k2, b2 = stack(gq), stack(gk), stack(bcs[b])
        q3 = q2.reshape(2 * nb, c, LANES)
        k3 = k2.reshape(2 * nb, c, LANES)
        b3 = b2.reshape(2 * nb, c, LANES)
        pair_terms = []
        for j in range(min(c, t_real)):
            decay = jnp.exp2(jnp.where(t_in >= j, b3 - b3[:, j:j + 1, :], -jnp.inf))
            pair_terms.append((q3 * k3[:, j:j + 1, :] * decay).reshape(2 * L, LANES).astype(BF16))
        kt2 = (k3 * jnp.exp2(b3[:, c - 1:c, :] - b3)).reshape(2 * L, LANES) if nb > 1 else None
        gl[b] = dict(gv=gv, q2=q2, k2=k2, b2=b2, pair_terms=pair_terms, kt2=kt2)
    yield
    for b in range(nchunk):
        acc = jnp.zeros((2 * L, LANES), F32)
        for j, pair_j in enumerate(gl[b]["pair_terms"]):
            acc = acc + _dot(pair_j, ee_ref[j])
        gl[b]["a_diag"] = jnp.where(lane_blk == row_blk, acc, 0.0)

    for bp in pairs:
        b, p = bp
        d = gl[b]
        rows_p = slice(p * L, (p + 1) * L)
        q_p, k_p, b_p = d["q2"][rows_p], d["k2"][rows_p], d["b2"][rows_p]
        if long_chunk:
            decay_col = jnp.exp2(b_p[L - SUBLANES:L].T[:, SUBLANES - 1:SUBLANES])
        else:
            decay_col = jnp.exp2(jnp.broadcast_to(b_p[L - 1:L], (LANES, LANES)).T)
        e = dict(q_dec=(q_p * jnp.exp2(b_p)).astype(BF16),
                 kh_t=(k_p * jnp.exp2(b_p[L - 1:L] - b_p)).T.astype(BF16),
                 decay_col=decay_col)
        if nb > 1:
            kt_p = d["kt2"][rows_p].astype(BF16)
            k_bd = jnp.concatenate([jnp.where(lane_head == hh, kt_p, 0.0) for hh in range(2)], axis=0)
            slabs, offs = [], []
            off = 0
            for j in range(nb - 1):
                lo = (j + 1) * c
                slabs.append(q_p[lo:] * jnp.exp2(b_p[lo:] - b_p[lo - 1:lo]))
                offs.append(off)
                off += L - lo
            e.update(q_var=jnp.concatenate(slabs, axis=0).astype(BF16), k_bd=k_bd, offs=offs)
        gl[bp] = e
    yield
    for bp in pairs:
        e = gl[bp]
        if nb > 1:
            e["r"] = _dot_nt(e["q_var"], e["k_bd"])
        if known(bp[0]):
            carried_gla(*bp)

    for g in groups:
        d = ml[g]
        d["sv"] = _dot(d["s"].astype(BF16), d["vb"])
        d["c_upd"] = _dot(d["kw_t"], d["vb"])
        d["den"] = jnp.sum(d["s"], axis=1, keepdims=True)
        if known(g[0]):
            d["qn"] = jnp.sum(d["qf"] * prev(g[0])["n"][g[1]], axis=1, keepdims=True)
    yield
    for bp in pairs:
        b, p = bp
        d, e = gl[b], gl[bp]
        a_p = d["a_diag"][p * L:(p + 1) * L]
        if nb > 1:
            blocks = []
            for i in range(nb):
                blk = a_p[i * c:(i + 1) * c]
                for j in range(i):
                    lo_r = e["offs"][j] + (i - j - 1) * c
                    blk = jnp.where(lane_blk == j, e["r"][lo_r:lo_r + c], blk)
                blocks.append(blk)
            a_p = jnp.concatenate(blocks, axis=0)
        v_f = d["gv"][:, p * 2 * GLA_DV:(p + 1) * 2 * GLA_DV]
        v_p = v_f.astype(BF16)
        if L < GLA_DK:
            v_rows = []
            for hh in range(2):
                v_rows += [jnp.where(v_lane_head == hh, v_f, 0.0), jnp.zeros((GLA_DK - L, 2 * GLA_DV), F32)]
            v_bd = jnp.concatenate(v_rows, axis=0).astype(BF16)
        else:
            v_bd = jnp.concatenate([jnp.where(v_lane_head == hh, v_p, 0.0) for hh in range(2)], axis=0)
        e["s_upd"] = [_dot(e["kh_t"][hh * GLA_DK:(hh + 1) * GLA_DK], v_p[:, hh * GLA_DV:(hh + 1) * GLA_DV])
                      for hh in range(2)]
        e.update(scores=a_p.astype(BF16), v_bd=v_bd)
    yield
    for bp in pairs:
        e = gl[bp]
        e["o_intra"] = _dot(e["scores"], e["v_bd"])

    yield
    ys = [[None] * (ML_HEADS + GLA_HEADS) for _ in range(nchunk)]

    def finish_mlstm(b):
        st = prev(b)
        for h in range(ML_HEADS):
            d = ml[(b, h)]
            mt, w_inter = d["mt"], d["w_inter"]
            qn = d["qn"] if known(b) else jnp.sum(d["qf"] * st["n"][h], axis=1, keepdims=True)
            den = d["den"] + w_inter * qn
            hh = (d["sv"] + w_inter * carried[(b, h)]) / jnp.maximum(jnp.abs(den), jnp.exp2(-mt))
            i_last = w_inter[L - 1:L]
            new_states[b]["c"][h] = i_last * st["c"][h] + d["c_upd"]
            new_states[b]["n"][h] = i_last * st["n"][h] + jnp.sum(d["kw"], axis=0, keepdims=True)
            hs = slice(h * ML_DV, (h + 1) * ML_DV)
            yn = hh * lax.rsqrt(jnp.mean(hh * hh, axis=-1, keepdims=True) + EPS) * gml[:, hs]
            ys[b][h] = jax.nn.sigmoid(zgs[b][:, ZG_MO + h * ML_DV:ZG_MO + (h + 1) * ML_DV]) * yn

    def finish_gla(b):
        st = prev(b)
        for p in range(GLA_PAIRS):
            e = gl[(b, p)]
            o = carried[(b, "gla", p)] + e["o_intra"]
            for hh in range(2):
                h = 2 * p + hh
                ds = slice(hh * GLA_DK, (hh + 1) * GLA_DK)
                vs = slice(hh * GLA_DV, (hh + 1) * GLA_DV)
                new_states[b]["s"][h] = e["decay_col"][ds] * st["s"][h] + e["s_upd"][hh]
                oh = o[:, vs]
                yn = oh * lax.rsqrt(jnp.mean(oh * oh, axis=-1, keepdims=True) + EPS) * ggl[:, h * GLA_DV:(h + 1) * GLA_DV]
                gg = zgs[b][:, ZG_GG + h * GLA_DV:ZG_GG + (h + 1) * GLA_DV]
                ys[b][ML_HEADS + h] = (gg * jax.nn.sigmoid(gg)) * yn

    if chain:
        for b in range(nchunk):
            if not known(b):
                for h in range(ML_HEADS):
                    carried_mlstm(b, h)
                for p in range(GLA_PAIRS):
                    carried_gla(b, p)
            finish_mlstm(b)
            finish_gla(b)
    else:
        for b in range(nchunk):
            finish_mlstm(b)
        for b in range(nchunk):
            finish_gla(b)
    return [jnp.concatenate(y, axis=1) for y in ys], new_states


def _drain(gen):
    try:
        while True:
            next(gen)
    except StopIteration as done:
        return done.value


def _mixer_compute(*args, **kwargs):
    return _drain(_mixer_stages(*args, **kwargs))


def _mixer_state_kernel(zq_ref, zg_ref, bias_ref, wa2_ref, ba_ref, gml_ref, ggl_ref, ee_ref, c0_ref, n0_ref, m0_ref,
                        s0_ref, y_ref, co_ref, no_ref, mo_ref, so_ref, *, L, t_real, nblk):
    consts = (bias_ref[...], wa2_ref[...], ba_ref[...], gml_ref[...], ggl_ref[...])
    lane = lax.broadcasted_iota(jnp.int32, (1, LANES), 1)

    def chunk_rows(ref, b):
        rows = ref[b * t_real:(b + 1) * t_real, :]
        return jnp.concatenate([rows, jnp.zeros((L - t_real, rows.shape[1]), rows.dtype)], axis=0)

    states = []
    for b in range(nblk):
        n_all = n0_ref[b]
        m_all = m0_ref[b]
        states.append(dict(c=[c0_ref[b, h] for h in range(ML_HEADS)],
                           n=[n_all[h:h + 1, :] for h in range(ML_HEADS)],
                           m=[m_all[:, h:h + 1] for h in range(ML_HEADS)],
                           s=[s0_ref[b, h] for h in range(GLA_HEADS)]))
    ys, new_states = _mixer_compute([chunk_rows(zq_ref, b) for b in range(nblk)],
                                    [chunk_rows(zg_ref, b) for b in range(nblk)], states, consts, ee_ref, L=L,
                                    t_real=t_real, chain=False)
    for b in range(nblk):
        st = new_states[b]
        y_ref[b * t_real:(b + 1) * t_real, :] = ys[b][0:t_real].astype(y_ref.dtype)
        for h in range(ML_HEADS):
            co_ref[b, h] = st["c"][h]
            so_ref[b, h] = st["s"][h]
        no_ref[b] = jnp.concatenate(st["n"], axis=0)
        m_row = jnp.zeros((1, LANES), F32)
        for h in range(ML_HEADS):
            m_row = jnp.where(lane == h, st["m"][h], m_row)
        mo_ref[b] = m_row[:, 0:ML_HEADS]


def _mixer_state(zq, zg, w, state, t_real, nblk):
    bsz = zq.shape[0] // t_real
    L = -(-t_real // SUBLANES) * SUBLANES
    rows = lambda width: pl.BlockSpec((nblk * t_real, width), lambda b: (b, 0))
    per_b = lambda *tail: pl.BlockSpec((nblk,) + tail, lambda b: (b,) + (0,) * len(tail))
    state_specs = [per_b(ML_HEADS, ML_DK, ML_DV), per_b(ML_HEADS, ML_DK), per_b(1, ML_HEADS),
                   per_b(GLA_HEADS, GLA_DK, GLA_DV)]
    consts = [_const_spec((1, SMALL_W)), _const_spec((SMALL_W, GLA_QK_W)), _const_spec((1, GLA_QK_W)),
              _const_spec((1, ML_HEADS * ML_DV)), _const_spec((1, GLA_V_W)), _const_spec((GLA_SUB, LANES, LANES))]
    return pl.pallas_call(
        functools.partial(_mixer_state_kernel, L=L, t_real=t_real, nblk=nblk),
        grid=(bsz // nblk,),
        in_specs=[rows(ZQ_W), rows(ZG_W)] + consts + state_specs,
        out_specs=[rows(D_MODEL)] + state_specs,
        out_shape=[jax.ShapeDtypeStruct((bsz * t_real, D_MODEL), F32),
                   jax.ShapeDtypeStruct((bsz, ML_HEADS, ML_DK, ML_DV), F32),
                   jax.ShapeDtypeStruct((bsz, ML_HEADS, ML_DK), F32),
                   jax.ShapeDtypeStruct((bsz, 1, ML_HEADS), F32),
                   jax.ShapeDtypeStruct((bsz, GLA_HEADS, GLA_DK, GLA_DV), F32)],
        compiler_params=_params(("arbitrary",)),
        name="mixer_state",
    )(zq, zg, w["gate_bias"], w["w_a2"], w["b_a"], w["mlstm_out_g"], w["gla_out_g"], w["diag_sum"], *state)


def _ffn_in_pieces(x, g1_ref, wg_ref, wu_ref, wd_ref, gm_ref, wq_ref, wgt_ref, x1_ref, zq_out, zg_out):
    h = _rms(x, g1_ref[...]).astype(BF16)
    acts = []
    for lo in range(0, D_FF, MXU_WIDTH):
        g = _dot(h, wg_ref[:, lo:lo + MXU_WIDTH])
        yield
        u = _dot(h, wu_ref[:, lo:lo + MXU_WIDTH])
        yield
        acts.append(((g * jax.nn.sigmoid(g)) * u).astype(BF16))
    a = jnp.concatenate(acts, axis=1)
    down = []
    for lo in range(0, D_MODEL, MXU_WIDTH):
        down.append(_dot(a, wd_ref[:, lo:lo + MXU_WIDTH]))
        yield
    x1 = x + 0.5 * jnp.concatenate(down, axis=1)
    x1_ref[...] = x1
    hm = _rms(x1, gm_ref[...]).astype(BF16)
    for lo in range(0, ZQ_W, MXU_WIDTH):
        zq_out[:, lo:lo + MXU_WIDTH] = _dot(hm, wq_ref[:, lo:lo + MXU_WIDTH]).astype(zq_out.dtype)
        yield
    for lo in range(0, ZG_W, MXU_WIDTH):
        hi = min(lo + MXU_WIDTH, ZG_W)
        zg_out[:, lo:hi] = _dot(hm, wgt_ref[:, lo:hi])
        yield


ATTN_STAGE_EVERY = 7
FFN_PIECES_PER_MIXER_STAGE = 1


def _ffn_mix_kernel(*refs, tiles_per_batch, n_cast):
    it = iter(refs)
    (x_ref, g1_ref, wg_ref, wu_ref, wd_ref, gm_ref, wq_ref, wgt_ref, bias_ref, wa2_ref, ba_ref, gml_ref, ggl_ref,
     ee_ref, qs_ref, ks_ref, vs_ref) = (next(it) for _ in range(17))
    cast_src = [next(it) for _ in range(n_cast)]
    x1_ref, ym_ref, co_ref, no_ref, mo_ref, so_ref, os_ref = (next(it) for _ in range(7))
    cast_dst = [next(it) for _ in range(n_cast)]
    zq_s, zg_s, c_s, n_s, m_s, s_s = (next(it) for _ in range(6))
    i = pl.program_id(0)
    for src, dst in zip(cast_src, cast_dst):
        dst[...] = src[...].astype(dst.dtype)

    @pl.when(i == 0)
    def _init():
        for ref in (zq_s, zg_s, c_s, n_s, m_s, s_s):
            ref[...] = jnp.zeros_like(ref)

    consts = (bias_ref[...], wa2_ref[...], ba_ref[...], gml_ref[...], ggl_ref[...])
    n_chunks = zq_s.shape[0] // CHUNK
    rows = [slice(k * CHUNK, (k + 1) * CHUNK) for k in range(n_chunks)]
    zqs = [zq_s[r, :] for r in rows]
    zgs = [zg_s[r, :] for r in rows]
    starts_batch = (i - 1) % tiles_per_batch == 0
    carry = lambda v: jnp.where(starts_batch, 0.0, v)
    state = dict(c=[carry(c_s[h]) for h in range(ML_HEADS)], n=[carry(n_s[h:h + 1, :]) for h in range(ML_HEADS)],
                 m=[carry(m_s[h:h + 1, 0:1]) for h in range(ML_HEADS)], s=[carry(s_s[h]) for h in range(GLA_HEADS)])

    def mixers():
        st = state
        for k in range(n_chunks):
            ys, new = yield from _mixer_stages([zqs[k]], [zgs[k]], [st], consts, ee_ref, L=CHUNK, t_real=CHUNK,
                                               chain=True)
            ym_ref[rows[k], :] = ys[0].astype(ym_ref.dtype)
            st = new[0]
            yield
        return st

    ffn = _ffn_in_pieces(x_ref[...], g1_ref, wg_ref, wu_ref, wd_ref, gm_ref, wq_ref, wgt_ref, x1_ref, zq_s, zg_s)
    mix = mixers()
    attn = _xattn_cache_stages(qs_ref, ks_ref, vs_ref, os_ref)
    live = {"ffn": True, "attn": True}

    def advance(name, gen):
        if live[name]:
            try:
                next(gen)
            except StopIteration:
                live[name] = False

    slot = 0
    while True:
        try:
            next(mix)
        except StopIteration as done:
            last = done.value
            break
        for _ in range(FFN_PIECES_PER_MIXER_STAGE):
            advance("ffn", ffn)
        if slot % ATTN_STAGE_EVERY == ATTN_STAGE_EVERY // 2:
            advance("attn", attn)
        slot += 1
    for name, gen in (("ffn", ffn), ("attn", attn)):
        if live[name]:
            _drain(gen)

    for h in range(ML_HEADS):
        c_s[h] = last["c"][h]
        n_s[h:h + 1, :] = last["n"][h]
        m_s[h:h + 1, :] = jnp.broadcast_to(last["m"][h], (1, LANES))
        s_s[h] = last["s"][h]

    @pl.when(jnp.logical_and(i >= 1, (i - 1) % tiles_per_batch == tiles_per_batch - 1))
    def _emit_state():
        lane = lax.broadcasted_iota(jnp.int32, (1, LANES), 1)
        for h in range(ML_HEADS):
            co_ref[0, h] = last["c"][h]
            so_ref[0, h] = last["s"][h]
        no_ref[0] = jnp.concatenate(last["n"], axis=0)
        m_row = jnp.zeros((1, LANES), F32)
        for h in range(ML_HEADS):
            m_row = jnp.where(lane == h, last["m"][h], m_row)
        mo_ref[0] = m_row[:, 0:ML_HEADS]


def _ffn_mix(x, w, tm, q_s, k_cache, v_cache, to_cast):
    bsz, t, _ = x.shape
    tiles_per_batch = t // tm
    n_tiles = bsz * tiles_per_batch
    assert t % tm == 0 and tm % CHUNK == 0
    bs = k_cache.shape[0]
    ts = q_s.shape[0] // bs
    assert bs % n_tiles == 0
    sb = bs // n_tiles
    cur_tile = lambda i: jnp.minimum(i, n_tiles - 1)
    prev_tile = lambda i: jnp.maximum(i - 1, 0)
    cur = lambda width: pl.BlockSpec((tm, width), lambda i: (cur_tile(i), 0))
    per_b = lambda *tail: pl.BlockSpec((1,) + tail, lambda i: (prev_tile(i) // tiles_per_batch,) + (0,) * len(tail))
    state_specs = [per_b(ML_HEADS, ML_DK, ML_DV), per_b(ML_HEADS, ML_DK), per_b(1, ML_HEADS),
                   per_b(GLA_HEADS, GLA_DK, GLA_DV)]
    qo_s = pl.BlockSpec((sb * ts, D_MODEL), lambda i: (cur_tile(i), 0))
    kv_s = pl.BlockSpec((sb, N_MEM * CACHE_ROW_GROUP, LANES), lambda i: (cur_tile(i), 0, 0))
    cast_specs = []
    for a in to_cast:
        rb = next(r for r in range(BF16_SUBLANES, a.shape[0] + 1, BF16_SUBLANES)
                  if a.shape[0] % r == 0 and a.shape[0] // r <= n_tiles)
        cast_specs.append(pl.BlockSpec((rb, a.shape[1]), lambda i, last=a.shape[0] // rb - 1: (jnp.minimum(i, last), 0)))
    x1, ym, c_new, n_new, m_new, s_new, o_s, *cast = pl.pallas_call(
        functools.partial(_ffn_mix_kernel, tiles_per_batch=tiles_per_batch, n_cast=len(to_cast)),
        grid=(n_tiles + 1,),
        in_specs=[cur(D_MODEL), _const_spec((1, D_MODEL)), _const_spec((D_MODEL, D_FF)),
                  _const_spec((D_MODEL, D_FF)), _const_spec((D_FF, D_MODEL)), _const_spec((1, D_MODEL)),
                  _const_spec((D_MODEL, ZQ_W)), _const_spec((D_MODEL, ZG_W)), _const_spec((1, SMALL_W)),
                  _const_spec((SMALL_W, GLA_QK_W)), _const_spec((1, GLA_QK_W)), _const_spec((1, ML_HEADS * ML_DV)),
                  _const_spec((1, GLA_V_W)), _const_spec((GLA_SUB, LANES, LANES)), qo_s, kv_s, kv_s] + cast_specs,
        out_specs=([cur(D_MODEL), pl.BlockSpec((tm, D_MODEL), lambda i: (prev_tile(i), 0))] + state_specs + [qo_s]
                   + cast_specs),
        out_shape=[jax.ShapeDtypeStruct((n_tiles * tm, D_MODEL), F32),
                   jax.ShapeDtypeStruct((n_tiles * tm, D_MODEL), BF16),
                   jax.ShapeDtypeStruct((bsz, ML_HEADS, ML_DK, ML_DV), F32),
                   jax.ShapeDtypeStruct((bsz, ML_HEADS, ML_DK), F32),
                   jax.ShapeDtypeStruct((bsz, 1, ML_HEADS), F32),
                   jax.ShapeDtypeStruct((bsz, GLA_HEADS, GLA_DK, GLA_DV), F32),
                   jax.ShapeDtypeStruct((bs * ts, D_MODEL), F32)] + [jax.ShapeDtypeStruct(a.shape, BF16) for a in to_cast],
        scratch_shapes=[pltpu.VMEM((tm, ZQ_W), BF16), pltpu.VMEM((tm, ZG_W), F32),
                        pltpu.VMEM((ML_HEADS, ML_DK, ML_DV), F32), pltpu.VMEM((SUBLANES, LANES), F32),
                        pltpu.VMEM((SUBLANES, LANES), F32), pltpu.VMEM((GLA_HEADS, GLA_DK, GLA_DV), F32)],
        compiler_params=_params(("arbitrary",)),
        name="ffn_mix",
    )(x.reshape(bsz * t, D_MODEL), w["ffn1_g"], w["ffn1_wg"], w["ffn1_wu"], w["ffn1_wd"], w["mix_g"], w["w_in_q"],
      w["w_in_g"], w["gate_bias"], w["w_a2"], w["b_a"], w["mlstm_out_g"], w["gla_out_g"], w["diag_sum"],
      q_s, _cache_rows_view(k_cache), _cache_rows_view(v_cache), *to_cast)
    return x1, ym, (c_new[None], n_new[None], m_new.reshape(1, bsz, ML_HEADS), s_new[None]), o_s, cast


def _post_mix_kernel(x1_ref, ym_ref, wout_ref, gx_ref, wq_ref, x2_ref, q_ref):
    x2 = x1_ref[...] + _dot(ym_ref[...].astype(BF16), wout_ref[...])
    x2_ref[...] = x2
    hq = _rms(x2, gx_ref[...]).astype(BF16)
    q_ref[...] = _dot(hq, wq_ref[...]).astype(q_ref.dtype)


def _post_mix(x1, ym, w, tm, q_dtype):
    n = x1.shape[0]
    row = lambda: pl.BlockSpec((tm, D_MODEL), lambda i: (i, 0))
    return pl.pallas_call(
        _post_mix_kernel,
        grid=(n // tm,),
        in_specs=[row(), row(), _const_spec((D_MODEL, D_MODEL)), _const_spec((1, D_MODEL)),
                  _const_spec((D_MODEL, D_MODEL))],
        out_specs=[row(), row()],
        out_shape=[jax.ShapeDtypeStruct((n, D_MODEL), F32), jax.ShapeDtypeStruct((n, D_MODEL), q_dtype)],
        compiler_params=_params(("arbitrary",)),
        name="post_mix",
    )(x1, ym, w["w_out"], w["xattn_g"], w["xattn_wq"])


def _softmax(s):
    e = jnp.exp(s - jnp.max(s, axis=-1, keepdims=True))
    return e / jnp.sum(e, axis=-1, keepdims=True)


def _post_fused_kernel(x1_ref, ym_ref, k_ref, v_ref, wout_ref, gx_ref, wq_ref, wo_ref, g2_ref, wg_ref, wu_ref,
                       wd_ref, gf_ref, y_ref):
    x2 = x1_ref[0] + _dot(ym_ref[0].astype(BF16), wout_ref[...])
    q = _dot(_rms(x2, gx_ref[...]).astype(BF16), wq_ref[...]).astype(BF16)
    heads = [slice(h * XA_DH, (h + 1) * XA_DH) for h in range(XA_HEADS)]
    k_full, v_full = _cache_rows_load(k_ref, 0), _cache_rows_load(v_ref, 0)
    scores = [_dot_nt(q[:, hs], k_full[:, hs]) * (XA_DH ** -0.5) for hs in heads]
    probs = [_softmax(s).astype(BF16) for s in scores]
    o = jnp.concatenate([_dot(p, v_full[:, hs]).astype(BF16) for hs, p in zip(heads, probs)], axis=1)
    x3 = x2 + _dot(o, wo_ref[...])
    x4 = _swiglu_residual(x3, g2_ref, wg_ref, wu_ref, wd_ref)
    y_ref[0] = _rms(x4, gf_ref[...])


def _post_fused(x1, ym, k, v, w, tm):
    bsz, t = x1.shape[0], x1.shape[1]
    row = lambda: pl.BlockSpec((1, tm, D_MODEL), lambda b, j: (b, j, 0))
    kv = pl.BlockSpec((1, N_MEM * CACHE_ROW_GROUP, LANES), lambda b, j: (b, 0, 0))
    sq = _const_spec((D_MODEL, D_MODEL))
    vec = _const_spec((1, D_MODEL))
    return pl.pallas_call(
        _post_fused_kernel,
        grid=(bsz, t // tm),
        in_specs=[row(), row(), kv, kv, sq, vec, sq, sq, vec, _const_spec((D_MODEL, D_FF)),
                  _const_spec((D_MODEL, D_FF)), _const_spec((D_FF, D_MODEL)), vec],
        out_specs=row(),
        out_shape=jax.ShapeDtypeStruct((bsz, t, D_MODEL), F32),
        compiler_params=_params(("arbitrary", "arbitrary")),
        name="post_fused",
    )(x1, ym, k, v, w["w_out"], w["xattn_g"], w["xattn_wq"], w["xattn_wo"], w["ffn2_g"], w["ffn2_wg"],
      w["ffn2_wu"], w["ffn2_wd"], w["final_g"])


XA_LANE_TILES = XA_DH // LANES
CACHE_ROW_GROUP = XA_LANE_TILES * XA_HEADS


def _cache_rows_view(x):
    bsz = x.shape[0]
    x = x.reshape(bsz, N_MEM, XA_HEADS, XA_LANE_TILES, LANES)
    return x.transpose(0, 1, 3, 2, 4).reshape(bsz, N_MEM * CACHE_ROW_GROUP, LANES)


def _cache_rows_unview(x):
    bsz = x.shape[0]
    x = x.reshape(bsz, N_MEM, XA_LANE_TILES, XA_HEADS, LANES)
    return x.transpose(0, 1, 3, 2, 4).reshape(bsz, N_MEM, XA_HEADS, XA_DH)


def _cache_rows_store(ref, b, x):
    for h in range(XA_HEADS):
        for lt in range(XA_LANE_TILES):
            lo = h * XA_DH + lt * LANES
            ref[b, pl.ds(lt * XA_HEADS + h, N_MEM, stride=CACHE_ROW_GROUP), :] = x[:, lo:lo + LANES]


def _cache_rows_load(ref, b):
    cols = [ref[b, pl.ds(lt * XA_HEADS + h, N_MEM, stride=CACHE_ROW_GROUP), :]
            for h in range(XA_HEADS) for lt in range(XA_LANE_TILES)]
    return jnp.concatenate(cols, axis=1).astype(BF16)


def _xattn_cache_stages(q_ref, k_ref, v_ref, o_ref):
    bb = k_ref.shape[0]
    tq = q_ref.shape[0] // bb
    lane_head = lax.broadcasted_iota(jnp.int32, (1, D_MODEL), 1) // XA_DH
    qs = [q_ref[b * tq:(b + 1) * tq, :] for b in range(bb)]
    q_bds = [jnp.concatenate([jnp.where(lane_head == h, q, 0.0) for h in range(XA_HEADS)], axis=0).astype(BF16)
             for q in qs]
    k_fulls = [_cache_rows_load(k_ref, b) for b in range(bb)]
    yield
    scores = [_dot_nt(q_bds[b], k_fulls[b]) * (XA_DH ** -0.5) for b in range(bb)]
    yield
    p_all = _softmax(jnp.concatenate(scores, axis=0)).astype(BF16)
    v_fulls = [_cache_rows_load(v_ref, b) for b in range(bb)]
    yield
    rows = XA_HEADS * tq
    o_fulls = [_dot(p_all[b * rows:(b + 1) * rows], v_fulls[b]) for b in range(bb)]
    yield
    for b in range(bb):
        o = jnp.zeros((tq, D_MODEL), F32)
        for h in range(XA_HEADS):
            o = jnp.where(lane_head == h, o_fulls[b][h * tq:(h + 1) * tq], o)
        o_ref[b * tq:(b + 1) * tq, :] = o.astype(o_ref.dtype)


def _ffn_out_kernel(x2_ref, o_ref, wo_ref, g2_ref, wg_ref, wu_ref, wd_ref, gf_ref, y_ref):
    x3 = x2_ref[...] + _dot(o_ref[...].astype(BF16), wo_ref[...])
    x4 = _swiglu_residual(x3, g2_ref, wg_ref, wu_ref, wd_ref)
    y_ref[...] = _rms(x4, gf_ref[...])


def _ffn_out(x2, o, w, tm):
    n = x2.shape[0]
    row = lambda: pl.BlockSpec((tm, D_MODEL), lambda i: (i, 0))
    return pl.pallas_call(
        _ffn_out_kernel,
        grid=(n // tm,),
        in_specs=[row(), row(), _const_spec((D_MODEL, D_MODEL)), _const_spec((1, D_MODEL)),
                  _const_spec((D_MODEL, D_FF)), _const_spec((D_MODEL, D_FF)), _const_spec((D_FF, D_MODEL)),
                  _const_spec((1, D_MODEL))],
        out_specs=row(),
        out_shape=jax.ShapeDtypeStruct((n, D_MODEL), F32),
        compiler_params=_params(("arbitrary",)),
        name="ffn_out",
    )(x2, o, w["xattn_wo"], w["ffn2_g"], w["ffn2_wg"], w["ffn2_wu"], w["ffn2_wd"], w["final_g"])


def _memkv_kernel(m_ref, g_ref, wk_ref, wv_ref, k_ref, v_ref):
    hn = _rms(m_ref[0], g_ref[...]).astype(BF16)
    _cache_rows_store(k_ref, 0, _dot(hn, wk_ref[...]))
    _cache_rows_store(v_ref, 0, _dot(hn, wv_ref[...]))


def _memkv(mem, w):
    bsz = mem.shape[0]
    rows = pl.BlockSpec((1, N_MEM * CACHE_ROW_GROUP, LANES), lambda b: (b, 0, 0))
    return pl.pallas_call(
        _memkv_kernel,
        grid=(bsz,),
        in_specs=[pl.BlockSpec((1, N_MEM, D_MODEL), lambda b: (b, 0, 0)), _const_spec((1, D_MODEL)),
                  _const_spec((D_MODEL, D_MODEL)), _const_spec((D_MODEL, D_MODEL))],
        out_specs=[rows, rows],
        out_shape=[jax.ShapeDtypeStruct((bsz, N_MEM * CACHE_ROW_GROUP, LANES), F32)] * 2,
        compiler_params=_params(("arbitrary",)),
        name="memkv",
    )(mem, w["mem_g"], w["xattn_wk"], w["xattn_wv"])


IN_SIZES = (("mq", ML_HEADS * ML_DK), ("mk", ML_HEADS * ML_DK), ("mv", ML_HEADS * ML_DV), ("mi", ML_HEADS),
            ("mf", ML_HEADS), ("mo", ML_HEADS * ML_DV), ("gq", GLA_QK_W), ("gk", GLA_QK_W), ("gv", GLA_V_W),
            ("ga", GLA_RANK), ("gg", GLA_V_W))
IN_OFFSET = {name: sum(width for _, width in IN_SIZES[:i]) for i, (name, _) in enumerate(IN_SIZES)}
D_IN = sum(width for _, width in IN_SIZES)
IN_MOVES = ((0, ZQ_MQ, IN_OFFSET["mq"], ZQ_GQ - ZQ_MQ), (0, ZQ_GQ, IN_OFFSET["gq"], ZQ_W - ZQ_GQ),
            (1, ZG_MO, IN_OFFSET["mo"], ZG_GG - ZG_MO), (1, ZG_GG, IN_OFFSET["gg"], ZG_SMALL - ZG_GG))
TM_REGROUP = MXU_WIDTH
assert IN_OFFSET["mf"] == IN_OFFSET["mi"] + ML_HEADS
assert all(IN_OFFSET[name] % SUBLANES == 0 for name in ("mq", "mi", "mo", "gq", "ga", "gg"))


def _w_in_regroup_kernel(wt_ref, q_ref, g_ref):
    def put(out_ref, dst, rows_t):
        out_ref[:, dst:dst + LANES] = rows_t.T.astype(BF16)

    for slab, dst, src, width in IN_MOVES:
        for off in range(0, width, LANES):
            put((q_ref, g_ref)[slab], dst + off, wt_ref[src + off:src + off + LANES, :])
    narrow = 2 * ML_HEADS + GLA_RANK
    put(g_ref, ZG_SMALL, jnp.concatenate(
        [wt_ref[IN_OFFSET["mi"]:IN_OFFSET["mi"] + 2 * ML_HEADS, :],
         wt_ref[IN_OFFSET["ga"]:IN_OFFSET["ga"] + GLA_RANK, :],
         jnp.zeros((SMALL_W - narrow, wt_ref.shape[1]), F32)], axis=0))


def _w_in_regroup(w_in):
    assert w_in.shape == (D_MODEL, D_IN)
    rows = lambda width: pl.BlockSpec((TM_REGROUP, width), lambda i: (i, 0))
    return pl.pallas_call(
        _w_in_regroup_kernel,
        grid=(D_MODEL // TM_REGROUP,),
        in_specs=[pl.BlockSpec((D_IN, TM_REGROUP), lambda i: (0, i))],
        out_specs=[rows(ZQ_W), rows(ZG_W)],
        out_shape=[jax.ShapeDtypeStruct((D_MODEL, ZQ_W), BF16), jax.ShapeDtypeStruct((D_MODEL, ZG_W), BF16)],
        compiler_params=_params(("arbitrary",)),
        name="w_in_regroup",
    )(w_in.T)


def _prep_weights(p):
    bf = lambda a: a.astype(BF16)
    row = lambda a: a.reshape(1, -1).astype(F32)
    w_in_q, w_in_g = _w_in_regroup(p["w_in"])
    gate_bias = jnp.concatenate([p["mlstm_b_i"], p["mlstm_b_f"], jnp.zeros((SMALL_W - 2 * ML_HEADS,), F32)])
    w_a2 = jnp.zeros((SMALL_W, GLA_QK_W), F32).at[2 * ML_HEADS:2 * ML_HEADS + GLA_RANK].set(p["gla_w_a2"])
    lane = jnp.arange(LANES)
    same_head = (lane[:, None] // GLA_DK) == (lane[None, :] // GLA_DK)
    diag_sum = jnp.stack([same_head & ((lane[None, :] % GLA_SUB) == j) for j in range(GLA_SUB)]).astype(BF16)
    return dict(
        ffn1_wg=p["ffn1_w_gate"], ffn1_wu=p["ffn1_w_up"], ffn1_wd=p["ffn1_w_down"],
        ffn1_g=row(p["ffn1_norm_g"]), mix_g=row(p["mix_norm_g"]), w_in_q=w_in_q, w_in_g=w_in_g,
        gate_bias=row(gate_bias), w_a2=bf(w_a2), b_a=row(p["gla_b_a"]), mlstm_out_g=row(p["mlstm_out_g"]),
        gla_out_g=row(p["gla_out_g"]), diag_sum=diag_sum,
        w_out=bf(p["w_out"]), xattn_g=row(p["xattn_norm_g"]), xattn_wq=bf(p["xattn_w_q"]),
        mem_g=row(p["mem_norm_g"]), ffn2_g=row(p["ffn2_norm_g"]), final_g=row(p["final_g"]),
        xattn_wo=p["xattn_w_o"], xattn_wk=p["xattn_w_k"], xattn_wv=p["xattn_w_v"], ffn2_wg=p["ffn2_w_gate"],
        ffn2_wu=p["ffn2_w_up"], ffn2_wd=p["ffn2_w_down"])


FFN1_WEIGHTS = ("ffn1_wg", "ffn1_wu", "ffn1_wd")
LATE_WEIGHTS = ("xattn_wo", "xattn_wk", "xattn_wv", "ffn2_wg", "ffn2_wu", "ffn2_wd")


def kernel(x_prompt, x_sample, mem_prompt, cache_mem_k, cache_mem_v, state_mlstm_c, state_mlstm_n, state_mlstm_m, state_gla_s, ffn1_norm_g, ffn1_w_gate, ffn1_w_up, ffn1_w_down, mix_norm_g, w_in, mlstm_b_i, mlstm_b_f, mlstm_out_g, gla_w_a2, gla_b_a, gla_out_g, w_out, xattn_norm_g, mem_norm_g, xattn_w_q, xattn_w_k, xattn_w_v, xattn_w_o, ffn2_norm_g, ffn2_w_gate, ffn2_w_up, ffn2_w_down, final_norm_g):
    assert ffn1_norm_g.shape[0] == 1, "single-layer stack"
    layer = dict(ffn1_norm_g=ffn1_norm_g, ffn1_w_gate=ffn1_w_gate, ffn1_w_up=ffn1_w_up, ffn1_w_down=ffn1_w_down,
                 mix_norm_g=mix_norm_g, w_in=w_in, mlstm_b_i=mlstm_b_i, mlstm_b_f=mlstm_b_f,
                 mlstm_out_g=mlstm_out_g, gla_w_a2=gla_w_a2, gla_b_a=gla_b_a, gla_out_g=gla_out_g, w_out=w_out,
                 xattn_norm_g=xattn_norm_g, mem_norm_g=mem_norm_g, xattn_w_q=xattn_w_q, xattn_w_k=xattn_w_k,
                 xattn_w_v=xattn_w_v, xattn_w_o=xattn_w_o, ffn2_norm_g=ffn2_norm_g, ffn2_w_gate=ffn2_w_gate,
                 ffn2_w_up=ffn2_w_up, ffn2_w_down=ffn2_w_down)
    p = {name: arr[0] for name, arr in layer.items()}
    p["final_g"] = final_norm_g
    w = _prep_weights(p)

    bp, tp, _ = x_prompt.shape
    bs, ts, _ = x_sample.shape

    state = (state_mlstm_c[0], state_mlstm_n[0], state_mlstm_m[0].reshape(bs, 1, ML_HEADS), state_gla_s[0])
    x1_s, zq_s, zg_s, *ffn1 = _ffn_in_stream(x_sample.reshape(bs * ts, D_MODEL), w, F32)
    w.update(zip(FFN1_WEIGHTS, ffn1))
    ym_s, c_s, n_s, m_s, s_s = _mixer_state(zq_s, zg_s, w, state, ts, SAMPLE_MIXER_BATCHES)
    states_s = (c_s[None], n_s[None], m_s.reshape(1, bs, ML_HEADS), s_s[None])
    x2_s, q_s = _post_mix(x1_s, ym_s, w, TM_FFN_IN, F32)

    x1_p, ym_p, states_p, o_s, late = _ffn_mix(x_prompt, w, TM_FFN_IN, q_s, cache_mem_k[0], cache_mem_v[0],
                                               [w[name] for name in LATE_WEIGHTS])
    w.update(zip(LATE_WEIGHTS, late))
    mem_k_p, mem_v_p = _memkv(mem_prompt, w)
    y_p = _post_fused(x1_p.reshape(bp, tp, D_MODEL), ym_p.reshape(bp, tp, D_MODEL), mem_k_p, mem_v_p, w, TM_POST)
    y_s = _ffn_out(x2_s, o_s, w, TM_FFN_IN).reshape(bs, ts, D_MODEL)

    return (y_p, y_s, _cache_rows_unview(mem_k_p)[None], _cache_rows_unview(mem_v_p)[None]) + states_p + states_s
```

```python
import functools

import jax
import jax.numpy as jnp
from jax import lax
from jax.experimental import pallas as pl
from jax.experimental.pallas import tpu as pltpu

F32 = jnp.float32
BF16 = jnp.bfloat16

D_MODEL = 1024
D_FF = 2816
ML_HEADS = 4
ML_DK = 128
ML_DV = 128
GLA_HEADS = 4
GLA_DK = 64
GLA_DV = 128
GLA_RANK = 16
GLA_TAU = 16.0
N_MEM = 256
XA_HEADS = 4
XA_DH = D_MODEL // XA_HEADS
EPS = 1e-6
CHUNK = 64
LOG2_E = 1.4426950408889634
LN_2 = 0.6931471805599453
LANES = 128
SUBLANES = 8
BF16_SUBLANES = 2 * SUBLANES

ZQ_MQ, ZQ_MK, ZQ_MV = 0, 512, 1024
ZQ_GQ, ZQ_GK, ZQ_GV = 1536, 1792, 2048
ZQ_W = 2560
ZG_MO, ZG_GG, ZG_SMALL = 0, 512, 1024
ZG_W = 1152
SMALL_W = LANES
GLA_QK_W = GLA_HEADS * GLA_DK
GLA_V_W = GLA_HEADS * GLA_DV
GLA_PAIRS = GLA_HEADS // 2
GLA_SUB = SUBLANES

MXU_WIDTH = 256
FF_CHUNKS = ((0, 6 * MXU_WIDTH), (6 * MXU_WIDTH, D_FF))
VMEM_LIMIT_BYTES = 56 * 1024 * 1024

TM_FFN_IN = MXU_WIDTH
TM_POST = 2 * MXU_WIDTH
SAMPLE_MIXER_BATCHES = 16


def _rms(x, g):
    return x * lax.rsqrt(jnp.mean(x * x, axis=-1, keepdims=True) + EPS) * g


def _log_sigmoid(x):
    return jnp.minimum(x, 0.0) - jnp.log1p(jnp.exp(-jnp.abs(x)))


def _dot(a, b):
    return jnp.dot(a, b, preferred_element_type=F32)


def _dot_nt(a, b):
    return lax.dot_general(a, b, (((1,), (1,)), ((), ())), preferred_element_type=F32)


def _dot_f32(a, b):
    return jnp.dot(a, b, precision=lax.Precision.HIGHEST, preferred_element_type=F32)


def _swiglu_residual(x, g_ref, wg_ref, wu_ref, wd_ref):
    h = _rms(x, g_ref[...]).astype(BF16)
    acts = []
    for lo, hi in FF_CHUNKS:
        g = _dot(h, wg_ref[:, lo:hi])
        u = _dot(h, wu_ref[:, lo:hi])
        acts.append(((g * jax.nn.sigmoid(g)) * u).astype(BF16))
    acc = jnp.zeros_like(x)
    for (lo, hi), a in zip(FF_CHUNKS, acts):
        acc = acc + _dot(a, wd_ref[lo:hi, :])
    return x + 0.5 * acc


def _const_spec(shape):
    nd = len(shape)
    return pl.BlockSpec(shape, lambda *_: (0,) * nd, pipeline_mode=pl.Buffered(1))


def _params(sem):
    return pltpu.CompilerParams(dimension_semantics=sem, vmem_limit_bytes=VMEM_LIMIT_BYTES)


ZQ_BLOCKS = 4


def _ffn_in_stream_kernel(x_ref, g1_ref, wg_ref, wu_ref, wd_ref, gm_ref, wq_ref, wgt_ref,
                          x1_ref, zq_ref, zg_ref, wg_out, wu_out, wd_out, h_s, acc_s, *, n_ff):
    j = pl.program_id(0)

    @pl.when(j == 0)
    def _norm():
        h_s[...] = _rms(x_ref[...], g1_ref[...]).astype(BF16)
        acc_s[...] = jnp.zeros_like(acc_s)

    @pl.when(j < n_ff)
    def _ffn_chunk():
        wg, wu, wd = (ref[...].astype(BF16) for ref in (wg_ref, wu_ref, wd_ref))
        wg_out[...] = wg
        wu_out[...] = wu
        wd_out[...] = wd
        h = h_s[...]
        g = _dot(h, wg)
        u = _dot(h, wu)
        acc_s[...] += _dot(((g * jax.nn.sigmoid(g)) * u).astype(BF16), wd)

    @pl.when(j == n_ff)
    def _residual():
        x1 = x_ref[...] + 0.5 * acc_s[...]
        x1_ref[...] = x1
        h_s[...] = _rms(x1, gm_ref[...]).astype(BF16)

    @pl.when((j >= n_ff) & (j < n_ff + ZQ_BLOCKS))
    def _zq_block():
        zq_ref[...] = _dot(h_s[...], wq_ref[...]).astype(zq_ref.dtype)

    @pl.when(j == n_ff + ZQ_BLOCKS)
    def _zg():
        zg_ref[...] = _dot(h_s[...], wgt_ref[...])


def _ffn_in_stream(x, w, zq_dtype):
    n = x.shape[0]
    n_ff = D_FF // MXU_WIDTH
    zq_blk = ZQ_W // ZQ_BLOCKS
    ff = lambda j: jnp.minimum(j, n_ff - 1)
    zqb = lambda j: jnp.clip(j - n_ff, 0, ZQ_BLOCKS - 1)
    ff_cols = pl.BlockSpec((D_MODEL, MXU_WIDTH), lambda j: (0, ff(j)))
    ff_rows = pl.BlockSpec((MXU_WIDTH, D_MODEL), lambda j: (ff(j), 0))
    resident = lambda width: pl.BlockSpec((n, width), lambda j: (0, 0))
    return pl.pallas_call(
        functools.partial(_ffn_in_stream_kernel, n_ff=n_ff),
        grid=(n_ff + ZQ_BLOCKS + 1,),
        in_specs=[_const_spec((n, D_MODEL)), _const_spec((1, D_MODEL)), ff_cols, ff_cols, ff_rows,
                  _const_spec((1, D_MODEL)), pl.BlockSpec((D_MODEL, zq_blk), lambda j: (0, zqb(j))),
                  _const_spec((D_MODEL, ZG_W))],
        out_specs=[resident(D_MODEL), pl.BlockSpec((n, zq_blk), lambda j: (0, zqb(j))), resident(ZG_W),
                   ff_cols, ff_cols, ff_rows],
        out_shape=[jax.ShapeDtypeStruct((n, D_MODEL), F32), jax.ShapeDtypeStruct((n, ZQ_W), zq_dtype),
                   jax.ShapeDtypeStruct((n, ZG_W), F32), jax.ShapeDtypeStruct((D_MODEL, D_FF), BF16),
                   jax.ShapeDtypeStruct((D_MODEL, D_FF), BF16), jax.ShapeDtypeStruct((D_FF, D_MODEL), BF16)],
        scratch_shapes=[pltpu.VMEM((n, D_MODEL), BF16), pltpu.VMEM((n, D_MODEL), F32)],
        compiler_params=_params(("arbitrary",)),
        name="ffn_in",
    )(x, w["ffn1_g"], w["ffn1_wg"], w["ffn1_wu"], w["ffn1_wd"], w["mix_g"], w["w_in_q"], w["w_in_g"])


def _mixer_stages(zqs, zgs, states, consts, ee_ref, *, L, t_real, chain):
    bias, wa2, ba, gml, ggl = consts
    nchunk = len(zqs)
    groups = [(b, h) for b in range(nchunk) for h in range(ML_HEADS)]
    pairs = [(b, p) for b in range(nchunk) for p in range(GLA_PAIRS)]
    padded = t_real < L
    valid = lax.broadcasted_iota(jnp.int32, (L, 1), 0) < t_real
    rr = lax.broadcasted_iota(jnp.int32, (L, L), 0)
    cc = lax.broadcasted_iota(jnp.int32, (L, L), 1)
    tril = cc <= rr
    c = GLA_SUB
    nb = L // c
    lane = lax.broadcasted_iota(jnp.int32, (1, LANES), 1)
    lane_blk = (lane % GLA_DK) // c
    lane_head = lane // GLA_DK
    v_lane_head = lax.broadcasted_iota(jnp.int32, (1, 2 * GLA_DV), 1) // GLA_DV
    row_blk = (lax.broadcasted_iota(jnp.int32, (2 * L, 1), 0) % L) // c
    t_in = lax.broadcasted_iota(jnp.int32, (1, c, 1), 1)
    new_states = [dict(c=[None] * ML_HEADS, n=[None] * ML_HEADS, m=[None] * ML_HEADS, s=[None] * GLA_HEADS)
                  for _ in range(nchunk)]
    prev = lambda b: new_states[b - 1] if chain and b > 0 else states[0 if chain else b]
    known = lambda b: not chain or b == 0

    carried = {}

    def carried_mlstm(b, h):
        st = prev(b)
        d = ml[(b, h)]
        carried[(b, h)] = _dot(d["qb"], st["c"][h].astype(BF16))

    def carried_gla(b, p):
        s_prev = prev(b)["s"]
        zero_blk = jnp.zeros((GLA_DK, GLA_DV), BF16)
        s_bd = jnp.concatenate(
            [jnp.concatenate([s_prev[2 * p].astype(BF16), zero_blk], axis=1),
             jnp.concatenate([zero_blk, s_prev[2 * p + 1].astype(BF16)], axis=1)], axis=0)
        carried[(b, "gla", p)] = _dot(gl[(b, p)]["q_dec"], s_bd)

    smalls, sms, lfs, b_cols, b_rows, sm_ts, las, bcs = [], [], [], [], [], [], [], []
    for b in range(nchunk):
        small = zgs[b][:, ZG_SMALL:ZG_SMALL + SMALL_W]
        sm = small + bias
        lf = _log_sigmoid(sm) * LOG2_E
        sm = sm * LOG2_E
        if padded:
            sm = jnp.where(valid, sm, -jnp.inf)
            lf = jnp.where(valid, lf, 0.0)
        smalls.append(small)
        sms.append(sm)
        lfs.append(lf)

    long_chunk = L > SUBLANES
    col_w = 1 if long_chunk else LANES
    tril_f, triu_f = tril.astype(F32), (rr <= cc).astype(F32)

    def cumsum_rows(x):
        if long_chunk:
            return _dot_f32(tril_f, x)
        acc = x[0:1]
        rows = [acc]
        for r in range(1, L):
            acc = acc + x[r:r + 1]
            rows.append(acc)
        return jnp.concatenate(rows, axis=0)

    lf_ts = [lf.T[0:SUBLANES] for lf in lfs] if long_chunk else None
    yield
    for b in range(nchunk):
        b_cols.append(cumsum_rows(lfs[b]))
        b_rows.append(_dot_f32(lf_ts[b], triu_f) if long_chunk else b_cols[b].T[0:SUBLANES])
        sm_ts.append(sms[b].T)
        la = _log_sigmoid(_dot(smalls[b].astype(BF16), wa2) + ba) * (LOG2_E / GLA_TAU)
        las.append(jnp.where(valid, la, 0.0) if padded else la)

    ml = {}
    for g in groups:
        b, h = g
        zq = zqs[b]
        qf = zq[:, ZQ_MQ + h * ML_DK:ZQ_MQ + (h + 1) * ML_DK].astype(F32)
        kf = zq[:, ZQ_MK + h * ML_DK:ZQ_MK + (h + 1) * ML_DK].astype(F32) * (ML_DK ** -0.5)
        vf = zq[:, ZQ_MV + h * ML_DV:ZQ_MV + (h + 1) * ML_DV].astype(F32)
        if padded:
            kf = jnp.where(valid, kf, 0.0)
            vf = jnp.where(valid, vf, 0.0)
        ml[g] = dict(qf=qf, qb=qf.astype(BF16), kf=kf, kb=kf.astype(BF16), vb=vf.astype(BF16))
    yield
    for b in range(nchunk):
        bcs.append(cumsum_rows(las[b]))
    for g in groups:
        d = ml[g]
        d["qk"] = _dot_nt(d["qb"], d["kb"])
        if known(g[0]):
            carried_mlstm(*g)

    yield
    for g in groups:
        b, h = g
        d = ml[g]
        i_col = jnp.broadcast_to(sms[b][:, h:h + 1], (L, col_w))
        b_col = jnp.broadcast_to(b_cols[b][:, ML_HEADS + h:ML_HEADS + h + 1], (L, col_w))
        b_row = b_rows[b][ML_HEADS + h:ML_HEADS + h + 1, :]
        i_row = sm_ts[b][h:h + 1, :]
        a_col = b_col + prev(b)["m"][h] * LOG2_E
        dm = jnp.where(tril, b_col[:, :L] - (b_row - i_row), -jnp.inf)
        mt = jnp.maximum(a_col, jnp.max(dm, axis=1, keepdims=True))
        w_inter = jnp.exp2(a_col - mt)
        s = d["qk"] * jnp.exp2(dm - mt[:, :L])
        kw = d["kf"] * jnp.exp2((b_col[L - 1:L] - mt[L - 1:L]) - (b_col - i_col))
        d.update(mt=mt, w_inter=w_inter, s=s, kw=kw, kw_t=kw.T.astype(BF16))
        new_states[b]["m"][h] = mt[L - 1:L, 0:1] * LN_2

    gl = {}
    for b in range(nchunk):
        zq = zqs[b]
        gq = zq[:, ZQ_GQ:ZQ_GQ + GLA_QK_W].astype(F32) * (GLA_DK ** -0.5)
        gk = zq[:, ZQ_GK:ZQ_GK + GLA_QK_W].astype(F32)
        gv = zq[:, ZQ_GV:ZQ_GV + GLA_V_W].astype(F32)
        if padded:
            gk = jnp.where(valid, gk, 0.0)
            gv = jnp.where(valid, gv, 0.0)
        stack = lambda x: jnp.concatenate([x[:, :LANES], x[:, LANES:]], axis=0)
        q2, k2, b2 = stack(gq), stack(gk), stack(bcs[b])
        q3 = q2.reshape(2 * nb, c, LANES)
        k3 = k2.reshape(2 * nb, c, LANES)
        b3 = b2.reshape(2 * nb, c, LANES)
        pair_terms = []
        for j in range(min(c, t_real)):
            decay = jnp.exp2(jnp.where(t_in >= j, b3 - b3[:, j:j + 1, :], -jnp.inf))
            pair_terms.append((q3 * k3[:, j:j + 1, :] * decay).reshape(2 * L, LANES).astype(BF16))
        kt2 = (k3 * jnp.exp2(b3[:, c - 1:c, :] - b3)).reshape(2 * L, LANES) if nb > 1 else None
        gl[b] = dict(gv=gv, q2=q2, k2=k2, b2=b2, pair_terms=pair_terms, kt2=kt2)
    yield
    for b in range(nchunk):
        acc = jnp.zeros((2 * L, LANES), F32)
        for j, pair_j in enumerate(gl[b]["pair_terms"]):
            acc = acc + _dot(pair_j, ee_ref[j])
        gl[b]["a_diag"] = jnp.where(lane_blk == row_blk, acc, 0.0)

    for bp in pairs:
        b, p = bp
        d = gl[b]
        rows_p = slice(p * L, (p + 1) * L)
        q_p, k_p, b_p = d["q2"][rows_p], d["k2"][rows_p], d["b2"][rows_p]
        if long_chunk:
            decay_col = jnp.exp2(b_p[L - SUBLANES:L].T[:, SUBLANES - 1:SUBLANES])
        else:
            decay_col = jnp.exp2(jnp.broadcast_to(b_p[L - 1:L], (LANES, LANES)).T)
        e = dict(q_dec=(q_p * jnp.exp2(b_p)).astype(BF16),
                 kh_t=(k_p * jnp.exp2(b_p[L - 1:L] - b_p)).T.astype(BF16),
                 decay_col=decay_col)
        if nb > 1:
            kt_p = d["kt2"][rows_p].astype(BF16)
            k_bd = jnp.concatenate([jnp.where(lane_head == hh, kt_p, 0.0) for hh in range(2)], axis=0)
            slabs, offs = [], []
            off = 0
            for j in range(nb - 1):
                lo = (j + 1) * c
                slabs.append(q_p[lo:] * jnp.exp2(b_p[lo:] - b_p[lo - 1:lo]))
                offs.append(off)
                off += L - lo
            e.update(q_var=jnp.concatenate(slabs, axis=0).astype(BF16), k_bd=k_bd, offs=offs)
        gl[bp] = e
    yield
    for bp in pairs:
        e = gl[bp]
        if nb > 1:
            e["r"] = _dot_nt(e["q_var"], e["k_bd"])
        if known(bp[0]):
            carried_gla(*bp)

    for g in groups:
        d = ml[g]
        d["sv"] = _dot(d["s"].astype(BF16), d["vb"])
        d["c_upd"] = _dot(d["kw_t"], d["vb"])
        d["den"] = jnp.sum(d["s"], axis=1, keepdims=True)
        if known(g[0]):
            d["qn"] = jnp.sum(d["qf"] * prev(g[0])["n"][g[1]], axis=1, keepdims=True)
    yield
    for bp in pairs:
        b, p = bp
        d, e = gl[b], gl[bp]
        a_p = d["a_diag"][p * L:(p + 1) * L]
        if nb > 1:
            blocks = []
            for i in range(nb):
                blk = a_p[i * c:(i + 1) * c]
                for j in range(i):
                    lo_r = e["offs"][j] + (i - j - 1) * c
                    blk = jnp.where(lane_blk == j, e["r"][lo_r:lo_r + c], blk)
                blocks.append(blk)
            a_p = jnp.concatenate(blocks, axis=0)
        v_f = d["gv"][:, p * 2 * GLA_DV:(p + 1) * 2 * GLA_DV]
        v_p = v_f.astype(BF16)
        if L < GLA_DK:
            v_rows = []
            for hh in range(2):
                v_rows += [jnp.where(v_lane_head == hh, v_f, 0.0), jnp.zeros((GLA_DK - L, 2 * GLA_DV), F32)]
            v_bd = jnp.concatenate(v_rows, axis=0).astype(BF16)
        else:
            v_bd = jnp.concatenate([jnp.where(v_lane_head == hh, v_p, 0.0) for hh in range(2)], axis=0)
        e["s_upd"] = [_dot(e["kh_t"][hh * GLA_DK:(hh + 1) * GLA_DK], v_p[:, hh * GLA_DV:(hh + 1) * GLA_DV])
                      for hh in range(2)]
        e.update(scores=a_p.astype(BF16), v_bd=v_bd)
    yield
    for bp in pairs:
        e = gl[bp]
        e["o_intra"] = _dot(e["scores"], e["v_bd"])

    yield
    ys = [[None] * (ML_HEADS + GLA_HEADS) for _ in range(nchunk)]

    def finish_mlstm(b):
        st = prev(b)
        for h in range(ML_HEADS):
            d = ml[(b, h)]
            mt, w_inter = d["mt"], d["w_inter"]
            qn = d["qn"] if known(b) else jnp.sum(d["qf"] * st["n"][h], axis=1, keepdims=True)
            den = d["den"] + w_inter * qn
            hh = (d["sv"] + w_inter * carried[(b, h)]) / jnp.maximum(jnp.abs(den), jnp.exp2(-mt))
            i_last = w_inter[L - 1:L]
            new_states[b]["c"][h] = i_last * st["c"][h] + d["c_upd"]
            new_states[b]["n"][h] = i_last * st["n"][h] + jnp.sum(d["kw"], axis=0, keepdims=True)
            hs = slice(h * ML_DV, (h + 1) * ML_DV)
            yn = hh * lax.rsqrt(jnp.mean(hh * hh, axis=-1, keepdims=True) + EPS) * gml[:, hs]
            ys[b][h] = jax.nn.sigmoid(zgs[b][:, ZG_MO + h * ML_DV:ZG_MO + (h + 1) * ML_DV]) * yn

    def finish_gla(b):
        st = prev(b)
        for p in range(GLA_PAIRS):
            e = gl[(b, p)]
            o = carried[(b, "gla", p)] + e["o_intra"]
            for hh in range(2):
                h = 2 * p + hh
                ds = slice(hh * GLA_DK, (hh + 1) * GLA_DK)
                vs = slice(hh * GLA_DV, (hh + 1) * GLA_DV)
                new_states[b]["s"][h] = e["decay_col"][ds] * st["s"][h] + e["s_upd"][hh]
                oh = o[:, vs]
                yn = oh * lax.rsqrt(jnp.mean(oh * oh, axis=-1, keepdims=True) + EPS) * ggl[:, h * GLA_DV:(h + 1) * GLA_DV]
                gg = zgs[b][:, ZG_GG + h * GLA_DV:ZG_GG + (h + 1) * GLA_DV]
                ys[b][ML_HEADS + h] = (gg * jax.nn.sigmoid(gg)) * yn

    if chain:
        for b in range(nchunk):
            if not known(b):
                for h in range(ML_HEADS):
                    carried_mlstm(b, h)
                for p in range(GLA_PAIRS):
                    carried_gla(b, p)
            finish_mlstm(b)
            finish_gla(b)
    else:
        for b in range(nchunk):
            finish_mlstm(b)
        for b in range(nchunk):
            finish_gla(b)
    return [jnp.concatenate(y, axis=1) for y in ys], new_states


def _drain(gen):
    try:
        while True:
            next(gen)
    except StopIteration as done:
        return done.value


def _mixer_compute(*args, **kwargs):
    return _drain(_mixer_stages(*args, **kwargs))


def _mixer_state_kernel(zq_ref, zg_ref, bias_ref, wa2_ref, ba_ref, gml_ref, ggl_ref, ee_ref, c0_ref, n0_ref, m0_ref,
                        s0_ref, y_ref, co_ref, no_ref, mo_ref, so_ref, *, L, t_real, nblk):
    consts = (bias_ref[...], wa2_ref[...], ba_ref[...], gml_ref[...], ggl_ref[...])
    lane = lax.broadcasted_iota(jnp.int32, (1, LANES), 1)

    def chunk_rows(ref, b):
        rows = ref[b * t_real:(b + 1) * t_real, :]
        return jnp.concatenate([rows, jnp.zeros((L - t_real, rows.shape[1]), rows.dtype)], axis=0)

    states = []
    for b in range(nblk):
        n_all = n0_ref[b]
        m_all = m0_ref[b]
        states.append(dict(c=[c0_ref[b, h] for h in range(ML_HEADS)],
                           n=[n_all[h:h + 1, :] for h in range(ML_HEADS)],
                           m=[m_all[:, h:h + 1] for h in range(ML_HEADS)],
                           s=[s0_ref[b, h] for h in range(GLA_HEADS)]))
    ys, new_states = _mixer_compute([chunk_rows(zq_ref, b) for b in range(nblk)],
                                    [chunk_rows(zg_ref, b) for b in range(nblk)], states, consts, ee_ref, L=L,
                                    t_real=t_real, chain=False)
    for b in range(nblk):
        st = new_states[b]
        y_ref[b * t_real:(b + 1) * t_real, :] = ys[b][0:t_real].astype(y_ref.dtype)
        for h in range(ML_HEADS):
            co_ref[b, h] = st["c"][h]
            so_ref[b, h] = st["s"][h]
        no_ref[b] = jnp.concatenate(st["n"], axis=0)
        m_row = jnp.zeros((1, LANES), F32)
        for h in range(ML_HEADS):
            m_row = jnp.where(lane == h, st["m"][h], m_row)
        mo_ref[b] = m_row[:, 0:ML_HEADS]


def _mixer_state(zq, zg, w, state, t_real, nblk):
    bsz = zq.shape[0] // t_real
    L = -(-t_real // SUBLANES) * SUBLANES
    rows = lambda width: pl.BlockSpec((nblk * t_real, width), lambda b: (b, 0))
    per_b = lambda *tail: pl.BlockSpec((nblk,) + tail, lambda b: (b,) + (0,) * len(tail))
    state_specs = [per_b(ML_HEADS, ML_DK, ML_DV), per_b(ML_HEADS, ML_DK), per_b(1, ML_HEADS),
                   per_b(GLA_HEADS, GLA_DK, GLA_DV)]
    consts = [_const_spec((1, SMALL_W)), _const_spec((SMALL_W, GLA_QK_W)), _const_spec((1, GLA_QK_W)),
              _const_spec((1, ML_HEADS * ML_DV)), _const_spec((1, GLA_V_W)), _const_spec((GLA_SUB, LANES, LANES))]
    return pl.pallas_call(
        functools.partial(_mixer_state_kernel, L=L, t_real=t_real, nblk=nblk),
        grid=(bsz // nblk,),
        in_specs=[rows(ZQ_W), rows(ZG_W)] + consts + state_specs,
        out_specs=[rows(D_MODEL)] + state_specs,
        out_shape=[jax.ShapeDtypeStruct((bsz * t_real, D_MODEL), F32),
                   jax.ShapeDtypeStruct((bsz, ML_HEADS, ML_DK, ML_DV), F32),
                   jax.ShapeDtypeStruct((bsz, ML_HEADS, ML_DK), F32),
                   jax.ShapeDtypeStruct((bsz, 1, ML_HEADS), F32),
                   jax.ShapeDtypeStruct((bsz, GLA_HEADS, GLA_DK, GLA_DV), F32)],
        compiler_params=_params(("arbitrary",)),
        name="mixer_state",
    )(zq, zg, w["gate_bias"], w["w_a2"], w["b_a"], w["mlstm_out_g"], w["gla_out_g"], w["diag_sum"], *state)


def _ffn_in_pieces(x, g1_ref, wg_ref, wu_ref, wd_ref, gm_ref, wq_ref, wgt_ref, x1_ref, zq_out, zg_out):
    h = _rms(x, g1_ref[...]).astype(BF16)
    acts = []
    for lo in range(0, D_FF, MXU_WIDTH):
        g = _dot(h, wg_ref[:, lo:lo + MXU_WIDTH])
        yield
        u = _dot(h, wu_ref[:, lo:lo + MXU_WIDTH])
        yield
        acts.append(((g * jax.nn.sigmoid(g)) * u).astype(BF16))
    a = jnp.concatenate(acts, axis=1)
    down = []
    for lo in range(0, D_MODEL, MXU_WIDTH):
        down.append(_dot(a, wd_ref[:, lo:lo + MXU_WIDTH]))
        yield
    x1 = x + 0.5 * jnp.concatenate(down, axis=1)
    x1_ref[...] = x1
    hm = _rms(x1, gm_ref[...]).astype(BF16)
    for lo in range(0, ZQ_W, MXU_WIDTH):
        zq_out[:, lo:lo + MXU_WIDTH] = _dot(hm, wq_ref[:, lo:lo + MXU_WIDTH]).astype(zq_out.dtype)
        yield
    for lo in range(0, ZG_W, MXU_WIDTH):
        hi = min(lo + MXU_WIDTH, ZG_W)
        zg_out[:, lo:hi] = _dot(hm, wgt_ref[:, lo:hi])
        yield


ATTN_SLOTS = (0, 8, 16, 26, 30)
FFN_PIECES_PER_MIXER_STAGE = 1
FFN_START_SLOT = 2


def _ffn_mix_kernel(*refs, tiles_per_batch, n_cast):
    it = iter(refs)
    (x_ref, g1_ref, wg_ref, wu_ref, wd_ref, gm_ref, wq_ref, wgt_ref, bias_ref, wa2_ref, ba_ref, gml_ref, ggl_ref,
     ee_ref, qs_ref, ks_ref, vs_ref) = (next(it) for _ in range(17))
    cast_src = [next(it) for _ in range(n_cast)]
    x1_ref, ym_ref, co_ref, no_ref, mo_ref, so_ref, os_ref = (next(it) for _ in range(7))
    cast_dst = [next(it) for _ in range(n_cast)]
    zq_s, zg_s, c_s, n_s, m_s, s_s = (next(it) for _ in range(6))
    i = pl.program_id(0)

    @pl.when(i == 0)
    def _init():
        for ref in (zq_s, zg_s, c_s, n_s, m_s, s_s):
            ref[...] = jnp.zeros_like(ref)

    for src, dst in zip(cast_src, cast_dst):
        dst[...] = src[...].astype(dst.dtype)

    consts = (bias_ref[...], wa2_ref[...], ba_ref[...], gml_ref[...], ggl_ref[...])
    n_chunks = zq_s.shape[0] // CHUNK
    rows = [slice(k * CHUNK, (k + 1) * CHUNK) for k in range(n_chunks)]
    zqs = [zq_s[r, :] for r in rows]
    zgs = [zg_s[r, :] for r in rows]
    starts_batch = (i - 1) % tiles_per_batch == 0
    carry = lambda v: jnp.where(starts_batch, 0.0, v)
    state = dict(c=[carry(c_s[h]) for h in range(ML_HEADS)], n=[carry(n_s[h:h + 1, :]) for h in range(ML_HEADS)],
                 m=[carry(m_s[h:h + 1, 0:1]) for h in range(ML_HEADS)], s=[carry(s_s[h]) for h in range(GLA_HEADS)])

    def mixers():
        st = state
        for k in range(n_chunks):
            ys, new = yield from _mixer_stages([zqs[k]], [zgs[k]], [st], consts, ee_ref, L=CHUNK, t_real=CHUNK,
                                               chain=True)
            ym_ref[rows[k], :] = ys[0].astype(ym_ref.dtype)
            st = new[0]
            yield
        return st

    ffn = _ffn_in_pieces(x_ref[...], g1_ref, wg_ref, wu_ref, wd_ref, gm_ref, wq_ref, wgt_ref, x1_ref, zq_s, zg_s)
    mix = mixers()
    attn = _xattn_cache_stages(qs_ref, ks_ref, vs_ref, os_ref)
    live = {"ffn": True, "attn": True}

    def advance(name, gen):
        if live[name]:
            try:
                next(gen)
            except StopIteration:
                live[name] = False

    slot = 0
    while True:
        try:
            next(mix)
        except StopIteration as done:
            last = done.value
            break
        for _ in range(FFN_PIECES_PER_MIXER_STAGE if slot >= FFN_START_SLOT else 0):
            advance("ffn", ffn)
        if slot in ATTN_SLOTS:
            advance("attn", attn)
        slot += 1
    for name, gen in (("ffn", ffn), ("attn", attn)):
        if live[name]:
            _drain(gen)

    for h in range(ML_HEADS):
        c_s[h] = last["c"][h]
        n_s[h:h + 1, :] = last["n"][h]
        m_s[h:h + 1, :] = jnp.broadcast_to(last["m"][h], (1, LANES))
        s_s[h] = last["s"][h]

    @pl.when(jnp.logical_and(i >= 1, (i - 1) % tiles_per_batch == tiles_per_batch - 1))
    def _emit_state():
        lane = lax.broadcasted_iota(jnp.int32, (1, LANES), 1)
        for h in range(ML_HEADS):
            co_ref[0, h] = last["c"][h]
            so_ref[0, h] = last["s"][h]
        no_ref[0] = jnp.concatenate(last["n"], axis=0)
        m_row = jnp.zeros((1, LANES), F32)
        for h in range(ML_HEADS):
            m_row = jnp.where(lane == h, last["m"][h], m_row)
        mo_ref[0] = m_row[:, 0:ML_HEADS]


def _ffn_mix(x, w, tm, q_s, k_cache, v_cache, to_cast):
    bsz, t, _ = x.shape
    tiles_per_batch = t // tm
    n_tiles = bsz * tiles_per_batch
    assert t % tm == 0 and tm % CHUNK == 0
    bs = k_cache.shape[0]
    ts = q_s.shape[0] // bs
    assert bs % n_tiles == 0
    sb = bs // n_tiles
    cur_tile = lambda i: jnp.minimum(i, n_tiles - 1)
    prev_tile = lambda i: jnp.maximum(i - 1, 0)
    cur = lambda width: pl.BlockSpec((tm, width), lambda i: (cur_tile(i), 0))
    per_b = lambda *tail: pl.BlockSpec((1,) + tail, lambda i: (prev_tile(i) // tiles_per_batch,) + (0,) * len(tail))
    state_specs = [per_b(ML_HEADS, ML_DK, ML_DV), per_b(ML_HEADS, ML_DK), per_b(1, ML_HEADS),
                   per_b(GLA_HEADS, GLA_DK, GLA_DV)]
    qo_s = pl.BlockSpec((sb * ts, D_MODEL), lambda i: (cur_tile(i), 0))
    kv_s = pl.BlockSpec((sb, N_MEM * CACHE_ROW_GROUP, LANES), lambda i: (cur_tile(i), 0, 0))
    cast_specs = []
    for a in to_cast:
        rb = next(r for r in range(BF16_SUBLANES, a.shape[0] + 1, BF16_SUBLANES)
                  if a.shape[0] % r == 0 and a.shape[0] // r <= n_tiles)
        cast_specs.append(pl.BlockSpec((rb, a.shape[1]), lambda i, last=a.shape[0] // rb - 1: (jnp.minimum(i, last), 0)))
    x1, ym, c_new, n_new, m_new, s_new, o_s, *cast = pl.pallas_call(
        functools.partial(_ffn_mix_kernel, tiles_per_batch=tiles_per_batch, n_cast=len(to_cast)),
        grid=(n_tiles + 1,),
        in_specs=[cur(D_MODEL), _const_spec((1, D_MODEL)), _const_spec((D_MODEL, D_FF)),
                  _const_spec((D_MODEL, D_FF)), _const_spec((D_FF, D_MODEL)), _const_spec((1, D_MODEL)),
                  _const_spec((D_MODEL, ZQ_W)), _const_spec((D_MODEL, ZG_W)), _const_spec((1, SMALL_W)),
                  _const_spec((SMALL_W, GLA_QK_W)), _const_spec((1, GLA_QK_W)), _const_spec((1, ML_HEADS * ML_DV)),
                  _const_spec((1, GLA_V_W)), _const_spec((GLA_SUB, LANES, LANES)), qo_s, kv_s, kv_s] + cast_specs,
        out_specs=([cur(D_MODEL), pl.BlockSpec((tm, D_MODEL), lambda i: (prev_tile(i), 0))] + state_specs + [qo_s]
                   + cast_specs),
        out_shape=[jax.ShapeDtypeStruct((n_tiles * tm, D_MODEL), F32),
                   jax.ShapeDtypeStruct((n_tiles * tm, D_MODEL), BF16),
                   jax.ShapeDtypeStruct((bsz, ML_HEADS, ML_DK, ML_DV), F32),
                   jax.ShapeDtypeStruct((bsz, ML_HEADS, ML_DK), F32),
                   jax.ShapeDtypeStruct((bsz, 1, ML_HEADS), F32),
                   jax.ShapeDtypeStruct((bsz, GLA_HEADS, GLA_DK, GLA_DV), F32),
                   jax.ShapeDtypeStruct((bs * ts, D_MODEL), F32)] + [jax.ShapeDtypeStruct(a.shape, BF16) for a in to_cast],
        scratch_shapes=[pltpu.VMEM((tm, ZQ_W), BF16), pltpu.VMEM((tm, ZG_W), F32),
                        pltpu.VMEM((ML_HEADS, ML_DK, ML_DV), F32), pltpu.VMEM((SUBLANES, LANES), F32),
                        pltpu.VMEM((SUBLANES, LANES), F32), pltpu.VMEM((GLA_HEADS, GLA_DK, GLA_DV), F32)],
        compiler_params=_params(("arbitrary",)),
        name="ffn_mix",
    )(x.reshape(bsz * t, D_MODEL), w["ffn1_g"], w["ffn1_wg"], w["ffn1_wu"], w["ffn1_wd"], w["mix_g"], w["w_in_q"],
      w["w_in_g"], w["gate_bias"], w["w_a2"], w["b_a"], w["mlstm_out_g"], w["gla_out_g"], w["diag_sum"],
      q_s, _cache_rows_view(k_cache), _cache_rows_view(v_cache), *to_cast)
    return x1, ym, (c_new[None], n_new[None], m_new.reshape(1, bsz, ML_HEADS), s_new[None]), o_s, cast


def _post_mix_kernel(x1_ref, ym_ref, wout_ref, gx_ref, wq_ref, x2_ref, q_ref):
    x2 = x1_ref[...] + _dot(ym_ref[...].astype(BF16), wout_ref[...])
    x2_ref[...] = x2
    hq = _rms(x2, gx_ref[...]).astype(BF16)
    q_ref[...] = _dot(hq, wq_ref[...]).astype(q_ref.dtype)


def _post_mix(x1, ym, w, tm, q_dtype):
    n = x1.shape[0]
    row = lambda: pl.BlockSpec((tm, D_MODEL), lambda i: (i, 0))
    return pl.pallas_call(
        _post_mix_kernel,
        grid=(n // tm,),
        in_specs=[row(), row(), _const_spec((D_MODEL, D_MODEL)), _const_spec((1, D_MODEL)),
                  _const_spec((D_MODEL, D_MODEL))],
        out_specs=[row(), row()],
        out_shape=[jax.ShapeDtypeStruct((n, D_MODEL), F32), jax.ShapeDtypeStruct((n, D_MODEL), q_dtype)],
        compiler_params=_params(("arbitrary",)),
        name="post_mix",
    )(x1, ym, w["w_out"], w["xattn_g"], w["xattn_wq"])


def _softmax(s):
    e = jnp.exp(s - jnp.max(s, axis=-1, keepdims=True))
    return e / jnp.sum(e, axis=-1, keepdims=True)


def _post_fused_kernel(x1_ref, ym_ref, k_ref, v_ref, wout_ref, gx_ref, wq_ref, wo_ref, g2_ref, wg_ref, wu_ref,
                       wd_ref, gf_ref, y_ref):
    x2 = x1_ref[0] + _dot(ym_ref[0].astype(BF16), wout_ref[...])
    q = _dot(_rms(x2, gx_ref[...]).astype(BF16), wq_ref[...]).astype(BF16)
    heads = [slice(h * XA_DH, (h + 1) * XA_DH) for h in range(XA_HEADS)]
    k_full, v_full = _cache_rows_load(k_ref, 0), _cache_rows_load(v_ref, 0)
    scores = [_dot_nt(q[:, hs], k_full[:, hs]) * (XA_DH ** -0.5) for hs in heads]
    probs = [_softmax(s).astype(BF16) for s in scores]
    o = jnp.concatenate([_dot(p, v_full[:, hs]).astype(BF16) for hs, p in zip(heads, probs)], axis=1)
    x3 = x2 + _dot(o, wo_ref[...])
    x4 = _swiglu_residual(x3, g2_ref, wg_ref, wu_ref, wd_ref)
    y_ref[0] = _rms(x4, gf_ref[...])


def _post_fused(x1, ym, k, v, w, tm):
    bsz, t = x1.shape[0], x1.shape[1]
    row = lambda: pl.BlockSpec((1, tm, D_MODEL), lambda b, j: (b, j, 0))
    kv = pl.BlockSpec((1, N_MEM * CACHE_ROW_GROUP, LANES), lambda b, j: (b, 0, 0))
    sq = _const_spec((D_MODEL, D_MODEL))
    vec = _const_spec((1, D_MODEL))
    return pl.pallas_call(
        _post_fused_kernel,
        grid=(bsz, t // tm),
        in_specs=[row(), row(), kv, kv, sq, vec, sq, sq, vec, _const_spec((D_MODEL, D_FF)),
                  _const_spec((D_MODEL, D_FF)), _const_spec((D_FF, D_MODEL)), vec],
        out_specs=row(),
        out_shape=jax.ShapeDtypeStruct((bsz, t, D_MODEL), F32),
        compiler_params=_params(("arbitrary", "arbitrary")),
        name="post_fused",
    )(x1, ym, k, v, w["w_out"], w["xattn_g"], w["xattn_wq"], w["xattn_wo"], w["ffn2_g"], w["ffn2_wg"],
      w["ffn2_wu"], w["ffn2_wd"], w["final_g"])


XA_LANE_TILES = XA_DH // LANES
CACHE_ROW_GROUP = XA_LANE_TILES * XA_HEADS


def _cache_rows_view(x):
    bsz = x.shape[0]
    x = x.reshape(bsz, N_MEM, XA_HEADS, XA_LANE_TILES, LANES)
    return x.transpose(0, 1, 3, 2, 4).reshape(bsz, N_MEM * CACHE_ROW_GROUP, LANES)


def _cache_rows_unview(x):
    bsz = x.shape[0]
    x = x.reshape(bsz, N_MEM, XA_LANE_TILES, XA_HEADS, LANES)
    return x.transpose(0, 1, 3, 2, 4).reshape(bsz, N_MEM, XA_HEADS, XA_DH)


def _cache_rows_store(ref, b, x):
    for h in range(XA_HEADS):
        for lt in range(XA_LANE_TILES):
            lo = h * XA_DH + lt * LANES
            ref[b, pl.ds(lt * XA_HEADS + h, N_MEM, stride=CACHE_ROW_GROUP), :] = x[:, lo:lo + LANES]


def _cache_rows_load(ref, b):
    cols = [ref[b, pl.ds(lt * XA_HEADS + h, N_MEM, stride=CACHE_ROW_GROUP), :]
            for h in range(XA_HEADS) for lt in range(XA_LANE_TILES)]
    return jnp.concatenate(cols, axis=1).astype(BF16)


def _xattn_cache_stages(q_ref, k_ref, v_ref, o_ref):
    bb = k_ref.shape[0]
    tq = q_ref.shape[0] // bb
    lane_head = lax.broadcasted_iota(jnp.int32, (1, D_MODEL), 1) // XA_DH
    qs = [q_ref[b * tq:(b + 1) * tq, :] for b in range(bb)]
    q_bds = [jnp.concatenate([jnp.where(lane_head == h, q, 0.0) for h in range(XA_HEADS)], axis=0).astype(BF16)
             for q in qs]
    k_fulls = [_cache_rows_load(k_ref, b) for b in range(bb)]
    yield
    scores = [_dot_nt(q_bds[b], k_fulls[b]) * (XA_DH ** -0.5) for b in range(bb)]
    yield
    p_all = _softmax(jnp.concatenate(scores, axis=0)).astype(BF16)
    v_fulls = [_cache_rows_load(v_ref, b) for b in range(bb)]
    yield
    rows = XA_HEADS * tq
    o_fulls = [_dot(p_all[b * rows:(b + 1) * rows], v_fulls[b]) for b in range(bb)]
    yield
    for b in range(bb):
        o = jnp.zeros((tq, D_MODEL), F32)
        for h in range(XA_HEADS):
            o = jnp.where(lane_head == h, o_fulls[b][h * tq:(h + 1) * tq], o)
        o_ref[b * tq:(b + 1) * tq, :] = o.astype(o_ref.dtype)


def _ffn_out_kernel(x2_ref, o_ref, wo_ref, g2_ref, wg_ref, wu_ref, wd_ref, gf_ref, y_ref):
    x3 = x2_ref[...] + _dot(o_ref[...].astype(BF16), wo_ref[...])
    x4 = _swiglu_residual(x3, g2_ref, wg_ref, wu_ref, wd_ref)
    y_ref[...] = _rms(x4, gf_ref[...])


def _ffn_out(x2, o, w, tm):
    n = x2.shape[0]
    row = lambda: pl.BlockSpec((tm, D_MODEL), lambda i: (i, 0))
    return pl.pallas_call(
        _ffn_out_kernel,
        grid=(n // tm,),
        in_specs=[row(), row(), _const_spec((D_MODEL, D_MODEL)), _const_spec((1, D_MODEL)),
                  _const_spec((D_MODEL, D_FF)), _const_spec((D_MODEL, D_FF)), _const_spec((D_FF, D_MODEL)),
                  _const_spec((1, D_MODEL))],
        out_specs=row(),
        out_shape=jax.ShapeDtypeStruct((n, D_MODEL), F32),
        compiler_params=_params(("arbitrary",)),
        name="ffn_out",
    )(x2, o, w["xattn_wo"], w["ffn2_g"], w["ffn2_wg"], w["ffn2_wu"], w["ffn2_wd"], w["final_g"])


def _memkv_kernel(m_ref, g_ref, wk_ref, wv_ref, k_ref, v_ref):
    hn = _rms(m_ref[0], g_ref[...]).astype(BF16)
    _cache_rows_store(k_ref, 0, _dot(hn, wk_ref[...]))
    _cache_rows_store(v_ref, 0, _dot(hn, wv_ref[...]))


def _memkv(mem, w):
    bsz = mem.shape[0]
    rows = pl.BlockSpec((1, N_MEM * CACHE_ROW_GROUP, LANES), lambda b: (b, 0, 0))
    return pl.pallas_call(
        _memkv_kernel,
        grid=(bsz,),
        in_specs=[pl.BlockSpec((1, N_MEM, D_MODEL), lambda b: (b, 0, 0)), _const_spec((1, D_MODEL)),
                  _const_spec((D_MODEL, D_MODEL)), _const_spec((D_MODEL, D_MODEL))],
        out_specs=[rows, rows],
        out_shape=[jax.ShapeDtypeStruct((bsz, N_MEM * CACHE_ROW_GROUP, LANES), F32)] * 2,
        compiler_params=_params(("arbitrary",)),
        name="memkv",
    )(mem, w["mem_g"], w["xattn_wk"], w["xattn_wv"])


IN_SIZES = (("mq", ML_HEADS * ML_DK), ("mk", ML_HEADS * ML_DK), ("mv", ML_HEADS * ML_DV), ("mi", ML_HEADS),
            ("mf", ML_HEADS), ("mo", ML_HEADS * ML_DV), ("gq", GLA_QK_W), ("gk", GLA_QK_W), ("gv", GLA_V_W),
            ("ga", GLA_RANK), ("gg", GLA_V_W))
IN_OFFSET = {name: sum(width for _, width in IN_SIZES[:i]) for i, (name, _) in enumerate(IN_SIZES)}
D_IN = sum(width for _, width in IN_SIZES)
IN_MOVES = ((0, ZQ_MQ, IN_OFFSET["mq"], ZQ_GQ - ZQ_MQ), (0, ZQ_GQ, IN_OFFSET["gq"], ZQ_W - ZQ_GQ),
            (1, ZG_MO, IN_OFFSET["mo"], ZG_GG - ZG_MO), (1, ZG_GG, IN_OFFSET["gg"], ZG_SMALL - ZG_GG))
TM_REGROUP = MXU_WIDTH
assert IN_OFFSET["mf"] == IN_OFFSET["mi"] + ML_HEADS
assert all(IN_OFFSET[name] % SUBLANES == 0 for name in ("mq", "mi", "mo", "gq", "ga", "gg"))


def _w_in_regroup_kernel(wt_ref, q_ref, g_ref):
    def put(out_ref, dst, rows_t):
        out_ref[:, dst:dst + LANES] = rows_t.T.astype(BF16)

    for slab, dst, src, width in IN_MOVES:
        for off in range(0, width, LANES):
            put((q_ref, g_ref)[slab], dst + off, wt_ref[src + off:src + off + LANES, :])
    narrow = 2 * ML_HEADS + GLA_RANK
    put(g_ref, ZG_SMALL, jnp.concatenate(
        [wt_ref[IN_OFFSET["mi"]:IN_OFFSET["mi"] + 2 * ML_HEADS, :],
         wt_ref[IN_OFFSET["ga"]:IN_OFFSET["ga"] + GLA_RANK, :],
         jnp.zeros((SMALL_W - narrow, wt_ref.shape[1]), F32)], axis=0))


def _w_in_regroup(w_in):
    assert w_in.shape == (D_MODEL, D_IN)
    rows = lambda width: pl.BlockSpec((TM_REGROUP, width), lambda i: (i, 0))
    return pl.pallas_call(
        _w_in_regroup_kernel,
        grid=(D_MODEL // TM_REGROUP,),
        in_specs=[pl.BlockSpec((D_IN, TM_REGROUP), lambda i: (0, i))],
        out_specs=[rows(ZQ_W), rows(ZG_W)],
        out_shape=[jax.ShapeDtypeStruct((D_MODEL, ZQ_W), BF16), jax.ShapeDtypeStruct((D_MODEL, ZG_W), BF16)],
        compiler_params=_params(("arbitrary",)),
        name="w_in_regroup",
    )(w_in.T)


def _prep_weights(p):
    bf = lambda a: a.astype(BF16)
    row = lambda a: a.reshape(1, -1).astype(F32)
    w_in_q, w_in_g = _w_in_regroup(p["w_in"])
    gate_bias = jnp.concatenate([p["mlstm_b_i"], p["mlstm_b_f"], jnp.zeros((SMALL_W - 2 * ML_HEADS,), F32)])
    w_a2 = jnp.zeros((SMALL_W, GLA_QK_W), F32).at[2 * ML_HEADS:2 * ML_HEADS + GLA_RANK].set(p["gla_w_a2"])
    lane = jnp.arange(LANES)
    same_head = (lane[:, None] // GLA_DK) == (lane[None, :] // GLA_DK)
    diag_sum = jnp.stack([same_head & ((lane[None, :] % GLA_SUB) == j) for j in range(GLA_SUB)]).astype(BF16)
    return dict(
        ffn1_wg=p["ffn1_w_gate"], ffn1_wu=p["ffn1_w_up"], ffn1_wd=p["ffn1_w_down"],
        ffn1_g=row(p["ffn1_norm_g"]), mix_g=row(p["mix_norm_g"]), w_in_q=w_in_q, w_in_g=w_in_g,
        gate_bias=row(gate_bias), w_a2=bf(w_a2), b_a=row(p["gla_b_a"]), mlstm_out_g=row(p["mlstm_out_g"]),
        gla_out_g=row(p["gla_out_g"]), diag_sum=diag_sum,
        w_out=bf(p["w_out"]), xattn_g=row(p["xattn_norm_g"]), xattn_wq=bf(p["xattn_w_q"]),
        mem_g=row(p["mem_norm_g"]), ffn2_g=row(p["ffn2_norm_g"]), final_g=row(p["final_g"]),
        xattn_wo=p["xattn_w_o"], xattn_wk=p["xattn_w_k"], xattn_wv=p["xattn_w_v"], ffn2_wg=p["ffn2_w_gate"],
        ffn2_wu=p["ffn2_w_up"], ffn2_wd=p["ffn2_w_down"])


FFN1_WEIGHTS = ("ffn1_wg", "ffn1_wu", "ffn1_wd")
LATE_WEIGHTS = ("xattn_wo", "xattn_wk", "xattn_wv", "ffn2_wg", "ffn2_wu", "ffn2_wd")


def kernel(x_prompt, x_sample, mem_prompt, cache_mem_k, cache_mem_v, state_mlstm_c, state_mlstm_n, state_mlstm_m, state_gla_s, ffn1_norm_g, ffn1_w_gate, ffn1_w_up, ffn1_w_down, mix_norm_g, w_in, mlstm_b_i, mlstm_b_f, mlstm_out_g, gla_w_a2, gla_b_a, gla_out_g, w_out, xattn_norm_g, mem_norm_g, xattn_w_q, xattn_w_k, xattn_w_v, xattn_w_o, ffn2_norm_g, ffn2_w_gate, ffn2_w_up, ffn2_w_down, final_norm_g):
    assert ffn1_norm_g.shape[0] == 1, "single-layer stack"
    layer = dict(ffn1_norm_g=ffn1_norm_g, ffn1_w_gate=ffn1_w_gate, ffn1_w_up=ffn1_w_up, ffn1_w_down=ffn1_w_down,
                 mix_norm_g=mix_norm_g, w_in=w_in, mlstm_b_i=mlstm_b_i, mlstm_b_f=mlstm_b_f,
                 mlstm_out_g=mlstm_out_g, gla_w_a2=gla_w_a2, gla_b_a=gla_b_a, gla_out_g=gla_out_g, w_out=w_out,
                 xattn_norm_g=xattn_norm_g, mem_norm_g=mem_norm_g, xattn_w_q=xattn_w_q, xattn_w_k=xattn_w_k,
                 xattn_w_v=xattn_w_v, xattn_w_o=xattn_w_o, ffn2_norm_g=ffn2_norm_g, ffn2_w_gate=ffn2_w_gate,
                 ffn2_w_up=ffn2_w_up, ffn2_w_down=ffn2_w_down)
    p = {name: arr[0] for name, arr in layer.items()}
    p["final_g"] = final_norm_g
    w = _prep_weights(p)

    bp, tp, _ = x_prompt.shape
    bs, ts, _ = x_sample.shape

    state = (state_mlstm_c[0], state_mlstm_n[0], state_mlstm_m[0].reshape(bs, 1, ML_HEADS), state_gla_s[0])
    x1_s, zq_s, zg_s, *ffn1 = _ffn_in_stream(x_sample.reshape(bs * ts, D_MODEL), w, F32)
    w.update(zip(FFN1_WEIGHTS, ffn1))
    ym_s, c_s, n_s, m_s, s_s = _mixer_state(zq_s, zg_s, w, state, ts, SAMPLE_MIXER_BATCHES)
    states_s = (c_s[None], n_s[None], m_s.reshape(1, bs, ML_HEADS), s_s[None])
    x2_s, q_s = _post_mix(x1_s, ym_s, w, TM_FFN_IN, F32)

    x1_p, ym_p, states_p, o_s, late = _ffn_mix(x_prompt, w, TM_FFN_IN, q_s, cache_mem_k[0], cache_mem_v[0],
                                               [w[name] for name in LATE_WEIGHTS])
    w.update(zip(LATE_WEIGHTS, late))
    mem_k_p, mem_v_p = _memkv(mem_prompt, w)
    y_p = _post_fused(x1_p.reshape(bp, tp, D_MODEL), ym_p.reshape(bp, tp, D_MODEL), mem_k_p, mem_v_p, w, TM_POST)
    y_s = _ffn_out(x2_s, o_s, w, TM_FFN_IN).reshape(bs, ts, D_MODEL)

    return (y_p, y_s, _cache_rows_unview(mem_k_p)[None], _cache_rows_unview(mem_v_p)[None]) + states_p + states_s
```

```python
import functools

import jax
import jax.numpy as jnp
from jax import lax
from jax.experimental import pallas as pl
from jax.experimental.pallas import tpu as pltpu

F32 = jnp.float32
BF16 = jnp.bfloat16

D_MODEL = 1024
D_FF = 2816
ML_HEADS = 4
ML_DK = 128
ML_DV = 128
GLA_HEADS = 4
GLA_DK = 64
GLA_DV = 128
GLA_RANK = 16
GLA_TAU = 16.0
N_MEM = 256
XA_HEADS = 4
XA_DH = D_MODEL // XA_HEADS
EPS = 1e-6
CHUNK = 64
LOG2_E = 1.4426950408889634
LN_2 = 0.6931471805599453
LANES = 128
SUBLANES = 8
BF16_SUBLANES = 2 * SUBLANES

ZQ_MQ, ZQ_MK, ZQ_MV = 0, 512, 1024
ZQ_GQ, ZQ_GK, ZQ_GV = 1536, 1792, 2048
ZQ_W = 2560
ZG_MO, ZG_GG, ZG_SMALL = 0, 512, 1024
ZG_W = 1152
SMALL_W = LANES
GLA_QK_W = GLA_HEADS * GLA_DK
GLA_V_W = GLA_HEADS * GLA_DV
GLA_PAIRS = GLA_HEADS // 2
GLA_SUB = SUBLANES

MXU_WIDTH = 256
FF_CHUNKS = ((0, 6 * MXU_WIDTH), (6 * MXU_WIDTH, D_FF))
VMEM_LIMIT_BYTES = 56 * 1024 * 1024

TM_FFN_IN = MXU_WIDTH
TM_POST = 2 * MXU_WIDTH
SAMPLE_MIXER_BATCHES = 16


def _rms(x, g):
    return x * lax.rsqrt(jnp.mean(x * x, axis=-1, keepdims=True) + EPS) * g


def _log_sigmoid(x):
    return jnp.minimum(x, 0.0) - jnp.log1p(jnp.exp(-jnp.abs(x)))


def _dot(a, b):
    return jnp.dot(a, b, preferred_element_type=F32)


def _dot_nt(a, b):
    return lax.dot_general(a, b, (((1,), (1,)), ((), ())), preferred_element_type=F32)


def _dot_f32(a, b):
    return jnp.dot(a, b, precision=lax.Precision.HIGHEST, preferred_element_type=F32)


def _swiglu_residual(x, g_ref, wg_ref, wu_ref, wd_ref):
    h = _rms(x, g_ref[...]).astype(BF16)
    acts = []
    for lo, hi in FF_CHUNKS:
        g = _dot(h, wg_ref[:, lo:hi])
        u = _dot(h, wu_ref[:, lo:hi])
        acts.append(((g * jax.nn.sigmoid(g)) * u).astype(BF16))
    acc = jnp.zeros_like(x)
    for (lo, hi), a in zip(FF_CHUNKS, acts):
        acc = acc + _dot(a, wd_ref[lo:hi, :])
    return x + 0.5 * acc


def _const_spec(shape):
    nd = len(shape)
    return pl.BlockSpec(shape, lambda *_: (0,) * nd, pipeline_mode=pl.Buffered(1))


def _params(sem):
    return pltpu.CompilerParams(dimension_semantics=sem, vmem_limit_bytes=VMEM_LIMIT_BYTES)


ZQ_BLOCKS = 4


def _ffn_in_stream_kernel(x_ref, g1_ref, wg_ref, wu_ref, wd_ref, gm_ref, wq_ref, wgt_ref, sq1_ref, sq2_ref,
                          x1_ref, zq_ref, zg_ref, wg_out, wu_out, wd_out, sq1_out, sq2_out, h_s, acc_s, *, n_ff):
    j = pl.program_id(0)

    @pl.when(j == 0)
    def _norm():
        h_s[...] = _rms(x_ref[...], g1_ref[...]).astype(BF16)
        acc_s[...] = jnp.zeros_like(acc_s)

    @pl.when(j < n_ff)
    def _ffn_chunk():
        wg, wu, wd = (ref[...].astype(BF16) for ref in (wg_ref, wu_ref, wd_ref))
        wg_out[...] = wg
        wu_out[...] = wu
        wd_out[...] = wd
        h = h_s[...]
        g = _dot(h, wg)
        u = _dot(h, wu)
        acc_s[...] += _dot(((g * jax.nn.sigmoid(g)) * u).astype(BF16), wd)

    @pl.when(j == n_ff)
    def _residual():
        x1 = x_ref[...] + 0.5 * acc_s[...]
        x1_ref[...] = x1
        h_s[...] = _rms(x1, gm_ref[...]).astype(BF16)

    @pl.when((j >= n_ff) & (j < n_ff + ZQ_BLOCKS))
    def _zq_block():
        zq_ref[...] = _dot(h_s[...], wq_ref[...]).astype(zq_ref.dtype)
        sq1_out[...] = sq1_ref[...].astype(BF16)
        sq2_out[...] = sq2_ref[...].astype(BF16)

    @pl.when(j == n_ff + ZQ_BLOCKS)
    def _zg():
        zg_ref[...] = _dot(h_s[...], wgt_ref[...])


def _ffn_in_stream(x, w, zq_dtype):
    n = x.shape[0]
    n_ff = D_FF // MXU_WIDTH
    zq_blk = ZQ_W // ZQ_BLOCKS
    ff = lambda j: jnp.minimum(j, n_ff - 1)
    zqb = lambda j: jnp.clip(j - n_ff, 0, ZQ_BLOCKS - 1)
    ff_cols = pl.BlockSpec((D_MODEL, MXU_WIDTH), lambda j: (0, ff(j)))
    ff_rows = pl.BlockSpec((MXU_WIDTH, D_MODEL), lambda j: (ff(j), 0))
    resident = lambda width: pl.BlockSpec((n, width), lambda j: (0, 0))
    sq_rows = pl.BlockSpec((D_MODEL // ZQ_BLOCKS, D_MODEL), lambda j: (zqb(j), 0))
    return pl.pallas_call(
        functools.partial(_ffn_in_stream_kernel, n_ff=n_ff),
        grid=(n_ff + ZQ_BLOCKS + 1,),
        in_specs=[_const_spec((n, D_MODEL)), _const_spec((1, D_MODEL)), ff_cols, ff_cols, ff_rows,
                  _const_spec((1, D_MODEL)), pl.BlockSpec((D_MODEL, zq_blk), lambda j: (0, zqb(j))),
                  _const_spec((D_MODEL, ZG_W)), sq_rows, sq_rows],
        out_specs=[resident(D_MODEL), pl.BlockSpec((n, zq_blk), lambda j: (0, zqb(j))), resident(ZG_W),
                   ff_cols, ff_cols, ff_rows, sq_rows, sq_rows],
        out_shape=[jax.ShapeDtypeStruct((n, D_MODEL), F32), jax.ShapeDtypeStruct((n, ZQ_W), zq_dtype),
                   jax.ShapeDtypeStruct((n, ZG_W), F32), jax.ShapeDtypeStruct((D_MODEL, D_FF), BF16),
                   jax.ShapeDtypeStruct((D_MODEL, D_FF), BF16), jax.ShapeDtypeStruct((D_FF, D_MODEL), BF16),
                   jax.ShapeDtypeStruct((D_MODEL, D_MODEL), BF16), jax.ShapeDtypeStruct((D_MODEL, D_MODEL), BF16)],
        scratch_shapes=[pltpu.VMEM((n, D_MODEL), BF16), pltpu.VMEM((n, D_MODEL), F32)],
        compiler_params=_params(("arbitrary",)),
        name="ffn_in",
    )(x, w["ffn1_g"], w["ffn1_wg"], w["ffn1_wu"], w["ffn1_wd"], w["mix_g"], w["w_in_q"], w["w_in_g"],
      w["w_out"], w["xattn_wq"])


def _mixer_stages(zqs, zgs, states, consts, ee_ref, *, L, t_real, chain):
    bias, wa2, ba, gml, ggl = consts
    nchunk = len(zqs)
    groups = [(b, h) for b in range(nchunk) for h in range(ML_HEADS)]
    pairs = [(b, p) for b in range(nchunk) for p in range(GLA_PAIRS)]
    padded = t_real < L
    valid = lax.broadcasted_iota(jnp.int32, (L, 1), 0) < t_real
    rr = lax.broadcasted_iota(jnp.int32, (L, L), 0)
    cc = lax.broadcasted_iota(jnp.int32, (L, L), 1)
    tril = cc <= rr
    c = GLA_SUB
    nb = L // c
    lane = lax.broadcasted_iota(jnp.int32, (1, LANES), 1)
    lane_blk = (lane % GLA_DK) // c
    lane_head = lane // GLA_DK
    v_lane_head = lax.broadcasted_iota(jnp.int32, (1, 2 * GLA_DV), 1) // GLA_DV
    row_blk = (lax.broadcasted_iota(jnp.int32, (2 * L, 1), 0) % L) // c
    t_in = lax.broadcasted_iota(jnp.int32, (1, c, 1), 1)
    new_states = [dict(c=[None] * ML_HEADS, n=[None] * ML_HEADS, m=[None] * ML_HEADS, s=[None] * GLA_HEADS)
                  for _ in range(nchunk)]
    prev = lambda b: new_states[b - 1] if chain and b > 0 else states[0 if chain else b]
    known = lambda b: not chain or b == 0

    carried = {}

    def carried_mlstm(b, h):
        st = prev(b)
        d = ml[(b, h)]
        carried[(b, h)] = _dot(d["qb"], st["c"][h].astype(BF16))

    def carried_gla(b, p):
        s_prev = prev(b)["s"]
        zero_blk = jnp.zeros((GLA_DK, GLA_DV), BF16)
        s_bd = jnp.concatenate(
            [jnp.concatenate([s_prev[2 * p].astype(BF16), zero_blk], axis=1),
             jnp.concatenate([zero_blk, s_prev[2 * p + 1].astype(BF16)], axis=1)], axis=0)
        carried[(b, "gla", p)] = _dot(gl[(b, p)]["q_dec"], s_bd)

    smalls, sms, lfs, b_cols, b_rows, sm_ts, las, bcs = [], [], [], [], [], [], [], []
    for b in range(nchunk):
        small = zgs[b][:, ZG_SMALL:ZG_SMALL + SMALL_W]
        sm = small + bias
        lf = _log_sigmoid(sm) * LOG2_E
        sm = sm * LOG2_E
        if padded:
            sm = jnp.where(valid, sm, -jnp.inf)
            lf = jnp.where(valid, lf, 0.0)
        smalls.append(small)
        sms.append(sm)
        lfs.append(lf)

    long_chunk = L > SUBLANES
    col_w = 1 if long_chunk else LANES
    tril_f, triu_f = tril.astype(F32), (rr <= cc).astype(F32)

    def cumsum_rows(x):
        if long_chunk:
            return _dot_f32(tril_f, x)
        acc = x[0:1]
        rows = [acc]
        for r in range(1, L):
            acc = acc + x[r:r + 1]
            rows.append(acc)
        return jnp.concatenate(rows, axis=0)

    lf_ts = [lf.T[0:SUBLANES] for lf in lfs] if long_chunk else None
    yield
    for b in range(nchunk):
        b_cols.append(cumsum_rows(lfs[b]))
        b_rows.append(_dot_f32(lf_ts[b], triu_f) if long_chunk else b_cols[b].T[0:SUBLANES])
        sm_ts.append(sms[b].T)
        la = _log_sigmoid(_dot(smalls[b].astype(BF16), wa2) + ba) * (LOG2_E / GLA_TAU)
        las.append(jnp.where(valid, la, 0.0) if padded else la)

    ml = {}
    for g in groups:
        b, h = g
        zq = zqs[b]
        qf = zq[:, ZQ_MQ + h * ML_DK:ZQ_MQ + (h + 1) * ML_DK].astype(F32)
        kf = zq[:, ZQ_MK + h * ML_DK:ZQ_MK + (h + 1) * ML_DK].astype(F32) * (ML_DK ** -0.5)
        vf = zq[:, ZQ_MV + h * ML_DV:ZQ_MV + (h + 1) * ML_DV].astype(F32)
        if padded:
            kf = jnp.where(valid, kf, 0.0)
            vf = jnp.where(valid, vf, 0.0)
        ml[g] = dict(qf=qf, qb=qf.astype(BF16), kf=kf, kb=kf.astype(BF16), vb=vf.astype(BF16))
    yield
    for b in range(nchunk):
        bcs.append(cumsum_rows(las[b]))
    for g in groups:
        d = ml[g]
        d["qk"] = _dot_nt(d["qb"], d["kb"])
        if known(g[0]):
            carried_mlstm(*g)

    yield
    for g in groups:
        b, h = g
        d = ml[g]
        i_col = jnp.broadcast_to(sms[b][:, h:h + 1], (L, col_w))
        b_col = jnp.broadcast_to(b_cols[b][:, ML_HEADS + h:ML_HEADS + h + 1], (L, col_w))
        b_row = b_rows[b][ML_HEADS + h:ML_HEADS + h + 1, :]
        i_row = sm_ts[b][h:h + 1, :]
        a_col = b_col + prev(b)["m"][h] * LOG2_E
        dm = jnp.where(tril, b_col[:, :L] - (b_row - i_row), -jnp.inf)
        mt = jnp.maximum(a_col, jnp.max(dm, axis=1, keepdims=True))
        w_inter = jnp.exp2(a_col - mt)
        s = d["qk"] * jnp.exp2(dm - mt[:, :L])
        kw = d["kf"] * jnp.exp2((b_col[L - 1:L] - mt[L - 1:L]) - (b_col - i_col))
        d.update(mt=mt, w_inter=w_inter, s=s, kw=kw, kw_t=kw.T.astype(BF16))
        new_states[b]["m"][h] = mt[L - 1:L, 0:1] * LN_2

    gl = {}
    for b in range(nchunk):
        zq = zqs[b]
        gq = zq[:, ZQ_GQ:ZQ_GQ + GLA_QK_W].astype(F32) * (GLA_DK ** -0.5)
        gk = zq[:, ZQ_GK:ZQ_GK + GLA_QK_W].astype(F32)
        gv = zq[:, ZQ_GV:ZQ_GV + GLA_V_W].astype(F32)
        if padded:
            gk = jnp.where(valid, gk, 0.0)
            gv = jnp.where(valid, gv, 0.0)
        stack = lambda x: jnp.concatenate([x[:, :LANES], x[:, LANES:]], axis=0)
        q2, k2, b2 = stack(gq), stack(gk), stack(bcs[b])
        q3 = q2.reshape(2 * nb, c, LANES)
        k3 = k2.reshape(2 * nb, c, LANES)
        b3 = b2.reshape(2 * nb, c, LANES)
        pair_terms = []
        for j in range(min(c, t_real)):
            decay = jnp.exp2(jnp.where(t_in >= j, b3 - b3[:, j:j + 1, :], -jnp.inf))
            pair_terms.append((q3 * k3[:, j:j + 1, :] * decay).reshape(2 * L, LANES).astype(BF16))
        kt2 = (k3 * jnp.exp2(b3[:, c - 1:c, :] - b3)).reshape(2 * L, LANES) if nb > 1 else None
        gl[b] = dict(gv=gv, q2=q2, k2=k2, b2=b2, pair_terms=pair_terms, kt2=kt2)
    yield
    for b in range(nchunk):
        acc = jnp.zeros((2 * L, LANES), F32)
        for j, pair_j in enumerate(gl[b]["pair_terms"]):
            acc = acc + _dot(pair_j, ee_ref[j])
        gl[b]["a_diag"] = jnp.where(lane_blk == row_blk, acc, 0.0)

    for bp in pairs:
        b, p = bp
        d = gl[b]
        rows_p = slice(p * L, (p + 1) * L)
        q_p, k_p, b_p = d["q2"][rows_p], d["k2"][rows_p], d["b2"][rows_p]
        if long_chunk:
            decay_col = jnp.exp2(b_p[L - SUBLANES:L].T[:, SUBLANES - 1:SUBLANES])
        else:
            decay_col = jnp.exp2(jnp.broadcast_to(b_p[L - 1:L], (LANES, LANES)).T)
        e = dict(q_dec=(q_p * jnp.exp2(b_p)).astype(BF16),
                 kh_t=(k_p * jnp.exp2(b_p[L - 1:L] - b_p)).T.astype(BF16),
                 decay_col=decay_col)
        if nb > 1:
            kt_p = d["kt2"][rows_p].astype(BF16)
            k_bd = jnp.concatenate([jnp.where(lane_head == hh, kt_p, 0.0) for hh in range(2)], axis=0)
            slabs, offs = [], []
            off = 0
            for j in range(nb - 1):
                lo = (j + 1) * c
                slabs.append(q_p[lo:] * jnp.exp2(b_p[lo:] - b_p[lo - 1:lo]))
                offs.append(off)
                off += L - lo
            e.update(q_var=jnp.concatenate(slabs, axis=0).astype(BF16), k_bd=k_bd, offs=offs)
        gl[bp] = e
    yield
    for bp in pairs:
        e = gl[bp]
        if nb > 1:
            e["r"] = _dot_nt(e["q_var"], e["k_bd"])
        if known(bp[0]):
            carried_gla(*bp)

    for g in groups:
        d = ml[g]
        d["sv"] = _dot(d["s"].astype(BF16), d["vb"])
        d["c_upd"] = _dot(d["kw_t"], d["vb"])
        d["den"] = jnp.sum(d["s"], axis=1, keepdims=True)
        if known(g[0]):
            d["qn"] = jnp.sum(d["qf"] * prev(g[0])["n"][g[1]], axis=1, keepdims=True)
    yield
    for bp in pairs:
        b, p = bp
        d, e = gl[b], gl[bp]
        a_p = d["a_diag"][p * L:(p + 1) * L]
        if nb > 1:
            blocks = []
            for i in range(nb):
                blk = a_p[i * c:(i + 1) * c]
                for j in range(i):
                    lo_r = e["offs"][j] + (i - j - 1) * c
                    blk = jnp.where(lane_blk == j, e["r"][lo_r:lo_r + c], blk)
                blocks.append(blk)
            a_p = jnp.concatenate(blocks, axis=0)
        v_f = d["gv"][:, p * 2 * GLA_DV:(p + 1) * 2 * GLA_DV]
        v_p = v_f.astype(BF16)
        if L < GLA_DK:
            v_rows = []
            for hh in range(2):
                v_rows += [jnp.where(v_lane_head == hh, v_f, 0.0), jnp.zeros((GLA_DK - L, 2 * GLA_DV), F32)]
            v_bd = jnp.concatenate(v_rows, axis=0).astype(BF16)
        else:
            v_bd = jnp.concatenate([jnp.where(v_lane_head == hh, v_p, 0.0) for hh in range(2)], axis=0)
        e["s_upd"] = [_dot(e["kh_t"][hh * GLA_DK:(hh + 1) * GLA_DK], v_p[:, hh * GLA_DV:(hh + 1) * GLA_DV])
                      for hh in range(2)]
        e.update(scores=a_p.astype(BF16), v_bd=v_bd)
    yield
    for bp in pairs:
        e = gl[bp]
        e["o_intra"] = _dot(e["scores"], e["v_bd"])

    yield
    ys = [[None] * (ML_HEADS + GLA_HEADS) for _ in range(nchunk)]

    def finish_mlstm(b):
        st = prev(b)
        for h in range(ML_HEADS):
            d = ml[(b, h)]
            mt, w_inter = d["mt"], d["w_inter"]
            qn = d["qn"] if known(b) else jnp.sum(d["qf"] * st["n"][h], axis=1, keepdims=True)
            den = d["den"] + w_inter * qn
            hh = (d["sv"] + w_inter * carried[(b, h)]) / jnp.maximum(jnp.abs(den), jnp.exp2(-mt))
            i_last = w_inter[L - 1:L]
            new_states[b]["c"][h] = i_last * st["c"][h] + d["c_upd"]
            new_states[b]["n"][h] = i_last * st["n"][h] + jnp.sum(d["kw"], axis=0, keepdims=True)
            hs = slice(h * ML_DV, (h + 1) * ML_DV)
            yn = hh * lax.rsqrt(jnp.mean(hh * hh, axis=-1, keepdims=True) + EPS) * gml[:, hs]
            ys[b][h] = jax.nn.sigmoid(zgs[b][:, ZG_MO + h * ML_DV:ZG_MO + (h + 1) * ML_DV]) * yn

    def finish_gla(b):
        st = prev(b)
        for p in range(GLA_PAIRS):
            e = gl[(b, p)]
            o = carried[(b, "gla", p)] + e["o_intra"]
            for hh in range(2):
                h = 2 * p + hh
                ds = slice(hh * GLA_DK, (hh + 1) * GLA_DK)
                vs = slice(hh * GLA_DV, (hh + 1) * GLA_DV)
                new_states[b]["s"][h] = e["decay_col"][ds] * st["s"][h] + e["s_upd"][hh]
                oh = o[:, vs]
                yn = oh * lax.rsqrt(jnp.mean(oh * oh, axis=-1, keepdims=True) + EPS) * ggl[:, h * GLA_DV:(h + 1) * GLA_DV]
                gg = zgs[b][:, ZG_GG + h * GLA_DV:ZG_GG + (h + 1) * GLA_DV]
                ys[b][ML_HEADS + h] = (gg * jax.nn.sigmoid(gg)) * yn

    if chain:
        for b in range(nchunk):
            if not known(b):
                for h in range(ML_HEADS):
                    carried_mlstm(b, h)
                for p in range(GLA_PAIRS):
                    carried_gla(b, p)
            finish_mlstm(b)
            finish_gla(b)
    else:
        for b in range(nchunk):
            finish_mlstm(b)
        for b in range(nchunk):
            finish_gla(b)
    return [jnp.concatenate(y, axis=1) for y in ys], new_states


def _drain(gen):
    try:
        while True:
            next(gen)
    except StopIteration as done:
        return done.value


def _mixer_compute(*args, **kwargs):
    return _drain(_mixer_stages(*args, **kwargs))


def _mixer_state_kernel(zq_ref, zg_ref, bias_ref, wa2_ref, ba_ref, gml_ref, ggl_ref, ee_ref, c0_ref, n0_ref, m0_ref,
                        s0_ref, y_ref, co_ref, no_ref, mo_ref, so_ref, *, L, t_real, nblk):
    consts = (bias_ref[...], wa2_ref[...], ba_ref[...], gml_ref[...], ggl_ref[...])
    lane = lax.broadcasted_iota(jnp.int32, (1, LANES), 1)

    def chunk_rows(ref, b):
        rows = ref[b * t_real:(b + 1) * t_real, :]
        return jnp.concatenate([rows, jnp.zeros((L - t_real, rows.shape[1]), rows.dtype)], axis=0)

    states = []
    for b in range(nblk):
        n_all = n0_ref[b]
        m_all = m0_ref[b]
        states.append(dict(c=[c0_ref[b, h] for h in range(ML_HEADS)],
                           n=[n_all[h:h + 1, :] for h in range(ML_HEADS)],
                           m=[m_all[:, h:h + 1] for h in range(ML_HEADS)],
                           s=[s0_ref[b, h] for h in range(GLA_HEADS)]))
    ys, new_states = _mixer_compute([chunk_rows(zq_ref, b) for b in range(nblk)],
                                    [chunk_rows(zg_ref, b) for b in range(nblk)], states, consts, ee_ref, L=L,
                                    t_real=t_real, chain=False)
    for b in range(nblk):
        st = new_states[b]
        y_ref[b * t_real:(b + 1) * t_real, :] = ys[b][0:t_real].astype(y_ref.dtype)
        for h in range(ML_HEADS):
            co_ref[b, h] = st["c"][h]
            so_ref[b, h] = st["s"][h]
        no_ref[b] = jnp.concatenate(st["n"], axis=0)
        m_row = jnp.zeros((1, LANES), F32)
        for h in range(ML_HEADS):
            m_row = jnp.where(lane == h, st["m"][h], m_row)
        mo_ref[b] = m_row[:, 0:ML_HEADS]


def _mixer_state(zq, zg, w, state, t_real, nblk):
    bsz = zq.shape[0] // t_real
    L = -(-t_real // SUBLANES) * SUBLANES
    rows = lambda width: pl.BlockSpec((nblk * t_real, width), lambda b: (b, 0))
    per_b = lambda *tail: pl.BlockSpec((nblk,) + tail, lambda b: (b,) + (0,) * len(tail))
    state_specs = [per_b(ML_HEADS, ML_DK, ML_DV), per_b(ML_HEADS, ML_DK), per_b(1, ML_HEADS),
                   per_b(GLA_HEADS, GLA_DK, GLA_DV)]
    consts = [_const_spec((1, SMALL_W)), _const_spec((SMALL_W, GLA_QK_W)), _const_spec((1, GLA_QK_W)),
              _const_spec((1, ML_HEADS * ML_DV)), _const_spec((1, GLA_V_W)), _const_spec((GLA_SUB, LANES, LANES))]
    return pl.pallas_call(
        functools.partial(_mixer_state_kernel, L=L, t_real=t_real, nblk=nblk),
        grid=(bsz // nblk,),
        in_specs=[rows(ZQ_W), rows(ZG_W)] + consts + state_specs,
        out_specs=[rows(D_MODEL)] + state_specs,
        out_shape=[jax.ShapeDtypeStruct((bsz * t_real, D_MODEL), F32),
                   jax.ShapeDtypeStruct((bsz, ML_HEADS, ML_DK, ML_DV), F32),
                   jax.ShapeDtypeStruct((bsz, ML_HEADS, ML_DK), F32),
                   jax.ShapeDtypeStruct((bsz, 1, ML_HEADS), F32),
                   jax.ShapeDtypeStruct((bsz, GLA_HEADS, GLA_DK, GLA_DV), F32)],
        compiler_params=_params(("arbitrary",)),
        name="mixer_state",
    )(zq, zg, w["gate_bias"], w["w_a2"], w["b_a"], w["mlstm_out_g"], w["gla_out_g"], w["diag_sum"], *state)


def _ffn_in_pieces(x, g1_ref, wg_ref, wu_ref, wd_ref, gm_ref, wq_ref, wgt_ref, x1_ref, zq_out, zg_out):
    h = _rms(x, g1_ref[...]).astype(BF16)
    acts = []
    for lo in range(0, D_FF, MXU_WIDTH):
        g = _dot(h, wg_ref[:, lo:lo + MXU_WIDTH])
        yield
        u = _dot(h, wu_ref[:, lo:lo + MXU_WIDTH])
        yield
        acts.append(((g * jax.nn.sigmoid(g)) * u).astype(BF16))
    a = jnp.concatenate(acts, axis=1)
    down = []
    for lo in range(0, D_MODEL, MXU_WIDTH):
        down.append(_dot(a, wd_ref[:, lo:lo + MXU_WIDTH]))
        yield
    x1 = x + 0.5 * jnp.concatenate(down, axis=1)
    x1_ref[...] = x1
    hm = _rms(x1, gm_ref[...]).astype(BF16)
    for lo in range(0, ZQ_W, MXU_WIDTH):
        zq_out[:, lo:lo + MXU_WIDTH] = _dot(hm, wq_ref[:, lo:lo + MXU_WIDTH]).astype(zq_out.dtype)
        yield
    for lo in range(0, ZG_W, MXU_WIDTH):
        hi = min(lo + MXU_WIDTH, ZG_W)
        zg_out[:, lo:hi] = _dot(hm, wgt_ref[:, lo:hi])
        yield


ATTN_STAGE_EVERY = 7
FFN_PIECES_PER_MIXER_STAGE = 1


def _ffn_mix_kernel(*refs, tiles_per_batch, n_cast):
    it = iter(refs)
    (x_ref, g1_ref, wg_ref, wu_ref, wd_ref, gm_ref, wq_ref, wgt_ref, bias_ref, wa2_ref, ba_ref, gml_ref, ggl_ref,
     ee_ref, qs_ref, ks_ref, vs_ref) = (next(it) for _ in range(17))
    cast_src = [next(it) for _ in range(n_cast)]
    x1_ref, ym_ref, co_ref, no_ref, mo_ref, so_ref, os_ref = (next(it) for _ in range(7))
    cast_dst = [next(it) for _ in range(n_cast)]
    zq_s, zg_s, c_s, n_s, m_s, s_s = (next(it) for _ in range(6))
    i = pl.program_id(0)
    for src, dst in zip(cast_src, cast_dst):
        dst[...] = src[...].astype(dst.dtype)

    @pl.when(i == 0)
    def _init():
        for ref in (zq_s, zg_s, c_s, n_s, m_s, s_s):
            ref[...] = jnp.zeros_like(ref)

    consts = (bias_ref[...], wa2_ref[...], ba_ref[...], gml_ref[...], ggl_ref[...])
    n_chunks = zq_s.shape[0] // CHUNK
    rows = [slice(k * CHUNK, (k + 1) * CHUNK) for k in range(n_chunks)]
    zqs = [zq_s[r, :] for r in rows]
    zgs = [zg_s[r, :] for r in rows]
    starts_batch = (i - 1) % tiles_per_batch == 0
    carry = lambda v: jnp.where(starts_batch, 0.0, v)
    state = dict(c=[carry(c_s[h]) for h in range(ML_HEADS)], n=[carry(n_s[h:h + 1, :]) for h in range(ML_HEADS)],
                 m=[carry(m_s[h:h + 1, 0:1]) for h in range(ML_HEADS)], s=[carry(s_s[h]) for h in range(GLA_HEADS)])

    def mixers():
        st = state
        for k in range(n_chunks):
            ys, new = yield from _mixer_stages([zqs[k]], [zgs[k]], [st], consts, ee_ref, L=CHUNK, t_real=CHUNK,
                                               chain=True)
            ym_ref[rows[k], :] = ys[0].astype(ym_ref.dtype)
            st = new[0]
            yield
        return st

    ffn = _ffn_in_pieces(x_ref[...], g1_ref, wg_ref, wu_ref, wd_ref, gm_ref, wq_ref, wgt_ref, x1_ref, zq_s, zg_s)
    mix = mixers()
    attn = _xattn_cache_stages(qs_ref, ks_ref, vs_ref, os_ref)
    live = {"ffn": True, "attn": True}

    def advance(name, gen):
        if live[name]:
            try:
                next(gen)
            except StopIteration:
                live[name] = False

    slot = 0
    while True:
        try:
            next(mix)
        except StopIteration as done:
            last = done.value
            break
        for _ in range(FFN_PIECES_PER_MIXER_STAGE):
            advance("ffn", ffn)
        if slot % ATTN_STAGE_EVERY == ATTN_STAGE_EVERY // 2:
            advance("attn", attn)
        slot += 1
    for name, gen in (("ffn", ffn), ("attn", attn)):
        if live[name]:
            _drain(gen)

    for h in range(ML_HEADS):
        c_s[h] = last["c"][h]
        n_s[h:h + 1, :] = last["n"][h]
        m_s[h:h + 1, :] = jnp.broadcast_to(last["m"][h], (1, LANES))
        s_s[h] = last["s"][h]

    @pl.when(jnp.logical_and(i >= 1, (i - 1) % tiles_per_batch == tiles_per_batch - 1))
    def _emit_state():
        lane = lax.broadcasted_iota(jnp.int32, (1, LANES), 1)
        for h in range(ML_HEADS):
            co_ref[0, h] = last["c"][h]
            so_ref[0, h] = last["s"][h]
        no_ref[0] = jnp.concatenate(last["n"], axis=0)
        m_row = jnp.zeros((1, LANES), F32)
        for h in range(ML_HEADS):
            m_row = jnp.where(lane == h, last["m"][h], m_row)
        mo_ref[0] = m_row[:, 0:ML_HEADS]


def _ffn_mix(x, w, tm, q_s, k_cache, v_cache, to_cast):
    bsz, t, _ = x.shape
    tiles_per_batch = t // tm
    n_tiles = bsz * tiles_per_batch
    assert t % tm == 0 and tm % CHUNK == 0
    bs = k_cache.shape[0]
    ts = q_s.shape[0] // bs
    assert bs % n_tiles == 0
    sb = bs // n_tiles
    cur_tile = lambda i: jnp.minimum(i, n_tiles - 1)
    prev_tile = lambda i: jnp.maximum(i - 1, 0)
    cur = lambda width: pl.BlockSpec((tm, width), lambda i: (cur_tile(i), 0))
    per_b = lambda *tail: pl.BlockSpec((1,) + tail, lambda i: (prev_tile(i) // tiles_per_batch,) + (0,) * len(tail))
    state_specs = [per_b(ML_HEADS, ML_DK, ML_DV), per_b(ML_HEADS, ML_DK), per_b(1, ML_HEADS),
                   per_b(GLA_HEADS, GLA_DK, GLA_DV)]
    qo_s = pl.BlockSpec((sb * ts, D_MODEL), lambda i: (cur_tile(i), 0))
    kv_s = pl.BlockSpec((sb, N_MEM * CACHE_ROW_GROUP, LANES), lambda i: (cur_tile(i), 0, 0))
    cast_specs = []
    for a in to_cast:
        rb = next(r for r in range(BF16_SUBLANES, a.shape[0] + 1, BF16_SUBLANES)
                  if a.shape[0] % r == 0 and a.shape[0] // r <= n_tiles)
        cast_specs.append(pl.BlockSpec((rb, a.shape[1]), lambda i, last=a.shape[0] // rb - 1: (jnp.minimum(i, last), 0)))
    x1, ym, c_new, n_new, m_new, s_new, o_s, *cast = pl.pallas_call(
        functools.partial(_ffn_mix_kernel, tiles_per_batch=tiles_per_batch, n_cast=len(to_cast)),
        grid=(n_tiles + 1,),
        in_specs=[cur(D_MODEL), _const_spec((1, D_MODEL)), _const_spec((D_MODEL, D_FF)),
                  _const_spec((D_MODEL, D_FF)), _const_spec((D_FF, D_MODEL)), _const_spec((1, D_MODEL)),
                  _const_spec((D_MODEL, ZQ_W)), _const_spec((D_MODEL, ZG_W)), _const_spec((1, SMALL_W)),
                  _const_spec((SMALL_W, GLA_QK_W)), _const_spec((1, GLA_QK_W)), _const_spec((1, ML_HEADS * ML_DV)),
                  _const_spec((1, GLA_V_W)), _const_spec((GLA_SUB, LANES, LANES)), qo_s, kv_s, kv_s] + cast_specs,
        out_specs=([cur(D_MODEL), pl.BlockSpec((tm, D_MODEL), lambda i: (prev_tile(i), 0))] + state_specs + [qo_s]
                   + cast_specs),
        out_shape=[jax.ShapeDtypeStruct((n_tiles * tm, D_MODEL), F32),
                   jax.ShapeDtypeStruct((n_tiles * tm, D_MODEL), BF16),
                   jax.ShapeDtypeStruct((bsz, ML_HEADS, ML_DK, ML_DV), F32),
                   jax.ShapeDtypeStruct((bsz, ML_HEADS, ML_DK), F32),
                   jax.ShapeDtypeStruct((bsz, 1, ML_HEADS), F32),
                   jax.ShapeDtypeStruct((bsz, GLA_HEADS, GLA_DK, GLA_DV), F32),
                   jax.ShapeDtypeStruct((bs * ts, D_MODEL), F32)] + [jax.ShapeDtypeStruct(a.shape, BF16) for a in to_cast],
        scratch_shapes=[pltpu.VMEM((tm, ZQ_W), BF16), pltpu.VMEM((tm, ZG_W), F32),
                        pltpu.VMEM((ML_HEADS, ML_DK, ML_DV), F32), pltpu.VMEM((SUBLANES, LANES), F32),
                        pltpu.VMEM((SUBLANES, LANES), F32), pltpu.VMEM((GLA_HEADS, GLA_DK, GLA_DV), F32)],
        compiler_params=_params(("arbitrary",)),
        name="ffn_mix",
    )(x.reshape(bsz * t, D_MODEL), w["ffn1_g"], w["ffn1_wg"], w["ffn1_wu"], w["ffn1_wd"], w["mix_g"], w["w_in_q"],
      w["w_in_g"], w["gate_bias"], w["w_a2"], w["b_a"], w["mlstm_out_g"], w["gla_out_g"], w["diag_sum"],
      q_s, _cache_rows_view(k_cache), _cache_rows_view(v_cache), *to_cast)
    return x1, ym, (c_new[None], n_new[None], m_new.reshape(1, bsz, ML_HEADS), s_new[None]), o_s, cast


def _post_mix_kernel(x1_ref, ym_ref, wout_ref, gx_ref, wq_ref, x2_ref, q_ref):
    x2 = x1_ref[...] + _dot(ym_ref[...].astype(BF16), wout_ref[...])
    x2_ref[...] = x2
    hq = _rms(x2, gx_ref[...]).astype(BF16)
    q_ref[...] = _dot(hq, wq_ref[...]).astype(q_ref.dtype)


def _post_mix(x1, ym, w, tm, q_dtype):
    n = x1.shape[0]
    row = lambda: pl.BlockSpec((tm, D_MODEL), lambda i: (i, 0))
    return pl.pallas_call(
        _post_mix_kernel,
        grid=(n // tm,),
        in_specs=[row(), row(), _const_spec((D_MODEL, D_MODEL)), _const_spec((1, D_MODEL)),
                  _const_spec((D_MODEL, D_MODEL))],
        out_specs=[row(), row()],
        out_shape=[jax.ShapeDtypeStruct((n, D_MODEL), F32), jax.ShapeDtypeStruct((n, D_MODEL), q_dtype)],
        compiler_params=_params(("arbitrary",)),
        name="post_mix",
    )(x1, ym, w["w_out"], w["xattn_g"], w["xattn_wq"])


def _softmax(s):
    e = jnp.exp(s - jnp.max(s, axis=-1, keepdims=True))
    return e / jnp.sum(e, axis=-1, keepdims=True)


def _post_fused_kernel(x1_ref, ym_ref, mem_ref, gm_ref, wk_ref, wv_ref, wout_ref, gx_ref, wq_ref, wo_ref, g2_ref,
                       wg_ref, wu_ref, wd_ref, gf_ref, y_ref, ko_ref, vo_ref, k_s, v_s):
    @pl.when(pl.program_id(1) == 0)
    def _memkv():
        hn = _rms(mem_ref[0], gm_ref[...]).astype(BF16)
        for w_ref, out_ref, s_ref in ((wk_ref, ko_ref, k_s), (wv_ref, vo_ref, v_s)):
            kv = _dot(hn, w_ref[...])
            _cache_rows_store(out_ref, 0, kv)
            s_ref[...] = kv.astype(BF16)

    x2 = x1_ref[0] + _dot(ym_ref[0].astype(BF16), wout_ref[...])
    q = _dot(_rms(x2, gx_ref[...]).astype(BF16), wq_ref[...]).astype(BF16)
    heads = [slice(h * XA_DH, (h + 1) * XA_DH) for h in range(XA_HEADS)]
    k_full, v_full = k_s[...], v_s[...]
    scores = [_dot_nt(q[:, hs], k_full[:, hs]) * (XA_DH ** -0.5) for hs in heads]
    probs = [_softmax(s).astype(BF16) for s in scores]
    o = jnp.concatenate([_dot(p, v_full[:, hs]).astype(BF16) for hs, p in zip(heads, probs)], axis=1)
    x3 = x2 + _dot(o, wo_ref[...])
    x4 = _swiglu_residual(x3, g2_ref, wg_ref, wu_ref, wd_ref)
    y_ref[0] = _rms(x4, gf_ref[...])


def _post_fused(x1, ym, mem, w, tm):
    bsz, t = x1.shape[0], x1.shape[1]
    row = lambda: pl.BlockSpec((1, tm, D_MODEL), lambda b, j: (b, j, 0))
    kv = pl.BlockSpec((1, N_MEM * CACHE_ROW_GROUP, LANES), lambda b, j: (b, 0, 0))
    sq = _const_spec((D_MODEL, D_MODEL))
    vec = _const_spec((1, D_MODEL))
    kv_shape = jax.ShapeDtypeStruct((bsz, N_MEM * CACHE_ROW_GROUP, LANES), F32)
    return pl.pallas_call(
        _post_fused_kernel,
        grid=(bsz, t // tm),
        in_specs=[row(), row(), pl.BlockSpec((1, N_MEM, D_MODEL), lambda b, j: (b, 0, 0)), vec, sq, sq,
                  sq, vec, sq, sq, vec, _const_spec((D_MODEL, D_FF)),
                  _const_spec((D_MODEL, D_FF)), _const_spec((D_FF, D_MODEL)), vec],
        out_specs=[row(), kv, kv],
        out_shape=[jax.ShapeDtypeStruct((bsz, t, D_MODEL), F32), kv_shape, kv_shape],
        scratch_shapes=[pltpu.VMEM((N_MEM, D_MODEL), BF16)] * 2,
        compiler_params=_params(("arbitrary", "arbitrary")),
        name="post_fused",
    )(x1, ym, mem, w["mem_g"], w["xattn_wk"], w["xattn_wv"], w["w_out"], w["xattn_g"], w["xattn_wq"],
      w["xattn_wo"], w["ffn2_g"], w["ffn2_wg"], w["ffn2_wu"], w["ffn2_wd"], w["final_g"])


XA_LANE_TILES = XA_DH // LANES
CACHE_ROW_GROUP = XA_LANE_TILES * XA_HEADS


def _cache_rows_view(x):
    bsz = x.shape[0]
    x = x.reshape(bsz, N_MEM, XA_HEADS, XA_LANE_TILES, LANES)
    return x.transpose(0, 1, 3, 2, 4).reshape(bsz, N_MEM * CACHE_ROW_GROUP, LANES)


def _cache_rows_unview(x):
    bsz = x.shape[0]
    x = x.reshape(bsz, N_MEM, XA_LANE_TILES, XA_HEADS, LANES)
    return x.transpose(0, 1, 3, 2, 4).reshape(bsz, N_MEM, XA_HEADS, XA_DH)


def _cache_rows_store(ref, b, x):
    for h in range(XA_HEADS):
        for lt in range(XA_LANE_TILES):
            lo = h * XA_DH + lt * LANES
            ref[b, pl.ds(lt * XA_HEADS + h, N_MEM, stride=CACHE_ROW_GROUP), :] = x[:, lo:lo + LANES]


def _cache_rows_load(ref, b):
    cols = [ref[b, pl.ds(lt * XA_HEADS + h, N_MEM, stride=CACHE_ROW_GROUP), :]
            for h in range(XA_HEADS) for lt in range(XA_LANE_TILES)]
    return jnp.concatenate(cols, axis=1).astype(BF16)


def _xattn_cache_stages(q_ref, k_ref, v_ref, o_ref):
    bb = k_ref.shape[0]
    tq = q_ref.shape[0] // bb
    lane_head = lax.broadcasted_iota(jnp.int32, (1, D_MODEL), 1) // XA_DH
    qs = [q_ref[b * tq:(b + 1) * tq, :] for b in range(bb)]
    q_bds = [jnp.concatenate([jnp.where(lane_head == h, q, 0.0) for h in range(XA_HEADS)], axis=0).astype(BF16)
             for q in qs]
    k_fulls = [_cache_rows_load(k_ref, b) for b in range(bb)]
    yield
    scores = [_dot_nt(q_bds[b], k_fulls[b]) * (XA_DH ** -0.5) for b in range(bb)]
    yield
    p_all = _softmax(jnp.concatenate(scores, axis=0)).astype(BF16)
    v_fulls = [_cache_rows_load(v_ref, b) for b in range(bb)]
    yield
    rows = XA_HEADS * tq
    o_fulls = [_dot(p_all[b * rows:(b + 1) * rows], v_fulls[b]) for b in range(bb)]
    yield
    for b in range(bb):
        o = jnp.zeros((tq, D_MODEL), F32)
        for h in range(XA_HEADS):
            o = jnp.where(lane_head == h, o_fulls[b][h * tq:(h + 1) * tq], o)
        o_ref[b * tq:(b + 1) * tq, :] = o.astype(o_ref.dtype)


def _ffn_out_stream_kernel(x2_ref, o_ref, wo_ref, g2_ref, wg_ref, wu_ref, wd_ref, gf_ref, y_ref, x3_s, h_s, acc_s):
    j = pl.program_id(0)

    @pl.when(j == 0)
    def _attn_out():
        x3 = x2_ref[...] + _dot(o_ref[...].astype(BF16), wo_ref[...])
        x3_s[...] = x3
        h_s[...] = _rms(x3, g2_ref[...]).astype(BF16)
        acc_s[...] = jnp.zeros_like(acc_s)

    h = h_s[...]
    g = _dot(h, wg_ref[...])
    u = _dot(h, wu_ref[...])
    acc_s[...] += _dot(((g * jax.nn.sigmoid(g)) * u).astype(BF16), wd_ref[...])

    @pl.when(j == pl.num_programs(0) - 1)
    def _final_norm():
        y_ref[...] = _rms(x3_s[...] + 0.5 * acc_s[...], gf_ref[...])


def _ffn_out_stream(x2, o, w):
    n = x2.shape[0]
    ff_cols = pl.BlockSpec((D_MODEL, MXU_WIDTH), lambda j: (0, j))
    ff_rows = pl.BlockSpec((MXU_WIDTH, D_MODEL), lambda j: (j, 0))
    return pl.pallas_call(
        _ffn_out_stream_kernel,
        grid=(D_FF // MXU_WIDTH,),
        in_specs=[_const_spec((n, D_MODEL)), _const_spec((n, D_MODEL)), _const_spec((D_MODEL, D_MODEL)),
                  _const_spec((1, D_MODEL)), ff_cols, ff_cols, ff_rows, _const_spec((1, D_MODEL))],
        out_specs=pl.BlockSpec((n, D_MODEL), lambda j: (0, 0)),
        out_shape=jax.ShapeDtypeStruct((n, D_MODEL), F32),
        scratch_shapes=[pltpu.VMEM((n, D_MODEL), F32), pltpu.VMEM((n, D_MODEL), BF16),
                        pltpu.VMEM((n, D_MODEL), F32)],
        compiler_params=_params(("arbitrary",)),
        name="ffn_out",
    )(x2, o, w["xattn_wo"], w["ffn2_g"], w["ffn2_wg"], w["ffn2_wu"], w["ffn2_wd"], w["final_g"])


IN_SIZES = (("mq", ML_HEADS * ML_DK), ("mk", ML_HEADS * ML_DK), ("mv", ML_HEADS * ML_DV), ("mi", ML_HEADS),
            ("mf", ML_HEADS), ("mo", ML_HEADS * ML_DV), ("gq", GLA_QK_W), ("gk", GLA_QK_W), ("gv", GLA_V_W),
            ("ga", GLA_RANK), ("gg", GLA_V_W))
IN_OFFSET = {name: sum(width for _, width in IN_SIZES[:i]) for i, (name, _) in enumerate(IN_SIZES)}
D_IN = sum(width for _, width in IN_SIZES)
IN_MOVES = ((0, ZQ_MQ, IN_OFFSET["mq"], ZQ_GQ - ZQ_MQ), (0, ZQ_GQ, IN_OFFSET["gq"], ZQ_W - ZQ_GQ),
            (1, ZG_MO, IN_OFFSET["mo"], ZG_GG - ZG_MO), (1, ZG_GG, IN_OFFSET["gg"], ZG_SMALL - ZG_GG))
TM_REGROUP = MXU_WIDTH
assert IN_OFFSET["mf"] == IN_OFFSET["mi"] + ML_HEADS
assert all(IN_OFFSET[name] % SUBLANES == 0 for name in ("mq", "mi", "mo", "gq", "ga", "gg"))


def _w_in_regroup_kernel(wt_ref, q_ref, g_ref):
    def put(out_ref, dst, rows_t):
        out_ref[:, dst:dst + LANES] = rows_t.T.astype(BF16)

    for slab, dst, src, width in IN_MOVES:
        for off in range(0, width, LANES):
            put((q_ref, g_ref)[slab], dst + off, wt_ref[src + off:src + off + LANES, :])
    narrow = 2 * ML_HEADS + GLA_RANK
    put(g_ref, ZG_SMALL, jnp.concatenate(
        [wt_ref[IN_OFFSET["mi"]:IN_OFFSET["mi"] + 2 * ML_HEADS, :],
         wt_ref[IN_OFFSET["ga"]:IN_OFFSET["ga"] + GLA_RANK, :],
         jnp.zeros((SMALL_W - narrow, wt_ref.shape[1]), F32)], axis=0))


def _w_in_regroup(w_in):
    assert w_in.shape == (D_MODEL, D_IN)
    rows = lambda width: pl.BlockSpec((TM_REGROUP, width), lambda i: (i, 0))
    return pl.pallas_call(
        _w_in_regroup_kernel,
        grid=(D_MODEL // TM_REGROUP,),
        in_specs=[pl.BlockSpec((D_IN, TM_REGROUP), lambda i: (0, i))],
        out_specs=[rows(ZQ_W), rows(ZG_W)],
        out_shape=[jax.ShapeDtypeStruct((D_MODEL, ZQ_W), BF16), jax.ShapeDtypeStruct((D_MODEL, ZG_W), BF16)],
        compiler_params=_params(("arbitrary",)),
        name="w_in_regroup",
    )(w_in.T)


def _prep_weights(p):
    bf = lambda a: a.astype(BF16)
    row = lambda a: a.reshape(1, -1).astype(F32)
    w_in_q, w_in_g = _w_in_regroup(p["w_in"])
    gate_bias = jnp.concatenate([p["mlstm_b_i"], p["mlstm_b_f"], jnp.zeros((SMALL_W - 2 * ML_HEADS,), F32)])
    w_a2 = jnp.zeros((SMALL_W, GLA_QK_W), F32).at[2 * ML_HEADS:2 * ML_HEADS + GLA_RANK].set(p["gla_w_a2"])
    lane = jnp.arange(LANES)
    same_head = (lane[:, None] // GLA_DK) == (lane[None, :] // GLA_DK)
    diag_sum = jnp.stack([same_head & ((lane[None, :] % GLA_SUB) == j) for j in range(GLA_SUB)]).astype(BF16)
    return dict(
        ffn1_wg=p["ffn1_w_gate"], ffn1_wu=p["ffn1_w_up"], ffn1_wd=p["ffn1_w_down"],
        ffn1_g=row(p["ffn1_norm_g"]), mix_g=row(p["mix_norm_g"]), w_in_q=w_in_q, w_in_g=w_in_g,
        gate_bias=row(gate_bias), w_a2=bf(w_a2), b_a=row(p["gla_b_a"]), mlstm_out_g=row(p["mlstm_out_g"]),
        gla_out_g=row(p["gla_out_g"]), diag_sum=diag_sum,
        w_out=p["w_out"], xattn_g=row(p["xattn_norm_g"]), xattn_wq=p["xattn_w_q"],
        mem_g=row(p["mem_norm_g"]), ffn2_g=row(p["ffn2_norm_g"]), final_g=row(p["final_g"]),
        xattn_wo=p["xattn_w_o"], xattn_wk=p["xattn_w_k"], xattn_wv=p["xattn_w_v"], ffn2_wg=p["ffn2_w_gate"],
        ffn2_wu=p["ffn2_w_up"], ffn2_wd=p["ffn2_w_down"])


EARLY_WEIGHTS = ("ffn1_wg", "ffn1_wu", "ffn1_wd", "w_out", "xattn_wq")
LATE_WEIGHTS = ("xattn_wo", "xattn_wk", "xattn_wv", "ffn2_wg", "ffn2_wu", "ffn2_wd")


def kernel(x_prompt, x_sample, mem_prompt, cache_mem_k, cache_mem_v, state_mlstm_c, state_mlstm_n, state_mlstm_m, state_gla_s, ffn1_norm_g, ffn1_w_gate, ffn1_w_up, ffn1_w_down, mix_norm_g, w_in, mlstm_b_i, mlstm_b_f, mlstm_out_g, gla_w_a2, gla_b_a, gla_out_g, w_out, xattn_norm_g, mem_norm_g, xattn_w_q, xattn_w_k, xattn_w_v, xattn_w_o, ffn2_norm_g, ffn2_w_gate, ffn2_w_up, ffn2_w_down, final_norm_g):
    assert ffn1_norm_g.shape[0] == 1, "single-layer stack"
    layer = dict(ffn1_norm_g=ffn1_norm_g, ffn1_w_gate=ffn1_w_gate, ffn1_w_up=ffn1_w_up, ffn1_w_down=ffn1_w_down,
                 mix_norm_g=mix_norm_g, w_in=w_in, mlstm_b_i=mlstm_b_i, mlstm_b_f=mlstm_b_f,
                 mlstm_out_g=mlstm_out_g, gla_w_a2=gla_w_a2, gla_b_a=gla_b_a, gla_out_g=gla_out_g, w_out=w_out,
                 xattn_norm_g=xattn_norm_g, mem_norm_g=mem_norm_g, xattn_w_q=xattn_w_q, xattn_w_k=xattn_w_k,
                 xattn_w_v=xattn_w_v, xattn_w_o=xattn_w_o, ffn2_norm_g=ffn2_norm_g, ffn2_w_gate=ffn2_w_gate,
                 ffn2_w_up=ffn2_w_up, ffn2_w_down=ffn2_w_down)
    p = {name: arr[0] for name, arr in layer.items()}
    p["final_g"] = final_norm_g
    w = _prep_weights(p)

    bp, tp, _ = x_prompt.shape
    bs, ts, _ = x_sample.shape

    state = (state_mlstm_c[0], state_mlstm_n[0], state_mlstm_m[0].reshape(bs, 1, ML_HEADS), state_gla_s[0])
    x1_s, zq_s, zg_s, *early = _ffn_in_stream(x_sample.reshape(bs * ts, D_MODEL), w, F32)
    w.update(zip(EARLY_WEIGHTS, early))
    ym_s, c_s, n_s, m_s, s_s = _mixer_state(zq_s, zg_s, w, state, ts, SAMPLE_MIXER_BATCHES)
    states_s = (c_s[None], n_s[None], m_s.reshape(1, bs, ML_HEADS), s_s[None])
    x2_s, q_s = _post_mix(x1_s, ym_s, w, TM_FFN_IN, F32)

    x1_p, ym_p, states_p, o_s, late = _ffn_mix(x_prompt, w, TM_FFN_IN, q_s, cache_mem_k[0], cache_mem_v[0],
                                               [w[name] for name in LATE_WEIGHTS])
    w.update(zip(LATE_WEIGHTS, late))
    y_p, mem_k_p, mem_v_p = _post_fused(x1_p.reshape(bp, tp, D_MODEL), ym_p.reshape(bp, tp, D_MODEL), mem_prompt, w,
                                        TM_POST)
    y_s = _ffn_out_stream(x2_s, o_s, w).reshape(bs, ts, D_MODEL)

    return (y_p, y_s, _cache_rows_unview(mem_k_p)[None], _cache_rows_unview(mem_v_p)[None]) + states_p + states_s
```

```python
import functools

import jax
import jax.numpy as jnp
from jax import lax
from jax.experimental import pallas as pl
from jax.experimental.pallas import tpu as pltpu

F32 = jnp.float32
BF16 = jnp.bfloat16

D_MODEL = 1024
D_FF = 2816
ML_HEADS = 4
ML_DK = 128
ML_DV = 128
GLA_HEADS = 4
GLA_DK = 64
GLA_DV = 128
GLA_RANK = 16
GLA_TAU = 16.0
N_MEM = 256
XA_HEADS = 4
XA_DH = D_MODEL // XA_HEADS
EPS = 1e-6
CHUNK = 64
LOG2_E = 1.4426950408889634
LN_2 = 0.6931471805599453
LANES = 128
SUBLANES = 8
BF16_SUBLANES = 2 * SUBLANES

ZQ_MQ, ZQ_MK, ZQ_MV = 0, 512, 1024
ZQ_GQ, ZQ_GK, ZQ_GV = 1536, 1792, 2048
ZQ_W = 2560
ZG_MO, ZG_GG, ZG_SMALL = 0, 512, 1024
ZG_W = 1152
SMALL_W = LANES
GLA_QK_W = GLA_HEADS * GLA_DK
GLA_V_W = GLA_HEADS * GLA_DV
GLA_PAIRS = GLA_HEADS // 2
GLA_SUB = SUBLANES

MXU_WIDTH = 256
FF_CHUNKS = ((0, 6 * MXU_WIDTH), (6 * MXU_WIDTH, D_FF))
VMEM_LIMIT_BYTES = 56 * 1024 * 1024

TM_FFN_IN = MXU_WIDTH
TM_POST = 2 * MXU_WIDTH
SAMPLE_MIXER_BATCHES = 16


def _rms(x, g):
    return x * lax.rsqrt(jnp.mean(x * x, axis=-1, keepdims=True) + EPS) * g


def _log_sigmoid(x):
    return jnp.minimum(x, 0.0) - jnp.log1p(jnp.exp(-jnp.abs(x)))


def _dot(a, b):
    return jnp.dot(a, b, preferred_element_type=F32)


def _dot_nt(a, b):
    return lax.dot_general(a, b, (((1,), (1,)), ((), ())), preferred_element_type=F32)


def _dot_f32(a, b):
    return jnp.dot(a, b, precision=lax.Precision.HIGHEST, preferred_element_type=F32)


def _swiglu_residual(x, g_ref, wg_ref, wu_ref, wd_ref):
    h = _rms(x, g_ref[...]).astype(BF16)
    acts = []
    for lo, hi in FF_CHUNKS:
        g = _dot(h, wg_ref[:, lo:hi])
        u = _dot(h, wu_ref[:, lo:hi])
        acts.append(((g * jax.nn.sigmoid(g)) * u).astype(BF16))
    acc = jnp.zeros_like(x)
    for (lo, hi), a in zip(FF_CHUNKS, acts):
        acc = acc + _dot(a, wd_ref[lo:hi, :])
    return x + 0.5 * acc


def _const_spec(shape):
    nd = len(shape)
    return pl.BlockSpec(shape, lambda *_: (0,) * nd, pipeline_mode=pl.Buffered(1))


def _params(sem):
    return pltpu.CompilerParams(dimension_semantics=sem, vmem_limit_bytes=VMEM_LIMIT_BYTES)


ZQ_BLOCKS = 4


def _ffn_in_stream_kernel(x_ref, g1_ref, wg_ref, wu_ref, wd_ref, gm_ref, wq_ref, wgt_ref, sq1_ref, sq2_ref,
                          x1_ref, zq_ref, zg_ref, wg_out, wu_out, wd_out, sq1_out, sq2_out, h_s, acc_s, x_s, *, n_ff):
    j = pl.program_id(0)

    @pl.when(j == 0)
    def _norm():
        x = x_ref[...].reshape(x_s.shape)
        x_s[...] = x
        h_s[...] = _rms(x, g1_ref[...]).astype(BF16)
        acc_s[...] = jnp.zeros_like(acc_s)

    @pl.when(j < n_ff)
    def _ffn_chunk():
        wg, wu, wd = (ref[...].astype(BF16) for ref in (wg_ref, wu_ref, wd_ref))
        wg_out[...] = wg
        wu_out[...] = wu
        wd_out[...] = wd
        h = h_s[...]
        g = _dot(h, wg)
        u = _dot(h, wu)
        acc_s[...] += _dot(((g * jax.nn.sigmoid(g)) * u).astype(BF16), wd)

    @pl.when(j == n_ff)
    def _residual():
        x1 = x_s[...] + 0.5 * acc_s[...]
        x1_ref[...] = x1
        h_s[...] = _rms(x1, gm_ref[...]).astype(BF16)

    @pl.when((j >= n_ff) & (j < n_ff + ZQ_BLOCKS))
    def _zq_block():
        zq_ref[...] = _dot(h_s[...], wq_ref[...]).astype(zq_ref.dtype)
        sq1_out[...] = sq1_ref[...].astype(BF16)
        sq2_out[...] = sq2_ref[...].astype(BF16)

    @pl.when(j == n_ff + ZQ_BLOCKS)
    def _zg():
        zg_ref[...] = _dot(h_s[...], wgt_ref[...])


def _ffn_in_stream(x, w, zq_dtype):
    n = x.shape[0] * x.shape[1]
    n_ff = D_FF // MXU_WIDTH
    zq_blk = ZQ_W // ZQ_BLOCKS
    ff = lambda j: jnp.minimum(j, n_ff - 1)
    zqb = lambda j: jnp.clip(j - n_ff, 0, ZQ_BLOCKS - 1)
    ff_cols = pl.BlockSpec((D_MODEL, MXU_WIDTH), lambda j: (0, ff(j)))
    ff_rows = pl.BlockSpec((MXU_WIDTH, D_MODEL), lambda j: (ff(j), 0))
    resident = lambda width: pl.BlockSpec((n, width), lambda j: (0, 0))
    sq_rows = pl.BlockSpec((D_MODEL // ZQ_BLOCKS, D_MODEL), lambda j: (zqb(j), 0))
    return pl.pallas_call(
        functools.partial(_ffn_in_stream_kernel, n_ff=n_ff),
        grid=(n_ff + ZQ_BLOCKS + 1,),
        in_specs=[_const_spec(x.shape), _const_spec((1, D_MODEL)), ff_cols, ff_cols, ff_rows,
                  _const_spec((1, D_MODEL)), pl.BlockSpec((D_MODEL, zq_blk), lambda j: (0, zqb(j))),
                  _const_spec((D_MODEL, ZG_W)), sq_rows, sq_rows],
        out_specs=[resident(D_MODEL), pl.BlockSpec((n, zq_blk), lambda j: (0, zqb(j))), resident(ZG_W),
                   ff_cols, ff_cols, ff_rows, sq_rows, sq_rows],
        out_shape=[jax.ShapeDtypeStruct((n, D_MODEL), F32), jax.ShapeDtypeStruct((n, ZQ_W), zq_dtype),
                   jax.ShapeDtypeStruct((n, ZG_W), F32), jax.ShapeDtypeStruct((D_MODEL, D_FF), BF16),
                   jax.ShapeDtypeStruct((D_MODEL, D_FF), BF16), jax.ShapeDtypeStruct((D_FF, D_MODEL), BF16),
                   jax.ShapeDtypeStruct((D_MODEL, D_MODEL), BF16), jax.ShapeDtypeStruct((D_MODEL, D_MODEL), BF16)],
        scratch_shapes=[pltpu.VMEM((n, D_MODEL), BF16), pltpu.VMEM((n, D_MODEL), F32),
                        pltpu.VMEM((n, D_MODEL), F32)],
        compiler_params=_params(("arbitrary",)),
        name="ffn_in",
    )(x, w["ffn1_g"], w["ffn1_wg"], w["ffn1_wu"], w["ffn1_wd"], w["mix_g"], w["w_in_q"], w["w_in_g"],
      w["w_out"], w["xattn_wq"])


def _mixer_stages(zqs, zgs, states, consts, ee_ref, *, L, t_real, chain):
    bias, wa2, ba, gml, ggl = consts
    nchunk = len(zqs)
    groups = [(b, h) for b in range(nchunk) for h in range(ML_HEADS)]
    pairs = [(b, p) for b in range(nchunk) for p in range(GLA_PAIRS)]
    padded = t_real < L
    valid = lax.broadcasted_iota(jnp.int32, (L, 1), 0) < t_real
    rr = lax.broadcasted_iota(jnp.int32, (L, L), 0)
    cc = lax.broadcasted_iota(jnp.int32, (L, L), 1)
    tril = cc <= rr
    c = GLA_SUB
    nb = L // c
    lane = lax.broadcasted_iota(jnp.int32, (1, LANES), 1)
    lane_blk = (lane % GLA_DK) // c
    lane_head = lane // GLA_DK
    v_lane_head = lax.broadcasted_iota(jnp.int32, (1, 2 * GLA_DV), 1) // GLA_DV
    row_blk = (lax.broadcasted_iota(jnp.int32, (2 * L, 1), 0) % L) // c
    t_in = lax.broadcasted_iota(jnp.int32, (1, c, 1), 1)
    new_states = [dict(c=[None] * ML_HEADS, n=[None] * ML_HEADS, m=[None] * ML_HEADS, s=[None] * GLA_HEADS)
                  for _ in range(nchunk)]
    prev = lambda b: new_states[b - 1] if chain and b > 0 else states[0 if chain else b]
    known = lambda b: not chain or b == 0

    carried = {}

    def carried_mlstm(b, h):
        st = prev(b)
        d = ml[(b, h)]
        carried[(b, h)] = _dot(d["qb"], st["c"][h].astype(BF16))

    def carried_gla(b, p):
        s_prev = prev(b)["s"]
        zero_blk = jnp.zeros((GLA_DK, GLA_DV), BF16)
        s_bd = jnp.concatenate(
            [jnp.concatenate([s_prev[2 * p].astype(BF16), zero_blk], axis=1),
             jnp.concatenate([zero_blk, s_prev[2 * p + 1].astype(BF16)], axis=1)], axis=0)
        carried[(b, "gla", p)] = _dot(gl[(b, p)]["q_dec"], s_bd)

    smalls, sms, lfs, b_cols, b_rows, sm_ts, las, bcs = [], [], [], [], [], [], [], []
    for b in range(nchunk):
        small = zgs[b][:, ZG_SMALL:ZG_SMALL + SMALL_W]
        sm = small + bias
        lf = _log_sigmoid(sm) * LOG2_E
        sm = sm * LOG2_E
        if padded:
            sm = jnp.where(valid, sm, -jnp.inf)
            lf = jnp.where(valid, lf, 0.0)
        smalls.append(small)
        sms.append(sm)
        lfs.append(lf)

    long_chunk = L > SUBLANES
    col_w = 1 if long_chunk else LANES
    tril_f, triu_f = tril.astype(F32), (rr <= cc).astype(F32)

    def cumsum_rows(x):
        if long_chunk:
            return _dot_f32(tril_f, x)
        acc = x[0:1]
        rows = [acc]
        for r in range(1, L):
            acc = acc + x[r:r + 1]
            rows.append(acc)
        return jnp.concatenate(rows, axis=0)

    lf_ts = [lf.T[0:SUBLANES] for lf in lfs] if long_chunk else None
    yield
    for b in range(nchunk):
        b_cols.append(cumsum_rows(lfs[b]))
        b_rows.append(_dot_f32(lf_ts[b], triu_f) if long_chunk else b_cols[b].T[0:SUBLANES])
        sm_ts.append(sms[b].T)
        la = _log_sigmoid(_dot(smalls[b].astype(BF16), wa2) + ba) * (LOG2_E / GLA_TAU)
        las.append(jnp.where(valid, la, 0.0) if padded else la)

    ml = {}
    for g in groups:
        b, h = g
        zq = zqs[b]
        qf = zq[:, ZQ_MQ + h * ML_DK:ZQ_MQ + (h + 1) * ML_DK].astype(F32)
        kf = zq[:, ZQ_MK + h * ML_DK:ZQ_MK + (h + 1) * ML_DK].astype(F32) * (ML_DK ** -0.5)
        vf = zq[:, ZQ_MV + h * ML_DV:ZQ_MV + (h + 1) * ML_DV].astype(F32)
        if padded:
            kf = jnp.where(valid, kf, 0.0)
            vf = jnp.where(valid, vf, 0.0)
        ml[g] = dict(qf=qf, qb=qf.astype(BF16), kf=kf, kb=kf.astype(BF16), vb=vf.astype(BF16))
    yield
    for b in range(nchunk):
        bcs.append(cumsum_rows(las[b]))
    for g in groups:
        d = ml[g]
        d["qk"] = _dot_nt(d["qb"], d["kb"])
        if known(g[0]):
            carried_mlstm(*g)

    yield
    for g in groups:
        b, h = g
        d = ml[g]
        i_col = jnp.broadcast_to(sms[b][:, h:h + 1], (L, col_w))
        b_col = jnp.broadcast_to(b_cols[b][:, ML_HEADS + h:ML_HEADS + h + 1], (L, col_w))
        b_row = b_rows[b][ML_HEADS + h:ML_HEADS + h + 1, :]
        i_row = sm_ts[b][h:h + 1, :]
        a_col = b_col + prev(b)["m"][h] * LOG2_E
        dm = jnp.where(tril, b_col[:, :L] - (b_row - i_row), -jnp.inf)
        mt = jnp.maximum(a_col, jnp.max(dm, axis=1, keepdims=True))
        w_inter = jnp.exp2(a_col - mt)
        s = d["qk"] * jnp.exp2(dm - mt[:, :L])
        kw = d["kf"] * jnp.exp2((b_col[L - 1:L] - mt[L - 1:L]) - (b_col - i_col))
        d.update(mt=mt, w_inter=w_inter, s=s, kw=kw, kw_t=kw.T.astype(BF16))
        new_states[b]["m"][h] = mt[L - 1:L, 0:1] * LN_2

    gl = {}
    for b in range(nchunk):
        zq = zqs[b]
        gq = zq[:, ZQ_GQ:ZQ_GQ + GLA_QK_W].astype(F32) * (GLA_DK ** -0.5)
        gk = zq[:, ZQ_GK:ZQ_GK + GLA_QK_W].astype(F32)
        gv = zq[:, ZQ_GV:ZQ_GV + GLA_V_W].astype(F32)
        if padded:
            gk = jnp.where(valid, gk, 0.0)
            gv = jnp.where(valid, gv, 0.0)
        stack = lambda x: jnp.concatenate([x[:, :LANES], x[:, LANES:]], axis=0)
        q2, k2, b2 = stack(gq), stack(gk), stack(bcs[b])
        q3 = q2.reshape(2 * nb, c, LANES)
        k3 = k2.reshape(2 * nb, c, LANES)
        b3 = b2.reshape(2 * nb, c, LANES)
        pair_terms = []
        for j in range(min(c, t_real)):
            decay = jnp.exp2(jnp.where(t_in >= j, b3 - b3[:, j:j + 1, :], -jnp.inf))
            pair_terms.append((q3 * k3[:, j:j + 1, :] * decay).reshape(2 * L, LANES).astype(BF16))
        kt2 = (k3 * jnp.exp2(b3[:, c - 1:c, :] - b3)).reshape(2 * L, LANES) if nb > 1 else None
        gl[b] = dict(gv=gv, q2=q2, k2=k2, b2=b2, pair_terms=pair_terms, kt2=kt2)
    yield
    for b in range(nchunk):
        acc = jnp.zeros((2 * L, LANES), F32)
        for j, pair_j in enumerate(gl[b]["pair_terms"]):
            acc = acc + _dot(pair_j, ee_ref[j])
        gl[b]["a_diag"] = jnp.where(lane_blk == row_blk, acc, 0.0)

    for bp in pairs:
        b, p = bp
        d = gl[b]
        rows_p = slice(p * L, (p + 1) * L)
        q_p, k_p, b_p = d["q2"][rows_p], d["k2"][rows_p], d["b2"][rows_p]
        if long_chunk:
            decay_col = jnp.exp2(b_p[L - SUBLANES:L].T[:, SUBLANES - 1:SUBLANES])
        else:
            decay_col = jnp.exp2(jnp.broadcast_to(b_p[L - 1:L], (LANES, LANES)).T)
        e = dict(q_dec=(q_p * jnp.exp2(b_p)).astype(BF16),
                 kh_t=(k_p * jnp.exp2(b_p[L - 1:L] - b_p)).T.astype(BF16),
                 decay_col=decay_col)
        if nb > 1:
            kt_p = d["kt2"][rows_p].astype(BF16)
            k_bd = jnp.concatenate([jnp.where(lane_head == hh, kt_p, 0.0) for hh in range(2)], axis=0)
            slabs, offs = [], []
            off = 0
            for j in range(nb - 1):
                lo = (j + 1) * c
                slabs.append(q_p[lo:] * jnp.exp2(b_p[lo:] - b_p[lo - 1:lo]))
                offs.append(off)
                off += L - lo
            e.update(q_var=jnp.concatenate(slabs, axis=0).astype(BF16), k_bd=k_bd, offs=offs)
        gl[bp] = e
    yield
    for bp in pairs:
        e = gl[bp]
        if nb > 1:
            e["r"] = _dot_nt(e["q_var"], e["k_bd"])
        if known(bp[0]):
            carried_gla(*bp)

    for g in groups:
        d = ml[g]
        d["sv"] = _dot(d["s"].astype(BF16), d["vb"])
        d["c_upd"] = _dot(d["kw_t"], d["vb"])
        d["den"] = jnp.sum(d["s"], axis=1, keepdims=True)
        if known(g[0]):
            d["qn"] = jnp.sum(d["qf"] * prev(g[0])["n"][g[1]], axis=1, keepdims=True)
    yield
    for bp in pairs:
        b, p = bp
        d, e = gl[b], gl[bp]
        a_p = d["a_diag"][p * L:(p + 1) * L]
        if nb > 1:
            blocks = []
            for i in range(nb):
                blk = a_p[i * c:(i + 1) * c]
                for j in range(i):
                    lo_r = e["offs"][j] + (i - j - 1) * c
                    blk = jnp.where(lane_blk == j, e["r"][lo_r:lo_r + c], blk)
                blocks.append(blk)
            a_p = jnp.concatenate(blocks, axis=0)
        v_f = d["gv"][:, p * 2 * GLA_DV:(p + 1) * 2 * GLA_DV]
        v_p = v_f.astype(BF16)
        if L < GLA_DK:
            v_rows = []
            for hh in range(2):
                v_rows += [jnp.where(v_lane_head == hh, v_f, 0.0), jnp.zeros((GLA_DK - L, 2 * GLA_DV), F32)]
            v_bd = jnp.concatenate(v_rows, axis=0).astype(BF16)
        else:
            v_bd = jnp.concatenate([jnp.where(v_lane_head == hh, v_p, 0.0) for hh in range(2)], axis=0)
        e["s_upd"] = [_dot(e["kh_t"][hh * GLA_DK:(hh + 1) * GLA_DK], v_p[:, hh * GLA_DV:(hh + 1) * GLA_DV])
                      for hh in range(2)]
        e.update(scores=a_p.astype(BF16), v_bd=v_bd)
    yield
    for bp in pairs:
        e = gl[bp]
        e["o_intra"] = _dot(e["scores"], e["v_bd"])

    yield
    ys = [[None] * (ML_HEADS + GLA_HEADS) for _ in range(nchunk)]

    def finish_mlstm(b):
        st = prev(b)
        for h in range(ML_HEADS):
            d = ml[(b, h)]
            mt, w_inter = d["mt"], d["w_inter"]
            qn = d["qn"] if known(b) else jnp.sum(d["qf"] * st["n"][h], axis=1, keepdims=True)
            den = d["den"] + w_inter * qn
            hh = (d["sv"] + w_inter * carried[(b, h)]) / jnp.maximum(jnp.abs(den), jnp.exp2(-mt))
            i_last = w_inter[L - 1:L]
            new_states[b]["c"][h] = i_last * st["c"][h] + d["c_upd"]
            new_states[b]["n"][h] = i_last * st["n"][h] + jnp.sum(d["kw"], axis=0, keepdims=True)
            hs = slice(h * ML_DV, (h + 1) * ML_DV)
            yn = hh * lax.rsqrt(jnp.mean(hh * hh, axis=-1, keepdims=True) + EPS) * gml[:, hs]
            ys[b][h] = jax.nn.sigmoid(zgs[b][:, ZG_MO + h * ML_DV:ZG_MO + (h + 1) * ML_DV]) * yn

    def finish_gla(b):
        st = prev(b)
        for p in range(GLA_PAIRS):
            e = gl[(b, p)]
            o = carried[(b, "gla", p)] + e["o_intra"]
            for hh in range(2):
                h = 2 * p + hh
                ds = slice(hh * GLA_DK, (hh + 1) * GLA_DK)
                vs = slice(hh * GLA_DV, (hh + 1) * GLA_DV)
                new_states[b]["s"][h] = e["decay_col"][ds] * st["s"][h] + e["s_upd"][hh]
                oh = o[:, vs]
                yn = oh * lax.rsqrt(jnp.mean(oh * oh, axis=-1, keepdims=True) + EPS) * ggl[:, h * GLA_DV:(h + 1) * GLA_DV]
                gg = zgs[b][:, ZG_GG + h * GLA_DV:ZG_GG + (h + 1) * GLA_DV]
                ys[b][ML_HEADS + h] = (gg * jax.nn.sigmoid(gg)) * yn

    if chain:
        for b in range(nchunk):
            if not known(b):
                for h in range(ML_HEADS):
                    carried_mlstm(b, h)
                for p in range(GLA_PAIRS):
                    carried_gla(b, p)
            finish_mlstm(b)
            finish_gla(b)
    else:
        for b in range(nchunk):
            finish_mlstm(b)
        for b in range(nchunk):
            finish_gla(b)
    return [jnp.concatenate(y, axis=1) for y in ys], new_states


def _drain(gen):
    try:
        while True:
            next(gen)
    except StopIteration as done:
        return done.value


def _mixer_compute(*args, **kwargs):
    return _drain(_mixer_stages(*args, **kwargs))


def _mixer_state_kernel(zq_ref, zg_ref, bias_ref, wa2_ref, ba_ref, gml_ref, ggl_ref, ee_ref, c0_ref, n0_ref, m0_ref,
                        s0_ref, y_ref, co_ref, no_ref, mo_ref, so_ref, *, L, t_real, nblk):
    consts = (bias_ref[...], wa2_ref[...], ba_ref[...], gml_ref[...], ggl_ref[...])
    lane = lax.broadcasted_iota(jnp.int32, (1, LANES), 1)

    def chunk_rows(ref, b):
        rows = ref[b * t_real:(b + 1) * t_real, :]
        return jnp.concatenate([rows, jnp.zeros((L - t_real, rows.shape[1]), rows.dtype)], axis=0)

    states = []
    for b in range(nblk):
        n_all = n0_ref[b]
        m_all = m0_ref[b]
        states.append(dict(c=[c0_ref[b, h] for h in range(ML_HEADS)],
                           n=[n_all[h:h + 1, :] for h in range(ML_HEADS)],
                           m=[m_all[:, h:h + 1] for h in range(ML_HEADS)],
                           s=[s0_ref[b, h] for h in range(GLA_HEADS)]))
    ys, new_states = _mixer_compute([chunk_rows(zq_ref, b) for b in range(nblk)],
                                    [chunk_rows(zg_ref, b) for b in range(nblk)], states, consts, ee_ref, L=L,
                                    t_real=t_real, chain=False)
    for b in range(nblk):
        st = new_states[b]
        y_ref[b * t_real:(b + 1) * t_real, :] = ys[b][0:t_real].astype(y_ref.dtype)
        for h in range(ML_HEADS):
            co_ref[b, h] = st["c"][h]
            so_ref[b, h] = st["s"][h]
        no_ref[b] = jnp.concatenate(st["n"], axis=0)
        m_row = jnp.zeros((1, LANES), F32)
        for h in range(ML_HEADS):
            m_row = jnp.where(lane == h, st["m"][h], m_row)
        mo_ref[b] = m_row[:, 0:ML_HEADS]


def _mixer_state(zq, zg, w, state, t_real, nblk):
    bsz = zq.shape[0] // t_real
    L = -(-t_real // SUBLANES) * SUBLANES
    rows = lambda width: pl.BlockSpec((nblk * t_real, width), lambda b: (b, 0))
    per_b = lambda *tail: pl.BlockSpec((nblk,) + tail, lambda b: (b,) + (0,) * len(tail))
    state_specs = [per_b(ML_HEADS, ML_DK, ML_DV), per_b(ML_HEADS, ML_DK), per_b(1, ML_HEADS),
                   per_b(GLA_HEADS, GLA_DK, GLA_DV)]
    consts = [_const_spec((1, SMALL_W)), _const_spec((SMALL_W, GLA_QK_W)), _const_spec((1, GLA_QK_W)),
              _const_spec((1, ML_HEADS * ML_DV)), _const_spec((1, GLA_V_W)), _const_spec((GLA_SUB, LANES, LANES))]
    return pl.pallas_call(
        functools.partial(_mixer_state_kernel, L=L, t_real=t_real, nblk=nblk),
        grid=(bsz // nblk,),
        in_specs=[rows(ZQ_W), rows(ZG_W)] + consts + state_specs,
        out_specs=[rows(D_MODEL)] + state_specs,
        out_shape=[jax.ShapeDtypeStruct((bsz * t_real, D_MODEL), F32),
                   jax.ShapeDtypeStruct((bsz, ML_HEADS, ML_DK, ML_DV), F32),
                   jax.ShapeDtypeStruct((bsz, ML_HEADS, ML_DK), F32),
                   jax.ShapeDtypeStruct((bsz, 1, ML_HEADS), F32),
                   jax.ShapeDtypeStruct((bsz, GLA_HEADS, GLA_DK, GLA_DV), F32)],
        compiler_params=_params(("arbitrary",)),
        name="mixer_state",
    )(zq, zg, w["gate_bias"], w["w_a2"], w["b_a"], w["mlstm_out_g"], w["gla_out_g"], w["diag_sum"], *state)


def _ffn_in_pieces(x, g1_ref, wg_ref, wu_ref, wd_ref, gm_ref, wq_ref, wgt_ref, x1_ref, zq_out, zg_out):
    h = _rms(x, g1_ref[...]).astype(BF16)
    acts = []
    for lo in range(0, D_FF, MXU_WIDTH):
        g = _dot(h, wg_ref[:, lo:lo + MXU_WIDTH])
        yield
        u = _dot(h, wu_ref[:, lo:lo + MXU_WIDTH])
        yield
        acts.append(((g * jax.nn.sigmoid(g)) * u).astype(BF16))
    a = jnp.concatenate(acts, axis=1)
    down = []
    for lo in range(0, D_MODEL, MXU_WIDTH):
        down.append(_dot(a, wd_ref[:, lo:lo + MXU_WIDTH]))
        yield
    x1 = x + 0.5 * jnp.concatenate(down, axis=1)
    x1_ref[...] = x1
    hm = _rms(x1, gm_ref[...]).astype(BF16)
    for lo in range(0, ZQ_W, MXU_WIDTH):
        zq_out[:, lo:lo + MXU_WIDTH] = _dot(hm, wq_ref[:, lo:lo + MXU_WIDTH]).astype(zq_out.dtype)
        yield
    for lo in range(0, ZG_W, MXU_WIDTH):
        hi = min(lo + MXU_WIDTH, ZG_W)
        zg_out[:, lo:hi] = _dot(hm, wgt_ref[:, lo:hi])
        yield


ATTN_STAGE_EVERY = 7
FFN_PIECES_PER_MIXER_STAGE = 1


def _ffn_mix_kernel(*refs, tiles_per_batch, n_cast):
    it = iter(refs)
    (x_ref, g1_ref, wg_ref, wu_ref, wd_ref, gm_ref, wq_ref, wgt_ref, bias_ref, wa2_ref, ba_ref, gml_ref, ggl_ref,
     ee_ref, qs_ref, ks_ref, vs_ref) = (next(it) for _ in range(17))
    cast_src = [next(it) for _ in range(n_cast)]
    x1_ref, ym_ref, co_ref, no_ref, mo_ref, so_ref, os_ref = (next(it) for _ in range(7))
    cast_dst = [next(it) for _ in range(n_cast)]
    zq_s, zg_s, c_s, n_s, m_s, s_s = (next(it) for _ in range(6))
    i = pl.program_id(0)
    for src, dst in zip(cast_src, cast_dst):
        dst[...] = src[...].astype(dst.dtype)

    @pl.when(i == 0)
    def _init():
        for ref in (zq_s, zg_s, c_s, n_s, m_s, s_s):
            ref[...] = jnp.zeros_like(ref)

    consts = (bias_ref[...], wa2_ref[...], ba_ref[...], gml_ref[...], ggl_ref[...])
    n_chunks = zq_s.shape[0] // CHUNK
    rows = [slice(k * CHUNK, (k + 1) * CHUNK) for k in range(n_chunks)]
    zqs = [zq_s[r, :] for r in rows]
    zgs = [zg_s[r, :] for r in rows]
    starts_batch = (i - 1) % tiles_per_batch == 0
    carry = lambda v: jnp.where(starts_batch, 0.0, v)
    state = dict(c=[carry(c_s[h]) for h in range(ML_HEADS)], n=[carry(n_s[h:h + 1, :]) for h in range(ML_HEADS)],
                 m=[carry(m_s[h:h + 1, 0:1]) for h in range(ML_HEADS)], s=[carry(s_s[h]) for h in range(GLA_HEADS)])

    def mixers():
        st = state
        for k in range(n_chunks):
            ys, new = yield from _mixer_stages([zqs[k]], [zgs[k]], [st], consts, ee_ref, L=CHUNK, t_real=CHUNK,
                                               chain=True)
            ym_ref[rows[k], :] = ys[0].astype(ym_ref.dtype)
            st = new[0]
            yield
        return st

    ffn = _ffn_in_pieces(x_ref[...], g1_ref, wg_ref, wu_ref, wd_ref, gm_ref, wq_ref, wgt_ref, x1_ref, zq_s, zg_s)
    mix = mixers()
    attn = _xattn_cache_stages(qs_ref, ks_ref, vs_ref, os_ref)
    live = {"ffn": True, "attn": True}

    def advance(name, gen):
        if live[name]:
            try:
                next(gen)
            except StopIteration:
                live[name] = False

    slot = 0
    while True:
        try:
            next(mix)
        except StopIteration as done:
            last = done.value
            break
        for _ in range(FFN_PIECES_PER_MIXER_STAGE):
            advance("ffn", ffn)
        if slot % ATTN_STAGE_EVERY == ATTN_STAGE_EVERY // 2:
            advance("attn", attn)
        slot += 1
    for name, gen in (("ffn", ffn), ("attn", attn)):
        if live[name]:
            _drain(gen)

    for h in range(ML_HEADS):
        c_s[h] = last["c"][h]
        n_s[h:h + 1, :] = last["n"][h]
        m_s[h:h + 1, :] = jnp.broadcast_to(last["m"][h], (1, LANES))
        s_s[h] = last["s"][h]

    @pl.when(jnp.logical_and(i >= 1, (i - 1) % tiles_per_batch == tiles_per_batch - 1))
    def _emit_state():
        lane = lax.broadcasted_iota(jnp.int32, (1, LANES), 1)
        for h in range(ML_HEADS):
            co_ref[0, h] = last["c"][h]
            so_ref[0, h] = last["s"][h]
        no_ref[0] = jnp.concatenate(last["n"], axis=0)
        m_row = jnp.zeros((1, LANES), F32)
        for h in range(ML_HEADS):
            m_row = jnp.where(lane == h, last["m"][h], m_row)
        mo_ref[0] = m_row[:, 0:ML_HEADS]


def _ffn_mix(x, w, tm, q_s, k_cache, v_cache, to_cast):
    bsz, t, _ = x.shape
    tiles_per_batch = t // tm
    n_tiles = bsz * tiles_per_batch
    assert t % tm == 0 and tm % CHUNK == 0
    bs = k_cache.shape[0]
    ts = q_s.shape[0] // bs
    assert bs % n_tiles == 0
    sb = bs // n_tiles
    cur_tile = lambda i: jnp.minimum(i, n_tiles - 1)
    prev_tile = lambda i: jnp.maximum(i - 1, 0)
    cur = lambda width: pl.BlockSpec((tm, width), lambda i: (cur_tile(i), 0))
    per_b = lambda *tail: pl.BlockSpec((1,) + tail, lambda i: (prev_tile(i) // tiles_per_batch,) + (0,) * len(tail))
    state_specs = [per_b(ML_HEADS, ML_DK, ML_DV), per_b(ML_HEADS, ML_DK), per_b(1, ML_HEADS),
                   per_b(GLA_HEADS, GLA_DK, GLA_DV)]
    qo_s = pl.BlockSpec((sb * ts, D_MODEL), lambda i: (cur_tile(i), 0))
    kv_s = pl.BlockSpec((sb, N_MEM * CACHE_ROW_GROUP, LANES), lambda i: (cur_tile(i), 0, 0))
    cast_specs = []
    for a in to_cast:
        rb = next(r for r in range(BF16_SUBLANES, a.shape[0] + 1, BF16_SUBLANES)
                  if a.shape[0] % r == 0 and a.shape[0] // r <= n_tiles)
        cast_specs.append(pl.BlockSpec((rb, a.shape[1]), lambda i, last=a.shape[0] // rb - 1: (jnp.minimum(i, last), 0)))
    x1, ym, c_new, n_new, m_new, s_new, o_s, *cast = pl.pallas_call(
        functools.partial(_ffn_mix_kernel, tiles_per_batch=tiles_per_batch, n_cast=len(to_cast)),
        grid=(n_tiles + 1,),
        in_specs=[cur(D_MODEL), _const_spec((1, D_MODEL)), _const_spec((D_MODEL, D_FF)),
                  _const_spec((D_MODEL, D_FF)), _const_spec((D_FF, D_MODEL)), _const_spec((1, D_MODEL)),
                  _const_spec((D_MODEL, ZQ_W)), _const_spec((D_MODEL, ZG_W)), _const_spec((1, SMALL_W)),
                  _const_spec((SMALL_W, GLA_QK_W)), _const_spec((1, GLA_QK_W)), _const_spec((1, ML_HEADS * ML_DV)),
                  _const_spec((1, GLA_V_W)), _const_spec((GLA_SUB, LANES, LANES)), qo_s, kv_s, kv_s] + cast_specs,
        out_specs=([cur(D_MODEL), pl.BlockSpec((tm, D_MODEL), lambda i: (prev_tile(i), 0))] + state_specs + [qo_s]
                   + cast_specs),
        out_shape=[jax.ShapeDtypeStruct((n_tiles * tm, D_MODEL), F32),
                   jax.ShapeDtypeStruct((n_tiles * tm, D_MODEL), BF16),
                   jax.ShapeDtypeStruct((bsz, ML_HEADS, ML_DK, ML_DV), F32),
                   jax.ShapeDtypeStruct((bsz, ML_HEADS, ML_DK), F32),
                   jax.ShapeDtypeStruct((bsz, 1, ML_HEADS), F32),
                   jax.ShapeDtypeStruct((bsz, GLA_HEADS, GLA_DK, GLA_DV), F32),
                   jax.ShapeDtypeStruct((bs * ts, D_MODEL), F32)] + [jax.ShapeDtypeStruct(a.shape, BF16) for a in to_cast],
        scratch_shapes=[pltpu.VMEM((tm, ZQ_W), BF16), pltpu.VMEM((tm, ZG_W), F32),
                        pltpu.VMEM((ML_HEADS, ML_DK, ML_DV), F32), pltpu.VMEM((SUBLANES, LANES), F32),
                        pltpu.VMEM((SUBLANES, LANES), F32), pltpu.VMEM((GLA_HEADS, GLA_DK, GLA_DV), F32)],
        compiler_params=_params(("arbitrary",)),
        name="ffn_mix",
    )(x.reshape(bsz * t, D_MODEL), w["ffn1_g"], w["ffn1_wg"], w["ffn1_wu"], w["ffn1_wd"], w["mix_g"], w["w_in_q"],
      w["w_in_g"], w["gate_bias"], w["w_a2"], w["b_a"], w["mlstm_out_g"], w["gla_out_g"], w["diag_sum"],
      q_s, _cache_rows_view(k_cache), _cache_rows_view(v_cache), *to_cast)
    return x1, ym, (c_new[None], n_new[None], m_new.reshape(1, bsz, ML_HEADS), s_new[None]), o_s, cast


def _post_mix_kernel(x1_ref, ym_ref, wout_ref, gx_ref, wq_ref, x2_ref, q_ref):
    x2 = x1_ref[...] + _dot(ym_ref[...].astype(BF16), wout_ref[...])
    x2_ref[...] = x2
    hq = _rms(x2, gx_ref[...]).astype(BF16)
    q_ref[...] = _dot(hq, wq_ref[...]).astype(q_ref.dtype)


def _post_mix(x1, ym, w, tm, q_dtype):
    n = x1.shape[0]
    row = lambda: pl.BlockSpec((tm, D_MODEL), lambda i: (i, 0))
    return pl.pallas_call(
        _post_mix_kernel,
        grid=(n // tm,),
        in_specs=[row(), row(), _const_spec((D_MODEL, D_MODEL)), _const_spec((1, D_MODEL)),
                  _const_spec((D_MODEL, D_MODEL))],
        out_specs=[row(), row()],
        out_shape=[jax.ShapeDtypeStruct((n, D_MODEL), F32), jax.ShapeDtypeStruct((n, D_MODEL), q_dtype)],
        compiler_params=_params(("arbitrary",)),
        name="post_mix",
    )(x1, ym, w["w_out"], w["xattn_g"], w["xattn_wq"])


def _softmax(s):
    e = jnp.exp(s - jnp.max(s, axis=-1, keepdims=True))
    return e / jnp.sum(e, axis=-1, keepdims=True)


def _post_fused_kernel(x1_ref, ym_ref, mem_ref, gm_ref, wk_ref, wv_ref, wout_ref, gx_ref, wq_ref, wo_ref, g2_ref,
                       wg_ref, wu_ref, wd_ref, gf_ref, y_ref, ko_ref, vo_ref, k_s, v_s):
    @pl.when(pl.program_id(1) == 0)
    def _memkv():
        hn = _rms(mem_ref[0], gm_ref[...]).astype(BF16)
        for w_ref, out_ref, s_ref in ((wk_ref, ko_ref, k_s), (wv_ref, vo_ref, v_s)):
            kv = _dot(hn, w_ref[...])
            _cache_rows_store(out_ref, 0, kv)
            s_ref[...] = kv.astype(BF16)

    x2 = x1_ref[0] + _dot(ym_ref[0].astype(BF16), wout_ref[...])
    q = _dot(_rms(x2, gx_ref[...]).astype(BF16), wq_ref[...]).astype(BF16)
    heads = [slice(h * XA_DH, (h + 1) * XA_DH) for h in range(XA_HEADS)]
    k_full, v_full = k_s[...], v_s[...]
    scores = [_dot_nt(q[:, hs], k_full[:, hs]) * (XA_DH ** -0.5) for hs in heads]
    probs = [_softmax(s).astype(BF16) for s in scores]
    o = jnp.concatenate([_dot(p, v_full[:, hs]).astype(BF16) for hs, p in zip(heads, probs)], axis=1)
    x3 = x2 + _dot(o, wo_ref[...])
    x4 = _swiglu_residual(x3, g2_ref, wg_ref, wu_ref, wd_ref)
    y_ref[0] = _rms(x4, gf_ref[...])


def _post_fused(x1, ym, mem, w, tm):
    bsz, t = x1.shape[0], x1.shape[1]
    row = lambda: pl.BlockSpec((1, tm, D_MODEL), lambda b, j: (b, j, 0))
    kv = pl.BlockSpec((1, N_MEM * CACHE_ROW_GROUP, LANES), lambda b, j: (b, 0, 0))
    sq = _const_spec((D_MODEL, D_MODEL))
    vec = _const_spec((1, D_MODEL))
    kv_shape = jax.ShapeDtypeStruct((bsz, N_MEM * CACHE_ROW_GROUP, LANES), F32)
    return pl.pallas_call(
        _post_fused_kernel,
        grid=(bsz, t // tm),
        in_specs=[row(), row(), pl.BlockSpec((1, N_MEM, D_MODEL), lambda b, j: (b, 0, 0)), vec, sq, sq,
                  sq, vec, sq, sq, vec, _const_spec((D_MODEL, D_FF)),
                  _const_spec((D_MODEL, D_FF)), _const_spec((D_FF, D_MODEL)), vec],
        out_specs=[row(), kv, kv],
        out_shape=[jax.ShapeDtypeStruct((bsz, t, D_MODEL), F32), kv_shape, kv_shape],
        scratch_shapes=[pltpu.VMEM((N_MEM, D_MODEL), BF16)] * 2,
        compiler_params=_params(("arbitrary", "arbitrary")),
        name="post_fused",
    )(x1, ym, mem, w["mem_g"], w["xattn_wk"], w["xattn_wv"], w["w_out"], w["xattn_g"], w["xattn_wq"],
      w["xattn_wo"], w["ffn2_g"], w["ffn2_wg"], w["ffn2_wu"], w["ffn2_wd"], w["final_g"])


XA_LANE_TILES = XA_DH // LANES
CACHE_ROW_GROUP = XA_LANE_TILES * XA_HEADS


def _cache_rows_view(x):
    bsz = x.shape[0]
    x = x.reshape(bsz, N_MEM, XA_HEADS, XA_LANE_TILES, LANES)
    return x.transpose(0, 1, 3, 2, 4).reshape(bsz, N_MEM * CACHE_ROW_GROUP, LANES)


def _cache_rows_unview(x):
    bsz = x.shape[0]
    x = x.reshape(bsz, N_MEM, XA_LANE_TILES, XA_HEADS, LANES)
    return x.transpose(0, 1, 3, 2, 4).reshape(bsz, N_MEM, XA_HEADS, XA_DH)


def _cache_rows_store(ref, b, x):
    for h in range(XA_HEADS):
        for lt in range(XA_LANE_TILES):
            lo = h * XA_DH + lt * LANES
            ref[b, pl.ds(lt * XA_HEADS + h, N_MEM, stride=CACHE_ROW_GROUP), :] = x[:, lo:lo + LANES]


def _cache_rows_load(ref, b):
    cols = [ref[b, pl.ds(lt * XA_HEADS + h, N_MEM, stride=CACHE_ROW_GROUP), :]
            for h in range(XA_HEADS) for lt in range(XA_LANE_TILES)]
    return jnp.concatenate(cols, axis=1).astype(BF16)


def _xattn_cache_stages(q_ref, k_ref, v_ref, o_ref):
    bb = k_ref.shape[0]
    tq = q_ref.shape[0] // bb
    lane_head = lax.broadcasted_iota(jnp.int32, (1, D_MODEL), 1) // XA_DH
    qs = [q_ref[b * tq:(b + 1) * tq, :] for b in range(bb)]
    q_bds = [jnp.concatenate([jnp.where(lane_head == h, q, 0.0) for h in range(XA_HEADS)], axis=0).astype(BF16)
             for q in qs]
    k_fulls = [_cache_rows_load(k_ref, b) for b in range(bb)]
    yield
    scores = [_dot_nt(q_bds[b], k_fulls[b]) * (XA_DH ** -0.5) for b in range(bb)]
    yield
    p_all = _softmax(jnp.concatenate(scores, axis=0)).astype(BF16)
    v_fulls = [_cache_rows_load(v_ref, b) for b in range(bb)]
    yield
    rows = XA_HEADS * tq
    o_fulls = [_dot(p_all[b * rows:(b + 1) * rows], v_fulls[b]) for b in range(bb)]
    yield
    for b in range(bb):
        o = jnp.zeros((tq, D_MODEL), F32)
        for h in range(XA_HEADS):
            o = jnp.where(lane_head == h, o_fulls[b][h * tq:(h + 1) * tq], o)
        o_ref[b * tq:(b + 1) * tq, :] = o.astype(o_ref.dtype)


def _ffn_out_stream_kernel(x2_ref, o_ref, wo_ref, g2_ref, wg_ref, wu_ref, wd_ref, gf_ref, y_ref, x3_s, h_s, acc_s):
    j = pl.program_id(0)

    @pl.when(j == 0)
    def _attn_out():
        x3 = x2_ref[...] + _dot(o_ref[...].astype(BF16), wo_ref[...])
        x3_s[...] = x3
        h_s[...] = _rms(x3, g2_ref[...]).astype(BF16)
        acc_s[...] = jnp.zeros_like(acc_s)

    h = h_s[...]
    g = _dot(h, wg_ref[...])
    u = _dot(h, wu_ref[...])
    acc_s[...] += _dot(((g * jax.nn.sigmoid(g)) * u).astype(BF16), wd_ref[...])

    @pl.when(j == pl.num_programs(0) - 1)
    def _final_norm():
        y_ref[...] = _rms(x3_s[...] + 0.5 * acc_s[...], gf_ref[...]).reshape(y_ref.shape)


def _ffn_out_stream(x2, o, w, bsz):
    n = x2.shape[0]
    y_shape = (bsz, n // bsz, D_MODEL)
    ff_cols = pl.BlockSpec((D_MODEL, MXU_WIDTH), lambda j: (0, j))
    ff_rows = pl.BlockSpec((MXU_WIDTH, D_MODEL), lambda j: (j, 0))
    return pl.pallas_call(
        _ffn_out_stream_kernel,
        grid=(D_FF // MXU_WIDTH,),
        in_specs=[_const_spec((n, D_MODEL)), _const_spec((n, D_MODEL)), _const_spec((D_MODEL, D_MODEL)),
                  _const_spec((1, D_MODEL)), ff_cols, ff_cols, ff_rows, _const_spec((1, D_MODEL))],
        out_specs=pl.BlockSpec(y_shape, lambda j: (0, 0, 0)),
        out_shape=jax.ShapeDtypeStruct(y_shape, F32),
        scratch_shapes=[pltpu.VMEM((n, D_MODEL), F32), pltpu.VMEM((n, D_MODEL), BF16),
                        pltpu.VMEM((n, D_MODEL), F32)],
        compiler_params=_params(("arbitrary",)),
        name="ffn_out",
    )(x2, o, w["xattn_wo"], w["ffn2_g"], w["ffn2_wg"], w["ffn2_wu"], w["ffn2_wd"], w["final_g"])


IN_SIZES = (("mq", ML_HEADS * ML_DK), ("mk", ML_HEADS * ML_DK), ("mv", ML_HEADS * ML_DV), ("mi", ML_HEADS),
            ("mf", ML_HEADS), ("mo", ML_HEADS * ML_DV), ("gq", GLA_QK_W), ("gk", GLA_QK_W), ("gv", GLA_V_W),
            ("ga", GLA_RANK), ("gg", GLA_V_W))
IN_OFFSET = {name: sum(width for _, width in IN_SIZES[:i]) for i, (name, _) in enumerate(IN_SIZES)}
D_IN = sum(width for _, width in IN_SIZES)
IN_MOVES = ((0, ZQ_MQ, IN_OFFSET["mq"], ZQ_GQ - ZQ_MQ), (0, ZQ_GQ, IN_OFFSET["gq"], ZQ_W - ZQ_GQ),
            (1, ZG_MO, IN_OFFSET["mo"], ZG_GG - ZG_MO), (1, ZG_GG, IN_OFFSET["gg"], ZG_SMALL - ZG_GG))
TM_REGROUP = MXU_WIDTH
assert IN_OFFSET["mf"] == IN_OFFSET["mi"] + ML_HEADS
assert all(IN_OFFSET[name] % SUBLANES == 0 for name in ("mq", "mi", "mo", "gq", "ga", "gg"))


def _w_in_regroup_kernel(wt_ref, q_ref, g_ref):
    def put(out_ref, dst, rows_t):
        out_ref[:, dst:dst + LANES] = rows_t.T.astype(BF16)

    for slab, dst, src, width in IN_MOVES:
        for off in range(0, width, LANES):
            put((q_ref, g_ref)[slab], dst + off, wt_ref[src + off:src + off + LANES, :])
    narrow = 2 * ML_HEADS + GLA_RANK
    put(g_ref, ZG_SMALL, jnp.concatenate(
        [wt_ref[IN_OFFSET["mi"]:IN_OFFSET["mi"] + 2 * ML_HEADS, :],
         wt_ref[IN_OFFSET["ga"]:IN_OFFSET["ga"] + GLA_RANK, :],
         jnp.zeros((SMALL_W - narrow, wt_ref.shape[1]), F32)], axis=0))


def _w_in_regroup(w_in):
    assert w_in.shape == (D_MODEL, D_IN)
    rows = lambda width: pl.BlockSpec((TM_REGROUP, width), lambda i: (i, 0))
    return pl.pallas_call(
        _w_in_regroup_kernel,
        grid=(D_MODEL // TM_REGROUP,),
        in_specs=[pl.BlockSpec((D_IN, TM_REGROUP), lambda i: (0, i))],
        out_specs=[rows(ZQ_W), rows(ZG_W)],
        out_shape=[jax.ShapeDtypeStruct((D_MODEL, ZQ_W), BF16), jax.ShapeDtypeStruct((D_MODEL, ZG_W), BF16)],
        compiler_params=_params(("arbitrary",)),
        name="w_in_regroup",
    )(w_in.T)


def _prep_weights(p):
    bf = lambda a: a.astype(BF16)
    row = lambda a: a.reshape(1, -1).astype(F32)
    w_in_q, w_in_g = _w_in_regroup(p["w_in"])
    gate_bias = jnp.concatenate([p["mlstm_b_i"], p["mlstm_b_f"], jnp.zeros((SMALL_W - 2 * ML_HEADS,), F32)])
    w_a2 = jnp.zeros((SMALL_W, GLA_QK_W), F32).at[2 * ML_HEADS:2 * ML_HEADS + GLA_RANK].set(p["gla_w_a2"])
    lane = jnp.arange(LANES)
    same_head = (lane[:, None] // GLA_DK) == (lane[None, :] // GLA_DK)
    diag_sum = jnp.stack([same_head & ((lane[None, :] % GLA_SUB) == j) for j in range(GLA_SUB)]).astype(BF16)
    return dict(
        ffn1_wg=p["ffn1_w_gate"], ffn1_wu=p["ffn1_w_up"], ffn1_wd=p["ffn1_w_down"],
        ffn1_g=row(p["ffn1_norm_g"]), mix_g=row(p["mix_norm_g"]), w_in_q=w_in_q, w_in_g=w_in_g,
        gate_bias=row(gate_bias), w_a2=bf(w_a2), b_a=row(p["gla_b_a"]), mlstm_out_g=row(p["mlstm_out_g"]),
        gla_out_g=row(p["gla_out_g"]), diag_sum=diag_sum,
        w_out=p["w_out"], xattn_g=row(p["xattn_norm_g"]), xattn_wq=p["xattn_w_q"],
        mem_g=row(p["mem_norm_g"]), ffn2_g=row(p["ffn2_norm_g"]), final_g=row(p["final_g"]),
        xattn_wo=p["xattn_w_o"], xattn_wk=p["xattn_w_k"], xattn_wv=p["xattn_w_v"], ffn2_wg=p["ffn2_w_gate"],
        ffn2_wu=p["ffn2_w_up"], ffn2_wd=p["ffn2_w_down"])


EARLY_WEIGHTS = ("ffn1_wg", "ffn1_wu", "ffn1_wd", "w_out", "xattn_wq")
LATE_WEIGHTS = ("xattn_wo", "xattn_wk", "xattn_wv", "ffn2_wg", "ffn2_wu", "ffn2_wd")


def kernel(x_prompt, x_sample, mem_prompt, cache_mem_k, cache_mem_v, state_mlstm_c, state_mlstm_n, state_mlstm_m, state_gla_s, ffn1_norm_g, ffn1_w_gate, ffn1_w_up, ffn1_w_down, mix_norm_g, w_in, mlstm_b_i, mlstm_b_f, mlstm_out_g, gla_w_a2, gla_b_a, gla_out_g, w_out, xattn_norm_g, mem_norm_g, xattn_w_q, xattn_w_k, xattn_w_v, xattn_w_o, ffn2_norm_g, ffn2_w_gate, ffn2_w_up, ffn2_w_down, final_norm_g):
    assert ffn1_norm_g.shape[0] == 1, "single-layer stack"
    layer = dict(ffn1_norm_g=ffn1_norm_g, ffn1_w_gate=ffn1_w_gate, ffn1_w_up=ffn1_w_up, ffn1_w_down=ffn1_w_down,
                 mix_norm_g=mix_norm_g, w_in=w_in, mlstm_b_i=mlstm_b_i, mlstm_b_f=mlstm_b_f,
                 mlstm_out_g=mlstm_out_g, gla_w_a2=gla_w_a2, gla_b_a=gla_b_a, gla_out_g=gla_out_g, w_out=w_out,
                 xattn_norm_g=xattn_norm_g, mem_norm_g=mem_norm_g, xattn_w_q=xattn_w_q, xattn_w_k=xattn_w_k,
                 xattn_w_v=xattn_w_v, xattn_w_o=xattn_w_o, ffn2_norm_g=ffn2_norm_g, ffn2_w_gate=ffn2_w_gate,
                 ffn2_w_up=ffn2_w_up, ffn2_w_down=ffn2_w_down)
    p = {name: arr[0] for name, arr in layer.items()}
    p["final_g"] = final_norm_g
    w = _prep_weights(p)

    bp, tp, _ = x_prompt.shape
    bs, ts, _ = x_sample.shape

    state = (state_mlstm_c[0], state_mlstm_n[0], state_mlstm_m[0].reshape(bs, 1, ML_HEADS), state_gla_s[0])
    x1_s, zq_s, zg_s, *early = _ffn_in_stream(x_sample, w, F32)
    w.update(zip(EARLY_WEIGHTS, early))
    ym_s, c_s, n_s, m_s, s_s = _mixer_state(zq_s, zg_s, w, state, ts, SAMPLE_MIXER_BATCHES)
    states_s = (c_s[None], n_s[None], m_s.reshape(1, bs, ML_HEADS), s_s[None])
    x2_s, q_s = _post_mix(x1_s, ym_s, w, TM_FFN_IN, F32)

    x1_p, ym_p, states_p, o_s, late = _ffn_mix(x_prompt, w, TM_FFN_IN, q_s, cache_mem_k[0], cache_mem_v[0],
                                               [w[name] for name in LATE_WEIGHTS])
    w.update(zip(LATE_WEIGHTS, late))
    y_p, mem_k_p, mem_v_p = _post_fused(x1_p.reshape(bp, tp, D_MODEL), ym_p.reshape(bp, tp, D_MODEL), mem_prompt, w,
                                        TM_POST)
    y_s = _ffn_out_stream(x2_s, o_s, w, bs)

    return (y_p, y_s, _cache_rows_unview(mem_k_p)[None], _cache_rows_unview(mem_v_p)[None]) + states_p + states_s
```

```python
import functools

import jax
import jax.numpy as jnp
import numpy as np
from jax import lax
from jax.experimental import pallas as pl
from jax.experimental.pallas import tpu as pltpu

F32 = jnp.float32
BF16 = jnp.bfloat16

D_MODEL = 1024
D_FF = 2816
ML_HEADS = 4
ML_DK = 128
ML_DV = 128
GLA_HEADS = 4
GLA_DK = 64
GLA_DV = 128
GLA_RANK = 16
GLA_TAU = 16.0
N_MEM = 256
XA_HEADS = 4
XA_DH = D_MODEL // XA_HEADS
EPS = 1e-6
CHUNK = 64
LOG2_E = 1.4426950408889634
LN_2 = 0.6931471805599453
LANES = 128
SUBLANES = 8
BF16_SUBLANES = 2 * SUBLANES

ZQ_MQ, ZQ_MK, ZQ_MV = 0, 512, 1024
ZQ_GQ, ZQ_GK, ZQ_GV = 1536, 1792, 2048
ZQ_W = 2560
ZG_MO, ZG_GG, ZG_SMALL = 0, 512, 1024
ZG_W = 1152
SMALL_W = LANES
GLA_QK_W = GLA_HEADS * GLA_DK
GLA_V_W = GLA_HEADS * GLA_DV
GLA_PAIRS = GLA_HEADS // 2
GLA_SUB = SUBLANES

MXU_WIDTH = 256
FF_CHUNKS = ((0, 6 * MXU_WIDTH), (6 * MXU_WIDTH, D_FF))
VMEM_LIMIT_BYTES = 56 * 1024 * 1024

TM_FFN_IN = MXU_WIDTH
TM_POST = 2 * MXU_WIDTH
SAMPLE_MIXER_BATCHES = 16


def _rms(x, g):
    return x * lax.rsqrt(jnp.mean(x * x, axis=-1, keepdims=True) + EPS) * g


def _log_sigmoid(x):
    return jnp.minimum(x, 0.0) - jnp.log1p(jnp.exp(-jnp.abs(x)))


def _dot(a, b):
    return jnp.dot(a, b, preferred_element_type=F32)


def _dot_nt(a, b):
    return lax.dot_general(a, b, (((1,), (1,)), ((), ())), preferred_element_type=F32)


def _dot_f32(a, b):
    return jnp.dot(a, b, precision=lax.Precision.HIGHEST, preferred_element_type=F32)


def _swiglu_residual(x, g_ref, wg_ref, wu_ref, wd_ref):
    h = _rms(x, g_ref[...]).astype(BF16)
    acts = []
    for lo, hi in FF_CHUNKS:
        g = _dot(h, wg_ref[:, lo:hi])
        u = _dot(h, wu_ref[:, lo:hi])
        acts.append(((g * jax.nn.sigmoid(g)) * u).astype(BF16))
    acc = jnp.zeros_like(x)
    for (lo, hi), a in zip(FF_CHUNKS, acts):
        acc = acc + _dot(a, wd_ref[lo:hi, :])
    return x + 0.5 * acc


def _const_spec(shape):
    nd = len(shape)
    return pl.BlockSpec(shape, lambda *_: (0,) * nd, pipeline_mode=pl.Buffered(1))


def _params(sem):
    return pltpu.CompilerParams(dimension_semantics=sem, vmem_limit_bytes=VMEM_LIMIT_BYTES)


ZQ_BLOCKS = 4


def _ffn_in_stream_kernel(x_ref, g1_ref, wg_ref, wu_ref, wd_ref, gm_ref, wq_ref, wgt_ref, sq1_ref, sq2_ref,
                          x1_ref, zq_ref, zg_ref, wg_out, wu_out, wd_out, sq1_out, sq2_out, h_s, acc_s, x_s, *, n_ff):
    j = pl.program_id(0)

    @pl.when(j == 0)
    def _norm():
        x = x_ref[...].reshape(x_s.shape)
        x_s[...] = x
        h_s[...] = _rms(x, g1_ref[...]).astype(BF16)
        acc_s[...] = jnp.zeros_like(acc_s)

    @pl.when(j < n_ff)
    def _ffn_chunk():
        wg, wu, wd = (ref[...].astype(BF16) for ref in (wg_ref, wu_ref, wd_ref))
        wg_out[...] = wg
        wu_out[...] = wu
        wd_out[...] = wd
        h = h_s[...]
        g = _dot(h, wg)
        u = _dot(h, wu)
        acc_s[...] += _dot(((g * jax.nn.sigmoid(g)) * u).astype(BF16), wd)

    @pl.when(j == n_ff)
    def _residual():
        x1 = x_s[...] + 0.5 * acc_s[...]
        x1_ref[...] = x1
        h_s[...] = _rms(x1, gm_ref[...]).astype(BF16)

    @pl.when((j >= n_ff) & (j < n_ff + ZQ_BLOCKS))
    def _zq_block():
        zq_ref[...] = _dot(h_s[...], wq_ref[...]).astype(zq_ref.dtype)
        sq1_out[...] = sq1_ref[...].astype(BF16)
        sq2_out[...] = sq2_ref[...].astype(BF16)

    @pl.when(j == n_ff + ZQ_BLOCKS)
    def _zg():
        zg_ref[...] = _dot(h_s[...], wgt_ref[...])


def _ffn_in_stream(x, w, zq_dtype):
    n = x.shape[0] * x.shape[1]
    n_ff = D_FF // MXU_WIDTH
    zq_blk = ZQ_W // ZQ_BLOCKS
    ff = lambda j: jnp.minimum(j, n_ff - 1)
    zqb = lambda j: jnp.clip(j - n_ff, 0, ZQ_BLOCKS - 1)
    ff_cols = pl.BlockSpec((D_MODEL, MXU_WIDTH), lambda j: (0, ff(j)))
    ff_rows = pl.BlockSpec((MXU_WIDTH, D_MODEL), lambda j: (ff(j), 0))
    resident = lambda width: pl.BlockSpec((n, width), lambda j: (0, 0))
    sq_rows = pl.BlockSpec((D_MODEL // ZQ_BLOCKS, D_MODEL), lambda j: (zqb(j), 0))
    return pl.pallas_call(
        functools.partial(_ffn_in_stream_kernel, n_ff=n_ff),
        grid=(n_ff + ZQ_BLOCKS + 1,),
        in_specs=[_const_spec(x.shape), _const_spec((1, D_MODEL)), ff_cols, ff_cols, ff_rows,
                  _const_spec((1, D_MODEL)), pl.BlockSpec((D_MODEL, zq_blk), lambda j: (0, zqb(j))),
                  _const_spec((D_MODEL, ZG_W)), sq_rows, sq_rows],
        out_specs=[resident(D_MODEL), pl.BlockSpec((n, zq_blk), lambda j: (0, zqb(j))), resident(ZG_W),
                   ff_cols, ff_cols, ff_rows, sq_rows, sq_rows],
        out_shape=[jax.ShapeDtypeStruct((n, D_MODEL), F32), jax.ShapeDtypeStruct((n, ZQ_W), zq_dtype),
                   jax.ShapeDtypeStruct((n, ZG_W), F32), jax.ShapeDtypeStruct((D_MODEL, D_FF), BF16),
                   jax.ShapeDtypeStruct((D_MODEL, D_FF), BF16), jax.ShapeDtypeStruct((D_FF, D_MODEL), BF16),
                   jax.ShapeDtypeStruct((D_MODEL, D_MODEL), BF16), jax.ShapeDtypeStruct((D_MODEL, D_MODEL), BF16)],
        scratch_shapes=[pltpu.VMEM((n, D_MODEL), BF16), pltpu.VMEM((n, D_MODEL), F32),
                        pltpu.VMEM((n, D_MODEL), F32)],
        compiler_params=_params(("arbitrary",)),
        name="ffn_in",
    )(x, w["ffn1_g"], w["ffn1_wg"], w["ffn1_wu"], w["ffn1_wd"], w["mix_g"], w["w_in_q"], w["w_in_g"],
      w["w_out"], w["xattn_wq"])


def _mixer_stages(zqs, zgs, states, consts, ee_ref, *, L, t_real, chain):
    bias, wa2, ba, gml, ggl = consts
    nchunk = len(zqs)
    groups = [(b, h) for b in range(nchunk) for h in range(ML_HEADS)]
    pairs = [(b, p) for b in range(nchunk) for p in range(GLA_PAIRS)]
    padded = t_real < L
    valid = lax.broadcasted_iota(jnp.int32, (L, 1), 0) < t_real
    rr = lax.broadcasted_iota(jnp.int32, (L, L), 0)
    cc = lax.broadcasted_iota(jnp.int32, (L, L), 1)
    tril = cc <= rr
    c = GLA_SUB
    nb = L // c
    lane = lax.broadcasted_iota(jnp.int32, (1, LANES), 1)
    lane_blk = (lane % GLA_DK) // c
    lane_head = lane // GLA_DK
    v_lane_head = lax.broadcasted_iota(jnp.int32, (1, 2 * GLA_DV), 1) // GLA_DV
    row_blk = (lax.broadcasted_iota(jnp.int32, (2 * L, 1), 0) % L) // c
    t_in = lax.broadcasted_iota(jnp.int32, (1, c, 1), 1)
    new_states = [dict(c=[None] * ML_HEADS, n=[None] * ML_HEADS, m=[None] * ML_HEADS, s=[None] * GLA_HEADS)
                  for _ in range(nchunk)]
    prev = lambda b: new_states[b - 1] if chain and b > 0 else states[0 if chain else b]
    known = lambda b: not chain or b == 0

    carried = {}

    def carried_mlstm(b, h):
        st = prev(b)
        d = ml[(b, h)]
        carried[(b, h)] = _dot(d["qb"], st["c"][h].astype(BF16))

    def carried_gla(b, p):
        s_prev = prev(b)["s"]
        zero_blk = jnp.zeros((GLA_DK, GLA_DV), BF16)
        s_bd = jnp.concatenate(
            [jnp.concatenate([s_prev[2 * p].astype(BF16), zero_blk], axis=1),
             jnp.concatenate([zero_blk, s_prev[2 * p + 1].astype(BF16)], axis=1)], axis=0)
        carried[(b, "gla", p)] = _dot(gl[(b, p)]["q_dec"], s_bd)

    smalls, sms, lfs, b_cols, b_rows, sm_ts, las, bcs = [], [], [], [], [], [], [], []
    for b in range(nchunk):
        small = zgs[b][:, ZG_SMALL:ZG_SMALL + SMALL_W]
        sm = small + bias
        lf = _log_sigmoid(sm) * LOG2_E
        sm = sm * LOG2_E
        if padded:
            sm = jnp.where(valid, sm, -jnp.inf)
            lf = jnp.where(valid, lf, 0.0)
        smalls.append(small)
        sms.append(sm)
        lfs.append(lf)

    long_chunk = L > SUBLANES
    col_w = 1 if long_chunk else LANES
    tril_f, triu_f = tril.astype(F32), (rr <= cc).astype(F32)

    def cumsum_rows(x):
        if long_chunk:
            return _dot_f32(tril_f, x)
        acc = x[0:1]
        rows = [acc]
        for r in range(1, L):
            acc = acc + x[r:r + 1]
            rows.append(acc)
        return jnp.concatenate(rows, axis=0)

    lf_ts = [lf.T[0:SUBLANES] for lf in lfs] if long_chunk else None
    yield
    for b in range(nchunk):
        b_cols.append(cumsum_rows(lfs[b]))
        b_rows.append(_dot_f32(lf_ts[b], triu_f) if long_chunk else b_cols[b].T[0:SUBLANES])
        sm_ts.append(sms[b].T)
        la = _log_sigmoid(_dot(smalls[b].astype(BF16), wa2) + ba) * (LOG2_E / GLA_TAU)
        las.append(jnp.where(valid, la, 0.0) if padded else la)

    ml = {}
    for g in groups:
        b, h = g
        zq = zqs[b]
        qf = zq[:, ZQ_MQ + h * ML_DK:ZQ_MQ + (h + 1) * ML_DK].astype(F32)
        kf = zq[:, ZQ_MK + h * ML_DK:ZQ_MK + (h + 1) * ML_DK].astype(F32) * (ML_DK ** -0.5)
        vf = zq[:, ZQ_MV + h * ML_DV:ZQ_MV + (h + 1) * ML_DV].astype(F32)
        if padded:
            kf = jnp.where(valid, kf, 0.0)
            vf = jnp.where(valid, vf, 0.0)
        ml[g] = dict(qf=qf, qb=qf.astype(BF16), kf=kf, kb=kf.astype(BF16), vb=vf.astype(BF16))
    yield
    for b in range(nchunk):
        bcs.append(cumsum_rows(las[b]))
    for g in groups:
        d = ml[g]
        d["qk"] = _dot_nt(d["qb"], d["kb"])
        if known(g[0]):
            carried_mlstm(*g)

    yield
    for g in groups:
        b, h = g
        d = ml[g]
        i_col = jnp.broadcast_to(sms[b][:, h:h + 1], (L, col_w))
        b_col = jnp.broadcast_to(b_cols[b][:, ML_HEADS + h:ML_HEADS + h + 1], (L, col_w))
        b_row = b_rows[b][ML_HEADS + h:ML_HEADS + h + 1, :]
        i_row = sm_ts[b][h:h + 1, :]
        a_col = b_col + prev(b)["m"][h] * LOG2_E
        dm = jnp.where(tril, b_col[:, :L] - (b_row - i_row), -jnp.inf)
        mt = jnp.maximum(a_col, jnp.max(dm, axis=1, keepdims=True))
        w_inter = jnp.exp2(a_col - mt)
        s = d["qk"] * jnp.exp2(dm - mt[:, :L])
        kw = d["kf"] * jnp.exp2((b_col[L - 1:L] - mt[L - 1:L]) - (b_col - i_col))
        d.update(mt=mt, w_inter=w_inter, s=s, kw=kw, kw_t=kw.T.astype(BF16))
        new_states[b]["m"][h] = mt[L - 1:L, 0:1] * LN_2

    gl = {}
    for b in range(nchunk):
        zq = zqs[b]
        gq = zq[:, ZQ_GQ:ZQ_GQ + GLA_QK_W].astype(F32) * (GLA_DK ** -0.5)
        gk = zq[:, ZQ_GK:ZQ_GK + GLA_QK_W].astype(F32)
        gv = zq[:, ZQ_GV:ZQ_GV + GLA_V_W].astype(F32)
        if padded:
            gk = jnp.where(valid, gk, 0.0)
            gv = jnp.where(valid, gv, 0.0)
        stack = lambda x: jnp.concatenate([x[:, :LANES], x[:, LANES:]], axis=0)
        q2, k2, b2 = stack(gq), stack(gk), stack(bcs[b])
        q3 = q2.reshape(2 * nb, c, LANES)
        k3 = k2.reshape(2 * nb, c, LANES)
        b3 = b2.reshape(2 * nb, c, LANES)
        pair_terms = []
        for j in range(min(c, t_real)):
            decay = jnp.exp2(jnp.where(t_in >= j, b3 - b3[:, j:j + 1, :], -jnp.inf))
            pair_terms.append((q3 * k3[:, j:j + 1, :] * decay).reshape(2 * L, LANES).astype(BF16))
        kt2 = (k3 * jnp.exp2(b3[:, c - 1:c, :] - b3)).reshape(2 * L, LANES) if nb > 1 else None
        gl[b] = dict(gv=gv, q2=q2, k2=k2, b2=b2, pair_terms=pair_terms, kt2=kt2)
    yield
    for b in range(nchunk):
        acc = jnp.zeros((2 * L, LANES), F32)
        for j, pair_j in enumerate(gl[b]["pair_terms"]):
            acc = acc + _dot(pair_j, ee_ref[j])
        gl[b]["a_diag"] = jnp.where(lane_blk == row_blk, acc, 0.0)

    for bp in pairs:
        b, p = bp
        d = gl[b]
        rows_p = slice(p * L, (p + 1) * L)
        q_p, k_p, b_p = d["q2"][rows_p], d["k2"][rows_p], d["b2"][rows_p]
        if long_chunk:
            decay_col = jnp.exp2(b_p[L - SUBLANES:L].T[:, SUBLANES - 1:SUBLANES])
        else:
            decay_col = jnp.exp2(jnp.broadcast_to(b_p[L - 1:L], (LANES, LANES)).T)
        e = dict(q_dec=(q_p * jnp.exp2(b_p)).astype(BF16),
                 kh_t=(k_p * jnp.exp2(b_p[L - 1:L] - b_p)).T.astype(BF16),
                 decay_col=decay_col)
        if nb > 1:
            kt_p = d["kt2"][rows_p].astype(BF16)
            k_bd = jnp.concatenate([jnp.where(lane_head == hh, kt_p, 0.0) for hh in range(2)], axis=0)
            slabs, offs = [], []
            off = 0
            for j in range(nb - 1):
                lo = (j + 1) * c
                slabs.append(q_p[lo:] * jnp.exp2(b_p[lo:] - b_p[lo - 1:lo]))
                offs.append(off)
                off += L - lo
            e.update(q_var=jnp.concatenate(slabs, axis=0).astype(BF16), k_bd=k_bd, offs=offs)
        gl[bp] = e
    yield
    for bp in pairs:
        e = gl[bp]
        if nb > 1:
            e["r"] = _dot_nt(e["q_var"], e["k_bd"])
        if known(bp[0]):
            carried_gla(*bp)

    for g in groups:
        d = ml[g]
        d["sv"] = _dot(d["s"].astype(BF16), d["vb"])
        d["c_upd"] = _dot(d["kw_t"], d["vb"])
        d["den"] = jnp.sum(d["s"], axis=1, keepdims=True)
        if known(g[0]):
            d["qn"] = jnp.sum(d["qf"] * prev(g[0])["n"][g[1]], axis=1, keepdims=True)
    yield
    for bp in pairs:
        b, p = bp
        d, e = gl[b], gl[bp]
        a_p = d["a_diag"][p * L:(p + 1) * L]
        if nb > 1:
            blocks = []
            for i in range(nb):
                blk = a_p[i * c:(i + 1) * c]
                for j in range(i):
                    lo_r = e["offs"][j] + (i - j - 1) * c
                    blk = jnp.where(lane_blk == j, e["r"][lo_r:lo_r + c], blk)
                blocks.append(blk)
            a_p = jnp.concatenate(blocks, axis=0)
        v_f = d["gv"][:, p * 2 * GLA_DV:(p + 1) * 2 * GLA_DV]
        v_p = v_f.astype(BF16)
        if L < GLA_DK:
            v_rows = []
            for hh in range(2):
                v_rows += [jnp.where(v_lane_head == hh, v_f, 0.0), jnp.zeros((GLA_DK - L, 2 * GLA_DV), F32)]
            v_bd = jnp.concatenate(v_rows, axis=0).astype(BF16)
        else:
            v_bd = jnp.concatenate([jnp.where(v_lane_head == hh, v_p, 0.0) for hh in range(2)], axis=0)
        e["s_upd"] = [_dot(e["kh_t"][hh * GLA_DK:(hh + 1) * GLA_DK], v_p[:, hh * GLA_DV:(hh + 1) * GLA_DV])
                      for hh in range(2)]
        e.update(scores=a_p.astype(BF16), v_bd=v_bd)
    yield
    for bp in pairs:
        e = gl[bp]
        e["o_intra"] = _dot(e["scores"], e["v_bd"])

    yield
    ys = [[None] * (ML_HEADS + GLA_HEADS) for _ in range(nchunk)]

    def finish_mlstm(b):
        st = prev(b)
        for h in range(ML_HEADS):
            d = ml[(b, h)]
            mt, w_inter = d["mt"], d["w_inter"]
            qn = d["qn"] if known(b) else jnp.sum(d["qf"] * st["n"][h], axis=1, keepdims=True)
            den = d["den"] + w_inter * qn
            hh = (d["sv"] + w_inter * carried[(b, h)]) / jnp.maximum(jnp.abs(den), jnp.exp2(-mt))
            i_last = w_inter[L - 1:L]
            new_states[b]["c"][h] = i_last * st["c"][h] + d["c_upd"]
            new_states[b]["n"][h] = i_last * st["n"][h] + jnp.sum(d["kw"], axis=0, keepdims=True)
            hs = slice(h * ML_DV, (h + 1) * ML_DV)
            yn = hh * lax.rsqrt(jnp.mean(hh * hh, axis=-1, keepdims=True) + EPS) * gml[:, hs]
            ys[b][h] = jax.nn.sigmoid(zgs[b][:, ZG_MO + h * ML_DV:ZG_MO + (h + 1) * ML_DV]) * yn

    def finish_gla(b):
        st = prev(b)
        for p in range(GLA_PAIRS):
            e = gl[(b, p)]
            o = carried[(b, "gla", p)] + e["o_intra"]
            for hh in range(2):
                h = 2 * p + hh
                ds = slice(hh * GLA_DK, (hh + 1) * GLA_DK)
                vs = slice(hh * GLA_DV, (hh + 1) * GLA_DV)
                new_states[b]["s"][h] = e["decay_col"][ds] * st["s"][h] + e["s_upd"][hh]
                oh = o[:, vs]
                yn = oh * lax.rsqrt(jnp.mean(oh * oh, axis=-1, keepdims=True) + EPS) * ggl[:, h * GLA_DV:(h + 1) * GLA_DV]
                gg = zgs[b][:, ZG_GG + h * GLA_DV:ZG_GG + (h + 1) * GLA_DV]
                ys[b][ML_HEADS + h] = (gg * jax.nn.sigmoid(gg)) * yn

    if chain:
        for b in range(nchunk):
            if not known(b):
                for h in range(ML_HEADS):
                    carried_mlstm(b, h)
                for p in range(GLA_PAIRS):
                    carried_gla(b, p)
            finish_mlstm(b)
            finish_gla(b)
    else:
        for b in range(nchunk):
            finish_mlstm(b)
        for b in range(nchunk):
            finish_gla(b)
    return [jnp.concatenate(y, axis=1) for y in ys], new_states


def _drain(gen):
    try:
        while True:
            next(gen)
    except StopIteration as done:
        return done.value


def _mixer_compute(*args, **kwargs):
    return _drain(_mixer_stages(*args, **kwargs))


def _mixer_state_kernel(zq_ref, zg_ref, bias_ref, wa2_ref, ba_ref, gml_ref, ggl_ref, ee_ref, c0_ref, n0_ref, m0_ref,
                        s0_ref, y_ref, co_ref, no_ref, mo_ref, so_ref, *, L, t_real, nblk):
    consts = (bias_ref[...], wa2_ref[...], ba_ref[...], gml_ref[...], ggl_ref[...])
    lane = lax.broadcasted_iota(jnp.int32, (1, LANES), 1)

    def chunk_rows(ref, b):
        rows = ref[b * t_real:(b + 1) * t_real, :]
        return jnp.concatenate([rows, jnp.zeros((L - t_real, rows.shape[1]), rows.dtype)], axis=0)

    states = []
    for b in range(nblk):
        n_all = n0_ref[b]
        m_all = m0_ref[b]
        states.append(dict(c=[c0_ref[b, h] for h in range(ML_HEADS)],
                           n=[n_all[h:h + 1, :] for h in range(ML_HEADS)],
                           m=[m_all[:, h:h + 1] for h in range(ML_HEADS)],
                           s=[s0_ref[b, h] for h in range(GLA_HEADS)]))
    ys, new_states = _mixer_compute([chunk_rows(zq_ref, b) for b in range(nblk)],
                                    [chunk_rows(zg_ref, b) for b in range(nblk)], states, consts, ee_ref, L=L,
                                    t_real=t_real, chain=False)
    for b in range(nblk):
        st = new_states[b]
        y_ref[b * t_real:(b + 1) * t_real, :] = ys[b][0:t_real].astype(y_ref.dtype)
        for h in range(ML_HEADS):
            co_ref[b, h] = st["c"][h]
            so_ref[b, h] = st["s"][h]
        no_ref[b] = jnp.concatenate(st["n"], axis=0)
        m_row = jnp.zeros((1, LANES), F32)
        for h in range(ML_HEADS):
            m_row = jnp.where(lane == h, st["m"][h], m_row)
        mo_ref[b] = m_row[:, 0:ML_HEADS]


def _mixer_state(zq, zg, w, state, t_real, nblk):
    bsz = zq.shape[0] // t_real
    L = -(-t_real // SUBLANES) * SUBLANES
    rows = lambda width: pl.BlockSpec((nblk * t_real, width), lambda b: (b, 0))
    per_b = lambda *tail: pl.BlockSpec((nblk,) + tail, lambda b: (b,) + (0,) * len(tail))
    state_specs = [per_b(ML_HEADS, ML_DK, ML_DV), per_b(ML_HEADS, ML_DK), per_b(1, ML_HEADS),
                   per_b(GLA_HEADS, GLA_DK, GLA_DV)]
    consts = [_const_spec((1, SMALL_W)), _const_spec((SMALL_W, GLA_QK_W)), _const_spec((1, GLA_QK_W)),
              _const_spec((1, ML_HEADS * ML_DV)), _const_spec((1, GLA_V_W)), _const_spec((GLA_SUB, LANES, LANES))]
    return pl.pallas_call(
        functools.partial(_mixer_state_kernel, L=L, t_real=t_real, nblk=nblk),
        grid=(bsz // nblk,),
        in_specs=[rows(ZQ_W), rows(ZG_W)] + consts + state_specs,
        out_specs=[rows(D_MODEL)] + state_specs,
        out_shape=[jax.ShapeDtypeStruct((bsz * t_real, D_MODEL), F32),
                   jax.ShapeDtypeStruct((bsz, ML_HEADS, ML_DK, ML_DV), F32),
                   jax.ShapeDtypeStruct((bsz, ML_HEADS, ML_DK), F32),
                   jax.ShapeDtypeStruct((bsz, 1, ML_HEADS), F32),
                   jax.ShapeDtypeStruct((bsz, GLA_HEADS, GLA_DK, GLA_DV), F32)],
        compiler_params=_params(("arbitrary",)),
        name="mixer_state",
    )(zq, zg, w["gate_bias"], w["w_a2"], w["b_a"], w["mlstm_out_g"], w["gla_out_g"], w["diag_sum"], *state)


def _ffn_in_pieces(x, g1_ref, wg_ref, wu_ref, wd_ref, gm_ref, wq_ref, wgt_ref, x1_ref, zq_out, zg_out):
    h = _rms(x, g1_ref[...]).astype(BF16)
    acts = []
    for lo in range(0, D_FF, MXU_WIDTH):
        g = _dot(h, wg_ref[:, lo:lo + MXU_WIDTH])
        yield
        u = _dot(h, wu_ref[:, lo:lo + MXU_WIDTH])
        yield
        acts.append(((g * jax.nn.sigmoid(g)) * u).astype(BF16))
    a = jnp.concatenate(acts, axis=1)
    down = []
    for lo in range(0, D_MODEL, MXU_WIDTH):
        down.append(_dot(a, wd_ref[:, lo:lo + MXU_WIDTH]))
        yield
    x1 = x + 0.5 * jnp.concatenate(down, axis=1)
    x1_ref[...] = x1
    hm = _rms(x1, gm_ref[...]).astype(BF16)
    for lo in range(0, ZQ_W, MXU_WIDTH):
        zq_out[:, lo:lo + MXU_WIDTH] = _dot(hm, wq_ref[:, lo:lo + MXU_WIDTH]).astype(zq_out.dtype)
        yield
    for lo in range(0, ZG_W, MXU_WIDTH):
        hi = min(lo + MXU_WIDTH, ZG_W)
        zg_out[:, lo:hi] = _dot(hm, wgt_ref[:, lo:hi])
        yield


ATTN_STAGE_EVERY = 7
FFN_PIECES_PER_MIXER_STAGE = 1


def _ffn_mix_kernel(*refs, tiles_per_batch, n_cast):
    it = iter(refs)
    (x_ref, g1_ref, wg_ref, wu_ref, wd_ref, gm_ref, wq_ref, wgt_ref, bias_ref, wa2_ref, ba_ref, gml_ref, ggl_ref,
     ee_ref, qs_ref, ks_ref, vs_ref) = (next(it) for _ in range(17))
    cast_src = [next(it) for _ in range(n_cast)]
    x1_ref, ym_ref, co_ref, no_ref, mo_ref, so_ref, os_ref = (next(it) for _ in range(7))
    cast_dst = [next(it) for _ in range(n_cast)]
    zq_s, zg_s, c_s, n_s, m_s, s_s = (next(it) for _ in range(6))
    i = pl.program_id(0)
    for src, dst in zip(cast_src, cast_dst):
        dst[...] = src[...].astype(dst.dtype)

    @pl.when(i == 0)
    def _init():
        for ref in (zq_s, zg_s, c_s, n_s, m_s, s_s):
            ref[...] = jnp.zeros_like(ref)

    consts = (bias_ref[...], wa2_ref[...], ba_ref[...], gml_ref[...], ggl_ref[...])
    n_chunks = zq_s.shape[0] // CHUNK
    rows = [slice(k * CHUNK, (k + 1) * CHUNK) for k in range(n_chunks)]
    zqs = [zq_s[r, :] for r in rows]
    zgs = [zg_s[r, :] for r in rows]
    starts_batch = (i - 1) % tiles_per_batch == 0
    carry = lambda v: jnp.where(starts_batch, 0.0, v)
    state = dict(c=[carry(c_s[h]) for h in range(ML_HEADS)], n=[carry(n_s[h:h + 1, :]) for h in range(ML_HEADS)],
                 m=[carry(m_s[h:h + 1, 0:1]) for h in range(ML_HEADS)], s=[carry(s_s[h]) for h in range(GLA_HEADS)])

    def mixers():
        st = state
        for k in range(n_chunks):
            ys, new = yield from _mixer_stages([zqs[k]], [zgs[k]], [st], consts, ee_ref, L=CHUNK, t_real=CHUNK,
                                               chain=True)
            ym_ref[rows[k], :] = ys[0].astype(ym_ref.dtype)
            st = new[0]
            yield
        return st

    ffn = _ffn_in_pieces(x_ref[...], g1_ref, wg_ref, wu_ref, wd_ref, gm_ref, wq_ref, wgt_ref, x1_ref, zq_s, zg_s)
    mix = mixers()
    attn = _xattn_cache_stages(qs_ref, ks_ref, vs_ref, os_ref)
    live = {"ffn": True, "attn": True}

    def advance(name, gen):
        if live[name]:
            try:
                next(gen)
            except StopIteration:
                live[name] = False

    slot = 0
    while True:
        try:
            next(mix)
        except StopIteration as done:
            last = done.value
            break
        for _ in range(FFN_PIECES_PER_MIXER_STAGE):
            advance("ffn", ffn)
        if slot % ATTN_STAGE_EVERY == ATTN_STAGE_EVERY // 2:
            advance("attn", attn)
        slot += 1
    for name, gen in (("ffn", ffn), ("attn", attn)):
        if live[name]:
            _drain(gen)

    for h in range(ML_HEADS):
        c_s[h] = last["c"][h]
        n_s[h:h + 1, :] = last["n"][h]
        m_s[h:h + 1, :] = jnp.broadcast_to(last["m"][h], (1, LANES))
        s_s[h] = last["s"][h]

    @pl.when(jnp.logical_and(i >= 1, (i - 1) % tiles_per_batch == tiles_per_batch - 1))
    def _emit_state():
        lane = lax.broadcasted_iota(jnp.int32, (1, LANES), 1)
        for h in range(ML_HEADS):
            co_ref[0, h] = last["c"][h]
            so_ref[0, h] = last["s"][h]
        no_ref[0] = jnp.concatenate(last["n"], axis=0)
        m_row = jnp.zeros((1, LANES), F32)
        for h in range(ML_HEADS):
            m_row = jnp.where(lane == h, last["m"][h], m_row)
        mo_ref[0] = m_row[:, 0:ML_HEADS]


def _ffn_mix(x, w, tm, q_s, k_cache, v_cache, to_cast):
    bsz, t, _ = x.shape
    tiles_per_batch = t // tm
    n_tiles = bsz * tiles_per_batch
    assert t % tm == 0 and tm % CHUNK == 0
    bs = k_cache.shape[0]
    ts = q_s.shape[0] // bs
    assert bs % n_tiles == 0
    sb = bs // n_tiles
    cur_tile = lambda i: jnp.minimum(i, n_tiles - 1)
    prev_tile = lambda i: jnp.maximum(i - 1, 0)
    cur = lambda width: pl.BlockSpec((tm, width), lambda i: (cur_tile(i), 0))
    per_b = lambda *tail: pl.BlockSpec((1,) + tail, lambda i: (prev_tile(i) // tiles_per_batch,) + (0,) * len(tail))
    state_specs = [per_b(ML_HEADS, ML_DK, ML_DV), per_b(ML_HEADS, ML_DK), per_b(1, ML_HEADS),
                   per_b(GLA_HEADS, GLA_DK, GLA_DV)]
    qo_s = pl.BlockSpec((sb * ts, D_MODEL), lambda i: (cur_tile(i), 0))
    kv_s = pl.BlockSpec((sb, N_MEM * CACHE_ROW_GROUP, LANES), lambda i: (cur_tile(i), 0, 0))
    cast_specs = []
    for a in to_cast:
        rb = next(r for r in range(BF16_SUBLANES, a.shape[0] + 1, BF16_SUBLANES)
                  if a.shape[0] % r == 0 and a.shape[0] // r <= n_tiles)
        cast_specs.append(pl.BlockSpec((rb, a.shape[1]), lambda i, last=a.shape[0] // rb - 1: (jnp.minimum(i, last), 0)))
    x1, ym, c_new, n_new, m_new, s_new, o_s, *cast = pl.pallas_call(
        functools.partial(_ffn_mix_kernel, tiles_per_batch=tiles_per_batch, n_cast=len(to_cast)),
        grid=(n_tiles + 1,),
        in_specs=[cur(D_MODEL), _const_spec((1, D_MODEL)), _const_spec((D_MODEL, D_FF)),
                  _const_spec((D_MODEL, D_FF)), _const_spec((D_FF, D_MODEL)), _const_spec((1, D_MODEL)),
                  _const_spec((D_MODEL, ZQ_W)), _const_spec((D_MODEL, ZG_W)), _const_spec((1, SMALL_W)),
                  _const_spec((SMALL_W, GLA_QK_W)), _const_spec((1, GLA_QK_W)), _const_spec((1, ML_HEADS * ML_DV)),
                  _const_spec((1, GLA_V_W)), _const_spec((GLA_SUB, LANES, LANES)), qo_s, kv_s, kv_s] + cast_specs,
        out_specs=([cur(D_MODEL), pl.BlockSpec((tm, D_MODEL), lambda i: (prev_tile(i), 0))] + state_specs + [qo_s]
                   + cast_specs),
        out_shape=[jax.ShapeDtypeStruct((n_tiles * tm, D_MODEL), F32),
                   jax.ShapeDtypeStruct((n_tiles * tm, D_MODEL), BF16),
                   jax.ShapeDtypeStruct((bsz, ML_HEADS, ML_DK, ML_DV), F32),
                   jax.ShapeDtypeStruct((bsz, ML_HEADS, ML_DK), F32),
                   jax.ShapeDtypeStruct((bsz, 1, ML_HEADS), F32),
                   jax.ShapeDtypeStruct((bsz, GLA_HEADS, GLA_DK, GLA_DV), F32),
                   jax.ShapeDtypeStruct((bs * ts, D_MODEL), F32)] + [jax.ShapeDtypeStruct(a.shape, BF16) for a in to_cast],
        scratch_shapes=[pltpu.VMEM((tm, ZQ_W), BF16), pltpu.VMEM((tm, ZG_W), F32),
                        pltpu.VMEM((ML_HEADS, ML_DK, ML_DV), F32), pltpu.VMEM((SUBLANES, LANES), F32),
                        pltpu.VMEM((SUBLANES, LANES), F32), pltpu.VMEM((GLA_HEADS, GLA_DK, GLA_DV), F32)],
        compiler_params=_params(("arbitrary",)),
        name="ffn_mix",
    )(x.reshape(bsz * t, D_MODEL), w["ffn1_g"], w["ffn1_wg"], w["ffn1_wu"], w["ffn1_wd"], w["mix_g"], w["w_in_q"],
      w["w_in_g"], w["gate_bias"], w["w_a2"], w["b_a"], w["mlstm_out_g"], w["gla_out_g"], w["diag_sum"],
      q_s, _cache_rows_view(k_cache), _cache_rows_view(v_cache), *to_cast)
    return x1, ym, (c_new[None], n_new[None], m_new.reshape(1, bsz, ML_HEADS), s_new[None]), o_s, cast


def _post_mix_kernel(x1_ref, ym_ref, wout_ref, gx_ref, wq_ref, x2_ref, q_ref):
    x2 = x1_ref[...] + _dot(ym_ref[...].astype(BF16), wout_ref[...])
    x2_ref[...] = x2
    hq = _rms(x2, gx_ref[...]).astype(BF16)
    q_ref[...] = _dot(hq, wq_ref[...]).astype(q_ref.dtype)


def _post_mix(x1, ym, w, tm, q_dtype):
    n = x1.shape[0]
    row = lambda: pl.BlockSpec((tm, D_MODEL), lambda i: (i, 0))
    return pl.pallas_call(
        _post_mix_kernel,
        grid=(n // tm,),
        in_specs=[row(), row(), _const_spec((D_MODEL, D_MODEL)), _const_spec((1, D_MODEL)),
                  _const_spec((D_MODEL, D_MODEL))],
        out_specs=[row(), row()],
        out_shape=[jax.ShapeDtypeStruct((n, D_MODEL), F32), jax.ShapeDtypeStruct((n, D_MODEL), q_dtype)],
        compiler_params=_params(("arbitrary",)),
        name="post_mix",
    )(x1, ym, w["w_out"], w["xattn_g"], w["xattn_wq"])


def _softmax(s):
    e = jnp.exp(s - jnp.max(s, axis=-1, keepdims=True))
    return e / jnp.sum(e, axis=-1, keepdims=True)


def _post_fused_kernel(x1_ref, ym_ref, mem_ref, gm_ref, wk_ref, wv_ref, wout_ref, gx_ref, wq_ref, wo_ref, g2_ref,
                       wg_ref, wu_ref, wd_ref, gf_ref, y_ref, ko_ref, vo_ref, k_s, v_s):
    @pl.when(pl.program_id(1) == 0)
    def _memkv():
        hn = _rms(mem_ref[0], gm_ref[...]).astype(BF16)
        for w_ref, out_ref, s_ref in ((wk_ref, ko_ref, k_s), (wv_ref, vo_ref, v_s)):
            kv = _dot(hn, w_ref[...])
            _cache_rows_store(out_ref, 0, kv)
            s_ref[...] = kv.astype(BF16)

    x2 = x1_ref[0] + _dot(ym_ref[0].astype(BF16), wout_ref[...])
    q = _dot(_rms(x2, gx_ref[...]).astype(BF16), wq_ref[...]).astype(BF16)
    heads = [slice(h * XA_DH, (h + 1) * XA_DH) for h in range(XA_HEADS)]
    k_full, v_full = k_s[...], v_s[...]
    scores = [_dot_nt(q[:, hs], k_full[:, hs]) * (XA_DH ** -0.5) for hs in heads]
    probs = [_softmax(s).astype(BF16) for s in scores]
    o = jnp.concatenate([_dot(p, v_full[:, hs]).astype(BF16) for hs, p in zip(heads, probs)], axis=1)
    x3 = x2 + _dot(o, wo_ref[...])
    x4 = _swiglu_residual(x3, g2_ref, wg_ref, wu_ref, wd_ref)
    y_ref[0] = _rms(x4, gf_ref[...])


def _post_fused(x1, ym, mem, w, tm):
    bsz, t = x1.shape[0], x1.shape[1]
    row = lambda: pl.BlockSpec((1, tm, D_MODEL), lambda b, j: (b, j, 0))
    kv = pl.BlockSpec((1, N_MEM * CACHE_ROW_GROUP, LANES), lambda b, j: (b, 0, 0))
    sq = _const_spec((D_MODEL, D_MODEL))
    vec = _const_spec((1, D_MODEL))
    kv_shape = jax.ShapeDtypeStruct((bsz, N_MEM * CACHE_ROW_GROUP, LANES), F32)
    return pl.pallas_call(
        _post_fused_kernel,
        grid=(bsz, t // tm),
        in_specs=[row(), row(), pl.BlockSpec((1, N_MEM, D_MODEL), lambda b, j: (b, 0, 0)), vec, sq, sq,
                  sq, vec, sq, sq, vec, _const_spec((D_MODEL, D_FF)),
                  _const_spec((D_MODEL, D_FF)), _const_spec((D_FF, D_MODEL)), vec],
        out_specs=[row(), kv, kv],
        out_shape=[jax.ShapeDtypeStruct((bsz, t, D_MODEL), F32), kv_shape, kv_shape],
        scratch_shapes=[pltpu.VMEM((N_MEM, D_MODEL), BF16)] * 2,
        compiler_params=_params(("arbitrary", "arbitrary")),
        name="post_fused",
    )(x1, ym, mem, w["mem_g"], w["xattn_wk"], w["xattn_wv"], w["w_out"], w["xattn_g"], w["xattn_wq"],
      w["xattn_wo"], w["ffn2_g"], w["ffn2_wg"], w["ffn2_wu"], w["ffn2_wd"], w["final_g"])


XA_LANE_TILES = XA_DH // LANES
CACHE_ROW_GROUP = XA_LANE_TILES * XA_HEADS


def _cache_rows_view(x):
    bsz = x.shape[0]
    x = x.reshape(bsz, N_MEM, XA_HEADS, XA_LANE_TILES, LANES)
    return x.transpose(0, 1, 3, 2, 4).reshape(bsz, N_MEM * CACHE_ROW_GROUP, LANES)


def _cache_rows_unview(x):
    bsz = x.shape[0]
    x = x.reshape(bsz, N_MEM, XA_LANE_TILES, XA_HEADS, LANES)
    return x.transpose(0, 1, 3, 2, 4).reshape(bsz, N_MEM, XA_HEADS, XA_DH)


def _cache_rows_store(ref, b, x):
    for h in range(XA_HEADS):
        for lt in range(XA_LANE_TILES):
            lo = h * XA_DH + lt * LANES
            ref[b, pl.ds(lt * XA_HEADS + h, N_MEM, stride=CACHE_ROW_GROUP), :] = x[:, lo:lo + LANES]


def _cache_rows_load(ref, b):
    cols = [ref[b, pl.ds(lt * XA_HEADS + h, N_MEM, stride=CACHE_ROW_GROUP), :]
            for h in range(XA_HEADS) for lt in range(XA_LANE_TILES)]
    return jnp.concatenate(cols, axis=1).astype(BF16)


def _xattn_cache_stages(q_ref, k_ref, v_ref, o_ref):
    bb = k_ref.shape[0]
    tq = q_ref.shape[0] // bb
    lane_head = lax.broadcasted_iota(jnp.int32, (1, D_MODEL), 1) // XA_DH
    qs = [q_ref[b * tq:(b + 1) * tq, :] for b in range(bb)]
    q_bds = [jnp.concatenate([jnp.where(lane_head == h, q, 0.0) for h in range(XA_HEADS)], axis=0).astype(BF16)
             for q in qs]
    k_fulls = [_cache_rows_load(k_ref, b) for b in range(bb)]
    yield
    scores = [_dot_nt(q_bds[b], k_fulls[b]) * (XA_DH ** -0.5) for b in range(bb)]
    yield
    p_all = _softmax(jnp.concatenate(scores, axis=0)).astype(BF16)
    v_fulls = [_cache_rows_load(v_ref, b) for b in range(bb)]
    yield
    rows = XA_HEADS * tq
    o_fulls = [_dot(p_all[b * rows:(b + 1) * rows], v_fulls[b]) for b in range(bb)]
    yield
    for b in range(bb):
        o = jnp.zeros((tq, D_MODEL), F32)
        for h in range(XA_HEADS):
            o = jnp.where(lane_head == h, o_fulls[b][h * tq:(h + 1) * tq], o)
        o_ref[b * tq:(b + 1) * tq, :] = o.astype(o_ref.dtype)


def _ffn_out_stream_kernel(x2_ref, o_ref, wo_ref, g2_ref, wg_ref, wu_ref, wd_ref, gf_ref, y_ref, x3_s, h_s, acc_s):
    j = pl.program_id(0)

    @pl.when(j == 0)
    def _attn_out():
        x3 = x2_ref[...] + _dot(o_ref[...].astype(BF16), wo_ref[...])
        x3_s[...] = x3
        h_s[...] = _rms(x3, g2_ref[...]).astype(BF16)
        acc_s[...] = jnp.zeros_like(acc_s)

    h = h_s[...]
    g = _dot(h, wg_ref[...])
    u = _dot(h, wu_ref[...])
    acc_s[...] += _dot(((g * jax.nn.sigmoid(g)) * u).astype(BF16), wd_ref[...])

    @pl.when(j == pl.num_programs(0) - 1)
    def _final_norm():
        y_ref[...] = _rms(x3_s[...] + 0.5 * acc_s[...], gf_ref[...]).reshape(y_ref.shape)


def _ffn_out_stream(x2, o, w, bsz):
    n = x2.shape[0]
    y_shape = (bsz, n // bsz, D_MODEL)
    ff_cols = pl.BlockSpec((D_MODEL, MXU_WIDTH), lambda j: (0, j))
    ff_rows = pl.BlockSpec((MXU_WIDTH, D_MODEL), lambda j: (j, 0))
    return pl.pallas_call(
        _ffn_out_stream_kernel,
        grid=(D_FF // MXU_WIDTH,),
        in_specs=[_const_spec((n, D_MODEL)), _const_spec((n, D_MODEL)), _const_spec((D_MODEL, D_MODEL)),
                  _const_spec((1, D_MODEL)), ff_cols, ff_cols, ff_rows, _const_spec((1, D_MODEL))],
        out_specs=pl.BlockSpec(y_shape, lambda j: (0, 0, 0)),
        out_shape=jax.ShapeDtypeStruct(y_shape, F32),
        scratch_shapes=[pltpu.VMEM((n, D_MODEL), F32), pltpu.VMEM((n, D_MODEL), BF16),
                        pltpu.VMEM((n, D_MODEL), F32)],
        compiler_params=_params(("arbitrary",)),
        name="ffn_out",
    )(x2, o, w["xattn_wo"], w["ffn2_g"], w["ffn2_wg"], w["ffn2_wu"], w["ffn2_wd"], w["final_g"])


IN_SIZES = (("mq", ML_HEADS * ML_DK), ("mk", ML_HEADS * ML_DK), ("mv", ML_HEADS * ML_DV), ("mi", ML_HEADS),
            ("mf", ML_HEADS), ("mo", ML_HEADS * ML_DV), ("gq", GLA_QK_W), ("gk", GLA_QK_W), ("gv", GLA_V_W),
            ("ga", GLA_RANK), ("gg", GLA_V_W))
IN_OFFSET = {name: sum(width for _, width in IN_SIZES[:i]) for i, (name, _) in enumerate(IN_SIZES)}
D_IN = sum(width for _, width in IN_SIZES)
IN_MOVES = ((0, ZQ_MQ, IN_OFFSET["mq"], ZQ_GQ - ZQ_MQ), (0, ZQ_GQ, IN_OFFSET["gq"], ZQ_W - ZQ_GQ),
            (1, ZG_MO, IN_OFFSET["mo"], ZG_GG - ZG_MO), (1, ZG_GG, IN_OFFSET["gg"], ZG_SMALL - ZG_GG))
TM_REGROUP = MXU_WIDTH
assert IN_OFFSET["mf"] == IN_OFFSET["mi"] + ML_HEADS
assert all(IN_OFFSET[name] % SUBLANES == 0 for name in ("mq", "mi", "mo", "gq", "ga", "gg"))


def _w_in_regroup_kernel(wt_ref, wa2_ref, q_ref, g_ref, wa2_out):
    narrow = 2 * ML_HEADS + GLA_RANK
    wa2_out[...] = jnp.concatenate([jnp.zeros((2 * ML_HEADS, GLA_QK_W), F32), wa2_ref[...],
                                    jnp.zeros((SMALL_W - narrow, GLA_QK_W), F32)], axis=0).astype(BF16)

    def put(out_ref, dst, rows_t):
        out_ref[:, dst:dst + LANES] = rows_t.T.astype(BF16)

    for slab, dst, src, width in IN_MOVES:
        for off in range(0, width, LANES):
            put((q_ref, g_ref)[slab], dst + off, wt_ref[src + off:src + off + LANES, :])
    put(g_ref, ZG_SMALL, jnp.concatenate(
        [wt_ref[IN_OFFSET["mi"]:IN_OFFSET["mi"] + 2 * ML_HEADS, :],
         wt_ref[IN_OFFSET["ga"]:IN_OFFSET["ga"] + GLA_RANK, :],
         jnp.zeros((SMALL_W - narrow, wt_ref.shape[1]), F32)], axis=0))


def _w_in_regroup(w_in, w_a2):
    assert w_in.shape == (D_MODEL, D_IN)
    rows = lambda width: pl.BlockSpec((TM_REGROUP, width), lambda i: (i, 0))
    return pl.pallas_call(
        _w_in_regroup_kernel,
        grid=(D_MODEL // TM_REGROUP,),
        in_specs=[pl.BlockSpec((D_IN, TM_REGROUP), lambda i: (0, i)), _const_spec((GLA_RANK, GLA_QK_W))],
        out_specs=[rows(ZQ_W), rows(ZG_W), pl.BlockSpec((SMALL_W, GLA_QK_W), lambda i: (0, 0))],
        out_shape=[jax.ShapeDtypeStruct((D_MODEL, ZQ_W), BF16), jax.ShapeDtypeStruct((D_MODEL, ZG_W), BF16),
                   jax.ShapeDtypeStruct((SMALL_W, GLA_QK_W), BF16)],
        compiler_params=_params(("arbitrary",)),
        name="w_in_regroup",
    )(w_in.T, w_a2)


def _prep_weights(p):
    row = lambda a: a.reshape(1, -1).astype(F32)
    w_in_q, w_in_g, w_a2 = _w_in_regroup(p["w_in"], p["gla_w_a2"])
    gate_bias = jnp.concatenate([p["mlstm_b_i"], p["mlstm_b_f"], jnp.zeros((SMALL_W - 2 * ML_HEADS,), F32)])
    lane = np.arange(LANES)
    same_head = (lane[:, None] // GLA_DK) == (lane[None, :] // GLA_DK)
    diag_sum = jnp.asarray(np.stack([same_head & ((lane[None, :] % GLA_SUB) == j) for j in range(GLA_SUB)]), BF16)
    return dict(
        ffn1_wg=p["ffn1_w_gate"], ffn1_wu=p["ffn1_w_up"], ffn1_wd=p["ffn1_w_down"],
        ffn1_g=row(p["ffn1_norm_g"]), mix_g=row(p["mix_norm_g"]), w_in_q=w_in_q, w_in_g=w_in_g,
        gate_bias=row(gate_bias), w_a2=w_a2, b_a=row(p["gla_b_a"]), mlstm_out_g=row(p["mlstm_out_g"]),
        gla_out_g=row(p["gla_out_g"]), diag_sum=diag_sum,
        w_out=p["w_out"], xattn_g=row(p["xattn_norm_g"]), xattn_wq=p["xattn_w_q"],
        mem_g=row(p["mem_norm_g"]), ffn2_g=row(p["ffn2_norm_g"]), final_g=row(p["final_g"]),
        xattn_wo=p["xattn_w_o"], xattn_wk=p["xattn_w_k"], xattn_wv=p["xattn_w_v"], ffn2_wg=p["ffn2_w_gate"],
        ffn2_wu=p["ffn2_w_up"], ffn2_wd=p["ffn2_w_down"])


EARLY_WEIGHTS = ("ffn1_wg", "ffn1_wu", "ffn1_wd", "w_out", "xattn_wq")
LATE_WEIGHTS = ("xattn_wo", "xattn_wk", "xattn_wv", "ffn2_wg", "ffn2_wu", "ffn2_wd")


def kernel(x_prompt, x_sample, mem_prompt, cache_mem_k, cache_mem_v, state_mlstm_c, state_mlstm_n, state_mlstm_m, state_gla_s, ffn1_norm_g, ffn1_w_gate, ffn1_w_up, ffn1_w_down, mix_norm_g, w_in, mlstm_b_i, mlstm_b_f, mlstm_out_g, gla_w_a2, gla_b_a, gla_out_g, w_out, xattn_norm_g, mem_norm_g, xattn_w_q, xattn_w_k, xattn_w_v, xattn_w_o, ffn2_norm_g, ffn2_w_gate, ffn2_w_up, ffn2_w_down, final_norm_g):
    assert ffn1_norm_g.shape[0] == 1, "single-layer stack"
    layer = dict(ffn1_norm_g=ffn1_norm_g, ffn1_w_gate=ffn1_w_gate, ffn1_w_up=ffn1_w_up, ffn1_w_down=ffn1_w_down,
                 mix_norm_g=mix_norm_g, w_in=w_in, mlstm_b_i=mlstm_b_i, mlstm_b_f=mlstm_b_f,
                 mlstm_out_g=mlstm_out_g, gla_w_a2=gla_w_a2, gla_b_a=gla_b_a, gla_out_g=gla_out_g, w_out=w_out,
                 xattn_norm_g=xattn_norm_g, mem_norm_g=mem_norm_g, xattn_w_q=xattn_w_q, xattn_w_k=xattn_w_k,
                 xattn_w_v=xattn_w_v, xattn_w_o=xattn_w_o, ffn2_norm_g=ffn2_norm_g, ffn2_w_gate=ffn2_w_gate,
                 ffn2_w_up=ffn2_w_up, ffn2_w_down=ffn2_w_down)
    p = {name: arr[0] for name, arr in layer.items()}
    p["final_g"] = final_norm_g
    w = _prep_weights(p)

    bp, tp, _ = x_prompt.shape
    bs, ts, _ = x_sample.shape

    state = (state_mlstm_c[0], state_mlstm_n[0], state_mlstm_m[0].reshape(bs, 1, ML_HEADS), state_gla_s[0])
    x1_s, zq_s, zg_s, *early = _ffn_in_stream(x_sample, w, F32)
    w.update(zip(EARLY_WEIGHTS, early))
    ym_s, c_s, n_s, m_s, s_s = _mixer_state(zq_s, zg_s, w, state, ts, SAMPLE_MIXER_BATCHES)
    states_s = (c_s[None], n_s[None], m_s.reshape(1, bs, ML_HEADS), s_s[None])
    x2_s, q_s = _post_mix(x1_s, ym_s, w, TM_FFN_IN, F32)

    x1_p, ym_p, states_p, o_s, late = _ffn_mix(x_prompt, w, TM_FFN_IN, q_s, cache_mem_k[0], cache_mem_v[0],
                                               [w[name] for name in LATE_WEIGHTS])
    w.update(zip(LATE_WEIGHTS, late))
    y_p, mem_k_p, mem_v_p = _post_fused(x1_p.reshape(bp, tp, D_MODEL), ym_p.reshape(bp, tp, D_MODEL), mem_prompt, w,
                                        TM_POST)
    y_s = _ffn_out_stream(x2_s, o_s, w, bs)

    return (y_p, y_s, _cache_rows_unview(mem_k_p)[None], _cache_rows_unview(mem_v_p)[None]) + states_p + states_s
```

```python
import functools

import jax
import jax.numpy as jnp
import numpy as np
from jax import lax
from jax.experimental import pallas as pl
from jax.experimental.pallas import tpu as pltpu

F32 = jnp.float32
BF16 = jnp.bfloat16

D_MODEL = 1024
D_FF = 2816
ML_HEADS = 4
ML_DK = 128
ML_DV = 128
GLA_HEADS = 4
GLA_DK = 64
GLA_DV = 128
GLA_RANK = 16
GLA_TAU = 16.0
N_MEM = 256
XA_HEADS = 4
XA_DH = D_MODEL // XA_HEADS
EPS = 1e-6
CHUNK = 64
LOG2_E = 1.4426950408889634
LN_2 = 0.6931471805599453
LANES = 128
SUBLANES = 8
BF16_SUBLANES = 2 * SUBLANES

ZQ_MQ, ZQ_MK, ZQ_MV = 0, 512, 1024
ZQ_GQ, ZQ_GK, ZQ_GV = 1536, 1792, 2048
ZQ_W = 2560
ZG_MO, ZG_GG, ZG_SMALL = 0, 512, 1024
ZG_W = 1152
SMALL_W = LANES
GLA_QK_W = GLA_HEADS * GLA_DK
GLA_V_W = GLA_HEADS * GLA_DV
GLA_PAIRS = GLA_HEADS // 2
GLA_SUB = SUBLANES

MXU_WIDTH = 256
FF_CHUNKS = ((0, 6 * MXU_WIDTH), (6 * MXU_WIDTH, D_FF))
VMEM_LIMIT_BYTES = 56 * 1024 * 1024

TM_FFN_IN = MXU_WIDTH
TM_POST = 2 * MXU_WIDTH
SAMPLE_MIXER_BATCHES = 16


def _rms(x, g):
    return x * lax.rsqrt(jnp.mean(x * x, axis=-1, keepdims=True) + EPS) * g


def _log_sigmoid(x):
    return jnp.minimum(x, 0.0) - jnp.log1p(jnp.exp(-jnp.abs(x)))


def _dot(a, b):
    return jnp.dot(a, b, preferred_element_type=F32)


def _dot_nt(a, b):
    return lax.dot_general(a, b, (((1,), (1,)), ((), ())), preferred_element_type=F32)


def _dot_f32(a, b):
    return jnp.dot(a, b, precision=lax.Precision.HIGHEST, preferred_element_type=F32)


def _swiglu_residual(x, g_ref, wg_ref, wu_ref, wd_ref):
    h = _rms(x, g_ref[...]).astype(BF16)
    acts = []
    for lo, hi in FF_CHUNKS:
        g = _dot(h, wg_ref[:, lo:hi])
        u = _dot(h, wu_ref[:, lo:hi])
        acts.append(((g * jax.nn.sigmoid(g)) * u).astype(BF16))
    acc = jnp.zeros_like(x)
    for (lo, hi), a in zip(FF_CHUNKS, acts):
        acc = acc + _dot(a, wd_ref[lo:hi, :])
    return x + 0.5 * acc


def _const_spec(shape):
    nd = len(shape)
    return pl.BlockSpec(shape, lambda *_: (0,) * nd, pipeline_mode=pl.Buffered(1))


def _params(sem):
    return pltpu.CompilerParams(dimension_semantics=sem, vmem_limit_bytes=VMEM_LIMIT_BYTES)


ZQ_BLOCKS = 4


def _ffn_in_stream_kernel(x_ref, g1_ref, wg_ref, wu_ref, wd_ref, gm_ref, wq_ref, wgt_ref, sq1_ref, sq2_ref,
                          x1_ref, zq_ref, zg_ref, wg_out, wu_out, wd_out, sq1_out, sq2_out, h_s, acc_s, x_s, *, n_ff):
    j = pl.program_id(0)

    @pl.when(j == 0)
    def _norm():
        x = x_ref[...].reshape(x_s.shape)
        x_s[...] = x
        h_s[...] = _rms(x, g1_ref[...]).astype(BF16)
        acc_s[...] = jnp.zeros_like(acc_s)

    @pl.when(j < n_ff)
    def _ffn_chunk():
        wg, wu, wd = (ref[...].astype(BF16) for ref in (wg_ref, wu_ref, wd_ref))
        wg_out[...] = wg
        wu_out[...] = wu
        wd_out[...] = wd
        h = h_s[...]
        g = _dot(h, wg)
        u = _dot(h, wu)
        acc_s[...] += _dot(((g * jax.nn.sigmoid(g)) * u).astype(BF16), wd)

    @pl.when(j == n_ff)
    def _residual():
        x1 = x_s[...] + 0.5 * acc_s[...]
        x1_ref[...] = x1
        h_s[...] = _rms(x1, gm_ref[...]).astype(BF16)

    @pl.when((j >= n_ff) & (j < n_ff + ZQ_BLOCKS))
    def _zq_block():
        zq_ref[...] = _dot(h_s[...], wq_ref[...]).astype(zq_ref.dtype)
        sq1_out[...] = sq1_ref[...].astype(BF16)
        sq2_out[...] = sq2_ref[...].astype(BF16)

    @pl.when(j == n_ff + ZQ_BLOCKS)
    def _zg():
        zg_ref[...] = _dot(h_s[...], wgt_ref[...])


def _ffn_in_stream(x, w, zq_dtype):
    n = x.shape[0] * x.shape[1]
    n_ff = D_FF // MXU_WIDTH
    zq_blk = ZQ_W // ZQ_BLOCKS
    ff = lambda j: jnp.minimum(j, n_ff - 1)
    zqb = lambda j: jnp.clip(j - n_ff, 0, ZQ_BLOCKS - 1)
    ff_cols = pl.BlockSpec((D_MODEL, MXU_WIDTH), lambda j: (0, ff(j)))
    ff_rows = pl.BlockSpec((MXU_WIDTH, D_MODEL), lambda j: (ff(j), 0))
    resident = lambda width: pl.BlockSpec((n, width), lambda j: (0, 0))
    sq_rows = pl.BlockSpec((D_MODEL // ZQ_BLOCKS, D_MODEL), lambda j: (zqb(j), 0))
    return pl.pallas_call(
        functools.partial(_ffn_in_stream_kernel, n_ff=n_ff),
        grid=(n_ff + ZQ_BLOCKS + 1,),
        in_specs=[_const_spec(x.shape), _const_spec((1, D_MODEL)), ff_cols, ff_cols, ff_rows,
                  _const_spec((1, D_MODEL)), pl.BlockSpec((D_MODEL, zq_blk), lambda j: (0, zqb(j))),
                  _const_spec((D_MODEL, ZG_W)), sq_rows, sq_rows],
        out_specs=[resident(D_MODEL), pl.BlockSpec((n, zq_blk), lambda j: (0, zqb(j))), resident(ZG_W),
                   ff_cols, ff_cols, ff_rows, sq_rows, sq_rows],
        out_shape=[jax.ShapeDtypeStruct((n, D_MODEL), F32), jax.ShapeDtypeStruct((n, ZQ_W), zq_dtype),
                   jax.ShapeDtypeStruct((n, ZG_W), F32), jax.ShapeDtypeStruct((D_MODEL, D_FF), BF16),
                   jax.ShapeDtypeStruct((D_MODEL, D_FF), BF16), jax.ShapeDtypeStruct((D_FF, D_MODEL), BF16),
                   jax.ShapeDtypeStruct((D_MODEL, D_MODEL), BF16), jax.ShapeDtypeStruct((D_MODEL, D_MODEL), BF16)],
        scratch_shapes=[pltpu.VMEM((n, D_MODEL), BF16), pltpu.VMEM((n, D_MODEL), F32),
                        pltpu.VMEM((n, D_MODEL), F32)],
        compiler_params=_params(("arbitrary",)),
        name="ffn_in",
    )(x, w["ffn1_g"], w["ffn1_wg"], w["ffn1_wu"], w["ffn1_wd"], w["mix_g"], w["w_in_q"], w["w_in_g"],
      w["w_out"], w["xattn_wq"])


def _mixer_stages(zqs, zgs, states, consts, ee_ref, *, L, t_real, chain):
    bias, wa2, ba, gml, ggl = consts
    nchunk = len(zqs)
    groups = [(b, h) for b in range(nchunk) for h in range(ML_HEADS)]
    pairs = [(b, p) for b in range(nchunk) for p in range(GLA_PAIRS)]
    padded = t_real < L
    valid = lax.broadcasted_iota(jnp.int32, (L, 1), 0) < t_real
    rr = lax.broadcasted_iota(jnp.int32, (L, L), 0)
    cc = lax.broadcasted_iota(jnp.int32, (L, L), 1)
    tril = cc <= rr
    c = GLA_SUB
    nb = L // c
    lane = lax.broadcasted_iota(jnp.int32, (1, LANES), 1)
    lane_blk = (lane % GLA_DK) // c
    lane_head = lane // GLA_DK
    v_lane_head = lax.broadcasted_iota(jnp.int32, (1, 2 * GLA_DV), 1) // GLA_DV
    row_blk = (lax.broadcasted_iota(jnp.int32, (2 * L, 1), 0) % L) // c
    t_in = lax.broadcasted_iota(jnp.int32, (1, c, 1), 1)
    new_states = [dict(c=[None] * ML_HEADS, n=[None] * ML_HEADS, m=[None] * ML_HEADS, s=[None] * GLA_HEADS)
                  for _ in range(nchunk)]
    prev = lambda b: new_states[b - 1] if chain and b > 0 else states[0 if chain else b]
    known = lambda b: not chain or b == 0

    carried = {}

    def carried_mlstm(b, h):
        st = prev(b)
        d = ml[(b, h)]
        carried[(b, h)] = _dot(d["qb"], st["c"][h].astype(BF16))

    def carried_gla(b, p):
        s_prev = prev(b)["s"]
        zero_blk = jnp.zeros((GLA_DK, GLA_DV), BF16)
        s_bd = jnp.concatenate(
            [jnp.concatenate([s_prev[2 * p].astype(BF16), zero_blk], axis=1),
             jnp.concatenate([zero_blk, s_prev[2 * p + 1].astype(BF16)], axis=1)], axis=0)
        carried[(b, "gla", p)] = _dot(gl[(b, p)]["q_dec"], s_bd)

    smalls, sms, lfs, b_cols, b_rows, sm_ts, las, bcs = [], [], [], [], [], [], [], []
    for b in range(nchunk):
        small = zgs[b][:, ZG_SMALL:ZG_SMALL + SMALL_W]
        sm = small + bias
        lf = _log_sigmoid(sm) * LOG2_E
        sm = sm * LOG2_E
        if padded:
            sm = jnp.where(valid, sm, -jnp.inf)
            lf = jnp.where(valid, lf, 0.0)
        smalls.append(small)
        sms.append(sm)
        lfs.append(lf)

    long_chunk = L > SUBLANES
    col_w = 1 if long_chunk else LANES
    tril_f, triu_f = tril.astype(F32), (rr <= cc).astype(F32)

    def cumsum_rows(x):
        if long_chunk:
            return _dot_f32(tril_f, x)
        acc = x[0:1]
        rows = [acc]
        for r in range(1, L):
            acc = acc + x[r:r + 1]
            rows.append(acc)
        return jnp.concatenate(rows, axis=0)

    lf_ts = [lf.T[0:SUBLANES] for lf in lfs] if long_chunk else None
    yield
    for b in range(nchunk):
        b_cols.append(cumsum_rows(lfs[b]))
        b_rows.append(_dot_f32(lf_ts[b], triu_f) if long_chunk else b_cols[b].T[0:SUBLANES])
        sm_ts.append(sms[b].T)
        la = _log_sigmoid(_dot(smalls[b].astype(BF16), wa2) + ba) * (LOG2_E / GLA_TAU)
        las.append(jnp.where(valid, la, 0.0) if padded else la)

    ml = {}
    for g in groups:
        b, h = g
        zq = zqs[b]
        qf = zq[:, ZQ_MQ + h * ML_DK:ZQ_MQ + (h + 1) * ML_DK].astype(F32)
        kf = zq[:, ZQ_MK + h * ML_DK:ZQ_MK + (h + 1) * ML_DK].astype(F32) * (ML_DK ** -0.5)
        vf = zq[:, ZQ_MV + h * ML_DV:ZQ_MV + (h + 1) * ML_DV].astype(F32)
        if padded:
            kf = jnp.where(valid, kf, 0.0)
            vf = jnp.where(valid, vf, 0.0)
        ml[g] = dict(qf=qf, qb=qf.astype(BF16), kf=kf, kb=kf.astype(BF16), vb=vf.astype(BF16))
    yield
    for b in range(nchunk):
        bcs.append(cumsum_rows(las[b]))
    for g in groups:
        d = ml[g]
        d["qk"] = _dot_nt(d["qb"], d["kb"])
        if known(g[0]):
            carried_mlstm(*g)

    yield
    for g in groups:
        b, h = g
        d = ml[g]
        i_col = jnp.broadcast_to(sms[b][:, h:h + 1], (L, col_w))
        b_col = jnp.broadcast_to(b_cols[b][:, ML_HEADS + h:ML_HEADS + h + 1], (L, col_w))
        b_row = b_rows[b][ML_HEADS + h:ML_HEADS + h + 1, :]
        i_row = sm_ts[b][h:h + 1, :]
        a_col = b_col + prev(b)["m"][h] * LOG2_E
        dm = jnp.where(tril, b_col[:, :L] - (b_row - i_row), -jnp.inf)
        mt = jnp.maximum(a_col, jnp.max(dm, axis=1, keepdims=True))
        w_inter = jnp.exp2(a_col - mt)
        s = d["qk"] * jnp.exp2(dm - mt[:, :L])
        kw = d["kf"] * jnp.exp2((b_col[L - 1:L] - mt[L - 1:L]) - (b_col - i_col))
        d.update(mt=mt, w_inter=w_inter, s=s, kw=kw, kw_t=kw.T.astype(BF16))
        new_states[b]["m"][h] = mt[L - 1:L, 0:1] * LN_2

    gl = {}
    for b in range(nchunk):
        zq = zqs[b]
        gq = zq[:, ZQ_GQ:ZQ_GQ + GLA_QK_W].astype(F32) * (GLA_DK ** -0.5)
        gk = zq[:, ZQ_GK:ZQ_GK + GLA_QK_W].astype(F32)
        gv = zq[:, ZQ_GV:ZQ_GV + GLA_V_W].astype(F32)
        if padded:
            gk = jnp.where(valid, gk, 0.0)
            gv = jnp.where(valid, gv, 0.0)
        stack = lambda x: jnp.concatenate([x[:, :LANES], x[:, LANES:]], axis=0)
        q2, k2, b2 = stack(gq), stack(gk), stack(bcs[b])
        q3 = q2.reshape(2 * nb, c, LANES)
        k3 = k2.reshape(2 * nb, c, LANES)
        b3 = b2.reshape(2 * nb, c, LANES)
        pair_terms = []
        for j in range(min(c, t_real)):
            decay = jnp.exp2(jnp.where(t_in >= j, b3 - b3[:, j:j + 1, :], -jnp.inf))
            pair_terms.append((q3 * k3[:, j:j + 1, :] * decay).reshape(2 * L, LANES).astype(BF16))
        kt2 = (k3 * jnp.exp2(b3[:, c - 1:c, :] - b3)).reshape(2 * L, LANES) if nb > 1 else None
        gl[b] = dict(gv=gv, q2=q2, k2=k2, b2=b2, pair_terms=pair_terms, kt2=kt2)
    yield
    for b in range(nchunk):
        acc = jnp.zeros((2 * L, LANES), F32)
        for j, pair_j in enumerate(gl[b]["pair_terms"]):
            acc = acc + _dot(pair_j, ee_ref[j])
        gl[b]["a_diag"] = jnp.where(lane_blk == row_blk, acc, 0.0)

    for bp in pairs:
        b, p = bp
        d = gl[b]
        rows_p = slice(p * L, (p + 1) * L)
        q_p, k_p, b_p = d["q2"][rows_p], d["k2"][rows_p], d["b2"][rows_p]
        if long_chunk:
            decay_col = jnp.exp2(b_p[L - SUBLANES:L].T[:, SUBLANES - 1:SUBLANES])
        else:
            decay_col = jnp.exp2(jnp.broadcast_to(b_p[L - 1:L], (LANES, LANES)).T)
        e = dict(q_dec=(q_p * jnp.exp2(b_p)).astype(BF16),
                 kh_t=(k_p * jnp.exp2(b_p[L - 1:L] - b_p)).T.astype(BF16),
                 decay_col=decay_col)
        if nb > 1:
            kt_p = d["kt2"][rows_p].astype(BF16)
            k_bd = jnp.concatenate([jnp.where(lane_head == hh, kt_p, 0.0) for hh in range(2)], axis=0)
            slabs, offs = [], []
            off = 0
            for j in range(nb - 1):
                lo = (j + 1) * c
                slabs.append(q_p[lo:] * jnp.exp2(b_p[lo:] - b_p[lo - 1:lo]))
                offs.append(off)
                off += L - lo
            e.update(q_var=jnp.concatenate(slabs, axis=0).astype(BF16), k_bd=k_bd, offs=offs)
        gl[bp] = e
    yield
    for bp in pairs:
        e = gl[bp]
        if nb > 1:
            e["r"] = _dot_nt(e["q_var"], e["k_bd"])
        if known(bp[0]):
            carried_gla(*bp)

    for g in groups:
        d = ml[g]
        d["sv"] = _dot(d["s"].astype(BF16), d["vb"])
        d["c_upd"] = _dot(d["kw_t"], d["vb"])
        d["den"] = jnp.sum(d["s"], axis=1, keepdims=True)
        if known(g[0]):
            d["qn"] = jnp.sum(d["qf"] * prev(g[0])["n"][g[1]], axis=1, keepdims=True)
    yield
    for bp in pairs:
        b, p = bp
        d, e = gl[b], gl[bp]
        a_p = d["a_diag"][p * L:(p + 1) * L]
        if nb > 1:
            blocks = []
            for i in range(nb):
                blk = a_p[i * c:(i + 1) * c]
                for j in range(i):
                    lo_r = e["offs"][j] + (i - j - 1) * c
                    blk = jnp.where(lane_blk == j, e["r"][lo_r:lo_r + c], blk)
                blocks.append(blk)
            a_p = jnp.concatenate(blocks, axis=0)
        v_f = d["gv"][:, p * 2 * GLA_DV:(p + 1) * 2 * GLA_DV]
        v_p = v_f.astype(BF16)
        if L < GLA_DK:
            v_rows = []
            for hh in range(2):
                v_rows += [jnp.where(v_lane_head == hh, v_f, 0.0), jnp.zeros((GLA_DK - L, 2 * GLA_DV), F32)]
            v_bd = jnp.concatenate(v_rows, axis=0).astype(BF16)
        else:
            v_bd = jnp.concatenate([jnp.where(v_lane_head == hh, v_p, 0.0) for hh in range(2)], axis=0)
        e["s_upd"] = [_dot(e["kh_t"][hh * GLA_DK:(hh + 1) * GLA_DK], v_p[:, hh * GLA_DV:(hh + 1) * GLA_DV])
                      for hh in range(2)]
        e.update(scores=a_p.astype(BF16), v_bd=v_bd)
    yield
    for bp in pairs:
        e = gl[bp]
        e["o_intra"] = _dot(e["scores"], e["v_bd"])

    yield
    ys = [[None] * (ML_HEADS + GLA_HEADS) for _ in range(nchunk)]

    def finish_mlstm(b):
        st = prev(b)
        for h in range(ML_HEADS):
            d = ml[(b, h)]
            mt, w_inter = d["mt"], d["w_inter"]
            qn = d["qn"] if known(b) else jnp.sum(d["qf"] * st["n"][h], axis=1, keepdims=True)
            den = d["den"] + w_inter * qn
            hh = (d["sv"] + w_inter * carried[(b, h)]) / jnp.maximum(jnp.abs(den), jnp.exp2(-mt))
            i_last = w_inter[L - 1:L]
            new_states[b]["c"][h] = i_last * st["c"][h] + d["c_upd"]
            new_states[b]["n"][h] = i_last * st["n"][h] + jnp.sum(d["kw"], axis=0, keepdims=True)
            hs = slice(h * ML_DV, (h + 1) * ML_DV)
            yn = hh * lax.rsqrt(jnp.mean(hh * hh, axis=-1, keepdims=True) + EPS) * gml[:, hs]
            ys[b][h] = jax.nn.sigmoid(zgs[b][:, ZG_MO + h * ML_DV:ZG_MO + (h + 1) * ML_DV]) * yn

    def finish_gla(b):
        st = prev(b)
        for p in range(GLA_PAIRS):
            e = gl[(b, p)]
            o = carried[(b, "gla", p)] + e["o_intra"]
            for hh in range(2):
                h = 2 * p + hh
                ds = slice(hh * GLA_DK, (hh + 1) * GLA_DK)
                vs = slice(hh * GLA_DV, (hh + 1) * GLA_DV)
                new_states[b]["s"][h] = e["decay_col"][ds] * st["s"][h] + e["s_upd"][hh]
                oh = o[:, vs]
                yn = oh * lax.rsqrt(jnp.mean(oh * oh, axis=-1, keepdims=True) + EPS) * ggl[:, h * GLA_DV:(h + 1) * GLA_DV]
                gg = zgs[b][:, ZG_GG + h * GLA_DV:ZG_GG + (h + 1) * GLA_DV]
                ys[b][ML_HEADS + h] = (gg * jax.nn.sigmoid(gg)) * yn

    if chain:
        for b in range(nchunk):
            if not known(b):
                for h in range(ML_HEADS):
                    carried_mlstm(b, h)
                for p in range(GLA_PAIRS):
                    carried_gla(b, p)
            finish_mlstm(b)
            finish_gla(b)
    else:
        for b in range(nchunk):
            finish_mlstm(b)
        for b in range(nchunk):
            finish_gla(b)
    return [jnp.concatenate(y, axis=1) for y in ys], new_states


def _drain(gen):
    try:
        while True:
            next(gen)
    except StopIteration as done:
        return done.value


def _mixer_compute(*args, **kwargs):
    return _drain(_mixer_stages(*args, **kwargs))


def _mixer_state_kernel(zq_ref, zg_ref, bias_ref, wa2_ref, ba_ref, gml_ref, ggl_ref, ee_ref, c0_ref, n0_ref, m0_ref,
                        s0_ref, y_ref, co_ref, no_ref, mo_ref, so_ref, *, L, t_real, nblk):
    consts = (bias_ref[...], wa2_ref[...], ba_ref[...], gml_ref[...], ggl_ref[...])
    lane = lax.broadcasted_iota(jnp.int32, (1, LANES), 1)

    def chunk_rows(ref, b):
        rows = ref[b * t_real:(b + 1) * t_real, :]
        return jnp.concatenate([rows, jnp.zeros((L - t_real, rows.shape[1]), rows.dtype)], axis=0)

    states = []
    for b in range(nblk):
        n_all = n0_ref[b]
        m_all = m0_ref[b]
        states.append(dict(c=[c0_ref[b, h] for h in range(ML_HEADS)],
                           n=[n_all[h:h + 1, :] for h in range(ML_HEADS)],
                           m=[m_all[:, h:h + 1] for h in range(ML_HEADS)],
                           s=[s0_ref[b, h] for h in range(GLA_HEADS)]))
    ys, new_states = _mixer_compute([chunk_rows(zq_ref, b) for b in range(nblk)],
                                    [chunk_rows(zg_ref, b) for b in range(nblk)], states, consts, ee_ref, L=L,
                                    t_real=t_real, chain=False)
    for b in range(nblk):
        st = new_states[b]
        y_ref[b * t_real:(b + 1) * t_real, :] = ys[b][0:t_real].astype(y_ref.dtype)
        for h in range(ML_HEADS):
            co_ref[b, h] = st["c"][h]
            so_ref[b, h] = st["s"][h]
        no_ref[b] = jnp.concatenate(st["n"], axis=0)
        m_row = jnp.zeros((1, LANES), F32)
        for h in range(ML_HEADS):
            m_row = jnp.where(lane == h, st["m"][h], m_row)
        mo_ref[b] = m_row[:, 0:ML_HEADS]


def _mixer_state(zq, zg, w, state, t_real, nblk):
    bsz = zq.shape[0] // t_real
    L = -(-t_real // SUBLANES) * SUBLANES
    rows = lambda width: pl.BlockSpec((nblk * t_real, width), lambda b: (b, 0))
    per_b = lambda *tail: pl.BlockSpec((nblk,) + tail, lambda b: (b,) + (0,) * len(tail))
    state_specs = [per_b(ML_HEADS, ML_DK, ML_DV), per_b(ML_HEADS, ML_DK), per_b(1, ML_HEADS),
                   per_b(GLA_HEADS, GLA_DK, GLA_DV)]
    consts = [_const_spec((1, SMALL_W)), _const_spec((SMALL_W, GLA_QK_W)), _const_spec((1, GLA_QK_W)),
              _const_spec((1, ML_HEADS * ML_DV)), _const_spec((1, GLA_V_W)), _const_spec((GLA_SUB, LANES, LANES))]
    return pl.pallas_call(
        functools.partial(_mixer_state_kernel, L=L, t_real=t_real, nblk=nblk),
        grid=(bsz // nblk,),
        in_specs=[rows(ZQ_W), rows(ZG_W)] + consts + state_specs,
        out_specs=[rows(D_MODEL)] + state_specs,
        out_shape=[jax.ShapeDtypeStruct((bsz * t_real, D_MODEL), F32),
                   jax.ShapeDtypeStruct((bsz, ML_HEADS, ML_DK, ML_DV), F32),
                   jax.ShapeDtypeStruct((bsz, ML_HEADS, ML_DK), F32),
                   jax.ShapeDtypeStruct((bsz, 1, ML_HEADS), F32),
                   jax.ShapeDtypeStruct((bsz, GLA_HEADS, GLA_DK, GLA_DV), F32)],
        compiler_params=_params(("arbitrary",)),
        name="mixer_state",
    )(zq, zg, w["gate_bias"], w["w_a2"], w["b_a"], w["mlstm_out_g"], w["gla_out_g"], w["diag_sum"], *state)


def _ffn_in_pieces(x, g1_ref, wg_ref, wu_ref, wd_ref, gm_ref, wq_ref, wgt_ref, x1_ref, zq_out, zg_out):
    h = _rms(x, g1_ref[...]).astype(BF16)
    acts = []
    for lo in range(0, D_FF, MXU_WIDTH):
        g = _dot(h, wg_ref[:, lo:lo + MXU_WIDTH])
        yield
        u = _dot(h, wu_ref[:, lo:lo + MXU_WIDTH])
        yield
        acts.append(((g * jax.nn.sigmoid(g)) * u).astype(BF16))
    a = jnp.concatenate(acts, axis=1)
    down = []
    for lo in range(0, D_MODEL, MXU_WIDTH):
        down.append(_dot(a, wd_ref[:, lo:lo + MXU_WIDTH]))
        yield
    x1 = x + 0.5 * jnp.concatenate(down, axis=1)
    x1_ref[...] = x1
    hm = _rms(x1, gm_ref[...]).astype(BF16)
    for lo in range(0, ZQ_W, MXU_WIDTH):
        zq_out[:, lo:lo + MXU_WIDTH] = _dot(hm, wq_ref[:, lo:lo + MXU_WIDTH]).astype(zq_out.dtype)
        yield
    for lo in range(0, ZG_W, MXU_WIDTH):
        hi = min(lo + MXU_WIDTH, ZG_W)
        zg_out[:, lo:hi] = _dot(hm, wgt_ref[:, lo:hi])
        yield


ATTN_STAGE_EVERY = 7
FFN_PIECES_PER_MIXER_STAGE = 1


def _ffn_mix_kernel(*refs, tiles_per_batch, n_cast):
    it = iter(refs)
    (x_ref, g1_ref, wg_ref, wu_ref, wd_ref, gm_ref, wq_ref, wgt_ref, bias_ref, wa2_ref, ba_ref, gml_ref, ggl_ref,
     ee_ref, qs_ref, ks_ref, vs_ref) = (next(it) for _ in range(17))
    cast_src = [next(it) for _ in range(n_cast)]
    x1_ref, ym_ref, co_ref, no_ref, mo_ref, so_ref, os_ref = (next(it) for _ in range(7))
    cast_dst = [next(it) for _ in range(n_cast)]
    zq_s, zg_s, c_s, n_s, m_s, s_s = (next(it) for _ in range(6))
    i = pl.program_id(0)
    for src, dst in zip(cast_src, cast_dst):
        dst[...] = src[...].astype(dst.dtype)

    @pl.when(i == 0)
    def _init():
        for ref in (zq_s, zg_s, c_s, n_s, m_s, s_s):
            ref[...] = jnp.zeros_like(ref)

    consts = (bias_ref[...], wa2_ref[...], ba_ref[...], gml_ref[...], ggl_ref[...])
    n_chunks = zq_s.shape[0] // CHUNK
    rows = [slice(k * CHUNK, (k + 1) * CHUNK) for k in range(n_chunks)]
    zqs = [zq_s[r, :] for r in rows]
    zgs = [zg_s[r, :] for r in rows]
    starts_batch = (i - 1) % tiles_per_batch == 0
    carry = lambda v: jnp.where(starts_batch, 0.0, v)
    state = dict(c=[carry(c_s[h]) for h in range(ML_HEADS)], n=[carry(n_s[h:h + 1, :]) for h in range(ML_HEADS)],
                 m=[carry(m_s[h:h + 1, 0:1]) for h in range(ML_HEADS)], s=[carry(s_s[h]) for h in range(GLA_HEADS)])

    def mixers():
        st = state
        for k in range(n_chunks):
            ys, new = yield from _mixer_stages([zqs[k]], [zgs[k]], [st], consts, ee_ref, L=CHUNK, t_real=CHUNK,
                                               chain=True)
            ym_ref[rows[k], :] = ys[0].astype(ym_ref.dtype)
            st = new[0]
            yield
        return st

    ffn = _ffn_in_pieces(x_ref[...], g1_ref, wg_ref, wu_ref, wd_ref, gm_ref, wq_ref, wgt_ref, x1_ref, zq_s, zg_s)
    mix = mixers()
    attn = _xattn_cache_stages(qs_ref, ks_ref, vs_ref, os_ref)
    live = {"ffn": True, "attn": True}

    def advance(name, gen):
        if live[name]:
            try:
                next(gen)
            except StopIteration:
                live[name] = False

    slot = 0
    while True:
        try:
            next(mix)
        except StopIteration as done:
            last = done.value
            break
        for _ in range(FFN_PIECES_PER_MIXER_STAGE):
            advance("ffn", ffn)
        if slot % ATTN_STAGE_EVERY == ATTN_STAGE_EVERY // 2:
            advance("attn", attn)
        slot += 1
    for name, gen in (("ffn", ffn), ("attn", attn)):
        if live[name]:
            _drain(gen)

    for h in range(ML_HEADS):
        c_s[h] = last["c"][h]
        n_s[h:h + 1, :] = last["n"][h]
        m_s[h:h + 1, :] = jnp.broadcast_to(last["m"][h], (1, LANES))
        s_s[h] = last["s"][h]

    @pl.when(jnp.logical_and(i >= 1, (i - 1) % tiles_per_batch == tiles_per_batch - 1))
    def _emit_state():
        lane = lax.broadcasted_iota(jnp.int32, (1, LANES), 1)
        for h in range(ML_HEADS):
            co_ref[0, h] = last["c"][h]
            so_ref[0, h] = last["s"][h]
        no_ref[0] = jnp.concatenate(last["n"], axis=0)
        m_row = jnp.zeros((1, LANES), F32)
        for h in range(ML_HEADS):
            m_row = jnp.where(lane == h, last["m"][h], m_row)
        mo_ref[0] = m_row[:, 0:ML_HEADS]


def _ffn_mix(x, w, tm, q_s, k_cache, v_cache, to_cast):
    bsz, t, _ = x.shape
    tiles_per_batch = t // tm
    n_tiles = bsz * tiles_per_batch
    assert t % tm == 0 and tm % CHUNK == 0
    bs = k_cache.shape[0]
    ts = q_s.shape[0] // bs
    assert bs % n_tiles == 0
    sb = bs // n_tiles
    cur_tile = lambda i: jnp.minimum(i, n_tiles - 1)
    prev_tile = lambda i: jnp.maximum(i - 1, 0)
    cur = lambda width: pl.BlockSpec((tm, width), lambda i: (cur_tile(i), 0))
    per_b = lambda *tail: pl.BlockSpec((1,) + tail, lambda i: (prev_tile(i) // tiles_per_batch,) + (0,) * len(tail))
    state_specs = [per_b(ML_HEADS, ML_DK, ML_DV), per_b(ML_HEADS, ML_DK), per_b(1, ML_HEADS),
                   per_b(GLA_HEADS, GLA_DK, GLA_DV)]
    qo_s = pl.BlockSpec((sb * ts, D_MODEL), lambda i: (cur_tile(i), 0))
    kv_s = pl.BlockSpec((sb, N_MEM * CACHE_ROW_GROUP, LANES), lambda i: (cur_tile(i), 0, 0))
    cast_specs = []
    for a in to_cast:
        rb = next(r for r in range(BF16_SUBLANES, a.shape[0] + 1, BF16_SUBLANES)
                  if a.shape[0] % r == 0 and a.shape[0] // r <= n_tiles)
        cast_specs.append(pl.BlockSpec((rb, a.shape[1]), lambda i, last=a.shape[0] // rb - 1: (jnp.minimum(i, last), 0)))
    x1, ym, c_new, n_new, m_new, s_new, o_s, *cast = pl.pallas_call(
        functools.partial(_ffn_mix_kernel, tiles_per_batch=tiles_per_batch, n_cast=len(to_cast)),
        grid=(n_tiles + 1,),
        in_specs=[cur(D_MODEL), _const_spec((1, D_MODEL)), _const_spec((D_MODEL, D_FF)),
                  _const_spec((D_MODEL, D_FF)), _const_spec((D_FF, D_MODEL)), _const_spec((1, D_MODEL)),
                  _const_spec((D_MODEL, ZQ_W)), _const_spec((D_MODEL, ZG_W)), _const_spec((1, SMALL_W)),
                  _const_spec((SMALL_W, GLA_QK_W)), _const_spec((1, GLA_QK_W)), _const_spec((1, ML_HEADS * ML_DV)),
                  _const_spec((1, GLA_V_W)), _const_spec((GLA_SUB, LANES, LANES)), qo_s, kv_s, kv_s] + cast_specs,
        out_specs=([cur(D_MODEL), pl.BlockSpec((tm, D_MODEL), lambda i: (prev_tile(i), 0))] + state_specs + [qo_s]
                   + cast_specs),
        out_shape=[jax.ShapeDtypeStruct((n_tiles * tm, D_MODEL), F32),
                   jax.ShapeDtypeStruct((n_tiles * tm, D_MODEL), BF16),
                   jax.ShapeDtypeStruct((bsz, ML_HEADS, ML_DK, ML_DV), F32),
                   jax.ShapeDtypeStruct((bsz, ML_HEADS, ML_DK), F32),
                   jax.ShapeDtypeStruct((bsz, 1, ML_HEADS), F32),
                   jax.ShapeDtypeStruct((bsz, GLA_HEADS, GLA_DK, GLA_DV), F32),
                   jax.ShapeDtypeStruct((bs * ts, D_MODEL), F32)] + [jax.ShapeDtypeStruct(a.shape, BF16) for a in to_cast],
        scratch_shapes=[pltpu.VMEM((tm, ZQ_W), BF16), pltpu.VMEM((tm, ZG_W), F32),
                        pltpu.VMEM((ML_HEADS, ML_DK, ML_DV), F32), pltpu.VMEM((SUBLANES, LANES), F32),
                        pltpu.VMEM((SUBLANES, LANES), F32), pltpu.VMEM((GLA_HEADS, GLA_DK, GLA_DV), F32)],
        compiler_params=_params(("arbitrary",)),
        name="ffn_mix",
    )(x.reshape(bsz * t, D_MODEL), w["ffn1_g"], w["ffn1_wg"], w["ffn1_wu"], w["ffn1_wd"], w["mix_g"], w["w_in_q"],
      w["w_in_g"], w["gate_bias"], w["w_a2"], w["b_a"], w["mlstm_out_g"], w["gla_out_g"], w["diag_sum"],
      q_s, _cache_rows_view(k_cache), _cache_rows_view(v_cache), *to_cast)
    return x1, ym, (c_new[None], n_new[None], m_new.reshape(1, bsz, ML_HEADS), s_new[None]), o_s, cast


def _post_mix_kernel(x1_ref, ym_ref, wout_ref, gx_ref, wq_ref, x2_ref, q_ref):
    x2 = x1_ref[...] + _dot(ym_ref[...].astype(BF16), wout_ref[...])
    x2_ref[...] = x2
    hq = _rms(x2, gx_ref[...]).astype(BF16)
    q_ref[...] = _dot(hq, wq_ref[...]).astype(q_ref.dtype)


def _post_mix(x1, ym, w, tm, q_dtype):
    n = x1.shape[0]
    row = lambda: pl.BlockSpec((tm, D_MODEL), lambda i: (i, 0))
    return pl.pallas_call(
        _post_mix_kernel,
        grid=(n // tm,),
        in_specs=[row(), row(), _const_spec((D_MODEL, D_MODEL)), _const_spec((1, D_MODEL)),
                  _const_spec((D_MODEL, D_MODEL))],
        out_specs=[row(), row()],
        out_shape=[jax.ShapeDtypeStruct((n, D_MODEL), F32), jax.ShapeDtypeStruct((n, D_MODEL), q_dtype)],
        compiler_params=_params(("arbitrary",)),
        name="post_mix",
    )(x1, ym, w["w_out"], w["xattn_g"], w["xattn_wq"])


def _softmax(s):
    e = jnp.exp(s - jnp.max(s, axis=-1, keepdims=True))
    return e / jnp.sum(e, axis=-1, keepdims=True)


def _post_fused_kernel(x1_ref, ym_ref, mem_ref, gm_ref, wk_ref, wv_ref, wout_ref, gx_ref, wq_ref, wo_ref, g2_ref,
                       wg_ref, wu_ref, wd_ref, gf_ref, y_ref, ko_ref, vo_ref, k_s, v_s):
    @pl.when(pl.program_id(1) == 0)
    def _memkv():
        hn = _rms(mem_ref[0], gm_ref[...]).astype(BF16)
        for w_ref, out_ref, s_ref in ((wk_ref, ko_ref, k_s), (wv_ref, vo_ref, v_s)):
            kv = _dot(hn, w_ref[...])
            _cache_rows_store(out_ref, 0, kv)
            s_ref[...] = kv.astype(BF16)

    x2 = x1_ref[0] + _dot(ym_ref[0].astype(BF16), wout_ref[...])
    q = _dot(_rms(x2, gx_ref[...]).astype(BF16), wq_ref[...]).astype(BF16)
    heads = [slice(h * XA_DH, (h + 1) * XA_DH) for h in range(XA_HEADS)]
    k_full, v_full = k_s[...], v_s[...]
    scores = [_dot_nt(q[:, hs], k_full[:, hs]) * (XA_DH ** -0.5) for hs in heads]
    probs = [_softmax(s).astype(BF16) for s in scores]
    o = jnp.concatenate([_dot(p, v_full[:, hs]).astype(BF16) for hs, p in zip(heads, probs)], axis=1)
    x3 = x2 + _dot(o, wo_ref[...])
    x4 = _swiglu_residual(x3, g2_ref, wg_ref, wu_ref, wd_ref)
    y_ref[0] = _rms(x4, gf_ref[...])


def _post_fused(x1, ym, mem, w, tm):
    bsz, t = x1.shape[0], x1.shape[1]
    row = lambda: pl.BlockSpec((1, tm, D_MODEL), lambda b, j: (b, j, 0))
    kv = pl.BlockSpec((1, N_MEM * CACHE_ROW_GROUP, LANES), lambda b, j: (b, 0, 0))
    sq = _const_spec((D_MODEL, D_MODEL))
    vec = _const_spec((1, D_MODEL))
    kv_shape = jax.ShapeDtypeStruct((bsz, N_MEM * CACHE_ROW_GROUP, LANES), F32)
    return pl.pallas_call(
        _post_fused_kernel,
        grid=(bsz, t // tm),
        in_specs=[row(), row(), pl.BlockSpec((1, N_MEM, D_MODEL), lambda b, j: (b, 0, 0)), vec, sq, sq,
                  sq, vec, sq, sq, vec, _const_spec((D_MODEL, D_FF)),
                  _const_spec((D_MODEL, D_FF)), _const_spec((D_FF, D_MODEL)), vec],
        out_specs=[row(), kv, kv],
        out_shape=[jax.ShapeDtypeStruct((bsz, t, D_MODEL), F32), kv_shape, kv_shape],
        scratch_shapes=[pltpu.VMEM((N_MEM, D_MODEL), BF16)] * 2,
        compiler_params=_params(("arbitrary", "arbitrary")),
        name="post_fused",
    )(x1, ym, mem, w["mem_g"], w["xattn_wk"], w["xattn_wv"], w["w_out"], w["xattn_g"], w["xattn_wq"],
      w["xattn_wo"], w["ffn2_g"], w["ffn2_wg"], w["ffn2_wu"], w["ffn2_wd"], w["final_g"])


XA_LANE_TILES = XA_DH // LANES
CACHE_ROW_GROUP = XA_LANE_TILES * XA_HEADS


def _cache_rows_view(x):
    bsz = x.shape[0]
    x = x.reshape(bsz, N_MEM, XA_HEADS, XA_LANE_TILES, LANES)
    return x.transpose(0, 1, 3, 2, 4).reshape(bsz, N_MEM * CACHE_ROW_GROUP, LANES)


def _cache_rows_unview(x):
    bsz = x.shape[0]
    x = x.reshape(bsz, N_MEM, XA_LANE_TILES, XA_HEADS, LANES)
    return x.transpose(0, 1, 3, 2, 4).reshape(bsz, N_MEM, XA_HEADS, XA_DH)


def _cache_rows_store(ref, b, x):
    for h in range(XA_HEADS):
        for lt in range(XA_LANE_TILES):
            lo = h * XA_DH + lt * LANES
            ref[b, pl.ds(lt * XA_HEADS + h, N_MEM, stride=CACHE_ROW_GROUP), :] = x[:, lo:lo + LANES]


def _cache_rows_load(ref, b):
    cols = [ref[b, pl.ds(lt * XA_HEADS + h, N_MEM, stride=CACHE_ROW_GROUP), :]
            for h in range(XA_HEADS) for lt in range(XA_LANE_TILES)]
    return jnp.concatenate(cols, axis=1).astype(BF16)


def _xattn_cache_stages(q_ref, k_ref, v_ref, o_ref):
    bb = k_ref.shape[0]
    tq = q_ref.shape[0] // bb
    lane_head = lax.broadcasted_iota(jnp.int32, (1, D_MODEL), 1) // XA_DH
    qs = [q_ref[b * tq:(b + 1) * tq, :] for b in range(bb)]
    q_bds = [jnp.concatenate([jnp.where(lane_head == h, q, 0.0) for h in range(XA_HEADS)], axis=0).astype(BF16)
             for q in qs]
    k_fulls = [_cache_rows_load(k_ref, b) for b in range(bb)]
    yield
    scores = [_dot_nt(q_bds[b], k_fulls[b]) * (XA_DH ** -0.5) for b in range(bb)]
    yield
    p_all = _softmax(jnp.concatenate(scores, axis=0)).astype(BF16)
    v_fulls = [_cache_rows_load(v_ref, b) for b in range(bb)]
    yield
    rows = XA_HEADS * tq
    o_fulls = [_dot(p_all[b * rows:(b + 1) * rows], v_fulls[b]) for b in range(bb)]
    yield
    for b in range(bb):
        o = jnp.zeros((tq, D_MODEL), F32)
        for h in range(XA_HEADS):
            o = jnp.where(lane_head == h, o_fulls[b][h * tq:(h + 1) * tq], o)
        o_ref[b * tq:(b + 1) * tq, :] = o.astype(o_ref.dtype)


def _ffn_out_stream_kernel(x2_ref, o_ref, wo_ref, g2_ref, wg_ref, wu_ref, wd_ref, gf_ref, y_ref, x3_s, h_s, acc_s):
    j = pl.program_id(0)

    @pl.when(j == 0)
    def _attn_out():
        x3 = x2_ref[...] + _dot(o_ref[...].astype(BF16), wo_ref[...])
        x3_s[...] = x3
        h_s[...] = _rms(x3, g2_ref[...]).astype(BF16)
        acc_s[...] = jnp.zeros_like(acc_s)

    h = h_s[...]
    g = _dot(h, wg_ref[...])
    u = _dot(h, wu_ref[...])
    acc_s[...] += _dot(((g * jax.nn.sigmoid(g)) * u).astype(BF16), wd_ref[...])

    @pl.when(j == pl.num_programs(0) - 1)
    def _final_norm():
        y_ref[...] = _rms(x3_s[...] + 0.5 * acc_s[...], gf_ref[...]).reshape(y_ref.shape)


def _ffn_out_stream(x2, o, w, bsz):
    n = x2.shape[0]
    y_shape = (bsz, n // bsz, D_MODEL)
    ff_cols = pl.BlockSpec((D_MODEL, MXU_WIDTH), lambda j: (0, j))
    ff_rows = pl.BlockSpec((MXU_WIDTH, D_MODEL), lambda j: (j, 0))
    return pl.pallas_call(
        _ffn_out_stream_kernel,
        grid=(D_FF // MXU_WIDTH,),
        in_specs=[_const_spec((n, D_MODEL)), _const_spec((n, D_MODEL)), _const_spec((D_MODEL, D_MODEL)),
                  _const_spec((1, D_MODEL)), ff_cols, ff_cols, ff_rows, _const_spec((1, D_MODEL))],
        out_specs=pl.BlockSpec(y_shape, lambda j: (0, 0, 0)),
        out_shape=jax.ShapeDtypeStruct(y_shape, F32),
        scratch_shapes=[pltpu.VMEM((n, D_MODEL), F32), pltpu.VMEM((n, D_MODEL), BF16),
                        pltpu.VMEM((n, D_MODEL), F32)],
        compiler_params=_params(("arbitrary",)),
        name="ffn_out",
    )(x2, o, w["xattn_wo"], w["ffn2_g"], w["ffn2_wg"], w["ffn2_wu"], w["ffn2_wd"], w["final_g"])


IN_SIZES = (("mq", ML_HEADS * ML_DK), ("mk", ML_HEADS * ML_DK), ("mv", ML_HEADS * ML_DV), ("mi", ML_HEADS),
            ("mf", ML_HEADS), ("mo", ML_HEADS * ML_DV), ("gq", GLA_QK_W), ("gk", GLA_QK_W), ("gv", GLA_V_W),
            ("ga", GLA_RANK), ("gg", GLA_V_W))
IN_OFFSET = {name: sum(width for _, width in IN_SIZES[:i]) for i, (name, _) in enumerate(IN_SIZES)}
D_IN = sum(width for _, width in IN_SIZES)
IN_MOVES = ((0, ZQ_MQ, IN_OFFSET["mq"], ZQ_GQ - ZQ_MQ), (0, ZQ_GQ, IN_OFFSET["gq"], ZQ_W - ZQ_GQ),
            (1, ZG_MO, IN_OFFSET["mo"], ZG_GG - ZG_MO), (1, ZG_GG, IN_OFFSET["gg"], ZG_SMALL - ZG_GG))
TM_REGROUP = MXU_WIDTH
assert IN_OFFSET["mf"] == IN_OFFSET["mi"] + ML_HEADS
assert all(IN_OFFSET[name] % SUBLANES == 0 for name in ("mq", "mi", "mo", "gq", "ga", "gg"))


def _w_in_regroup_kernel(wt_ref, wa2_ref, bi_ref, bf_ref, q_ref, g_ref, wa2_out, bias_out):
    narrow = 2 * ML_HEADS + GLA_RANK
    wa2_out[...] = jnp.concatenate([jnp.zeros((2 * ML_HEADS, GLA_QK_W), F32), wa2_ref[...],
                                    jnp.zeros((SMALL_W - narrow, GLA_QK_W), F32)], axis=0).astype(BF16)
    lane = lax.broadcasted_iota(jnp.int32, (1, SMALL_W), 1)
    b_i, b_f = bi_ref[...], bf_ref[...]
    bias = jnp.zeros((1, SMALL_W), F32)
    for h in range(ML_HEADS):
        bias = jnp.where(lane == h, b_i[:, h:h + 1], bias)
        bias = jnp.where(lane == ML_HEADS + h, b_f[:, h:h + 1], bias)
    bias_out[...] = bias

    def put(out_ref, dst, rows_t):
        out_ref[:, dst:dst + LANES] = rows_t.T.astype(BF16)

    for slab, dst, src, width in IN_MOVES:
        for off in range(0, width, LANES):
            put((q_ref, g_ref)[slab], dst + off, wt_ref[src + off:src + off + LANES, :])
    put(g_ref, ZG_SMALL, jnp.concatenate(
        [wt_ref[IN_OFFSET["mi"]:IN_OFFSET["mi"] + 2 * ML_HEADS, :],
         wt_ref[IN_OFFSET["ga"]:IN_OFFSET["ga"] + GLA_RANK, :],
         jnp.zeros((SMALL_W - narrow, wt_ref.shape[1]), F32)], axis=0))


def _w_in_regroup(w_in, w_a2, b_i, b_f):
    assert w_in.shape == (D_MODEL, D_IN)
    rows = lambda width: pl.BlockSpec((TM_REGROUP, width), lambda i: (i, 0))
    return pl.pallas_call(
        _w_in_regroup_kernel,
        grid=(D_MODEL // TM_REGROUP,),
        in_specs=[pl.BlockSpec((D_IN, TM_REGROUP), lambda i: (0, i)), _const_spec((GLA_RANK, GLA_QK_W)),
                  _const_spec((1, ML_HEADS)), _const_spec((1, ML_HEADS))],
        out_specs=[rows(ZQ_W), rows(ZG_W), pl.BlockSpec((SMALL_W, GLA_QK_W), lambda i: (0, 0)),
                   pl.BlockSpec((1, SMALL_W), lambda i: (0, 0))],
        out_shape=[jax.ShapeDtypeStruct((D_MODEL, ZQ_W), BF16), jax.ShapeDtypeStruct((D_MODEL, ZG_W), BF16),
                   jax.ShapeDtypeStruct((SMALL_W, GLA_QK_W), BF16), jax.ShapeDtypeStruct((1, SMALL_W), F32)],
        compiler_params=_params(("arbitrary",)),
        name="w_in_regroup",
    )(w_in.T, w_a2, b_i.reshape(1, ML_HEADS), b_f.reshape(1, ML_HEADS))


def _prep_weights(p):
    row = lambda a: a.reshape(1, -1).astype(F32)
    w_in_q, w_in_g, w_a2, gate_bias = _w_in_regroup(p["w_in"], p["gla_w_a2"], p["mlstm_b_i"], p["mlstm_b_f"])
    lane = np.arange(LANES)
    same_head = (lane[:, None] // GLA_DK) == (lane[None, :] // GLA_DK)
    diag_sum = jnp.asarray(np.stack([same_head & ((lane[None, :] % GLA_SUB) == j) for j in range(GLA_SUB)]), BF16)
    return dict(
        ffn1_wg=p["ffn1_w_gate"], ffn1_wu=p["ffn1_w_up"], ffn1_wd=p["ffn1_w_down"],
        ffn1_g=row(p["ffn1_norm_g"]), mix_g=row(p["mix_norm_g"]), w_in_q=w_in_q, w_in_g=w_in_g,
        gate_bias=gate_bias, w_a2=w_a2, b_a=row(p["gla_b_a"]), mlstm_out_g=row(p["mlstm_out_g"]),
        gla_out_g=row(p["gla_out_g"]), diag_sum=diag_sum,
        w_out=p["w_out"], xattn_g=row(p["xattn_norm_g"]), xattn_wq=p["xattn_w_q"],
        mem_g=row(p["mem_norm_g"]), ffn2_g=row(p["ffn2_norm_g"]), final_g=row(p["final_g"]),
        xattn_wo=p["xattn_w_o"], xattn_wk=p["xattn_w_k"], xattn_wv=p["xattn_w_v"], ffn2_wg=p["ffn2_w_gate"],
        ffn2_wu=p["ffn2_w_up"], ffn2_wd=p["ffn2_w_down"])


EARLY_WEIGHTS = ("ffn1_wg", "ffn1_wu", "ffn1_wd", "w_out", "xattn_wq")
LATE_WEIGHTS = ("xattn_wo", "xattn_wk", "xattn_wv", "ffn2_wg", "ffn2_wu", "ffn2_wd")


def kernel(x_prompt, x_sample, mem_prompt, cache_mem_k, cache_mem_v, state_mlstm_c, state_mlstm_n, state_mlstm_m, state_gla_s, ffn1_norm_g, ffn1_w_gate, ffn1_w_up, ffn1_w_down, mix_norm_g, w_in, mlstm_b_i, mlstm_b_f, mlstm_out_g, gla_w_a2, gla_b_a, gla_out_g, w_out, xattn_norm_g, mem_norm_g, xattn_w_q, xattn_w_k, xattn_w_v, xattn_w_o, ffn2_norm_g, ffn2_w_gate, ffn2_w_up, ffn2_w_down, final_norm_g):
    assert ffn1_norm_g.shape[0] == 1, "single-layer stack"
    layer = dict(ffn1_norm_g=ffn1_norm_g, ffn1_w_gate=ffn1_w_gate, ffn1_w_up=ffn1_w_up, ffn1_w_down=ffn1_w_down,
                 mix_norm_g=mix_norm_g, w_in=w_in, mlstm_b_i=mlstm_b_i, mlstm_b_f=mlstm_b_f,
                 mlstm_out_g=mlstm_out_g, gla_w_a2=gla_w_a2, gla_b_a=gla_b_a, gla_out_g=gla_out_g, w_out=w_out,
                 xattn_norm_g=xattn_norm_g, mem_norm_g=mem_norm_g, xattn_w_q=xattn_w_q, xattn_w_k=xattn_w_k,
                 xattn_w_v=xattn_w_v, xattn_w_o=xattn_w_o, ffn2_norm_g=ffn2_norm_g, ffn2_w_gate=ffn2_w_gate,
                 ffn2_w_up=ffn2_w_up, ffn2_w_down=ffn2_w_down)
    p = {name: arr[0] for name, arr in layer.items()}
    p["final_g"] = final_norm_g
    w = _prep_weights(p)

    bp, tp, _ = x_prompt.shape
    bs, ts, _ = x_sample.shape

    state = (state_mlstm_c[0], state_mlstm_n[0], state_mlstm_m[0].reshape(bs, 1, ML_HEADS), state_gla_s[0])
    x1_s, zq_s, zg_s, *early = _ffn_in_stream(x_sample, w, F32)
    w.update(zip(EARLY_WEIGHTS, early))
    ym_s, c_s, n_s, m_s, s_s = _mixer_state(zq_s, zg_s, w, state, ts, SAMPLE_MIXER_BATCHES)
    states_s = (c_s[None], n_s[None], m_s.reshape(1, bs, ML_HEADS), s_s[None])
    x2_s, q_s = _post_mix(x1_s, ym_s, w, TM_FFN_IN, F32)

    x1_p, ym_p, states_p, o_s, late = _ffn_mix(x_prompt, w, TM_FFN_IN, q_s, cache_mem_k[0], cache_mem_v[0],
                                               [w[name] for name in LATE_WEIGHTS])
    w.update(zip(LATE_WEIGHTS, late))
    y_p, mem_k_p, mem_v_p = _post_fused(x1_p.reshape(bp, tp, D_MODEL), ym_p.reshape(bp, tp, D_MODEL), mem_prompt, w,
                                        TM_POST)
    y_s = _ffn_out_stream(x2_s, o_s, w, bs)

    return (y_p, y_s, _cache_rows_unview(mem_k_p)[None], _cache_rows_unview(mem_v_p)[None]) + states_p + states_s
```

```python
import functools

import jax
import jax.numpy as jnp
import numpy as np
from jax import lax
from jax.experimental import pallas as pl
from jax.experimental.pallas import tpu as pltpu

F32 = jnp.float32
BF16 = jnp.bfloat16

D_MODEL = 1024
D_FF = 2816
ML_HEADS = 4
ML_DK = 128
ML_DV = 128
GLA_HEADS = 4
GLA_DK = 64
GLA_DV = 128
GLA_RANK = 16
GLA_TAU = 16.0
N_MEM = 256
XA_HEADS = 4
XA_DH = D_MODEL // XA_HEADS
EPS = 1e-6
CHUNK = 64
LOG2_E = 1.4426950408889634
LN_2 = 0.6931471805599453
LANES = 128
SUBLANES = 8
BF16_SUBLANES = 2 * SUBLANES

ZQ_MQ, ZQ_MK, ZQ_MV = 0, 512, 1024
ZQ_GQ, ZQ_GK, ZQ_GV = 1536, 1792, 2048
ZQ_W = 2560
ZG_MO, ZG_GG, ZG_SMALL = 0, 512, 1024
ZG_W = 1152
SMALL_W = LANES
GLA_QK_W = GLA_HEADS * GLA_DK
GLA_V_W = GLA_HEADS * GLA_DV
GLA_PAIRS = GLA_HEADS // 2
GLA_SUB = SUBLANES

MXU_WIDTH = 256
FF_CHUNKS = ((0, 6 * MXU_WIDTH), (6 * MXU_WIDTH, D_FF))
VMEM_LIMIT_BYTES = 56 * 1024 * 1024

TM_FFN_IN = MXU_WIDTH
TM_POST = 2 * MXU_WIDTH
SAMPLE_MIXER_BATCHES = 16


def _rms(x, g):
    return x * lax.rsqrt(jnp.mean(x * x, axis=-1, keepdims=True) + EPS) * g


def _log_sigmoid(x):
    return jnp.minimum(x, 0.0) - jnp.log1p(jnp.exp(-jnp.abs(x)))


def _dot(a, b):
    return jnp.dot(a, b, preferred_element_type=F32)


def _dot_nt(a, b):
    return lax.dot_general(a, b, (((1,), (1,)), ((), ())), preferred_element_type=F32)


def _dot_f32(a, b):
    return jnp.dot(a, b, precision=lax.Precision.HIGHEST, preferred_element_type=F32)


def _swiglu_residual(x, g_ref, wg_ref, wu_ref, wd_ref):
    h = _rms(x, g_ref[...]).astype(BF16)
    acts = []
    for lo, hi in FF_CHUNKS:
        g = _dot(h, wg_ref[:, lo:hi])
        u = _dot(h, wu_ref[:, lo:hi])
        acts.append(((g * jax.nn.sigmoid(g)) * u).astype(BF16))
    acc = jnp.zeros_like(x)
    for (lo, hi), a in zip(FF_CHUNKS, acts):
        acc = acc + _dot(a, wd_ref[lo:hi, :])
    return x + 0.5 * acc


def _const_spec(shape):
    nd = len(shape)
    return pl.BlockSpec(shape, lambda *_: (0,) * nd, pipeline_mode=pl.Buffered(1))


def _params(sem, vmem_mib=None):
    limit = VMEM_LIMIT_BYTES if vmem_mib is None else vmem_mib * 1024 * 1024
    return pltpu.CompilerParams(dimension_semantics=sem, vmem_limit_bytes=limit)


ZQ_BLOCKS = 4


def _ffn_in_stream_kernel(x_ref, g1_ref, wg_ref, wu_ref, wd_ref, gm_ref, wq_ref, wgt_ref, sq1_ref, sq2_ref,
                          x1_ref, zq_ref, zg_ref, wg_out, wu_out, wd_out, sq1_out, sq2_out, h_s, acc_s, x_s, *, n_ff):
    j = pl.program_id(0)

    @pl.when(j == 0)
    def _norm():
        x = x_ref[...].reshape(x_s.shape)
        x_s[...] = x
        h_s[...] = _rms(x, g1_ref[...]).astype(BF16)
        acc_s[...] = jnp.zeros_like(acc_s)

    @pl.when(j < n_ff)
    def _ffn_chunk():
        wg, wu, wd = (ref[...].astype(BF16) for ref in (wg_ref, wu_ref, wd_ref))
        wg_out[...] = wg
        wu_out[...] = wu
        wd_out[...] = wd
        h = h_s[...]
        g = _dot(h, wg)
        u = _dot(h, wu)
        acc_s[...] += _dot(((g * jax.nn.sigmoid(g)) * u).astype(BF16), wd)

    @pl.when(j == n_ff)
    def _residual():
        x1 = x_s[...] + 0.5 * acc_s[...]
        x1_ref[...] = x1
        h_s[...] = _rms(x1, gm_ref[...]).astype(BF16)

    @pl.when((j >= n_ff) & (j < n_ff + ZQ_BLOCKS))
    def _zq_block():
        zq_ref[...] = _dot(h_s[...], wq_ref[...]).astype(zq_ref.dtype)
        sq1_out[...] = sq1_ref[...].astype(BF16)
        sq2_out[...] = sq2_ref[...].astype(BF16)

    @pl.when(j == n_ff + ZQ_BLOCKS)
    def _zg():
        zg_ref[...] = _dot(h_s[...], wgt_ref[...])


def _ffn_in_stream(x, w, zq_dtype):
    n = x.shape[0] * x.shape[1]
    n_ff = D_FF // MXU_WIDTH
    zq_blk = ZQ_W // ZQ_BLOCKS
    ff = lambda j: jnp.minimum(j, n_ff - 1)
    zqb = lambda j: jnp.clip(j - n_ff, 0, ZQ_BLOCKS - 1)
    ff_cols = pl.BlockSpec((D_MODEL, MXU_WIDTH), lambda j: (0, ff(j)))
    ff_rows = pl.BlockSpec((MXU_WIDTH, D_MODEL), lambda j: (ff(j), 0))
    resident = lambda width: pl.BlockSpec((n, width), lambda j: (0, 0))
    sq_rows = pl.BlockSpec((D_MODEL // ZQ_BLOCKS, D_MODEL), lambda j: (zqb(j), 0))
    return pl.pallas_call(
        functools.partial(_ffn_in_stream_kernel, n_ff=n_ff),
        grid=(n_ff + ZQ_BLOCKS + 1,),
        in_specs=[_const_spec(x.shape), _const_spec((1, D_MODEL)), ff_cols, ff_cols, ff_rows,
                  _const_spec((1, D_MODEL)), pl.BlockSpec((D_MODEL, zq_blk), lambda j: (0, zqb(j))),
                  _const_spec((D_MODEL, ZG_W)), sq_rows, sq_rows],
        out_specs=[resident(D_MODEL), pl.BlockSpec((n, zq_blk), lambda j: (0, zqb(j))), resident(ZG_W),
                   ff_cols, ff_cols, ff_rows, sq_rows, sq_rows],
        out_shape=[jax.ShapeDtypeStruct((n, D_MODEL), F32), jax.ShapeDtypeStruct((n, ZQ_W), zq_dtype),
                   jax.ShapeDtypeStruct((n, ZG_W), F32), jax.ShapeDtypeStruct((D_MODEL, D_FF), BF16),
                   jax.ShapeDtypeStruct((D_MODEL, D_FF), BF16), jax.ShapeDtypeStruct((D_FF, D_MODEL), BF16),
                   jax.ShapeDtypeStruct((D_MODEL, D_MODEL), BF16), jax.ShapeDtypeStruct((D_MODEL, D_MODEL), BF16)],
        scratch_shapes=[pltpu.VMEM((n, D_MODEL), BF16), pltpu.VMEM((n, D_MODEL), F32),
                        pltpu.VMEM((n, D_MODEL), F32)],
        compiler_params=_params(("arbitrary",), vmem_mib=40),
        name="ffn_in",
    )(x, w["ffn1_g"], w["ffn1_wg"], w["ffn1_wu"], w["ffn1_wd"], w["mix_g"], w["w_in_q"], w["w_in_g"],
      w["w_out"], w["xattn_wq"])


def _mixer_stages(zqs, zgs, states, consts, ee_ref, *, L, t_real, chain):
    bias, wa2, ba, gml, ggl = consts
    nchunk = len(zqs)
    groups = [(b, h) for b in range(nchunk) for h in range(ML_HEADS)]
    pairs = [(b, p) for b in range(nchunk) for p in range(GLA_PAIRS)]
    padded = t_real < L
    valid = lax.broadcasted_iota(jnp.int32, (L, 1), 0) < t_real
    rr = lax.broadcasted_iota(jnp.int32, (L, L), 0)
    cc = lax.broadcasted_iota(jnp.int32, (L, L), 1)
    tril = cc <= rr
    c = GLA_SUB
    nb = L // c
    lane = lax.broadcasted_iota(jnp.int32, (1, LANES), 1)
    lane_blk = (lane % GLA_DK) // c
    lane_head = lane // GLA_DK
    v_lane_head = lax.broadcasted_iota(jnp.int32, (1, 2 * GLA_DV), 1) // GLA_DV
    row_blk = (lax.broadcasted_iota(jnp.int32, (2 * L, 1), 0) % L) // c
    t_in = lax.broadcasted_iota(jnp.int32, (1, c, 1), 1)
    new_states = [dict(c=[None] * ML_HEADS, n=[None] * ML_HEADS, m=[None] * ML_HEADS, s=[None] * GLA_HEADS)
                  for _ in range(nchunk)]
    prev = lambda b: new_states[b - 1] if chain and b > 0 else states[0 if chain else b]
    known = lambda b: not chain or b == 0

    carried = {}

    def carried_mlstm(b, h):
        st = prev(b)
        d = ml[(b, h)]
        carried[(b, h)] = _dot(d["qb"], st["c"][h].astype(BF16))

    def carried_gla(b, p):
        s_prev = prev(b)["s"]
        zero_blk = jnp.zeros((GLA_DK, GLA_DV), BF16)
        s_bd = jnp.concatenate(
            [jnp.concatenate([s_prev[2 * p].astype(BF16), zero_blk], axis=1),
             jnp.concatenate([zero_blk, s_prev[2 * p + 1].astype(BF16)], axis=1)], axis=0)
        carried[(b, "gla", p)] = _dot(gl[(b, p)]["q_dec"], s_bd)

    smalls, sms, lfs, b_cols, b_rows, sm_ts, las, bcs = [], [], [], [], [], [], [], []
    for b in range(nchunk):
        small = zgs[b][:, ZG_SMALL:ZG_SMALL + SMALL_W]
        sm = small + bias
        lf = _log_sigmoid(sm) * LOG2_E
        sm = sm * LOG2_E
        if padded:
            sm = jnp.where(valid, sm, -jnp.inf)
            lf = jnp.where(valid, lf, 0.0)
        smalls.append(small)
        sms.append(sm)
        lfs.append(lf)

    long_chunk = L > SUBLANES
    col_w = 1 if long_chunk else LANES
    tril_f, triu_f = tril.astype(F32), (rr <= cc).astype(F32)

    def cumsum_rows(x):
        if long_chunk:
            return _dot_f32(tril_f, x)
        acc = x[0:1]
        rows = [acc]
        for r in range(1, L):
            acc = acc + x[r:r + 1]
            rows.append(acc)
        return jnp.concatenate(rows, axis=0)

    lf_ts = [lf.T[0:SUBLANES] for lf in lfs] if long_chunk else None
    yield
    for b in range(nchunk):
        b_cols.append(cumsum_rows(lfs[b]))
        b_rows.append(_dot_f32(lf_ts[b], triu_f) if long_chunk else b_cols[b].T[0:SUBLANES])
        sm_ts.append(sms[b].T)
        la = _log_sigmoid(_dot(smalls[b].astype(BF16), wa2) + ba) * (LOG2_E / GLA_TAU)
        las.append(jnp.where(valid, la, 0.0) if padded else la)

    ml = {}
    for g in groups:
        b, h = g
        zq = zqs[b]
        qf = zq[:, ZQ_MQ + h * ML_DK:ZQ_MQ + (h + 1) * ML_DK].astype(F32)
        kf = zq[:, ZQ_MK + h * ML_DK:ZQ_MK + (h + 1) * ML_DK].astype(F32) * (ML_DK ** -0.5)
        vf = zq[:, ZQ_MV + h * ML_DV:ZQ_MV + (h + 1) * ML_DV].astype(F32)
        if padded:
            kf = jnp.where(valid, kf, 0.0)
            vf = jnp.where(valid, vf, 0.0)
        ml[g] = dict(qf=qf, qb=qf.astype(BF16), kf=kf, kb=kf.astype(BF16), vb=vf.astype(BF16))
    yield
    for b in range(nchunk):
        bcs.append(cumsum_rows(las[b]))
    for g in groups:
        d = ml[g]
        d["qk"] = _dot_nt(d["qb"], d["kb"])
        if known(g[0]):
            carried_mlstm(*g)

    yield
    for g in groups:
        b, h = g
        d = ml[g]
        i_col = jnp.broadcast_to(sms[b][:, h:h + 1], (L, col_w))
        b_col = jnp.broadcast_to(b_cols[b][:, ML_HEADS + h:ML_HEADS + h + 1], (L, col_w))
        b_row = b_rows[b][ML_HEADS + h:ML_HEADS + h + 1, :]
        i_row = sm_ts[b][h:h + 1, :]
        a_col = b_col + prev(b)["m"][h] * LOG2_E
        dm = jnp.where(tril, b_col[:, :L] - (b_row - i_row), -jnp.inf)
        mt = jnp.maximum(a_col, jnp.max(dm, axis=1, keepdims=True))
        w_inter = jnp.exp2(a_col - mt)
        s = d["qk"] * jnp.exp2(dm - mt[:, :L])
        kw = d["kf"] * jnp.exp2((b_col[L - 1:L] - mt[L - 1:L]) - (b_col - i_col))
        d.update(mt=mt, w_inter=w_inter, s=s, kw=kw, kw_t=kw.T.astype(BF16))
        new_states[b]["m"][h] = mt[L - 1:L, 0:1] * LN_2

    gl = {}
    for b in range(nchunk):
        zq = zqs[b]
        gq = zq[:, ZQ_GQ:ZQ_GQ + GLA_QK_W].astype(F32) * (GLA_DK ** -0.5)
        gk = zq[:, ZQ_GK:ZQ_GK + GLA_QK_W].astype(F32)
        gv = zq[:, ZQ_GV:ZQ_GV + GLA_V_W].astype(F32)
        if padded:
            gk = jnp.where(valid, gk, 0.0)
            gv = jnp.where(valid, gv, 0.0)
        stack = lambda x: jnp.concatenate([x[:, :LANES], x[:, LANES:]], axis=0)
        q2, k2, b2 = stack(gq), stack(gk), stack(bcs[b])
        q3 = q2.reshape(2 * nb, c, LANES)
        k3 = k2.reshape(2 * nb, c, LANES)
        b3 = b2.reshape(2 * nb, c, LANES)
        pair_terms = []
        for j in range(min(c, t_real)):
            decay = jnp.exp2(jnp.where(t_in >= j, b3 - b3[:, j:j + 1, :], -jnp.inf))
            pair_terms.append((q3 * k3[:, j:j + 1, :] * decay).reshape(2 * L, LANES).astype(BF16))
        kt2 = (k3 * jnp.exp2(b3[:, c - 1:c, :] - b3)).reshape(2 * L, LANES) if nb > 1 else None
        gl[b] = dict(gv=gv, q2=q2, k2=k2, b2=b2, pair_terms=pair_terms, kt2=kt2)
    yield
    for b in range(nchunk):
        acc = jnp.zeros((2 * L, LANES), F32)
        for j, pair_j in enumerate(gl[b]["pair_terms"]):
            acc = acc + _dot(pair_j, ee_ref[j])
        gl[b]["a_diag"] = jnp.where(lane_blk == row_blk, acc, 0.0)

    for bp in pairs:
        b, p = bp
        d = gl[b]
        rows_p = slice(p * L, (p + 1) * L)
        q_p, k_p, b_p = d["q2"][rows_p], d["k2"][rows_p], d["b2"][rows_p]
        if long_chunk:
            decay_col = jnp.exp2(b_p[L - SUBLANES:L].T[:, SUBLANES - 1:SUBLANES])
        else:
            decay_col = jnp.exp2(jnp.broadcast_to(b_p[L - 1:L], (LANES, LANES)).T)
        e = dict(q_dec=(q_p * jnp.exp2(b_p)).astype(BF16),
                 kh_t=(k_p * jnp.exp2(b_p[L - 1:L] - b_p)).T.astype(BF16),
                 decay_col=decay_col)
        if nb > 1:
            kt_p = d["kt2"][rows_p].astype(BF16)
            k_bd = jnp.concatenate([jnp.where(lane_head == hh, kt_p, 0.0) for hh in range(2)], axis=0)
            slabs, offs = [], []
            off = 0
            for j in range(nb - 1):
                lo = (j + 1) * c
                slabs.append(q_p[lo:] * jnp.exp2(b_p[lo:] - b_p[lo - 1:lo]))
                offs.append(off)
                off += L - lo
            e.update(q_var=jnp.concatenate(slabs, axis=0).astype(BF16), k_bd=k_bd, offs=offs)
        gl[bp] = e
    yield
    for bp in pairs:
        e = gl[bp]
        if nb > 1:
            e["r"] = _dot_nt(e["q_var"], e["k_bd"])
        if known(bp[0]):
            carried_gla(*bp)

    for g in groups:
        d = ml[g]
        d["sv"] = _dot(d["s"].astype(BF16), d["vb"])
        d["c_upd"] = _dot(d["kw_t"], d["vb"])
        d["den"] = jnp.sum(d["s"], axis=1, keepdims=True)
        if known(g[0]):
            d["qn"] = jnp.sum(d["qf"] * prev(g[0])["n"][g[1]], axis=1, keepdims=True)
    yield
    for bp in pairs:
        b, p = bp
        d, e = gl[b], gl[bp]
        a_p = d["a_diag"][p * L:(p + 1) * L]
        if nb > 1:
            blocks = []
            for i in range(nb):
                blk = a_p[i * c:(i + 1) * c]
                for j in range(i):
                    lo_r = e["offs"][j] + (i - j - 1) * c
                    blk = jnp.where(lane_blk == j, e["r"][lo_r:lo_r + c], blk)
                blocks.append(blk)
            a_p = jnp.concatenate(blocks, axis=0)
        v_f = d["gv"][:, p * 2 * GLA_DV:(p + 1) * 2 * GLA_DV]
        v_p = v_f.astype(BF16)
        if L < GLA_DK:
            v_rows = []
            for hh in range(2):
                v_rows += [jnp.where(v_lane_head == hh, v_f, 0.0), jnp.zeros((GLA_DK - L, 2 * GLA_DV), F32)]
            v_bd = jnp.concatenate(v_rows, axis=0).astype(BF16)
        else:
            v_bd = jnp.concatenate([jnp.where(v_lane_head == hh, v_p, 0.0) for hh in range(2)], axis=0)
        e["s_upd"] = [_dot(e["kh_t"][hh * GLA_DK:(hh + 1) * GLA_DK], v_p[:, hh * GLA_DV:(hh + 1) * GLA_DV])
                      for hh in range(2)]
        e.update(scores=a_p.astype(BF16), v_bd=v_bd)
    yield
    for bp in pairs:
        e = gl[bp]
        e["o_intra"] = _dot(e["scores"], e["v_bd"])

    yield
    ys = [[None] * (ML_HEADS + GLA_HEADS) for _ in range(nchunk)]

    def finish_mlstm(b):
        st = prev(b)
        for h in range(ML_HEADS):
            d = ml[(b, h)]
            mt, w_inter = d["mt"], d["w_inter"]
            qn = d["qn"] if known(b) else jnp.sum(d["qf"] * st["n"][h], axis=1, keepdims=True)
            den = d["den"] + w_inter * qn
            hh = (d["sv"] + w_inter * carried[(b, h)]) / jnp.maximum(jnp.abs(den), jnp.exp2(-mt))
            i_last = w_inter[L - 1:L]
            new_states[b]["c"][h] = i_last * st["c"][h] + d["c_upd"]
            new_states[b]["n"][h] = i_last * st["n"][h] + jnp.sum(d["kw"], axis=0, keepdims=True)
            hs = slice(h * ML_DV, (h + 1) * ML_DV)
            yn = hh * lax.rsqrt(jnp.mean(hh * hh, axis=-1, keepdims=True) + EPS) * gml[:, hs]
            ys[b][h] = jax.nn.sigmoid(zgs[b][:, ZG_MO + h * ML_DV:ZG_MO + (h + 1) * ML_DV]) * yn

    def finish_gla(b):
        st = prev(b)
        for p in range(GLA_PAIRS):
            e = gl[(b, p)]
            o = carried[(b, "gla", p)] + e["o_intra"]
            for hh in range(2):
                h = 2 * p + hh
                ds = slice(hh * GLA_DK, (hh + 1) * GLA_DK)
                vs = slice(hh * GLA_DV, (hh + 1) * GLA_DV)
                new_states[b]["s"][h] = e["decay_col"][ds] * st["s"][h] + e["s_upd"][hh]
                oh = o[:, vs]
                yn = oh * lax.rsqrt(jnp.mean(oh * oh, axis=-1, keepdims=True) + EPS) * ggl[:, h * GLA_DV:(h + 1) * GLA_DV]
                gg = zgs[b][:, ZG_GG + h * GLA_DV:ZG_GG + (h + 1) * GLA_DV]
                ys[b][ML_HEADS + h] = (gg * jax.nn.sigmoid(gg)) * yn

    if chain:
        for b in range(nchunk):
            if not known(b):
                for h in range(ML_HEADS):
                    carried_mlstm(b, h)
                for p in range(GLA_PAIRS):
                    carried_gla(b, p)
            finish_mlstm(b)
            finish_gla(b)
    else:
        for b in range(nchunk):
            finish_mlstm(b)
        for b in range(nchunk):
            finish_gla(b)
    return [jnp.concatenate(y, axis=1) for y in ys], new_states


def _drain(gen):
    try:
        while True:
            next(gen)
    except StopIteration as done:
        return done.value


def _mixer_compute(*args, **kwargs):
    return _drain(_mixer_stages(*args, **kwargs))


def _mixer_state_kernel(zq_ref, zg_ref, bias_ref, wa2_ref, ba_ref, gml_ref, ggl_ref, ee_ref, c0_ref, n0_ref, m0_ref,
                        s0_ref, y_ref, co_ref, no_ref, mo_ref, so_ref, *, L, t_real, nblk):
    consts = (bias_ref[...], wa2_ref[...], ba_ref[...], gml_ref[...], ggl_ref[...])
    lane = lax.broadcasted_iota(jnp.int32, (1, LANES), 1)

    def chunk_rows(ref, b):
        rows = ref[b * t_real:(b + 1) * t_real, :]
        return jnp.concatenate([rows, jnp.zeros((L - t_real, rows.shape[1]), rows.dtype)], axis=0)

    states = []
    for b in range(nblk):
        n_all = n0_ref[b]
        m_all = m0_ref[b]
        states.append(dict(c=[c0_ref[b, h] for h in range(ML_HEADS)],
                           n=[n_all[h:h + 1, :] for h in range(ML_HEADS)],
                           m=[m_all[:, h:h + 1] for h in range(ML_HEADS)],
                           s=[s0_ref[b, h] for h in range(GLA_HEADS)]))
    ys, new_states = _mixer_compute([chunk_rows(zq_ref, b) for b in range(nblk)],
                                    [chunk_rows(zg_ref, b) for b in range(nblk)], states, consts, ee_ref, L=L,
                                    t_real=t_real, chain=False)
    for b in range(nblk):
        st = new_states[b]
        y_ref[b * t_real:(b + 1) * t_real, :] = ys[b][0:t_real].astype(y_ref.dtype)
        for h in range(ML_HEADS):
            co_ref[b, h] = st["c"][h]
            so_ref[b, h] = st["s"][h]
        no_ref[b] = jnp.concatenate(st["n"], axis=0)
        m_row = jnp.zeros((1, LANES), F32)
        for h in range(ML_HEADS):
            m_row = jnp.where(lane == h, st["m"][h], m_row)
        mo_ref[b] = m_row[:, 0:ML_HEADS]


def _mixer_state(zq, zg, w, state, t_real, nblk):
    bsz = zq.shape[0] // t_real
    L = -(-t_real // SUBLANES) * SUBLANES
    rows = lambda width: pl.BlockSpec((nblk * t_real, width), lambda b: (b, 0))
    per_b = lambda *tail: pl.BlockSpec((nblk,) + tail, lambda b: (b,) + (0,) * len(tail))
    state_specs = [per_b(ML_HEADS, ML_DK, ML_DV), per_b(ML_HEADS, ML_DK), per_b(1, ML_HEADS),
                   per_b(GLA_HEADS, GLA_DK, GLA_DV)]
    consts = [_const_spec((1, SMALL_W)), _const_spec((SMALL_W, GLA_QK_W)), _const_spec((1, GLA_QK_W)),
              _const_spec((1, ML_HEADS * ML_DV)), _const_spec((1, GLA_V_W)), _const_spec((GLA_SUB, LANES, LANES))]
    return pl.pallas_call(
        functools.partial(_mixer_state_kernel, L=L, t_real=t_real, nblk=nblk),
        grid=(bsz // nblk,),
        in_specs=[rows(ZQ_W), rows(ZG_W)] + consts + state_specs,
        out_specs=[rows(D_MODEL)] + state_specs,
        out_shape=[jax.ShapeDtypeStruct((bsz * t_real, D_MODEL), F32),
                   jax.ShapeDtypeStruct((bsz, ML_HEADS, ML_DK, ML_DV), F32),
                   jax.ShapeDtypeStruct((bsz, ML_HEADS, ML_DK), F32),
                   jax.ShapeDtypeStruct((bsz, 1, ML_HEADS), F32),
                   jax.ShapeDtypeStruct((bsz, GLA_HEADS, GLA_DK, GLA_DV), F32)],
        compiler_params=_params(("arbitrary",)),
        name="mixer_state",
    )(zq, zg, w["gate_bias"], w["w_a2"], w["b_a"], w["mlstm_out_g"], w["gla_out_g"], w["diag_sum"], *state)


def _ffn_in_pieces(x, g1_ref, wg_ref, wu_ref, wd_ref, gm_ref, wq_ref, wgt_ref, x1_ref, zq_out, zg_out):
    h = _rms(x, g1_ref[...]).astype(BF16)
    acts = []
    for lo in range(0, D_FF, MXU_WIDTH):
        g = _dot(h, wg_ref[:, lo:lo + MXU_WIDTH])
        yield
        u = _dot(h, wu_ref[:, lo:lo + MXU_WIDTH])
        yield
        acts.append(((g * jax.nn.sigmoid(g)) * u).astype(BF16))
    a = jnp.concatenate(acts, axis=1)
    down = []
    for lo in range(0, D_MODEL, MXU_WIDTH):
        down.append(_dot(a, wd_ref[:, lo:lo + MXU_WIDTH]))
        yield
    x1 = x + 0.5 * jnp.concatenate(down, axis=1)
    x1_ref[...] = x1
    hm = _rms(x1, gm_ref[...]).astype(BF16)
    for lo in range(0, ZQ_W, MXU_WIDTH):
        zq_out[:, lo:lo + MXU_WIDTH] = _dot(hm, wq_ref[:, lo:lo + MXU_WIDTH]).astype(zq_out.dtype)
        yield
    for lo in range(0, ZG_W, MXU_WIDTH):
        hi = min(lo + MXU_WIDTH, ZG_W)
        zg_out[:, lo:hi] = _dot(hm, wgt_ref[:, lo:hi])
        yield


ATTN_STAGE_EVERY = 7
FFN_PIECES_PER_MIXER_STAGE = 1


def _ffn_mix_kernel(*refs, tiles_per_batch, n_cast):
    it = iter(refs)
    (x_ref, g1_ref, wg_ref, wu_ref, wd_ref, gm_ref, wq_ref, wgt_ref, bias_ref, wa2_ref, ba_ref, gml_ref, ggl_ref,
     ee_ref, qs_ref, ks_ref, vs_ref) = (next(it) for _ in range(17))
    cast_src = [next(it) for _ in range(n_cast)]
    x1_ref, ym_ref, co_ref, no_ref, mo_ref, so_ref, os_ref = (next(it) for _ in range(7))
    cast_dst = [next(it) for _ in range(n_cast)]
    zq_s, zg_s, c_s, n_s, m_s, s_s = (next(it) for _ in range(6))
    i = pl.program_id(0)
    for src, dst in zip(cast_src, cast_dst):
        dst[...] = src[...].astype(dst.dtype)

    @pl.when(i == 0)
    def _init():
        for ref in (zq_s, zg_s, c_s, n_s, m_s, s_s):
            ref[...] = jnp.zeros_like(ref)

    consts = (bias_ref[...], wa2_ref[...], ba_ref[...], gml_ref[...], ggl_ref[...])
    n_chunks = zq_s.shape[0] // CHUNK
    rows = [slice(k * CHUNK, (k + 1) * CHUNK) for k in range(n_chunks)]
    zqs = [zq_s[r, :] for r in rows]
    zgs = [zg_s[r, :] for r in rows]
    starts_batch = (i - 1) % tiles_per_batch == 0
    carry = lambda v: jnp.where(starts_batch, 0.0, v)
    state = dict(c=[carry(c_s[h]) for h in range(ML_HEADS)], n=[carry(n_s[h:h + 1, :]) for h in range(ML_HEADS)],
                 m=[carry(m_s[h:h + 1, 0:1]) for h in range(ML_HEADS)], s=[carry(s_s[h]) for h in range(GLA_HEADS)])

    def mixers():
        st = state
        for k in range(n_chunks):
            ys, new = yield from _mixer_stages([zqs[k]], [zgs[k]], [st], consts, ee_ref, L=CHUNK, t_real=CHUNK,
                                               chain=True)
            ym_ref[rows[k], :] = ys[0].astype(ym_ref.dtype)
            st = new[0]
            yield
        return st

    ffn = _ffn_in_pieces(x_ref[...], g1_ref, wg_ref, wu_ref, wd_ref, gm_ref, wq_ref, wgt_ref, x1_ref, zq_s, zg_s)
    mix = mixers()
    attn = _xattn_cache_stages(qs_ref, ks_ref, vs_ref, os_ref)
    live = {"ffn": True, "attn": True}

    def advance(name, gen):
        if live[name]:
            try:
                next(gen)
            except StopIteration:
                live[name] = False

    slot = 0
    while True:
        try:
            next(mix)
        except StopIteration as done:
            last = done.value
            break
        for _ in range(FFN_PIECES_PER_MIXER_STAGE):
            advance("ffn", ffn)
        if slot % ATTN_STAGE_EVERY == ATTN_STAGE_EVERY // 2:
            advance("attn", attn)
        slot += 1
    for name, gen in (("ffn", ffn), ("attn", attn)):
        if live[name]:
            _drain(gen)

    for h in range(ML_HEADS):
        c_s[h] = last["c"][h]
        n_s[h:h + 1, :] = last["n"][h]
        m_s[h:h + 1, :] = jnp.broadcast_to(last["m"][h], (1, LANES))
        s_s[h] = last["s"][h]

    @pl.when(jnp.logical_and(i >= 1, (i - 1) % tiles_per_batch == tiles_per_batch - 1))
    def _emit_state():
        lane = lax.broadcasted_iota(jnp.int32, (1, LANES), 1)
        for h in range(ML_HEADS):
            co_ref[0, h] = last["c"][h]
            so_ref[0, h] = last["s"][h]
        no_ref[0] = jnp.concatenate(last["n"], axis=0)
        m_row = jnp.zeros((1, LANES), F32)
        for h in range(ML_HEADS):
            m_row = jnp.where(lane == h, last["m"][h], m_row)
        mo_ref[0] = m_row[:, 0:ML_HEADS]


def _ffn_mix(x, w, tm, q_s, k_cache, v_cache, to_cast):
    bsz, t, _ = x.shape
    tiles_per_batch = t // tm
    n_tiles = bsz * tiles_per_batch
    assert t % tm == 0 and tm % CHUNK == 0
    bs = k_cache.shape[0]
    ts = q_s.shape[0] // bs
    assert bs % n_tiles == 0
    sb = bs // n_tiles
    cur_tile = lambda i: jnp.minimum(i, n_tiles - 1)
    prev_tile = lambda i: jnp.maximum(i - 1, 0)
    cur = lambda width: pl.BlockSpec((tm, width), lambda i: (cur_tile(i), 0))
    per_b = lambda *tail: pl.BlockSpec((1,) + tail, lambda i: (prev_tile(i) // tiles_per_batch,) + (0,) * len(tail))
    state_specs = [per_b(ML_HEADS, ML_DK, ML_DV), per_b(ML_HEADS, ML_DK), per_b(1, ML_HEADS),
                   per_b(GLA_HEADS, GLA_DK, GLA_DV)]
    qo_s = pl.BlockSpec((sb * ts, D_MODEL), lambda i: (cur_tile(i), 0))
    kv_s = pl.BlockSpec((sb, N_MEM * CACHE_ROW_GROUP, LANES), lambda i: (cur_tile(i), 0, 0))
    cast_specs = []
    for a in to_cast:
        rb = next(r for r in range(BF16_SUBLANES, a.shape[0] + 1, BF16_SUBLANES)
                  if a.shape[0] % r == 0 and a.shape[0] // r <= n_tiles)
        cast_specs.append(pl.BlockSpec((rb, a.shape[1]), lambda i, last=a.shape[0] // rb - 1: (jnp.minimum(i, last), 0)))
    x1, ym, c_new, n_new, m_new, s_new, o_s, *cast = pl.pallas_call(
        functools.partial(_ffn_mix_kernel, tiles_per_batch=tiles_per_batch, n_cast=len(to_cast)),
        grid=(n_tiles + 1,),
        in_specs=[cur(D_MODEL), _const_spec((1, D_MODEL)), _const_spec((D_MODEL, D_FF)),
                  _const_spec((D_MODEL, D_FF)), _const_spec((D_FF, D_MODEL)), _const_spec((1, D_MODEL)),
                  _const_spec((D_MODEL, ZQ_W)), _const_spec((D_MODEL, ZG_W)), _const_spec((1, SMALL_W)),
                  _const_spec((SMALL_W, GLA_QK_W)), _const_spec((1, GLA_QK_W)), _const_spec((1, ML_HEADS * ML_DV)),
                  _const_spec((1, GLA_V_W)), _const_spec((GLA_SUB, LANES, LANES)), qo_s, kv_s, kv_s] + cast_specs,
        out_specs=([cur(D_MODEL), pl.BlockSpec((tm, D_MODEL), lambda i: (prev_tile(i), 0))] + state_specs + [qo_s]
                   + cast_specs),
        out_shape=[jax.ShapeDtypeStruct((n_tiles * tm, D_MODEL), F32),
                   jax.ShapeDtypeStruct((n_tiles * tm, D_MODEL), BF16),
                   jax.ShapeDtypeStruct((bsz, ML_HEADS, ML_DK, ML_DV), F32),
                   jax.ShapeDtypeStruct((bsz, ML_HEADS, ML_DK), F32),
                   jax.ShapeDtypeStruct((bsz, 1, ML_HEADS), F32),
                   jax.ShapeDtypeStruct((bsz, GLA_HEADS, GLA_DK, GLA_DV), F32),
                   jax.ShapeDtypeStruct((bs * ts, D_MODEL), F32)] + [jax.ShapeDtypeStruct(a.shape, BF16) for a in to_cast],
        scratch_shapes=[pltpu.VMEM((tm, ZQ_W), BF16), pltpu.VMEM((tm, ZG_W), F32),
                        pltpu.VMEM((ML_HEADS, ML_DK, ML_DV), F32), pltpu.VMEM((SUBLANES, LANES), F32),
                        pltpu.VMEM((SUBLANES, LANES), F32), pltpu.VMEM((GLA_HEADS, GLA_DK, GLA_DV), F32)],
        compiler_params=_params(("arbitrary",)),
        name="ffn_mix",
    )(x.reshape(bsz * t, D_MODEL), w["ffn1_g"], w["ffn1_wg"], w["ffn1_wu"], w["ffn1_wd"], w["mix_g"], w["w_in_q"],
      w["w_in_g"], w["gate_bias"], w["w_a2"], w["b_a"], w["mlstm_out_g"], w["gla_out_g"], w["diag_sum"],
      q_s, _cache_rows_view(k_cache), _cache_rows_view(v_cache), *to_cast)
    return x1, ym, (c_new[None], n_new[None], m_new.reshape(1, bsz, ML_HEADS), s_new[None]), o_s, cast


def _post_mix_kernel(x1_ref, ym_ref, wout_ref, gx_ref, wq_ref, x2_ref, q_ref):
    x2 = x1_ref[...] + _dot(ym_ref[...].astype(BF16), wout_ref[...])
    x2_ref[...] = x2
    hq = _rms(x2, gx_ref[...]).astype(BF16)
    q_ref[...] = _dot(hq, wq_ref[...]).astype(q_ref.dtype)


def _post_mix(x1, ym, w, tm, q_dtype):
    n = x1.shape[0]
    row = lambda: pl.BlockSpec((tm, D_MODEL), lambda i: (i, 0))
    return pl.pallas_call(
        _post_mix_kernel,
        grid=(n // tm,),
        in_specs=[row(), row(), _const_spec((D_MODEL, D_MODEL)), _const_spec((1, D_MODEL)),
                  _const_spec((D_MODEL, D_MODEL))],
        out_specs=[row(), row()],
        out_shape=[jax.ShapeDtypeStruct((n, D_MODEL), F32), jax.ShapeDtypeStruct((n, D_MODEL), q_dtype)],
        compiler_params=_params(("arbitrary",), vmem_mib=20),
        name="post_mix",
    )(x1, ym, w["w_out"], w["xattn_g"], w["xattn_wq"])


def _softmax(s):
    e = jnp.exp(s - jnp.max(s, axis=-1, keepdims=True))
    return e / jnp.sum(e, axis=-1, keepdims=True)


def _post_fused_kernel(x1_ref, ym_ref, mem_ref, gm_ref, wk_ref, wv_ref, wout_ref, gx_ref, wq_ref, wo_ref, g2_ref,
                       wg_ref, wu_ref, wd_ref, gf_ref, y_ref, ko_ref, vo_ref, k_s, v_s):
    @pl.when(pl.program_id(1) == 0)
    def _memkv():
        hn = _rms(mem_ref[0], gm_ref[...]).astype(BF16)
        for w_ref, out_ref, s_ref in ((wk_ref, ko_ref, k_s), (wv_ref, vo_ref, v_s)):
            kv = _dot(hn, w_ref[...])
            _cache_rows_store(out_ref, 0, kv)
            s_ref[...] = kv.astype(BF16)

    x2 = x1_ref[0] + _dot(ym_ref[0].astype(BF16), wout_ref[...])
    q = _dot(_rms(x2, gx_ref[...]).astype(BF16), wq_ref[...]).astype(BF16)
    heads = [slice(h * XA_DH, (h + 1) * XA_DH) for h in range(XA_HEADS)]
    k_full, v_full = k_s[...], v_s[...]
    scores = [_dot_nt(q[:, hs], k_full[:, hs]) * (XA_DH ** -0.5) for hs in heads]
    probs = [_softmax(s).astype(BF16) for s in scores]
    o = jnp.concatenate([_dot(p, v_full[:, hs]).astype(BF16) for hs, p in zip(heads, probs)], axis=1)
    x3 = x2 + _dot(o, wo_ref[...])
    x4 = _swiglu_residual(x3, g2_ref, wg_ref, wu_ref, wd_ref)
    y_ref[0] = _rms(x4, gf_ref[...])


def _post_fused(x1, ym, mem, w, tm):
    bsz, t = x1.shape[0], x1.shape[1]
    row = lambda: pl.BlockSpec((1, tm, D_MODEL), lambda b, j: (b, j, 0))
    kv = pl.BlockSpec((1, N_MEM * CACHE_ROW_GROUP, LANES), lambda b, j: (b, 0, 0))
    sq = _const_spec((D_MODEL, D_MODEL))
    vec = _const_spec((1, D_MODEL))
    kv_shape = jax.ShapeDtypeStruct((bsz, N_MEM * CACHE_ROW_GROUP, LANES), F32)
    return pl.pallas_call(
        _post_fused_kernel,
        grid=(bsz, t // tm),
        in_specs=[row(), row(), pl.BlockSpec((1, N_MEM, D_MODEL), lambda b, j: (b, 0, 0)), vec, sq, sq,
                  sq, vec, sq, sq, vec, _const_spec((D_MODEL, D_FF)),
                  _const_spec((D_MODEL, D_FF)), _const_spec((D_FF, D_MODEL)), vec],
        out_specs=[row(), kv, kv],
        out_shape=[jax.ShapeDtypeStruct((bsz, t, D_MODEL), F32), kv_shape, kv_shape],
        scratch_shapes=[pltpu.VMEM((N_MEM, D_MODEL), BF16)] * 2,
        compiler_params=_params(("arbitrary", "arbitrary")),
        name="post_fused",
    )(x1, ym, mem, w["mem_g"], w["xattn_wk"], w["xattn_wv"], w["w_out"], w["xattn_g"], w["xattn_wq"],
      w["xattn_wo"], w["ffn2_g"], w["ffn2_wg"], w["ffn2_wu"], w["ffn2_wd"], w["final_g"])


XA_LANE_TILES = XA_DH // LANES
CACHE_ROW_GROUP = XA_LANE_TILES * XA_HEADS


def _cache_rows_view(x):
    bsz = x.shape[0]
    x = x.reshape(bsz, N_MEM, XA_HEADS, XA_LANE_TILES, LANES)
    return x.transpose(0, 1, 3, 2, 4).reshape(bsz, N_MEM * CACHE_ROW_GROUP, LANES)


def _cache_rows_unview(x):
    bsz = x.shape[0]
    x = x.reshape(bsz, N_MEM, XA_LANE_TILES, XA_HEADS, LANES)
    return x.transpose(0, 1, 3, 2, 4).reshape(bsz, N_MEM, XA_HEADS, XA_DH)


def _cache_rows_store(ref, b, x):
    for h in range(XA_HEADS):
        for lt in range(XA_LANE_TILES):
            lo = h * XA_DH + lt * LANES
            ref[b, pl.ds(lt * XA_HEADS + h, N_MEM, stride=CACHE_ROW_GROUP), :] = x[:, lo:lo + LANES]


def _cache_rows_load(ref, b):
    cols = [ref[b, pl.ds(lt * XA_HEADS + h, N_MEM, stride=CACHE_ROW_GROUP), :]
            for h in range(XA_HEADS) for lt in range(XA_LANE_TILES)]
    return jnp.concatenate(cols, axis=1).astype(BF16)


def _xattn_cache_stages(q_ref, k_ref, v_ref, o_ref):
    bb = k_ref.shape[0]
    tq = q_ref.shape[0] // bb
    lane_head = lax.broadcasted_iota(jnp.int32, (1, D_MODEL), 1) // XA_DH
    qs = [q_ref[b * tq:(b + 1) * tq, :] for b in range(bb)]
    q_bds = [jnp.concatenate([jnp.where(lane_head == h, q, 0.0) for h in range(XA_HEADS)], axis=0).astype(BF16)
             for q in qs]
    k_fulls = [_cache_rows_load(k_ref, b) for b in range(bb)]
    yield
    scores = [_dot_nt(q_bds[b], k_fulls[b]) * (XA_DH ** -0.5) for b in range(bb)]
    yield
    p_all = _softmax(jnp.concatenate(scores, axis=0)).astype(BF16)
    v_fulls = [_cache_rows_load(v_ref, b) for b in range(bb)]
    yield
    rows = XA_HEADS * tq
    o_fulls = [_dot(p_all[b * rows:(b + 1) * rows], v_fulls[b]) for b in range(bb)]
    yield
    for b in range(bb):
        o = jnp.zeros((tq, D_MODEL), F32)
        for h in range(XA_HEADS):
            o = jnp.where(lane_head == h, o_fulls[b][h * tq:(h + 1) * tq], o)
        o_ref[b * tq:(b + 1) * tq, :] = o.astype(o_ref.dtype)


def _ffn_out_stream_kernel(x2_ref, o_ref, wo_ref, g2_ref, wg_ref, wu_ref, wd_ref, gf_ref, y_ref, x3_s, h_s, acc_s):
    j = pl.program_id(0)

    @pl.when(j == 0)
    def _attn_out():
        x3 = x2_ref[...] + _dot(o_ref[...].astype(BF16), wo_ref[...])
        x3_s[...] = x3
        h_s[...] = _rms(x3, g2_ref[...]).astype(BF16)
        acc_s[...] = jnp.zeros_like(acc_s)

    h = h_s[...]
    g = _dot(h, wg_ref[...])
    u = _dot(h, wu_ref[...])
    acc_s[...] += _dot(((g * jax.nn.sigmoid(g)) * u).astype(BF16), wd_ref[...])

    @pl.when(j == pl.num_programs(0) - 1)
    def _final_norm():
        y_ref[...] = _rms(x3_s[...] + 0.5 * acc_s[...], gf_ref[...]).reshape(y_ref.shape)


def _ffn_out_stream(x2, o, w, bsz):
    n = x2.shape[0]
    y_shape = (bsz, n // bsz, D_MODEL)
    ff_cols = pl.BlockSpec((D_MODEL, MXU_WIDTH), lambda j: (0, j))
    ff_rows = pl.BlockSpec((MXU_WIDTH, D_MODEL), lambda j: (j, 0))
    return pl.pallas_call(
        _ffn_out_stream_kernel,
        grid=(D_FF // MXU_WIDTH,),
        in_specs=[_const_spec((n, D_MODEL)), _const_spec((n, D_MODEL)), _const_spec((D_MODEL, D_MODEL)),
                  _const_spec((1, D_MODEL)), ff_cols, ff_cols, ff_rows, _const_spec((1, D_MODEL))],
        out_specs=pl.BlockSpec(y_shape, lambda j: (0, 0, 0)),
        out_shape=jax.ShapeDtypeStruct(y_shape, F32),
        scratch_shapes=[pltpu.VMEM((n, D_MODEL), F32), pltpu.VMEM((n, D_MODEL), BF16),
                        pltpu.VMEM((n, D_MODEL), F32)],
        compiler_params=_params(("arbitrary",)),
        name="ffn_out",
    )(x2, o, w["xattn_wo"], w["ffn2_g"], w["ffn2_wg"], w["ffn2_wu"], w["ffn2_wd"], w["final_g"])


IN_SIZES = (("mq", ML_HEADS * ML_DK), ("mk", ML_HEADS * ML_DK), ("mv", ML_HEADS * ML_DV), ("mi", ML_HEADS),
            ("mf", ML_HEADS), ("mo", ML_HEADS * ML_DV), ("gq", GLA_QK_W), ("gk", GLA_QK_W), ("gv", GLA_V_W),
            ("ga", GLA_RANK), ("gg", GLA_V_W))
IN_OFFSET = {name: sum(width for _, width in IN_SIZES[:i]) for i, (name, _) in enumerate(IN_SIZES)}
D_IN = sum(width for _, width in IN_SIZES)
IN_MOVES = ((0, ZQ_MQ, IN_OFFSET["mq"], ZQ_GQ - ZQ_MQ), (0, ZQ_GQ, IN_OFFSET["gq"], ZQ_W - ZQ_GQ),
            (1, ZG_MO, IN_OFFSET["mo"], ZG_GG - ZG_MO), (1, ZG_GG, IN_OFFSET["gg"], ZG_SMALL - ZG_GG))
TM_REGROUP = MXU_WIDTH
assert IN_OFFSET["mf"] == IN_OFFSET["mi"] + ML_HEADS
assert all(IN_OFFSET[name] % SUBLANES == 0 for name in ("mq", "mi", "mo", "gq", "ga", "gg"))


def _w_in_regroup_kernel(wt_ref, wa2_ref, q_ref, g_ref, wa2_out):
    narrow = 2 * ML_HEADS + GLA_RANK
    wa2_out[...] = jnp.concatenate([jnp.zeros((2 * ML_HEADS, GLA_QK_W), F32), wa2_ref[...],
                                    jnp.zeros((SMALL_W - narrow, GLA_QK_W), F32)], axis=0).astype(BF16)

    def put(out_ref, dst, rows_t):
        out_ref[:, dst:dst + LANES] = rows_t.T.astype(BF16)

    for slab, dst, src, width in IN_MOVES:
        for off in range(0, width, LANES):
            put((q_ref, g_ref)[slab], dst + off, wt_ref[src + off:src + off + LANES, :])
    put(g_ref, ZG_SMALL, jnp.concatenate(
        [wt_ref[IN_OFFSET["mi"]:IN_OFFSET["mi"] + 2 * ML_HEADS, :],
         wt_ref[IN_OFFSET["ga"]:IN_OFFSET["ga"] + GLA_RANK, :],
         jnp.zeros((SMALL_W - narrow, wt_ref.shape[1]), F32)], axis=0))


def _w_in_regroup(w_in, w_a2):
    assert w_in.shape == (D_MODEL, D_IN)
    rows = lambda width: pl.BlockSpec((TM_REGROUP, width), lambda i: (i, 0))
    return pl.pallas_call(
        _w_in_regroup_kernel,
        grid=(D_MODEL // TM_REGROUP,),
        in_specs=[pl.BlockSpec((D_IN, TM_REGROUP), lambda i: (0, i)), _const_spec((GLA_RANK, GLA_QK_W))],
        out_specs=[rows(ZQ_W), rows(ZG_W), pl.BlockSpec((SMALL_W, GLA_QK_W), lambda i: (0, 0))],
        out_shape=[jax.ShapeDtypeStruct((D_MODEL, ZQ_W), BF16), jax.ShapeDtypeStruct((D_MODEL, ZG_W), BF16),
                   jax.ShapeDtypeStruct((SMALL_W, GLA_QK_W), BF16)],
        compiler_params=_params(("arbitrary",)),
        name="w_in_regroup",
    )(w_in.T, w_a2)


def _prep_weights(p):
    row = lambda a: a.reshape(1, -1).astype(F32)
    w_in_q, w_in_g, w_a2 = _w_in_regroup(p["w_in"], p["gla_w_a2"])
    gate_bias = jnp.concatenate([p["mlstm_b_i"], p["mlstm_b_f"], jnp.zeros((SMALL_W - 2 * ML_HEADS,), F32)])
    lane = np.arange(LANES)
    same_head = (lane[:, None] // GLA_DK) == (lane[None, :] // GLA_DK)
    diag_sum = jnp.asarray(np.stack([same_head & ((lane[None, :] % GLA_SUB) == j) for j in range(GLA_SUB)]), BF16)
    return dict(
        ffn1_wg=p["ffn1_w_gate"], ffn1_wu=p["ffn1_w_up"], ffn1_wd=p["ffn1_w_down"],
        ffn1_g=row(p["ffn1_norm_g"]), mix_g=row(p["mix_norm_g"]), w_in_q=w_in_q, w_in_g=w_in_g,
        gate_bias=row(gate_bias), w_a2=w_a2, b_a=row(p["gla_b_a"]), mlstm_out_g=row(p["mlstm_out_g"]),
        gla_out_g=row(p["gla_out_g"]), diag_sum=diag_sum,
        w_out=p["w_out"], xattn_g=row(p["xattn_norm_g"]), xattn_wq=p["xattn_w_q"],
        mem_g=row(p["mem_norm_g"]), ffn2_g=row(p["ffn2_norm_g"]), final_g=row(p["final_g"]),
        xattn_wo=p["xattn_w_o"], xattn_wk=p["xattn_w_k"], xattn_wv=p["xattn_w_v"], ffn2_wg=p["ffn2_w_gate"],
        ffn2_wu=p["ffn2_w_up"], ffn2_wd=p["ffn2_w_down"])


EARLY_WEIGHTS = ("ffn1_wg", "ffn1_wu", "ffn1_wd", "w_out", "xattn_wq")
LATE_WEIGHTS = ("xattn_wo", "xattn_wk", "xattn_wv", "ffn2_wg", "ffn2_wu", "ffn2_wd")


def kernel(x_prompt, x_sample, mem_prompt, cache_mem_k, cache_mem_v, state_mlstm_c, state_mlstm_n, state_mlstm_m, state_gla_s, ffn1_norm_g, ffn1_w_gate, ffn1_w_up, ffn1_w_down, mix_norm_g, w_in, mlstm_b_i, mlstm_b_f, mlstm_out_g, gla_w_a2, gla_b_a, gla_out_g, w_out, xattn_norm_g, mem_norm_g, xattn_w_q, xattn_w_k, xattn_w_v, xattn_w_o, ffn2_norm_g, ffn2_w_gate, ffn2_w_up, ffn2_w_down, final_norm_g):
    assert ffn1_norm_g.shape[0] == 1, "single-layer stack"
    layer = dict(ffn1_norm_g=ffn1_norm_g, ffn1_w_gate=ffn1_w_gate, ffn1_w_up=ffn1_w_up, ffn1_w_down=ffn1_w_down,
                 mix_norm_g=mix_norm_g, w_in=w_in, mlstm_b_i=mlstm_b_i, mlstm_b_f=mlstm_b_f,
                 mlstm_out_g=mlstm_out_g, gla_w_a2=gla_w_a2, gla_b_a=gla_b_a, gla_out_g=gla_out_g, w_out=w_out,
                 xattn_norm_g=xattn_norm_g, mem_norm_g=mem_norm_g, xattn_w_q=xattn_w_q, xattn_w_k=xattn_w_k,
                 xattn_w_v=xattn_w_v, xattn_w_o=xattn_w_o, ffn2_norm_g=ffn2_norm_g, ffn2_w_gate=ffn2_w_gate,
                 ffn2_w_up=ffn2_w_up, ffn2_w_down=ffn2_w_down)
    p = {name: arr[0] for name, arr in layer.items()}
    p["final_g"] = final_norm_g
    w = _prep_weights(p)

    bp, tp, _ = x_prompt.shape
    bs, ts, _ = x_sample.shape

    state = (state_mlstm_c[0], state_mlstm_n[0], state_mlstm_m[0].reshape(bs, 1, ML_HEADS), state_gla_s[0])
    x1_s, zq_s, zg_s, *early = _ffn_in_stream(x_sample, w, F32)
    w.update(zip(EARLY_WEIGHTS, early))
    ym_s, c_s, n_s, m_s, s_s = _mixer_state(zq_s, zg_s, w, state, ts, SAMPLE_MIXER_BATCHES)
    states_s = (c_s[None], n_s[None], m_s.reshape(1, bs, ML_HEADS), s_s[None])
    x2_s, q_s = _post_mix(x1_s, ym_s, w, TM_FFN_IN, F32)

    x1_p, ym_p, states_p, o_s, late = _ffn_mix(x_prompt, w, TM_FFN_IN, q_s, cache_mem_k[0], cache_mem_v[0],
                                               [w[name] for name in LATE_WEIGHTS])
    w.update(zip(LATE_WEIGHTS, late))
    y_p, mem_k_p, mem_v_p = _post_fused(x1_p.reshape(bp, tp, D_MODEL), ym_p.reshape(bp, tp, D_MODEL), mem_prompt, w,
                                        TM_POST)
    y_s = _ffn_out_stream(x2_s, o_s, w, bs)

    return (y_p, y_s, _cache_rows_unview(mem_k_p)[None], _cache_rows_unview(mem_v_p)[None]) + states_p + states_s
```
